```python
import jax, jax.numpy as jnp
from jax import lax
import numpy as np

D_MODEL = 1024
BATCH = 32
SEQ = 2048
DEPTH = 4

CHUNK = 64
Q_BLOCK = 128
EPS = 1e-6
NEG_INF = -1e30

MLA_HEADS = 8
QK_NOPE = 64
QK_ROPE = 32
V_HEAD = 64
Q_LORA = 256
KV_LORA = 128
ROPE_THETA = 10000.0
MLA_WIDTH = MLA_HEADS * V_HEAD

POOL_WINDOWS = (2, 4, 8, 16)
POOL_GROUPS = 4
POOL_GROUP_DIM = 64
POOL_WIDTH = POOL_GROUPS * POOL_GROUP_DIM
POOL_MAX_WIN = 16

SGU_BLOCK = 128
SGU_GROUPS = 4
SGU_GROUP_DIM = 64
SGU_WIDTH = SGU_GROUPS * SGU_GROUP_DIM

CONV_K = 3
CONV_WIDTH = 256

N_BRANCH = 4
D_FF = ((8 * D_MODEL + 3 * 256 - 1) // (3 * 256)) * 256

IN_SIZES = (Q_LORA, KV_LORA, QK_ROPE, POOL_WIDTH, SGU_WIDTH, SGU_WIDTH,
            CONV_WIDTH, CONV_WIDTH, CONV_WIDTH, N_BRANCH * D_MODEL)
IN_WIDTH = sum(IN_SIZES)

kernel_name = "hybrid_gated_mla_pool_sgu_conv_trunk"


def rmsnorm(x, g):
    xf = x.astype(jnp.float32)
    y = xf * lax.rsqrt(jnp.mean(xf * xf, axis=-1, keepdims=True) + EPS)
    return (y * g.astype(jnp.float32)).astype(x.dtype)


def split_cols(p):
    out = []
    o = 0
    for n in IN_SIZES:
        out.append(p[..., o:o + n])
        o += n
    return out


def rope_tables(seq, dtype):
    inv = ROPE_THETA ** (-jnp.arange(0, QK_ROPE, 2, dtype=jnp.float32) / QK_ROPE)
    ang = jnp.arange(seq, dtype=jnp.float32)[:, None] * inv[None, :]
    return jnp.cos(ang).astype(dtype), jnp.sin(ang).astype(dtype)


def apply_rope(x, cos, sin):
    half = x.shape[-1] // 2
    x1, x2 = x[..., :half], x[..., half:]
    return jnp.concatenate([x1 * cos - x2 * sin, x2 * cos + x1 * sin], axis=-1)


def mla_branch(c_q, c_kv, k_r, g_cq, g_ckv, w_uq, w_ukv, cos, sin):
    B, S, _ = c_q.shape
    q = (rmsnorm(c_q, g_cq) @ w_uq).reshape(B, S, MLA_HEADS, QK_NOPE + QK_ROPE)
    q_nope = q[..., :QK_NOPE]
    q_rope = apply_rope(q[..., QK_NOPE:], cos[:, None, :], sin[:, None, :])
    kv = (rmsnorm(c_kv, g_ckv) @ w_ukv).reshape(B, S, MLA_HEADS, QK_NOPE + V_HEAD)
    k_nope, v = kv[..., :QK_NOPE], kv[..., QK_NOPE:]
    k_rope = apply_rope(k_r, cos, sin)
    scale = (QK_NOPE + QK_ROPE) ** -0.5
    chunk_id = jnp.arange(S) // CHUNK
    outs = []
    for blk in range(S // Q_BLOCK):
        q0 = blk * Q_BLOCK
        kend = q0 + Q_BLOCK
        s = (jnp.einsum('bqhd,bkhd->bhqk', q_nope[:, q0:kend], k_nope[:, :kend])
             + jnp.einsum('bqhr,bkr->bhqk', q_rope[:, q0:kend], k_rope[:, :kend]))
        s = s.astype(jnp.float32) * scale
        mask = chunk_id[q0:kend, None] >= chunk_id[None, :kend]
        s = jnp.where(mask[None, None], s, NEG_INF)
        p = jax.nn.softmax(s, axis=-1).astype(v.dtype)
        outs.append(jnp.einsum('bhqk,bkhd->bqhd', p, v[:, :kend]))
    return jnp.concatenate(outs, axis=1).reshape(B, S, MLA_WIDTH)


def pool_branch(z, w_grp, scale):
    B, S, _ = z.shape
    zg = z.reshape(B, S, POOL_GROUPS, POOL_GROUP_DIM).astype(jnp.float32)
    csp = jnp.pad(jnp.cumsum(zg, axis=1), ((0, 0), (POOL_MAX_WIN, 0), (0, 0), (0, 0)))
    t = jnp.arange(S)
    outs = []
    for g, w in enumerate(POOL_WINDOWS):
        win_sum = (csp[:, POOL_MAX_WIN:POOL_MAX_WIN + S, g]
                   - csp[:, POOL_MAX_WIN - w:POOL_MAX_WIN - w + S, g])
        count = jnp.minimum(t + 1, w).astype(jnp.float32)
        outs.append(win_sum / count[None, :, None] - zg[:, :, g])
    pooled = jnp.stack(outs, axis=2).astype(z.dtype)
    mixed = jnp.einsum('bsgc,gcd->bsgd', pooled, w_grp)
    return mixed.reshape(B, S, POOL_WIDTH) * scale


def sgu_branch(u, v, g_v, w_s, b_s):
    B, S, _ = u.shape
    n = S // SGU_BLOCK
    vn = rmsnorm(v, g_v).reshape(B, n, SGU_BLOCK, SGU_GROUPS, SGU_GROUP_DIM)
    pos_chunk = jnp.arange(SGU_BLOCK) // CHUNK
    mask = pos_chunk[:, None] >= pos_chunk[None, :]
    w = jnp.where(mask[None], w_s, 0)
    mixed = jnp.einsum('gij,bnjgc->bnigc', w, vn) + b_s.T[None, None, :, :, None]
    return u * mixed.reshape(B, S, SGU_WIDTH)


def conv_branch(b_gate, c_gate, x_in, w_conv):
    z = c_gate * x_in
    y = lax.conv_general_dilated(z, w_conv, window_strides=(1,),
                                 padding=[(CONV_K - 1, 0)],
                                 dimension_numbers=('NWC', 'WIO', 'NWC'),
                                 feature_group_count=CONV_WIDTH)
    return b_gate * y


def _fwd_setup_inputs(seed: int = 0) -> dict:
    key = jax.random.key(seed)
    ks = iter(jax.random.split(key, 32))

    def nrm(shape, fan_in):
        return jax.random.normal(next(ks), shape, jnp.float32) * (fan_in ** -0.5)

    def gain(shape):
        return 1.0 + 0.1 * jax.random.normal(next(ks), shape, jnp.float32)

    L = DEPTH
    return {
        "x": jax.random.normal(next(ks), (BATCH, SEQ, D_MODEL), jnp.float32),
        "w_in": nrm((L, D_MODEL, IN_WIDTH), D_MODEL),
        "g_pre_mix": gain((L, D_MODEL)),
        "g_cq": gain((L, Q_LORA)),
        "g_ckv": gain((L, KV_LORA)),
        "w_uq": nrm((L, Q_LORA, MLA_HEADS * (QK_NOPE + QK_ROPE)), Q_LORA),
        "w_ukv": nrm((L, KV_LORA, MLA_HEADS * (QK_NOPE + V_HEAD)), KV_LORA),
        "pool_w": nrm((L, POOL_GROUPS, POOL_GROUP_DIM, POOL_GROUP_DIM), POOL_GROUP_DIM),
        "pool_scale": gain((L, POOL_WIDTH)),
        "g_sgu_v": gain((L, SGU_WIDTH)),
        "sgu_w": nrm((L, SGU_GROUPS, SGU_BLOCK, SGU_BLOCK), SGU_BLOCK),
        "sgu_b": gain((L, SGU_GROUPS, SGU_BLOCK)),
        "conv_w": nrm((L, CONV_K, 1, CONV_WIDTH), CONV_K),
        "w_br_a": nrm((L, MLA_WIDTH, D_MODEL), MLA_WIDTH),
        "w_br_b": nrm((L, POOL_WIDTH, D_MODEL), POOL_WIDTH),
        "w_br_c": nrm((L, SGU_WIDTH, D_MODEL), SGU_WIDTH),
        "w_br_d": nrm((L, CONV_WIDTH, D_MODEL), CONV_WIDTH),
        "w_out": nrm((L, D_MODEL, D_MODEL), D_MODEL),
        "g_post_mix": gain((L, D_MODEL)),
        "g_pre_ffn": gain((L, D_MODEL)),
        "w_ffn_gate": nrm((L, D_MODEL, D_FF), D_MODEL),
        "w_ffn_up": nrm((L, D_MODEL, D_FF), D_MODEL),
        "w_ffn_down": nrm((L, D_FF, D_MODEL), D_FF),
        "g_post_ffn": gain((L, D_MODEL)),
    }


def _fwd_reference(x, w_in, g_pre_mix, g_cq, g_ckv, w_uq, w_ukv, pool_w, pool_scale,
              g_sgu_v, sgu_w, sgu_b, conv_w, w_br_a, w_br_b, w_br_c, w_br_d,
              w_out, g_post_mix, g_pre_ffn, w_ffn_gate, w_ffn_up, w_ffn_down,
              g_post_ffn):
    B, S, D = x.shape
    cos, sin = rope_tables(S, x.dtype)
    for l in range(DEPTH):
        h = rmsnorm(x, g_pre_mix[l])
        (c_q, c_kv, k_r, p_in, s_u, s_v, cv_b, cv_c, cv_x,
         gate_logits) = split_cols(h @ w_in[l])
        y_a = mla_branch(c_q, c_kv, k_r, g_cq[l], g_ckv[l], w_uq[l], w_ukv[l], cos, sin) @ w_br_a[l]
        y_b = pool_branch(p_in, pool_w[l], pool_scale[l]) @ w_br_b[l]
        y_c = sgu_branch(s_u, s_v, g_sgu_v[l], sgu_w[l], sgu_b[l]) @ w_br_c[l]
        y_d = conv_branch(cv_b, cv_c, cv_x, conv_w[l]) @ w_br_d[l]
        gates = jax.nn.sigmoid(gate_logits.astype(jnp.float32)).astype(x.dtype)
        gates = gates.reshape(B, S, N_BRANCH, D)
        merged = (gates[:, :, 0] * y_a + gates[:, :, 1] * y_b
                  + gates[:, :, 2] * y_c + gates[:, :, 3] * y_d)
        x = x + rmsnorm(merged @ w_out[l], g_post_mix[l])
        h = rmsnorm(x, g_pre_ffn[l])
        f = (jax.nn.silu(h @ w_ffn_gate[l]) * (h @ w_ffn_up[l])) @ w_ffn_down[l]
        x = x + rmsnorm(f, g_post_ffn[l])
    return x


import jax as _jax
import jax.numpy as _jnp

TWIN_FORMAT = 'train_step'
FWD_PARAMS = ['x', 'w_in', 'g_pre_mix', 'g_cq', 'g_ckv', 'w_uq', 'w_ukv', 'pool_w', 'pool_scale', 'g_sgu_v', 'sgu_w', 'sgu_b', 'conv_w', 'w_br_a', 'w_br_b', 'w_br_c', 'w_br_d', 'w_out', 'g_post_mix', 'g_pre_ffn', 'w_ffn_gate', 'w_ffn_up', 'w_ffn_down', 'g_post_ffn']
TWIN_WEIGHTS = ['w_in', 'g_pre_mix', 'g_cq', 'g_ckv', 'w_uq', 'w_ukv', 'pool_w', 'pool_scale', 'g_sgu_v', 'sgu_w', 'sgu_b', 'conv_w', 'w_br_a', 'w_br_b', 'w_br_c', 'w_br_d', 'w_out', 'g_post_mix', 'g_pre_ffn', 'w_ffn_gate', 'w_ffn_up', 'w_ffn_down', 'g_post_ffn']
TWIN_DIFF_INPUT = 'x'
TWIN_INPUTS = ['x', 'w_in', 'g_pre_mix', 'g_cq', 'g_ckv', 'w_uq', 'w_ukv', 'pool_w', 'pool_scale', 'g_sgu_v', 'sgu_w', 'sgu_b', 'conv_w', 'w_br_a', 'w_br_b', 'w_br_c', 'w_br_d', 'w_out', 'g_post_mix', 'g_pre_ffn', 'w_ffn_gate', 'w_ffn_up', 'w_ffn_down', 'g_post_ffn', 'loss_target', 'm_w_in', 'm_g_pre_mix', 'm_g_cq', 'm_g_ckv', 'm_w_uq', 'm_w_ukv', 'm_pool_w', 'm_pool_scale', 'm_g_sgu_v', 'm_sgu_w', 'm_sgu_b', 'm_conv_w', 'm_w_br_a', 'm_w_br_b', 'm_w_br_c', 'm_w_br_d', 'm_w_out', 'm_g_post_mix', 'm_g_pre_ffn', 'm_w_ffn_gate', 'm_w_ffn_up', 'm_w_ffn_down', 'm_g_post_ffn', 'v_w_in', 'v_g_pre_mix', 'v_g_cq', 'v_g_ckv', 'v_w_uq', 'v_w_ukv', 'v_pool_w', 'v_pool_scale', 'v_g_sgu_v', 'v_sgu_w', 'v_sgu_b', 'v_conv_w', 'v_w_br_a', 'v_w_br_b', 'v_w_br_c', 'v_w_br_d', 'v_w_out', 'v_g_post_mix', 'v_g_pre_ffn', 'v_w_ffn_gate', 'v_w_ffn_up', 'v_w_ffn_down', 'v_g_post_ffn']
TWIN_OUTPUTS = ['loss', 'grad_x', 'grad_w_in', 'grad_g_pre_mix', 'grad_g_cq', 'grad_g_ckv', 'grad_w_uq', 'grad_w_ukv', 'grad_pool_w', 'grad_pool_scale', 'grad_g_sgu_v', 'grad_sgu_w', 'grad_sgu_b', 'grad_conv_w', 'grad_w_br_a', 'grad_w_br_b', 'grad_w_br_c', 'grad_w_br_d', 'grad_w_out', 'grad_g_post_mix', 'grad_g_pre_ffn', 'grad_w_ffn_gate', 'grad_w_ffn_up', 'grad_w_ffn_down', 'grad_g_post_ffn', 'delta_w_in', 'delta_g_pre_mix', 'delta_g_cq', 'delta_g_ckv', 'delta_w_uq', 'delta_w_ukv', 'delta_pool_w', 'delta_pool_scale', 'delta_g_sgu_v', 'delta_sgu_w', 'delta_sgu_b', 'delta_conv_w', 'delta_w_br_a', 'delta_w_br_b', 'delta_w_br_c', 'delta_w_br_d', 'delta_w_out', 'delta_g_post_mix', 'delta_g_pre_ffn', 'delta_w_ffn_gate', 'delta_w_ffn_up', 'delta_w_ffn_down', 'delta_g_post_ffn', 'new_m_w_in', 'new_m_g_pre_mix', 'new_m_g_cq', 'new_m_g_ckv', 'new_m_w_uq', 'new_m_w_ukv', 'new_m_pool_w', 'new_m_pool_scale', 'new_m_g_sgu_v', 'new_m_sgu_w', 'new_m_sgu_b', 'new_m_conv_w', 'new_m_w_br_a', 'new_m_w_br_b', 'new_m_w_br_c', 'new_m_w_br_d', 'new_m_w_out', 'new_m_g_post_mix', 'new_m_g_pre_ffn', 'new_m_w_ffn_gate', 'new_m_w_ffn_up', 'new_m_w_ffn_down', 'new_m_g_post_ffn', 'new_v_w_in', 'new_v_g_pre_mix', 'new_v_g_cq', 'new_v_g_ckv', 'new_v_w_uq', 'new_v_w_ukv', 'new_v_pool_w', 'new_v_pool_scale', 'new_v_g_sgu_v', 'new_v_sgu_w', 'new_v_sgu_b', 'new_v_conv_w', 'new_v_w_br_a', 'new_v_w_br_b', 'new_v_w_br_c', 'new_v_w_br_d', 'new_v_w_out', 'new_v_g_post_mix', 'new_v_g_pre_ffn', 'new_v_w_ffn_gate', 'new_v_w_ffn_up', 'new_v_w_ffn_down', 'new_v_g_post_ffn']
TWIN_LEAF_KINDS = {'loss': 'loss', 'grad_x': 'grad_x', 'grad_w_in': 'grad_w', 'grad_g_pre_mix': 'grad_w', 'grad_g_cq': 'grad_w', 'grad_g_ckv': 'grad_w', 'grad_w_uq': 'grad_w', 'grad_w_ukv': 'grad_w', 'grad_pool_w': 'grad_w', 'grad_pool_scale': 'grad_w', 'grad_g_sgu_v': 'grad_w', 'grad_sgu_w': 'grad_w', 'grad_sgu_b': 'grad_w', 'grad_conv_w': 'grad_w', 'grad_w_br_a': 'grad_w', 'grad_w_br_b': 'grad_w', 'grad_w_br_c': 'grad_w', 'grad_w_br_d': 'grad_w', 'grad_w_out': 'grad_w', 'grad_g_post_mix': 'grad_w', 'grad_g_pre_ffn': 'grad_w', 'grad_w_ffn_gate': 'grad_w', 'grad_w_ffn_up': 'grad_w', 'grad_w_ffn_down': 'grad_w', 'grad_g_post_ffn': 'grad_w', 'delta_w_in': 'delta_w', 'delta_g_pre_mix': 'delta_w', 'delta_g_cq': 'delta_w', 'delta_g_ckv': 'delta_w', 'delta_w_uq': 'delta_w', 'delta_w_ukv': 'delta_w', 'delta_pool_w': 'delta_w', 'delta_pool_scale': 'delta_w', 'delta_g_sgu_v': 'delta_w', 'delta_sgu_w': 'delta_w', 'delta_sgu_b': 'delta_w', 'delta_conv_w': 'delta_w', 'delta_w_br_a': 'delta_w', 'delta_w_br_b': 'delta_w', 'delta_w_br_c': 'delta_w', 'delta_w_br_d': 'delta_w', 'delta_w_out': 'delta_w', 'delta_g_post_mix': 'delta_w', 'delta_g_pre_ffn': 'delta_w', 'delta_w_ffn_gate': 'delta_w', 'delta_w_ffn_up': 'delta_w', 'delta_w_ffn_down': 'delta_w', 'delta_g_post_ffn': 'delta_w', 'new_m_w_in': 'new_m', 'new_m_g_pre_mix': 'new_m', 'new_m_g_cq': 'new_m', 'new_m_g_ckv': 'new_m', 'new_m_w_uq': 'new_m', 'new_m_w_ukv': 'new_m', 'new_m_pool_w': 'new_m', 'new_m_pool_scale': 'new_m', 'new_m_g_sgu_v': 'new_m', 'new_m_sgu_w': 'new_m', 'new_m_sgu_b': 'new_m', 'new_m_conv_w': 'new_m', 'new_m_w_br_a': 'new_m', 'new_m_w_br_b': 'new_m', 'new_m_w_br_c': 'new_m', 'new_m_w_br_d': 'new_m', 'new_m_w_out': 'new_m', 'new_m_g_post_mix': 'new_m', 'new_m_g_pre_ffn': 'new_m', 'new_m_w_ffn_gate': 'new_m', 'new_m_w_ffn_up': 'new_m', 'new_m_w_ffn_down': 'new_m', 'new_m_g_post_ffn': 'new_m', 'new_v_w_in': 'new_v', 'new_v_g_pre_mix': 'new_v', 'new_v_g_cq': 'new_v', 'new_v_g_ckv': 'new_v', 'new_v_w_uq': 'new_v', 'new_v_w_ukv': 'new_v', 'new_v_pool_w': 'new_v', 'new_v_pool_scale': 'new_v', 'new_v_g_sgu_v': 'new_v', 'new_v_sgu_w': 'new_v', 'new_v_sgu_b': 'new_v', 'new_v_conv_w': 'new_v', 'new_v_w_br_a': 'new_v', 'new_v_w_br_b': 'new_v', 'new_v_w_br_c': 'new_v', 'new_v_w_br_d': 'new_v', 'new_v_w_out': 'new_v', 'new_v_g_post_mix': 'new_v', 'new_v_g_pre_ffn': 'new_v', 'new_v_w_ffn_gate': 'new_v', 'new_v_w_ffn_up': 'new_v', 'new_v_w_ffn_down': 'new_v', 'new_v_g_post_ffn': 'new_v'}


def _forward(args):
    return _fwd_reference(*[args[k] for k in FWD_PARAMS])


def _output_shape():
    out = _jax.eval_shape(lambda: _forward(_fwd_setup_inputs(0)))
    return out.shape, out.dtype

N_MICROBATCH = 1
ADAM_LR = 0.001
ADAM_B1 = 0.9
ADAM_B2 = 0.999
ADAM_EPS = 1e-08
ADAM_WD = 0.01
ADAM_STEP = 10
PER_EXAMPLE_BATCH_AXIS = {'x': 0, 'loss_target': 0}
SHARED_INPUTS = []
_WEIGHT_DTYPES = {'w_in': _jnp.float32, 'g_pre_mix': _jnp.float32, 'g_cq': _jnp.float32, 'g_ckv': _jnp.float32, 'w_uq': _jnp.float32, 'w_ukv': _jnp.float32, 'pool_w': _jnp.float32, 'pool_scale': _jnp.float32, 'g_sgu_v': _jnp.float32, 'sgu_w': _jnp.float32, 'sgu_b': _jnp.float32, 'conv_w': _jnp.float32, 'w_br_a': _jnp.float32, 'w_br_b': _jnp.float32, 'w_br_c': _jnp.float32, 'w_br_d': _jnp.float32, 'w_out': _jnp.float32, 'g_post_mix': _jnp.float32, 'g_pre_ffn': _jnp.float32, 'w_ffn_gate': _jnp.float32, 'w_ffn_up': _jnp.float32, 'w_ffn_down': _jnp.float32, 'g_post_ffn': _jnp.float32}
MOMENT_SCALE = {'w_in': 1.208257e+00, 'g_pre_mix': 2.967056e+00, 'g_cq': 3.177965e-01, 'g_ckv': 6.489606e-01, 'w_uq': 1.820160e-01, 'w_ukv': 2.168708e-01, 'pool_w': 2.720806e+00, 'pool_scale': 2.587443e+00, 'g_sgu_v': 1.747817e+00, 'sgu_w': 1.197751e+00, 'sgu_b': 1.339937e+00, 'conv_w': 2.144320e+00, 'w_br_a': 1.691053e-01, 'w_br_b': 1.374543e+00, 'w_br_c': 1.834465e+00, 'w_br_d': 1.020561e+00, 'w_out': 2.564372e+00, 'g_post_mix': 6.284946e+01, 'g_pre_ffn': 2.170852e+00, 'w_ffn_gate': 8.642705e-01, 'w_ffn_up': 1.005722e+00, 'w_ffn_down': 1.699766e+00, 'g_post_ffn': 6.306529e+01}


def _to_microbatches(a, axis):
    t = _jnp.moveaxis(a, axis, 0)
    t = t.reshape((N_MICROBATCH, t.shape[0] // N_MICROBATCH) + t.shape[1:])
    return _jnp.moveaxis(t, 1, axis + 1)


def setup_inputs(seed: int = 0) -> dict:
    inp = _fwd_setup_inputs(seed)
    key = _jax.random.fold_in(_jax.random.key(seed), 7919)
    shape, _ = _output_shape()
    out = dict(inp)
    out["loss_target"] = _jax.random.normal(_jax.random.fold_in(key, 0), shape, _jnp.float32)
    for i, name in enumerate(TWIN_WEIGHTS):
        w = inp[name].astype(_jnp.float32)
        if MOMENT_SCALE is None:
            s = _jnp.sqrt(_jnp.mean(_jnp.square(w)) + 1e-30)
        else:
            s = MOMENT_SCALE[name]
        km, kv = _jax.random.split(_jax.random.fold_in(key, i + 1))
        out[name] = w
        out["m_" + name] = s * _jax.random.normal(km, w.shape, _jnp.float32)
        out["v_" + name] = (s * s) * _jax.random.uniform(kv, w.shape, _jnp.float32, 0.5, 1.5)
    if N_MICROBATCH > 1:
        for name, axis in PER_EXAMPLE_BATCH_AXIS.items():
            out[name] = _to_microbatches(out[name], axis)
    return {'x': out['x'], 'w_in': out['w_in'], 'g_pre_mix': out['g_pre_mix'], 'g_cq': out['g_cq'], 'g_ckv': out['g_ckv'], 'w_uq': out['w_uq'], 'w_ukv': out['w_ukv'], 'pool_w': out['pool_w'], 'pool_scale': out['pool_scale'], 'g_sgu_v': out['g_sgu_v'], 'sgu_w': out['sgu_w'], 'sgu_b': out['sgu_b'], 'conv_w': out['conv_w'], 'w_br_a': out['w_br_a'], 'w_br_b': out['w_br_b'], 'w_br_c': out['w_br_c'], 'w_br_d': out['w_br_d'], 'w_out': out['w_out'], 'g_post_mix': out['g_post_mix'], 'g_pre_ffn': out['g_pre_ffn'], 'w_ffn_gate': out['w_ffn_gate'], 'w_ffn_up': out['w_ffn_up'], 'w_ffn_down': out['w_ffn_down'], 'g_post_ffn': out['g_post_ffn'], 'loss_target': out['loss_target'], 'm_w_in': out['m_w_in'], 'm_g_pre_mix': out['m_g_pre_mix'], 'm_g_cq': out['m_g_cq'], 'm_g_ckv': out['m_g_ckv'], 'm_w_uq': out['m_w_uq'], 'm_w_ukv': out['m_w_ukv'], 'm_pool_w': out['m_pool_w'], 'm_pool_scale': out['m_pool_scale'], 'm_g_sgu_v': out['m_g_sgu_v'], 'm_sgu_w': out['m_sgu_w'], 'm_sgu_b': out['m_sgu_b'], 'm_conv_w': out['m_conv_w'], 'm_w_br_a': out['m_w_br_a'], 'm_w_br_b': out['m_w_br_b'], 'm_w_br_c': out['m_w_br_c'], 'm_w_br_d': out['m_w_br_d'], 'm_w_out': out['m_w_out'], 'm_g_post_mix': out['m_g_post_mix'], 'm_g_pre_ffn': out['m_g_pre_ffn'], 'm_w_ffn_gate': out['m_w_ffn_gate'], 'm_w_ffn_up': out['m_w_ffn_up'], 'm_w_ffn_down': out['m_w_ffn_down'], 'm_g_post_ffn': out['m_g_post_ffn'], 'v_w_in': out['v_w_in'], 'v_g_pre_mix': out['v_g_pre_mix'], 'v_g_cq': out['v_g_cq'], 'v_g_ckv': out['v_g_ckv'], 'v_w_uq': out['v_w_uq'], 'v_w_ukv': out['v_w_ukv'], 'v_pool_w': out['v_pool_w'], 'v_pool_scale': out['v_pool_scale'], 'v_g_sgu_v': out['v_g_sgu_v'], 'v_sgu_w': out['v_sgu_w'], 'v_sgu_b': out['v_sgu_b'], 'v_conv_w': out['v_conv_w'], 'v_w_br_a': out['v_w_br_a'], 'v_w_br_b': out['v_w_br_b'], 'v_w_br_c': out['v_w_br_c'], 'v_w_br_d': out['v_w_br_d'], 'v_w_out': out['v_w_out'], 'v_g_post_mix': out['v_g_post_mix'], 'v_g_pre_ffn': out['v_g_pre_ffn'], 'v_w_ffn_gate': out['v_w_ffn_gate'], 'v_w_ffn_up': out['v_w_ffn_up'], 'v_w_ffn_down': out['v_w_ffn_down'], 'v_g_post_ffn': out['v_g_post_ffn']}


def _loss(weights, diff, rest, loss_target):
    with _jax.named_scope("forward"):
        args = {**rest, TWIN_DIFF_INPUT: diff, **{k: w.astype(_WEIGHT_DTYPES[k]) for k, w in weights.items()}}
        y = _forward(args)
    with _jax.named_scope("loss_head"):
        err = _jnp.square(y.astype(_jnp.float32) - loss_target)
        return 0.5 * _jnp.sum(_jnp.mean(err, axis=-1)) if err.ndim else 0.5 * err


def _adamw(w, g, m, v):
    m = ADAM_B1 * m + (1.0 - ADAM_B1) * g
    v = ADAM_B2 * v + (1.0 - ADAM_B2) * _jnp.square(g)
    m_hat = m / (1.0 - ADAM_B1 ** ADAM_STEP)
    v_hat = v / (1.0 - ADAM_B2 ** ADAM_STEP)
    delta = -ADAM_LR * (m_hat / (_jnp.sqrt(v_hat) + ADAM_EPS) + ADAM_WD * w)
    return delta, m, v


def reference(x, w_in, g_pre_mix, g_cq, g_ckv, w_uq, w_ukv, pool_w, pool_scale, g_sgu_v, sgu_w, sgu_b, conv_w, w_br_a, w_br_b, w_br_c, w_br_d, w_out, g_post_mix, g_pre_ffn, w_ffn_gate, w_ffn_up, w_ffn_down, g_post_ffn, loss_target, m_w_in, m_g_pre_mix, m_g_cq, m_g_ckv, m_w_uq, m_w_ukv, m_pool_w, m_pool_scale, m_g_sgu_v, m_sgu_w, m_sgu_b, m_conv_w, m_w_br_a, m_w_br_b, m_w_br_c, m_w_br_d, m_w_out, m_g_post_mix, m_g_pre_ffn, m_w_ffn_gate, m_w_ffn_up, m_w_ffn_down, m_g_post_ffn, v_w_in, v_g_pre_mix, v_g_cq, v_g_ckv, v_w_uq, v_w_ukv, v_pool_w, v_pool_scale, v_g_sgu_v, v_sgu_w, v_sgu_b, v_conv_w, v_w_br_a, v_w_br_b, v_w_br_c, v_w_br_d, v_w_out, v_g_post_mix, v_g_pre_ffn, v_w_ffn_gate, v_w_ffn_up, v_w_ffn_down, v_g_post_ffn):
    given = dict(x=x, w_in=w_in, g_pre_mix=g_pre_mix, g_cq=g_cq, g_ckv=g_ckv, w_uq=w_uq, w_ukv=w_ukv, pool_w=pool_w, pool_scale=pool_scale, g_sgu_v=g_sgu_v, sgu_w=sgu_w, sgu_b=sgu_b, conv_w=conv_w, w_br_a=w_br_a, w_br_b=w_br_b, w_br_c=w_br_c, w_br_d=w_br_d, w_out=w_out, g_post_mix=g_post_mix, g_pre_ffn=g_pre_ffn, w_ffn_gate=w_ffn_gate, w_ffn_up=w_ffn_up, w_ffn_down=w_ffn_down, g_post_ffn=g_post_ffn, loss_target=loss_target, m_w_in=m_w_in, m_g_pre_mix=m_g_pre_mix, m_g_cq=m_g_cq, m_g_ckv=m_g_ckv, m_w_uq=m_w_uq, m_w_ukv=m_w_ukv, m_pool_w=m_pool_w, m_pool_scale=m_pool_scale, m_g_sgu_v=m_g_sgu_v, m_sgu_w=m_sgu_w, m_sgu_b=m_sgu_b, m_conv_w=m_conv_w, m_w_br_a=m_w_br_a, m_w_br_b=m_w_br_b, m_w_br_c=m_w_br_c, m_w_br_d=m_w_br_d, m_w_out=m_w_out, m_g_post_mix=m_g_post_mix, m_g_pre_ffn=m_g_pre_ffn, m_w_ffn_gate=m_w_ffn_gate, m_w_ffn_up=m_w_ffn_up, m_w_ffn_down=m_w_ffn_down, m_g_post_ffn=m_g_post_ffn, v_w_in=v_w_in, v_g_pre_mix=v_g_pre_mix, v_g_cq=v_g_cq, v_g_ckv=v_g_ckv, v_w_uq=v_w_uq, v_w_ukv=v_w_ukv, v_pool_w=v_pool_w, v_pool_scale=v_pool_scale, v_g_sgu_v=v_g_sgu_v, v_sgu_w=v_sgu_w, v_sgu_b=v_sgu_b, v_conv_w=v_conv_w, v_w_br_a=v_w_br_a, v_w_br_b=v_w_br_b, v_w_br_c=v_w_br_c, v_w_br_d=v_w_br_d, v_w_out=v_w_out, v_g_post_mix=v_g_post_mix, v_g_pre_ffn=v_g_pre_ffn, v_w_ffn_gate=v_w_ffn_gate, v_w_ffn_up=v_w_ffn_up, v_w_ffn_down=v_w_ffn_down, v_g_post_ffn=v_g_post_ffn)
    weights = {n: given[n] for n in TWIN_WEIGHTS}
    shared = {n: given[n] for n in SHARED_INPUTS}
    per_example = {n: given[n] for n in ['x']}
    grad_fn = _jax.value_and_grad(_loss, argnums=(0, 1))

    def one_microbatch(ex, loss_target):
        ex = dict(ex)
        diff = ex.pop(TWIN_DIFF_INPUT)
        return grad_fn(weights, diff, {**shared, **ex}, loss_target)

    if N_MICROBATCH == 1:
        loss, (grad_w, grad_x) = one_microbatch(per_example, given["loss_target"])
    else:
        def body(carry, xs):
            loss_sum, grad_sum = carry
            l_k, (gw_k, gx_k) = one_microbatch(xs[0], xs[1])
            with _jax.named_scope("update"):
                return (loss_sum + l_k, _jax.tree.map(_jnp.add, grad_sum, gw_k)), gx_k

        init = (_jnp.zeros((), _jnp.float32), _jax.tree.map(_jnp.zeros_like, weights))
        (loss, grad_w), grad_x = _jax.lax.scan(body, init, (per_example, given["loss_target"]))
    with _jax.named_scope("update"):
        delta_w, new_m, new_v = {}, {}, {}
        for n in TWIN_WEIGHTS:
            delta_w[n], new_m[n], new_v[n] = _adamw(weights[n], grad_w[n], given["m_" + n], given["v_" + n])
    return (loss, grad_x, *[grad_w[n] for n in TWIN_WEIGHTS], *[delta_w[n] for n in TWIN_WEIGHTS],
            *[new_m[n] for n in TWIN_WEIGHTS], *[new_v[n] for n in TWIN_WEIGHTS])
```

```python
import functools

import numpy as np
import jax
import jax.numpy as jnp
from jax import lax
from jax.experimental import pallas as pl
from jax.experimental.pallas import tpu as pltpu

F32 = jnp.float32
BF16 = jnp.bfloat16

EPS = 1e-6
NEG_INF = -1e30
DEPTH = 4
HEADS = 8
QK_NOPE = 64
QK_ROPE = 32
V_HEAD = 64
HEAD_PAD = 128
Q_LORA = 256
KV_LORA = 128
ROPE_THETA = 10000.0
POOL_WINDOWS = (2, 4, 8, 16)
GROUPS = 4
GROUP_DIM = 64
MIX_W = GROUPS * GROUP_DIM
POOL_HALO = 16
CONV_HALO = 8
SGU_BLOCK = 128
CHUNK = 64
CHUNK_SHIFT = 6
GROUP_SHIFT = 6
N_BRANCH = 4
MLA_W = HEADS * HEAD_PAD
ATTN_SCALE = (QK_NOPE + QK_ROPE) ** -0.5
SEC_MLA = Q_LORA + KV_LORA + HEAD_PAD
SEC_MIX = 6 * MIX_W

ADAM_LR = 0.001
ADAM_B1 = 0.9
ADAM_B2 = 0.999
ADAM_EPS = 1e-08
ADAM_WD = 0.01
ADAM_STEP = 10

VMEM_LIMIT = 56 * 1024 * 1024
MESH = pl.DeviceIdType.MESH


def _tile(n, pref, mult=8):
    t = min(n, pref)
    while t > 0:
        if n % t == 0 and t % mult == 0:
            return t
        t -= 1
    return n


def _params(n_axes):
    return pltpu.CompilerParams(dimension_semantics=("arbitrary",) * n_axes, vmem_limit_bytes=VMEM_LIMIT)


def _dot(a, b):
    return jnp.dot(a, b, preferred_element_type=F32)


def _dot_nt(a, b):
    return lax.dot_general(a, b, (((1,), (1,)), ((), ())), preferred_element_type=F32)


def _dot_tn(a, b):
    return lax.dot_general(a, b, (((0,), (0,)), ((), ())), preferred_element_type=F32)


def _rms_r(x):
    return lax.rsqrt(jnp.mean(x * x, axis=-1, keepdims=True) + EPS)


def _rms_bwd(dy, x, g):
    r = _rms_r(x)
    u = dy * g
    dx = r * u - x * (r * r * r * jnp.mean(u * x, axis=-1, keepdims=True))
    dg = jnp.sum(dy * x * r, axis=0, keepdims=True)
    return dx, dg


def _sigmoid(x):
    return 1.0 / (1.0 + jnp.exp(-x))


def _shift_down(a, k):
    return pltpu.roll(a, k, 0)


def _shift_up(a, k):
    return pltpu.roll(a, a.shape[0] - k, 0)


def _rope(x, c, sa, sb):
    w = x.shape[-1]
    return x * c + pltpu.roll(x, QK_ROPE // 2, 1) * sa + pltpu.roll(x, w - QK_ROPE // 2, 1) * sb


def _rope_t(d, c, sa, sb):
    w = d.shape[-1]
    return d * c + pltpu.roll(d * sa, w - QK_ROPE // 2, 1) + pltpu.roll(d * sb, QK_ROPE // 2, 1)


def _full(shape):
    return pl.BlockSpec(shape, lambda *_: (0,) * len(shape))


def _norm_matmul(x, g, w, name):
    T, K = x.shape
    N = w.shape[1]
    tm, tn = _tile(T, 512), _tile(N, 1536, 128)

    def body(x_ref, g_ref, w_ref, o_ref, h_ref):
        @pl.when(pl.program_id(1) == 0)
        def _():
            xv = x_ref[...]
            h_ref[...] = (xv * _rms_r(xv) * g_ref[...]).astype(BF16)

        o_ref[...] = _dot(h_ref[...], w_ref[...])

    return pl.pallas_call(
        body, name=name, grid=(T // tm, N // tn),
        in_specs=[pl.BlockSpec((tm, K), lambda i, j: (i, 0)), _full((1, K)), pl.BlockSpec((K, tn), lambda i, j: (0, j))],
        out_specs=[pl.BlockSpec((tm, tn), lambda i, j: (i, j)), pl.BlockSpec((tm, K), lambda i, j: (i, 0))],
        out_shape=[jax.ShapeDtypeStruct((T, N), F32), jax.ShapeDtypeStruct((T, K), BF16)],
        compiler_params=_params(2),
    )(x, g, w)


def _mla_prep(P, D, g_cq, g_ckv, wuq, wukv, rope_c, rope_sa, rope_sb, S):
    T = P.shape[0]
    tm = _tile(S, 512)
    n_si = S // tm
    base = 4 * D

    def body(cq_ref, ckv_ref, kr_ref, gq_ref, gkv_ref, wq_ref, wkv_ref, c_ref, sa_ref, sb_ref,
             q_ref, k_ref, v_ref, hq_ref, hkv_ref):
        c, sa, sb = c_ref[...], sa_ref[...], sb_ref[...]
        cq = cq_ref[...]
        hq = (cq * _rms_r(cq) * gq_ref[...]).astype(BF16)
        hq_ref[...] = hq
        q = _dot(hq, wq_ref[...])
        q_ref[...] = _rope(q, jnp.tile(c, (1, HEADS)), jnp.tile(sa, (1, HEADS)), jnp.tile(sb, (1, HEADS))).astype(BF16)
        ckv = ckv_ref[...]
        hkv = (ckv * _rms_r(ckv) * gkv_ref[...]).astype(BF16)
        hkv_ref[...] = hkv
        kv = _dot(hkv, wkv_ref[...])
        kr = _rope(kr_ref[...], c, sa, sb)
        k_ref[...] = (kv[:, :MLA_W] + jnp.tile(kr, (1, HEADS))).astype(BF16)
        v_ref[...] = kv[:, MLA_W:].astype(BF16)

    tab = pl.BlockSpec((tm, HEAD_PAD), lambda i: (i % n_si, 0))
    row = lambda w: pl.BlockSpec((tm, w), lambda i: (i, 0))
    return pl.pallas_call(
        body, name="mla_prep", grid=(T // tm,),
        in_specs=[pl.BlockSpec((tm, Q_LORA), lambda i: (i, base // Q_LORA)),
                  pl.BlockSpec((tm, KV_LORA), lambda i: (i, (base + Q_LORA) // KV_LORA)),
                  pl.BlockSpec((tm, HEAD_PAD), lambda i: (i, (base + Q_LORA + KV_LORA) // HEAD_PAD)),
                  _full((1, Q_LORA)), _full((1, KV_LORA)), _full((Q_LORA, MLA_W)), _full((KV_LORA, 2 * MLA_W)),
                  tab, tab, tab],
        out_specs=[row(MLA_W), row(MLA_W), row(MLA_W), row(Q_LORA), row(KV_LORA)],
        out_shape=[jax.ShapeDtypeStruct((T, MLA_W), BF16)] * 3
        + [jax.ShapeDtypeStruct((T, Q_LORA), BF16), jax.ShapeDtypeStruct((T, KV_LORA), BF16)],
        compiler_params=_params(1),
    )(P, P, P, g_cq, g_ckv, wuq, wukv, rope_c, rope_sa, rope_sb)


def _chunk_mask(tq, tk):
    row = lax.broadcasted_iota(jnp.int32, (tq, tk), 0)
    col = lax.broadcasted_iota(jnp.int32, (tq, tk), 1)
    return (row >> CHUNK_SHIFT) >= (col >> CHUNK_SHIFT)


def _flash_fwd(Q, K, V, S):
    T = Q.shape[0]
    n_seq = T // S
    tq = _tile(S, 256, 128)
    nq = S // tq

    def body(q_ref, k_ref, v_ref, o_ref, lse_ref):
        qi = pl.program_id(1)
        mask = _chunk_mask(tq, tq)
        lane = lax.broadcasted_iota(jnp.int32, (tq, HEAD_PAD), 1)
        lse_all = jnp.zeros((tq, HEAD_PAD), F32)
        for h in range(HEADS):
            hs = slice(h * HEAD_PAD, (h + 1) * HEAD_PAD)
            q = q_ref[:, hs]

            def update(kb, carry, masked):
                m, l, acc = carry
                rows = pl.ds(pl.multiple_of(kb * tq, tq), tq)
                s = _dot_nt(q, k_ref[rows, hs]) * ATTN_SCALE
                if masked:
                    s = jnp.where(mask, s, NEG_INF)
                m_new = jnp.maximum(m, jnp.max(s, axis=-1, keepdims=True))
                alpha = jnp.exp(m - m_new)
                p = jnp.exp(s - m_new)
                l = alpha * l + jnp.sum(p, axis=-1, keepdims=True)
                acc = alpha * acc + _dot(p.astype(BF16), v_ref[rows, hs])
                return m_new, l, acc

            init = (jnp.full((tq, 1), NEG_INF, F32), jnp.zeros((tq, 1), F32), jnp.zeros((tq, HEAD_PAD), F32))
            carry = lax.fori_loop(0, qi, lambda kb, c: update(kb, c, False), init)
            m, l, acc = update(qi, carry, True)
            o_ref[:, hs] = (acc / l).astype(BF16)
            lse_all = jnp.where(lane == h, m + jnp.log(l), lse_all)
        lse_ref[...] = lse_all

    return pl.pallas_call(
        body, name="flash_fwd", grid=(n_seq, nq),
        in_specs=[pl.BlockSpec((tq, MLA_W), lambda b, i: (b * nq + i, 0)),
                  pl.BlockSpec((S, MLA_W), lambda b, i: (b, 0)), pl.BlockSpec((S, MLA_W), lambda b, i: (b, 0))],
        out_specs=[pl.BlockSpec((tq, MLA_W), lambda b, i: (b * nq + i, 0)),
                   pl.BlockSpec((tq, HEAD_PAD), lambda b, i: (b * nq + i, 0))],
        out_shape=[jax.ShapeDtypeStruct((T, MLA_W), BF16), jax.ShapeDtypeStruct((T, HEAD_PAD), F32)],
        compiler_params=_params(2),
    )(Q, K, V)


def _lane_group():
    return lax.broadcasted_iota(jnp.int32, (1, MIX_W), 1) >> GROUP_SHIFT


def _by_group(a0, a1, a2, a3):
    g = _lane_group()
    return jnp.where(g == 0, a0, jnp.where(g == 1, a1, jnp.where(g == 2, a2, a3)))


def _pool_count(si, tc, rows):
    pos = si * tc + lax.broadcasted_iota(jnp.int32, (rows, MIX_W), 0)
    win = _by_group(*POOL_WINDOWS)
    return jnp.minimum(pos + 1, win).astype(F32)


def _pool_fwd(z, z_prev, si, tc):
    ze = jnp.concatenate([z_prev, z], axis=0)
    s1 = ze + _shift_down(ze, 1)
    s2 = s1 + _shift_down(s1, 2)
    s4 = s2 + _shift_down(s2, 4)
    s8 = s4 + _shift_down(s4, 8)
    win_sum = _by_group(s1, s2, s4, s8)[POOL_HALO:]
    return win_sum / _pool_count(si, tc, tc) - z


def _sgu_weights(w_ref):
    row = lax.broadcasted_iota(jnp.int32, (SGU_BLOCK, SGU_BLOCK), 0)
    col = lax.broadcasted_iota(jnp.int32, (SGU_BLOCK, SGU_BLOCK), 1)
    keep = (row >> CHUNK_SHIFT) >= (col >> CHUNK_SHIFT)
    return keep, [jnp.where(keep, w_ref[g], 0.0).astype(BF16) for g in range(GROUPS)]


def _sgu_mix(vn_blk, wm, bias):
    g = _lane_group()
    mixed = bias
    for k in range(GROUPS):
        mixed = mixed + jnp.where(g == k, _dot(wm[k], vn_blk), 0.0)
    return mixed


def _conv_fwd(z, z_prev, w_ref):
    ze = jnp.concatenate([z_prev, z], axis=0)
    y = w_ref[0:1, :] * _shift_down(ze, 2) + w_ref[1:2, :] * _shift_down(ze, 1) + w_ref[2:3, :] * ze
    return y[CONV_HALO:]


def _mix_specs(T, D, tc):
    base = (4 * D + SEC_MLA) // MIX_W
    cur = lambda k: pl.BlockSpec((tc, MIX_W), lambda i: (i, base + k))
    prev = lambda k, halo: pl.BlockSpec((halo, MIX_W), lambda i: (jnp.maximum(i * (tc // halo) - 1, 0), base + k))
    nxt = lambda k, halo: pl.BlockSpec((halo, MIX_W), lambda i: (jnp.minimum((i + 1) * (tc // halo), T // halo - 1), base + k))
    return cur, prev, nxt


def _mixers_fwd(P, D, S, pool_bd, pool_scale, g_v, sgu_w, sgu_bias, conv_w):
    T = P.shape[0]
    tc = _tile(S, 512, SGU_BLOCK)
    n_si = S // tc
    cur, prev, _ = _mix_specs(T, D, tc)

    def body(z_ref, zp_ref, u_ref, v_ref, b_ref, c_ref, x_ref, cp_ref, xp_ref,
             pw_ref, ps_ref, gv_ref, sw_ref, sb_ref, cw_ref, ob_ref, oc_ref, od_ref):
        si = pl.program_id(0) % n_si
        first = si == 0
        z = z_ref[...]
        pooled = _pool_fwd(z, jnp.where(first, 0.0, zp_ref[...]), si, tc)
        ob_ref[...] = (_dot(pooled.astype(BF16), pw_ref[...]) * ps_ref[...]).astype(BF16)

        v = v_ref[...]
        vn = (v * _rms_r(v) * gv_ref[...]).astype(BF16)
        _, wm = _sgu_weights(sw_ref)
        for blk in range(tc // SGU_BLOCK):
            rows = slice(blk * SGU_BLOCK, (blk + 1) * SGU_BLOCK)
            oc_ref[rows, :] = (u_ref[rows, :] * _sgu_mix(vn[rows], wm, sb_ref[...])).astype(BF16)

        zc = c_ref[...] * x_ref[...]
        zc_prev = jnp.where(first, 0.0, cp_ref[...] * xp_ref[...])
        od_ref[...] = (b_ref[...] * _conv_fwd(zc, zc_prev, cw_ref)).astype(BF16)

    out = pl.BlockSpec((tc, MIX_W), lambda i: (i, 0))
    return pl.pallas_call(
        body, name="mixers_fwd", grid=(T // tc,),
        in_specs=[cur(0), prev(0, POOL_HALO), cur(1), cur(2), cur(3), cur(4), cur(5), prev(4, CONV_HALO), prev(5, CONV_HALO),
                  _full((MIX_W, MIX_W)), _full((1, MIX_W)), _full((1, MIX_W)), _full((GROUPS, SGU_BLOCK, SGU_BLOCK)),
                  _full((SGU_BLOCK, MIX_W)), _full((CONV_HALO, MIX_W))],
        out_specs=[out, out, out],
        out_shape=[jax.ShapeDtypeStruct((T, MIX_W), BF16)] * 3,
        compiler_params=_params(1),
    )(P, P, P, P, P, P, P, P, P, pool_bd, pool_scale, g_v, sgu_w, sgu_bias, conv_w)


def _merge_fwd(x, P, A, Bm, C, Dv, wa, wb, wc, wd, wout, g_post):
    T, D = x.shape
    tm = _tile(T, 256)

    def body(x_ref, lg_ref, a_ref, b_ref, c_ref, d_ref, wa_ref, wb_ref, wc_ref, wd_ref, wo_ref, g_ref,
             x1_ref, mg_ref, o_ref):
        merged = jnp.zeros((tm, D), F32)
        for k, (br, w) in enumerate(((a_ref, wa_ref), (b_ref, wb_ref), (c_ref, wc_ref), (d_ref, wd_ref))):
            merged = merged + _sigmoid(lg_ref[:, k * D:(k + 1) * D]) * _dot(br[...], w[...])
        mg = merged.astype(BF16)
        mg_ref[...] = mg
        o = _dot(mg, wo_ref[...])
        o_ref[...] = o
        x1_ref[...] = x_ref[...] + o * _rms_r(o) * g_ref[...]

    row = lambda w: pl.BlockSpec((tm, w), lambda i: (i, 0))
    return pl.pallas_call(
        body, name="merge_fwd", grid=(T // tm,),
        in_specs=[row(D), row(4 * D), row(MLA_W), row(MIX_W), row(MIX_W), row(MIX_W),
                  _full((MLA_W, D)), _full((MIX_W, D)), _full((MIX_W, D)), _full((MIX_W, D)), _full((D, D)), _full((1, D))],
        out_specs=[row(D), row(D), row(D)],
        out_shape=[jax.ShapeDtypeStruct((T, D), F32), jax.ShapeDtypeStruct((T, D), BF16), jax.ShapeDtypeStruct((T, D), F32)],
        compiler_params=_params(1),
    )(x, P, A, Bm, C, Dv, wa, wb, wc, wd, wout, g_post)


def _ffn_tiles(T, FF):
    return _tile(T, 1024), _tile(FF, 256, 128)


def _ffn_fwd(x1, g_pre, wg, wu, wdn, g_post):
    T, D = x1.shape
    FF = wg.shape[1]
    tm, tf = _ffn_tiles(T, FF)
    nf = FF // tf

    def body(x_ref, gp_ref, wg_ref, wu_ref, wd_ref, gq_ref, x2_ref, h_ref, gt_ref, up_ref, f_ref):
        j = pl.program_id(1)

        @pl.when(j == 0)
        def _():
            xv = x_ref[...]
            h_ref[...] = (xv * _rms_r(xv) * gp_ref[...]).astype(BF16)
            f_ref[...] = jnp.zeros_like(f_ref)

        h = h_ref[...]
        gt = _dot(h, wg_ref[...])
        up = _dot(h, wu_ref[...])
        gt_ref[...] = gt.astype(BF16)
        up_ref[...] = up.astype(BF16)
        a = (gt * _sigmoid(gt) * up).astype(BF16)
        f_ref[...] += _dot(a, wd_ref[...])

        @pl.when(j == nf - 1)
        def _():
            f = f_ref[...]
            x2_ref[...] = x_ref[...] + f * _rms_r(f) * gq_ref[...]

    row = pl.BlockSpec((tm, D), lambda i, j: (i, 0))
    col = pl.BlockSpec((tm, tf), lambda i, j: (i, j))
    return pl.pallas_call(
        body, name="ffn_fwd", grid=(T // tm, nf),
        in_specs=[row, _full((1, D)), pl.BlockSpec((D, tf), lambda i, j: (0, j)), pl.BlockSpec((D, tf), lambda i, j: (0, j)),
                  pl.BlockSpec((tf, D), lambda i, j: (j, 0)), _full((1, D))],
        out_specs=[row, row, col, col, row],
        out_shape=[jax.ShapeDtypeStruct((T, D), F32), jax.ShapeDtypeStruct((T, D), BF16),
                   jax.ShapeDtypeStruct((T, FF), BF16), jax.ShapeDtypeStruct((T, FF), BF16), jax.ShapeDtypeStruct((T, D), F32)],
        compiler_params=_params(2),
    )(x1, g_pre, wg, wu, wdn, g_post)


def _loss_grad(y, target):
    T, D = y.shape
    tm = _tile(T, 512)

    def body(y_ref, t_ref, l_ref, dy_ref):
        @pl.when(pl.program_id(0) == 0)
        def _():
            l_ref[...] = jnp.zeros_like(l_ref)

        d = y_ref[...] - t_ref[...]
        dy_ref[...] = d * (1.0 / D)
        e = jnp.sum((d * d).reshape(tm // 8, 8, D), axis=0)
        part = e[:, 0:128]
        for k in range(1, D // 128):
            part = part + e[:, k * 128:(k + 1) * 128]
        l_ref[...] += part * (0.5 / D)

    row = pl.BlockSpec((tm, D), lambda i: (i, 0))
    return pl.pallas_call(
        body, name="loss_grad", grid=(T // tm,),
        in_specs=[row, row], out_specs=[_full((8, 128)), row],
        out_shape=[jax.ShapeDtypeStruct((8, 128), F32), jax.ShapeDtypeStruct((T, D), F32)],
        compiler_params=_params(1),
    )(y, target)


def _matmul_tn(a, b, name):
    T, M = a.shape
    N = b.shape[1]
    tm, tn, tk = _tile(M, 1024, 128), _tile(N, 1536, 128), _tile(T, 512)

    def body(a_ref, b_ref, o_ref):
        @pl.when(pl.program_id(2) == 0)
        def _():
            o_ref[...] = jnp.zeros_like(o_ref)

        o_ref[...] += _dot_tn(a_ref[...], b_ref[...])

    return pl.pallas_call(
        body, name=name, grid=(M // tm, N // tn, T // tk),
        in_specs=[pl.BlockSpec((tk, tm), lambda i, j, k: (k, i)), pl.BlockSpec((tk, tn), lambda i, j, k: (k, j))],
        out_specs=pl.BlockSpec((tm, tn), lambda i, j, k: (i, j)),
        out_shape=jax.ShapeDtypeStruct((M, N), F32),
        compiler_params=_params(3),
    )(a, b)


def _ffn_bwd(dx2, x1, f, gt, up, g_pre, wg, wu, wdn, g_post):
    T, D = x1.shape
    FF = wg.shape[1]
    tm, tf = _ffn_tiles(T, FF)
    nf = FF // tf

    def body(dx2_ref, x1_ref, f_ref, gt_ref, up_ref, gp_ref, wg_ref, wu_ref, wd_ref, gq_ref,
             dx1_ref, df_ref, dgt_ref, dup_ref, a_ref, dgp_ref, dgq_ref, dh_acc):
        i, j = pl.program_id(0), pl.program_id(1)

        @pl.when((i == 0) & (j == 0))
        def _():
            dgp_ref[...] = jnp.zeros_like(dgp_ref)
            dgq_ref[...] = jnp.zeros_like(dgq_ref)

        @pl.when(j == 0)
        def _():
            df, dg = _rms_bwd(dx2_ref[...], f_ref[...], gq_ref[...])
            df_ref[...] = df.astype(BF16)
            dgq_ref[...] += dg
            dh_acc[...] = jnp.zeros_like(dh_acc)

        da = _dot_nt(df_ref[...], wd_ref[...])
        gt = gt_ref[...].astype(F32)
        up = up_ref[...].astype(F32)
        sig = _sigmoid(gt)
        silu = gt * sig
        dgt = (da * up * (sig * (1.0 + gt * (1.0 - sig)))).astype(BF16)
        dup = (da * silu).astype(BF16)
        dgt_ref[...] = dgt
        dup_ref[...] = dup
        a_ref[...] = (silu * up).astype(BF16)
        dh_acc[...] += _dot_nt(dgt, wg_ref[...]) + _dot_nt(dup, wu_ref[...])

        @pl.when(j == nf - 1)
        def _():
            dx, dg = _rms_bwd(dh_acc[...], x1_ref[...], gp_ref[...])
            dx1_ref[...] = dx2_ref[...] + dx
            dgp_ref[...] += dg

    row = pl.BlockSpec((tm, D), lambda i, j: (i, 0))
    col = pl.BlockSpec((tm, tf), lambda i, j: (i, j))
    return pl.pallas_call(
        body, name="ffn_bwd", grid=(T // tm, nf),
        in_specs=[row, row, row, col, col, _full((1, D)), pl.BlockSpec((D, tf), lambda i, j: (0, j)),
                  pl.BlockSpec((D, tf), lambda i, j: (0, j)), pl.BlockSpec((tf, D), lambda i, j: (j, 0)), _full((1, D))],
        out_specs=[row, row, col, col, col, _full((1, D)), _full((1, D))],
        out_shape=[jax.ShapeDtypeStruct((T, D), F32), jax.ShapeDtypeStruct((T, D), BF16),
                   jax.ShapeDtypeStruct((T, FF), BF16), jax.ShapeDtypeStruct((T, FF), BF16), jax.ShapeDtypeStruct((T, FF), BF16),
                   jax.ShapeDtypeStruct((1, D), F32), jax.ShapeDtypeStruct((1, D), F32)],
        scratch_shapes=[pltpu.VMEM((tm, D), F32)],
        compiler_params=_params(2),
    )(dx2, x1, f, gt, up, g_pre, wg, wu, wdn, g_post)


def _merge_bwd(dx1, o, P, A, Bm, C, Dv, wa, wb, wc, wd, wout, g_post):
    T, D = o.shape
    tm = _tile(T, 256)

    def body(dx1_ref, o_ref, lg_ref, a_ref, b_ref, c_ref, d_ref, wa_ref, wb_ref, wc_ref, wd_ref, wo_ref, g_ref,
             do_ref, dlg_ref, dya_ref, dyb_ref, dyc_ref, dyd_ref, da_ref, db_ref, dc_ref, dd_ref, dg_ref):
        @pl.when(pl.program_id(0) == 0)
        def _():
            dg_ref[...] = jnp.zeros_like(dg_ref)

        d_o, dg = _rms_bwd(dx1_ref[...], o_ref[...], g_ref[...])
        dg_ref[...] += dg
        d_o = d_o.astype(BF16)
        do_ref[...] = d_o
        dm = _dot_nt(d_o, wo_ref[...])
        branches = ((a_ref, wa_ref, dya_ref, da_ref), (b_ref, wb_ref, dyb_ref, db_ref),
                    (c_ref, wc_ref, dyc_ref, dc_ref), (d_ref, wd_ref, dyd_ref, dd_ref))
        for k, (br, w, dy_ref, dbr_ref) in enumerate(branches):
            gate = _sigmoid(lg_ref[:, k * D:(k + 1) * D])
            y = _dot(br[...], w[...])
            dlg_ref[:, k * D:(k + 1) * D] = (dm * y * gate * (1.0 - gate)).astype(BF16)
            dy = (dm * gate).astype(BF16)
            dy_ref[...] = dy
            dbr_ref[...] = _dot_nt(dy, w[...]).astype(dbr_ref.dtype)

    row = lambda w: pl.BlockSpec((tm, w), lambda i: (i, 0))
    return pl.pallas_call(
        body, name="merge_bwd", grid=(T // tm,),
        in_specs=[row(D), row(D), row(4 * D), row(MLA_W), row(MIX_W), row(MIX_W), row(MIX_W),
                  _full((MLA_W, D)), _full((MIX_W, D)), _full((MIX_W, D)), _full((MIX_W, D)), _full((D, D)), _full((1, D))],
        out_specs=[row(D), row(4 * D), row(D), row(D), row(D), row(D), row(MLA_W), row(MIX_W), row(MIX_W), row(MIX_W),
                   _full((1, D))],
        out_shape=[jax.ShapeDtypeStruct((T, D), BF16), jax.ShapeDtypeStruct((T, 4 * D), BF16)]
        + [jax.ShapeDtypeStruct((T, D), BF16)] * 4
        + [jax.ShapeDtypeStruct((T, MLA_W), BF16)] + [jax.ShapeDtypeStruct((T, MIX_W), F32)] * 3
        + [jax.ShapeDtypeStruct((1, D), F32)],
        compiler_params=_params(1),
    )(dx1, o, P, A, Bm, C, Dv, wa, wb, wc, wd, wout, g_post)


def _mixers_bwd(P, D, S, dBm, dC, dDv, pool_bd, pool_scale, g_v, sgu_w, sgu_bias, conv_w):
    T = P.shape[0]
    tc = _tile(S, 512, SGU_BLOCK)
    n_si = S // tc
    cur, prev, nxt = _mix_specs(T, D, tc)
    n_blk = tc // SGU_BLOCK

    def body(z_ref, zp_ref, u_ref, v_ref, b_ref, c_ref, x_ref, cp_ref, xp_ref, bn_ref,
             dbm_ref, dbmn_ref, dc_ref, ddv_ref, ddvn_ref,
             pw_ref, ps_ref, gv_ref, sw_ref, sb_ref, cw_ref,
             dp_ref, dpw_ref, dps_ref, dgv_ref, dsw_ref, dsb_ref, dcw_ref, dvn_acc):
        si = pl.program_id(0) % n_si
        first, last = si == 0, si == n_si - 1

        @pl.when(pl.program_id(0) == 0)
        def _():
            for r in (dpw_ref, dps_ref, dgv_ref, dsw_ref, dsb_ref, dcw_ref):
                r[...] = jnp.zeros_like(r)

        z = z_ref[...]
        pooled = _pool_fwd(z, jnp.where(first, 0.0, zp_ref[...]), si, tc).astype(BF16)
        dbm = dbm_ref[...]
        dps_ref[...] += jnp.sum(dbm * _dot(pooled, pw_ref[...]), axis=0, keepdims=True)
        dmix = (jnp.concatenate([dbm, jnp.where(last, 0.0, dbmn_ref[...])], axis=0) * ps_ref[...]).astype(BF16)
        dpw_ref[...] += _dot_tn(pooled, dmix[:tc])
        dpool = _dot_nt(dmix, pw_ref[...])
        e = dpool / _pool_count(si, tc, tc + POOL_HALO)
        f1 = e + _shift_up(e, 1)
        f2 = f1 + _shift_up(f1, 2)
        f4 = f2 + _shift_up(f2, 4)
        f8 = f4 + _shift_up(f4, 8)
        dp_ref[:, 0:MIX_W] = (_by_group(f1, f2, f4, f8)[:tc] - dpool[:tc]).astype(BF16)

        v = v_ref[...]
        vn = (v * _rms_r(v) * gv_ref[...]).astype(BF16)
        keep, wm = _sgu_weights(sw_ref)
        g = _lane_group()
        for blk in range(n_blk):
            rows = slice(blk * SGU_BLOCK, (blk + 1) * SGU_BLOCK)
            vb = vn[rows]
            dc = dc_ref[rows, :]
            dp_ref[rows, MIX_W:2 * MIX_W] = (dc * _sgu_mix(vb, wm, sb_ref[...])).astype(BF16)
            dmx = dc * u_ref[rows, :]
            dsb_ref[...] += dmx
            dvn = jnp.zeros((SGU_BLOCK, MIX_W), F32)
            for k in range(GROUPS):
                dmk = jnp.where(g == k, dmx, 0.0).astype(BF16)
                dsw_ref[k] += jnp.where(keep, _dot_nt(dmk, vb), 0.0)
                dvn = dvn + _dot_tn(wm[k], dmk)
            dvn_acc[rows, :] = dvn
        dv, dg = _rms_bwd(dvn_acc[...], v, gv_ref[...])
        dgv_ref[...] += dg
        dp_ref[:, 2 * MIX_W:3 * MIX_W] = dv.astype(BF16)

        cg, xg, bg = c_ref[...], x_ref[...], b_ref[...]
        zc = cg * xg
        ze = jnp.concatenate([jnp.where(first, 0.0, cp_ref[...] * xp_ref[...]), zc], axis=0)
        z1, z2 = _shift_down(ze, 1)[CONV_HALO:], _shift_down(ze, 2)[CONV_HALO:]
        ddv = ddv_ref[...]
        y = cw_ref[0:1, :] * z2 + cw_ref[1:2, :] * z1 + cw_ref[2:3, :] * zc
        dp_ref[:, 3 * MIX_W:4 * MIX_W] = (ddv * y).astype(BF16)
        dy = ddv * bg
        dcw_ref[0:1, :] += jnp.sum(dy * z2, axis=0, keepdims=True)
        dcw_ref[1:2, :] += jnp.sum(dy * z1, axis=0, keepdims=True)
        dcw_ref[2:3, :] += jnp.sum(dy * zc, axis=0, keepdims=True)
        dye = jnp.concatenate([dy, jnp.where(last, 0.0, ddvn_ref[...] * bn_ref[...])], axis=0)
        dz = (cw_ref[2:3, :] * dye + cw_ref[1:2, :] * _shift_up(dye, 1) + cw_ref[0:1, :] * _shift_up(dye, 2))[:tc]
        dp_ref[:, 4 * MIX_W:5 * MIX_W] = (dz * xg).astype(BF16)
        dp_ref[:, 5 * MIX_W:6 * MIX_W] = (dz * cg).astype(BF16)

    grad = lambda halo: pl.BlockSpec((halo, MIX_W), lambda i: (jnp.minimum((i + 1) * (tc // halo), T // halo - 1), 0))
    out = pl.BlockSpec((tc, MIX_W), lambda i: (i, 0))
    return pl.pallas_call(
        body, name="mixers_bwd", grid=(T // tc,),
        in_specs=[cur(0), prev(0, POOL_HALO), cur(1), cur(2), cur(3), cur(4), cur(5), prev(4, CONV_HALO), prev(5, CONV_HALO),
                  nxt(3, CONV_HALO), out, grad(POOL_HALO), out, out, grad(CONV_HALO),
                  _full((MIX_W, MIX_W)), _full((1, MIX_W)), _full((1, MIX_W)), _full((GROUPS, SGU_BLOCK, SGU_BLOCK)),
                  _full((SGU_BLOCK, MIX_W)), _full((CONV_HALO, MIX_W))],
        out_specs=[pl.BlockSpec((tc, SEC_MIX), lambda i: (i, 0)), _full((MIX_W, MIX_W)), _full((1, MIX_W)), _full((1, MIX_W)),
                   _full((GROUPS, SGU_BLOCK, SGU_BLOCK)), _full((SGU_BLOCK, MIX_W)), _full((CONV_HALO, MIX_W))],
        out_shape=[jax.ShapeDtypeStruct((T, SEC_MIX), BF16), jax.ShapeDtypeStruct((MIX_W, MIX_W), F32),
                   jax.ShapeDtypeStruct((1, MIX_W), F32), jax.ShapeDtypeStruct((1, MIX_W), F32),
                   jax.ShapeDtypeStruct((GROUPS, SGU_BLOCK, SGU_BLOCK), F32), jax.ShapeDtypeStruct((SGU_BLOCK, MIX_W), F32),
                   jax.ShapeDtypeStruct((CONV_HALO, MIX_W), F32)],
        scratch_shapes=[pltpu.VMEM((tc, MIX_W), F32)],
        compiler_params=_params(1),
    )(P, P, P, P, P, P, P, P, P, P, dBm, dBm, dC, dDv, dDv, pool_bd, pool_scale, g_v, sgu_w, sgu_bias, conv_w)


def _attn_delta(O, dO):
    T = O.shape[0]
    tm = _tile(T, 512)

    def body(o_ref, do_ref, d_ref):
        lane = lax.broadcasted_iota(jnp.int32, (tm, HEAD_PAD), 1)
        out = jnp.zeros((tm, HEAD_PAD), F32)
        for h in range(HEADS):
            hs = slice(h * HEAD_PAD, (h + 1) * HEAD_PAD)
            s = jnp.sum(o_ref[:, hs].astype(F32) * do_ref[:, hs].astype(F32), axis=-1, keepdims=True)
            out = jnp.where(lane == h, s, out)
        d_ref[...] = out

    row = lambda w: pl.BlockSpec((tm, w), lambda i: (i, 0))
    return pl.pallas_call(
        body, name="attn_delta", grid=(T // tm,), in_specs=[row(MLA_W), row(MLA_W)], out_specs=row(HEAD_PAD),
        out_shape=jax.ShapeDtypeStruct((T, HEAD_PAD), F32), compiler_params=_params(1),
    )(O, dO)


def _flash_bwd(Q, K, V, dO, lse, delta, S):
    T = Q.shape[0]
    n_seq = T // S
    tq = _tile(S, 256, 128)
    nq = S // tq

    def body(k_ref, v_ref, q_ref, do_ref, lse_ref, dl_ref, dq_ref, dk_ref, dv_ref):
        kb = pl.program_id(1)

        @pl.when(kb == 0)
        def _():
            dq_ref[...] = jnp.zeros_like(dq_ref)

        mask = _chunk_mask(tq, tq)
        lane = lax.broadcasted_iota(jnp.int32, (tq, HEAD_PAD), 1)
        for h in range(HEADS):
            hs = slice(h * HEAD_PAD, (h + 1) * HEAD_PAD)
            k = k_ref[:, hs]
            v = v_ref[:, hs]

            def update(qi, carry, masked):
                dk, dv = carry
                rows = pl.ds(pl.multiple_of(qi * tq, tq), tq)
                q = q_ref[rows, hs]
                do = do_ref[rows, hs]
                lse_h = jnp.sum(jnp.where(lane == h, lse_ref[rows, :], 0.0), axis=-1, keepdims=True)
                dl_h = jnp.sum(jnp.where(lane == h, dl_ref[rows, :], 0.0), axis=-1, keepdims=True)
                s = _dot_nt(q, k) * ATTN_SCALE
                if masked:
                    s = jnp.where(mask, s, NEG_INF)
                p = jnp.exp(s - lse_h)
                dv = dv + _dot_tn(p.astype(BF16), do)
                ds = (p * (_dot_nt(do, v) - dl_h) * ATTN_SCALE).astype(BF16)
                dk = dk + _dot_tn(ds, q)
                dq_ref[rows, hs] += _dot(ds, k)
                return dk, dv

            zero = jnp.zeros((tq, HEAD_PAD), F32)
            carry = update(kb, (zero, zero), True)
            dk, dv = lax.fori_loop(kb + 1, nq, lambda qi, c: update(qi, c, False), carry)
            dk_ref[:, hs] = dk
            dv_ref[:, hs] = dv

    tile = pl.BlockSpec((tq, MLA_W), lambda b, i: (b * nq + i, 0))
    seq = lambda w: pl.BlockSpec((S, w), lambda b, i: (b, 0))
    return pl.pallas_call(
        body, name="flash_bwd", grid=(n_seq, nq),
        in_specs=[tile, tile, seq(MLA_W), seq(MLA_W), seq(HEAD_PAD), seq(HEAD_PAD)],
        out_specs=[seq(MLA_W), tile, tile],
        out_shape=[jax.ShapeDtypeStruct((T, MLA_W), F32)] * 3,
        compiler_params=_params(2),
    )(K, V, Q, dO, lse, delta)


def _mla_bwd_post(P, D, S, dQ, dK, dV, g_cq, g_ckv, wuq, wukv, rope_c, rope_sa, rope_sb):
    T = P.shape[0]
    tm = _tile(S, 512)
    n_si = S // tm
    base = 4 * D

    def body(cq_ref, ckv_ref, dq_ref, dk_ref, dv_ref, gq_ref, gkv_ref, wq_ref, wkv_ref, c_ref, sa_ref, sb_ref,
             dp_ref, dqo_ref, dkvo_ref, dgq_ref, dgkv_ref):
        @pl.when(pl.program_id(0) == 0)
        def _():
            dgq_ref[...] = jnp.zeros_like(dgq_ref)
            dgkv_ref[...] = jnp.zeros_like(dgkv_ref)

        c, sa, sb = c_ref[...], sa_ref[...], sb_ref[...]
        dq = _rope_t(dq_ref[...], jnp.tile(c, (1, HEADS)), jnp.tile(sa, (1, HEADS)), jnp.tile(sb, (1, HEADS))).astype(BF16)
        dqo_ref[...] = dq
        dcq, dg = _rms_bwd(_dot_nt(dq, wq_ref[...]), cq_ref[...], gq_ref[...])
        dgq_ref[...] += dg
        dp_ref[:, 0:Q_LORA] = dcq.astype(BF16)

        dk = dk_ref[...]
        dkv = jnp.concatenate([dk.astype(BF16), dv_ref[...].astype(BF16)], axis=1)
        dkvo_ref[...] = dkv
        dckv, dg = _rms_bwd(_dot_nt(dkv, wkv_ref[...]), ckv_ref[...], gkv_ref[...])
        dgkv_ref[...] += dg
        dp_ref[:, Q_LORA:Q_LORA + KV_LORA] = dckv.astype(BF16)

        dkr = dk[:, 0:HEAD_PAD]
        for h in range(1, HEADS):
            dkr = dkr + dk[:, h * HEAD_PAD:(h + 1) * HEAD_PAD]
        lane = lax.broadcasted_iota(jnp.int32, (1, HEAD_PAD), 1)
        rope_lanes = (lane >= QK_NOPE) & (lane < QK_NOPE + QK_ROPE)
        dp_ref[:, Q_LORA + KV_LORA:SEC_MLA] = jnp.where(rope_lanes, _rope_t(dkr, c, sa, sb), 0.0).astype(BF16)

    tab = pl.BlockSpec((tm, HEAD_PAD), lambda i: (i % n_si, 0))
    row = lambda w: pl.BlockSpec((tm, w), lambda i: (i, 0))
    return pl.pallas_call(
        body, name="mla_bwd_post", grid=(T // tm,),
        in_specs=[pl.BlockSpec((tm, Q_LORA), lambda i: (i, base // Q_LORA)),
                  pl.BlockSpec((tm, KV_LORA), lambda i: (i, (base + Q_LORA) // KV_LORA)),
                  row(MLA_W), row(MLA_W), row(MLA_W),
                  _full((1, Q_LORA)), _full((1, KV_LORA)), _full((Q_LORA, MLA_W)), _full((KV_LORA, 2 * MLA_W)), tab, tab, tab],
        out_specs=[row(SEC_MLA), row(MLA_W), row(2 * MLA_W), _full((1, Q_LORA)), _full((1, KV_LORA))],
        out_shape=[jax.ShapeDtypeStruct((T, SEC_MLA), BF16), jax.ShapeDtypeStruct((T, MLA_W), BF16),
                   jax.ShapeDtypeStruct((T, 2 * MLA_W), BF16), jax.ShapeDtypeStruct((1, Q_LORA), F32),
                   jax.ShapeDtypeStruct((1, KV_LORA), F32)],
        compiler_params=_params(1),
    )(P, P, dQ, dK, dV, g_cq, g_ckv, wuq, wukv, rope_c, rope_sa, rope_sb)


def _proj_bwd(dx1, x, g, dPg, dPa, dPm, w_gates, w_mla, w_mix):
    T, D = x.shape
    tm = _tile(T, 256)

    def body(dx1_ref, x_ref, g_ref, dg_ref_in, da_ref, dm_ref, wg_ref, wa_ref, wm_ref, dx_ref, dg_ref):
        @pl.when(pl.program_id(0) == 0)
        def _():
            dg_ref[...] = jnp.zeros_like(dg_ref)

        dh = _dot_nt(dg_ref_in[...], wg_ref[...]) + _dot_nt(da_ref[...], wa_ref[...]) + _dot_nt(dm_ref[...], wm_ref[...])
        dx, dg = _rms_bwd(dh, x_ref[...], g_ref[...])
        dx_ref[...] = dx1_ref[...] + dx
        dg_ref[...] += dg

    row = lambda w: pl.BlockSpec((tm, w), lambda i: (i, 0))
    return pl.pallas_call(
        body, name="proj_bwd", grid=(T // tm,),
        in_specs=[row(D), row(D), _full((1, D)), row(4 * D), row(SEC_MLA), row(SEC_MIX),
                  _full((D, 4 * D)), _full((D, SEC_MLA)), _full((D, SEC_MIX))],
        out_specs=[row(D), _full((1, D))],
        out_shape=[jax.ShapeDtypeStruct((T, D), F32), jax.ShapeDtypeStruct((1, D), F32)],
        compiler_params=_params(1),
    )(dx1, x, g, dPg, dPa, dPm, w_gates, w_mla, w_mix)


def _adamw(w, g, m, v, name):
    R, C = w.shape
    tr = _tile(R, max(8, (1 << 19) // C))

    def body(w_ref, g_ref, m_ref, v_ref, d_ref, mo_ref, vo_ref):
        gv = g_ref[...]
        mn = ADAM_B1 * m_ref[...] + (1.0 - ADAM_B1) * gv
        vn = ADAM_B2 * v_ref[...] + (1.0 - ADAM_B2) * (gv * gv)
        mo_ref[...] = mn
        vo_ref[...] = vn
        m_hat = mn / (1.0 - ADAM_B1 ** ADAM_STEP)
        v_hat = vn / (1.0 - ADAM_B2 ** ADAM_STEP)
        d_ref[...] = -ADAM_LR * (m_hat / (jnp.sqrt(v_hat) + ADAM_EPS) + ADAM_WD * w_ref[...])

    blk = pl.BlockSpec((tr, C), lambda i: (i, 0))
    return pl.pallas_call(
        body, name=name, grid=(R // tr,), in_specs=[blk] * 4, out_specs=[blk] * 3,
        out_shape=[jax.ShapeDtypeStruct((R, C), F32)] * 3, compiler_params=_params(1),
    )(w, g, m, v)


def _add_halves(G, recv, half):
    n, R, C = G.shape
    hr = R // 2
    tr = _tile(hr, 512, 16)
    nb = hr // tr

    def body(half_ref, g_ref, r_ref, o_ref):
        o_ref[...] = (g_ref[...] + r_ref[...]).astype(BF16)

    grid_spec = pltpu.PrefetchScalarGridSpec(
        num_scalar_prefetch=1, grid=(n, nb),
        in_specs=[pl.BlockSpec((1, tr, C), lambda k, i, h: (k, h[0] * nb + i, 0)),
                  pl.BlockSpec((1, tr, C), lambda k, i, h: (k, i, 0))],
        out_specs=pl.BlockSpec((1, tr, C), lambda k, i, h: (k, i, 0)))
    return pl.pallas_call(
        body, name="rs_add_halves", grid_spec=grid_spec,
        out_shape=jax.ShapeDtypeStruct((n, hr, C), BF16), compiler_params=_params(2),
    )(half.reshape(1).astype(jnp.int32), G, recv)


def _sum_slots(slots):
    n, hr, C = slots.shape
    tr = _tile(hr, 512, 16)

    def body(s_ref, o_ref):
        acc = s_ref[0].astype(F32)
        for k in range(1, n):
            acc = acc + s_ref[k].astype(F32)
        o_ref[...] = acc

    return pl.pallas_call(
        body, name="rs_sum_slots", grid=(hr // tr,),
        in_specs=[pl.BlockSpec((n, tr, C), lambda i: (0, i, 0))], out_specs=pl.BlockSpec((tr, C), lambda i: (i, 0)),
        out_shape=jax.ShapeDtypeStruct((hr, C), F32), compiler_params=_params(1),
    )(slots)


HBM = pl.BlockSpec(memory_space=pltpu.HBM)


def _place():
    x, y, c = lax.axis_index("x"), lax.axis_index("y"), lax.axis_index("c")
    return x, y, c, 2 * x + y


def _chip_device(chip, c):
    return (chip // 2, chip % 2, c)


def _gather_weights(shard):
    R, C = shard.shape
    hr = R // 2

    def body(w_ref, o_ref, send_sems, recv_sems, local_sem):
        x, y, c, me = _place()
        sibling = (x, y, 1 - c)
        mine = pltpu.make_async_copy(w_ref, o_ref.at[me], local_sem)
        mine.start()

        def copy(k, src_chip, half, to):
            rows = pl.ds(half * hr, hr)
            src = w_ref.at[rows, :] if src_chip is None else o_ref.at[src_chip, rows, :]
            dst = o_ref.at[me if src_chip is None else src_chip, rows, :]
            return pltpu.make_async_remote_copy(src_ref=src, dst_ref=dst, send_sem=send_sems.at[k], recv_sem=recv_sems.at[k],
                                                device_id=to, device_id_type=MESH)

        first = [copy(d - 1, None, c, _chip_device(me ^ d, c)) for d in (1, 2, 3)]
        for cp in first:
            cp.start()
        passed = [copy(2 + d, me ^ d, c, sibling) for d in (1, 2, 3)]
        for d in (1, 2, 3):
            copy(d - 1, me ^ d, c, sibling).wait_recv()
            passed[d - 1].start()
        for d in (1, 2, 3):
            copy(2 + d, me ^ d, 1 - c, sibling).wait_recv()
        for cp in first + passed:
            cp.wait_send()
        mine.wait()

    return pl.pallas_call(
        body, name="gather_weights", in_specs=[HBM], out_specs=HBM,
        out_shape=jax.ShapeDtypeStruct((4, R, C), shard.dtype),
        scratch_shapes=[pltpu.SemaphoreType.DMA((6,)), pltpu.SemaphoreType.DMA((6,)), pltpu.SemaphoreType.DMA],
    )(shard)


def _exchange_halves(G):
    n, R, C = G.shape
    hr = R // 2

    def body(g_ref, o_ref, send_sem, recv_sem):
        x, y, c, _ = _place()
        cp = pltpu.make_async_remote_copy(src_ref=g_ref.at[:, pl.ds((1 - c) * hr, hr), :], dst_ref=o_ref,
                                          send_sem=send_sem, recv_sem=recv_sem, device_id=(x, y, 1 - c), device_id_type=MESH)
        cp.start()
        cp.wait()

    return pl.pallas_call(
        body, name="rs_exchange_halves", in_specs=[HBM], out_specs=HBM,
        out_shape=jax.ShapeDtypeStruct((n, hr, C), G.dtype),
        scratch_shapes=[pltpu.SemaphoreType.DMA, pltpu.SemaphoreType.DMA],
    )(G)


def _scatter_partials(H):
    n, hr, C = H.shape

    def body(h_ref, o_ref, send_sems, recv_sems, local_sem):
        x, y, c, me = _place()
        mine = pltpu.make_async_copy(h_ref.at[me], o_ref.at[me], local_sem)
        mine.start()
        sends = [pltpu.make_async_remote_copy(src_ref=h_ref.at[me ^ d], dst_ref=o_ref.at[me], send_sem=send_sems.at[d - 1],
                                              recv_sem=recv_sems.at[d - 1], device_id=_chip_device(me ^ d, c), device_id_type=MESH)
                 for d in (1, 2, 3)]
        for cp in sends:
            cp.start()
        for d in (1, 2, 3):
            pltpu.make_async_remote_copy(src_ref=h_ref.at[me ^ d], dst_ref=o_ref.at[me ^ d], send_sem=send_sems.at[d - 1],
                                         recv_sem=recv_sems.at[d - 1], device_id=_chip_device(me ^ d, c),
                                         device_id_type=MESH).wait_recv()
        for cp in sends:
            cp.wait_send()
        mine.wait()

    return pl.pallas_call(
        body, name="rs_scatter_partials", in_specs=[HBM], out_specs=HBM,
        out_shape=jax.ShapeDtypeStruct((n, hr, C), H.dtype),
        scratch_shapes=[pltpu.SemaphoreType.DMA((3,)), pltpu.SemaphoreType.DMA((3,)), pltpu.SemaphoreType.DMA],
    )(H)


def _join_halves(Sh):
    hr, C = Sh.shape

    def body(s_ref, o_ref, send_sem, recv_sem, local_sem):
        x, y, c, _ = _place()
        mine = pltpu.make_async_copy(s_ref, o_ref.at[pl.ds(c * hr, hr), :], local_sem)
        mine.start()
        cp = pltpu.make_async_remote_copy(src_ref=s_ref, dst_ref=o_ref.at[pl.ds(c * hr, hr), :], send_sem=send_sem,
                                          recv_sem=recv_sem, device_id=(x, y, 1 - c), device_id_type=MESH)
        cp.start()
        pltpu.make_async_remote_copy(src_ref=s_ref, dst_ref=o_ref.at[pl.ds((1 - c) * hr, hr), :], send_sem=send_sem,
                                     recv_sem=recv_sem, device_id=(x, y, 1 - c), device_id_type=MESH).wait_recv()
        cp.wait_send()
        mine.wait()

    return pl.pallas_call(
        body, name="rs_join_halves", in_specs=[HBM], out_specs=HBM,
        out_shape=jax.ShapeDtypeStruct((2 * hr, C), Sh.dtype),
        scratch_shapes=[pltpu.SemaphoreType.DMA, pltpu.SemaphoreType.DMA, pltpu.SemaphoreType.DMA],
    )(Sh)


def _all_reduce_small(v, name):
    R, C = v.shape

    def body(v_ref, o_ref, slots, send_sems, recv_sems):
        x, y, c, _ = _place()
        me = 4 * x + 2 * y + c
        slots[me] = v_ref[...]
        sends = []
        for d in range(1, 8):
            peer = me ^ d
            sends.append(pltpu.make_async_remote_copy(
                src_ref=v_ref, dst_ref=slots.at[me], send_sem=send_sems.at[d - 1], recv_sem=recv_sems.at[d - 1],
                device_id=(peer // 4, (peer // 2) % 2, peer % 2), device_id_type=MESH))
        for cp in sends:
            cp.start()
        for d in range(1, 8):
            peer = me ^ d
            pltpu.make_async_remote_copy(
                src_ref=v_ref, dst_ref=slots.at[peer], send_sem=send_sems.at[d - 1], recv_sem=recv_sems.at[d - 1],
                device_id=(peer // 4, (peer // 2) % 2, peer % 2), device_id_type=MESH).wait_recv()
        for cp in sends:
            cp.wait_send()
        acc = slots[0]
        for k in range(1, 8):
            acc = acc + slots[k]
        o_ref[...] = acc

    vm = pl.BlockSpec(memory_space=pltpu.VMEM)
    return pl.pallas_call(
        body, name=name, in_specs=[vm], out_specs=vm, out_shape=jax.ShapeDtypeStruct((R, C), F32),
        scratch_shapes=[pltpu.VMEM((8, R, C), F32), pltpu.SemaphoreType.DMA((7,)), pltpu.SemaphoreType.DMA((7,))],
    )(v)


SHARDED = ("w_in", "w_uq", "w_ukv", "conv_w", "w_br_a", "w_br_b", "w_br_c", "w_br_d", "w_out", "w_ffn_gate", "w_ffn_up",
           "w_ffn_down")
ROW_SHARDED = ("w_out", "w_ffn_down")
REPLICATED = ("g_pre_mix", "g_cq", "g_ckv", "pool_w", "pool_scale", "g_sgu_v", "sgu_w", "sgu_b", "g_post_mix", "g_pre_ffn",
              "g_post_ffn")
WEIGHTS = ("w_in", "g_pre_mix", "g_cq", "g_ckv", "w_uq", "w_ukv", "pool_w", "pool_scale", "g_sgu_v", "sgu_w", "sgu_b",
           "conv_w", "w_br_a", "w_br_b", "w_br_c", "w_br_d", "w_out", "g_post_mix", "g_pre_ffn", "w_ffn_gate", "w_ffn_up",
           "w_ffn_down", "g_post_ffn")
PACK_COLS = 1024
PACK_ROW_MULT = 64
GATHERED = tuple(n for n in SHARDED if n != "conv_w")


def _pack(parts):
    flat = jnp.concatenate([p.reshape(-1) for p in parts])
    rows = -(-flat.shape[0] // PACK_COLS)
    rows = -(-rows // PACK_ROW_MULT) * PACK_ROW_MULT
    return jnp.pad(flat, (0, rows * PACK_COLS - flat.shape[0])).reshape(rows, PACK_COLS)


def _unpack(packed, shapes):
    flat = packed.reshape(-1)
    out, o = [], 0
    for s in shapes:
        n = int(np.prod(s))
        out.append(flat[o:o + n].reshape(s))
        o += n
    return out


def _join_shards(name, parts):
    return jnp.concatenate(parts, axis=1 if name in ROW_SHARDED else parts[0].ndim - 1)


def _split_shards(name, full):
    return jnp.split(full, 4, axis=1 if name in ROW_SHARDED else full.ndim - 1)


def _pad_heads(w, real):
    lead = w.shape[:-1]
    w = w.reshape(lead + (HEADS, real))
    return jnp.pad(w, [(0, 0)] * len(lead) + [(0, 0), (0, HEAD_PAD - real)]).reshape(lead + (MLA_W,))


def _unpad_heads(w, real):
    lead = w.shape[:-1]
    return w.reshape(lead + (HEADS, HEAD_PAD))[..., :real].reshape(lead + (HEADS * real,))


IN_OFFSETS = {"cq": 0, "ckv": Q_LORA, "kr": Q_LORA + KV_LORA, "mix": Q_LORA + KV_LORA + QK_ROPE}
IN_GATES = Q_LORA + KV_LORA + QK_ROPE + SEC_MIX


def _pad_w_in(w):
    K = w.shape[0]
    z = lambda n: jnp.zeros((K, n), w.dtype)
    return jnp.concatenate([w[:, IN_GATES:], w[:, :IN_OFFSETS["kr"]], z(QK_NOPE), w[:, IN_OFFSETS["kr"]:IN_OFFSETS["mix"]],
                            z(HEAD_PAD - QK_NOPE - QK_ROPE), w[:, IN_OFFSETS["mix"]:IN_GATES]], axis=1)


def _unpad_w_in(d_gates, d_mla, d_mix):
    kr = d_mla[:, Q_LORA + KV_LORA + QK_NOPE:Q_LORA + KV_LORA + QK_NOPE + QK_ROPE]
    return jnp.concatenate([d_mla[:, :Q_LORA + KV_LORA], kr, d_mix, d_gates], axis=1)


def _rope_tables(S):
    half = QK_ROPE // 2
    inv = ROPE_THETA ** (-jnp.arange(0, QK_ROPE, 2, dtype=F32) / QK_ROPE)
    ang = jnp.arange(S, dtype=F32)[:, None] * inv[None, :]
    cos, sin = jnp.cos(ang), jnp.sin(ang)
    one, zero = jnp.ones((S, QK_NOPE), F32), jnp.zeros((S, half), F32)
    tail = HEAD_PAD - QK_NOPE - QK_ROPE
    c = jnp.concatenate([one, cos, cos, jnp.ones((S, tail), F32)], axis=1)
    sa = jnp.concatenate([0 * one, zero, sin, jnp.zeros((S, tail), F32)], axis=1)
    sb = jnp.concatenate([0 * one, -sin, zero, jnp.zeros((S, tail), F32)], axis=1)
    return c, sa, sb


def _layer_weights(full, l, D):
    w = {}
    w_in = _pad_w_in(full["w_in"][l])
    w["w_in"] = w_in
    w["w_in_gates"], w["w_in_mla"], w["w_in_mix"] = w_in[:, :4 * D], w_in[:, 4 * D:4 * D + SEC_MLA], w_in[:, 4 * D + SEC_MLA:]
    w["w_uq"] = _pad_heads(full["w_uq"][l], QK_NOPE + QK_ROPE)
    ukv = full["w_ukv"][l].reshape(KV_LORA, HEADS, QK_NOPE + V_HEAD)
    w["w_ukv"] = jnp.concatenate([_pad_heads(ukv[:, :, :QK_NOPE].reshape(KV_LORA, -1), QK_NOPE),
                                  _pad_heads(ukv[:, :, QK_NOPE:].reshape(KV_LORA, -1), V_HEAD)], axis=1)
    w["w_br_a"] = jnp.pad(full["w_br_a"][l].reshape(HEADS, V_HEAD, D), ((0, 0), (0, HEAD_PAD - V_HEAD), (0, 0))).reshape(MLA_W, D)
    for n in ("w_br_b", "w_br_c", "w_br_d", "w_out", "w_ffn_gate", "w_ffn_up", "w_ffn_down"):
        w[n] = full[n][l]
    for n in ("g_pre_mix", "g_cq", "g_ckv", "pool_scale", "g_sgu_v", "g_post_mix", "g_pre_ffn", "g_post_ffn"):
        w[n] = full[n][l].reshape(1, -1)
    pw = full["pool_w"][l]
    w["pool_bd"] = jax.scipy.linalg.block_diag(*[pw[g] for g in range(GROUPS)]).astype(BF16)
    w["sgu_w"] = full["sgu_w"][l]
    w["sgu_bias"] = jnp.repeat(full["sgu_b"][l].T, GROUP_DIM, axis=1)
    w["conv_w"] = jnp.pad(full["conv_w"][l].reshape(3, MIX_W), ((0, CONV_HALO - 3), (0, 0)))
    return w


def _layer_fwd(x, w, S, rope):
    D = x.shape[1]
    P, h = _norm_matmul(x, w["g_pre_mix"], w["w_in"], "proj_fwd")
    Q, K, V, hq, hkv = _mla_prep(P, D, w["g_cq"], w["g_ckv"], w["w_uq"], w["w_ukv"], *rope, S)
    A, lse = _flash_fwd(Q, K, V, S)
    Bm, C, Dv = _mixers_fwd(P, D, S, w["pool_bd"], w["pool_scale"], w["g_sgu_v"], w["sgu_w"], w["sgu_bias"], w["conv_w"])
    x1, merged, o = _merge_fwd(x, P, A, Bm, C, Dv, w["w_br_a"], w["w_br_b"], w["w_br_c"], w["w_br_d"], w["w_out"], w["g_post_mix"])
    x2, h2, gt, up, f = _ffn_fwd(x1, w["g_pre_ffn"], w["w_ffn_gate"], w["w_ffn_up"], w["w_ffn_down"], w["g_post_ffn"])
    saved = dict(x=x, P=P, h=h, Q=Q, K=K, V=V, hq=hq, hkv=hkv, A=A, lse=lse, Bm=Bm, C=C, Dv=Dv, x1=x1, merged=merged, o=o,
                 h2=h2, gt=gt, up=up, f=f)
    return x2, saved


def _layer_bwd(dx2, w, s, S, rope):
    D = dx2.shape[1]
    g = {}
    dx1, df, dgt, dup, act, g["g_pre_ffn"], g["g_post_ffn"] = _ffn_bwd(
        dx2, s["x1"], s["f"], s["gt"], s["up"], w["g_pre_ffn"], w["w_ffn_gate"], w["w_ffn_up"], w["w_ffn_down"], w["g_post_ffn"])
    g["w_ffn_down"] = _matmul_tn(act, df, "wgrad_ffn_down")
    g["w_ffn_gate"] = _matmul_tn(s["h2"], dgt, "wgrad_ffn_gate")
    g["w_ffn_up"] = _matmul_tn(s["h2"], dup, "wgrad_ffn_up")

    d_o, dPg, dya, dyb, dyc, dyd, dA, dBm, dC, dDv, g["g_post_mix"] = _merge_bwd(
        dx1, s["o"], s["P"], s["A"], s["Bm"], s["C"], s["Dv"], w["w_br_a"], w["w_br_b"], w["w_br_c"], w["w_br_d"], w["w_out"],
        w["g_post_mix"])
    g["w_out"] = _matmul_tn(s["merged"], d_o, "wgrad_out")
    g["w_br_a"] = _matmul_tn(s["A"], dya, "wgrad_br_a").reshape(HEADS, HEAD_PAD, D)[:, :V_HEAD].reshape(HEADS * V_HEAD, D)
    g["w_br_b"] = _matmul_tn(s["Bm"], dyb, "wgrad_br_b")
    g["w_br_c"] = _matmul_tn(s["C"], dyc, "wgrad_br_c")
    g["w_br_d"] = _matmul_tn(s["Dv"], dyd, "wgrad_br_d")

    dPm, d_pool_bd, g_ps, g_gv, g["sgu_w"], d_bias, d_cw = _mixers_bwd(
        s["P"], D, S, dBm, dC, dDv, w["pool_bd"], w["pool_scale"], w["g_sgu_v"], w["sgu_w"], w["sgu_bias"], w["conv_w"])
    g["pool_w"] = jnp.stack([d_pool_bd[k * GROUP_DIM:(k + 1) * GROUP_DIM, k * GROUP_DIM:(k + 1) * GROUP_DIM] for k in range(GROUPS)])
    g["pool_scale"], g["g_sgu_v"] = g_ps, g_gv
    g["sgu_b"] = d_bias.reshape(SGU_BLOCK, GROUPS, GROUP_DIM).sum(-1).T
    g["conv_w"] = d_cw[:3].reshape(3, 1, MIX_W)

    delta = _attn_delta(s["A"], dA)
    dQ, dK, dV = _flash_bwd(s["Q"], s["K"], s["V"], dA, s["lse"], delta, S)
    dPa, dq, dkv, g["g_cq"], g["g_ckv"] = _mla_bwd_post(s["P"], D, S, dQ, dK, dV, w["g_cq"], w["g_ckv"], w["w_uq"], w["w_ukv"], *rope)
    g["w_uq"] = _unpad_heads(_matmul_tn(s["hq"], dq, "wgrad_uq"), QK_NOPE + QK_ROPE)
    d_ukv = _matmul_tn(s["hkv"], dkv, "wgrad_ukv")
    dk_w = _unpad_heads(d_ukv[:, :MLA_W], QK_NOPE).reshape(KV_LORA, HEADS, QK_NOPE)
    dv_w = _unpad_heads(d_ukv[:, MLA_W:], V_HEAD).reshape(KV_LORA, HEADS, V_HEAD)
    g["w_ukv"] = jnp.concatenate([dk_w, dv_w], axis=-1).reshape(KV_LORA, HEADS * (QK_NOPE + V_HEAD))

    dx, g["g_pre_mix"] = _proj_bwd(dx1, s["x"], w["g_pre_mix"], dPg, dPa, dPm, w["w_in_gates"], w["w_in_mla"], w["w_in_mix"])
    g["w_in"] = _unpad_w_in(_matmul_tn(s["h"], dPg, "wgrad_in_gates"), _matmul_tn(s["h"], dPa, "wgrad_in_mla"),
                            _matmul_tn(s["h"], dPm, "wgrad_in_mix"))
    for n in ("g_pre_mix", "g_cq", "g_ckv", "pool_scale", "g_sgu_v", "g_post_mix", "g_pre_ffn", "g_post_ffn"):
        g[n] = g[n].reshape(-1)
    return dx, g


def _local_step(x, target, full):
    n_seq, S, D = x.shape
    rope = _rope_tables(S)
    xs = x.reshape(n_seq * S, D)
    weights, saved = [], []
    for l in range(DEPTH):
        w = _layer_weights(full, l, D)
        xs, s = _layer_fwd(xs, w, S, rope)
        weights.append(w)
        saved.append(s)
    loss_parts, dx = _loss_grad(xs, target.reshape(n_seq * S, D))
    grads = [None] * DEPTH
    for l in reversed(range(DEPTH)):
        dx, grads[l] = _layer_bwd(dx, weights[l], saved[l], S, rope)
    stacked = {n: jnp.stack([grads[l][n] for l in range(DEPTH)]) for n in WEIGHTS}
    return loss_parts, dx.reshape(n_seq, S, D), stacked


def _small_rows(n):
    return -(-n // (8 * 128)) * 8


def _to_small(parts):
    flat = jnp.concatenate([p.reshape(-1) for p in parts])
    rows = _small_rows(flat.shape[0])
    return jnp.pad(flat, (0, rows * 128 - flat.shape[0])).reshape(rows, 128)


def kernel(x, w_in, g_pre_mix, g_cq, g_ckv, w_uq, w_ukv, pool_w, pool_scale, g_sgu_v, sgu_w, sgu_b, conv_w, w_br_a, w_br_b, w_br_c, w_br_d, w_out, g_post_mix, g_pre_ffn, w_ffn_gate, w_ffn_up, w_ffn_down, g_post_ffn, loss_target, m_w_in, m_g_pre_mix, m_g_cq, m_g_ckv, m_w_uq, m_w_ukv, m_pool_w, m_pool_scale, m_g_sgu_v, m_sgu_w, m_sgu_b, m_conv_w, m_w_br_a, m_w_br_b, m_w_br_c, m_w_br_d, m_w_out, m_g_post_mix, m_g_pre_ffn, m_w_ffn_gate, m_w_ffn_up, m_w_ffn_down, m_g_post_ffn, v_w_in, v_g_pre_mix, v_g_cq, v_g_ckv, v_w_uq, v_w_ukv, v_pool_w, v_pool_scale, v_g_sgu_v, v_sgu_w, v_sgu_b, v_conv_w, v_w_br_a, v_w_br_b, v_w_br_c, v_w_br_d, v_w_out, v_g_post_mix, v_g_pre_ffn, v_w_ffn_gate, v_w_ffn_up, v_w_ffn_down, v_g_post_ffn):
    local = dict(locals())
    W = {n: local[n] for n in WEIGHTS}
    M = {n: local["m_" + n] for n in WEIGHTS}
    V = {n: local["v_" + n] for n in WEIGHTS}
    chip = 2 * lax.axis_index("x") + lax.axis_index("y")
    core = lax.axis_index("c")

    shard_shapes = [W[n].shape for n in GATHERED]
    gathered = _gather_weights(_pack([W[n].astype(BF16) for n in GATHERED]))
    per_chip = [_unpack(gathered[k], shard_shapes) for k in range(4)]
    full = {n: _join_shards(n, [per_chip[k][i] for k in range(4)]) for i, n in enumerate(GATHERED)}
    conv_shape = conv_w.shape
    conv_full_shape = conv_shape[:-1] + (4 * conv_shape[-1],)
    placed = lax.dynamic_update_slice(jnp.zeros(conv_full_shape, F32), conv_w, (0, 0, 0, chip * conv_shape[-1]))
    n_conv = int(np.prod(conv_full_shape))
    conv_sum = _all_reduce_small(_to_small([placed]), "gather_conv_w")
    full["conv_w"] = 0.5 * conv_sum.reshape(-1)[:n_conv].reshape(conv_full_shape)
    for n in REPLICATED:
        full[n] = W[n]

    loss_parts, grad_x, grads = _local_step(x, loss_target, full)
    loss = lax.psum(jnp.sum(loss_parts), ("x", "y", "c"))

    small_shapes = [W[n].shape for n in REPLICATED]
    small_sum = _all_reduce_small(_to_small([grads[n] for n in REPLICATED]), "reduce_small_grads")
    small_grads = dict(zip(REPLICATED, _unpack(small_sum, small_shapes)))

    parts = {n: _split_shards(n, grads[n]) for n in SHARDED}
    G = jnp.stack([_pack([parts[n][k] for n in SHARDED]) for k in range(4)])
    H = _add_halves(G, _exchange_halves(G), core)
    reduced = _join_halves(_sum_slots(_scatter_partials(H)))
    shard_grads = dict(zip(SHARDED, _unpack(reduced, [W[n].shape for n in SHARDED])))

    out_g, out_d, out_m, out_v = {}, {}, {}, {}
    for n in SHARDED:
        if n == "conv_w":
            continue
        shp = W[n].shape
        two_d = lambda a: a.reshape(-1, shp[-1])
        d, m2, v2 = _adamw(two_d(W[n]), two_d(shard_grads[n]), two_d(M[n]), two_d(V[n]), "adamw_" + n)
        out_g[n], out_d[n], out_m[n], out_v[n] = shard_grads[n], d.reshape(shp), m2.reshape(shp), v2.reshape(shp)
    rest = REPLICATED + ("conv_w",)
    rest_grads = {**small_grads, "conv_w": shard_grads["conv_w"]}
    rest_shapes = [W[n].shape for n in rest]
    d, m2, v2 = _adamw(_to_small([W[n] for n in rest]), _to_small([rest_grads[n] for n in rest]),
                       _to_small([M[n] for n in rest]), _to_small([V[n] for n in rest]), "adamw_small")
    for n, dd, mm, vv in zip(rest, _unpack(d, rest_shapes), _unpack(m2, rest_shapes), _unpack(v2, rest_shapes)):
        out_g[n], out_d[n], out_m[n], out_v[n] = rest_grads[n], dd, mm, vv

    return (loss, grad_x, *[out_g[n] for n in WEIGHTS], *[out_d[n] for n in WEIGHTS], *[out_m[n] for n in WEIGHTS],
            *[out_v[n] for n in WEIGHTS])
```

```python
import functools

import numpy as np
import jax
import jax.numpy as jnp
from jax import lax
from jax.experimental import pallas as pl
from jax.experimental.pallas import tpu as pltpu

F32 = jnp.float32
BF16 = jnp.bfloat16

EPS = 1e-6
NEG_INF = -1e30
DEPTH = 4
HEADS = 8
QK_NOPE = 64
QK_ROPE = 32
V_HEAD = 64
HEAD_PAD = 128
Q_LORA = 256
KV_LORA = 128
ROPE_THETA = 10000.0
POOL_WINDOWS = (2, 4, 8, 16)
GROUPS = 4
GROUP_DIM = 64
MIX_W = GROUPS * GROUP_DIM
POOL_HALO = 16
CONV_HALO = 8
SGU_BLOCK = 128
CHUNK = 64
CHUNK_SHIFT = 6
GROUP_SHIFT = 6
N_BRANCH = 4
MLA_W = HEADS * HEAD_PAD
N_CHIPS = 4
CHIP_HEADS_W = MLA_W // N_CHIPS
CHIP_KV = 2 * CHIP_HEADS_W
ATTN_SCALE = (QK_NOPE + QK_ROPE) ** -0.5
SEC_MLA = Q_LORA + KV_LORA + HEAD_PAD
SEC_MIX = 6 * MIX_W

ADAM_LR = 0.001
ADAM_B1 = 0.9
ADAM_B2 = 0.999
ADAM_EPS = 1e-08
ADAM_WD = 0.01
ADAM_STEP = 10

VMEM_LIMIT = 56 * 1024 * 1024
MESH = pl.DeviceIdType.MESH


def _tile(n, pref, mult=8):
    t = min(n, pref)
    while t > 0:
        if n % t == 0 and t % mult == 0:
            return t
        t -= 1
    return n


def _params(n_axes):
    return pltpu.CompilerParams(dimension_semantics=("arbitrary",) * n_axes, vmem_limit_bytes=VMEM_LIMIT)


def _dot(a, b):
    return jnp.dot(a, b, preferred_element_type=F32)


def _dot_nt(a, b):
    return lax.dot_general(a, b, (((1,), (1,)), ((), ())), preferred_element_type=F32)


def _dot_tn(a, b):
    return lax.dot_general(a, b, (((0,), (0,)), ((), ())), preferred_element_type=F32)


def _rms_r(x):
    return lax.rsqrt(jnp.mean(x * x, axis=-1, keepdims=True) + EPS)


def _rms_bwd(dy, x, g):
    r = _rms_r(x)
    u = dy * g
    dx = r * u - x * (r * r * r * jnp.mean(u * x, axis=-1, keepdims=True))
    dg = jnp.sum(dy * x * r, axis=0, keepdims=True)
    return dx, dg


def _sigmoid(x):
    return 1.0 / (1.0 + jnp.exp(-x))


def _shift_down(a, k):
    return pltpu.roll(a, k, 0)


def _shift_up(a, k):
    return pltpu.roll(a, a.shape[0] - k, 0)


def _rope(x, c, sa, sb):
    w = x.shape[-1]
    return x * c + pltpu.roll(x, QK_ROPE // 2, 1) * sa + pltpu.roll(x, w - QK_ROPE // 2, 1) * sb


def _rope_t(d, c, sa, sb):
    w = d.shape[-1]
    return d * c + pltpu.roll(d * sa, w - QK_ROPE // 2, 1) + pltpu.roll(d * sb, QK_ROPE // 2, 1)


def _full(shape):
    return pl.BlockSpec(shape, lambda *_: (0,) * len(shape))


def _norm_matmul(x, g, w, name):
    T, K = x.shape
    N = w.shape[1]
    tm, tn = _tile(T, 512), _tile(N, 1536, 128)

    def body(x_ref, g_ref, w_ref, o_ref, h_ref):
        @pl.when(pl.program_id(1) == 0)
        def _():
            xv = x_ref[...]
            h_ref[...] = (xv * _rms_r(xv) * g_ref[...]).astype(BF16)

        o_ref[...] = _dot(h_ref[...], w_ref[...])

    return pl.pallas_call(
        body, name=name, grid=(T // tm, N // tn),
        in_specs=[pl.BlockSpec((tm, K), lambda i, j: (i, 0)), _full((1, K)), pl.BlockSpec((K, tn), lambda i, j: (0, j))],
        out_specs=[pl.BlockSpec((tm, tn), lambda i, j: (i, j)), pl.BlockSpec((tm, K), lambda i, j: (i, 0))],
        out_shape=[jax.ShapeDtypeStruct((T, N), F32), jax.ShapeDtypeStruct((T, K), BF16)],
        compiler_params=_params(2),
    )(x, g, w)


def _mla_prep(P, D, g_cq, g_ckv, wuq, wukv, rope_c, rope_sa, rope_sb, S):
    T = P.shape[0]
    tm = _tile(S, 512)
    n_si = S // tm
    base = 4 * D

    def body(cq_ref, ckv_ref, kr_ref, gq_ref, gkv_ref, wq_ref, wkv_ref, c_ref, sa_ref, sb_ref,
             q_ref, k_ref, v_ref, hq_ref, hkv_ref):
        c, sa, sb = c_ref[...], sa_ref[...], sb_ref[...]
        cq = cq_ref[...]
        hq = (cq * _rms_r(cq) * gq_ref[...]).astype(BF16)
        hq_ref[...] = hq
        q = _dot(hq, wq_ref[...])
        q_ref[...] = _rope(q, jnp.tile(c, (1, HEADS)), jnp.tile(sa, (1, HEADS)), jnp.tile(sb, (1, HEADS))).astype(BF16)
        ckv = ckv_ref[...]
        hkv = (ckv * _rms_r(ckv) * gkv_ref[...]).astype(BF16)
        hkv_ref[...] = hkv
        kv = _dot(hkv, wkv_ref[...])
        kr = _rope(kr_ref[...], c, sa, sb)
        k_nope = jnp.concatenate([kv[:, j * CHIP_KV:j * CHIP_KV + CHIP_HEADS_W] for j in range(N_CHIPS)], axis=1)
        k_ref[...] = (k_nope + jnp.tile(kr, (1, HEADS))).astype(BF16)
        v_ref[...] = jnp.concatenate([kv[:, j * CHIP_KV + CHIP_HEADS_W:(j + 1) * CHIP_KV] for j in range(N_CHIPS)],
                                     axis=1).astype(BF16)

    tab = pl.BlockSpec((tm, HEAD_PAD), lambda i: (i % n_si, 0))
    row = lambda w: pl.BlockSpec((tm, w), lambda i: (i, 0))
    return pl.pallas_call(
        body, name="mla_prep", grid=(T // tm,),
        in_specs=[pl.BlockSpec((tm, Q_LORA), lambda i: (i, base // Q_LORA)),
                  pl.BlockSpec((tm, KV_LORA), lambda i: (i, (base + Q_LORA) // KV_LORA)),
                  pl.BlockSpec((tm, HEAD_PAD), lambda i: (i, (base + Q_LORA + KV_LORA) // HEAD_PAD)),
                  _full((1, Q_LORA)), _full((1, KV_LORA)), _full((Q_LORA, MLA_W)), _full((KV_LORA, 2 * MLA_W)),
                  tab, tab, tab],
        out_specs=[row(MLA_W), row(MLA_W), row(MLA_W), row(Q_LORA), row(KV_LORA)],
        out_shape=[jax.ShapeDtypeStruct((T, MLA_W), BF16)] * 3
        + [jax.ShapeDtypeStruct((T, Q_LORA), BF16), jax.ShapeDtypeStruct((T, KV_LORA), BF16)],
        compiler_params=_params(1),
    )(P, P, P, g_cq, g_ckv, wuq, wukv, rope_c, rope_sa, rope_sb)


def _chunk_mask(tq, tk):
    row = lax.broadcasted_iota(jnp.int32, (tq, tk), 0)
    col = lax.broadcasted_iota(jnp.int32, (tq, tk), 1)
    return (row >> CHUNK_SHIFT) >= (col >> CHUNK_SHIFT)


def _flash_fwd(Q, K, V, S):
    T = Q.shape[0]
    n_seq = T // S
    tq = _tile(S, 256, 128)
    nq = S // tq

    def body(q_ref, k_ref, v_ref, o_ref, lse_ref):
        qi = pl.program_id(1)
        mask = _chunk_mask(tq, tq)
        lane = lax.broadcasted_iota(jnp.int32, (tq, HEAD_PAD), 1)
        lse_all = jnp.zeros((tq, HEAD_PAD), F32)
        for h in range(HEADS):
            hs = slice(h * HEAD_PAD, (h + 1) * HEAD_PAD)
            q = q_ref[:, hs]

            def update(kb, carry, masked):
                m, l, acc = carry
                rows = pl.ds(pl.multiple_of(kb * tq, tq), tq)
                s = _dot_nt(q, k_ref[rows, hs]) * ATTN_SCALE
                if masked:
                    s = jnp.where(mask, s, NEG_INF)
                m_new = jnp.maximum(m, jnp.max(s, axis=-1, keepdims=True))
                alpha = jnp.exp(m - m_new)
                p = jnp.exp(s - m_new)
                l = alpha * l + jnp.sum(p, axis=-1, keepdims=True)
                acc = alpha * acc + _dot(p.astype(BF16), v_ref[rows, hs])
                return m_new, l, acc

            init = (jnp.full((tq, 1), NEG_INF, F32), jnp.zeros((tq, 1), F32), jnp.zeros((tq, HEAD_PAD), F32))
            carry = lax.fori_loop(0, qi, lambda kb, c: update(kb, c, False), init)
            m, l, acc = update(qi, carry, True)
            o_ref[:, hs] = (acc / l).astype(BF16)
            lse_all = jnp.where(lane == h, m + jnp.log(l), lse_all)
        lse_ref[...] = lse_all

    return pl.pallas_call(
        body, name="flash_fwd", grid=(n_seq, nq),
        in_specs=[pl.BlockSpec((tq, MLA_W), lambda b, i: (b * nq + i, 0)),
                  pl.BlockSpec((S, MLA_W), lambda b, i: (b, 0)), pl.BlockSpec((S, MLA_W), lambda b, i: (b, 0))],
        out_specs=[pl.BlockSpec((tq, MLA_W), lambda b, i: (b * nq + i, 0)),
                   pl.BlockSpec((tq, HEAD_PAD), lambda b, i: (b * nq + i, 0))],
        out_shape=[jax.ShapeDtypeStruct((T, MLA_W), BF16), jax.ShapeDtypeStruct((T, HEAD_PAD), F32)],
        compiler_params=_params(2),
    )(Q, K, V)


def _lane_group():
    return lax.broadcasted_iota(jnp.int32, (1, MIX_W), 1) >> GROUP_SHIFT


def _by_group(a0, a1, a2, a3):
    g = _lane_group()
    return jnp.where(g == 0, a0, jnp.where(g == 1, a1, jnp.where(g == 2, a2, a3)))


def _pool_count(si, tc, rows):
    pos = si * tc + lax.broadcasted_iota(jnp.int32, (rows, MIX_W), 0)
    win = _by_group(*POOL_WINDOWS)
    return jnp.minimum(pos + 1, win).astype(F32)


def _pool_fwd(z, z_prev, si, tc):
    ze = jnp.concatenate([z_prev, z], axis=0)
    s1 = ze + _shift_down(ze, 1)
    s2 = s1 + _shift_down(s1, 2)
    s4 = s2 + _shift_down(s2, 4)
    s8 = s4 + _shift_down(s4, 8)
    win_sum = _by_group(s1, s2, s4, s8)[POOL_HALO:]
    return win_sum / _pool_count(si, tc, tc) - z


def _sgu_weights(w_ref):
    row = lax.broadcasted_iota(jnp.int32, (SGU_BLOCK, SGU_BLOCK), 0)
    col = lax.broadcasted_iota(jnp.int32, (SGU_BLOCK, SGU_BLOCK), 1)
    keep = (row >> CHUNK_SHIFT) >= (col >> CHUNK_SHIFT)
    return keep, [jnp.where(keep, w_ref[g], 0.0).astype(BF16) for g in range(GROUPS)]


def _sgu_mix(vn_blk, wm, bias):
    g = _lane_group()
    mixed = bias
    for k in range(GROUPS):
        mixed = mixed + jnp.where(g == k, _dot(wm[k], vn_blk), 0.0)
    return mixed


def _conv_fwd(z, z_prev, w_ref):
    ze = jnp.concatenate([z_prev, z], axis=0)
    y = w_ref[0:1, :] * _shift_down(ze, 2) + w_ref[1:2, :] * _shift_down(ze, 1) + w_ref[2:3, :] * ze
    return y[CONV_HALO:]


def _mix_specs(T, D, tc):
    base = (4 * D + SEC_MLA) // MIX_W
    cur = lambda k: pl.BlockSpec((tc, MIX_W), lambda i: (i, base + k))
    prev = lambda k, halo: pl.BlockSpec((halo, MIX_W), lambda i: (jnp.maximum(i * (tc // halo) - 1, 0), base + k))
    nxt = lambda k, halo: pl.BlockSpec((halo, MIX_W), lambda i: (jnp.minimum((i + 1) * (tc // halo), T // halo - 1), base + k))
    return cur, prev, nxt


def _mixers_fwd(P, D, S, pool_bd, pool_scale, g_v, sgu_w, sgu_bias, conv_w):
    T = P.shape[0]
    tc = _tile(S, 512, SGU_BLOCK)
    n_si = S // tc
    cur, prev, _ = _mix_specs(T, D, tc)

    def body(z_ref, zp_ref, u_ref, v_ref, b_ref, c_ref, x_ref, cp_ref, xp_ref,
             pw_ref, ps_ref, gv_ref, sw_ref, sb_ref, cw_ref, ob_ref, oc_ref, od_ref):
        si = pl.program_id(0) % n_si
        first = si == 0
        z = z_ref[...]
        pooled = _pool_fwd(z, jnp.where(first, 0.0, zp_ref[...]), si, tc)
        ob_ref[...] = (_dot(pooled.astype(BF16), pw_ref[...]) * ps_ref[...]).astype(BF16)

        v = v_ref[...]
        vn = (v * _rms_r(v) * gv_ref[...]).astype(BF16)
        _, wm = _sgu_weights(sw_ref)
        for blk in range(tc // SGU_BLOCK):
            rows = slice(blk * SGU_BLOCK, (blk + 1) * SGU_BLOCK)
            oc_ref[rows, :] = (u_ref[rows, :] * _sgu_mix(vn[rows], wm, sb_ref[...])).astype(BF16)

        zc = c_ref[...] * x_ref[...]
        zc_prev = jnp.where(first, 0.0, cp_ref[...] * xp_ref[...])
        od_ref[...] = (b_ref[...] * _conv_fwd(zc, zc_prev, cw_ref)).astype(BF16)

    out = pl.BlockSpec((tc, MIX_W), lambda i: (i, 0))
    return pl.pallas_call(
        body, name="mixers_fwd", grid=(T // tc,),
        in_specs=[cur(0), prev(0, POOL_HALO), cur(1), cur(2), cur(3), cur(4), cur(5), prev(4, CONV_HALO), prev(5, CONV_HALO),
                  _full((MIX_W, MIX_W)), _full((1, MIX_W)), _full((1, MIX_W)), _full((GROUPS, SGU_BLOCK, SGU_BLOCK)),
                  _full((SGU_BLOCK, MIX_W)), _full((CONV_HALO, MIX_W))],
        out_specs=[out, out, out],
        out_shape=[jax.ShapeDtypeStruct((T, MIX_W), BF16)] * 3,
        compiler_params=_params(1),
    )(P, P, P, P, P, P, P, P, P, pool_bd, pool_scale, g_v, sgu_w, sgu_bias, conv_w)


def _merge_fwd(x, P, A, Bm, C, Dv, wa, wb, wc, wd, wout, g_post):
    T, D = x.shape
    tm = _tile(T, 256)

    def body(x_ref, lg_ref, a_ref, b_ref, c_ref, d_ref, wa_ref, wb_ref, wc_ref, wd_ref, wo_ref, g_ref,
             x1_ref, mg_ref, o_ref):
        merged = jnp.zeros((tm, D), F32)
        for k, (br, w) in enumerate(((a_ref, wa_ref), (b_ref, wb_ref), (c_ref, wc_ref), (d_ref, wd_ref))):
            merged = merged + _sigmoid(lg_ref[:, k * D:(k + 1) * D]) * _dot(br[...], w[...])
        mg = merged.astype(BF16)
        mg_ref[...] = mg
        o = _dot(mg, wo_ref[...])
        o_ref[...] = o
        x1_ref[...] = x_ref[...] + o * _rms_r(o) * g_ref[...]

    row = lambda w: pl.BlockSpec((tm, w), lambda i: (i, 0))
    return pl.pallas_call(
        body, name="merge_fwd", grid=(T // tm,),
        in_specs=[row(D), row(4 * D), row(MLA_W), row(MIX_W), row(MIX_W), row(MIX_W),
                  _full((MLA_W, D)), _full((MIX_W, D)), _full((MIX_W, D)), _full((MIX_W, D)), _full((D, D)), _full((1, D))],
        out_specs=[row(D), row(D), row(D)],
        out_shape=[jax.ShapeDtypeStruct((T, D), F32), jax.ShapeDtypeStruct((T, D), BF16), jax.ShapeDtypeStruct((T, D), F32)],
        compiler_params=_params(1),
    )(x, P, A, Bm, C, Dv, wa, wb, wc, wd, wout, g_post)


def _ffn_specs(T, D, Fc, l):
    tm = _tile(T, 512)
    row = pl.BlockSpec((tm, D), lambda i, j: (i, 0))
    col = pl.BlockSpec((None, tm, Fc), lambda i, j: (j, i, 0))
    w_in = pl.BlockSpec((None, None, D, Fc), lambda i, j: (l, j, 0, 0))
    w_out = pl.BlockSpec((None, None, Fc, D), lambda i, j: (l, j, 0, 0))
    return tm, row, col, w_in, w_out


def _ffn_fwd(x1, g_pre, wg, wu, wdn, g_post, l):
    T, D = x1.shape
    nf, Fc = wg.shape[1], wg.shape[3]
    tm, row, col, w_in, w_out = _ffn_specs(T, D, Fc, l)

    def body(x_ref, gp_ref, wg_ref, wu_ref, wd_ref, gq_ref, x2_ref, h_ref, gt_ref, up_ref, f_ref):
        j = pl.program_id(1)

        @pl.when(j == 0)
        def _():
            xv = x_ref[...]
            h_ref[...] = (xv * _rms_r(xv) * gp_ref[...]).astype(BF16)
            f_ref[...] = jnp.zeros_like(f_ref)

        h = h_ref[...]
        gt = _dot(h, wg_ref[...])
        up = _dot(h, wu_ref[...])
        gt_ref[...] = gt.astype(BF16)
        up_ref[...] = up.astype(BF16)
        a = (gt * _sigmoid(gt) * up).astype(BF16)
        f_ref[...] += _dot(a, wd_ref[...])

        @pl.when(j == nf - 1)
        def _():
            f = f_ref[...]
            x2_ref[...] = x_ref[...] + f * _rms_r(f) * gq_ref[...]

    return pl.pallas_call(
        body, name="ffn_fwd", grid=(T // tm, nf),
        in_specs=[row, _full((1, D)), w_in, w_in, w_out, _full((1, D))],
        out_specs=[row, row, col, col, row],
        out_shape=[jax.ShapeDtypeStruct((T, D), F32), jax.ShapeDtypeStruct((T, D), BF16),
                   jax.ShapeDtypeStruct((nf, T, Fc), BF16), jax.ShapeDtypeStruct((nf, T, Fc), BF16),
                   jax.ShapeDtypeStruct((T, D), F32)],
        compiler_params=_params(2),
    )(x1, g_pre, wg, wu, wdn, g_post)


def _loss_grad(y, target):
    T, D = y.shape
    tm = _tile(T, 512)

    def body(y_ref, t_ref, l_ref, dy_ref):
        @pl.when(pl.program_id(0) == 0)
        def _():
            l_ref[...] = jnp.zeros_like(l_ref)

        d = y_ref[...] - t_ref[...]
        dy_ref[...] = d * (1.0 / D)
        e = jnp.sum((d * d).reshape(tm // 8, 8, D), axis=0)
        part = e[:, 0:128]
        for k in range(1, D // 128):
            part = part + e[:, k * 128:(k + 1) * 128]
        l_ref[...] += part * (0.5 / D)

    row = pl.BlockSpec((tm, D), lambda i: (i, 0))
    return pl.pallas_call(
        body, name="loss_grad", grid=(T // tm,),
        in_specs=[row, row], out_specs=[_full((8, 128)), row],
        out_shape=[jax.ShapeDtypeStruct((8, 128), F32), jax.ShapeDtypeStruct((T, D), F32)],
        compiler_params=_params(1),
    )(y, target)


def _matmul_tn(a, b, name):
    T, M = a.shape
    N = b.shape[1]
    tm, tn, tk = _tile(M, 1024, 128), _tile(N, 1536, 128), _tile(T, 512)

    def body(a_ref, b_ref, o_ref):
        @pl.when(pl.program_id(2) == 0)
        def _():
            o_ref[...] = jnp.zeros_like(o_ref)

        o_ref[...] += _dot_tn(a_ref[...], b_ref[...])

    return pl.pallas_call(
        body, name=name, grid=(M // tm, N // tn, T // tk),
        in_specs=[pl.BlockSpec((tk, tm), lambda i, j, k: (k, i)), pl.BlockSpec((tk, tn), lambda i, j, k: (k, j))],
        out_specs=pl.BlockSpec((tm, tn), lambda i, j, k: (i, j)),
        out_shape=jax.ShapeDtypeStruct((M, N), F32),
        compiler_params=_params(3),
    )(a, b)


def _wgrad_chip(buf, l, n_layers, a, b, rows, cols, a_mode, b_mode, name):
    T = a.shape[-2]
    tk = _tile(T, 1024)

    def spec(mode, width):
        if mode == "all":
            return pl.BlockSpec((tk, width), lambda k, t: (t, 0))
        if mode == "cols":
            return pl.BlockSpec((tk, width), lambda k, t: (t, k))
        return pl.BlockSpec((None, tk, width), lambda k, t: (k, t, 0))

    def body(a_ref, b_ref, *rest):
        o_ref = rest[-1]

        @pl.when(pl.program_id(1) == 0)
        def _():
            o_ref[...] = jnp.zeros_like(o_ref)

        o_ref[...] += _dot_tn(a_ref[...], b_ref[...])

    keep = [] if buf is None else [buf]
    return pl.pallas_call(
        body, name=name, grid=(N_CHIPS, T // tk),
        in_specs=[spec(a_mode, rows), spec(b_mode, cols)] + [pl.BlockSpec(memory_space=pl.ANY)] * len(keep),
        out_specs=pl.BlockSpec((None, None, rows, cols), lambda k, t: (k, l, 0, 0)),
        out_shape=jax.ShapeDtypeStruct((N_CHIPS, n_layers, rows, cols), F32),
        input_output_aliases={2: 0} if keep else {},
        compiler_params=_params(2),
    )(a, b, *keep)


def _ffn_bwd(dx2, x1, f, gt, up, g_pre, wg, wu, wdn, g_post, l):
    T, D = x1.shape
    nf, Fc = wg.shape[1], wg.shape[3]
    tm, row, col, w_in, w_out = _ffn_specs(T, D, Fc, l)

    def body(dx2_ref, x1_ref, f_ref, gt_ref, up_ref, gp_ref, wg_ref, wu_ref, wd_ref, gq_ref,
             dx1_ref, df_ref, dgt_ref, dup_ref, a_ref, dgp_ref, dgq_ref, dh_acc):
        i, j = pl.program_id(0), pl.program_id(1)

        @pl.when((i == 0) & (j == 0))
        def _():
            dgp_ref[...] = jnp.zeros_like(dgp_ref)
            dgq_ref[...] = jnp.zeros_like(dgq_ref)

        @pl.when(j == 0)
        def _():
            df, dg = _rms_bwd(dx2_ref[...], f_ref[...], gq_ref[...])
            df_ref[...] = df.astype(BF16)
            dgq_ref[...] += dg
            dh_acc[...] = jnp.zeros_like(dh_acc)

        da = _dot_nt(df_ref[...], wd_ref[...])
        gt = gt_ref[...].astype(F32)
        up = up_ref[...].astype(F32)
        sig = _sigmoid(gt)
        silu = gt * sig
        dgt = (da * up * (sig * (1.0 + gt * (1.0 - sig)))).astype(BF16)
        dup = (da * silu).astype(BF16)
        dgt_ref[...] = dgt
        dup_ref[...] = dup
        a_ref[...] = (silu * up).astype(BF16)
        dh_acc[...] += _dot_nt(dgt, wg_ref[...]) + _dot_nt(dup, wu_ref[...])

        @pl.when(j == nf - 1)
        def _():
            dx, dg = _rms_bwd(dh_acc[...], x1_ref[...], gp_ref[...])
            dx1_ref[...] = dx2_ref[...] + dx
            dgp_ref[...] += dg

    return pl.pallas_call(
        body, name="ffn_bwd", grid=(T // tm, nf),
        in_specs=[row, row, row, col, col, _full((1, D)), w_in, w_in, w_out, _full((1, D))],
        out_specs=[row, row, col, col, col, _full((1, D)), _full((1, D))],
        out_shape=[jax.ShapeDtypeStruct((T, D), F32), jax.ShapeDtypeStruct((T, D), BF16)]
        + [jax.ShapeDtypeStruct((nf, T, Fc), BF16)] * 3
        + [jax.ShapeDtypeStruct((1, D), F32), jax.ShapeDtypeStruct((1, D), F32)],
        scratch_shapes=[pltpu.VMEM((tm, D), F32)],
        compiler_params=_params(2),
    )(dx2, x1, f, gt, up, g_pre, wg, wu, wdn, g_post)


def _merge_bwd(dx1, o, P, A, Bm, C, Dv, wa, wb, wc, wd, wout, g_post):
    T, D = o.shape
    tm = _tile(T, 256)

    def body(dx1_ref, o_ref, lg_ref, a_ref, b_ref, c_ref, d_ref, wa_ref, wb_ref, wc_ref, wd_ref, wo_ref, g_ref,
             do_ref, dlg_ref, dya_ref, dyb_ref, dyc_ref, dyd_ref, da_ref, db_ref, dc_ref, dd_ref, dg_ref):
        @pl.when(pl.program_id(0) == 0)
        def _():
            dg_ref[...] = jnp.zeros_like(dg_ref)

        d_o, dg = _rms_bwd(dx1_ref[...], o_ref[...], g_ref[...])
        dg_ref[...] += dg
        d_o = d_o.astype(BF16)
        do_ref[...] = d_o
        dm = _dot_nt(d_o, wo_ref[...])
        branches = ((a_ref, wa_ref, dya_ref, da_ref), (b_ref, wb_ref, dyb_ref, db_ref),
                    (c_ref, wc_ref, dyc_ref, dc_ref), (d_ref, wd_ref, dyd_ref, dd_ref))
        for k, (br, w, dy_ref, dbr_ref) in enumerate(branches):
            gate = _sigmoid(lg_ref[:, k * D:(k + 1) * D])
            y = _dot(br[...], w[...])
            dlg_ref[:, k * D:(k + 1) * D] = (dm * y * gate * (1.0 - gate)).astype(BF16)
            dy = (dm * gate).astype(BF16)
            dy_ref[...] = dy
            dbr_ref[...] = _dot_nt(dy, w[...]).astype(dbr_ref.dtype)

    row = lambda w: pl.BlockSpec((tm, w), lambda i: (i, 0))
    return pl.pallas_call(
        body, name="merge_bwd", grid=(T // tm,),
        in_specs=[row(D), row(D), row(4 * D), row(MLA_W), row(MIX_W), row(MIX_W), row(MIX_W),
                  _full((MLA_W, D)), _full((MIX_W, D)), _full((MIX_W, D)), _full((MIX_W, D)), _full((D, D)), _full((1, D))],
        out_specs=[row(D), row(4 * D), row(D), row(D), row(D), row(D), row(MLA_W), row(MIX_W), row(MIX_W), row(MIX_W),
                   _full((1, D))],
        out_shape=[jax.ShapeDtypeStruct((T, D), BF16), jax.ShapeDtypeStruct((T, 4 * D), BF16)]
        + [jax.ShapeDtypeStruct((T, D), BF16)] * 4
        + [jax.ShapeDtypeStruct((T, MLA_W), BF16)] + [jax.ShapeDtypeStruct((T, MIX_W), F32)] * 3
        + [jax.ShapeDtypeStruct((1, D), F32)],
        compiler_params=_params(1),
    )(dx1, o, P, A, Bm, C, Dv, wa, wb, wc, wd, wout, g_post)


def _mixers_bwd(P, D, S, dBm, dC, dDv, pool_bd, pool_scale, g_v, sgu_w, sgu_bias, conv_w):
    T = P.shape[0]
    tc = _tile(S, 512, SGU_BLOCK)
    n_si = S // tc
    cur, prev, nxt = _mix_specs(T, D, tc)
    n_blk = tc // SGU_BLOCK

    def body(z_ref, zp_ref, u_ref, v_ref, b_ref, c_ref, x_ref, cp_ref, xp_ref, bn_ref,
             dbm_ref, dbmn_ref, dc_ref, ddv_ref, ddvn_ref,
             pw_ref, ps_ref, gv_ref, sw_ref, sb_ref, cw_ref,
             dp_ref, dpw_ref, dps_ref, dgv_ref, dsw_ref, dsb_ref, dcw_ref, dvn_acc):
        si = pl.program_id(0) % n_si
        first, last = si == 0, si == n_si - 1

        @pl.when(pl.program_id(0) == 0)
        def _():
            for r in (dpw_ref, dps_ref, dgv_ref, dsw_ref, dsb_ref, dcw_ref):
                r[...] = jnp.zeros_like(r)

        z = z_ref[...]
        pooled = _pool_fwd(z, jnp.where(first, 0.0, zp_ref[...]), si, tc).astype(BF16)
        dbm = dbm_ref[...]
        dps_ref[...] += jnp.sum(dbm * _dot(pooled, pw_ref[...]), axis=0, keepdims=True)
        dmix = (jnp.concatenate([dbm, jnp.where(last, 0.0, dbmn_ref[...])], axis=0) * ps_ref[...]).astype(BF16)
        dpw_ref[...] += _dot_tn(pooled, dmix[:tc])
        dpool = _dot_nt(dmix, pw_ref[...])
        e = dpool / _pool_count(si, tc, tc + POOL_HALO)
        f1 = e + _shift_up(e, 1)
        f2 = f1 + _shift_up(f1, 2)
        f4 = f2 + _shift_up(f2, 4)
        f8 = f4 + _shift_up(f4, 8)
        dp_ref[:, 0:MIX_W] = (_by_group(f1, f2, f4, f8)[:tc] - dpool[:tc]).astype(BF16)

        v = v_ref[...]
        vn = (v * _rms_r(v) * gv_ref[...]).astype(BF16)
        keep, wm = _sgu_weights(sw_ref)
        g = _lane_group()
        for blk in range(n_blk):
            rows = slice(blk * SGU_BLOCK, (blk + 1) * SGU_BLOCK)
            vb = vn[rows]
            dc = dc_ref[rows, :]
            dp_ref[rows, MIX_W:2 * MIX_W] = (dc * _sgu_mix(vb, wm, sb_ref[...])).astype(BF16)
            dmx = dc * u_ref[rows, :]
            dsb_ref[...] += dmx
            dvn = jnp.zeros((SGU_BLOCK, MIX_W), F32)
            for k in range(GROUPS):
                dmk = jnp.where(g == k, dmx, 0.0).astype(BF16)
                dsw_ref[k] += jnp.where(keep, _dot_nt(dmk, vb), 0.0)
                dvn = dvn + _dot_tn(wm[k], dmk)
            dvn_acc[rows, :] = dvn
        dv, dg = _rms_bwd(dvn_acc[...], v, gv_ref[...])
        dgv_ref[...] += dg
        dp_ref[:, 2 * MIX_W:3 * MIX_W] = dv.astype(BF16)

        cg, xg, bg = c_ref[...], x_ref[...], b_ref[...]
        zc = cg * xg
        ze = jnp.concatenate([jnp.where(first, 0.0, cp_ref[...] * xp_ref[...]), zc], axis=0)
        z1, z2 = _shift_down(ze, 1)[CONV_HALO:], _shift_down(ze, 2)[CONV_HALO:]
        ddv = ddv_ref[...]
        y = cw_ref[0:1, :] * z2 + cw_ref[1:2, :] * z1 + cw_ref[2:3, :] * zc
        dp_ref[:, 3 * MIX_W:4 * MIX_W] = (ddv * y).astype(BF16)
        dy = ddv * bg
        dcw_ref[0:1, :] += jnp.sum(dy * z2, axis=0, keepdims=True)
        dcw_ref[1:2, :] += jnp.sum(dy * z1, axis=0, keepdims=True)
        dcw_ref[2:3, :] += jnp.sum(dy * zc, axis=0, keepdims=True)
        dye = jnp.concatenate([dy, jnp.where(last, 0.0, ddvn_ref[...] * bn_ref[...])], axis=0)
        dz = (cw_ref[2:3, :] * dye + cw_ref[1:2, :] * _shift_up(dye, 1) + cw_ref[0:1, :] * _shift_up(dye, 2))[:tc]
        dp_ref[:, 4 * MIX_W:5 * MIX_W] = (dz * xg).astype(BF16)
        dp_ref[:, 5 * MIX_W:6 * MIX_W] = (dz * cg).astype(BF16)

    grad = lambda halo: pl.BlockSpec((halo, MIX_W), lambda i: (jnp.minimum((i + 1) * (tc // halo), T // halo - 1), 0))
    out = pl.BlockSpec((tc, MIX_W), lambda i: (i, 0))
    return pl.pallas_call(
        body, name="mixers_bwd", grid=(T // tc,),
        in_specs=[cur(0), prev(0, POOL_HALO), cur(1), cur(2), cur(3), cur(4), cur(5), prev(4, CONV_HALO), prev(5, CONV_HALO),
                  nxt(3, CONV_HALO), out, grad(POOL_HALO), out, out, grad(CONV_HALO),
                  _full((MIX_W, MIX_W)), _full((1, MIX_W)), _full((1, MIX_W)), _full((GROUPS, SGU_BLOCK, SGU_BLOCK)),
                  _full((SGU_BLOCK, MIX_W)), _full((CONV_HALO, MIX_W))],
        out_specs=[pl.BlockSpec((tc, SEC_MIX), lambda i: (i, 0)), _full((MIX_W, MIX_W)), _full((1, MIX_W)), _full((1, MIX_W)),
                   _full((GROUPS, SGU_BLOCK, SGU_BLOCK)), _full((SGU_BLOCK, MIX_W)), _full((CONV_HALO, MIX_W))],
        out_shape=[jax.ShapeDtypeStruct((T, SEC_MIX), BF16), jax.ShapeDtypeStruct((MIX_W, MIX_W), F32),
                   jax.ShapeDtypeStruct((1, MIX_W), F32), jax.ShapeDtypeStruct((1, MIX_W), F32),
                   jax.ShapeDtypeStruct((GROUPS, SGU_BLOCK, SGU_BLOCK), F32), jax.ShapeDtypeStruct((SGU_BLOCK, MIX_W), F32),
                   jax.ShapeDtypeStruct((CONV_HALO, MIX_W), F32)],
        scratch_shapes=[pltpu.VMEM((tc, MIX_W), F32)],
        compiler_params=_params(1),
    )(P, P, P, P, P, P, P, P, P, P, dBm, dBm, dC, dDv, dDv, pool_bd, pool_scale, g_v, sgu_w, sgu_bias, conv_w)


def _attn_delta(O, dO):
    T = O.shape[0]
    tm = _tile(T, 512)

    def body(o_ref, do_ref, d_ref):
        lane = lax.broadcasted_iota(jnp.int32, (tm, HEAD_PAD), 1)
        out = jnp.zeros((tm, HEAD_PAD), F32)
        for h in range(HEADS):
            hs = slice(h * HEAD_PAD, (h + 1) * HEAD_PAD)
            s = jnp.sum(o_ref[:, hs].astype(F32) * do_ref[:, hs].astype(F32), axis=-1, keepdims=True)
            out = jnp.where(lane == h, s, out)
        d_ref[...] = out

    row = lambda w: pl.BlockSpec((tm, w), lambda i: (i, 0))
    return pl.pallas_call(
        body, name="attn_delta", grid=(T // tm,), in_specs=[row(MLA_W), row(MLA_W)], out_specs=row(HEAD_PAD),
        out_shape=jax.ShapeDtypeStruct((T, HEAD_PAD), F32), compiler_params=_params(1),
    )(O, dO)


def _flash_bwd(Q, K, V, dO, lse, delta, S):
    T = Q.shape[0]
    n_seq = T // S
    tq = _tile(S, 256, 128)
    nq = S // tq

    def body(k_ref, v_ref, q_ref, do_ref, lse_ref, dl_ref, dq_ref, dk_ref, dv_ref):
        kb = pl.program_id(1)

        @pl.when(kb == 0)
        def _():
            dq_ref[...] = jnp.zeros_like(dq_ref)

        mask = _chunk_mask(tq, tq)
        lane = lax.broadcasted_iota(jnp.int32, (tq, HEAD_PAD), 1)
        for h in range(HEADS):
            hs = slice(h * HEAD_PAD, (h + 1) * HEAD_PAD)
            k = k_ref[:, hs]
            v = v_ref[:, hs]

            def update(qi, carry, masked):
                dk, dv = carry
                rows = pl.ds(pl.multiple_of(qi * tq, tq), tq)
                q = q_ref[rows, hs]
                do = do_ref[rows, hs]
                lse_h = jnp.sum(jnp.where(lane == h, lse_ref[rows, :], 0.0), axis=-1, keepdims=True)
                dl_h = jnp.sum(jnp.where(lane == h, dl_ref[rows, :], 0.0), axis=-1, keepdims=True)
                s = _dot_nt(q, k) * ATTN_SCALE
                if masked:
                    s = jnp.where(mask, s, NEG_INF)
                p = jnp.exp(s - lse_h)
                dv = dv + _dot_tn(p.astype(BF16), do)
                ds = (p * (_dot_nt(do, v) - dl_h) * ATTN_SCALE).astype(BF16)
                dk = dk + _dot_tn(ds, q)
                dq_ref[rows, hs] += _dot(ds, k)
                return dk, dv

            zero = jnp.zeros((tq, HEAD_PAD), F32)
            carry = update(kb, (zero, zero), True)
            dk, dv = lax.fori_loop(kb + 1, nq, lambda qi, c: update(qi, c, False), carry)
            dk_ref[:, hs] = dk
            dv_ref[:, hs] = dv

    tile = pl.BlockSpec((tq, MLA_W), lambda b, i: (b * nq + i, 0))
    seq = lambda w: pl.BlockSpec((S, w), lambda b, i: (b, 0))
    return pl.pallas_call(
        body, name="flash_bwd", grid=(n_seq, nq),
        in_specs=[tile, tile, seq(MLA_W), seq(MLA_W), seq(HEAD_PAD), seq(HEAD_PAD)],
        out_specs=[seq(MLA_W), tile, tile],
        out_shape=[jax.ShapeDtypeStruct((T, MLA_W), F32)] * 3,
        compiler_params=_params(2),
    )(K, V, Q, dO, lse, delta)


def _mla_bwd_post(P, D, S, dQ, dK, dV, g_cq, g_ckv, wuq, wukv, rope_c, rope_sa, rope_sb):
    T = P.shape[0]
    tm = _tile(S, 512)
    n_si = S // tm
    base = 4 * D

    def body(cq_ref, ckv_ref, dq_ref, dk_ref, dv_ref, gq_ref, gkv_ref, wq_ref, wkv_ref, c_ref, sa_ref, sb_ref,
             dp_ref, dqo_ref, dkvo_ref, dgq_ref, dgkv_ref):
        @pl.when(pl.program_id(0) == 0)
        def _():
            dgq_ref[...] = jnp.zeros_like(dgq_ref)
            dgkv_ref[...] = jnp.zeros_like(dgkv_ref)

        c, sa, sb = c_ref[...], sa_ref[...], sb_ref[...]
        dq = _rope_t(dq_ref[...], jnp.tile(c, (1, HEADS)), jnp.tile(sa, (1, HEADS)), jnp.tile(sb, (1, HEADS))).astype(BF16)
        dqo_ref[...] = dq
        dcq, dg = _rms_bwd(_dot_nt(dq, wq_ref[...]), cq_ref[...], gq_ref[...])
        dgq_ref[...] += dg
        dp_ref[:, 0:Q_LORA] = dcq.astype(BF16)

        dk = dk_ref[...]
        dkb, dvb = dk.astype(BF16), dv_ref[...].astype(BF16)
        dkv = jnp.concatenate([p[:, j * CHIP_HEADS_W:(j + 1) * CHIP_HEADS_W] for j in range(N_CHIPS) for p in (dkb, dvb)], axis=1)
        dkvo_ref[...] = dkv
        dckv, dg = _rms_bwd(_dot_nt(dkv, wkv_ref[...]), ckv_ref[...], gkv_ref[...])
        dgkv_ref[...] += dg
        dp_ref[:, Q_LORA:Q_LORA + KV_LORA] = dckv.astype(BF16)

        dkr = dk[:, 0:HEAD_PAD]
        for h in range(1, HEADS):
            dkr = dkr + dk[:, h * HEAD_PAD:(h + 1) * HEAD_PAD]
        lane = lax.broadcasted_iota(jnp.int32, (1, HEAD_PAD), 1)
        rope_lanes = (lane >= QK_NOPE) & (lane < QK_NOPE + QK_ROPE)
        dp_ref[:, Q_LORA + KV_LORA:SEC_MLA] = jnp.where(rope_lanes, _rope_t(dkr, c, sa, sb), 0.0).astype(BF16)

    tab = pl.BlockSpec((tm, HEAD_PAD), lambda i: (i % n_si, 0))
    row = lambda w: pl.BlockSpec((tm, w), lambda i: (i, 0))
    return pl.pallas_call(
        body, name="mla_bwd_post", grid=(T // tm,),
        in_specs=[pl.BlockSpec((tm, Q_LORA), lambda i: (i, base // Q_LORA)),
                  pl.BlockSpec((tm, KV_LORA), lambda i: (i, (base + Q_LORA) // KV_LORA)),
                  row(MLA_W), row(MLA_W), row(MLA_W),
                  _full((1, Q_LORA)), _full((1, KV_LORA)), _full((Q_LORA, MLA_W)), _full((KV_LORA, 2 * MLA_W)), tab, tab, tab],
        out_specs=[row(SEC_MLA), row(MLA_W), row(2 * MLA_W), _full((1, Q_LORA)), _full((1, KV_LORA))],
        out_shape=[jax.ShapeDtypeStruct((T, SEC_MLA), BF16), jax.ShapeDtypeStruct((T, MLA_W), BF16),
                   jax.ShapeDtypeStruct((T, 2 * MLA_W), BF16), jax.ShapeDtypeStruct((1, Q_LORA), F32),
                   jax.ShapeDtypeStruct((1, KV_LORA), F32)],
        compiler_params=_params(1),
    )(P, P, dQ, dK, dV, g_cq, g_ckv, wuq, wukv, rope_c, rope_sa, rope_sb)


def _proj_bwd(dx1, x, g, dPg, dPa, dPm, w_gates, w_mla, w_mix):
    T, D = x.shape
    tm = _tile(T, 256)

    def body(dx1_ref, x_ref, g_ref, dg_ref_in, da_ref, dm_ref, wg_ref, wa_ref, wm_ref, dx_ref, dg_ref):
        @pl.when(pl.program_id(0) == 0)
        def _():
            dg_ref[...] = jnp.zeros_like(dg_ref)

        dh = _dot_nt(dg_ref_in[...], wg_ref[...]) + _dot_nt(da_ref[...], wa_ref[...]) + _dot_nt(dm_ref[...], wm_ref[...])
        dx, dg = _rms_bwd(dh, x_ref[...], g_ref[...])
        dx_ref[...] = dx1_ref[...] + dx
        dg_ref[...] += dg

    row = lambda w: pl.BlockSpec((tm, w), lambda i: (i, 0))
    return pl.pallas_call(
        body, name="proj_bwd", grid=(T // tm,),
        in_specs=[row(D), row(D), _full((1, D)), row(4 * D), row(SEC_MLA), row(SEC_MIX),
                  _full((D, 4 * D)), _full((D, SEC_MLA)), _full((D, SEC_MIX))],
        out_specs=[row(D), _full((1, D))],
        out_shape=[jax.ShapeDtypeStruct((T, D), F32), jax.ShapeDtypeStruct((1, D), F32)],
        compiler_params=_params(1),
    )(dx1, x, g, dPg, dPa, dPm, w_gates, w_mla, w_mix)


def _adamw(w, g, m, v, name):
    R, C = w.shape
    tr = _tile(R, max(8, (1 << 19) // C))

    def body(w_ref, g_ref, m_ref, v_ref, d_ref, mo_ref, vo_ref):
        gv = g_ref[...]
        mn = ADAM_B1 * m_ref[...] + (1.0 - ADAM_B1) * gv
        vn = ADAM_B2 * v_ref[...] + (1.0 - ADAM_B2) * (gv * gv)
        mo_ref[...] = mn
        vo_ref[...] = vn
        m_hat = mn / (1.0 - ADAM_B1 ** ADAM_STEP)
        v_hat = vn / (1.0 - ADAM_B2 ** ADAM_STEP)
        d_ref[...] = -ADAM_LR * (m_hat / (jnp.sqrt(v_hat) + ADAM_EPS) + ADAM_WD * w_ref[...])

    blk = pl.BlockSpec((tr, C), lambda i: (i, 0))
    return pl.pallas_call(
        body, name=name, grid=(R // tr,), in_specs=[blk] * 4, out_specs=[blk] * 3,
        out_shape=[jax.ShapeDtypeStruct((R, C), F32)] * 3, compiler_params=_params(1),
    )(w, g, m, v)


def _rows_tile(rows, cols):
    return _tile(rows, max(16, (1 << 19) // cols), 16)


def _add_halves(G, recv, half, name):
    n, L, R, C = G.shape
    hr = R // 2
    tr = _rows_tile(hr, C)
    nb = hr // tr

    def body(half_ref, g_ref, r_ref, o_ref):
        o_ref[...] = (g_ref[...] + r_ref[...]).astype(BF16)

    grid_spec = pltpu.PrefetchScalarGridSpec(
        num_scalar_prefetch=1, grid=(n * L, nb),
        in_specs=[pl.BlockSpec((1, tr, C), lambda k, i, h: (k, h[0] * nb + i, 0)),
                  pl.BlockSpec((1, tr, C), lambda k, i, h: (k, i, 0))],
        out_specs=pl.BlockSpec((1, tr, C), lambda k, i, h: (k, i, 0)))
    out = pl.pallas_call(
        body, name="rs_add_halves_" + name, grid_spec=grid_spec,
        out_shape=jax.ShapeDtypeStruct((n * L, hr, C), BF16), compiler_params=_params(2),
    )(half.reshape(1).astype(jnp.int32), G.reshape(n * L, R, C), recv.reshape(n * L, hr, C))
    return out.reshape(n, L, hr, C)


def _sum_slots(slots, name):
    n, L, hr, C = slots.shape
    rows = L * hr
    tr = _rows_tile(rows, C)

    def body(s_ref, o_ref):
        acc = s_ref[0].astype(F32)
        for k in range(1, n):
            acc = acc + s_ref[k].astype(F32)
        o_ref[...] = acc

    out = pl.pallas_call(
        body, name="rs_sum_slots_" + name, grid=(rows // tr,),
        in_specs=[pl.BlockSpec((n, tr, C), lambda i: (0, i, 0))], out_specs=pl.BlockSpec((tr, C), lambda i: (i, 0)),
        out_shape=jax.ShapeDtypeStruct((rows, C), F32), compiler_params=_params(1),
    )(slots.reshape(n, rows, C))
    return out.reshape(L, hr, C)


HBM = pl.BlockSpec(memory_space=pltpu.HBM)


def _place():
    x, y, c = lax.axis_index("x"), lax.axis_index("y"), lax.axis_index("c")
    return x, y, c, 2 * x + y


def _chip_device(chip, c):
    return (chip // 2, chip % 2, c)


def _remote(src, dst, send_sem, recv_sem, to):
    return pltpu.make_async_remote_copy(src_ref=src, dst_ref=dst, send_sem=send_sem, recv_sem=recv_sem, device_id=to,
                                        device_id_type=MESH)


def _gather_weights(shards):
    n = len(shards)

    def body(*refs):
        w_refs, o_refs = refs[:n], refs[n:2 * n]
        send_sems, recv_sems, local_sems = refs[2 * n:]
        x, y, c, me = _place()
        sibling = (x, y, 1 - c)
        local = [pltpu.make_async_copy(w_refs[t], o_refs[t].at[:, me], local_sems.at[t]) for t in range(n)]
        for cp in local:
            cp.start()

        def copy(t, k, src_chip, half, to):
            hr = w_refs[t].shape[1] // 2
            rows = pl.ds(half * hr, hr)
            src = w_refs[t].at[:, rows, :] if src_chip is None else o_refs[t].at[:, src_chip, rows, :]
            dst = o_refs[t].at[:, me if src_chip is None else src_chip, rows, :]
            return _remote(src, dst, send_sems.at[6 * t + k], recv_sems.at[6 * t + k], to)

        first = [copy(t, d - 1, None, c, _chip_device(me ^ d, c)) for t in range(n) for d in (1, 2, 3)]
        for cp in first:
            cp.start()
        passed = []
        for t in range(n):
            for d in (1, 2, 3):
                copy(t, d - 1, me ^ d, c, sibling).wait_recv()
                passed.append(copy(t, 2 + d, me ^ d, c, sibling))
                passed[-1].start()
        for t in range(n):
            for d in (1, 2, 3):
                copy(t, 2 + d, me ^ d, 1 - c, sibling).wait_recv()
        for cp in first + passed:
            cp.wait_send()
        for cp in local:
            cp.wait()

    return pl.pallas_call(
        body, name="gather_weights", in_specs=[HBM] * n, out_specs=[HBM] * n,
        out_shape=[jax.ShapeDtypeStruct((s.shape[0], N_CHIPS) + s.shape[1:], s.dtype) for s in shards],
        scratch_shapes=[pltpu.SemaphoreType.DMA((6 * n,)), pltpu.SemaphoreType.DMA((6 * n,)), pltpu.SemaphoreType.DMA((n,))],
    )(*shards)


def _exchange_halves(Gs):
    n = len(Gs)

    def body(*refs):
        g_refs, o_refs = refs[:n], refs[n:2 * n]
        send_sems, recv_sems = refs[2 * n:]
        x, y, c, _ = _place()
        copies = []
        for t in range(n):
            hr = g_refs[t].shape[2] // 2
            copies.append(_remote(g_refs[t].at[:, :, pl.ds((1 - c) * hr, hr), :], o_refs[t], send_sems.at[t], recv_sems.at[t],
                                  (x, y, 1 - c)))
            copies[-1].start()
        for cp in copies:
            cp.wait()

    return pl.pallas_call(
        body, name="rs_exchange_halves", in_specs=[HBM] * n, out_specs=[HBM] * n,
        out_shape=[jax.ShapeDtypeStruct(g.shape[:2] + (g.shape[2] // 2, g.shape[3]), g.dtype) for g in Gs],
        scratch_shapes=[pltpu.SemaphoreType.DMA((n,)), pltpu.SemaphoreType.DMA((n,))],
    )(*Gs)


def _scatter_partials(Hs):
    n = len(Hs)

    def body(*refs):
        h_refs, o_refs = refs[:n], refs[n:2 * n]
        send_sems, recv_sems, local_sems = refs[2 * n:]
        x, y, c, me = _place()
        local = [pltpu.make_async_copy(h_refs[t].at[me], o_refs[t].at[me], local_sems.at[t]) for t in range(n)]
        for cp in local:
            cp.start()
        sends = [_remote(h_refs[t].at[me ^ d], o_refs[t].at[me], send_sems.at[3 * t + d - 1], recv_sems.at[3 * t + d - 1],
                         _chip_device(me ^ d, c)) for t in range(n) for d in (1, 2, 3)]
        for cp in sends:
            cp.start()
        for t in range(n):
            for d in (1, 2, 3):
                _remote(h_refs[t].at[me ^ d], o_refs[t].at[me ^ d], send_sems.at[3 * t + d - 1], recv_sems.at[3 * t + d - 1],
                        _chip_device(me ^ d, c)).wait_recv()
        for cp in sends:
            cp.wait_send()
        for cp in local:
            cp.wait()

    return pl.pallas_call(
        body, name="rs_scatter_partials", in_specs=[HBM] * n, out_specs=[HBM] * n,
        out_shape=[jax.ShapeDtypeStruct(h.shape, h.dtype) for h in Hs],
        scratch_shapes=[pltpu.SemaphoreType.DMA((3 * n,)), pltpu.SemaphoreType.DMA((3 * n,)), pltpu.SemaphoreType.DMA((n,))],
    )(*Hs)


def _join_halves(Shs):
    n = len(Shs)

    def body(*refs):
        s_refs, o_refs = refs[:n], refs[n:2 * n]
        send_sems, recv_sems, local_sems = refs[2 * n:]
        x, y, c, _ = _place()
        local, sends = [], []
        for t in range(n):
            hr = s_refs[t].shape[1]
            local.append(pltpu.make_async_copy(s_refs[t], o_refs[t].at[:, pl.ds(c * hr, hr), :], local_sems.at[t]))
            local[-1].start()
            sends.append(_remote(s_refs[t], o_refs[t].at[:, pl.ds(c * hr, hr), :], send_sems.at[t], recv_sems.at[t], (x, y, 1 - c)))
            sends[-1].start()
        for t in range(n):
            hr = s_refs[t].shape[1]
            _remote(s_refs[t], o_refs[t].at[:, pl.ds((1 - c) * hr, hr), :], send_sems.at[t], recv_sems.at[t],
                    (x, y, 1 - c)).wait_recv()
        for cp in sends:
            cp.wait_send()
        for cp in local:
            cp.wait()

    return pl.pallas_call(
        body, name="rs_join_halves", in_specs=[HBM] * n, out_specs=[HBM] * n,
        out_shape=[jax.ShapeDtypeStruct((s.shape[0], 2 * s.shape[1], s.shape[2]), s.dtype) for s in Shs],
        scratch_shapes=[pltpu.SemaphoreType.DMA((n,)), pltpu.SemaphoreType.DMA((n,)), pltpu.SemaphoreType.DMA((n,))],
    )(*Shs)


def _all_reduce_small(v, name):
    R, C = v.shape

    def body(v_ref, o_ref, slots, send_sems, recv_sems):
        x, y, c, _ = _place()
        me = 4 * x + 2 * y + c
        slots[me] = v_ref[...]
        sends = []
        for d in range(1, 8):
            peer = me ^ d
            sends.append(pltpu.make_async_remote_copy(
                src_ref=v_ref, dst_ref=slots.at[me], send_sem=send_sems.at[d - 1], recv_sem=recv_sems.at[d - 1],
                device_id=(peer // 4, (peer // 2) % 2, peer % 2), device_id_type=MESH))
        for cp in sends:
            cp.start()
        for d in range(1, 8):
            peer = me ^ d
            pltpu.make_async_remote_copy(
                src_ref=v_ref, dst_ref=slots.at[peer], send_sem=send_sems.at[d - 1], recv_sem=recv_sems.at[d - 1],
                device_id=(peer // 4, (peer // 2) % 2, peer % 2), device_id_type=MESH).wait_recv()
        for cp in sends:
            cp.wait_send()
        acc = slots[0]
        for k in range(1, 8):
            acc = acc + slots[k]
        o_ref[...] = acc

    vm = pl.BlockSpec(memory_space=pltpu.VMEM)
    return pl.pallas_call(
        body, name=name, in_specs=[vm], out_specs=vm, out_shape=jax.ShapeDtypeStruct((R, C), F32),
        scratch_shapes=[pltpu.VMEM((8, R, C), F32), pltpu.SemaphoreType.DMA((7,)), pltpu.SemaphoreType.DMA((7,))],
    )(v)


SHARDED = ("w_in", "w_uq", "w_ukv", "conv_w", "w_br_a", "w_br_b", "w_br_c", "w_br_d", "w_out", "w_ffn_gate", "w_ffn_up",
           "w_ffn_down")
ROW_SHARDED = ("w_out", "w_ffn_down")
REPLICATED = ("g_pre_mix", "g_cq", "g_ckv", "pool_w", "pool_scale", "g_sgu_v", "sgu_w", "sgu_b", "g_post_mix", "g_pre_ffn",
              "g_post_ffn")
WEIGHTS = ("w_in", "g_pre_mix", "g_cq", "g_ckv", "w_uq", "w_ukv", "pool_w", "pool_scale", "g_sgu_v", "sgu_w", "sgu_b",
           "conv_w", "w_br_a", "w_br_b", "w_br_c", "w_br_d", "w_out", "g_post_mix", "g_pre_ffn", "w_ffn_gate", "w_ffn_up",
           "w_ffn_down", "g_post_ffn")
GATHERED = tuple(n for n in SHARDED if n != "conv_w")


def _unpack(packed, shapes):
    flat = packed.reshape(-1)
    out, o = [], 0
    for s in shapes:
        n = int(np.prod(s))
        out.append(flat[o:o + n].reshape(s))
        o += n
    return out


def _join_cols(g, l):
    return jnp.concatenate([g[l, k] for k in range(N_CHIPS)], axis=1)


def _pad_heads(w, real):
    lead = w.shape[:-1]
    w = w.reshape(lead + (HEADS, real))
    return jnp.pad(w, [(0, 0)] * len(lead) + [(0, 0), (0, HEAD_PAD - real)]).reshape(lead + (MLA_W,))


def _unpad_heads(w, real):
    lead = w.shape[:-1]
    return w.reshape(lead + (HEADS, HEAD_PAD))[..., :real].reshape(lead + (HEADS * real,))


IN_OFFSETS = {"cq": 0, "ckv": Q_LORA, "kr": Q_LORA + KV_LORA, "mix": Q_LORA + KV_LORA + QK_ROPE}
IN_GATES = Q_LORA + KV_LORA + QK_ROPE + SEC_MIX


def _pad_w_in(w):
    K = w.shape[0]
    z = lambda n: jnp.zeros((K, n), w.dtype)
    return jnp.concatenate([w[:, IN_GATES:], w[:, :IN_OFFSETS["kr"]], z(QK_NOPE), w[:, IN_OFFSETS["kr"]:IN_OFFSETS["mix"]],
                            z(HEAD_PAD - QK_NOPE - QK_ROPE), w[:, IN_OFFSETS["mix"]:IN_GATES]], axis=1)


def _unpad_w_in(d_gates, d_mla, d_mix):
    kr = d_mla[:, Q_LORA + KV_LORA + QK_NOPE:Q_LORA + KV_LORA + QK_NOPE + QK_ROPE]
    return jnp.concatenate([d_mla[:, :Q_LORA + KV_LORA], kr, d_mix, d_gates], axis=1)


def _rope_tables(S):
    half = QK_ROPE // 2
    inv = ROPE_THETA ** (-jnp.arange(0, QK_ROPE, 2, dtype=F32) / QK_ROPE)
    ang = jnp.arange(S, dtype=F32)[:, None] * inv[None, :]
    cos, sin = jnp.cos(ang), jnp.sin(ang)
    one, zero = jnp.ones((S, QK_NOPE), F32), jnp.zeros((S, half), F32)
    tail = HEAD_PAD - QK_NOPE - QK_ROPE
    c = jnp.concatenate([one, cos, cos, jnp.ones((S, tail), F32)], axis=1)
    sa = jnp.concatenate([0 * one, zero, sin, jnp.zeros((S, tail), F32)], axis=1)
    sb = jnp.concatenate([0 * one, -sin, zero, jnp.zeros((S, tail), F32)], axis=1)
    return c, sa, sb


def _layer_weights(gathered, full, l, D):
    w = {}
    w_in = _pad_w_in(_join_cols(gathered["w_in"], l))
    w["w_in"] = w_in
    w["w_in_gates"], w["w_in_mla"], w["w_in_mix"] = w_in[:, :4 * D], w_in[:, 4 * D:4 * D + SEC_MLA], w_in[:, 4 * D + SEC_MLA:]
    w["w_uq"] = _pad_heads(_join_cols(gathered["w_uq"], l), QK_NOPE + QK_ROPE)
    ukv = _join_cols(gathered["w_ukv"], l).reshape(KV_LORA, HEADS, QK_NOPE + V_HEAD)
    pad = ((0, 0), (0, 0), (0, HEAD_PAD - QK_NOPE))
    k_pad = jnp.pad(ukv[:, :, :QK_NOPE], pad).reshape(KV_LORA, N_CHIPS, CHIP_HEADS_W)
    v_pad = jnp.pad(ukv[:, :, QK_NOPE:], pad).reshape(KV_LORA, N_CHIPS, CHIP_HEADS_W)
    w["w_ukv"] = jnp.concatenate([k_pad, v_pad], axis=2).reshape(KV_LORA, 2 * MLA_W)
    w["w_br_a"] = jnp.pad(_join_cols(gathered["w_br_a"], l).reshape(HEADS, V_HEAD, D),
                          ((0, 0), (0, HEAD_PAD - V_HEAD), (0, 0))).reshape(MLA_W, D)
    for n in ("w_br_b", "w_br_c", "w_br_d"):
        w[n] = _join_cols(gathered[n], l)
    w["w_out"] = gathered["w_out"][l].reshape(D, D)
    for n in ("g_pre_mix", "g_cq", "g_ckv", "pool_scale", "g_sgu_v", "g_post_mix", "g_pre_ffn", "g_post_ffn"):
        w[n] = full[n][l].reshape(1, -1)
    pw = full["pool_w"][l]
    w["pool_bd"] = jax.scipy.linalg.block_diag(*[pw[g] for g in range(GROUPS)]).astype(BF16)
    w["sgu_w"] = full["sgu_w"][l]
    w["sgu_bias"] = jnp.repeat(full["sgu_b"][l].T, GROUP_DIM, axis=1)
    w["conv_w"] = jnp.pad(full["conv_w"][l].reshape(3, MIX_W), ((0, CONV_HALO - 3), (0, 0)))
    return w


def _layer_fwd(x, w, gathered, l, S, rope):
    D = x.shape[1]
    P, h = _norm_matmul(x, w["g_pre_mix"], w["w_in"], "proj_fwd")
    Q, K, V, hq, hkv = _mla_prep(P, D, w["g_cq"], w["g_ckv"], w["w_uq"], w["w_ukv"], *rope, S)
    A, lse = _flash_fwd(Q, K, V, S)
    Bm, C, Dv = _mixers_fwd(P, D, S, w["pool_bd"], w["pool_scale"], w["g_sgu_v"], w["sgu_w"], w["sgu_bias"], w["conv_w"])
    x1, merged, o = _merge_fwd(x, P, A, Bm, C, Dv, w["w_br_a"], w["w_br_b"], w["w_br_c"], w["w_br_d"], w["w_out"], w["g_post_mix"])
    x2, h2, gt, up, f = _ffn_fwd(x1, w["g_pre_ffn"], gathered["w_ffn_gate"], gathered["w_ffn_up"], gathered["w_ffn_down"],
                                 w["g_post_ffn"], l)
    saved = dict(x=x, P=P, h=h, Q=Q, K=K, V=V, hq=hq, hkv=hkv, A=A, lse=lse, Bm=Bm, C=C, Dv=Dv, x1=x1, merged=merged, o=o,
                 h2=h2, gt=gt, up=up, f=f)
    return x2, saved


def _layer_bwd(dx2, w, gathered, l, s, S, rope, bufs):
    D = dx2.shape[1]
    Fc = gathered["w_ffn_gate"].shape[3]
    g = {}

    def wgrad(n, a, b, rows, cols, a_mode, b_mode):
        bufs[n] = _wgrad_chip(bufs.get(n), l, DEPTH, a, b, rows, cols, a_mode, b_mode, "wgrad_" + n)

    dx1, df, dgt, dup, act, g["g_pre_ffn"], g["g_post_ffn"] = _ffn_bwd(
        dx2, s["x1"], s["f"], s["gt"], s["up"], w["g_pre_ffn"], gathered["w_ffn_gate"], gathered["w_ffn_up"],
        gathered["w_ffn_down"], w["g_post_ffn"], l)
    wgrad("w_ffn_down", act, df, Fc, D, "slab", "all")
    wgrad("w_ffn_gate", s["h2"], dgt, D, Fc, "all", "slab")
    wgrad("w_ffn_up", s["h2"], dup, D, Fc, "all", "slab")

    d_o, dPg, dya, dyb, dyc, dyd, dA, dBm, dC, dDv, g["g_post_mix"] = _merge_bwd(
        dx1, s["o"], s["P"], s["A"], s["Bm"], s["C"], s["Dv"], w["w_br_a"], w["w_br_b"], w["w_br_c"], w["w_br_d"], w["w_out"],
        w["g_post_mix"])
    wgrad("w_out", s["merged"], d_o, D // N_CHIPS, D, "cols", "all")
    wgrad("w_br_a", s["A"], dya, MLA_W, D // N_CHIPS, "all", "cols")
    wgrad("w_br_b", s["Bm"], dyb, MIX_W, D // N_CHIPS, "all", "cols")
    wgrad("w_br_c", s["C"], dyc, MIX_W, D // N_CHIPS, "all", "cols")
    wgrad("w_br_d", s["Dv"], dyd, MIX_W, D // N_CHIPS, "all", "cols")

    dPm, d_pool_bd, g_ps, g_gv, g["sgu_w"], d_bias, d_cw = _mixers_bwd(
        s["P"], D, S, dBm, dC, dDv, w["pool_bd"], w["pool_scale"], w["g_sgu_v"], w["sgu_w"], w["sgu_bias"], w["conv_w"])
    g["pool_w"] = jnp.stack([d_pool_bd[k * GROUP_DIM:(k + 1) * GROUP_DIM, k * GROUP_DIM:(k + 1) * GROUP_DIM] for k in range(GROUPS)])
    g["pool_scale"], g["g_sgu_v"] = g_ps, g_gv
    g["sgu_b"] = d_bias.reshape(SGU_BLOCK, GROUPS, GROUP_DIM).sum(-1).T
    g["conv_w"] = d_cw[:3].reshape(3, 1, MIX_W)

    delta = _attn_delta(s["A"], dA)
    dQ, dK, dV = _flash_bwd(s["Q"], s["K"], s["V"], dA, s["lse"], delta, S)
    dPa, dq, dkv, g["g_cq"], g["g_ckv"] = _mla_bwd_post(s["P"], D, S, dQ, dK, dV, w["g_cq"], w["g_ckv"], w["w_uq"], w["w_ukv"], *rope)
    wgrad("w_uq", s["hq"], dq, Q_LORA, CHIP_HEADS_W, "all", "cols")
    wgrad("w_ukv", s["hkv"], dkv, KV_LORA, CHIP_KV, "all", "cols")

    dx, g["g_pre_mix"] = _proj_bwd(dx1, s["x"], w["g_pre_mix"], dPg, dPa, dPm, w["w_in_gates"], w["w_in_mla"], w["w_in_mix"])
    d_w_in = _unpad_w_in(_matmul_tn(s["h"], dPg, "wgrad_in_gates"), _matmul_tn(s["h"], dPa, "wgrad_in_mla"),
                         _matmul_tn(s["h"], dPm, "wgrad_in_mix"))
    g["w_in"] = d_w_in.reshape(D, N_CHIPS, -1).transpose(1, 0, 2)
    for n in ("g_pre_mix", "g_cq", "g_ckv", "pool_scale", "g_sgu_v", "g_post_mix", "g_pre_ffn", "g_post_ffn"):
        g[n] = g[n].reshape(-1)
    return dx, g


SMALL = REPLICATED + ("conv_w",)


def _local_step(x, target, gathered, full):
    n_seq, S, D = x.shape
    rope = _rope_tables(S)
    xs = x.reshape(n_seq * S, D)
    weights, saved = [], []
    for l in range(DEPTH):
        w = _layer_weights(gathered, full, l, D)
        xs, s = _layer_fwd(xs, w, gathered, l, S, rope)
        weights.append(w)
        saved.append(s)
    loss_parts, dx = _loss_grad(xs, target.reshape(n_seq * S, D))
    grads, bufs = [None] * DEPTH, {}
    for l in reversed(range(DEPTH)):
        dx, grads[l] = _layer_bwd(dx, weights[l], gathered, l, saved[l], S, rope, bufs)
    bufs["w_in"] = jnp.stack([grads[l]["w_in"] for l in range(DEPTH)], axis=1)
    small = {n: jnp.stack([grads[l][n] for l in range(DEPTH)]) for n in SMALL}
    return loss_parts, dx.reshape(n_seq, S, D), bufs, small


def _unpad_reduced(n, r):
    L = r.shape[0]
    if n == "w_uq":
        return r.reshape(L, Q_LORA, 2, HEAD_PAD)[..., :QK_NOPE + QK_ROPE].reshape(L, Q_LORA, -1)
    if n == "w_ukv":
        r = r.reshape(L, KV_LORA, 2, 2, HEAD_PAD)[..., :QK_NOPE]
        return jnp.concatenate([r[:, :, 0], r[:, :, 1]], axis=-1).reshape(L, KV_LORA, -1)
    if n == "w_br_a":
        return r.reshape(L, HEADS, HEAD_PAD, -1)[:, :, :V_HEAD].reshape(L, HEADS * V_HEAD, -1)
    return r


def _small_rows(n):
    return -(-n // (8 * 128)) * 8


def _to_small(parts):
    flat = jnp.concatenate([p.reshape(-1) for p in parts])
    rows = _small_rows(flat.shape[0])
    return jnp.pad(flat, (0, rows * 128 - flat.shape[0])).reshape(rows, 128)


def kernel(x, w_in, g_pre_mix, g_cq, g_ckv, w_uq, w_ukv, pool_w, pool_scale, g_sgu_v, sgu_w, sgu_b, conv_w, w_br_a, w_br_b, w_br_c, w_br_d, w_out, g_post_mix, g_pre_ffn, w_ffn_gate, w_ffn_up, w_ffn_down, g_post_ffn, loss_target, m_w_in, m_g_pre_mix, m_g_cq, m_g_ckv, m_w_uq, m_w_ukv, m_pool_w, m_pool_scale, m_g_sgu_v, m_sgu_w, m_sgu_b, m_conv_w, m_w_br_a, m_w_br_b, m_w_br_c, m_w_br_d, m_w_out, m_g_post_mix, m_g_pre_ffn, m_w_ffn_gate, m_w_ffn_up, m_w_ffn_down, m_g_post_ffn, v_w_in, v_g_pre_mix, v_g_cq, v_g_ckv, v_w_uq, v_w_ukv, v_pool_w, v_pool_scale, v_g_sgu_v, v_sgu_w, v_sgu_b, v_conv_w, v_w_br_a, v_w_br_b, v_w_br_c, v_w_br_d, v_w_out, v_g_post_mix, v_g_pre_ffn, v_w_ffn_gate, v_w_ffn_up, v_w_ffn_down, v_g_post_ffn):
    local = dict(locals())
    W = {n: local[n] for n in WEIGHTS}
    M = {n: local["m_" + n] for n in WEIGHTS}
    V = {n: local["v_" + n] for n in WEIGHTS}
    chip = 2 * lax.axis_index("x") + lax.axis_index("y")
    core = lax.axis_index("c")

    two_d = lambda a: a.reshape(a.shape[0], -1, a.shape[-1])
    gathered = dict(zip(GATHERED, _gather_weights([two_d(W[n].astype(BF16)) for n in GATHERED])))
    conv_shape = conv_w.shape
    conv_cols = conv_shape[-1]
    conv_full_shape = conv_shape[:-1] + (N_CHIPS * conv_cols,)
    placed = lax.dynamic_update_slice(jnp.zeros(conv_full_shape, F32), conv_w, (0, 0, 0, chip * conv_cols))
    n_conv = int(np.prod(conv_full_shape))
    conv_sum = _all_reduce_small(_to_small([placed]), "gather_conv_w")
    full = {n: W[n] for n in REPLICATED}
    full["conv_w"] = 0.5 * conv_sum.reshape(-1)[:n_conv].reshape(conv_full_shape)

    loss_parts, grad_x, bufs, small = _local_step(x, loss_target, gathered, full)
    loss = lax.psum(jnp.sum(loss_parts), ("x", "y", "c"))

    small_sum = _all_reduce_small(_to_small([small[n] for n in SMALL]), "reduce_small_grads")
    small_grads = dict(zip(SMALL, _unpack(small_sum, [small[n].shape for n in SMALL])))
    small_grads["conv_w"] = lax.dynamic_slice(small_grads["conv_w"], (0, 0, 0, chip * conv_cols), conv_shape)

    Gs = [bufs[n] for n in GATHERED]
    Hs = [_add_halves(g, r, core, n) for n, g, r in zip(GATHERED, Gs, _exchange_halves(Gs))]
    halves = [_sum_slots(s, n) for n, s in zip(GATHERED, _scatter_partials(Hs))]
    shard_grads = {n: _unpad_reduced(n, r).reshape(W[n].shape) for n, r in zip(GATHERED, _join_halves(halves))}

    out_g, out_d, out_m, out_v = {}, {}, {}, {}
    for n in GATHERED:
        shp = W[n].shape
        flat = lambda a: a.reshape(-1, shp[-1])
        d, m2, v2 = _adamw(flat(W[n]), flat(shard_grads[n]), flat(M[n]), flat(V[n]), "adamw_" + n)
        out_g[n], out_d[n], out_m[n], out_v[n] = shard_grads[n], d.reshape(shp), m2.reshape(shp), v2.reshape(shp)
    rest_shapes = [W[n].shape for n in SMALL]
    d, m2, v2 = _adamw(_to_small([W[n] for n in SMALL]), _to_small([small_grads[n] for n in SMALL]),
                       _to_small([M[n] for n in SMALL]), _to_small([V[n] for n in SMALL]), "adamw_small")
    for n, dd, mm, vv in zip(SMALL, _unpack(d, rest_shapes), _unpack(m2, rest_shapes), _unpack(v2, rest_shapes)):
        out_g[n], out_d[n], out_m[n], out_v[n] = small_grads[n], dd, mm, vv

    return (loss, grad_x, *[out_g[n] for n in WEIGHTS], *[out_d[n] for n in WEIGHTS], *[out_m[n] for n in WEIGHTS],
            *[out_v[n] for n in WEIGHTS])
```

```python
import functools

import numpy as np
import jax
import jax.numpy as jnp
from jax import lax
from jax.experimental import pallas as pl
from jax.experimental.pallas import tpu as pltpu

F32 = jnp.float32
BF16 = jnp.bfloat16

EPS = 1e-6
NEG_INF = -1e30
DEPTH = 4
HEADS = 8
QK_NOPE = 64
QK_ROPE = 32
V_HEAD = 64
HEAD_PAD = 128
Q_LORA = 256
KV_LORA = 128
ROPE_THETA = 10000.0
POOL_WINDOWS = (2, 4, 8, 16)
GROUPS = 4
GROUP_DIM = 64
MIX_W = GROUPS * GROUP_DIM
POOL_HALO = 16
CONV_HALO = 8
SGU_BLOCK = 128
CHUNK = 64
CHUNK_SHIFT = 6
SOFTMAX_ROWS = 32
GROUP_SHIFT = 6
N_BRANCH = 4
MLA_W = HEADS * HEAD_PAD
N_CHIPS = 4
CHIP_HEADS_W = MLA_W // N_CHIPS
CHIP_KV = 2 * CHIP_HEADS_W
ATTN_SCALE = (QK_NOPE + QK_ROPE) ** -0.5
LOG2E = 1.4426950408889634
LN2 = 0.6931471805599453
SEC_MLA = Q_LORA + KV_LORA + HEAD_PAD
SEC_MIX = 6 * MIX_W

ADAM_LR = 0.001
ADAM_B1 = 0.9
ADAM_B2 = 0.999
ADAM_EPS = 1e-08
ADAM_WD = 0.01
ADAM_STEP = 10

VMEM_LIMIT = 56 * 1024 * 1024
MESH = pl.DeviceIdType.MESH


def _tile(n, pref, mult=8):
    t = min(n, pref)
    while t > 0:
        if n % t == 0 and t % mult == 0:
            return t
        t -= 1
    return n


def _params(n_axes):
    return pltpu.CompilerParams(dimension_semantics=("arbitrary",) * n_axes, vmem_limit_bytes=VMEM_LIMIT)


def _dot(a, b):
    return jnp.dot(a, b, preferred_element_type=F32)


def _dot_nt(a, b):
    return lax.dot_general(a, b, (((1,), (1,)), ((), ())), preferred_element_type=F32)


def _dot_tn(a, b):
    return lax.dot_general(a, b, (((0,), (0,)), ((), ())), preferred_element_type=F32)


def _rms_r(x):
    return lax.rsqrt(jnp.mean(x * x, axis=-1, keepdims=True) + EPS)


def _rms_bwd(dy, x, g):
    r = _rms_r(x)
    u = dy * g
    dx = r * u - x * (r * r * r * jnp.mean(u * x, axis=-1, keepdims=True))
    dg = jnp.sum(dy * x * r, axis=0, keepdims=True)
    return dx, dg


def _sigmoid(x):
    return 1.0 / (1.0 + jnp.exp(-x))


def _shift_down(a, k):
    return pltpu.roll(a, k, 0)


def _shift_up(a, k):
    return pltpu.roll(a, a.shape[0] - k, 0)


def _rope(x, c, sa, sb):
    w = x.shape[-1]
    return x * c + pltpu.roll(x, QK_ROPE // 2, 1) * sa + pltpu.roll(x, w - QK_ROPE // 2, 1) * sb


def _rope_t(d, c, sa, sb):
    w = d.shape[-1]
    return d * c + pltpu.roll(d * sa, w - QK_ROPE // 2, 1) + pltpu.roll(d * sb, QK_ROPE // 2, 1)


def _full(shape):
    return pl.BlockSpec(shape, lambda *_: (0,) * len(shape))


def _norm_matmul(x, g, w, name):
    T, K = x.shape
    N = w.shape[1]
    tm, tn = _tile(T, 512), _tile(N, 1536, 128)

    def body(x_ref, g_ref, w_ref, o_ref, h_ref):
        @pl.when(pl.program_id(1) == 0)
        def _():
            xv = x_ref[...]
            h_ref[...] = (xv * _rms_r(xv) * g_ref[...]).astype(BF16)

        o_ref[...] = _dot(h_ref[...], w_ref[...])

    return pl.pallas_call(
        body, name=name, grid=(T // tm, N // tn),
        in_specs=[pl.BlockSpec((tm, K), lambda i, j: (i, 0)), _full((1, K)), pl.BlockSpec((K, tn), lambda i, j: (0, j))],
        out_specs=[pl.BlockSpec((tm, tn), lambda i, j: (i, j)), pl.BlockSpec((tm, K), lambda i, j: (i, 0))],
        out_shape=[jax.ShapeDtypeStruct((T, N), F32), jax.ShapeDtypeStruct((T, K), BF16)],
        compiler_params=_params(2),
    )(x, g, w)


def _mla_prep(P, D, g_cq, g_ckv, wuq, wukv, rope_c, rope_sa, rope_sb, S):
    T = P.shape[0]
    tm = _tile(S, 512)
    n_si = S // tm
    base = 4 * D

    def body(cq_ref, ckv_ref, kr_ref, gq_ref, gkv_ref, wq_ref, wkv_ref, c_ref, sa_ref, sb_ref,
             q_ref, k_ref, v_ref, hq_ref, hkv_ref):
        c, sa, sb = c_ref[...], sa_ref[...], sb_ref[...]
        cq = cq_ref[...]
        hq = (cq * _rms_r(cq) * gq_ref[...]).astype(BF16)
        hq_ref[...] = hq
        q = _dot(hq, wq_ref[...])
        q = _rope(q, jnp.tile(c, (1, HEADS)), jnp.tile(sa, (1, HEADS)), jnp.tile(sb, (1, HEADS)))
        q_ref[...] = (q * (ATTN_SCALE * LOG2E)).astype(BF16)
        ckv = ckv_ref[...]
        hkv = (ckv * _rms_r(ckv) * gkv_ref[...]).astype(BF16)
        hkv_ref[...] = hkv
        kv = _dot(hkv, wkv_ref[...])
        kr = _rope(kr_ref[...], c, sa, sb)
        k_nope = jnp.concatenate([kv[:, j * CHIP_KV:j * CHIP_KV + CHIP_HEADS_W] for j in range(N_CHIPS)], axis=1)
        k_ref[...] = (k_nope + jnp.tile(kr, (1, HEADS))).astype(BF16)
        v = jnp.concatenate([kv[:, j * CHIP_KV + CHIP_HEADS_W:(j + 1) * CHIP_KV] for j in range(N_CHIPS)], axis=1)
        ones_lane = (lax.broadcasted_iota(jnp.int32, (1, MLA_W), 1) & (HEAD_PAD - 1)) == V_HEAD
        v_ref[...] = jnp.where(ones_lane, 1.0, v).astype(BF16)

    tab = pl.BlockSpec((tm, HEAD_PAD), lambda i: (i % n_si, 0))
    row = lambda w: pl.BlockSpec((tm, w), lambda i: (i, 0))
    return pl.pallas_call(
        body, name="mla_prep", grid=(T // tm,),
        in_specs=[pl.BlockSpec((tm, Q_LORA), lambda i: (i, base // Q_LORA)),
                  pl.BlockSpec((tm, KV_LORA), lambda i: (i, (base + Q_LORA) // KV_LORA)),
                  pl.BlockSpec((tm, HEAD_PAD), lambda i: (i, (base + Q_LORA + KV_LORA) // HEAD_PAD)),
                  _full((1, Q_LORA)), _full((1, KV_LORA)), _full((Q_LORA, MLA_W)), _full((KV_LORA, 2 * MLA_W)),
                  tab, tab, tab],
        out_specs=[row(MLA_W), row(MLA_W), row(MLA_W), row(Q_LORA), row(KV_LORA)],
        out_shape=[jax.ShapeDtypeStruct((T, MLA_W), BF16)] * 3
        + [jax.ShapeDtypeStruct((T, Q_LORA), BF16), jax.ShapeDtypeStruct((T, KV_LORA), BF16)],
        compiler_params=_params(1),
    )(P, P, P, g_cq, g_ckv, wuq, wukv, rope_c, rope_sa, rope_sb)


def _chunk_mask(tq, tk):
    row = lax.broadcasted_iota(jnp.int32, (tq, tk), 0)
    col = lax.broadcasted_iota(jnp.int32, (tq, tk), 1)
    return (row >> CHUNK_SHIFT) >= (col >> CHUNK_SHIFT)


def _flash_fwd(Q, K, V, S):
    T = Q.shape[0]
    n_seq = T // S
    tq = _tile(S, 256, 128)
    nq = S // tq

    def body(q_ref, k_ref, v_ref, o_ref, lse_ref, m_s, acc_s, s_s, p_s, a_s):
        qi = pl.program_id(1)
        m_s[...] = jnp.full(m_s.shape, NEG_INF, F32)
        acc_s[...] = jnp.zeros_like(acc_s)

        def block(kb, masked):
            rows = pl.ds(pl.multiple_of(kb * tq, tq), tq)
            for h in range(HEADS):
                hs = slice(h * HEAD_PAD, (h + 1) * HEAD_PAD)
                s_s[h] = _dot_nt(q_ref[:, hs], k_ref[rows, hs])
            def softmax_head(h):
                for r in range(0, tq, SOFTMAX_ROWS):
                    rs = slice(r, r + SOFTMAX_ROWS)
                    s = s_s[h, rs, :]
                    if masked:
                        row = r + lax.broadcasted_iota(jnp.int32, (SOFTMAX_ROWS, tq), 0)
                        col = lax.broadcasted_iota(jnp.int32, (SOFTMAX_ROWS, tq), 1)
                        s = jnp.where((row >> CHUNK_SHIFT) >= (col >> CHUNK_SHIFT), s, NEG_INF)
                    m_old = m_s[h, rs]
                    m_new = jnp.maximum(m_old, jnp.max(s, axis=-1, keepdims=True))
                    m_s[h, rs] = m_new
                    a_s[h, rs] = jnp.exp2(m_old - m_new)
                    for half in range(tq // HEAD_PAD):
                        cs = slice(half * HEAD_PAD, (half + 1) * HEAD_PAD)
                        p_s[h, rs, cs] = jnp.exp2(s[:, cs] - m_new).astype(BF16)

            for h in range(HEADS):
                softmax_head(h)
            for h in range(HEADS):
                hs = slice(h * HEAD_PAD, (h + 1) * HEAD_PAD)
                acc_s[:, hs] = a_s[h] * acc_s[:, hs] + _dot(p_s[h], v_ref[rows, hs])

        def full_block(kb, carry):
            block(kb, False)
            return carry

        lax.fori_loop(0, qi, full_block, 0)
        block(qi, True)
        lane = lax.broadcasted_iota(jnp.int32, (tq, HEAD_PAD), 1)
        lse_all = jnp.zeros((tq, HEAD_PAD), F32)
        for h in range(HEADS):
            hs = slice(h * HEAD_PAD, (h + 1) * HEAD_PAD)
            acc = acc_s[:, hs]
            l = jnp.sum(jnp.where(lane == V_HEAD, acc, 0.0), axis=-1, keepdims=True)
            o_ref[:, hs] = (acc / l).astype(BF16)
            lse_all = jnp.where(lane == h, m_s[h] + jnp.log2(l), lse_all)
        lse_ref[...] = lse_all

    return pl.pallas_call(
        body, name="flash_fwd", grid=(n_seq, nq),
        in_specs=[pl.BlockSpec((tq, MLA_W), lambda b, i: (b * nq + i, 0)),
                  pl.BlockSpec((S, MLA_W), lambda b, i: (b, 0)), pl.BlockSpec((S, MLA_W), lambda b, i: (b, 0))],
        out_specs=[pl.BlockSpec((tq, MLA_W), lambda b, i: (b * nq + i, 0)),
                   pl.BlockSpec((tq, HEAD_PAD), lambda b, i: (b * nq + i, 0))],
        out_shape=[jax.ShapeDtypeStruct((T, MLA_W), BF16), jax.ShapeDtypeStruct((T, HEAD_PAD), F32)],
        scratch_shapes=[pltpu.VMEM((HEADS, tq, HEAD_PAD), F32), pltpu.VMEM((tq, MLA_W), F32), pltpu.VMEM((HEADS, tq, tq), F32),
                        pltpu.VMEM((HEADS, tq, tq), BF16), pltpu.VMEM((HEADS, tq, HEAD_PAD), F32)],
        compiler_params=_params(2),
    )(Q, K, V)


def _lane_group():
    return lax.broadcasted_iota(jnp.int32, (1, MIX_W), 1) >> GROUP_SHIFT


def _by_group(a0, a1, a2, a3):
    g = _lane_group()
    return jnp.where(g == 0, a0, jnp.where(g == 1, a1, jnp.where(g == 2, a2, a3)))


def _pool_count(si, tc, rows):
    pos = si * tc + lax.broadcasted_iota(jnp.int32, (rows, MIX_W), 0)
    win = _by_group(*POOL_WINDOWS)
    return jnp.minimum(pos + 1, win).astype(F32)


def _pool_fwd(z, z_prev, si, tc):
    ze = jnp.concatenate([z_prev, z], axis=0)
    s1 = ze + _shift_down(ze, 1)
    s2 = s1 + _shift_down(s1, 2)
    s4 = s2 + _shift_down(s2, 4)
    s8 = s4 + _shift_down(s4, 8)
    win_sum = _by_group(s1, s2, s4, s8)[POOL_HALO:]
    return win_sum / _pool_count(si, tc, tc) - z


def _sgu_weights(w_ref):
    row = lax.broadcasted_iota(jnp.int32, (SGU_BLOCK, SGU_BLOCK), 0)
    col = lax.broadcasted_iota(jnp.int32, (SGU_BLOCK, SGU_BLOCK), 1)
    keep = (row >> CHUNK_SHIFT) >= (col >> CHUNK_SHIFT)
    return keep, [jnp.where(keep, w_ref[g], 0.0).astype(BF16) for g in range(GROUPS)]


def _sgu_mix(vn_blk, wm, bias):
    g = _lane_group()
    mixed = bias
    for k in range(GROUPS):
        mixed = mixed + jnp.where(g == k, _dot(wm[k], vn_blk), 0.0)
    return mixed


def _conv_fwd(z, z_prev, w_ref):
    ze = jnp.concatenate([z_prev, z], axis=0)
    y = w_ref[0:1, :] * _shift_down(ze, 2) + w_ref[1:2, :] * _shift_down(ze, 1) + w_ref[2:3, :] * ze
    return y[CONV_HALO:]


def _mix_specs(T, D, tc):
    base = (4 * D + SEC_MLA) // MIX_W
    cur = lambda k: pl.BlockSpec((tc, MIX_W), lambda i: (i, base + k))
    prev = lambda k, halo: pl.BlockSpec((halo, MIX_W), lambda i: (jnp.maximum(i * (tc // halo) - 1, 0), base + k))
    nxt = lambda k, halo: pl.BlockSpec((halo, MIX_W), lambda i: (jnp.minimum((i + 1) * (tc // halo), T // halo - 1), base + k))
    return cur, prev, nxt


def _mixers_fwd(P, D, S, pool_bd, pool_scale, g_v, sgu_w, sgu_bias, conv_w):
    T = P.shape[0]
    tc = _tile(S, 512, SGU_BLOCK)
    n_si = S // tc
    cur, prev, _ = _mix_specs(T, D, tc)

    def body(z_ref, zp_ref, u_ref, v_ref, b_ref, c_ref, x_ref, cp_ref, xp_ref,
             pw_ref, ps_ref, gv_ref, sw_ref, sb_ref, cw_ref, ob_ref, oc_ref, od_ref):
        si = pl.program_id(0) % n_si
        first = si == 0
        z = z_ref[...]
        pooled = _pool_fwd(z, jnp.where(first, 0.0, zp_ref[...]), si, tc)
        ob_ref[...] = (_dot(pooled.astype(BF16), pw_ref[...]) * ps_ref[...]).astype(BF16)

        v = v_ref[...]
        vn = (v * _rms_r(v) * gv_ref[...]).astype(BF16)
        _, wm = _sgu_weights(sw_ref)
        for blk in range(tc // SGU_BLOCK):
            rows = slice(blk * SGU_BLOCK, (blk + 1) * SGU_BLOCK)
            oc_ref[rows, :] = (u_ref[rows, :] * _sgu_mix(vn[rows], wm, sb_ref[...])).astype(BF16)

        zc = c_ref[...] * x_ref[...]
        zc_prev = jnp.where(first, 0.0, cp_ref[...] * xp_ref[...])
        od_ref[...] = (b_ref[...] * _conv_fwd(zc, zc_prev, cw_ref)).astype(BF16)

    out = pl.BlockSpec((tc, MIX_W), lambda i: (i, 0))
    return pl.pallas_call(
        body, name="mixers_fwd", grid=(T // tc,),
        in_specs=[cur(0), prev(0, POOL_HALO), cur(1), cur(2), cur(3), cur(4), cur(5), prev(4, CONV_HALO), prev(5, CONV_HALO),
                  _full((MIX_W, MIX_W)), _full((1, MIX_W)), _full((1, MIX_W)), _full((GROUPS, SGU_BLOCK, SGU_BLOCK)),
                  _full((SGU_BLOCK, MIX_W)), _full((CONV_HALO, MIX_W))],
        out_specs=[out, out, out],
        out_shape=[jax.ShapeDtypeStruct((T, MIX_W), BF16)] * 3,
        compiler_params=_params(1),
    )(P, P, P, P, P, P, P, P, P, pool_bd, pool_scale, g_v, sgu_w, sgu_bias, conv_w)


def _merge_fwd(x, P, A, Bm, C, Dv, wa, wb, wc, wd, wout, g_post):
    T, D = x.shape
    tm = _tile(T, 256)

    def body(x_ref, lg_ref, a_ref, b_ref, c_ref, d_ref, wa_ref, wb_ref, wc_ref, wd_ref, wo_ref, g_ref,
             x1_ref, mg_ref, o_ref):
        merged = jnp.zeros((tm, D), F32)
        for k, (br, w) in enumerate(((a_ref, wa_ref), (b_ref, wb_ref), (c_ref, wc_ref), (d_ref, wd_ref))):
            merged = merged + _sigmoid(lg_ref[:, k * D:(k + 1) * D]) * _dot(br[...], w[...])
        mg = merged.astype(BF16)
        mg_ref[...] = mg
        o = _dot(mg, wo_ref[...])
        o_ref[...] = o
        x1_ref[...] = x_ref[...] + o * _rms_r(o) * g_ref[...]

    row = lambda w: pl.BlockSpec((tm, w), lambda i: (i, 0))
    return pl.pallas_call(
        body, name="merge_fwd", grid=(T // tm,),
        in_specs=[row(D), row(4 * D), row(MLA_W), row(MIX_W), row(MIX_W), row(MIX_W),
                  _full((MLA_W, D)), _full((MIX_W, D)), _full((MIX_W, D)), _full((MIX_W, D)), _full((D, D)), _full((1, D))],
        out_specs=[row(D), row(D), row(D)],
        out_shape=[jax.ShapeDtypeStruct((T, D), F32), jax.ShapeDtypeStruct((T, D), BF16), jax.ShapeDtypeStruct((T, D), F32)],
        compiler_params=_params(1),
    )(x, P, A, Bm, C, Dv, wa, wb, wc, wd, wout, g_post)


def _ffn_specs(T, D, Fc, l):
    tm = _tile(T, 512)
    row = pl.BlockSpec((tm, D), lambda i, j: (i, 0))
    col = pl.BlockSpec((None, tm, Fc), lambda i, j: (j, i, 0))
    w_in = pl.BlockSpec((None, None, D, Fc), lambda i, j: (l, j, 0, 0))
    w_out = pl.BlockSpec((None, None, Fc, D), lambda i, j: (l, j, 0, 0))
    return tm, row, col, w_in, w_out


def _ffn_fwd(x1, g_pre, wg, wu, wdn, g_post, l):
    T, D = x1.shape
    nf, Fc = wg.shape[1], wg.shape[3]
    tm, row, col, w_in, w_out = _ffn_specs(T, D, Fc, l)

    def body(x_ref, gp_ref, wg_ref, wu_ref, wd_ref, gq_ref, x2_ref, h_ref, gt_ref, up_ref, f_ref):
        j = pl.program_id(1)

        @pl.when(j == 0)
        def _():
            xv = x_ref[...]
            h_ref[...] = (xv * _rms_r(xv) * gp_ref[...]).astype(BF16)
            f_ref[...] = jnp.zeros_like(f_ref)

        h = h_ref[...]
        gt = _dot(h, wg_ref[...])
        up = _dot(h, wu_ref[...])
        gt_ref[...] = gt.astype(BF16)
        up_ref[...] = up.astype(BF16)
        a = (gt * _sigmoid(gt) * up).astype(BF16)
        f_ref[...] += _dot(a, wd_ref[...])

        @pl.when(j == nf - 1)
        def _():
            f = f_ref[...]
            x2_ref[...] = x_ref[...] + f * _rms_r(f) * gq_ref[...]

    return pl.pallas_call(
        body, name="ffn_fwd", grid=(T // tm, nf),
        in_specs=[row, _full((1, D)), w_in, w_in, w_out, _full((1, D))],
        out_specs=[row, row, col, col, row],
        out_shape=[jax.ShapeDtypeStruct((T, D), F32), jax.ShapeDtypeStruct((T, D), BF16),
                   jax.ShapeDtypeStruct((nf, T, Fc), BF16), jax.ShapeDtypeStruct((nf, T, Fc), BF16),
                   jax.ShapeDtypeStruct((T, D), F32)],
        compiler_params=_params(2),
    )(x1, g_pre, wg, wu, wdn, g_post)


def _loss_grad(y, target):
    T, D = y.shape
    tm = _tile(T, 512)

    def body(y_ref, t_ref, l_ref, dy_ref):
        @pl.when(pl.program_id(0) == 0)
        def _():
            l_ref[...] = jnp.zeros_like(l_ref)

        d = y_ref[...] - t_ref[...]
        dy_ref[...] = d * (1.0 / D)
        e = jnp.sum((d * d).reshape(tm // 8, 8, D), axis=0)
        part = e[:, 0:128]
        for k in range(1, D // 128):
            part = part + e[:, k * 128:(k + 1) * 128]
        l_ref[...] += part * (0.5 / D)

    row = pl.BlockSpec((tm, D), lambda i: (i, 0))
    return pl.pallas_call(
        body, name="loss_grad", grid=(T // tm,),
        in_specs=[row, row], out_specs=[_full((8, 128)), row],
        out_shape=[jax.ShapeDtypeStruct((8, 128), F32), jax.ShapeDtypeStruct((T, D), F32)],
        compiler_params=_params(1),
    )(y, target)


def _matmul_tn(a, b, name):
    T, M = a.shape
    N = b.shape[1]
    tm, tn, tk = _tile(M, 1024, 128), _tile(N, 1536, 128), _tile(T, 512)

    def body(a_ref, b_ref, o_ref):
        @pl.when(pl.program_id(2) == 0)
        def _():
            o_ref[...] = jnp.zeros_like(o_ref)

        o_ref[...] += _dot_tn(a_ref[...], b_ref[...])

    return pl.pallas_call(
        body, name=name, grid=(M // tm, N // tn, T // tk),
        in_specs=[pl.BlockSpec((tk, tm), lambda i, j, k: (k, i)), pl.BlockSpec((tk, tn), lambda i, j, k: (k, j))],
        out_specs=pl.BlockSpec((tm, tn), lambda i, j, k: (i, j)),
        out_shape=jax.ShapeDtypeStruct((M, N), F32),
        compiler_params=_params(3),
    )(a, b)


def _wgrad_chip(buf, l, n_layers, a, b, rows, cols, a_mode, b_mode, name):
    T = a.shape[-2]
    tk = _tile(T, 1024)

    def spec(mode, width):
        if mode == "all":
            return pl.BlockSpec((tk, width), lambda k, t: (t, 0))
        if mode == "cols":
            return pl.BlockSpec((tk, width), lambda k, t: (t, k))
        return pl.BlockSpec((None, tk, width), lambda k, t: (k, t, 0))

    def body(a_ref, b_ref, *rest):
        o_ref = rest[-1]

        @pl.when(pl.program_id(1) == 0)
        def _():
            o_ref[...] = jnp.zeros_like(o_ref)

        o_ref[...] += _dot_tn(a_ref[...], b_ref[...])

    keep = [] if buf is None else [buf]
    return pl.pallas_call(
        body, name=name, grid=(N_CHIPS, T // tk),
        in_specs=[spec(a_mode, rows), spec(b_mode, cols)] + [pl.BlockSpec(memory_space=pl.ANY)] * len(keep),
        out_specs=pl.BlockSpec((None, None, rows, cols), lambda k, t: (k, l, 0, 0)),
        out_shape=jax.ShapeDtypeStruct((N_CHIPS, n_layers, rows, cols), F32),
        input_output_aliases={2: 0} if keep else {},
        compiler_params=_params(2),
    )(a, b, *keep)


def _ffn_bwd(dx2, x1, f, gt, up, g_pre, wg, wu, wdn, g_post, l):
    T, D = x1.shape
    nf, Fc = wg.shape[1], wg.shape[3]
    tm, row, col, w_in, w_out = _ffn_specs(T, D, Fc, l)

    def body(dx2_ref, x1_ref, f_ref, gt_ref, up_ref, gp_ref, wg_ref, wu_ref, wd_ref, gq_ref,
             dx1_ref, df_ref, dgt_ref, dup_ref, a_ref, dgp_ref, dgq_ref, dh_acc):
        i, j = pl.program_id(0), pl.program_id(1)

        @pl.when((i == 0) & (j == 0))
        def _():
            dgp_ref[...] = jnp.zeros_like(dgp_ref)
            dgq_ref[...] = jnp.zeros_like(dgq_ref)

        @pl.when(j == 0)
        def _():
            df, dg = _rms_bwd(dx2_ref[...], f_ref[...], gq_ref[...])
            df_ref[...] = df.astype(BF16)
            dgq_ref[...] += dg
            dh_acc[...] = jnp.zeros_like(dh_acc)

        da = _dot_nt(df_ref[...], wd_ref[...])
        gt = gt_ref[...].astype(F32)
        up = up_ref[...].astype(F32)
        sig = _sigmoid(gt)
        silu = gt * sig
        dgt = (da * up * (sig * (1.0 + gt * (1.0 - sig)))).astype(BF16)
        dup = (da * silu).astype(BF16)
        dgt_ref[...] = dgt
        dup_ref[...] = dup
        a_ref[...] = (silu * up).astype(BF16)
        dh_acc[...] += _dot_nt(dgt, wg_ref[...]) + _dot_nt(dup, wu_ref[...])

        @pl.when(j == nf - 1)
        def _():
            dx, dg = _rms_bwd(dh_acc[...], x1_ref[...], gp_ref[...])
            dx1_ref[...] = dx2_ref[...] + dx
            dgp_ref[...] += dg

    return pl.pallas_call(
        body, name="ffn_bwd", grid=(T // tm, nf),
        in_specs=[row, row, row, col, col, _full((1, D)), w_in, w_in, w_out, _full((1, D))],
        out_specs=[row, row, col, col, col, _full((1, D)), _full((1, D))],
        out_shape=[jax.ShapeDtypeStruct((T, D), F32), jax.ShapeDtypeStruct((T, D), BF16)]
        + [jax.ShapeDtypeStruct((nf, T, Fc), BF16)] * 3
        + [jax.ShapeDtypeStruct((1, D), F32), jax.ShapeDtypeStruct((1, D), F32)],
        scratch_shapes=[pltpu.VMEM((tm, D), F32)],
        compiler_params=_params(2),
    )(dx2, x1, f, gt, up, g_pre, wg, wu, wdn, g_post)


def _merge_bwd(dx1, o, P, A, Bm, C, Dv, wa, wb, wc, wd, wout, g_post):
    T, D = o.shape
    tm = _tile(T, 256)

    def body(dx1_ref, o_ref, lg_ref, a_ref, b_ref, c_ref, d_ref, wa_ref, wb_ref, wc_ref, wd_ref, wo_ref, g_ref,
             do_ref, dlg_ref, dya_ref, dyb_ref, dyc_ref, dyd_ref, da_ref, db_ref, dc_ref, dd_ref, dg_ref):
        @pl.when(pl.program_id(0) == 0)
        def _():
            dg_ref[...] = jnp.zeros_like(dg_ref)

        d_o, dg = _rms_bwd(dx1_ref[...], o_ref[...], g_ref[...])
        dg_ref[...] += dg
        d_o = d_o.astype(BF16)
        do_ref[...] = d_o
        dm = _dot_nt(d_o, wo_ref[...])
        branches = ((a_ref, wa_ref, dya_ref, da_ref), (b_ref, wb_ref, dyb_ref, db_ref),
                    (c_ref, wc_ref, dyc_ref, dc_ref), (d_ref, wd_ref, dyd_ref, dd_ref))
        for k, (br, w, dy_ref, dbr_ref) in enumerate(branches):
            gate = _sigmoid(lg_ref[:, k * D:(k + 1) * D])
            y = _dot(br[...], w[...])
            dlg_ref[:, k * D:(k + 1) * D] = (dm * y * gate * (1.0 - gate)).astype(BF16)
            dy = (dm * gate).astype(BF16)
            dy_ref[...] = dy
            dbr_ref[...] = _dot_nt(dy, w[...]).astype(dbr_ref.dtype)

    row = lambda w: pl.BlockSpec((tm, w), lambda i: (i, 0))
    return pl.pallas_call(
        body, name="merge_bwd", grid=(T // tm,),
        in_specs=[row(D), row(D), row(4 * D), row(MLA_W), row(MIX_W), row(MIX_W), row(MIX_W),
                  _full((MLA_W, D)), _full((MIX_W, D)), _full((MIX_W, D)), _full((MIX_W, D)), _full((D, D)), _full((1, D))],
        out_specs=[row(D), row(4 * D), row(D), row(D), row(D), row(D), row(MLA_W), row(MIX_W), row(MIX_W), row(MIX_W),
                   _full((1, D))],
        out_shape=[jax.ShapeDtypeStruct((T, D), BF16), jax.ShapeDtypeStruct((T, 4 * D), BF16)]
        + [jax.ShapeDtypeStruct((T, D), BF16)] * 4
        + [jax.ShapeDtypeStruct((T, MLA_W), BF16)] + [jax.ShapeDtypeStruct((T, MIX_W), F32)] * 3
        + [jax.ShapeDtypeStruct((1, D), F32)],
        compiler_params=_params(1),
    )(dx1, o, P, A, Bm, C, Dv, wa, wb, wc, wd, wout, g_post)


def _mixers_bwd(P, D, S, dBm, dC, dDv, pool_bd, pool_scale, g_v, sgu_w, sgu_bias, conv_w):
    T = P.shape[0]
    tc = _tile(S, 512, SGU_BLOCK)
    n_si = S // tc
    cur, prev, nxt = _mix_specs(T, D, tc)
    n_blk = tc // SGU_BLOCK

    def body(z_ref, zp_ref, u_ref, v_ref, b_ref, c_ref, x_ref, cp_ref, xp_ref, bn_ref,
             dbm_ref, dbmn_ref, dc_ref, ddv_ref, ddvn_ref,
             pw_ref, ps_ref, gv_ref, sw_ref, sb_ref, cw_ref,
             dp_ref, dpw_ref, dps_ref, dgv_ref, dsw_ref, dsb_ref, dcw_ref, dvn_acc):
        si = pl.program_id(0) % n_si
        first, last = si == 0, si == n_si - 1

        @pl.when(pl.program_id(0) == 0)
        def _():
            for r in (dpw_ref, dps_ref, dgv_ref, dsw_ref, dsb_ref, dcw_ref):
                r[...] = jnp.zeros_like(r)

        z = z_ref[...]
        pooled = _pool_fwd(z, jnp.where(first, 0.0, zp_ref[...]), si, tc).astype(BF16)
        dbm = dbm_ref[...]
        dps_ref[...] += jnp.sum(dbm * _dot(pooled, pw_ref[...]), axis=0, keepdims=True)
        dmix = (jnp.concatenate([dbm, jnp.where(last, 0.0, dbmn_ref[...])], axis=0) * ps_ref[...]).astype(BF16)
        dpw_ref[...] += _dot_tn(pooled, dmix[:tc])
        dpool = _dot_nt(dmix, pw_ref[...])
        e = dpool / _pool_count(si, tc, tc + POOL_HALO)
        f1 = e + _shift_up(e, 1)
        f2 = f1 + _shift_up(f1, 2)
        f4 = f2 + _shift_up(f2, 4)
        f8 = f4 + _shift_up(f4, 8)
        dp_ref[:, 0:MIX_W] = (_by_group(f1, f2, f4, f8)[:tc] - dpool[:tc]).astype(BF16)

        v = v_ref[...]
        vn = (v * _rms_r(v) * gv_ref[...]).astype(BF16)
        keep, wm = _sgu_weights(sw_ref)
        g = _lane_group()
        for blk in range(n_blk):
            rows = slice(blk * SGU_BLOCK, (blk + 1) * SGU_BLOCK)
            vb = vn[rows]
            dc = dc_ref[rows, :]
            dp_ref[rows, MIX_W:2 * MIX_W] = (dc * _sgu_mix(vb, wm, sb_ref[...])).astype(BF16)
            dmx = dc * u_ref[rows, :]
            dsb_ref[...] += dmx
            dvn = jnp.zeros((SGU_BLOCK, MIX_W), F32)
            for k in range(GROUPS):
                dmk = jnp.where(g == k, dmx, 0.0).astype(BF16)
                dsw_ref[k] += jnp.where(keep, _dot_nt(dmk, vb), 0.0)
                dvn = dvn + _dot_tn(wm[k], dmk)
            dvn_acc[rows, :] = dvn
        dv, dg = _rms_bwd(dvn_acc[...], v, gv_ref[...])
        dgv_ref[...] += dg
        dp_ref[:, 2 * MIX_W:3 * MIX_W] = dv.astype(BF16)

        cg, xg, bg = c_ref[...], x_ref[...], b_ref[...]
        zc = cg * xg
        ze = jnp.concatenate([jnp.where(first, 0.0, cp_ref[...] * xp_ref[...]), zc], axis=0)
        z1, z2 = _shift_down(ze, 1)[CONV_HALO:], _shift_down(ze, 2)[CONV_HALO:]
        ddv = ddv_ref[...]
        y = cw_ref[0:1, :] * z2 + cw_ref[1:2, :] * z1 + cw_ref[2:3, :] * zc
        dp_ref[:, 3 * MIX_W:4 * MIX_W] = (ddv * y).astype(BF16)
        dy = ddv * bg
        dcw_ref[0:1, :] += jnp.sum(dy * z2, axis=0, keepdims=True)
        dcw_ref[1:2, :] += jnp.sum(dy * z1, axis=0, keepdims=True)
        dcw_ref[2:3, :] += jnp.sum(dy * zc, axis=0, keepdims=True)
        dye = jnp.concatenate([dy, jnp.where(last, 0.0, ddvn_ref[...] * bn_ref[...])], axis=0)
        dz = (cw_ref[2:3, :] * dye + cw_ref[1:2, :] * _shift_up(dye, 1) + cw_ref[0:1, :] * _shift_up(dye, 2))[:tc]
        dp_ref[:, 4 * MIX_W:5 * MIX_W] = (dz * xg).astype(BF16)
        dp_ref[:, 5 * MIX_W:6 * MIX_W] = (dz * cg).astype(BF16)

    grad = lambda halo: pl.BlockSpec((halo, MIX_W), lambda i: (jnp.minimum((i + 1) * (tc // halo), T // halo - 1), 0))
    out = pl.BlockSpec((tc, MIX_W), lambda i: (i, 0))
    return pl.pallas_call(
        body, name="mixers_bwd", grid=(T // tc,),
        in_specs=[cur(0), prev(0, POOL_HALO), cur(1), cur(2), cur(3), cur(4), cur(5), prev(4, CONV_HALO), prev(5, CONV_HALO),
                  nxt(3, CONV_HALO), out, grad(POOL_HALO), out, out, grad(CONV_HALO),
                  _full((MIX_W, MIX_W)), _full((1, MIX_W)), _full((1, MIX_W)), _full((GROUPS, SGU_BLOCK, SGU_BLOCK)),
                  _full((SGU_BLOCK, MIX_W)), _full((CONV_HALO, MIX_W))],
        out_specs=[pl.BlockSpec((tc, SEC_MIX), lambda i: (i, 0)), _full((MIX_W, MIX_W)), _full((1, MIX_W)), _full((1, MIX_W)),
                   _full((GROUPS, SGU_BLOCK, SGU_BLOCK)), _full((SGU_BLOCK, MIX_W)), _full((CONV_HALO, MIX_W))],
        out_shape=[jax.ShapeDtypeStruct((T, SEC_MIX), BF16), jax.ShapeDtypeStruct((MIX_W, MIX_W), F32),
                   jax.ShapeDtypeStruct((1, MIX_W), F32), jax.ShapeDtypeStruct((1, MIX_W), F32),
                   jax.ShapeDtypeStruct((GROUPS, SGU_BLOCK, SGU_BLOCK), F32), jax.ShapeDtypeStruct((SGU_BLOCK, MIX_W), F32),
                   jax.ShapeDtypeStruct((CONV_HALO, MIX_W), F32)],
        scratch_shapes=[pltpu.VMEM((tc, MIX_W), F32)],
        compiler_params=_params(1),
    )(P, P, P, P, P, P, P, P, P, P, dBm, dBm, dC, dDv, dDv, pool_bd, pool_scale, g_v, sgu_w, sgu_bias, conv_w)


def _attn_tile(S):
    return _tile(S, 256, 128)


def _attn_stats(O, dO, lse, S):
    T = O.shape[0]
    tq = _attn_tile(S)

    def body(o_ref, do_ref, lse_ref, lt_ref, dt_ref):
        lane = lax.broadcasted_iota(jnp.int32, (tq, HEAD_PAD), 1)
        delta = jnp.zeros((tq, HEAD_PAD), F32)
        for h in range(HEADS):
            hs = slice(h * HEAD_PAD, (h + 1) * HEAD_PAD)
            s = jnp.sum(o_ref[:, hs].astype(F32) * do_ref[:, hs].astype(F32), axis=-1, keepdims=True)
            delta = jnp.where(lane == h, s, delta)
        delta_t, lse_t = delta.T, lse_ref[...].T
        for h in range(HEADS):
            lt_ref[h, 0] = lse_t[h:h + 1, :]
            dt_ref[h, 0] = delta_t[h:h + 1, :]

    row = lambda w: pl.BlockSpec((tq, w), lambda i: (i, 0))
    out = pl.BlockSpec((HEADS, 1, 1, tq), lambda i: (0, i, 0, 0))
    return pl.pallas_call(
        body, name="attn_stats", grid=(T // tq,), in_specs=[row(MLA_W), row(MLA_W), row(HEAD_PAD)], out_specs=[out, out],
        out_shape=[jax.ShapeDtypeStruct((HEADS, T // tq, 1, tq), F32)] * 2, compiler_params=_params(1),
    )(O, dO, lse)


def _flash_bwd(Q, K, V, dO, lse_t, delta_t, S):
    T = Q.shape[0]
    n_seq = T // S
    tq = _attn_tile(S)
    nq = S // tq

    def body(k_ref, v_ref, q_ref, do_ref, lse_ref, dl_ref, dq_ref, dk_ref, dv_ref, s_s, dp_s, p_s, ds_s):
        kb = pl.program_id(1)

        @pl.when(kb == 0)
        def _():
            dq_ref[...] = jnp.zeros_like(dq_ref)

        dk_ref[...] = jnp.zeros_like(dk_ref)
        dv_ref[...] = jnp.zeros_like(dv_ref)

        def block(qi, masked):
            rows = pl.ds(pl.multiple_of(qi * tq, tq), tq)
            for h in range(HEADS):
                hs = slice(h * HEAD_PAD, (h + 1) * HEAD_PAD)
                s_s[h] = _dot_nt(k_ref[:, hs], q_ref[rows, hs])
                dp_s[h] = _dot_nt(v_ref[:, hs], do_ref[rows, hs])
            for h in range(HEADS):
                lse_row, dl_row = lse_ref[h, qi], dl_ref[h, qi]
                for r in range(0, tq, SOFTMAX_ROWS):
                    rs = slice(r, r + SOFTMAX_ROWS)
                    s = s_s[h, rs, :]
                    if masked:
                        key = r + lax.broadcasted_iota(jnp.int32, (SOFTMAX_ROWS, tq), 0)
                        query = lax.broadcasted_iota(jnp.int32, (SOFTMAX_ROWS, tq), 1)
                        s = jnp.where((query >> CHUNK_SHIFT) >= (key >> CHUNK_SHIFT), s, NEG_INF)
                    p = jnp.exp2(s - lse_row)
                    p_s[h, rs, :] = p.astype(BF16)
                    ds_s[h, rs, :] = (p * (dp_s[h, rs, :] - dl_row)).astype(BF16)
            for h in range(HEADS):
                hs = slice(h * HEAD_PAD, (h + 1) * HEAD_PAD)
                dv_ref[:, hs] += _dot(p_s[h], do_ref[rows, hs])
                dk_ref[:, hs] += _dot(ds_s[h], q_ref[rows, hs])
                dq_ref[rows, hs] += _dot_tn(ds_s[h], k_ref[:, hs])

        def full_block(qi, carry):
            block(qi, False)
            return carry

        block(kb, True)
        lax.fori_loop(kb + 1, nq, full_block, 0)
        dk_ref[...] = dk_ref[...] * LN2

    tile = pl.BlockSpec((tq, MLA_W), lambda b, i: (b * nq + i, 0))
    seq = pl.BlockSpec((S, MLA_W), lambda b, i: (b, 0))
    stat = pl.BlockSpec((HEADS, nq, 1, tq), lambda b, i: (0, b, 0, 0))
    return pl.pallas_call(
        body, name="flash_bwd", grid=(n_seq, nq),
        in_specs=[tile, tile, seq, seq, stat, stat],
        out_specs=[seq, tile, tile],
        out_shape=[jax.ShapeDtypeStruct((T, MLA_W), F32)] * 3,
        scratch_shapes=[pltpu.VMEM((HEADS, tq, tq), F32), pltpu.VMEM((HEADS, tq, tq), F32),
                        pltpu.VMEM((HEADS, tq, tq), BF16), pltpu.VMEM((HEADS, tq, tq), BF16)],
        compiler_params=_params(2),
    )(K, V, Q, dO, lse_t, delta_t)


def _mla_bwd_post(P, D, S, dQ, dK, dV, g_cq, g_ckv, wuq, wukv, rope_c, rope_sa, rope_sb):
    T = P.shape[0]
    tm = _tile(S, 512)
    n_si = S // tm
    base = 4 * D

    def body(cq_ref, ckv_ref, dq_ref, dk_ref, dv_ref, gq_ref, gkv_ref, wq_ref, wkv_ref, c_ref, sa_ref, sb_ref,
             dp_ref, dqo_ref, dkvo_ref, dgq_ref, dgkv_ref):
        @pl.when(pl.program_id(0) == 0)
        def _():
            dgq_ref[...] = jnp.zeros_like(dgq_ref)
            dgkv_ref[...] = jnp.zeros_like(dgkv_ref)

        c, sa, sb = c_ref[...], sa_ref[...], sb_ref[...]
        dq = _rope_t(dq_ref[...] * ATTN_SCALE, jnp.tile(c, (1, HEADS)), jnp.tile(sa, (1, HEADS)),
                     jnp.tile(sb, (1, HEADS))).astype(BF16)
        dqo_ref[...] = dq
        dcq, dg = _rms_bwd(_dot_nt(dq, wq_ref[...]), cq_ref[...], gq_ref[...])
        dgq_ref[...] += dg
        dp_ref[:, 0:Q_LORA] = dcq.astype(BF16)

        dk = dk_ref[...]
        dkb, dvb = dk.astype(BF16), dv_ref[...].astype(BF16)
        dkv = jnp.concatenate([p[:, j * CHIP_HEADS_W:(j + 1) * CHIP_HEADS_W] for j in range(N_CHIPS) for p in (dkb, dvb)], axis=1)
        dkvo_ref[...] = dkv
        dckv, dg = _rms_bwd(_dot_nt(dkv, wkv_ref[...]), ckv_ref[...], gkv_ref[...])
        dgkv_ref[...] += dg
        dp_ref[:, Q_LORA:Q_LORA + KV_LORA] = dckv.astype(BF16)

        dkr = dk[:, 0:HEAD_PAD]
        for h in range(1, HEADS):
            dkr = dkr + dk[:, h * HEAD_PAD:(h + 1) * HEAD_PAD]
        lane = lax.broadcasted_iota(jnp.int32, (1, HEAD_PAD), 1)
        rope_lanes = (lane >= QK_NOPE) & (lane < QK_NOPE + QK_ROPE)
        dp_ref[:, Q_LORA + KV_LORA:SEC_MLA] = jnp.where(rope_lanes, _rope_t(dkr, c, sa, sb), 0.0).astype(BF16)

    tab = pl.BlockSpec((tm, HEAD_PAD), lambda i: (i % n_si, 0))
    row = lambda w: pl.BlockSpec((tm, w), lambda i: (i, 0))
    return pl.pallas_call(
        body, name="mla_bwd_post", grid=(T // tm,),
        in_specs=[pl.BlockSpec((tm, Q_LORA), lambda i: (i, base // Q_LORA)),
                  pl.BlockSpec((tm, KV_LORA), lambda i: (i, (base + Q_LORA) // KV_LORA)),
                  row(MLA_W), row(MLA_W), row(MLA_W),
                  _full((1, Q_LORA)), _full((1, KV_LORA)), _full((Q_LORA, MLA_W)), _full((KV_LORA, 2 * MLA_W)), tab, tab, tab],
        out_specs=[row(SEC_MLA), row(MLA_W), row(2 * MLA_W), _full((1, Q_LORA)), _full((1, KV_LORA))],
        out_shape=[jax.ShapeDtypeStruct((T, SEC_MLA), BF16), jax.ShapeDtypeStruct((T, MLA_W), BF16),
                   jax.ShapeDtypeStruct((T, 2 * MLA_W), BF16), jax.ShapeDtypeStruct((1, Q_LORA), F32),
                   jax.ShapeDtypeStruct((1, KV_LORA), F32)],
        compiler_params=_params(1),
    )(P, P, dQ, dK, dV, g_cq, g_ckv, wuq, wukv, rope_c, rope_sa, rope_sb)


def _proj_bwd(dx1, x, g, dPg, dPa, dPm, w_gates, w_mla, w_mix):
    T, D = x.shape
    tm = _tile(T, 256)

    def body(dx1_ref, x_ref, g_ref, dg_ref_in, da_ref, dm_ref, wg_ref, wa_ref, wm_ref, dx_ref, dg_ref):
        @pl.when(pl.program_id(0) == 0)
        def _():
            dg_ref[...] = jnp.zeros_like(dg_ref)

        dh = _dot_nt(dg_ref_in[...], wg_ref[...]) + _dot_nt(da_ref[...], wa_ref[...]) + _dot_nt(dm_ref[...], wm_ref[...])
        dx, dg = _rms_bwd(dh, x_ref[...], g_ref[...])
        dx_ref[...] = dx1_ref[...] + dx
        dg_ref[...] += dg

    row = lambda w: pl.BlockSpec((tm, w), lambda i: (i, 0))
    return pl.pallas_call(
        body, name="proj_bwd", grid=(T // tm,),
        in_specs=[row(D), row(D), _full((1, D)), row(4 * D), row(SEC_MLA), row(SEC_MIX),
                  _full((D, 4 * D)), _full((D, SEC_MLA)), _full((D, SEC_MIX))],
        out_specs=[row(D), _full((1, D))],
        out_shape=[jax.ShapeDtypeStruct((T, D), F32), jax.ShapeDtypeStruct((1, D), F32)],
        compiler_params=_params(1),
    )(dx1, x, g, dPg, dPa, dPm, w_gates, w_mla, w_mix)


def _adamw(w, g, m, v, name):
    R, C = w.shape
    tr = _tile(R, max(8, (1 << 19) // C))

    def body(w_ref, g_ref, m_ref, v_ref, d_ref, mo_ref, vo_ref):
        gv = g_ref[...]
        mn = ADAM_B1 * m_ref[...] + (1.0 - ADAM_B1) * gv
        vn = ADAM_B2 * v_ref[...] + (1.0 - ADAM_B2) * (gv * gv)
        mo_ref[...] = mn
        vo_ref[...] = vn
        m_hat = mn / (1.0 - ADAM_B1 ** ADAM_STEP)
        v_hat = vn / (1.0 - ADAM_B2 ** ADAM_STEP)
        d_ref[...] = -ADAM_LR * (m_hat / (jnp.sqrt(v_hat) + ADAM_EPS) + ADAM_WD * w_ref[...])

    blk = pl.BlockSpec((tr, C), lambda i: (i, 0))
    return pl.pallas_call(
        body, name=name, grid=(R // tr,), in_specs=[blk] * 4, out_specs=[blk] * 3,
        out_shape=[jax.ShapeDtypeStruct((R, C), F32)] * 3, compiler_params=_params(1),
    )(w, g, m, v)


def _rows_tile(rows, cols):
    return _tile(rows, max(16, (1 << 19) // cols), 16)


def _add_halves(G, recv, half, name):
    n, L, R, C = G.shape
    hr = R // 2
    tr = _rows_tile(hr, C)
    nb = hr // tr

    def body(half_ref, g_ref, r_ref, o_ref):
        o_ref[...] = (g_ref[...] + r_ref[...]).astype(BF16)

    grid_spec = pltpu.PrefetchScalarGridSpec(
        num_scalar_prefetch=1, grid=(n * L, nb),
        in_specs=[pl.BlockSpec((1, tr, C), lambda k, i, h: (k, h[0] * nb + i, 0)),
                  pl.BlockSpec((1, tr, C), lambda k, i, h: (k, i, 0))],
        out_specs=pl.BlockSpec((1, tr, C), lambda k, i, h: (k, i, 0)))
    out = pl.pallas_call(
        body, name="rs_add_halves_" + name, grid_spec=grid_spec,
        out_shape=jax.ShapeDtypeStruct((n * L, hr, C), BF16), compiler_params=_params(2),
    )(half.reshape(1).astype(jnp.int32), G.reshape(n * L, R, C), recv.reshape(n * L, hr, C))
    return out.reshape(n, L, hr, C)


def _sum_slots(H, slots, place, name):
    n, L, hr, C = slots.shape
    tr = _rows_tile(hr, C)
    nb = hr // tr

    def body(x_ref, y_ref, c_ref, own_ref, s1_ref, s2_ref, s3_ref, o_ref):
        o_ref[...] = ((own_ref[...].astype(F32) + s1_ref[...].astype(F32)) + s2_ref[...].astype(F32)) + s3_ref[...].astype(F32)

    def src(fx, fy):
        def index(l, j, px, py, pc):
            cx = px[0] + fx - 2 * fx * px[0]
            cy = py[0] + fy - 2 * fy * py[0]
            return (2 * cx + cy, l, j, 0)
        return pl.BlockSpec((None, None, tr, C), index)

    grid_spec = pltpu.PrefetchScalarGridSpec(
        num_scalar_prefetch=3, grid=(L, nb), in_specs=[src(0, 0), src(0, 1), src(1, 0), src(1, 1)],
        out_specs=pl.BlockSpec((None, tr, C), lambda l, j, px, py, pc: (l, pc[0] * nb + j, 0)))
    return pl.pallas_call(
        body, name="rs_sum_slots_" + name, grid_spec=grid_spec,
        out_shape=jax.ShapeDtypeStruct((L, 2 * hr, C), F32), compiler_params=_params(2),
    )(*place, H, slots, slots, slots)


HBM = pl.BlockSpec(memory_space=pltpu.HBM)


def _place():
    x, y, c = lax.axis_index("x"), lax.axis_index("y"), lax.axis_index("c")
    return x, y, c, 2 * x + y


def _chip_device(chip, c):
    return (chip // 2, chip % 2, c)


def _remote(src, dst, send_sem, recv_sem, to):
    return pltpu.make_async_remote_copy(src_ref=src, dst_ref=dst, send_sem=send_sem, recv_sem=recv_sem, device_id=to,
                                        device_id_type=MESH)


def _place_own(w, chip, name):
    L, R, C = w.shape
    tr = _rows_tile(R, C)

    def body(p_ref, w_ref, o_ref):
        o_ref[...] = w_ref[...].astype(BF16)

    grid_spec = pltpu.PrefetchScalarGridSpec(
        num_scalar_prefetch=1, grid=(L, R // tr), in_specs=[pl.BlockSpec((None, tr, C), lambda l, j, p: (l, j, 0))],
        out_specs=pl.BlockSpec((None, None, tr, C), lambda l, j, p: (l, p[0], j, 0)))
    return pl.pallas_call(
        body, name="place_" + name, grid_spec=grid_spec,
        out_shape=jax.ShapeDtypeStruct((L, N_CHIPS, R, C), BF16), compiler_params=_params(2),
    )(chip.reshape(1).astype(jnp.int32), w)


def _gather_weights(bufs):
    n = len(bufs)

    def body(*refs):
        o_refs = refs[n:2 * n]
        send_sems, recv_sems = refs[2 * n:]
        x, y, c, me = _place()
        sibling = (x, y, 1 - c)

        def copy(t, k, chip, half, to):
            hr = o_refs[t].shape[2] // 2
            block = o_refs[t].at[:, chip, pl.ds(half * hr, hr), :]
            return _remote(block, block, send_sems.at[6 * t + k], recv_sems.at[6 * t + k], to)

        first = [copy(t, d - 1, me, c, _chip_device(me ^ d, c)) for t in range(n) for d in (1, 2, 3)]
        for cp in first:
            cp.start()
        passed = []
        for t in range(n):
            for d in (1, 2, 3):
                copy(t, d - 1, me ^ d, c, sibling).wait_recv()
                passed.append(copy(t, 2 + d, me ^ d, c, sibling))
                passed[-1].start()
        for t in range(n):
            for d in (1, 2, 3):
                copy(t, 2 + d, me ^ d, 1 - c, sibling).wait_recv()
        for cp in first + passed:
            cp.wait_send()

    return pl.pallas_call(
        body, name="gather_weights", in_specs=[HBM] * n, out_specs=[HBM] * n,
        out_shape=[jax.ShapeDtypeStruct(b.shape, b.dtype) for b in bufs],
        input_output_aliases={t: t for t in range(n)},
        scratch_shapes=[pltpu.SemaphoreType.DMA((6 * n,)), pltpu.SemaphoreType.DMA((6 * n,))],
    )(*bufs)


def _exchange_halves(Gs):
    n = len(Gs)

    def body(*refs):
        g_refs, o_refs = refs[:n], refs[n:2 * n]
        send_sems, recv_sems = refs[2 * n:]
        x, y, c, _ = _place()
        copies = []
        for t in range(n):
            hr = g_refs[t].shape[2] // 2
            copies.append(_remote(g_refs[t].at[:, :, pl.ds((1 - c) * hr, hr), :], o_refs[t], send_sems.at[t], recv_sems.at[t],
                                  (x, y, 1 - c)))
            copies[-1].start()
        for cp in copies:
            cp.wait()

    return pl.pallas_call(
        body, name="rs_exchange_halves", in_specs=[HBM] * n, out_specs=[HBM] * n,
        out_shape=[jax.ShapeDtypeStruct(g.shape[:2] + (g.shape[2] // 2, g.shape[3]), g.dtype) for g in Gs],
        scratch_shapes=[pltpu.SemaphoreType.DMA((n,)), pltpu.SemaphoreType.DMA((n,))],
    )(*Gs)


def _scatter_partials(Hs):
    n = len(Hs)

    def body(*refs):
        h_refs, o_refs = refs[:n], refs[n:2 * n]
        send_sems, recv_sems = refs[2 * n:]
        x, y, c, me = _place()
        sends = [_remote(h_refs[t].at[me ^ d], o_refs[t].at[me], send_sems.at[3 * t + d - 1], recv_sems.at[3 * t + d - 1],
                         _chip_device(me ^ d, c)) for t in range(n) for d in (1, 2, 3)]
        for cp in sends:
            cp.start()
        for t in range(n):
            for d in (1, 2, 3):
                _remote(h_refs[t].at[me ^ d], o_refs[t].at[me ^ d], send_sems.at[3 * t + d - 1], recv_sems.at[3 * t + d - 1],
                        _chip_device(me ^ d, c)).wait_recv()
        for cp in sends:
            cp.wait_send()

    return pl.pallas_call(
        body, name="rs_scatter_partials", in_specs=[HBM] * n, out_specs=[HBM] * n,
        out_shape=[jax.ShapeDtypeStruct(h.shape, h.dtype) for h in Hs],
        scratch_shapes=[pltpu.SemaphoreType.DMA((3 * n,)), pltpu.SemaphoreType.DMA((3 * n,))],
    )(*Hs)


def _join_halves(bufs):
    n = len(bufs)

    def body(*refs):
        o_refs = refs[n:2 * n]
        send_sems, recv_sems = refs[2 * n:]
        x, y, c, _ = _place()

        def half(t, which):
            hr = o_refs[t].shape[1] // 2
            return o_refs[t].at[:, pl.ds(which * hr, hr), :]

        sends = [_remote(half(t, c), half(t, c), send_sems.at[t], recv_sems.at[t], (x, y, 1 - c)) for t in range(n)]
        for cp in sends:
            cp.start()
        for t in range(n):
            _remote(half(t, 1 - c), half(t, 1 - c), send_sems.at[t], recv_sems.at[t], (x, y, 1 - c)).wait_recv()
        for cp in sends:
            cp.wait_send()

    return pl.pallas_call(
        body, name="rs_join_halves", in_specs=[HBM] * n, out_specs=[HBM] * n,
        out_shape=[jax.ShapeDtypeStruct(b.shape, b.dtype) for b in bufs],
        input_output_aliases={t: t for t in range(n)},
        scratch_shapes=[pltpu.SemaphoreType.DMA((n,)), pltpu.SemaphoreType.DMA((n,))],
    )(*bufs)


def _all_reduce_small(v, name):
    R, C = v.shape

    def body(v_ref, o_ref, slots, send_sems, recv_sems):
        x, y, c, _ = _place()
        me = 4 * x + 2 * y + c
        slots[me] = v_ref[...]
        sends = []
        for d in range(1, 8):
            peer = me ^ d
            sends.append(pltpu.make_async_remote_copy(
                src_ref=v_ref, dst_ref=slots.at[me], send_sem=send_sems.at[d - 1], recv_sem=recv_sems.at[d - 1],
                device_id=(peer // 4, (peer // 2) % 2, peer % 2), device_id_type=MESH))
        for cp in sends:
            cp.start()
        for d in range(1, 8):
            peer = me ^ d
            pltpu.make_async_remote_copy(
                src_ref=v_ref, dst_ref=slots.at[peer], send_sem=send_sems.at[d - 1], recv_sem=recv_sems.at[d - 1],
                device_id=(peer // 4, (peer // 2) % 2, peer % 2), device_id_type=MESH).wait_recv()
        for cp in sends:
            cp.wait_send()
        acc = slots[0]
        for k in range(1, 8):
            acc = acc + slots[k]
        o_ref[...] = acc

    vm = pl.BlockSpec(memory_space=pltpu.VMEM)
    return pl.pallas_call(
        body, name=name, in_specs=[vm], out_specs=vm, out_shape=jax.ShapeDtypeStruct((R, C), F32),
        scratch_shapes=[pltpu.VMEM((8, R, C), F32), pltpu.SemaphoreType.DMA((7,)), pltpu.SemaphoreType.DMA((7,))],
    )(v)


SHARDED = ("w_in", "w_uq", "w_ukv", "conv_w", "w_br_a", "w_br_b", "w_br_c", "w_br_d", "w_out", "w_ffn_gate", "w_ffn_up",
           "w_ffn_down")
ROW_SHARDED = ("w_out", "w_ffn_down")
REPLICATED = ("g_pre_mix", "g_cq", "g_ckv", "pool_w", "pool_scale", "g_sgu_v", "sgu_w", "sgu_b", "g_post_mix", "g_pre_ffn",
              "g_post_ffn")
WEIGHTS = ("w_in", "g_pre_mix", "g_cq", "g_ckv", "w_uq", "w_ukv", "pool_w", "pool_scale", "g_sgu_v", "sgu_w", "sgu_b",
           "conv_w", "w_br_a", "w_br_b", "w_br_c", "w_br_d", "w_out", "g_post_mix", "g_pre_ffn", "w_ffn_gate", "w_ffn_up",
           "w_ffn_down", "g_post_ffn")
GATHERED = tuple(n for n in SHARDED if n != "conv_w")


def _unpack(packed, shapes):
    flat = packed.reshape(-1)
    out, o = [], 0
    for s in shapes:
        n = int(np.prod(s))
        out.append(flat[o:o + n].reshape(s))
        o += n
    return out


def _join_cols(g, l):
    return jnp.concatenate([g[l, k] for k in range(N_CHIPS)], axis=1)


def _pad_heads(w, real):
    lead = w.shape[:-1]
    w = w.reshape(lead + (HEADS, real))
    return jnp.pad(w, [(0, 0)] * len(lead) + [(0, 0), (0, HEAD_PAD - real)]).reshape(lead + (MLA_W,))


def _unpad_heads(w, real):
    lead = w.shape[:-1]
    return w.reshape(lead + (HEADS, HEAD_PAD))[..., :real].reshape(lead + (HEADS * real,))


IN_OFFSETS = {"cq": 0, "ckv": Q_LORA, "kr": Q_LORA + KV_LORA, "mix": Q_LORA + KV_LORA + QK_ROPE}
IN_GATES = Q_LORA + KV_LORA + QK_ROPE + SEC_MIX


def _pad_w_in(w):
    K = w.shape[0]
    z = lambda n: jnp.zeros((K, n), w.dtype)
    return jnp.concatenate([w[:, IN_GATES:], w[:, :IN_OFFSETS["kr"]], z(QK_NOPE), w[:, IN_OFFSETS["kr"]:IN_OFFSETS["mix"]],
                            z(HEAD_PAD - QK_NOPE - QK_ROPE), w[:, IN_OFFSETS["mix"]:IN_GATES]], axis=1)


def _unpad_w_in(d_gates, d_mla, d_mix):
    kr = d_mla[:, Q_LORA + KV_LORA + QK_NOPE:Q_LORA + KV_LORA + QK_NOPE + QK_ROPE]
    return jnp.concatenate([d_mla[:, :Q_LORA + KV_LORA], kr, d_mix, d_gates], axis=1)


def _rope_tables(S):
    half = QK_ROPE // 2
    inv = ROPE_THETA ** (-jnp.arange(0, QK_ROPE, 2, dtype=F32) / QK_ROPE)
    ang = jnp.arange(S, dtype=F32)[:, None] * inv[None, :]
    cos, sin = jnp.cos(ang), jnp.sin(ang)
    one, zero = jnp.ones((S, QK_NOPE), F32), jnp.zeros((S, half), F32)
    tail = HEAD_PAD - QK_NOPE - QK_ROPE
    c = jnp.concatenate([one, cos, cos, jnp.ones((S, tail), F32)], axis=1)
    sa = jnp.concatenate([0 * one, zero, sin, jnp.zeros((S, tail), F32)], axis=1)
    sb = jnp.concatenate([0 * one, -sin, zero, jnp.zeros((S, tail), F32)], axis=1)
    return c, sa, sb


def _layer_weights(gathered, full, l, D):
    w = {}
    w_in = _pad_w_in(_join_cols(gathered["w_in"], l))
    w["w_in"] = w_in
    w["w_in_gates"], w["w_in_mla"], w["w_in_mix"] = w_in[:, :4 * D], w_in[:, 4 * D:4 * D + SEC_MLA], w_in[:, 4 * D + SEC_MLA:]
    w["w_uq"] = _pad_heads(_join_cols(gathered["w_uq"], l), QK_NOPE + QK_ROPE)
    ukv = _join_cols(gathered["w_ukv"], l).reshape(KV_LORA, HEADS, QK_NOPE + V_HEAD)
    pad = ((0, 0), (0, 0), (0, HEAD_PAD - QK_NOPE))
    k_pad = jnp.pad(ukv[:, :, :QK_NOPE], pad).reshape(KV_LORA, N_CHIPS, CHIP_HEADS_W)
    v_pad = jnp.pad(ukv[:, :, QK_NOPE:], pad).reshape(KV_LORA, N_CHIPS, CHIP_HEADS_W)
    w["w_ukv"] = jnp.concatenate([k_pad, v_pad], axis=2).reshape(KV_LORA, 2 * MLA_W)
    w["w_br_a"] = jnp.pad(_join_cols(gathered["w_br_a"], l).reshape(HEADS, V_HEAD, D),
                          ((0, 0), (0, HEAD_PAD - V_HEAD), (0, 0))).reshape(MLA_W, D)
    for n in ("w_br_b", "w_br_c", "w_br_d"):
        w[n] = _join_cols(gathered[n], l)
    w["w_out"] = gathered["w_out"][l].reshape(D, D)
    for n in ("g_pre_mix", "g_cq", "g_ckv", "pool_scale", "g_sgu_v", "g_post_mix", "g_pre_ffn", "g_post_ffn"):
        w[n] = full[n][l].reshape(1, -1)
    pw = full["pool_w"][l]
    w["pool_bd"] = jax.scipy.linalg.block_diag(*[pw[g] for g in range(GROUPS)]).astype(BF16)
    w["sgu_w"] = full["sgu_w"][l]
    w["sgu_bias"] = jnp.repeat(full["sgu_b"][l].T, GROUP_DIM, axis=1)
    w["conv_w"] = jnp.pad(full["conv_w"][l].reshape(3, MIX_W), ((0, CONV_HALO - 3), (0, 0)))
    return w


def _layer_fwd(x, w, gathered, l, S, rope):
    D = x.shape[1]
    P, h = _norm_matmul(x, w["g_pre_mix"], w["w_in"], "proj_fwd")
    Q, K, V, hq, hkv = _mla_prep(P, D, w["g_cq"], w["g_ckv"], w["w_uq"], w["w_ukv"], *rope, S)
    A, lse = _flash_fwd(Q, K, V, S)
    Bm, C, Dv = _mixers_fwd(P, D, S, w["pool_bd"], w["pool_scale"], w["g_sgu_v"], w["sgu_w"], w["sgu_bias"], w["conv_w"])
    x1, merged, o = _merge_fwd(x, P, A, Bm, C, Dv, w["w_br_a"], w["w_br_b"], w["w_br_c"], w["w_br_d"], w["w_out"], w["g_post_mix"])
    x2, h2, gt, up, f = _ffn_fwd(x1, w["g_pre_ffn"], gathered["w_ffn_gate"], gathered["w_ffn_up"], gathered["w_ffn_down"],
                                 w["g_post_ffn"], l)
    saved = dict(x=x, P=P, h=h, Q=Q, K=K, V=V, hq=hq, hkv=hkv, A=A, lse=lse, Bm=Bm, C=C, Dv=Dv, x1=x1, merged=merged, o=o,
                 h2=h2, gt=gt, up=up, f=f)
    return x2, saved


def _layer_bwd(dx2, w, gathered, l, s, S, rope, bufs):
    D = dx2.shape[1]
    Fc = gathered["w_ffn_gate"].shape[3]
    g = {}

    def wgrad(n, a, b, rows, cols, a_mode, b_mode):
        bufs[n] = _wgrad_chip(bufs.get(n), l, DEPTH, a, b, rows, cols, a_mode, b_mode, "wgrad_" + n)

    dx1, df, dgt, dup, act, g["g_pre_ffn"], g["g_post_ffn"] = _ffn_bwd(
        dx2, s["x1"], s["f"], s["gt"], s["up"], w["g_pre_ffn"], gathered["w_ffn_gate"], gathered["w_ffn_up"],
        gathered["w_ffn_down"], w["g_post_ffn"], l)
    wgrad("w_ffn_down", act, df, Fc, D, "slab", "all")
    wgrad("w_ffn_gate", s["h2"], dgt, D, Fc, "all", "slab")
    wgrad("w_ffn_up", s["h2"], dup, D, Fc, "all", "slab")

    d_o, dPg, dya, dyb, dyc, dyd, dA, dBm, dC, dDv, g["g_post_mix"] = _merge_bwd(
        dx1, s["o"], s["P"], s["A"], s["Bm"], s["C"], s["Dv"], w["w_br_a"], w["w_br_b"], w["w_br_c"], w["w_br_d"], w["w_out"],
        w["g_post_mix"])
    wgrad("w_out", s["merged"], d_o, D // N_CHIPS, D, "cols", "all")
    wgrad("w_br_a", s["A"], dya, MLA_W, D // N_CHIPS, "all", "cols")
    wgrad("w_br_b", s["Bm"], dyb, MIX_W, D // N_CHIPS, "all", "cols")
    wgrad("w_br_c", s["C"], dyc, MIX_W, D // N_CHIPS, "all", "cols")
    wgrad("w_br_d", s["Dv"], dyd, MIX_W, D // N_CHIPS, "all", "cols")

    dPm, d_pool_bd, g_ps, g_gv, g["sgu_w"], d_bias, d_cw = _mixers_bwd(
        s["P"], D, S, dBm, dC, dDv, w["pool_bd"], w["pool_scale"], w["g_sgu_v"], w["sgu_w"], w["sgu_bias"], w["conv_w"])
    g["pool_w"] = jnp.stack([d_pool_bd[k * GROUP_DIM:(k + 1) * GROUP_DIM, k * GROUP_DIM:(k + 1) * GROUP_DIM] for k in range(GROUPS)])
    g["pool_scale"], g["g_sgu_v"] = g_ps, g_gv
    g["sgu_b"] = d_bias.reshape(SGU_BLOCK, GROUPS, GROUP_DIM).sum(-1).T
    g["conv_w"] = d_cw[:3].reshape(3, 1, MIX_W)

    lse_t, delta_t = _attn_stats(s["A"], dA, s["lse"], S)
    dQ, dK, dV = _flash_bwd(s["Q"], s["K"], s["V"], dA, lse_t, delta_t, S)
    dPa, dq, dkv, g["g_cq"], g["g_ckv"] = _mla_bwd_post(s["P"], D, S, dQ, dK, dV, w["g_cq"], w["g_ckv"], w["w_uq"], w["w_ukv"], *rope)
    wgrad("w_uq", s["hq"], dq, Q_LORA, CHIP_HEADS_W, "all", "cols")
    wgrad("w_ukv", s["hkv"], dkv, KV_LORA, CHIP_KV, "all", "cols")

    dx, g["g_pre_mix"] = _proj_bwd(dx1, s["x"], w["g_pre_mix"], dPg, dPa, dPm, w["w_in_gates"], w["w_in_mla"], w["w_in_mix"])
    d_w_in = _unpad_w_in(_matmul_tn(s["h"], dPg, "wgrad_in_gates"), _matmul_tn(s["h"], dPa, "wgrad_in_mla"),
                         _matmul_tn(s["h"], dPm, "wgrad_in_mix"))
    g["w_in"] = d_w_in.reshape(D, N_CHIPS, -1).transpose(1, 0, 2)
    for n in ("g_pre_mix", "g_cq", "g_ckv", "pool_scale", "g_sgu_v", "g_post_mix", "g_pre_ffn", "g_post_ffn"):
        g[n] = g[n].reshape(-1)
    return dx, g


SMALL = REPLICATED + ("conv_w",)


def _local_step(x, target, gathered, full):
    n_seq, S, D = x.shape
    rope = _rope_tables(S)
    xs = x.reshape(n_seq * S, D)
    weights, saved = [], []
    for l in range(DEPTH):
        w = _layer_weights(gathered, full, l, D)
        xs, s = _layer_fwd(xs, w, gathered, l, S, rope)
        weights.append(w)
        saved.append(s)
    loss_parts, dx = _loss_grad(xs, target.reshape(n_seq * S, D))
    grads, bufs = [None] * DEPTH, {}
    for l in reversed(range(DEPTH)):
        dx, grads[l] = _layer_bwd(dx, weights[l], gathered, l, saved[l], S, rope, bufs)
    bufs["w_in"] = jnp.stack([grads[l]["w_in"] for l in range(DEPTH)], axis=1)
    small = {n: jnp.stack([grads[l][n] for l in range(DEPTH)]) for n in SMALL}
    return loss_parts, dx.reshape(n_seq, S, D), bufs, small


def _unpad_reduced(n, r):
    L = r.shape[0]
    if n == "w_uq":
        return r.reshape(L, Q_LORA, 2, HEAD_PAD)[..., :QK_NOPE + QK_ROPE].reshape(L, Q_LORA, -1)
    if n == "w_ukv":
        r = r.reshape(L, KV_LORA, 2, 2, HEAD_PAD)[..., :QK_NOPE]
        return jnp.concatenate([r[:, :, 0], r[:, :, 1]], axis=-1).reshape(L, KV_LORA, -1)
    if n == "w_br_a":
        return r.reshape(L, HEADS, HEAD_PAD, -1)[:, :, :V_HEAD].reshape(L, HEADS * V_HEAD, -1)
    return r


def _small_rows(n):
    return -(-n // (8 * 128)) * 8


def _to_small(parts):
    flat = jnp.concatenate([p.reshape(-1) for p in parts])
    rows = _small_rows(flat.shape[0])
    return jnp.pad(flat, (0, rows * 128 - flat.shape[0])).reshape(rows, 128)


def kernel(x, w_in, g_pre_mix, g_cq, g_ckv, w_uq, w_ukv, pool_w, pool_scale, g_sgu_v, sgu_w, sgu_b, conv_w, w_br_a, w_br_b, w_br_c, w_br_d, w_out, g_post_mix, g_pre_ffn, w_ffn_gate, w_ffn_up, w_ffn_down, g_post_ffn, loss_target, m_w_in, m_g_pre_mix, m_g_cq, m_g_ckv, m_w_uq, m_w_ukv, m_pool_w, m_pool_scale, m_g_sgu_v, m_sgu_w, m_sgu_b, m_conv_w, m_w_br_a, m_w_br_b, m_w_br_c, m_w_br_d, m_w_out, m_g_post_mix, m_g_pre_ffn, m_w_ffn_gate, m_w_ffn_up, m_w_ffn_down, m_g_post_ffn, v_w_in, v_g_pre_mix, v_g_cq, v_g_ckv, v_w_uq, v_w_ukv, v_pool_w, v_pool_scale, v_g_sgu_v, v_sgu_w, v_sgu_b, v_conv_w, v_w_br_a, v_w_br_b, v_w_br_c, v_w_br_d, v_w_out, v_g_post_mix, v_g_pre_ffn, v_w_ffn_gate, v_w_ffn_up, v_w_ffn_down, v_g_post_ffn):
    local = dict(locals())
    W = {n: local[n] for n in WEIGHTS}
    M = {n: local["m_" + n] for n in WEIGHTS}
    V = {n: local["v_" + n] for n in WEIGHTS}
    chip = 2 * lax.axis_index("x") + lax.axis_index("y")
    core = lax.axis_index("c")

    gathered = dict(zip(GATHERED, _gather_weights([_place_own(W[n], chip, n) for n in GATHERED])))
    conv_shape = conv_w.shape
    conv_cols = conv_shape[-1]
    conv_full_shape = conv_shape[:-1] + (N_CHIPS * conv_cols,)
    placed = lax.dynamic_update_slice(jnp.zeros(conv_full_shape, F32), conv_w, (0, 0, 0, chip * conv_cols))
    n_conv = int(np.prod(conv_full_shape))
    conv_sum = _all_reduce_small(_to_small([placed]), "gather_conv_w")
    full = {n: W[n] for n in REPLICATED}
    full["conv_w"] = 0.5 * conv_sum.reshape(-1)[:n_conv].reshape(conv_full_shape)

    loss_parts, grad_x, bufs, small = _local_step(x, loss_target, gathered, full)
    loss = lax.psum(jnp.sum(loss_parts), ("x", "y", "c"))

    small_sum = _all_reduce_small(_to_small([small[n] for n in SMALL]), "reduce_small_grads")
    small_grads = dict(zip(SMALL, _unpack(small_sum, [small[n].shape for n in SMALL])))
    small_grads["conv_w"] = lax.dynamic_slice(small_grads["conv_w"], (0, 0, 0, chip * conv_cols), conv_shape)

    Gs = [bufs[n] for n in GATHERED]
    Hs = [_add_halves(g, r, core, n) for n, g, r in zip(GATHERED, Gs, _exchange_halves(Gs))]
    place = [lax.axis_index(a).reshape(1).astype(jnp.int32) for a in ("x", "y", "c")]
    halves = [_sum_slots(h, s, place, n) for n, h, s in zip(GATHERED, Hs, _scatter_partials(Hs))]
    shard_grads = {n: _unpad_reduced(n, r).reshape(W[n].shape) for n, r in zip(GATHERED, _join_halves(halves))}

    out_g, out_d, out_m, out_v = {}, {}, {}, {}
    for n in GATHERED:
        shp = W[n].shape
        flat = lambda a: a.reshape(-1, shp[-1])
        d, m2, v2 = _adamw(flat(W[n]), flat(shard_grads[n]), flat(M[n]), flat(V[n]), "adamw_" + n)
        out_g[n], out_d[n], out_m[n], out_v[n] = shard_grads[n], d.reshape(shp), m2.reshape(shp), v2.reshape(shp)
    rest_shapes = [W[n].shape for n in SMALL]
    d, m2, v2 = _adamw(_to_small([W[n] for n in SMALL]), _to_small([small_grads[n] for n in SMALL]),
                       _to_small([M[n] for n in SMALL]), _to_small([V[n] for n in SMALL]), "adamw_small")
    for n, dd, mm, vv in zip(SMALL, _unpack(d, rest_shapes), _unpack(m2, rest_shapes), _unpack(v2, rest_shapes)):
        out_g[n], out_d[n], out_m[n], out_v[n] = small_grads[n], dd, mm, vv

    return (loss, grad_x, *[out_g[n] for n in WEIGHTS], *[out_d[n] for n in WEIGHTS], *[out_m[n] for n in WEIGHTS],
            *[out_v[n] for n in WEIGHTS])
```

```python
import functools

import numpy as np
import jax
import jax.numpy as jnp
from jax import lax
from jax.experimental import pallas as pl
from jax.experimental.pallas import tpu as pltpu

F32 = jnp.float32
BF16 = jnp.bfloat16

EPS = 1e-6
NEG_INF = -1e30
DEPTH = 4
HEADS = 8
QK_NOPE = 64
QK_ROPE = 32
V_HEAD = 64
HEAD_PAD = 128
Q_LORA = 256
KV_LORA = 128
ROPE_THETA = 10000.0
POOL_WINDOWS = (2, 4, 8, 16)
GROUPS = 4
GROUP_DIM = 64
MIX_W = GROUPS * GROUP_DIM
POOL_HALO = 16
CONV_HALO = 8
SGU_BLOCK = 128
CHUNK = 64
CHUNK_SHIFT = 6
ACT_ROWS = 16
NORM_ROWS = 16
ROW_PARTS = 2
SOFTMAX_ROWS = 32
GROUP_SHIFT = 6
N_BRANCH = 4
MLA_W = HEADS * HEAD_PAD
N_CHIPS = 4
CHIP_HEADS_W = MLA_W // N_CHIPS
CHIP_KV = 2 * CHIP_HEADS_W
ATTN_SCALE = (QK_NOPE + QK_ROPE) ** -0.5
LOG2E = 1.4426950408889634
LN2 = 0.6931471805599453
SEC_MLA = Q_LORA + KV_LORA + HEAD_PAD
SEC_MIX = 6 * MIX_W

ADAM_LR = 0.001
ADAM_B1 = 0.9
ADAM_B2 = 0.999
ADAM_EPS = 1e-08
ADAM_WD = 0.01
ADAM_STEP = 10

VMEM_LIMIT = 56 * 1024 * 1024
MESH = pl.DeviceIdType.MESH


def _tile(n, pref, mult=8):
    t = min(n, pref)
    while t > 0:
        if n % t == 0 and t % mult == 0:
            return t
        t -= 1
    return n


def _params(n_axes):
    return pltpu.CompilerParams(dimension_semantics=("arbitrary",) * n_axes, vmem_limit_bytes=VMEM_LIMIT)


def _dot(a, b):
    return jnp.dot(a, b, preferred_element_type=F32)


def _dot_nt(a, b):
    return lax.dot_general(a, b, (((1,), (1,)), ((), ())), preferred_element_type=F32)


def _dot_tn(a, b):
    return lax.dot_general(a, b, (((0,), (0,)), ((), ())), preferred_element_type=F32)


def _rms_r(x):
    return lax.rsqrt(jnp.mean(x * x, axis=-1, keepdims=True) + EPS)


def _rms_bwd(dy, x, g):
    r = _rms_r(x)
    u = dy * g
    dx = r * u - x * (r * r * r * jnp.mean(u * x, axis=-1, keepdims=True))
    dg = jnp.sum(dy * x * r, axis=0, keepdims=True)
    return dx, dg


def _sigmoid(x):
    return 1.0 / (1.0 + jnp.exp(-x))


def _shift_down(a, k):
    return pltpu.roll(a, k, 0)


def _shift_up(a, k):
    return pltpu.roll(a, a.shape[0] - k, 0)


def _rope(x, c, sa, sb):
    w = x.shape[-1]
    return x * c + pltpu.roll(x, QK_ROPE // 2, 1) * sa + pltpu.roll(x, w - QK_ROPE // 2, 1) * sb


def _rope_t(d, c, sa, sb):
    w = d.shape[-1]
    return d * c + pltpu.roll(d * sa, w - QK_ROPE // 2, 1) + pltpu.roll(d * sb, QK_ROPE // 2, 1)


def _full(shape):
    return pl.BlockSpec(shape, lambda *_: (0,) * len(shape))


def _norm_matmul(x, g, w, name):
    T, K = x.shape
    N = w.shape[1]
    tm, tn = _tile(T, 512), _tile(N, 1536, 128)

    def body(x_ref, g_ref, w_ref, o_ref, h_ref):
        @pl.when(pl.program_id(1) == 0)
        def _():
            xv = x_ref[...]
            h_ref[...] = (xv * _rms_r(xv) * g_ref[...]).astype(BF16)

        o_ref[...] = _dot(h_ref[...], w_ref[...])

    return pl.pallas_call(
        body, name=name, grid=(T // tm, N // tn),
        in_specs=[pl.BlockSpec((tm, K), lambda i, j: (i, 0)), _full((1, K)), pl.BlockSpec((K, tn), lambda i, j: (0, j))],
        out_specs=[pl.BlockSpec((tm, tn), lambda i, j: (i, j)), pl.BlockSpec((tm, K), lambda i, j: (i, 0))],
        out_shape=[jax.ShapeDtypeStruct((T, N), F32), jax.ShapeDtypeStruct((T, K), BF16)],
        compiler_params=_params(2),
    )(x, g, w)


def _mla_prep(P, D, g_cq, g_ckv, wuq, wukv, rope_c, rope_sa, rope_sb, S):
    T = P.shape[0]
    tm = _tile(S, 512)
    n_si = S // tm
    base = 4 * D

    def body(cq_ref, ckv_ref, kr_ref, gq_ref, gkv_ref, wq_ref, wkv_ref, c_ref, sa_ref, sb_ref,
             q_ref, k_ref, v_ref, hq_ref, hkv_ref):
        c, sa, sb = c_ref[...], sa_ref[...], sb_ref[...]
        cq = cq_ref[...]
        hq = (cq * _rms_r(cq) * gq_ref[...]).astype(BF16)
        hq_ref[...] = hq
        q = _dot(hq, wq_ref[...])
        q = _rope(q, jnp.tile(c, (1, HEADS)), jnp.tile(sa, (1, HEADS)), jnp.tile(sb, (1, HEADS)))
        q_ref[...] = (q * (ATTN_SCALE * LOG2E)).astype(BF16)
        ckv = ckv_ref[...]
        hkv = (ckv * _rms_r(ckv) * gkv_ref[...]).astype(BF16)
        hkv_ref[...] = hkv
        kv = _dot(hkv, wkv_ref[...])
        kr = _rope(kr_ref[...], c, sa, sb)
        k_nope = jnp.concatenate([kv[:, j * CHIP_KV:j * CHIP_KV + CHIP_HEADS_W] for j in range(N_CHIPS)], axis=1)
        k_ref[...] = (k_nope + jnp.tile(kr, (1, HEADS))).astype(BF16)
        v = jnp.concatenate([kv[:, j * CHIP_KV + CHIP_HEADS_W:(j + 1) * CHIP_KV] for j in range(N_CHIPS)], axis=1)
        ones_lane = (lax.broadcasted_iota(jnp.int32, (1, MLA_W), 1) & (HEAD_PAD - 1)) == V_HEAD
        v_ref[...] = jnp.where(ones_lane, 1.0, v).astype(BF16)

    tab = pl.BlockSpec((tm, HEAD_PAD), lambda i: (i % n_si, 0))
    row = lambda w: pl.BlockSpec((tm, w), lambda i: (i, 0))
    return pl.pallas_call(
        body, name="mla_prep", grid=(T // tm,),
        in_specs=[pl.BlockSpec((tm, Q_LORA), lambda i: (i, base // Q_LORA)),
                  pl.BlockSpec((tm, KV_LORA), lambda i: (i, (base + Q_LORA) // KV_LORA)),
                  pl.BlockSpec((tm, HEAD_PAD), lambda i: (i, (base + Q_LORA + KV_LORA) // HEAD_PAD)),
                  _full((1, Q_LORA)), _full((1, KV_LORA)), _full((Q_LORA, MLA_W)), _full((KV_LORA, 2 * MLA_W)),
                  tab, tab, tab],
        out_specs=[row(MLA_W), row(MLA_W), row(MLA_W), row(Q_LORA), row(KV_LORA)],
        out_shape=[jax.ShapeDtypeStruct((T, MLA_W), BF16)] * 3
        + [jax.ShapeDtypeStruct((T, Q_LORA), BF16), jax.ShapeDtypeStruct((T, KV_LORA), BF16)],
        compiler_params=_params(1),
    )(P, P, P, g_cq, g_ckv, wuq, wukv, rope_c, rope_sa, rope_sb)


def _chunk_mask(tq, tk):
    row = lax.broadcasted_iota(jnp.int32, (tq, tk), 0)
    col = lax.broadcasted_iota(jnp.int32, (tq, tk), 1)
    return (row >> CHUNK_SHIFT) >= (col >> CHUNK_SHIFT)


def _flash_fwd(Q, K, V, S):
    T = Q.shape[0]
    n_seq = T // S
    tq = _tile(S, 256, 128)
    nq = S // tq

    def body(q_ref, k_ref, v_ref, o_ref, lse_ref, m_s, acc_s, s_s, p_s, a_s):
        qi = pl.program_id(1)
        m_s[...] = jnp.full(m_s.shape, NEG_INF, F32)
        acc_s[...] = jnp.zeros_like(acc_s)

        def block(kb, masked):
            rows = pl.ds(pl.multiple_of(kb * tq, tq), tq)
            for h in range(HEADS):
                hs = slice(h * HEAD_PAD, (h + 1) * HEAD_PAD)
                s_s[h] = _dot_nt(q_ref[:, hs], k_ref[rows, hs])
            def softmax_head(h):
                for r in range(0, tq, SOFTMAX_ROWS):
                    rs = slice(r, r + SOFTMAX_ROWS)
                    s = s_s[h, rs, :]
                    if masked:
                        row = r + lax.broadcasted_iota(jnp.int32, (SOFTMAX_ROWS, tq), 0)
                        col = lax.broadcasted_iota(jnp.int32, (SOFTMAX_ROWS, tq), 1)
                        s = jnp.where((row >> CHUNK_SHIFT) >= (col >> CHUNK_SHIFT), s, NEG_INF)
                    m_old = m_s[h, rs]
                    m_new = jnp.maximum(m_old, jnp.max(s, axis=-1, keepdims=True))
                    m_s[h, rs] = m_new
                    a_s[h, rs] = jnp.exp2(m_old - m_new)
                    for half in range(tq // HEAD_PAD):
                        cs = slice(half * HEAD_PAD, (half + 1) * HEAD_PAD)
                        p_s[h, rs, cs] = jnp.exp2(s[:, cs] - m_new).astype(BF16)

            for h in range(HEADS):
                softmax_head(h)
            for h in range(HEADS):
                hs = slice(h * HEAD_PAD, (h + 1) * HEAD_PAD)
                acc_s[:, hs] = a_s[h] * acc_s[:, hs] + _dot(p_s[h], v_ref[rows, hs])

        def full_block(kb, carry):
            block(kb, False)
            return carry

        lax.fori_loop(0, qi, full_block, 0)
        block(qi, True)
        lane = lax.broadcasted_iota(jnp.int32, (tq, HEAD_PAD), 1)
        lse_all = jnp.zeros((tq, HEAD_PAD), F32)
        for h in range(HEADS):
            hs = slice(h * HEAD_PAD, (h + 1) * HEAD_PAD)
            acc = acc_s[:, hs]
            l = jnp.sum(jnp.where(lane == V_HEAD, acc, 0.0), axis=-1, keepdims=True)
            o_ref[:, hs] = (acc / l).astype(BF16)
            lse_all = jnp.where(lane == h, m_s[h] + jnp.log2(l), lse_all)
        lse_ref[...] = lse_all

    return pl.pallas_call(
        body, name="flash_fwd", grid=(n_seq, nq),
        in_specs=[pl.BlockSpec((tq, MLA_W), lambda b, i: (b * nq + i, 0)),
                  pl.BlockSpec((S, MLA_W), lambda b, i: (b, 0)), pl.BlockSpec((S, MLA_W), lambda b, i: (b, 0))],
        out_specs=[pl.BlockSpec((tq, MLA_W), lambda b, i: (b * nq + i, 0)),
                   pl.BlockSpec((tq, HEAD_PAD), lambda b, i: (b * nq + i, 0))],
        out_shape=[jax.ShapeDtypeStruct((T, MLA_W), BF16), jax.ShapeDtypeStruct((T, HEAD_PAD), F32)],
        scratch_shapes=[pltpu.VMEM((HEADS, tq, HEAD_PAD), F32), pltpu.VMEM((tq, MLA_W), F32), pltpu.VMEM((HEADS, tq, tq), F32),
                        pltpu.VMEM((HEADS, tq, tq), BF16), pltpu.VMEM((HEADS, tq, HEAD_PAD), F32)],
        compiler_params=_params(2),
    )(Q, K, V)


def _lane_group():
    return lax.broadcasted_iota(jnp.int32, (1, MIX_W), 1) >> GROUP_SHIFT


def _by_group(a0, a1, a2, a3):
    g = _lane_group()
    return jnp.where(g == 0, a0, jnp.where(g == 1, a1, jnp.where(g == 2, a2, a3)))


def _pool_count(si, tc, rows):
    pos = si * tc + lax.broadcasted_iota(jnp.int32, (rows, MIX_W), 0)
    win = _by_group(*POOL_WINDOWS)
    return jnp.minimum(pos + 1, win).astype(F32)


def _pool_fwd(z, z_prev, si, tc):
    ze = jnp.concatenate([z_prev, z], axis=0)
    s1 = ze + _shift_down(ze, 1)
    s2 = s1 + _shift_down(s1, 2)
    s4 = s2 + _shift_down(s2, 4)
    s8 = s4 + _shift_down(s4, 8)
    win_sum = _by_group(s1, s2, s4, s8)[POOL_HALO:]
    return win_sum / _pool_count(si, tc, tc) - z


def _sgu_weights(w_ref):
    row = lax.broadcasted_iota(jnp.int32, (SGU_BLOCK, SGU_BLOCK), 0)
    col = lax.broadcasted_iota(jnp.int32, (SGU_BLOCK, SGU_BLOCK), 1)
    keep = (row >> CHUNK_SHIFT) >= (col >> CHUNK_SHIFT)
    return keep, [jnp.where(keep, w_ref[g], 0.0).astype(BF16) for g in range(GROUPS)]


def _sgu_mix(vn_blk, wm, bias):
    g = _lane_group()
    mixed = bias
    for k in range(GROUPS):
        mixed = mixed + jnp.where(g == k, _dot(wm[k], vn_blk), 0.0)
    return mixed


def _conv_fwd(z, z_prev, w_ref):
    ze = jnp.concatenate([z_prev, z], axis=0)
    y = w_ref[0:1, :] * _shift_down(ze, 2) + w_ref[1:2, :] * _shift_down(ze, 1) + w_ref[2:3, :] * ze
    return y[CONV_HALO:]


def _mix_specs(T, D, tc):
    base = (4 * D + SEC_MLA) // MIX_W
    cur = lambda k: pl.BlockSpec((tc, MIX_W), lambda i: (i, base + k))
    prev = lambda k, halo: pl.BlockSpec((halo, MIX_W), lambda i: (jnp.maximum(i * (tc // halo) - 1, 0), base + k))
    nxt = lambda k, halo: pl.BlockSpec((halo, MIX_W), lambda i: (jnp.minimum((i + 1) * (tc // halo), T // halo - 1), base + k))
    return cur, prev, nxt


def _mixers_fwd(P, D, S, pool_bd, pool_scale, g_v, sgu_w, sgu_bias, conv_w):
    T = P.shape[0]
    tc = _tile(S, 512, SGU_BLOCK)
    n_si = S // tc
    cur, prev, _ = _mix_specs(T, D, tc)

    def body(z_ref, zp_ref, u_ref, v_ref, b_ref, c_ref, x_ref, cp_ref, xp_ref,
             pw_ref, ps_ref, gv_ref, sw_ref, sb_ref, cw_ref, ob_ref, oc_ref, od_ref):
        si = pl.program_id(0) % n_si
        first = si == 0
        z = z_ref[...]
        pooled = _pool_fwd(z, jnp.where(first, 0.0, zp_ref[...]), si, tc)
        ob_ref[...] = (_dot(pooled.astype(BF16), pw_ref[...]) * ps_ref[...]).astype(BF16)

        v = v_ref[...]
        vn = (v * _rms_r(v) * gv_ref[...]).astype(BF16)
        _, wm = _sgu_weights(sw_ref)
        for blk in range(tc // SGU_BLOCK):
            rows = slice(blk * SGU_BLOCK, (blk + 1) * SGU_BLOCK)
            oc_ref[rows, :] = (u_ref[rows, :] * _sgu_mix(vn[rows], wm, sb_ref[...])).astype(BF16)

        zc = c_ref[...] * x_ref[...]
        zc_prev = jnp.where(first, 0.0, cp_ref[...] * xp_ref[...])
        od_ref[...] = (b_ref[...] * _conv_fwd(zc, zc_prev, cw_ref)).astype(BF16)

    out = pl.BlockSpec((tc, MIX_W), lambda i: (i, 0))
    return pl.pallas_call(
        body, name="mixers_fwd", grid=(T // tc,),
        in_specs=[cur(0), prev(0, POOL_HALO), cur(1), cur(2), cur(3), cur(4), cur(5), prev(4, CONV_HALO), prev(5, CONV_HALO),
                  _full((MIX_W, MIX_W)), _full((1, MIX_W)), _full((1, MIX_W)), _full((GROUPS, SGU_BLOCK, SGU_BLOCK)),
                  _full((SGU_BLOCK, MIX_W)), _full((CONV_HALO, MIX_W))],
        out_specs=[out, out, out],
        out_shape=[jax.ShapeDtypeStruct((T, MIX_W), BF16)] * 3,
        compiler_params=_params(1),
    )(P, P, P, P, P, P, P, P, P, pool_bd, pool_scale, g_v, sgu_w, sgu_bias, conv_w)


def _merge_fwd(x, P, A, Bm, C, Dv, wa, wb, wc, wd, wout, g_post):
    T, D = x.shape
    tm = _tile(T, 256)

    def body(x_ref, lg_ref, a_ref, b_ref, c_ref, d_ref, wa_ref, wb_ref, wc_ref, wd_ref, wo_ref, g_ref,
             x1_ref, mg_ref, o_ref):
        merged = jnp.zeros((tm, D), F32)
        for k, (br, w) in enumerate(((a_ref, wa_ref), (b_ref, wb_ref), (c_ref, wc_ref), (d_ref, wd_ref))):
            merged = merged + _sigmoid(lg_ref[:, k * D:(k + 1) * D]) * _dot(br[...], w[...])
        mg = merged.astype(BF16)
        mg_ref[...] = mg
        o = _dot(mg, wo_ref[...])
        o_ref[...] = o
        x1_ref[...] = x_ref[...] + o * _rms_r(o) * g_ref[...]

    row = lambda w: pl.BlockSpec((tm, w), lambda i: (i, 0))
    return pl.pallas_call(
        body, name="merge_fwd", grid=(T // tm,),
        in_specs=[row(D), row(4 * D), row(MLA_W), row(MIX_W), row(MIX_W), row(MIX_W),
                  _full((MLA_W, D)), _full((MIX_W, D)), _full((MIX_W, D)), _full((MIX_W, D)), _full((D, D)), _full((1, D))],
        out_specs=[row(D), row(D), row(D)],
        out_shape=[jax.ShapeDtypeStruct((T, D), F32), jax.ShapeDtypeStruct((T, D), BF16), jax.ShapeDtypeStruct((T, D), F32)],
        compiler_params=_params(1),
    )(x, P, A, Bm, C, Dv, wa, wb, wc, wd, wout, g_post)


def _ffn_specs(T, D, Fc, l):
    tm = _tile(T, 512)
    row = pl.BlockSpec((tm, D), lambda i, j: (i, 0))
    col = pl.BlockSpec((None, tm, Fc), lambda i, j: (j, i, 0))
    w_in = pl.BlockSpec((None, None, D, Fc), lambda i, j: (l, j, 0, 0))
    w_out = pl.BlockSpec((None, None, Fc, D), lambda i, j: (l, j, 0, 0))
    return tm, row, col, w_in, w_out


def _ffn_fwd(x1, g_pre, wg, wu, wdn, g_post, l):
    T, D = x1.shape
    nf, Fc = wg.shape[1], wg.shape[3]
    tm, row, col, w_in, w_out = _ffn_specs(T, D, Fc, l)

    def body(x_ref, gp_ref, wg_ref, wu_ref, wd_ref, gq_ref, x2_ref, h_ref, gt_ref, up_ref, f_ref, gt_s, up_s, a_s):
        j = pl.program_id(1)

        @pl.when(j == 0)
        def _():
            for r in range(0, tm, NORM_ROWS):
                rs = slice(r, r + NORM_ROWS)
                xv = x_ref[rs, :]
                h_ref[rs, :] = (xv * _rms_r(xv) * gp_ref[...]).astype(BF16)
            f_ref[...] = jnp.zeros_like(f_ref)

        gt_s[...] = _dot(h_ref[...], wg_ref[...])
        up_s[...] = _dot(h_ref[...], wu_ref[...])
        for r in range(0, tm, ACT_ROWS):
            rs = slice(r, r + ACT_ROWS)
            gt, up = gt_s[rs, :], up_s[rs, :]
            gt_ref[rs, :] = gt.astype(BF16)
            up_ref[rs, :] = up.astype(BF16)
            a_s[rs, :] = (gt * _sigmoid(gt) * up).astype(BF16)
        f_ref[...] += _dot(a_s[...], wd_ref[...])

        @pl.when(j == nf - 1)
        def _():
            for r in range(0, tm, NORM_ROWS):
                rs = slice(r, r + NORM_ROWS)
                f = f_ref[rs, :]
                x2_ref[rs, :] = x_ref[rs, :] + f * _rms_r(f) * gq_ref[...]

    return pl.pallas_call(
        body, name="ffn_fwd", grid=(T // tm, nf),
        in_specs=[row, _full((1, D)), w_in, w_in, w_out, _full((1, D))],
        out_specs=[row, row, col, col, row],
        out_shape=[jax.ShapeDtypeStruct((T, D), F32), jax.ShapeDtypeStruct((T, D), BF16),
                   jax.ShapeDtypeStruct((nf, T, Fc), BF16), jax.ShapeDtypeStruct((nf, T, Fc), BF16),
                   jax.ShapeDtypeStruct((T, D), F32)],
        scratch_shapes=[pltpu.VMEM((tm, Fc), F32), pltpu.VMEM((tm, Fc), F32), pltpu.VMEM((tm, Fc), BF16)],
        compiler_params=_params(2),
    )(x1, g_pre, wg, wu, wdn, g_post)


def _loss_grad(y, target):
    T, D = y.shape
    tm = _tile(T, 512)

    def body(y_ref, t_ref, l_ref, dy_ref):
        @pl.when(pl.program_id(0) == 0)
        def _():
            l_ref[...] = jnp.zeros_like(l_ref)

        d = y_ref[...] - t_ref[...]
        dy_ref[...] = d * (1.0 / D)
        e = jnp.sum((d * d).reshape(tm // 8, 8, D), axis=0)
        part = e[:, 0:128]
        for k in range(1, D // 128):
            part = part + e[:, k * 128:(k + 1) * 128]
        l_ref[...] += part * (0.5 / D)

    row = pl.BlockSpec((tm, D), lambda i: (i, 0))
    return pl.pallas_call(
        body, name="loss_grad", grid=(T // tm,),
        in_specs=[row, row], out_specs=[_full((8, 128)), row],
        out_shape=[jax.ShapeDtypeStruct((8, 128), F32), jax.ShapeDtypeStruct((T, D), F32)],
        compiler_params=_params(1),
    )(y, target)


def _matmul_tn(a, b, name):
    T, M = a.shape
    N = b.shape[1]
    tm, tn, tk = _tile(M, 1024, 128), _tile(N, 1536, 128), _tile(T, 512)

    def body(a_ref, b_ref, o_ref):
        @pl.when(pl.program_id(2) == 0)
        def _():
            o_ref[...] = jnp.zeros_like(o_ref)

        o_ref[...] += _dot_tn(a_ref[...], b_ref[...])

    return pl.pallas_call(
        body, name=name, grid=(M // tm, N // tn, T // tk),
        in_specs=[pl.BlockSpec((tk, tm), lambda i, j, k: (k, i)), pl.BlockSpec((tk, tn), lambda i, j, k: (k, j))],
        out_specs=pl.BlockSpec((tm, tn), lambda i, j, k: (i, j)),
        out_shape=jax.ShapeDtypeStruct((M, N), F32),
        compiler_params=_params(3),
    )(a, b)


def _wgrad_chip(buf, l, n_layers, a, b, rows, cols, a_mode, b_mode, name):
    T = a.shape[-2]
    tk = _tile(T, 1024)

    def spec(mode, width):
        if mode == "all":
            return pl.BlockSpec((tk, width), lambda k, t: (t, 0))
        if mode == "cols":
            return pl.BlockSpec((tk, width), lambda k, t: (t, k))
        return pl.BlockSpec((None, tk, width), lambda k, t: (k, t, 0))

    def body(a_ref, b_ref, *rest):
        o_ref = rest[-1]

        @pl.when(pl.program_id(1) == 0)
        def _():
            o_ref[...] = jnp.zeros_like(o_ref)

        o_ref[...] += _dot_tn(a_ref[...], b_ref[...])

    keep = [] if buf is None else [buf]
    return pl.pallas_call(
        body, name=name, grid=(N_CHIPS, T // tk),
        in_specs=[spec(a_mode, rows), spec(b_mode, cols)] + [pl.BlockSpec(memory_space=pl.ANY)] * len(keep),
        out_specs=pl.BlockSpec((None, None, rows, cols), lambda k, t: (k, l, 0, 0)),
        out_shape=jax.ShapeDtypeStruct((N_CHIPS, n_layers, rows, cols), F32),
        input_output_aliases={2: 0} if keep else {},
        compiler_params=_params(2),
    )(a, b, *keep)


def _ffn_bwd(dx2, x1, f, gt, up, g_pre, wg, wu, wdn, g_post, l):
    T, D = x1.shape
    nf, Fc = wg.shape[1], wg.shape[3]
    tm, row, col, w_in, w_out = _ffn_specs(T, D, Fc, l)

    def body(dx2_ref, x1_ref, f_ref, gt_ref, up_ref, gp_ref, wg_ref, wu_ref, wd_ref, gq_ref,
             dx1_ref, df_ref, dgt_ref, dup_ref, a_ref, dgp_ref, dgq_ref, dh_acc, da_s):
        i, j = pl.program_id(0), pl.program_id(1)

        @pl.when((i == 0) & (j == 0))
        def _():
            dgp_ref[...] = jnp.zeros_like(dgp_ref)
            dgq_ref[...] = jnp.zeros_like(dgq_ref)

        @pl.when(j == 0)
        def _():
            dg_sum = jnp.zeros((1, D), F32)
            for r in range(0, tm, NORM_ROWS):
                rs = slice(r, r + NORM_ROWS)
                df, dg = _rms_bwd(dx2_ref[rs, :], f_ref[rs, :], gq_ref[...])
                df_ref[rs, :] = df.astype(BF16)
                dg_sum = dg_sum + dg
            dgq_ref[...] += dg_sum
            dh_acc[...] = jnp.zeros_like(dh_acc)

        parts = [slice(p * tm // ROW_PARTS, (p + 1) * tm // ROW_PARTS) for p in range(ROW_PARTS)]
        for ps in parts:
            da_s[ps, :] = _dot_nt(df_ref[ps, :], wd_ref[...])
        for ps in parts:
            for r in range(ps.start, ps.stop, ACT_ROWS):
                rs = slice(r, r + ACT_ROWS)
                da = da_s[rs, :]
                gt = gt_ref[rs, :].astype(F32)
                up = up_ref[rs, :].astype(F32)
                sig = _sigmoid(gt)
                silu = gt * sig
                dgt_ref[rs, :] = (da * up * (sig * (1.0 + gt * (1.0 - sig)))).astype(BF16)
                dup_ref[rs, :] = (da * silu).astype(BF16)
                a_ref[rs, :] = (silu * up).astype(BF16)
            dh_acc[ps, :] += _dot_nt(dgt_ref[ps, :], wg_ref[...]) + _dot_nt(dup_ref[ps, :], wu_ref[...])

        @pl.when(j == nf - 1)
        def _():
            dg_sum = jnp.zeros((1, D), F32)
            for r in range(0, tm, NORM_ROWS):
                rs = slice(r, r + NORM_ROWS)
                dx, dg = _rms_bwd(dh_acc[rs, :], x1_ref[rs, :], gp_ref[...])
                dx1_ref[rs, :] = dx2_ref[rs, :] + dx
                dg_sum = dg_sum + dg
            dgp_ref[...] += dg_sum

    return pl.pallas_call(
        body, name="ffn_bwd", grid=(T // tm, nf),
        in_specs=[row, row, row, col, col, _full((1, D)), w_in, w_in, w_out, _full((1, D))],
        out_specs=[row, row, col, col, col, _full((1, D)), _full((1, D))],
        out_shape=[jax.ShapeDtypeStruct((T, D), F32), jax.ShapeDtypeStruct((T, D), BF16)]
        + [jax.ShapeDtypeStruct((nf, T, Fc), BF16)] * 3
        + [jax.ShapeDtypeStruct((1, D), F32), jax.ShapeDtypeStruct((1, D), F32)],
        scratch_shapes=[pltpu.VMEM((tm, D), F32), pltpu.VMEM((tm, Fc), F32)],
        compiler_params=_params(2),
    )(dx2, x1, f, gt, up, g_pre, wg, wu, wdn, g_post)


def _wgrad_out(l, n_layers, rows, cols):
    spec = pl.BlockSpec((N_CHIPS, None, rows, cols), lambda *_: (0, l, 0, 0))
    return spec, jax.ShapeDtypeStruct((N_CHIPS, n_layers, rows, cols), F32)


def _merge_bwd(dx1, o, merged, P, A, Bm, C, Dv, wa, wb, wc, wd, wout, g_post, l, bufs):
    T, D = o.shape
    tm = _tile(T, 256)
    Dc = D // N_CHIPS
    names = ("w_out", "w_br_a", "w_br_b", "w_br_c", "w_br_d")

    def body(dx1_ref, o_ref, mg_ref, lg_ref, a_ref, b_ref, c_ref, d_ref, wa_ref, wb_ref, wc_ref, wd_ref, wo_ref, g_ref, *rest):
        dlg_ref, da_ref, db_ref, dc_ref, dd_ref, dg_ref, go_ref, ga_ref, gb_ref, gc_ref, gd_ref = rest[-11:]

        @pl.when(pl.program_id(0) == 0)
        def _():
            for r in (dg_ref, go_ref, ga_ref, gb_ref, gc_ref, gd_ref):
                r[...] = jnp.zeros_like(r)

        d_o, dg = _rms_bwd(dx1_ref[...], o_ref[...], g_ref[...])
        dg_ref[...] += dg
        d_o = d_o.astype(BF16)
        for k in range(N_CHIPS):
            go_ref[k] += _dot_tn(mg_ref[:, k * Dc:(k + 1) * Dc], d_o)
        dm = _dot_nt(d_o, wo_ref[...])
        branches = ((a_ref, wa_ref, da_ref, ga_ref), (b_ref, wb_ref, db_ref, gb_ref),
                    (c_ref, wc_ref, dc_ref, gc_ref), (d_ref, wd_ref, dd_ref, gd_ref))
        for j, (br, w, dbr_ref, gw_ref) in enumerate(branches):
            gate = _sigmoid(lg_ref[:, j * D:(j + 1) * D])
            y = _dot(br[...], w[...])
            dlg_ref[:, j * D:(j + 1) * D] = (dm * y * gate * (1.0 - gate)).astype(BF16)
            dy = (dm * gate).astype(BF16)
            dbr_ref[...] = _dot_nt(dy, w[...]).astype(dbr_ref.dtype)
            for k in range(N_CHIPS):
                gw_ref[k] += _dot_tn(br[...], dy[:, k * Dc:(k + 1) * Dc])

    row = lambda w: pl.BlockSpec((tm, w), lambda i: (i, 0))
    wg = [_wgrad_out(l, DEPTH, r, c) for r, c in ((Dc, D), (MLA_W, Dc), (MIX_W, Dc), (MIX_W, Dc), (MIX_W, Dc))]
    keep = [] if bufs is None else [bufs[n] for n in names]
    n_in = 14
    out = pl.pallas_call(
        body, name="merge_bwd", grid=(T // tm,),
        in_specs=[row(D), row(D), row(D), row(4 * D), row(MLA_W), row(MIX_W), row(MIX_W), row(MIX_W),
                  _full((MLA_W, D)), _full((MIX_W, D)), _full((MIX_W, D)), _full((MIX_W, D)), _full((D, D)), _full((1, D))]
        + [pl.BlockSpec(memory_space=pl.ANY)] * len(keep),
        out_specs=[row(4 * D), row(MLA_W), row(MIX_W), row(MIX_W), row(MIX_W), _full((1, D))] + [s for s, _ in wg],
        out_shape=[jax.ShapeDtypeStruct((T, 4 * D), BF16), jax.ShapeDtypeStruct((T, MLA_W), BF16)]
        + [jax.ShapeDtypeStruct((T, MIX_W), F32)] * 3 + [jax.ShapeDtypeStruct((1, D), F32)] + [s for _, s in wg],
        input_output_aliases={n_in + i: 6 + i for i in range(len(keep))},
        compiler_params=_params(1),
    )(dx1, o, merged, P, A, Bm, C, Dv, wa, wb, wc, wd, wout, g_post, *keep)
    return out[:6], dict(zip(names, out[6:]))


def _mixers_bwd(P, D, S, dBm, dC, dDv, pool_bd, pool_scale, g_v, sgu_w, sgu_bias, conv_w):
    T = P.shape[0]
    tc = _tile(S, 512, SGU_BLOCK)
    n_si = S // tc
    cur, prev, nxt = _mix_specs(T, D, tc)
    n_blk = tc // SGU_BLOCK

    def body(z_ref, zp_ref, u_ref, v_ref, b_ref, c_ref, x_ref, cp_ref, xp_ref, bn_ref,
             dbm_ref, dbmn_ref, dc_ref, ddv_ref, ddvn_ref,
             pw_ref, ps_ref, gv_ref, sw_ref, sb_ref, cw_ref,
             dp_ref, dpw_ref, dps_ref, dgv_ref, dsw_ref, dsb_ref, dcw_ref, dvn_acc):
        si = pl.program_id(0) % n_si
        first, last = si == 0, si == n_si - 1

        @pl.when(pl.program_id(0) == 0)
        def _():
            for r in (dpw_ref, dps_ref, dgv_ref, dsw_ref, dsb_ref, dcw_ref):
                r[...] = jnp.zeros_like(r)

        z = z_ref[...]
        pooled = _pool_fwd(z, jnp.where(first, 0.0, zp_ref[...]), si, tc).astype(BF16)
        dbm = dbm_ref[...]
        dps_ref[...] += jnp.sum(dbm * _dot(pooled, pw_ref[...]), axis=0, keepdims=True)
        dmix = (jnp.concatenate([dbm, jnp.where(last, 0.0, dbmn_ref[...])], axis=0) * ps_ref[...]).astype(BF16)
        dpw_ref[...] += _dot_tn(pooled, dmix[:tc])
        dpool = _dot_nt(dmix, pw_ref[...])
        e = dpool / _pool_count(si, tc, tc + POOL_HALO)
        f1 = e + _shift_up(e, 1)
        f2 = f1 + _shift_up(f1, 2)
        f4 = f2 + _shift_up(f2, 4)
        f8 = f4 + _shift_up(f4, 8)
        dp_ref[:, 0:MIX_W] = (_by_group(f1, f2, f4, f8)[:tc] - dpool[:tc]).astype(BF16)

        v = v_ref[...]
        vn = (v * _rms_r(v) * gv_ref[...]).astype(BF16)
        keep, wm = _sgu_weights(sw_ref)
        g = _lane_group()
        for blk in range(n_blk):
            rows = slice(blk * SGU_BLOCK, (blk + 1) * SGU_BLOCK)
            vb = vn[rows]
            dc = dc_ref[rows, :]
            dp_ref[rows, MIX_W:2 * MIX_W] = (dc * _sgu_mix(vb, wm, sb_ref[...])).astype(BF16)
            dmx = dc * u_ref[rows, :]
            dsb_ref[...] += dmx
            dvn = jnp.zeros((SGU_BLOCK, MIX_W), F32)
            for k in range(GROUPS):
                dmk = jnp.where(g == k, dmx, 0.0).astype(BF16)
                dsw_ref[k] += jnp.where(keep, _dot_nt(dmk, vb), 0.0)
                dvn = dvn + _dot_tn(wm[k], dmk)
            dvn_acc[rows, :] = dvn
        dv, dg = _rms_bwd(dvn_acc[...], v, gv_ref[...])
        dgv_ref[...] += dg
        dp_ref[:, 2 * MIX_W:3 * MIX_W] = dv.astype(BF16)

        cg, xg, bg = c_ref[...], x_ref[...], b_ref[...]
        zc = cg * xg
        ze = jnp.concatenate([jnp.where(first, 0.0, cp_ref[...] * xp_ref[...]), zc], axis=0)
        z1, z2 = _shift_down(ze, 1)[CONV_HALO:], _shift_down(ze, 2)[CONV_HALO:]
        ddv = ddv_ref[...]
        y = cw_ref[0:1, :] * z2 + cw_ref[1:2, :] * z1 + cw_ref[2:3, :] * zc
        dp_ref[:, 3 * MIX_W:4 * MIX_W] = (ddv * y).astype(BF16)
        dy = ddv * bg
        dcw_ref[0:1, :] += jnp.sum(dy * z2, axis=0, keepdims=True)
        dcw_ref[1:2, :] += jnp.sum(dy * z1, axis=0, keepdims=True)
        dcw_ref[2:3, :] += jnp.sum(dy * zc, axis=0, keepdims=True)
        dye = jnp.concatenate([dy, jnp.where(last, 0.0, ddvn_ref[...] * bn_ref[...])], axis=0)
        dz = (cw_ref[2:3, :] * dye + cw_ref[1:2, :] * _shift_up(dye, 1) + cw_ref[0:1, :] * _shift_up(dye, 2))[:tc]
        dp_ref[:, 4 * MIX_W:5 * MIX_W] = (dz * xg).astype(BF16)
        dp_ref[:, 5 * MIX_W:6 * MIX_W] = (dz * cg).astype(BF16)

    grad = lambda halo: pl.BlockSpec((halo, MIX_W), lambda i: (jnp.minimum((i + 1) * (tc // halo), T // halo - 1), 0))
    out = pl.BlockSpec((tc, MIX_W), lambda i: (i, 0))
    return pl.pallas_call(
        body, name="mixers_bwd", grid=(T // tc,),
        in_specs=[cur(0), prev(0, POOL_HALO), cur(1), cur(2), cur(3), cur(4), cur(5), prev(4, CONV_HALO), prev(5, CONV_HALO),
                  nxt(3, CONV_HALO), out, grad(POOL_HALO), out, out, grad(CONV_HALO),
                  _full((MIX_W, MIX_W)), _full((1, MIX_W)), _full((1, MIX_W)), _full((GROUPS, SGU_BLOCK, SGU_BLOCK)),
                  _full((SGU_BLOCK, MIX_W)), _full((CONV_HALO, MIX_W))],
        out_specs=[pl.BlockSpec((tc, SEC_MIX), lambda i: (i, 0)), _full((MIX_W, MIX_W)), _full((1, MIX_W)), _full((1, MIX_W)),
                   _full((GROUPS, SGU_BLOCK, SGU_BLOCK)), _full((SGU_BLOCK, MIX_W)), _full((CONV_HALO, MIX_W))],
        out_shape=[jax.ShapeDtypeStruct((T, SEC_MIX), BF16), jax.ShapeDtypeStruct((MIX_W, MIX_W), F32),
                   jax.ShapeDtypeStruct((1, MIX_W), F32), jax.ShapeDtypeStruct((1, MIX_W), F32),
                   jax.ShapeDtypeStruct((GROUPS, SGU_BLOCK, SGU_BLOCK), F32), jax.ShapeDtypeStruct((SGU_BLOCK, MIX_W), F32),
                   jax.ShapeDtypeStruct((CONV_HALO, MIX_W), F32)],
        scratch_shapes=[pltpu.VMEM((tc, MIX_W), F32)],
        compiler_params=_params(1),
    )(P, P, P, P, P, P, P, P, P, P, dBm, dBm, dC, dDv, dDv, pool_bd, pool_scale, g_v, sgu_w, sgu_bias, conv_w)


def _attn_tile(S):
    return _tile(S, 256, 128)


def _attn_stats(O, dO, lse, S):
    T = O.shape[0]
    tq = _attn_tile(S)

    def body(o_ref, do_ref, lse_ref, lt_ref, dt_ref):
        lane = lax.broadcasted_iota(jnp.int32, (tq, HEAD_PAD), 1)
        delta = jnp.zeros((tq, HEAD_PAD), F32)
        for h in range(HEADS):
            hs = slice(h * HEAD_PAD, (h + 1) * HEAD_PAD)
            s = jnp.sum(o_ref[:, hs].astype(F32) * do_ref[:, hs].astype(F32), axis=-1, keepdims=True)
            delta = jnp.where(lane == h, s, delta)
        delta_t, lse_t = delta.T, lse_ref[...].T
        for h in range(HEADS):
            lt_ref[h, 0] = lse_t[h:h + 1, :]
            dt_ref[h, 0] = delta_t[h:h + 1, :]

    row = lambda w: pl.BlockSpec((tq, w), lambda i: (i, 0))
    out = pl.BlockSpec((HEADS, 1, 1, tq), lambda i: (0, i, 0, 0))
    return pl.pallas_call(
        body, name="attn_stats", grid=(T // tq,), in_specs=[row(MLA_W), row(MLA_W), row(HEAD_PAD)], out_specs=[out, out],
        out_shape=[jax.ShapeDtypeStruct((HEADS, T // tq, 1, tq), F32)] * 2, compiler_params=_params(1),
    )(O, dO, lse)


def _flash_bwd(Q, K, V, dO, lse_t, delta_t, S):
    T = Q.shape[0]
    n_seq = T // S
    tq = _attn_tile(S)
    nq = S // tq

    def body(k_ref, v_ref, q_ref, do_ref, lse_ref, dl_ref, dq_ref, dk_ref, dv_ref, s_s, dp_s, p_s, ds_s):
        kb = pl.program_id(1)

        @pl.when(kb == 0)
        def _():
            dq_ref[...] = jnp.zeros_like(dq_ref)

        dk_ref[...] = jnp.zeros_like(dk_ref)
        dv_ref[...] = jnp.zeros_like(dv_ref)

        def block(qi, masked):
            rows = pl.ds(pl.multiple_of(qi * tq, tq), tq)
            for h in range(HEADS):
                hs = slice(h * HEAD_PAD, (h + 1) * HEAD_PAD)
                s_s[h] = _dot_nt(k_ref[:, hs], q_ref[rows, hs])
                dp_s[h] = _dot_nt(v_ref[:, hs], do_ref[rows, hs])
            for h in range(HEADS):
                lse_row, dl_row = lse_ref[h, qi], dl_ref[h, qi]
                for r in range(0, tq, SOFTMAX_ROWS):
                    rs = slice(r, r + SOFTMAX_ROWS)
                    s = s_s[h, rs, :]
                    if masked:
                        key = r + lax.broadcasted_iota(jnp.int32, (SOFTMAX_ROWS, tq), 0)
                        query = lax.broadcasted_iota(jnp.int32, (SOFTMAX_ROWS, tq), 1)
                        s = jnp.where((query >> CHUNK_SHIFT) >= (key >> CHUNK_SHIFT), s, NEG_INF)
                    p = jnp.exp2(s - lse_row)
                    p_s[h, rs, :] = p.astype(BF16)
                    ds_s[h, rs, :] = (p * (dp_s[h, rs, :] - dl_row)).astype(BF16)
            for h in range(HEADS):
                hs = slice(h * HEAD_PAD, (h + 1) * HEAD_PAD)
                dv_ref[:, hs] += _dot(p_s[h], do_ref[rows, hs])
                dk_ref[:, hs] += _dot(ds_s[h], q_ref[rows, hs])
                dq_ref[rows, hs] += _dot_tn(ds_s[h], k_ref[:, hs])

        def full_block(qi, carry):
            block(qi, False)
            return carry

        block(kb, True)
        lax.fori_loop(kb + 1, nq, full_block, 0)
        dk_ref[...] = dk_ref[...] * LN2

    tile = pl.BlockSpec((tq, MLA_W), lambda b, i: (b * nq + i, 0))
    seq = pl.BlockSpec((S, MLA_W), lambda b, i: (b, 0))
    stat = pl.BlockSpec((HEADS, nq, 1, tq), lambda b, i: (0, b, 0, 0))
    return pl.pallas_call(
        body, name="flash_bwd", grid=(n_seq, nq),
        in_specs=[tile, tile, seq, seq, stat, stat],
        out_specs=[seq, tile, tile],
        out_shape=[jax.ShapeDtypeStruct((T, MLA_W), F32)] * 3,
        scratch_shapes=[pltpu.VMEM((HEADS, tq, tq), F32), pltpu.VMEM((HEADS, tq, tq), F32),
                        pltpu.VMEM((HEADS, tq, tq), BF16), pltpu.VMEM((HEADS, tq, tq), BF16)],
        compiler_params=_params(2),
    )(K, V, Q, dO, lse_t, delta_t)


def _mla_bwd_post(P, D, S, dQ, dK, dV, hq, hkv, g_cq, g_ckv, wuq, wukv, rope_c, rope_sa, rope_sb, l, bufs):
    T = P.shape[0]
    tm = _tile(S, 512)
    n_si = S // tm
    base = 4 * D
    names = ("w_uq", "w_ukv")

    def body(cq_ref, ckv_ref, dq_ref, dk_ref, dv_ref, hq_ref, hkv_ref, gq_ref, gkv_ref, wq_ref, wkv_ref, c_ref, sa_ref, sb_ref,
             *rest):
        dp_ref, dgq_ref, dgkv_ref, guq_ref, gukv_ref = rest[-5:]

        @pl.when(pl.program_id(0) == 0)
        def _():
            for r in (dgq_ref, dgkv_ref, guq_ref, gukv_ref):
                r[...] = jnp.zeros_like(r)

        c, sa, sb = c_ref[...], sa_ref[...], sb_ref[...]
        dq = _rope_t(dq_ref[...] * ATTN_SCALE, jnp.tile(c, (1, HEADS)), jnp.tile(sa, (1, HEADS)),
                     jnp.tile(sb, (1, HEADS))).astype(BF16)
        dcq, dg = _rms_bwd(_dot_nt(dq, wq_ref[...]), cq_ref[...], gq_ref[...])
        dgq_ref[...] += dg
        dp_ref[:, 0:Q_LORA] = dcq.astype(BF16)

        dk = dk_ref[...]
        dkb, dvb = dk.astype(BF16), dv_ref[...].astype(BF16)
        dkv = jnp.concatenate([p[:, j * CHIP_HEADS_W:(j + 1) * CHIP_HEADS_W] for j in range(N_CHIPS) for p in (dkb, dvb)], axis=1)
        for k in range(N_CHIPS):
            guq_ref[k] += _dot_tn(hq_ref[...], dq[:, k * CHIP_HEADS_W:(k + 1) * CHIP_HEADS_W])
            gukv_ref[k] += _dot_tn(hkv_ref[...], dkv[:, k * CHIP_KV:(k + 1) * CHIP_KV])
        dckv, dg = _rms_bwd(_dot_nt(dkv, wkv_ref[...]), ckv_ref[...], gkv_ref[...])
        dgkv_ref[...] += dg
        dp_ref[:, Q_LORA:Q_LORA + KV_LORA] = dckv.astype(BF16)

        dkr = dk[:, 0:HEAD_PAD]
        for h in range(1, HEADS):
            dkr = dkr + dk[:, h * HEAD_PAD:(h + 1) * HEAD_PAD]
        lane = lax.broadcasted_iota(jnp.int32, (1, HEAD_PAD), 1)
        rope_lanes = (lane >= QK_NOPE) & (lane < QK_NOPE + QK_ROPE)
        dp_ref[:, Q_LORA + KV_LORA:SEC_MLA] = jnp.where(rope_lanes, _rope_t(dkr, c, sa, sb), 0.0).astype(BF16)

    tab = pl.BlockSpec((tm, HEAD_PAD), lambda i: (i % n_si, 0))
    row = lambda w: pl.BlockSpec((tm, w), lambda i: (i, 0))
    wg = [_wgrad_out(l, DEPTH, Q_LORA, CHIP_HEADS_W), _wgrad_out(l, DEPTH, KV_LORA, CHIP_KV)]
    keep = [] if bufs is None else [bufs[n] for n in names]
    n_in = 14
    out = pl.pallas_call(
        body, name="mla_bwd_post", grid=(T // tm,),
        in_specs=[pl.BlockSpec((tm, Q_LORA), lambda i: (i, base // Q_LORA)),
                  pl.BlockSpec((tm, KV_LORA), lambda i: (i, (base + Q_LORA) // KV_LORA)),
                  row(MLA_W), row(MLA_W), row(MLA_W), row(Q_LORA), row(KV_LORA),
                  _full((1, Q_LORA)), _full((1, KV_LORA)), _full((Q_LORA, MLA_W)), _full((KV_LORA, 2 * MLA_W)), tab, tab, tab]
        + [pl.BlockSpec(memory_space=pl.ANY)] * len(keep),
        out_specs=[row(SEC_MLA), _full((1, Q_LORA)), _full((1, KV_LORA))] + [s for s, _ in wg],
        out_shape=[jax.ShapeDtypeStruct((T, SEC_MLA), BF16), jax.ShapeDtypeStruct((1, Q_LORA), F32),
                   jax.ShapeDtypeStruct((1, KV_LORA), F32)] + [s for _, s in wg],
        input_output_aliases={n_in + i: 3 + i for i in range(len(keep))},
        compiler_params=_params(1),
    )(P, P, dQ, dK, dV, hq, hkv, g_cq, g_ckv, wuq, wukv, rope_c, rope_sa, rope_sb, *keep)
    return out[:3], dict(zip(names, out[3:]))


def _proj_bwd(dx1, x, g, dPg, dPa, dPm, w_gates, w_mla, w_mix):
    T, D = x.shape
    tm = _tile(T, 256)

    def body(dx1_ref, x_ref, g_ref, dg_ref_in, da_ref, dm_ref, wg_ref, wa_ref, wm_ref, dx_ref, dg_ref):
        @pl.when(pl.program_id(0) == 0)
        def _():
            dg_ref[...] = jnp.zeros_like(dg_ref)

        dh = _dot_nt(dg_ref_in[...], wg_ref[...]) + _dot_nt(da_ref[...], wa_ref[...]) + _dot_nt(dm_ref[...], wm_ref[...])
        dx, dg = _rms_bwd(dh, x_ref[...], g_ref[...])
        dx_ref[...] = dx1_ref[...] + dx
        dg_ref[...] += dg

    row = lambda w: pl.BlockSpec((tm, w), lambda i: (i, 0))
    return pl.pallas_call(
        body, name="proj_bwd", grid=(T // tm,),
        in_specs=[row(D), row(D), _full((1, D)), row(4 * D), row(SEC_MLA), row(SEC_MIX),
                  _full((D, 4 * D)), _full((D, SEC_MLA)), _full((D, SEC_MIX))],
        out_specs=[row(D), _full((1, D))],
        out_shape=[jax.ShapeDtypeStruct((T, D), F32), jax.ShapeDtypeStruct((1, D), F32)],
        compiler_params=_params(1),
    )(dx1, x, g, dPg, dPa, dPm, w_gates, w_mla, w_mix)


def _adamw(w, g, m, v, name):
    R, C = w.shape
    tr = _tile(R, max(8, (1 << 19) // C))

    def body(w_ref, g_ref, m_ref, v_ref, d_ref, mo_ref, vo_ref):
        gv = g_ref[...]
        mn = ADAM_B1 * m_ref[...] + (1.0 - ADAM_B1) * gv
        vn = ADAM_B2 * v_ref[...] + (1.0 - ADAM_B2) * (gv * gv)
        mo_ref[...] = mn
        vo_ref[...] = vn
        m_hat = mn / (1.0 - ADAM_B1 ** ADAM_STEP)
        v_hat = vn / (1.0 - ADAM_B2 ** ADAM_STEP)
        d_ref[...] = -ADAM_LR * (m_hat / (jnp.sqrt(v_hat) + ADAM_EPS) + ADAM_WD * w_ref[...])

    blk = pl.BlockSpec((tr, C), lambda i: (i, 0))
    return pl.pallas_call(
        body, name=name, grid=(R // tr,), in_specs=[blk] * 4, out_specs=[blk] * 3,
        out_shape=[jax.ShapeDtypeStruct((R, C), F32)] * 3, compiler_params=_params(1),
    )(w, g, m, v)


def _rows_tile(rows, cols):
    return _tile(rows, max(16, (1 << 19) // cols), 16)


def _add_halves(G, recv, half, name):
    n, L, R, C = G.shape
    hr = R // 2
    tr = _rows_tile(hr, C)
    nb = hr // tr

    def body(half_ref, g_ref, r_ref, o_ref):
        o_ref[...] = (g_ref[...] + r_ref[...]).astype(BF16)

    grid_spec = pltpu.PrefetchScalarGridSpec(
        num_scalar_prefetch=1, grid=(n * L, nb),
        in_specs=[pl.BlockSpec((1, tr, C), lambda k, i, h: (k, h[0] * nb + i, 0)),
                  pl.BlockSpec((1, tr, C), lambda k, i, h: (k, i, 0))],
        out_specs=pl.BlockSpec((1, tr, C), lambda k, i, h: (k, i, 0)))
    out = pl.pallas_call(
        body, name="rs_add_halves_" + name, grid_spec=grid_spec,
        out_shape=jax.ShapeDtypeStruct((n * L, hr, C), BF16), compiler_params=_params(2),
    )(half.reshape(1).astype(jnp.int32), G.reshape(n * L, R, C), recv.reshape(n * L, hr, C))
    return out.reshape(n, L, hr, C)


def _sum_slots(H, slots, place, name):
    n, L, hr, C = slots.shape
    tr = _rows_tile(hr, C)
    nb = hr // tr

    def body(x_ref, y_ref, c_ref, own_ref, s1_ref, s2_ref, s3_ref, o_ref):
        o_ref[...] = ((own_ref[...].astype(F32) + s1_ref[...].astype(F32)) + s2_ref[...].astype(F32)) + s3_ref[...].astype(F32)

    def src(fx, fy):
        def index(l, j, px, py, pc):
            cx = px[0] + fx - 2 * fx * px[0]
            cy = py[0] + fy - 2 * fy * py[0]
            return (2 * cx + cy, l, j, 0)
        return pl.BlockSpec((None, None, tr, C), index)

    grid_spec = pltpu.PrefetchScalarGridSpec(
        num_scalar_prefetch=3, grid=(L, nb), in_specs=[src(0, 0), src(0, 1), src(1, 0), src(1, 1)],
        out_specs=pl.BlockSpec((None, tr, C), lambda l, j, px, py, pc: (l, pc[0] * nb + j, 0)))
    return pl.pallas_call(
        body, name="rs_sum_slots_" + name, grid_spec=grid_spec,
        out_shape=jax.ShapeDtypeStruct((L, 2 * hr, C), F32), compiler_params=_params(2),
    )(*place, H, slots, slots, slots)


HBM = pl.BlockSpec(memory_space=pltpu.HBM)


def _place():
    x, y, c = lax.axis_index("x"), lax.axis_index("y"), lax.axis_index("c")
    return x, y, c, 2 * x + y


def _chip_device(chip, c):
    return (chip // 2, chip % 2, c)


def _remote(src, dst, send_sem, recv_sem, to):
    return pltpu.make_async_remote_copy(src_ref=src, dst_ref=dst, send_sem=send_sem, recv_sem=recv_sem, device_id=to,
                                        device_id_type=MESH)


def _place_own(w, chip, name):
    L, R, C = w.shape
    tr = _rows_tile(R, C)

    def body(p_ref, w_ref, o_ref):
        o_ref[...] = w_ref[...].astype(BF16)

    grid_spec = pltpu.PrefetchScalarGridSpec(
        num_scalar_prefetch=1, grid=(L, R // tr), in_specs=[pl.BlockSpec((None, tr, C), lambda l, j, p: (l, j, 0))],
        out_specs=pl.BlockSpec((None, None, tr, C), lambda l, j, p: (l, p[0], j, 0)))
    return pl.pallas_call(
        body, name="place_" + name, grid_spec=grid_spec,
        out_shape=jax.ShapeDtypeStruct((L, N_CHIPS, R, C), BF16), compiler_params=_params(2),
    )(chip.reshape(1).astype(jnp.int32), w)


def _gather_weights(bufs):
    n = len(bufs)

    def body(*refs):
        o_refs = refs[n:2 * n]
        send_sems, recv_sems = refs[2 * n:]
        x, y, c, me = _place()
        sibling = (x, y, 1 - c)

        def copy(t, k, chip, half, to):
            hr = o_refs[t].shape[2] // 2
            block = o_refs[t].at[:, chip, pl.ds(half * hr, hr), :]
            return _remote(block, block, send_sems.at[6 * t + k], recv_sems.at[6 * t + k], to)

        first = [copy(t, d - 1, me, c, _chip_device(me ^ d, c)) for t in range(n) for d in (1, 2, 3)]
        for cp in first:
            cp.start()
        passed = []
        for t in range(n):
            for d in (1, 2, 3):
                copy(t, d - 1, me ^ d, c, sibling).wait_recv()
                passed.append(copy(t, 2 + d, me ^ d, c, sibling))
                passed[-1].start()
        for t in range(n):
            for d in (1, 2, 3):
                copy(t, 2 + d, me ^ d, 1 - c, sibling).wait_recv()
        for cp in first + passed:
            cp.wait_send()

    return pl.pallas_call(
        body, name="gather_weights", in_specs=[HBM] * n, out_specs=[HBM] * n,
        out_shape=[jax.ShapeDtypeStruct(b.shape, b.dtype) for b in bufs],
        input_output_aliases={t: t for t in range(n)},
        scratch_shapes=[pltpu.SemaphoreType.DMA((6 * n,)), pltpu.SemaphoreType.DMA((6 * n,))],
    )(*bufs)


def _exchange_halves(Gs):
    n = len(Gs)

    def body(*refs):
        g_refs, o_refs = refs[:n], refs[n:2 * n]
        send_sems, recv_sems = refs[2 * n:]
        x, y, c, _ = _place()
        copies = []
        for t in range(n):
            hr = g_refs[t].shape[2] // 2
            copies.append(_remote(g_refs[t].at[:, :, pl.ds((1 - c) * hr, hr), :], o_refs[t], send_sems.at[t], recv_sems.at[t],
                                  (x, y, 1 - c)))
            copies[-1].start()
        for cp in copies:
            cp.wait()

    return pl.pallas_call(
        body, name="rs_exchange_halves", in_specs=[HBM] * n, out_specs=[HBM] * n,
        out_shape=[jax.ShapeDtypeStruct(g.shape[:2] + (g.shape[2] // 2, g.shape[3]), g.dtype) for g in Gs],
        scratch_shapes=[pltpu.SemaphoreType.DMA((n,)), pltpu.SemaphoreType.DMA((n,))],
    )(*Gs)


def _scatter_partials(Hs):
    n = len(Hs)

    def body(*refs):
        h_refs, o_refs = refs[:n], refs[n:2 * n]
        send_sems, recv_sems = refs[2 * n:]
        x, y, c, me = _place()
        sends = [_remote(h_refs[t].at[me ^ d], o_refs[t].at[me], send_sems.at[3 * t + d - 1], recv_sems.at[3 * t + d - 1],
                         _chip_device(me ^ d, c)) for t in range(n) for d in (1, 2, 3)]
        for cp in sends:
            cp.start()
        for t in range(n):
            for d in (1, 2, 3):
                _remote(h_refs[t].at[me ^ d], o_refs[t].at[me ^ d], send_sems.at[3 * t + d - 1], recv_sems.at[3 * t + d - 1],
                        _chip_device(me ^ d, c)).wait_recv()
        for cp in sends:
            cp.wait_send()

    return pl.pallas_call(
        body, name="rs_scatter_partials", in_specs=[HBM] * n, out_specs=[HBM] * n,
        out_shape=[jax.ShapeDtypeStruct(h.shape, h.dtype) for h in Hs],
        scratch_shapes=[pltpu.SemaphoreType.DMA((3 * n,)), pltpu.SemaphoreType.DMA((3 * n,))],
    )(*Hs)


def _join_halves(bufs):
    n = len(bufs)

    def body(*refs):
        o_refs = refs[n:2 * n]
        send_sems, recv_sems = refs[2 * n:]
        x, y, c, _ = _place()

        def half(t, which):
            hr = o_refs[t].shape[1] // 2
            return o_refs[t].at[:, pl.ds(which * hr, hr), :]

        sends = [_remote(half(t, c), half(t, c), send_sems.at[t], recv_sems.at[t], (x, y, 1 - c)) for t in range(n)]
        for cp in sends:
            cp.start()
        for t in range(n):
            _remote(half(t, 1 - c), half(t, 1 - c), send_sems.at[t], recv_sems.at[t], (x, y, 1 - c)).wait_recv()
        for cp in sends:
            cp.wait_send()

    return pl.pallas_call(
        body, name="rs_join_halves", in_specs=[HBM] * n, out_specs=[HBM] * n,
        out_shape=[jax.ShapeDtypeStruct(b.shape, b.dtype) for b in bufs],
        input_output_aliases={t: t for t in range(n)},
        scratch_shapes=[pltpu.SemaphoreType.DMA((n,)), pltpu.SemaphoreType.DMA((n,))],
    )(*bufs)


def _all_reduce_small(v, name):
    R, C = v.shape

    def body(v_ref, o_ref, slots, send_sems, recv_sems):
        x, y, c, _ = _place()
        me = 4 * x + 2 * y + c
        slots[me] = v_ref[...]
        sends = []
        for d in range(1, 8):
            peer = me ^ d
            sends.append(pltpu.make_async_remote_copy(
                src_ref=v_ref, dst_ref=slots.at[me], send_sem=send_sems.at[d - 1], recv_sem=recv_sems.at[d - 1],
                device_id=(peer // 4, (peer // 2) % 2, peer % 2), device_id_type=MESH))
        for cp in sends:
            cp.start()
        for d in range(1, 8):
            peer = me ^ d
            pltpu.make_async_remote_copy(
                src_ref=v_ref, dst_ref=slots.at[peer], send_sem=send_sems.at[d - 1], recv_sem=recv_sems.at[d - 1],
                device_id=(peer // 4, (peer // 2) % 2, peer % 2), device_id_type=MESH).wait_recv()
        for cp in sends:
            cp.wait_send()
        acc = slots[0]
        for k in range(1, 8):
            acc = acc + slots[k]
        o_ref[...] = acc

    vm = pl.BlockSpec(memory_space=pltpu.VMEM)
    return pl.pallas_call(
        body, name=name, in_specs=[vm], out_specs=vm, out_shape=jax.ShapeDtypeStruct((R, C), F32),
        scratch_shapes=[pltpu.VMEM((8, R, C), F32), pltpu.SemaphoreType.DMA((7,)), pltpu.SemaphoreType.DMA((7,))],
    )(v)


SHARDED = ("w_in", "w_uq", "w_ukv", "conv_w", "w_br_a", "w_br_b", "w_br_c", "w_br_d", "w_out", "w_ffn_gate", "w_ffn_up",
           "w_ffn_down")
ROW_SHARDED = ("w_out", "w_ffn_down")
REPLICATED = ("g_pre_mix", "g_cq", "g_ckv", "pool_w", "pool_scale", "g_sgu_v", "sgu_w", "sgu_b", "g_post_mix", "g_pre_ffn",
              "g_post_ffn")
WEIGHTS = ("w_in", "g_pre_mix", "g_cq", "g_ckv", "w_uq", "w_ukv", "pool_w", "pool_scale", "g_sgu_v", "sgu_w", "sgu_b",
           "conv_w", "w_br_a", "w_br_b", "w_br_c", "w_br_d", "w_out", "g_post_mix", "g_pre_ffn", "w_ffn_gate", "w_ffn_up",
           "w_ffn_down", "g_post_ffn")
GATHERED = tuple(n for n in SHARDED if n != "conv_w")


def _unpack(packed, shapes):
    flat = packed.reshape(-1)
    out, o = [], 0
    for s in shapes:
        n = int(np.prod(s))
        out.append(flat[o:o + n].reshape(s))
        o += n
    return out


def _join_cols(g, l):
    return jnp.concatenate([g[l, k] for k in range(N_CHIPS)], axis=1)


def _pad_heads(w, real):
    lead = w.shape[:-1]
    w = w.reshape(lead + (HEADS, real))
    return jnp.pad(w, [(0, 0)] * len(lead) + [(0, 0), (0, HEAD_PAD - real)]).reshape(lead + (MLA_W,))


def _unpad_heads(w, real):
    lead = w.shape[:-1]
    return w.reshape(lead + (HEADS, HEAD_PAD))[..., :real].reshape(lead + (HEADS * real,))


IN_OFFSETS = {"cq": 0, "ckv": Q_LORA, "kr": Q_LORA + KV_LORA, "mix": Q_LORA + KV_LORA + QK_ROPE}
IN_GATES = Q_LORA + KV_LORA + QK_ROPE + SEC_MIX


def _pad_w_in(w):
    K = w.shape[0]
    z = lambda n: jnp.zeros((K, n), w.dtype)
    return jnp.concatenate([w[:, IN_GATES:], w[:, :IN_OFFSETS["kr"]], z(QK_NOPE), w[:, IN_OFFSETS["kr"]:IN_OFFSETS["mix"]],
                            z(HEAD_PAD - QK_NOPE - QK_ROPE), w[:, IN_OFFSETS["mix"]:IN_GATES]], axis=1)


def _unpad_w_in(d_gates, d_mla, d_mix):
    kr = d_mla[:, Q_LORA + KV_LORA + QK_NOPE:Q_LORA + KV_LORA + QK_NOPE + QK_ROPE]
    return jnp.concatenate([d_mla[:, :Q_LORA + KV_LORA], kr, d_mix, d_gates], axis=1)


def _rope_tables(S):
    half = QK_ROPE // 2
    inv = ROPE_THETA ** (-jnp.arange(0, QK_ROPE, 2, dtype=F32) / QK_ROPE)
    ang = jnp.arange(S, dtype=F32)[:, None] * inv[None, :]
    cos, sin = jnp.cos(ang), jnp.sin(ang)
    one, zero = jnp.ones((S, QK_NOPE), F32), jnp.zeros((S, half), F32)
    tail = HEAD_PAD - QK_NOPE - QK_ROPE
    c = jnp.concatenate([one, cos, cos, jnp.ones((S, tail), F32)], axis=1)
    sa = jnp.concatenate([0 * one, zero, sin, jnp.zeros((S, tail), F32)], axis=1)
    sb = jnp.concatenate([0 * one, -sin, zero, jnp.zeros((S, tail), F32)], axis=1)
    return c, sa, sb


def _layer_weights(gathered, full, l, D):
    w = {}
    w_in = _pad_w_in(_join_cols(gathered["w_in"], l))
    w["w_in"] = w_in
    w["w_in_gates"], w["w_in_mla"], w["w_in_mix"] = w_in[:, :4 * D], w_in[:, 4 * D:4 * D + SEC_MLA], w_in[:, 4 * D + SEC_MLA:]
    w["w_uq"] = _pad_heads(_join_cols(gathered["w_uq"], l), QK_NOPE + QK_ROPE)
    ukv = _join_cols(gathered["w_ukv"], l).reshape(KV_LORA, HEADS, QK_NOPE + V_HEAD)
    pad = ((0, 0), (0, 0), (0, HEAD_PAD - QK_NOPE))
    k_pad = jnp.pad(ukv[:, :, :QK_NOPE], pad).reshape(KV_LORA, N_CHIPS, CHIP_HEADS_W)
    v_pad = jnp.pad(ukv[:, :, QK_NOPE:], pad).reshape(KV_LORA, N_CHIPS, CHIP_HEADS_W)
    w["w_ukv"] = jnp.concatenate([k_pad, v_pad], axis=2).reshape(KV_LORA, 2 * MLA_W)
    w["w_br_a"] = jnp.pad(_join_cols(gathered["w_br_a"], l).reshape(HEADS, V_HEAD, D),
                          ((0, 0), (0, HEAD_PAD - V_HEAD), (0, 0))).reshape(MLA_W, D)
    for n in ("w_br_b", "w_br_c", "w_br_d"):
        w[n] = _join_cols(gathered[n], l)
    w["w_out"] = gathered["w_out"][l].reshape(D, D)
    for n in ("g_pre_mix", "g_cq", "g_ckv", "pool_scale", "g_sgu_v", "g_post_mix", "g_pre_ffn", "g_post_ffn"):
        w[n] = full[n][l].reshape(1, -1)
    pw = full["pool_w"][l]
    w["pool_bd"] = jax.scipy.linalg.block_diag(*[pw[g] for g in range(GROUPS)]).astype(BF16)
    w["sgu_w"] = full["sgu_w"][l]
    w["sgu_bias"] = jnp.repeat(full["sgu_b"][l].T, GROUP_DIM, axis=1)
    w["conv_w"] = jnp.pad(full["conv_w"][l].reshape(3, MIX_W), ((0, CONV_HALO - 3), (0, 0)))
    return w


def _layer_fwd(x, w, gathered, l, S, rope):
    D = x.shape[1]
    P, h = _norm_matmul(x, w["g_pre_mix"], w["w_in"], "proj_fwd")
    Q, K, V, hq, hkv = _mla_prep(P, D, w["g_cq"], w["g_ckv"], w["w_uq"], w["w_ukv"], *rope, S)
    A, lse = _flash_fwd(Q, K, V, S)
    Bm, C, Dv = _mixers_fwd(P, D, S, w["pool_bd"], w["pool_scale"], w["g_sgu_v"], w["sgu_w"], w["sgu_bias"], w["conv_w"])
    x1, merged, o = _merge_fwd(x, P, A, Bm, C, Dv, w["w_br_a"], w["w_br_b"], w["w_br_c"], w["w_br_d"], w["w_out"], w["g_post_mix"])
    x2, h2, gt, up, f = _ffn_fwd(x1, w["g_pre_ffn"], gathered["w_ffn_gate"], gathered["w_ffn_up"], gathered["w_ffn_down"],
                                 w["g_post_ffn"], l)
    saved = dict(x=x, P=P, h=h, Q=Q, K=K, V=V, hq=hq, hkv=hkv, A=A, lse=lse, Bm=Bm, C=C, Dv=Dv, x1=x1, merged=merged, o=o,
                 h2=h2, gt=gt, up=up, f=f)
    return x2, saved


def _layer_bwd(dx2, w, gathered, l, s, S, rope, bufs):
    D = dx2.shape[1]
    Fc = gathered["w_ffn_gate"].shape[3]
    g = {}

    def wgrad(n, a, b, rows, cols, a_mode, b_mode):
        bufs[n] = _wgrad_chip(bufs.get(n), l, DEPTH, a, b, rows, cols, a_mode, b_mode, "wgrad_" + n)

    dx1, df, dgt, dup, act, g["g_pre_ffn"], g["g_post_ffn"] = _ffn_bwd(
        dx2, s["x1"], s["f"], s["gt"], s["up"], w["g_pre_ffn"], gathered["w_ffn_gate"], gathered["w_ffn_up"],
        gathered["w_ffn_down"], w["g_post_ffn"], l)
    wgrad("w_ffn_down", act, df, Fc, D, "slab", "all")
    wgrad("w_ffn_gate", s["h2"], dgt, D, Fc, "all", "slab")
    wgrad("w_ffn_up", s["h2"], dup, D, Fc, "all", "slab")

    (dPg, dA, dBm, dC, dDv, g["g_post_mix"]), filled = _merge_bwd(
        dx1, s["o"], s["merged"], s["P"], s["A"], s["Bm"], s["C"], s["Dv"], w["w_br_a"], w["w_br_b"], w["w_br_c"], w["w_br_d"],
        w["w_out"], w["g_post_mix"], l, bufs if "w_out" in bufs else None)
    bufs.update(filled)

    dPm, d_pool_bd, g_ps, g_gv, g["sgu_w"], d_bias, d_cw = _mixers_bwd(
        s["P"], D, S, dBm, dC, dDv, w["pool_bd"], w["pool_scale"], w["g_sgu_v"], w["sgu_w"], w["sgu_bias"], w["conv_w"])
    g["pool_w"] = jnp.stack([d_pool_bd[k * GROUP_DIM:(k + 1) * GROUP_DIM, k * GROUP_DIM:(k + 1) * GROUP_DIM] for k in range(GROUPS)])
    g["pool_scale"], g["g_sgu_v"] = g_ps, g_gv
    g["sgu_b"] = d_bias.reshape(SGU_BLOCK, GROUPS, GROUP_DIM).sum(-1).T
    g["conv_w"] = d_cw[:3].reshape(3, 1, MIX_W)

    lse_t, delta_t = _attn_stats(s["A"], dA, s["lse"], S)
    dQ, dK, dV = _flash_bwd(s["Q"], s["K"], s["V"], dA, lse_t, delta_t, S)
    (dPa, g["g_cq"], g["g_ckv"]), filled = _mla_bwd_post(
        s["P"], D, S, dQ, dK, dV, s["hq"], s["hkv"], w["g_cq"], w["g_ckv"], w["w_uq"], w["w_ukv"], *rope, l,
        bufs if "w_uq" in bufs else None)
    bufs.update(filled)

    dx, g["g_pre_mix"] = _proj_bwd(dx1, s["x"], w["g_pre_mix"], dPg, dPa, dPm, w["w_in_gates"], w["w_in_mla"], w["w_in_mix"])
    d_w_in = _unpad_w_in(_matmul_tn(s["h"], dPg, "wgrad_in_gates"), _matmul_tn(s["h"], dPa, "wgrad_in_mla"),
                         _matmul_tn(s["h"], dPm, "wgrad_in_mix"))
    g["w_in"] = d_w_in.reshape(D, N_CHIPS, -1).transpose(1, 0, 2)
    for n in ("g_pre_mix", "g_cq", "g_ckv", "pool_scale", "g_sgu_v", "g_post_mix", "g_pre_ffn", "g_post_ffn"):
        g[n] = g[n].reshape(-1)
    return dx, g


SMALL = REPLICATED + ("conv_w",)


def _local_step(x, target, gathered, full):
    n_seq, S, D = x.shape
    rope = _rope_tables(S)
    xs = x.reshape(n_seq * S, D)
    weights, saved = [], []
    for l in range(DEPTH):
        w = _layer_weights(gathered, full, l, D)
        xs, s = _layer_fwd(xs, w, gathered, l, S, rope)
        weights.append(w)
        saved.append(s)
    loss_parts, dx = _loss_grad(xs, target.reshape(n_seq * S, D))
    grads, bufs = [None] * DEPTH, {}
    for l in reversed(range(DEPTH)):
        dx, grads[l] = _layer_bwd(dx, weights[l], gathered, l, saved[l], S, rope, bufs)
    bufs["w_in"] = jnp.stack([grads[l]["w_in"] for l in range(DEPTH)], axis=1)
    small = {n: jnp.stack([grads[l][n] for l in range(DEPTH)]) for n in SMALL}
    return loss_parts, dx.reshape(n_seq, S, D), bufs, small


def _unpad_reduced(n, r):
    L = r.shape[0]
    if n == "w_uq":
        return r.reshape(L, Q_LORA, 2, HEAD_PAD)[..., :QK_NOPE + QK_ROPE].reshape(L, Q_LORA, -1)
    if n == "w_ukv":
        r = r.reshape(L, KV_LORA, 2, 2, HEAD_PAD)[..., :QK_NOPE]
        return jnp.concatenate([r[:, :, 0], r[:, :, 1]], axis=-1).reshape(L, KV_LORA, -1)
    if n == "w_br_a":
        return r.reshape(L, HEADS, HEAD_PAD, -1)[:, :, :V_HEAD].reshape(L, HEADS * V_HEAD, -1)
    return r


def _small_rows(n):
    return -(-n // (8 * 128)) * 8


def _to_small(parts):
    flat = jnp.concatenate([p.reshape(-1) for p in parts])
    rows = _small_rows(flat.shape[0])
    return jnp.pad(flat, (0, rows * 128 - flat.shape[0])).reshape(rows, 128)


def kernel(x, w_in, g_pre_mix, g_cq, g_ckv, w_uq, w_ukv, pool_w, pool_scale, g_sgu_v, sgu_w, sgu_b, conv_w, w_br_a, w_br_b, w_br_c, w_br_d, w_out, g_post_mix, g_pre_ffn, w_ffn_gate, w_ffn_up, w_ffn_down, g_post_ffn, loss_target, m_w_in, m_g_pre_mix, m_g_cq, m_g_ckv, m_w_uq, m_w_ukv, m_pool_w, m_pool_scale, m_g_sgu_v, m_sgu_w, m_sgu_b, m_conv_w, m_w_br_a, m_w_br_b, m_w_br_c, m_w_br_d, m_w_out, m_g_post_mix, m_g_pre_ffn, m_w_ffn_gate, m_w_ffn_up, m_w_ffn_down, m_g_post_ffn, v_w_in, v_g_pre_mix, v_g_cq, v_g_ckv, v_w_uq, v_w_ukv, v_pool_w, v_pool_scale, v_g_sgu_v, v_sgu_w, v_sgu_b, v_conv_w, v_w_br_a, v_w_br_b, v_w_br_c, v_w_br_d, v_w_out, v_g_post_mix, v_g_pre_ffn, v_w_ffn_gate, v_w_ffn_up, v_w_ffn_down, v_g_post_ffn):
    local = dict(locals())
    W = {n: local[n] for n in WEIGHTS}
    M = {n: local["m_" + n] for n in WEIGHTS}
    V = {n: local["v_" + n] for n in WEIGHTS}
    chip = 2 * lax.axis_index("x") + lax.axis_index("y")
    core = lax.axis_index("c")

    gathered = dict(zip(GATHERED, _gather_weights([_place_own(W[n], chip, n) for n in GATHERED])))
    conv_shape = conv_w.shape
    conv_cols = conv_shape[-1]
    conv_full_shape = conv_shape[:-1] + (N_CHIPS * conv_cols,)
    placed = lax.dynamic_update_slice(jnp.zeros(conv_full_shape, F32), conv_w, (0, 0, 0, chip * conv_cols))
    n_conv = int(np.prod(conv_full_shape))
    conv_sum = _all_reduce_small(_to_small([placed]), "gather_conv_w")
    full = {n: W[n] for n in REPLICATED}
    full["conv_w"] = 0.5 * conv_sum.reshape(-1)[:n_conv].reshape(conv_full_shape)

    loss_parts, grad_x, bufs, small = _local_step(x, loss_target, gathered, full)
    loss = lax.psum(jnp.sum(loss_parts), ("x", "y", "c"))

    small_sum = _all_reduce_small(_to_small([small[n] for n in SMALL]), "reduce_small_grads")
    small_grads = dict(zip(SMALL, _unpack(small_sum, [small[n].shape for n in SMALL])))
    small_grads["conv_w"] = lax.dynamic_slice(small_grads["conv_w"], (0, 0, 0, chip * conv_cols), conv_shape)

    Gs = [bufs[n] for n in GATHERED]
    Hs = [_add_halves(g, r, core, n) for n, g, r in zip(GATHERED, Gs, _exchange_halves(Gs))]
    place = [lax.axis_index(a).reshape(1).astype(jnp.int32) for a in ("x", "y", "c")]
    halves = [_sum_slots(h, s, place, n) for n, h, s in zip(GATHERED, Hs, _scatter_partials(Hs))]
    shard_grads = {n: _unpad_reduced(n, r).reshape(W[n].shape) for n, r in zip(GATHERED, _join_halves(halves))}

    out_g, out_d, out_m, out_v = {}, {}, {}, {}
    for n in GATHERED:
        shp = W[n].shape
        flat = lambda a: a.reshape(-1, shp[-1])
        d, m2, v2 = _adamw(flat(W[n]), flat(shard_grads[n]), flat(M[n]), flat(V[n]), "adamw_" + n)
        out_g[n], out_d[n], out_m[n], out_v[n] = shard_grads[n], d.reshape(shp), m2.reshape(shp), v2.reshape(shp)
    rest_shapes = [W[n].shape for n in SMALL]
    d, m2, v2 = _adamw(_to_small([W[n] for n in SMALL]), _to_small([small_grads[n] for n in SMALL]),
                       _to_small([M[n] for n in SMALL]), _to_small([V[n] for n in SMALL]), "adamw_small")
    for n, dd, mm, vv in zip(SMALL, _unpack(d, rest_shapes), _unpack(m2, rest_shapes), _unpack(v2, rest_shapes)):
        out_g[n], out_d[n], out_m[n], out_v[n] = small_grads[n], dd, mm, vv

    return (loss, grad_x, *[out_g[n] for n in WEIGHTS], *[out_d[n] for n in WEIGHTS], *[out_m[n] for n in WEIGHTS],
            *[out_v[n] for n in WEIGHTS])
```

```python
import functools

import numpy as np
import jax
import jax.numpy as jnp
from jax import lax
from jax.experimental import pallas as pl
from jax.experimental.pallas import tpu as pltpu

F32 = jnp.float32
BF16 = jnp.bfloat16

EPS = 1e-6
NEG_INF = -1e30
DEPTH = 4
HEADS = 8
QK_NOPE = 64
QK_ROPE = 32
V_HEAD = 64
HEAD_PAD = 128
Q_LORA = 256
KV_LORA = 128
ROPE_THETA = 10000.0
POOL_WINDOWS = (2, 4, 8, 16)
GROUPS = 4
GROUP_DIM = 64
MIX_W = GROUPS * GROUP_DIM
POOL_HALO = 16
CONV_HALO = 8
SGU_BLOCK = 128
CHUNK = 64
CHUNK_SHIFT = 6
ACT_ROWS = 16
NORM_ROWS = 16
ROW_PARTS = 2
SOFTMAX_ROWS = 32
GROUP_SHIFT = 6
N_BRANCH = 4
MLA_W = HEADS * HEAD_PAD
N_CHIPS = 4
CHIP_HEADS_W = MLA_W // N_CHIPS
CHIP_KV = 2 * CHIP_HEADS_W
ATTN_SCALE = (QK_NOPE + QK_ROPE) ** -0.5
LOG2E = 1.4426950408889634
LN2 = 0.6931471805599453
SEC_MLA = Q_LORA + KV_LORA + HEAD_PAD
SEC_MIX = 6 * MIX_W

ADAM_LR = 0.001
ADAM_B1 = 0.9
ADAM_B2 = 0.999
ADAM_EPS = 1e-08
ADAM_WD = 0.01
ADAM_STEP = 10

VMEM_LIMIT = 56 * 1024 * 1024
MESH = pl.DeviceIdType.MESH


def _tile(n, pref, mult=8):
    t = min(n, pref)
    while t > 0:
        if n % t == 0 and t % mult == 0:
            return t
        t -= 1
    return n


def _params(n_axes):
    return pltpu.CompilerParams(dimension_semantics=("arbitrary",) * n_axes, vmem_limit_bytes=VMEM_LIMIT)


def _dot(a, b):
    return jnp.dot(a, b, preferred_element_type=F32)


def _dot_nt(a, b):
    return lax.dot_general(a, b, (((1,), (1,)), ((), ())), preferred_element_type=F32)


def _dot_tn(a, b):
    return lax.dot_general(a, b, (((0,), (0,)), ((), ())), preferred_element_type=F32)


def _rms_r(x):
    return lax.rsqrt(jnp.mean(x * x, axis=-1, keepdims=True) + EPS)


def _rms_bwd(dy, x, g):
    r = _rms_r(x)
    u = dy * g
    dx = r * u - x * (r * r * r * jnp.mean(u * x, axis=-1, keepdims=True))
    dg = jnp.sum(dy * x * r, axis=0, keepdims=True)
    return dx, dg


def _sigmoid(x):
    return 1.0 / (1.0 + jnp.exp(-x))


def _shift_down(a, k):
    return pltpu.roll(a, k, 0)


def _shift_up(a, k):
    return pltpu.roll(a, a.shape[0] - k, 0)


def _rope(x, c, sa, sb):
    w = x.shape[-1]
    return x * c + pltpu.roll(x, QK_ROPE // 2, 1) * sa + pltpu.roll(x, w - QK_ROPE // 2, 1) * sb


def _rope_t(d, c, sa, sb):
    w = d.shape[-1]
    return d * c + pltpu.roll(d * sa, w - QK_ROPE // 2, 1) + pltpu.roll(d * sb, QK_ROPE // 2, 1)


def _full(shape):
    return pl.BlockSpec(shape, lambda *_: (0,) * len(shape))


def _gather_ride(bufs, l, stage):
    def copies(o_refs, send_sems, recv_sems):
        x, y, c, me = _place()
        sends, arrivals = [], []
        for t, o in enumerate(o_refs):
            hr = o.shape[2] // 2
            for d in (1, 2, 3):
                sems = (send_sems.at[3 * t + d - 1], recv_sems.at[3 * t + d - 1])
                mine = o.at[l, me, pl.ds(c * hr, hr), :]
                theirs = o.at[l, me ^ d, pl.ds(c * hr, hr), :]
                other_half = o.at[l, me ^ d, pl.ds((1 - c) * hr, hr), :]
                if stage == "chips":
                    sends.append(_remote(mine, mine, *sems, _chip_device(me ^ d, c)))
                    arrivals.append(_remote(theirs, theirs, *sems, _chip_device(me ^ d, c)))
                else:
                    sends.append(_remote(theirs, theirs, *sems, (x, y, 1 - c)))
                    arrivals.append(_remote(other_half, other_half, *sems, (x, y, 1 - c)))
        return sends, arrivals

    return dict(bufs=list(bufs), copies=copies, n_sems=3 * len(bufs))


def _ride_count(ride):
    return len(ride["bufs"]) if ride else 0


def _ride_specs(ride):
    if not ride:
        return [], [], [], [], []
    n = len(ride["bufs"])
    anywhere = pl.BlockSpec(memory_space=pl.ANY)
    sems = pltpu.SemaphoreType.DMA((ride["n_sems"],))
    return ([anywhere] * n, [anywhere] * n, [jax.ShapeDtypeStruct(b.shape, b.dtype) for b in ride["bufs"]], [sems, sems],
            ride["bufs"])


def _ride_refs(ride, out_refs, sem_refs):
    return (ride, out_refs, sem_refs) if ride else None


def _ride_start(riding, first):
    if riding:
        ride, o_refs, (send_sems, recv_sems) = riding

        @pl.when(first)
        def _():
            for cp in ride["copies"](o_refs, send_sems, recv_sems)[0]:
                cp.start()


def _ride_finish(riding, last):
    if riding:
        ride, o_refs, (send_sems, recv_sems) = riding

        @pl.when(last)
        def _():
            sends, arrivals = ride["copies"](o_refs, send_sems, recv_sems)
            for cp in arrivals:
                cp.wait_recv()
            for cp in sends:
                cp.wait_send()


def _norm_matmul(x, g, w, name):
    T, K = x.shape
    N = w.shape[1]
    tm, tn = _tile(T, 512), _tile(N, 1536, 128)

    def body(x_ref, g_ref, w_ref, o_ref, h_ref):
        @pl.when(pl.program_id(1) == 0)
        def _():
            xv = x_ref[...]
            h_ref[...] = (xv * _rms_r(xv) * g_ref[...]).astype(BF16)

        o_ref[...] = _dot(h_ref[...], w_ref[...])

    return pl.pallas_call(
        body, name=name, grid=(T // tm, N // tn),
        in_specs=[pl.BlockSpec((tm, K), lambda i, j: (i, 0)), _full((1, K)), pl.BlockSpec((K, tn), lambda i, j: (0, j))],
        out_specs=[pl.BlockSpec((tm, tn), lambda i, j: (i, j)), pl.BlockSpec((tm, K), lambda i, j: (i, 0))],
        out_shape=[jax.ShapeDtypeStruct((T, N), F32), jax.ShapeDtypeStruct((T, K), BF16)],
        compiler_params=_params(2),
    )(x, g, w)


def _mla_prep(P, D, g_cq, g_ckv, wuq, wukv, rope_c, rope_sa, rope_sb, S):
    T = P.shape[0]
    tm = _tile(S, 512)
    n_si = S // tm
    base = 4 * D

    def body(cq_ref, ckv_ref, kr_ref, gq_ref, gkv_ref, wq_ref, wkv_ref, c_ref, sa_ref, sb_ref,
             q_ref, k_ref, v_ref, hq_ref, hkv_ref):
        c, sa, sb = c_ref[...], sa_ref[...], sb_ref[...]
        cq = cq_ref[...]
        hq = (cq * _rms_r(cq) * gq_ref[...]).astype(BF16)
        hq_ref[...] = hq
        q = _dot(hq, wq_ref[...])
        q = _rope(q, jnp.tile(c, (1, HEADS)), jnp.tile(sa, (1, HEADS)), jnp.tile(sb, (1, HEADS)))
        q_ref[...] = (q * (ATTN_SCALE * LOG2E)).astype(BF16)
        ckv = ckv_ref[...]
        hkv = (ckv * _rms_r(ckv) * gkv_ref[...]).astype(BF16)
        hkv_ref[...] = hkv
        kv = _dot(hkv, wkv_ref[...])
        kr = _rope(kr_ref[...], c, sa, sb)
        k_nope = jnp.concatenate([kv[:, j * CHIP_KV:j * CHIP_KV + CHIP_HEADS_W] for j in range(N_CHIPS)], axis=1)
        k_ref[...] = (k_nope + jnp.tile(kr, (1, HEADS))).astype(BF16)
        v = jnp.concatenate([kv[:, j * CHIP_KV + CHIP_HEADS_W:(j + 1) * CHIP_KV] for j in range(N_CHIPS)], axis=1)
        ones_lane = (lax.broadcasted_iota(jnp.int32, (1, MLA_W), 1) & (HEAD_PAD - 1)) == V_HEAD
        v_ref[...] = jnp.where(ones_lane, 1.0, v).astype(BF16)

    tab = pl.BlockSpec((tm, HEAD_PAD), lambda i: (i % n_si, 0))
    row = lambda w: pl.BlockSpec((tm, w), lambda i: (i, 0))
    return pl.pallas_call(
        body, name="mla_prep", grid=(T // tm,),
        in_specs=[pl.BlockSpec((tm, Q_LORA), lambda i: (i, base // Q_LORA)),
                  pl.BlockSpec((tm, KV_LORA), lambda i: (i, (base + Q_LORA) // KV_LORA)),
                  pl.BlockSpec((tm, HEAD_PAD), lambda i: (i, (base + Q_LORA + KV_LORA) // HEAD_PAD)),
                  _full((1, Q_LORA)), _full((1, KV_LORA)), _full((Q_LORA, MLA_W)), _full((KV_LORA, 2 * MLA_W)),
                  tab, tab, tab],
        out_specs=[row(MLA_W), row(MLA_W), row(MLA_W), row(Q_LORA), row(KV_LORA)],
        out_shape=[jax.ShapeDtypeStruct((T, MLA_W), BF16)] * 3
        + [jax.ShapeDtypeStruct((T, Q_LORA), BF16), jax.ShapeDtypeStruct((T, KV_LORA), BF16)],
        compiler_params=_params(1),
    )(P, P, P, g_cq, g_ckv, wuq, wukv, rope_c, rope_sa, rope_sb)


def _chunk_mask(tq, tk):
    row = lax.broadcasted_iota(jnp.int32, (tq, tk), 0)
    col = lax.broadcasted_iota(jnp.int32, (tq, tk), 1)
    return (row >> CHUNK_SHIFT) >= (col >> CHUNK_SHIFT)


def _flash_fwd(Q, K, V, S, ride=None):
    T = Q.shape[0]
    n_seq = T // S
    tq = _tile(S, 256, 128)
    nq = S // tq
    n_ride = _ride_count(ride)

    def body(q_ref, k_ref, v_ref, *rest):
        o_ref, lse_ref = rest[n_ride:n_ride + 2]
        m_s, acc_s, s_s, p_s, a_s = rest[2 * n_ride + 2:2 * n_ride + 7]
        riding = _ride_refs(ride, rest[n_ride + 2:2 * n_ride + 2], rest[2 * n_ride + 7:])
        _ride_start(riding, (pl.program_id(0) == 0) & (pl.program_id(1) == 0))
        qi = pl.program_id(1)
        m_s[...] = jnp.full(m_s.shape, NEG_INF, F32)
        acc_s[...] = jnp.zeros_like(acc_s)

        def block(kb, masked):
            rows = pl.ds(pl.multiple_of(kb * tq, tq), tq)
            for h in range(HEADS):
                hs = slice(h * HEAD_PAD, (h + 1) * HEAD_PAD)
                s_s[h] = _dot_nt(q_ref[:, hs], k_ref[rows, hs])
            def softmax_head(h):
                for r in range(0, tq, SOFTMAX_ROWS):
                    rs = slice(r, r + SOFTMAX_ROWS)
                    s = s_s[h, rs, :]
                    if masked:
                        row = r + lax.broadcasted_iota(jnp.int32, (SOFTMAX_ROWS, tq), 0)
                        col = lax.broadcasted_iota(jnp.int32, (SOFTMAX_ROWS, tq), 1)
                        s = jnp.where((row >> CHUNK_SHIFT) >= (col >> CHUNK_SHIFT), s, NEG_INF)
                    m_old = m_s[h, rs]
                    m_new = jnp.maximum(m_old, jnp.max(s, axis=-1, keepdims=True))
                    m_s[h, rs] = m_new
                    a_s[h, rs] = jnp.exp2(m_old - m_new)
                    for half in range(tq // HEAD_PAD):
                        cs = slice(half * HEAD_PAD, (half + 1) * HEAD_PAD)
                        p_s[h, rs, cs] = jnp.exp2(s[:, cs] - m_new).astype(BF16)

            for h in range(HEADS):
                softmax_head(h)
            for h in range(HEADS):
                hs = slice(h * HEAD_PAD, (h + 1) * HEAD_PAD)
                acc_s[:, hs] = a_s[h] * acc_s[:, hs] + _dot(p_s[h], v_ref[rows, hs])

        def full_block(kb, carry):
            block(kb, False)
            return carry

        lax.fori_loop(0, qi, full_block, 0)
        block(qi, True)
        lane = lax.broadcasted_iota(jnp.int32, (tq, HEAD_PAD), 1)
        lse_all = jnp.zeros((tq, HEAD_PAD), F32)
        for h in range(HEADS):
            hs = slice(h * HEAD_PAD, (h + 1) * HEAD_PAD)
            acc = acc_s[:, hs]
            l = jnp.sum(jnp.where(lane == V_HEAD, acc, 0.0), axis=-1, keepdims=True)
            o_ref[:, hs] = (acc / l).astype(BF16)
            lse_all = jnp.where(lane == h, m_s[h] + jnp.log2(l), lse_all)
        lse_ref[...] = lse_all
        _ride_finish(riding, (pl.program_id(0) == n_seq - 1) & (pl.program_id(1) == nq - 1))

    ride_in, ride_out, ride_shapes, ride_scratch, ride_args = _ride_specs(ride)
    out = pl.pallas_call(
        body, name="flash_fwd", grid=(n_seq, nq),
        in_specs=[pl.BlockSpec((tq, MLA_W), lambda b, i: (b * nq + i, 0)),
                  pl.BlockSpec((S, MLA_W), lambda b, i: (b, 0)), pl.BlockSpec((S, MLA_W), lambda b, i: (b, 0))] + ride_in,
        out_specs=[pl.BlockSpec((tq, MLA_W), lambda b, i: (b * nq + i, 0)),
                   pl.BlockSpec((tq, HEAD_PAD), lambda b, i: (b * nq + i, 0))] + ride_out,
        out_shape=[jax.ShapeDtypeStruct((T, MLA_W), BF16), jax.ShapeDtypeStruct((T, HEAD_PAD), F32)] + ride_shapes,
        input_output_aliases={3 + i: 2 + i for i in range(n_ride)},
        scratch_shapes=[pltpu.VMEM((HEADS, tq, HEAD_PAD), F32), pltpu.VMEM((tq, MLA_W), F32), pltpu.VMEM((HEADS, tq, tq), F32),
                        pltpu.VMEM((HEADS, tq, tq), BF16), pltpu.VMEM((HEADS, tq, HEAD_PAD), F32)] + ride_scratch,
        compiler_params=_params(2),
    )(Q, K, V, *ride_args)
    return out[0], out[1], list(out[2:])


def _lane_group():
    return lax.broadcasted_iota(jnp.int32, (1, MIX_W), 1) >> GROUP_SHIFT


def _by_group(a0, a1, a2, a3):
    g = _lane_group()
    return jnp.where(g == 0, a0, jnp.where(g == 1, a1, jnp.where(g == 2, a2, a3)))


def _pool_count(si, tc, rows):
    pos = si * tc + lax.broadcasted_iota(jnp.int32, (rows, MIX_W), 0)
    win = _by_group(*POOL_WINDOWS)
    return jnp.minimum(pos + 1, win).astype(F32)


def _pool_fwd(z, z_prev, si, tc):
    ze = jnp.concatenate([z_prev, z], axis=0)
    s1 = ze + _shift_down(ze, 1)
    s2 = s1 + _shift_down(s1, 2)
    s4 = s2 + _shift_down(s2, 4)
    s8 = s4 + _shift_down(s4, 8)
    win_sum = _by_group(s1, s2, s4, s8)[POOL_HALO:]
    return win_sum / _pool_count(si, tc, tc) - z


def _sgu_weights(w_ref):
    row = lax.broadcasted_iota(jnp.int32, (SGU_BLOCK, SGU_BLOCK), 0)
    col = lax.broadcasted_iota(jnp.int32, (SGU_BLOCK, SGU_BLOCK), 1)
    keep = (row >> CHUNK_SHIFT) >= (col >> CHUNK_SHIFT)
    return keep, [jnp.where(keep, w_ref[g], 0.0).astype(BF16) for g in range(GROUPS)]


def _sgu_mix(vn_blk, wm, bias):
    g = _lane_group()
    mixed = bias
    for k in range(GROUPS):
        mixed = mixed + jnp.where(g == k, _dot(wm[k], vn_blk), 0.0)
    return mixed


def _conv_fwd(z, z_prev, w_ref):
    ze = jnp.concatenate([z_prev, z], axis=0)
    y = w_ref[0:1, :] * _shift_down(ze, 2) + w_ref[1:2, :] * _shift_down(ze, 1) + w_ref[2:3, :] * ze
    return y[CONV_HALO:]


def _mix_specs(T, D, tc):
    base = (4 * D + SEC_MLA) // MIX_W
    cur = lambda k: pl.BlockSpec((tc, MIX_W), lambda i: (i, base + k))
    prev = lambda k, halo: pl.BlockSpec((halo, MIX_W), lambda i: (jnp.maximum(i * (tc // halo) - 1, 0), base + k))
    nxt = lambda k, halo: pl.BlockSpec((halo, MIX_W), lambda i: (jnp.minimum((i + 1) * (tc // halo), T // halo - 1), base + k))
    return cur, prev, nxt


def _mixers_fwd(P, D, S, pool_bd, pool_scale, g_v, sgu_w, sgu_bias, conv_w, ride=None):
    T = P.shape[0]
    tc = _tile(S, 512, SGU_BLOCK)
    n_si = S // tc
    cur, prev, _ = _mix_specs(T, D, tc)
    n_ride = _ride_count(ride)

    def body(z_ref, zp_ref, u_ref, v_ref, b_ref, c_ref, x_ref, cp_ref, xp_ref,
             pw_ref, ps_ref, gv_ref, sw_ref, sb_ref, cw_ref, *rest):
        ob_ref, oc_ref, od_ref = rest[n_ride:n_ride + 3]
        riding = _ride_refs(ride, rest[n_ride + 3:2 * n_ride + 3], rest[2 * n_ride + 3:])
        _ride_start(riding, pl.program_id(0) == 0)
        si = pl.program_id(0) % n_si
        first = si == 0
        z = z_ref[...]
        pooled = _pool_fwd(z, jnp.where(first, 0.0, zp_ref[...]), si, tc)
        ob_ref[...] = (_dot(pooled.astype(BF16), pw_ref[...]) * ps_ref[...]).astype(BF16)

        v = v_ref[...]
        vn = (v * _rms_r(v) * gv_ref[...]).astype(BF16)
        _, wm = _sgu_weights(sw_ref)
        for blk in range(tc // SGU_BLOCK):
            rows = slice(blk * SGU_BLOCK, (blk + 1) * SGU_BLOCK)
            oc_ref[rows, :] = (u_ref[rows, :] * _sgu_mix(vn[rows], wm, sb_ref[...])).astype(BF16)

        zc = c_ref[...] * x_ref[...]
        zc_prev = jnp.where(first, 0.0, cp_ref[...] * xp_ref[...])
        od_ref[...] = (b_ref[...] * _conv_fwd(zc, zc_prev, cw_ref)).astype(BF16)
        _ride_finish(riding, pl.program_id(0) == T // tc - 1)

    out = pl.BlockSpec((tc, MIX_W), lambda i: (i, 0))
    ride_in, ride_out, ride_shapes, ride_scratch, ride_args = _ride_specs(ride)
    res = pl.pallas_call(
        body, name="mixers_fwd", grid=(T // tc,),
        in_specs=[cur(0), prev(0, POOL_HALO), cur(1), cur(2), cur(3), cur(4), cur(5), prev(4, CONV_HALO), prev(5, CONV_HALO),
                  _full((MIX_W, MIX_W)), _full((1, MIX_W)), _full((1, MIX_W)), _full((GROUPS, SGU_BLOCK, SGU_BLOCK)),
                  _full((SGU_BLOCK, MIX_W)), _full((CONV_HALO, MIX_W))] + ride_in,
        out_specs=[out, out, out] + ride_out,
        out_shape=[jax.ShapeDtypeStruct((T, MIX_W), BF16)] * 3 + ride_shapes,
        input_output_aliases={15 + i: 3 + i for i in range(n_ride)},
        scratch_shapes=ride_scratch,
        compiler_params=_params(1),
    )(P, P, P, P, P, P, P, P, P, pool_bd, pool_scale, g_v, sgu_w, sgu_bias, conv_w, *ride_args)
    return res[0], res[1], res[2], list(res[3:])


def _merge_fwd(x, P, A, Bm, C, Dv, wa, wb, wc, wd, wout, g_post):
    T, D = x.shape
    tm = _tile(T, 256)

    def body(x_ref, lg_ref, a_ref, b_ref, c_ref, d_ref, wa_ref, wb_ref, wc_ref, wd_ref, wo_ref, g_ref,
             x1_ref, mg_ref, o_ref):
        merged = jnp.zeros((tm, D), F32)
        for k, (br, w) in enumerate(((a_ref, wa_ref), (b_ref, wb_ref), (c_ref, wc_ref), (d_ref, wd_ref))):
            merged = merged + _sigmoid(lg_ref[:, k * D:(k + 1) * D]) * _dot(br[...], w[...])
        mg = merged.astype(BF16)
        mg_ref[...] = mg
        o = _dot(mg, wo_ref[...])
        o_ref[...] = o
        x1_ref[...] = x_ref[...] + o * _rms_r(o) * g_ref[...]

    row = lambda w: pl.BlockSpec((tm, w), lambda i: (i, 0))
    return pl.pallas_call(
        body, name="merge_fwd", grid=(T // tm,),
        in_specs=[row(D), row(4 * D), row(MLA_W), row(MIX_W), row(MIX_W), row(MIX_W),
                  _full((MLA_W, D)), _full((MIX_W, D)), _full((MIX_W, D)), _full((MIX_W, D)), _full((D, D)), _full((1, D))],
        out_specs=[row(D), row(D), row(D)],
        out_shape=[jax.ShapeDtypeStruct((T, D), F32), jax.ShapeDtypeStruct((T, D), BF16), jax.ShapeDtypeStruct((T, D), F32)],
        compiler_params=_params(1),
    )(x, P, A, Bm, C, Dv, wa, wb, wc, wd, wout, g_post)


def _ffn_specs(T, D, Fc, l):
    tm = _tile(T, 512)
    row = pl.BlockSpec((tm, D), lambda i, j: (i, 0))
    col = pl.BlockSpec((None, tm, Fc), lambda i, j: (j, i, 0))
    w_in = pl.BlockSpec((None, None, D, Fc), lambda i, j: (l, j, 0, 0))
    w_out = pl.BlockSpec((None, None, Fc, D), lambda i, j: (l, j, 0, 0))
    return tm, row, col, w_in, w_out


def _ffn_fwd(x1, g_pre, wg, wu, wdn, g_post, l):
    T, D = x1.shape
    nf, Fc = wg.shape[1], wg.shape[3]
    tm, row, col, w_in, w_out = _ffn_specs(T, D, Fc, l)

    def body(x_ref, gp_ref, wg_ref, wu_ref, wd_ref, gq_ref, x2_ref, h_ref, gt_ref, up_ref, f_ref, gt_s, up_s, a_s):
        j = pl.program_id(1)

        @pl.when(j == 0)
        def _():
            for r in range(0, tm, NORM_ROWS):
                rs = slice(r, r + NORM_ROWS)
                xv = x_ref[rs, :]
                h_ref[rs, :] = (xv * _rms_r(xv) * gp_ref[...]).astype(BF16)
            f_ref[...] = jnp.zeros_like(f_ref)

        gt_s[...] = _dot(h_ref[...], wg_ref[...])
        up_s[...] = _dot(h_ref[...], wu_ref[...])
        for r in range(0, tm, ACT_ROWS):
            rs = slice(r, r + ACT_ROWS)
            gt, up = gt_s[rs, :], up_s[rs, :]
            gt_ref[rs, :] = gt.astype(BF16)
            up_ref[rs, :] = up.astype(BF16)
            a_s[rs, :] = (gt * _sigmoid(gt) * up).astype(BF16)
        f_ref[...] += _dot(a_s[...], wd_ref[...])

        @pl.when(j == nf - 1)
        def _():
            for r in range(0, tm, NORM_ROWS):
                rs = slice(r, r + NORM_ROWS)
                f = f_ref[rs, :]
                x2_ref[rs, :] = x_ref[rs, :] + f * _rms_r(f) * gq_ref[...]

    return pl.pallas_call(
        body, name="ffn_fwd", grid=(T // tm, nf),
        in_specs=[row, _full((1, D)), w_in, w_in, w_out, _full((1, D))],
        out_specs=[row, row, col, col, row],
        out_shape=[jax.ShapeDtypeStruct((T, D), F32), jax.ShapeDtypeStruct((T, D), BF16),
                   jax.ShapeDtypeStruct((nf, T, Fc), BF16), jax.ShapeDtypeStruct((nf, T, Fc), BF16),
                   jax.ShapeDtypeStruct((T, D), F32)],
        scratch_shapes=[pltpu.VMEM((tm, Fc), F32), pltpu.VMEM((tm, Fc), F32), pltpu.VMEM((tm, Fc), BF16)],
        compiler_params=_params(2),
    )(x1, g_pre, wg, wu, wdn, g_post)


def _loss_grad(y, target):
    T, D = y.shape
    tm = _tile(T, 512)

    def body(y_ref, t_ref, l_ref, dy_ref):
        @pl.when(pl.program_id(0) == 0)
        def _():
            l_ref[...] = jnp.zeros_like(l_ref)

        d = y_ref[...] - t_ref[...]
        dy_ref[...] = d * (1.0 / D)
        e = jnp.sum((d * d).reshape(tm // 8, 8, D), axis=0)
        part = e[:, 0:128]
        for k in range(1, D // 128):
            part = part + e[:, k * 128:(k + 1) * 128]
        l_ref[...] += part * (0.5 / D)

    row = pl.BlockSpec((tm, D), lambda i: (i, 0))
    return pl.pallas_call(
        body, name="loss_grad", grid=(T // tm,),
        in_specs=[row, row], out_specs=[_full((8, 128)), row],
        out_shape=[jax.ShapeDtypeStruct((8, 128), F32), jax.ShapeDtypeStruct((T, D), F32)],
        compiler_params=_params(1),
    )(y, target)


def _matmul_tn(a, b, name):
    T, M = a.shape
    N = b.shape[1]
    tm, tn, tk = _tile(M, 1024, 128), _tile(N, 1536, 128), _tile(T, 512)

    def body(a_ref, b_ref, o_ref):
        @pl.when(pl.program_id(2) == 0)
        def _():
            o_ref[...] = jnp.zeros_like(o_ref)

        o_ref[...] += _dot_tn(a_ref[...], b_ref[...])

    return pl.pallas_call(
        body, name=name, grid=(M // tm, N // tn, T // tk),
        in_specs=[pl.BlockSpec((tk, tm), lambda i, j, k: (k, i)), pl.BlockSpec((tk, tn), lambda i, j, k: (k, j))],
        out_specs=pl.BlockSpec((tm, tn), lambda i, j, k: (i, j)),
        out_shape=jax.ShapeDtypeStruct((M, N), F32),
        compiler_params=_params(3),
    )(a, b)


def _wgrad_chip(buf, l, n_layers, a, b, rows, cols, a_mode, b_mode, name):
    T = a.shape[-2]
    tk = _tile(T, 1024)

    def spec(mode, width):
        if mode == "all":
            return pl.BlockSpec((tk, width), lambda k, t: (t, 0))
        if mode == "cols":
            return pl.BlockSpec((tk, width), lambda k, t: (t, k))
        return pl.BlockSpec((None, tk, width), lambda k, t: (k, t, 0))

    def body(a_ref, b_ref, *rest):
        o_ref = rest[-1]

        @pl.when(pl.program_id(1) == 0)
        def _():
            o_ref[...] = jnp.zeros_like(o_ref)

        o_ref[...] += _dot_tn(a_ref[...], b_ref[...])

    keep = [] if buf is None else [buf]
    return pl.pallas_call(
        body, name=name, grid=(N_CHIPS, T // tk),
        in_specs=[spec(a_mode, rows), spec(b_mode, cols)] + [pl.BlockSpec(memory_space=pl.ANY)] * len(keep),
        out_specs=pl.BlockSpec((None, None, rows, cols), lambda k, t: (k, l, 0, 0)),
        out_shape=jax.ShapeDtypeStruct((N_CHIPS, n_layers, rows, cols), F32),
        input_output_aliases={2: 0} if keep else {},
        compiler_params=_params(2),
    )(a, b, *keep)


def _ffn_bwd(dx2, x1, f, gt, up, g_pre, wg, wu, wdn, g_post, l):
    T, D = x1.shape
    nf, Fc = wg.shape[1], wg.shape[3]
    tm, row, col, w_in, w_out = _ffn_specs(T, D, Fc, l)

    def body(dx2_ref, x1_ref, f_ref, gt_ref, up_ref, gp_ref, wg_ref, wu_ref, wd_ref, gq_ref,
             dx1_ref, df_ref, dgt_ref, dup_ref, a_ref, dgp_ref, dgq_ref, dh_acc, da_s):
        i, j = pl.program_id(0), pl.program_id(1)

        @pl.when((i == 0) & (j == 0))
        def _():
            dgp_ref[...] = jnp.zeros_like(dgp_ref)
            dgq_ref[...] = jnp.zeros_like(dgq_ref)

        @pl.when(j == 0)
        def _():
            dg_sum = jnp.zeros((1, D), F32)
            for r in range(0, tm, NORM_ROWS):
                rs = slice(r, r + NORM_ROWS)
                df, dg = _rms_bwd(dx2_ref[rs, :], f_ref[rs, :], gq_ref[...])
                df_ref[rs, :] = df.astype(BF16)
                dg_sum = dg_sum + dg
            dgq_ref[...] += dg_sum
            dh_acc[...] = jnp.zeros_like(dh_acc)

        parts = [slice(p * tm // ROW_PARTS, (p + 1) * tm // ROW_PARTS) for p in range(ROW_PARTS)]
        for ps in parts:
            da_s[ps, :] = _dot_nt(df_ref[ps, :], wd_ref[...])
        for ps in parts:
            for r in range(ps.start, ps.stop, ACT_ROWS):
                rs = slice(r, r + ACT_ROWS)
                da = da_s[rs, :]
                gt = gt_ref[rs, :].astype(F32)
                up = up_ref[rs, :].astype(F32)
                sig = _sigmoid(gt)
                silu = gt * sig
                dgt_ref[rs, :] = (da * up * (sig * (1.0 + gt * (1.0 - sig)))).astype(BF16)
                dup_ref[rs, :] = (da * silu).astype(BF16)
                a_ref[rs, :] = (silu * up).astype(BF16)
            dh_acc[ps, :] += _dot_nt(dgt_ref[ps, :], wg_ref[...]) + _dot_nt(dup_ref[ps, :], wu_ref[...])

        @pl.when(j == nf - 1)
        def _():
            dg_sum = jnp.zeros((1, D), F32)
            for r in range(0, tm, NORM_ROWS):
                rs = slice(r, r + NORM_ROWS)
                dx, dg = _rms_bwd(dh_acc[rs, :], x1_ref[rs, :], gp_ref[...])
                dx1_ref[rs, :] = dx2_ref[rs, :] + dx
                dg_sum = dg_sum + dg
            dgp_ref[...] += dg_sum

    return pl.pallas_call(
        body, name="ffn_bwd", grid=(T // tm, nf),
        in_specs=[row, row, row, col, col, _full((1, D)), w_in, w_in, w_out, _full((1, D))],
        out_specs=[row, row, col, col, col, _full((1, D)), _full((1, D))],
        out_shape=[jax.ShapeDtypeStruct((T, D), F32), jax.ShapeDtypeStruct((T, D), BF16)]
        + [jax.ShapeDtypeStruct((nf, T, Fc), BF16)] * 3
        + [jax.ShapeDtypeStruct((1, D), F32), jax.ShapeDtypeStruct((1, D), F32)],
        scratch_shapes=[pltpu.VMEM((tm, D), F32), pltpu.VMEM((tm, Fc), F32)],
        compiler_params=_params(2),
    )(dx2, x1, f, gt, up, g_pre, wg, wu, wdn, g_post)


def _wgrad_out(l, n_layers, rows, cols):
    spec = pl.BlockSpec((N_CHIPS, None, rows, cols), lambda *_: (0, l, 0, 0))
    return spec, jax.ShapeDtypeStruct((N_CHIPS, n_layers, rows, cols), F32)


def _merge_bwd(dx1, o, merged, P, A, Bm, C, Dv, wa, wb, wc, wd, wout, g_post, l, bufs):
    T, D = o.shape
    tm = _tile(T, 256)
    Dc = D // N_CHIPS
    names = ("w_out", "w_br_a", "w_br_b", "w_br_c", "w_br_d")

    def body(dx1_ref, o_ref, mg_ref, lg_ref, a_ref, b_ref, c_ref, d_ref, wa_ref, wb_ref, wc_ref, wd_ref, wo_ref, g_ref, *rest):
        dlg_ref, da_ref, db_ref, dc_ref, dd_ref, dg_ref, go_ref, ga_ref, gb_ref, gc_ref, gd_ref = rest[-11:]

        @pl.when(pl.program_id(0) == 0)
        def _():
            for r in (dg_ref, go_ref, ga_ref, gb_ref, gc_ref, gd_ref):
                r[...] = jnp.zeros_like(r)

        d_o, dg = _rms_bwd(dx1_ref[...], o_ref[...], g_ref[...])
        dg_ref[...] += dg
        d_o = d_o.astype(BF16)
        for k in range(N_CHIPS):
            go_ref[k] += _dot_tn(mg_ref[:, k * Dc:(k + 1) * Dc], d_o)
        dm = _dot_nt(d_o, wo_ref[...])
        branches = ((a_ref, wa_ref, da_ref, ga_ref), (b_ref, wb_ref, db_ref, gb_ref),
                    (c_ref, wc_ref, dc_ref, gc_ref), (d_ref, wd_ref, dd_ref, gd_ref))
        for j, (br, w, dbr_ref, gw_ref) in enumerate(branches):
            gate = _sigmoid(lg_ref[:, j * D:(j + 1) * D])
            y = _dot(br[...], w[...])
            dlg_ref[:, j * D:(j + 1) * D] = (dm * y * gate * (1.0 - gate)).astype(BF16)
            dy = (dm * gate).astype(BF16)
            dbr_ref[...] = _dot_nt(dy, w[...]).astype(dbr_ref.dtype)
            for k in range(N_CHIPS):
                gw_ref[k] += _dot_tn(br[...], dy[:, k * Dc:(k + 1) * Dc])

    row = lambda w: pl.BlockSpec((tm, w), lambda i: (i, 0))
    wg = [_wgrad_out(l, DEPTH, r, c) for r, c in ((Dc, D), (MLA_W, Dc), (MIX_W, Dc), (MIX_W, Dc), (MIX_W, Dc))]
    keep = [] if bufs is None else [bufs[n] for n in names]
    n_in = 14
    out = pl.pallas_call(
        body, name="merge_bwd", grid=(T // tm,),
        in_specs=[row(D), row(D), row(D), row(4 * D), row(MLA_W), row(MIX_W), row(MIX_W), row(MIX_W),
                  _full((MLA_W, D)), _full((MIX_W, D)), _full((MIX_W, D)), _full((MIX_W, D)), _full((D, D)), _full((1, D))]
        + [pl.BlockSpec(memory_space=pl.ANY)] * len(keep),
        out_specs=[row(4 * D), row(MLA_W), row(MIX_W), row(MIX_W), row(MIX_W), _full((1, D))] + [s for s, _ in wg],
        out_shape=[jax.ShapeDtypeStruct((T, 4 * D), BF16), jax.ShapeDtypeStruct((T, MLA_W), BF16)]
        + [jax.ShapeDtypeStruct((T, MIX_W), F32)] * 3 + [jax.ShapeDtypeStruct((1, D), F32)] + [s for _, s in wg],
        input_output_aliases={n_in + i: 6 + i for i in range(len(keep))},
        compiler_params=_params(1),
    )(dx1, o, merged, P, A, Bm, C, Dv, wa, wb, wc, wd, wout, g_post, *keep)
    return out[:6], dict(zip(names, out[6:]))


def _mixers_bwd(P, D, S, dBm, dC, dDv, pool_bd, pool_scale, g_v, sgu_w, sgu_bias, conv_w):
    T = P.shape[0]
    tc = _tile(S, 512, SGU_BLOCK)
    n_si = S // tc
    cur, prev, nxt = _mix_specs(T, D, tc)
    n_blk = tc // SGU_BLOCK

    def body(z_ref, zp_ref, u_ref, v_ref, b_ref, c_ref, x_ref, cp_ref, xp_ref, bn_ref,
             dbm_ref, dbmn_ref, dc_ref, ddv_ref, ddvn_ref,
             pw_ref, ps_ref, gv_ref, sw_ref, sb_ref, cw_ref,
             dp_ref, dpw_ref, dps_ref, dgv_ref, dsw_ref, dsb_ref, dcw_ref, dvn_acc):
        si = pl.program_id(0) % n_si
        first, last = si == 0, si == n_si - 1

        @pl.when(pl.program_id(0) == 0)
        def _():
            for r in (dpw_ref, dps_ref, dgv_ref, dsw_ref, dsb_ref, dcw_ref):
                r[...] = jnp.zeros_like(r)

        z = z_ref[...]
        pooled = _pool_fwd(z, jnp.where(first, 0.0, zp_ref[...]), si, tc).astype(BF16)
        dbm = dbm_ref[...]
        dps_ref[...] += jnp.sum(dbm * _dot(pooled, pw_ref[...]), axis=0, keepdims=True)
        dmix = (jnp.concatenate([dbm, jnp.where(last, 0.0, dbmn_ref[...])], axis=0) * ps_ref[...]).astype(BF16)
        dpw_ref[...] += _dot_tn(pooled, dmix[:tc])
        dpool = _dot_nt(dmix, pw_ref[...])
        e = dpool / _pool_count(si, tc, tc + POOL_HALO)
        f1 = e + _shift_up(e, 1)
        f2 = f1 + _shift_up(f1, 2)
        f4 = f2 + _shift_up(f2, 4)
        f8 = f4 + _shift_up(f4, 8)
        dp_ref[:, 0:MIX_W] = (_by_group(f1, f2, f4, f8)[:tc] - dpool[:tc]).astype(BF16)

        v = v_ref[...]
        vn = (v * _rms_r(v) * gv_ref[...]).astype(BF16)
        keep, wm = _sgu_weights(sw_ref)
        g = _lane_group()
        for blk in range(n_blk):
            rows = slice(blk * SGU_BLOCK, (blk + 1) * SGU_BLOCK)
            vb = vn[rows]
            dc = dc_ref[rows, :]
            dp_ref[rows, MIX_W:2 * MIX_W] = (dc * _sgu_mix(vb, wm, sb_ref[...])).astype(BF16)
            dmx = dc * u_ref[rows, :]
            dsb_ref[...] += dmx
            dvn = jnp.zeros((SGU_BLOCK, MIX_W), F32)
            for k in range(GROUPS):
                dmk = jnp.where(g == k, dmx, 0.0).astype(BF16)
                dsw_ref[k] += jnp.where(keep, _dot_nt(dmk, vb), 0.0)
                dvn = dvn + _dot_tn(wm[k], dmk)
            dvn_acc[rows, :] = dvn
        dv, dg = _rms_bwd(dvn_acc[...], v, gv_ref[...])
        dgv_ref[...] += dg
        dp_ref[:, 2 * MIX_W:3 * MIX_W] = dv.astype(BF16)

        cg, xg, bg = c_ref[...], x_ref[...], b_ref[...]
        zc = cg * xg
        ze = jnp.concatenate([jnp.where(first, 0.0, cp_ref[...] * xp_ref[...]), zc], axis=0)
        z1, z2 = _shift_down(ze, 1)[CONV_HALO:], _shift_down(ze, 2)[CONV_HALO:]
        ddv = ddv_ref[...]
        y = cw_ref[0:1, :] * z2 + cw_ref[1:2, :] * z1 + cw_ref[2:3, :] * zc
        dp_ref[:, 3 * MIX_W:4 * MIX_W] = (ddv * y).astype(BF16)
        dy = ddv * bg
        dcw_ref[0:1, :] += jnp.sum(dy * z2, axis=0, keepdims=True)
        dcw_ref[1:2, :] += jnp.sum(dy * z1, axis=0, keepdims=True)
        dcw_ref[2:3, :] += jnp.sum(dy * zc, axis=0, keepdims=True)
        dye = jnp.concatenate([dy, jnp.where(last, 0.0, ddvn_ref[...] * bn_ref[...])], axis=0)
        dz = (cw_ref[2:3, :] * dye + cw_ref[1:2, :] * _shift_up(dye, 1) + cw_ref[0:1, :] * _shift_up(dye, 2))[:tc]
        dp_ref[:, 4 * MIX_W:5 * MIX_W] = (dz * xg).astype(BF16)
        dp_ref[:, 5 * MIX_W:6 * MIX_W] = (dz * cg).astype(BF16)

    grad = lambda halo: pl.BlockSpec((halo, MIX_W), lambda i: (jnp.minimum((i + 1) * (tc // halo), T // halo - 1), 0))
    out = pl.BlockSpec((tc, MIX_W), lambda i: (i, 0))
    return pl.pallas_call(
        body, name="mixers_bwd", grid=(T // tc,),
        in_specs=[cur(0), prev(0, POOL_HALO), cur(1), cur(2), cur(3), cur(4), cur(5), prev(4, CONV_HALO), prev(5, CONV_HALO),
                  nxt(3, CONV_HALO), out, grad(POOL_HALO), out, out, grad(CONV_HALO),
                  _full((MIX_W, MIX_W)), _full((1, MIX_W)), _full((1, MIX_W)), _full((GROUPS, SGU_BLOCK, SGU_BLOCK)),
                  _full((SGU_BLOCK, MIX_W)), _full((CONV_HALO, MIX_W))],
        out_specs=[pl.BlockSpec((tc, SEC_MIX), lambda i: (i, 0)), _full((MIX_W, MIX_W)), _full((1, MIX_W)), _full((1, MIX_W)),
                   _full((GROUPS, SGU_BLOCK, SGU_BLOCK)), _full((SGU_BLOCK, MIX_W)), _full((CONV_HALO, MIX_W))],
        out_shape=[jax.ShapeDtypeStruct((T, SEC_MIX), BF16), jax.ShapeDtypeStruct((MIX_W, MIX_W), F32),
                   jax.ShapeDtypeStruct((1, MIX_W), F32), jax.ShapeDtypeStruct((1, MIX_W), F32),
                   jax.ShapeDtypeStruct((GROUPS, SGU_BLOCK, SGU_BLOCK), F32), jax.ShapeDtypeStruct((SGU_BLOCK, MIX_W), F32),
                   jax.ShapeDtypeStruct((CONV_HALO, MIX_W), F32)],
        scratch_shapes=[pltpu.VMEM((tc, MIX_W), F32)],
        compiler_params=_params(1),
    )(P, P, P, P, P, P, P, P, P, P, dBm, dBm, dC, dDv, dDv, pool_bd, pool_scale, g_v, sgu_w, sgu_bias, conv_w)


def _attn_tile(S):
    return _tile(S, 256, 128)


def _attn_stats(O, dO, lse, S):
    T = O.shape[0]
    tq = _attn_tile(S)

    def body(o_ref, do_ref, lse_ref, lt_ref, dt_ref):
        lane = lax.broadcasted_iota(jnp.int32, (tq, HEAD_PAD), 1)
        delta = jnp.zeros((tq, HEAD_PAD), F32)
        for h in range(HEADS):
            hs = slice(h * HEAD_PAD, (h + 1) * HEAD_PAD)
            s = jnp.sum(o_ref[:, hs].astype(F32) * do_ref[:, hs].astype(F32), axis=-1, keepdims=True)
            delta = jnp.where(lane == h, s, delta)
        delta_t, lse_t = delta.T, lse_ref[...].T
        for h in range(HEADS):
            lt_ref[h, 0] = lse_t[h:h + 1, :]
            dt_ref[h, 0] = delta_t[h:h + 1, :]

    row = lambda w: pl.BlockSpec((tq, w), lambda i: (i, 0))
    out = pl.BlockSpec((HEADS, 1, 1, tq), lambda i: (0, i, 0, 0))
    return pl.pallas_call(
        body, name="attn_stats", grid=(T // tq,), in_specs=[row(MLA_W), row(MLA_W), row(HEAD_PAD)], out_specs=[out, out],
        out_shape=[jax.ShapeDtypeStruct((HEADS, T // tq, 1, tq), F32)] * 2, compiler_params=_params(1),
    )(O, dO, lse)


def _flash_bwd(Q, K, V, dO, lse_t, delta_t, S):
    T = Q.shape[0]
    n_seq = T // S
    tq = _attn_tile(S)
    nq = S // tq

    def body(k_ref, v_ref, q_ref, do_ref, lse_ref, dl_ref, dq_ref, dk_ref, dv_ref, s_s, dp_s, p_s, ds_s):
        kb = pl.program_id(1)

        @pl.when(kb == 0)
        def _():
            dq_ref[...] = jnp.zeros_like(dq_ref)

        dk_ref[...] = jnp.zeros_like(dk_ref)
        dv_ref[...] = jnp.zeros_like(dv_ref)

        def block(qi, masked):
            rows = pl.ds(pl.multiple_of(qi * tq, tq), tq)
            for h in range(HEADS):
                hs = slice(h * HEAD_PAD, (h + 1) * HEAD_PAD)
                s_s[h] = _dot_nt(k_ref[:, hs], q_ref[rows, hs])
                dp_s[h] = _dot_nt(v_ref[:, hs], do_ref[rows, hs])
            for h in range(HEADS):
                lse_row, dl_row = lse_ref[h, qi], dl_ref[h, qi]
                for r in range(0, tq, SOFTMAX_ROWS):
                    rs = slice(r, r + SOFTMAX_ROWS)
                    s = s_s[h, rs, :]
                    if masked:
                        key = r + lax.broadcasted_iota(jnp.int32, (SOFTMAX_ROWS, tq), 0)
                        query = lax.broadcasted_iota(jnp.int32, (SOFTMAX_ROWS, tq), 1)
                        s = jnp.where((query >> CHUNK_SHIFT) >= (key >> CHUNK_SHIFT), s, NEG_INF)
                    p = jnp.exp2(s - lse_row)
                    p_s[h, rs, :] = p.astype(BF16)
                    ds_s[h, rs, :] = (p * (dp_s[h, rs, :] - dl_row)).astype(BF16)
            for h in range(HEADS):
                hs = slice(h * HEAD_PAD, (h + 1) * HEAD_PAD)
                dv_ref[:, hs] += _dot(p_s[h], do_ref[rows, hs])
                dk_ref[:, hs] += _dot(ds_s[h], q_ref[rows, hs])
                dq_ref[rows, hs] += _dot_tn(ds_s[h], k_ref[:, hs])

        def full_block(qi, carry):
            block(qi, False)
            return carry

        block(kb, True)
        lax.fori_loop(kb + 1, nq, full_block, 0)
        dk_ref[...] = dk_ref[...] * LN2

    tile = pl.BlockSpec((tq, MLA_W), lambda b, i: (b * nq + i, 0))
    seq = pl.BlockSpec((S, MLA_W), lambda b, i: (b, 0))
    stat = pl.BlockSpec((HEADS, nq, 1, tq), lambda b, i: (0, b, 0, 0))
    return pl.pallas_call(
        body, name="flash_bwd", grid=(n_seq, nq),
        in_specs=[tile, tile, seq, seq, stat, stat],
        out_specs=[seq, tile, tile],
        out_shape=[jax.ShapeDtypeStruct((T, MLA_W), F32)] * 3,
        scratch_shapes=[pltpu.VMEM((HEADS, tq, tq), F32), pltpu.VMEM((HEADS, tq, tq), F32),
                        pltpu.VMEM((HEADS, tq, tq), BF16), pltpu.VMEM((HEADS, tq, tq), BF16)],
        compiler_params=_params(2),
    )(K, V, Q, dO, lse_t, delta_t)


def _mla_bwd_post(P, D, S, dQ, dK, dV, hq, hkv, g_cq, g_ckv, wuq, wukv, rope_c, rope_sa, rope_sb, l, bufs):
    T = P.shape[0]
    tm = _tile(S, 512)
    n_si = S // tm
    base = 4 * D
    names = ("w_uq", "w_ukv")

    def body(cq_ref, ckv_ref, dq_ref, dk_ref, dv_ref, hq_ref, hkv_ref, gq_ref, gkv_ref, wq_ref, wkv_ref, c_ref, sa_ref, sb_ref,
             *rest):
        dp_ref, dgq_ref, dgkv_ref, guq_ref, gukv_ref = rest[-5:]

        @pl.when(pl.program_id(0) == 0)
        def _():
            for r in (dgq_ref, dgkv_ref, guq_ref, gukv_ref):
                r[...] = jnp.zeros_like(r)

        c, sa, sb = c_ref[...], sa_ref[...], sb_ref[...]
        dq = _rope_t(dq_ref[...] * ATTN_SCALE, jnp.tile(c, (1, HEADS)), jnp.tile(sa, (1, HEADS)),
                     jnp.tile(sb, (1, HEADS))).astype(BF16)
        dcq, dg = _rms_bwd(_dot_nt(dq, wq_ref[...]), cq_ref[...], gq_ref[...])
        dgq_ref[...] += dg
        dp_ref[:, 0:Q_LORA] = dcq.astype(BF16)

        dk = dk_ref[...]
        dkb, dvb = dk.astype(BF16), dv_ref[...].astype(BF16)
        dkv = jnp.concatenate([p[:, j * CHIP_HEADS_W:(j + 1) * CHIP_HEADS_W] for j in range(N_CHIPS) for p in (dkb, dvb)], axis=1)
        for k in range(N_CHIPS):
            guq_ref[k] += _dot_tn(hq_ref[...], dq[:, k * CHIP_HEADS_W:(k + 1) * CHIP_HEADS_W])
            gukv_ref[k] += _dot_tn(hkv_ref[...], dkv[:, k * CHIP_KV:(k + 1) * CHIP_KV])
        dckv, dg = _rms_bwd(_dot_nt(dkv, wkv_ref[...]), ckv_ref[...], gkv_ref[...])
        dgkv_ref[...] += dg
        dp_ref[:, Q_LORA:Q_LORA + KV_LORA] = dckv.astype(BF16)

        dkr = dk[:, 0:HEAD_PAD]
        for h in range(1, HEADS):
            dkr = dkr + dk[:, h * HEAD_PAD:(h + 1) * HEAD_PAD]
        lane = lax.broadcasted_iota(jnp.int32, (1, HEAD_PAD), 1)
        rope_lanes = (lane >= QK_NOPE) & (lane < QK_NOPE + QK_ROPE)
        dp_ref[:, Q_LORA + KV_LORA:SEC_MLA] = jnp.where(rope_lanes, _rope_t(dkr, c, sa, sb), 0.0).astype(BF16)

    tab = pl.BlockSpec((tm, HEAD_PAD), lambda i: (i % n_si, 0))
    row = lambda w: pl.BlockSpec((tm, w), lambda i: (i, 0))
    wg = [_wgrad_out(l, DEPTH, Q_LORA, CHIP_HEADS_W), _wgrad_out(l, DEPTH, KV_LORA, CHIP_KV)]
    keep = [] if bufs is None else [bufs[n] for n in names]
    n_in = 14
    out = pl.pallas_call(
        body, name="mla_bwd_post", grid=(T // tm,),
        in_specs=[pl.BlockSpec((tm, Q_LORA), lambda i: (i, base // Q_LORA)),
                  pl.BlockSpec((tm, KV_LORA), lambda i: (i, (base + Q_LORA) // KV_LORA)),
                  row(MLA_W), row(MLA_W), row(MLA_W), row(Q_LORA), row(KV_LORA),
                  _full((1, Q_LORA)), _full((1, KV_LORA)), _full((Q_LORA, MLA_W)), _full((KV_LORA, 2 * MLA_W)), tab, tab, tab]
        + [pl.BlockSpec(memory_space=pl.ANY)] * len(keep),
        out_specs=[row(SEC_MLA), _full((1, Q_LORA)), _full((1, KV_LORA))] + [s for s, _ in wg],
        out_shape=[jax.ShapeDtypeStruct((T, SEC_MLA), BF16), jax.ShapeDtypeStruct((1, Q_LORA), F32),
                   jax.ShapeDtypeStruct((1, KV_LORA), F32)] + [s for _, s in wg],
        input_output_aliases={n_in + i: 3 + i for i in range(len(keep))},
        compiler_params=_params(1),
    )(P, P, dQ, dK, dV, hq, hkv, g_cq, g_ckv, wuq, wukv, rope_c, rope_sa, rope_sb, *keep)
    return out[:3], dict(zip(names, out[3:]))


def _proj_bwd(dx1, x, g, dPg, dPa, dPm, w_gates, w_mla, w_mix):
    T, D = x.shape
    tm = _tile(T, 256)

    def body(dx1_ref, x_ref, g_ref, dg_ref_in, da_ref, dm_ref, wg_ref, wa_ref, wm_ref, dx_ref, dg_ref):
        @pl.when(pl.program_id(0) == 0)
        def _():
            dg_ref[...] = jnp.zeros_like(dg_ref)

        dh = _dot_nt(dg_ref_in[...], wg_ref[...]) + _dot_nt(da_ref[...], wa_ref[...]) + _dot_nt(dm_ref[...], wm_ref[...])
        dx, dg = _rms_bwd(dh, x_ref[...], g_ref[...])
        dx_ref[...] = dx1_ref[...] + dx
        dg_ref[...] += dg

    row = lambda w: pl.BlockSpec((tm, w), lambda i: (i, 0))
    return pl.pallas_call(
        body, name="proj_bwd", grid=(T // tm,),
        in_specs=[row(D), row(D), _full((1, D)), row(4 * D), row(SEC_MLA), row(SEC_MIX),
                  _full((D, 4 * D)), _full((D, SEC_MLA)), _full((D, SEC_MIX))],
        out_specs=[row(D), _full((1, D))],
        out_shape=[jax.ShapeDtypeStruct((T, D), F32), jax.ShapeDtypeStruct((1, D), F32)],
        compiler_params=_params(1),
    )(dx1, x, g, dPg, dPa, dPm, w_gates, w_mla, w_mix)


def _adamw(w, g, m, v, name):
    R, C = w.shape
    tr = _tile(R, max(8, (1 << 19) // C))

    def body(w_ref, g_ref, m_ref, v_ref, d_ref, mo_ref, vo_ref):
        gv = g_ref[...]
        mn = ADAM_B1 * m_ref[...] + (1.0 - ADAM_B1) * gv
        vn = ADAM_B2 * v_ref[...] + (1.0 - ADAM_B2) * (gv * gv)
        mo_ref[...] = mn
        vo_ref[...] = vn
        m_hat = mn / (1.0 - ADAM_B1 ** ADAM_STEP)
        v_hat = vn / (1.0 - ADAM_B2 ** ADAM_STEP)
        d_ref[...] = -ADAM_LR * (m_hat / (jnp.sqrt(v_hat) + ADAM_EPS) + ADAM_WD * w_ref[...])

    blk = pl.BlockSpec((tr, C), lambda i: (i, 0))
    return pl.pallas_call(
        body, name=name, grid=(R // tr,), in_specs=[blk] * 4, out_specs=[blk] * 3,
        out_shape=[jax.ShapeDtypeStruct((R, C), F32)] * 3, compiler_params=_params(1),
    )(w, g, m, v)


def _rows_tile(rows, cols):
    return _tile(rows, max(16, (1 << 19) // cols), 16)


def _add_halves(G, recv, half, name):
    n, L, R, C = G.shape
    hr = R // 2
    tr = _rows_tile(hr, C)
    nb = hr // tr

    def body(half_ref, g_ref, r_ref, o_ref):
        o_ref[...] = (g_ref[...] + r_ref[...]).astype(BF16)

    grid_spec = pltpu.PrefetchScalarGridSpec(
        num_scalar_prefetch=1, grid=(n * L, nb),
        in_specs=[pl.BlockSpec((1, tr, C), lambda k, i, h: (k, h[0] * nb + i, 0)),
                  pl.BlockSpec((1, tr, C), lambda k, i, h: (k, i, 0))],
        out_specs=pl.BlockSpec((1, tr, C), lambda k, i, h: (k, i, 0)))
    out = pl.pallas_call(
        body, name="rs_add_halves_" + name, grid_spec=grid_spec,
        out_shape=jax.ShapeDtypeStruct((n * L, hr, C), BF16), compiler_params=_params(2),
    )(half.reshape(1).astype(jnp.int32), G.reshape(n * L, R, C), recv.reshape(n * L, hr, C))
    return out.reshape(n, L, hr, C)


def _sum_slots(H, slots, place, name):
    n, L, hr, C = slots.shape
    tr = _rows_tile(hr, C)
    nb = hr // tr

    def body(x_ref, y_ref, c_ref, own_ref, s1_ref, s2_ref, s3_ref, o_ref):
        o_ref[...] = ((own_ref[...].astype(F32) + s1_ref[...].astype(F32)) + s2_ref[...].astype(F32)) + s3_ref[...].astype(F32)

    def src(fx, fy):
        def index(l, j, px, py, pc):
            cx = px[0] + fx - 2 * fx * px[0]
            cy = py[0] + fy - 2 * fy * py[0]
            return (2 * cx + cy, l, j, 0)
        return pl.BlockSpec((None, None, tr, C), index)

    grid_spec = pltpu.PrefetchScalarGridSpec(
        num_scalar_prefetch=3, grid=(L, nb), in_specs=[src(0, 0), src(0, 1), src(1, 0), src(1, 1)],
        out_specs=pl.BlockSpec((None, tr, C), lambda l, j, px, py, pc: (l, pc[0] * nb + j, 0)))
    return pl.pallas_call(
        body, name="rs_sum_slots_" + name, grid_spec=grid_spec,
        out_shape=jax.ShapeDtypeStruct((L, 2 * hr, C), F32), compiler_params=_params(2),
    )(*place, H, slots, slots, slots)


HBM = pl.BlockSpec(memory_space=pltpu.HBM)


def _place():
    x, y, c = lax.axis_index("x"), lax.axis_index("y"), lax.axis_index("c")
    return x, y, c, 2 * x + y


def _chip_device(chip, c):
    return (chip // 2, chip % 2, c)


def _remote(src, dst, send_sem, recv_sem, to):
    return pltpu.make_async_remote_copy(src_ref=src, dst_ref=dst, send_sem=send_sem, recv_sem=recv_sem, device_id=to,
                                        device_id_type=MESH)


def _place_own(w, chip, name):
    L, R, C = w.shape
    tr = _rows_tile(R, C)

    def body(p_ref, w_ref, o_ref):
        o_ref[...] = w_ref[...].astype(BF16)

    grid_spec = pltpu.PrefetchScalarGridSpec(
        num_scalar_prefetch=1, grid=(L, R // tr), in_specs=[pl.BlockSpec((None, tr, C), lambda l, j, p: (l, j, 0))],
        out_specs=pl.BlockSpec((None, None, tr, C), lambda l, j, p: (l, p[0], j, 0)))
    return pl.pallas_call(
        body, name="place_" + name, grid_spec=grid_spec,
        out_shape=jax.ShapeDtypeStruct((L, N_CHIPS, R, C), BF16), compiler_params=_params(2),
    )(chip.reshape(1).astype(jnp.int32), w)


def _gather_weights(bufs, l):
    n = len(bufs)

    def body(*refs):
        o_refs = refs[n:2 * n]
        send_sems, recv_sems = refs[2 * n:]
        x, y, c, me = _place()
        sibling = (x, y, 1 - c)

        def copy(t, k, chip, half, to):
            hr = o_refs[t].shape[2] // 2
            block = o_refs[t].at[l, chip, pl.ds(half * hr, hr), :]
            return _remote(block, block, send_sems.at[6 * t + k], recv_sems.at[6 * t + k], to)

        first = [copy(t, d - 1, me, c, _chip_device(me ^ d, c)) for t in range(n) for d in (1, 2, 3)]
        for cp in first:
            cp.start()
        passed = []
        for t in range(n):
            for d in (1, 2, 3):
                copy(t, d - 1, me ^ d, c, sibling).wait_recv()
                passed.append(copy(t, 2 + d, me ^ d, c, sibling))
                passed[-1].start()
        for t in range(n):
            for d in (1, 2, 3):
                copy(t, 2 + d, me ^ d, 1 - c, sibling).wait_recv()
        for cp in first + passed:
            cp.wait_send()

    return pl.pallas_call(
        body, name="gather_weights", in_specs=[HBM] * n, out_specs=[HBM] * n,
        out_shape=[jax.ShapeDtypeStruct(b.shape, b.dtype) for b in bufs],
        input_output_aliases={t: t for t in range(n)},
        scratch_shapes=[pltpu.SemaphoreType.DMA((6 * n,)), pltpu.SemaphoreType.DMA((6 * n,))],
    )(*bufs)


def _exchange_halves(Gs):
    n = len(Gs)

    def body(*refs):
        g_refs, o_refs = refs[:n], refs[n:2 * n]
        send_sems, recv_sems = refs[2 * n:]
        x, y, c, _ = _place()
        copies = []
        for t in range(n):
            hr = g_refs[t].shape[2] // 2
            copies.append(_remote(g_refs[t].at[:, :, pl.ds((1 - c) * hr, hr), :], o_refs[t], send_sems.at[t], recv_sems.at[t],
                                  (x, y, 1 - c)))
            copies[-1].start()
        for cp in copies:
            cp.wait()

    return pl.pallas_call(
        body, name="rs_exchange_halves", in_specs=[HBM] * n, out_specs=[HBM] * n,
        out_shape=[jax.ShapeDtypeStruct(g.shape[:2] + (g.shape[2] // 2, g.shape[3]), g.dtype) for g in Gs],
        scratch_shapes=[pltpu.SemaphoreType.DMA((n,)), pltpu.SemaphoreType.DMA((n,))],
    )(*Gs)


def _scatter_partials(Hs):
    n = len(Hs)

    def body(*refs):
        h_refs, o_refs = refs[:n], refs[n:2 * n]
        send_sems, recv_sems = refs[2 * n:]
        x, y, c, me = _place()
        sends = [_remote(h_refs[t].at[me ^ d], o_refs[t].at[me], send_sems.at[3 * t + d - 1], recv_sems.at[3 * t + d - 1],
                         _chip_device(me ^ d, c)) for t in range(n) for d in (1, 2, 3)]
        for cp in sends:
            cp.start()
        for t in range(n):
            for d in (1, 2, 3):
                _remote(h_refs[t].at[me ^ d], o_refs[t].at[me ^ d], send_sems.at[3 * t + d - 1], recv_sems.at[3 * t + d - 1],
                        _chip_device(me ^ d, c)).wait_recv()
        for cp in sends:
            cp.wait_send()

    return pl.pallas_call(
        body, name="rs_scatter_partials", in_specs=[HBM] * n, out_specs=[HBM] * n,
        out_shape=[jax.ShapeDtypeStruct(h.shape, h.dtype) for h in Hs],
        scratch_shapes=[pltpu.SemaphoreType.DMA((3 * n,)), pltpu.SemaphoreType.DMA((3 * n,))],
    )(*Hs)


def _join_halves(bufs):
    n = len(bufs)

    def body(*refs):
        o_refs = refs[n:2 * n]
        send_sems, recv_sems = refs[2 * n:]
        x, y, c, _ = _place()

        def half(t, which):
            hr = o_refs[t].shape[1] // 2
            return o_refs[t].at[:, pl.ds(which * hr, hr), :]

        sends = [_remote(half(t, c), half(t, c), send_sems.at[t], recv_sems.at[t], (x, y, 1 - c)) for t in range(n)]
        for cp in sends:
            cp.start()
        for t in range(n):
            _remote(half(t, 1 - c), half(t, 1 - c), send_sems.at[t], recv_sems.at[t], (x, y, 1 - c)).wait_recv()
        for cp in sends:
            cp.wait_send()

    return pl.pallas_call(
        body, name="rs_join_halves", in_specs=[HBM] * n, out_specs=[HBM] * n,
        out_shape=[jax.ShapeDtypeStruct(b.shape, b.dtype) for b in bufs],
        input_output_aliases={t: t for t in range(n)},
        scratch_shapes=[pltpu.SemaphoreType.DMA((n,)), pltpu.SemaphoreType.DMA((n,))],
    )(*bufs)


def _all_reduce_small(v, name):
    R, C = v.shape

    def body(v_ref, o_ref, slots, send_sems, recv_sems):
        x, y, c, _ = _place()
        me = 4 * x + 2 * y + c
        slots[me] = v_ref[...]
        sends = []
        for d in range(1, 8):
            peer = me ^ d
            sends.append(pltpu.make_async_remote_copy(
                src_ref=v_ref, dst_ref=slots.at[me], send_sem=send_sems.at[d - 1], recv_sem=recv_sems.at[d - 1],
                device_id=(peer // 4, (peer // 2) % 2, peer % 2), device_id_type=MESH))
        for cp in sends:
            cp.start()
        for d in range(1, 8):
            peer = me ^ d
            pltpu.make_async_remote_copy(
                src_ref=v_ref, dst_ref=slots.at[peer], send_sem=send_sems.at[d - 1], recv_sem=recv_sems.at[d - 1],
                device_id=(peer // 4, (peer // 2) % 2, peer % 2), device_id_type=MESH).wait_recv()
        for cp in sends:
            cp.wait_send()
        acc = slots[0]
        for k in range(1, 8):
            acc = acc + slots[k]
        o_ref[...] = acc

    vm = pl.BlockSpec(memory_space=pltpu.VMEM)
    return pl.pallas_call(
        body, name=name, in_specs=[vm], out_specs=vm, out_shape=jax.ShapeDtypeStruct((R, C), F32),
        scratch_shapes=[pltpu.VMEM((8, R, C), F32), pltpu.SemaphoreType.DMA((7,)), pltpu.SemaphoreType.DMA((7,))],
    )(v)


SHARDED = ("w_in", "w_uq", "w_ukv", "conv_w", "w_br_a", "w_br_b", "w_br_c", "w_br_d", "w_out", "w_ffn_gate", "w_ffn_up",
           "w_ffn_down")
ROW_SHARDED = ("w_out", "w_ffn_down")
REPLICATED = ("g_pre_mix", "g_cq", "g_ckv", "pool_w", "pool_scale", "g_sgu_v", "sgu_w", "sgu_b", "g_post_mix", "g_pre_ffn",
              "g_post_ffn")
WEIGHTS = ("w_in", "g_pre_mix", "g_cq", "g_ckv", "w_uq", "w_ukv", "pool_w", "pool_scale", "g_sgu_v", "sgu_w", "sgu_b",
           "conv_w", "w_br_a", "w_br_b", "w_br_c", "w_br_d", "w_out", "g_post_mix", "g_pre_ffn", "w_ffn_gate", "w_ffn_up",
           "w_ffn_down", "g_post_ffn")
GATHERED = tuple(n for n in SHARDED if n != "conv_w")


def _unpack(packed, shapes):
    flat = packed.reshape(-1)
    out, o = [], 0
    for s in shapes:
        n = int(np.prod(s))
        out.append(flat[o:o + n].reshape(s))
        o += n
    return out


def _join_cols(g, l):
    return jnp.concatenate([g[l, k] for k in range(N_CHIPS)], axis=1)


def _pad_heads(w, real):
    lead = w.shape[:-1]
    w = w.reshape(lead + (HEADS, real))
    return jnp.pad(w, [(0, 0)] * len(lead) + [(0, 0), (0, HEAD_PAD - real)]).reshape(lead + (MLA_W,))


def _unpad_heads(w, real):
    lead = w.shape[:-1]
    return w.reshape(lead + (HEADS, HEAD_PAD))[..., :real].reshape(lead + (HEADS * real,))


IN_OFFSETS = {"cq": 0, "ckv": Q_LORA, "kr": Q_LORA + KV_LORA, "mix": Q_LORA + KV_LORA + QK_ROPE}
IN_GATES = Q_LORA + KV_LORA + QK_ROPE + SEC_MIX


def _pad_w_in(w):
    K = w.shape[0]
    z = lambda n: jnp.zeros((K, n), w.dtype)
    return jnp.concatenate([w[:, IN_GATES:], w[:, :IN_OFFSETS["kr"]], z(QK_NOPE), w[:, IN_OFFSETS["kr"]:IN_OFFSETS["mix"]],
                            z(HEAD_PAD - QK_NOPE - QK_ROPE), w[:, IN_OFFSETS["mix"]:IN_GATES]], axis=1)


def _unpad_w_in(d_gates, d_mla, d_mix):
    kr = d_mla[:, Q_LORA + KV_LORA + QK_NOPE:Q_LORA + KV_LORA + QK_NOPE + QK_ROPE]
    return jnp.concatenate([d_mla[:, :Q_LORA + KV_LORA], kr, d_mix, d_gates], axis=1)


def _rope_tables(S):
    half = QK_ROPE // 2
    inv = ROPE_THETA ** (-jnp.arange(0, QK_ROPE, 2, dtype=F32) / QK_ROPE)
    ang = jnp.arange(S, dtype=F32)[:, None] * inv[None, :]
    cos, sin = jnp.cos(ang), jnp.sin(ang)
    one, zero = jnp.ones((S, QK_NOPE), F32), jnp.zeros((S, half), F32)
    tail = HEAD_PAD - QK_NOPE - QK_ROPE
    c = jnp.concatenate([one, cos, cos, jnp.ones((S, tail), F32)], axis=1)
    sa = jnp.concatenate([0 * one, zero, sin, jnp.zeros((S, tail), F32)], axis=1)
    sb = jnp.concatenate([0 * one, -sin, zero, jnp.zeros((S, tail), F32)], axis=1)
    return c, sa, sb


def _layer_weights(gathered, full, l, D):
    w = {}
    w_in = _pad_w_in(_join_cols(gathered["w_in"], l))
    w["w_in"] = w_in
    w["w_in_gates"], w["w_in_mla"], w["w_in_mix"] = w_in[:, :4 * D], w_in[:, 4 * D:4 * D + SEC_MLA], w_in[:, 4 * D + SEC_MLA:]
    w["w_uq"] = _pad_heads(_join_cols(gathered["w_uq"], l), QK_NOPE + QK_ROPE)
    ukv = _join_cols(gathered["w_ukv"], l).reshape(KV_LORA, HEADS, QK_NOPE + V_HEAD)
    pad = ((0, 0), (0, 0), (0, HEAD_PAD - QK_NOPE))
    k_pad = jnp.pad(ukv[:, :, :QK_NOPE], pad).reshape(KV_LORA, N_CHIPS, CHIP_HEADS_W)
    v_pad = jnp.pad(ukv[:, :, QK_NOPE:], pad).reshape(KV_LORA, N_CHIPS, CHIP_HEADS_W)
    w["w_ukv"] = jnp.concatenate([k_pad, v_pad], axis=2).reshape(KV_LORA, 2 * MLA_W)
    w["w_br_a"] = jnp.pad(_join_cols(gathered["w_br_a"], l).reshape(HEADS, V_HEAD, D),
                          ((0, 0), (0, HEAD_PAD - V_HEAD), (0, 0))).reshape(MLA_W, D)
    for n in ("w_br_b", "w_br_c", "w_br_d"):
        w[n] = _join_cols(gathered[n], l)
    w["w_out"] = gathered["w_out"][l].reshape(D, D)
    for n in ("g_pre_mix", "g_cq", "g_ckv", "pool_scale", "g_sgu_v", "g_post_mix", "g_pre_ffn", "g_post_ffn"):
        w[n] = full[n][l].reshape(1, -1)
    pw = full["pool_w"][l]
    w["pool_bd"] = jax.scipy.linalg.block_diag(*[pw[g] for g in range(GROUPS)]).astype(BF16)
    w["sgu_w"] = full["sgu_w"][l]
    w["sgu_bias"] = jnp.repeat(full["sgu_b"][l].T, GROUP_DIM, axis=1)
    w["conv_w"] = jnp.pad(full["conv_w"][l].reshape(3, MIX_W), ((0, CONV_HALO - 3), (0, 0)))
    return w


def _layer_fwd(x, w, gathered, l, S, rope, gather_next):
    D = x.shape[1]
    bufs = [gathered[n] for n in GATHERED]
    P, h = _norm_matmul(x, w["g_pre_mix"], w["w_in"], "proj_fwd")
    Q, K, V, hq, hkv = _mla_prep(P, D, w["g_cq"], w["g_ckv"], w["w_uq"], w["w_ukv"], *rope, S)
    A, lse, bufs = _flash_fwd(Q, K, V, S, _gather_ride(bufs, l + 1, "chips") if gather_next else None)
    Bm, C, Dv, bufs = _mixers_fwd(P, D, S, w["pool_bd"], w["pool_scale"], w["g_sgu_v"], w["sgu_w"], w["sgu_bias"], w["conv_w"],
                                  _gather_ride(bufs, l + 1, "cores") if gather_next else None)
    if gather_next:
        gathered = dict(zip(GATHERED, bufs))
    x1, merged, o = _merge_fwd(x, P, A, Bm, C, Dv, w["w_br_a"], w["w_br_b"], w["w_br_c"], w["w_br_d"], w["w_out"], w["g_post_mix"])
    x2, h2, gt, up, f = _ffn_fwd(x1, w["g_pre_ffn"], gathered["w_ffn_gate"], gathered["w_ffn_up"], gathered["w_ffn_down"],
                                 w["g_post_ffn"], l)
    saved = dict(x=x, P=P, h=h, Q=Q, K=K, V=V, hq=hq, hkv=hkv, A=A, lse=lse, Bm=Bm, C=C, Dv=Dv, x1=x1, merged=merged, o=o,
                 h2=h2, gt=gt, up=up, f=f)
    return x2, saved, gathered


def _layer_bwd(dx2, w, gathered, l, s, S, rope, bufs):
    D = dx2.shape[1]
    Fc = gathered["w_ffn_gate"].shape[3]
    g = {}

    def wgrad(n, a, b, rows, cols, a_mode, b_mode):
        bufs[n] = _wgrad_chip(bufs.get(n), l, DEPTH, a, b, rows, cols, a_mode, b_mode, "wgrad_" + n)

    dx1, df, dgt, dup, act, g["g_pre_ffn"], g["g_post_ffn"] = _ffn_bwd(
        dx2, s["x1"], s["f"], s["gt"], s["up"], w["g_pre_ffn"], gathered["w_ffn_gate"], gathered["w_ffn_up"],
        gathered["w_ffn_down"], w["g_post_ffn"], l)
    wgrad("w_ffn_down", act, df, Fc, D, "slab", "all")
    wgrad("w_ffn_gate", s["h2"], dgt, D, Fc, "all", "slab")
    wgrad("w_ffn_up", s["h2"], dup, D, Fc, "all", "slab")

    (dPg, dA, dBm, dC, dDv, g["g_post_mix"]), filled = _merge_bwd(
        dx1, s["o"], s["merged"], s["P"], s["A"], s["Bm"], s["C"], s["Dv"], w["w_br_a"], w["w_br_b"], w["w_br_c"], w["w_br_d"],
        w["w_out"], w["g_post_mix"], l, bufs if "w_out" in bufs else None)
    bufs.update(filled)

    dPm, d_pool_bd, g_ps, g_gv, g["sgu_w"], d_bias, d_cw = _mixers_bwd(
        s["P"], D, S, dBm, dC, dDv, w["pool_bd"], w["pool_scale"], w["g_sgu_v"], w["sgu_w"], w["sgu_bias"], w["conv_w"])
    g["pool_w"] = jnp.stack([d_pool_bd[k * GROUP_DIM:(k + 1) * GROUP_DIM, k * GROUP_DIM:(k + 1) * GROUP_DIM] for k in range(GROUPS)])
    g["pool_scale"], g["g_sgu_v"] = g_ps, g_gv
    g["sgu_b"] = d_bias.reshape(SGU_BLOCK, GROUPS, GROUP_DIM).sum(-1).T
    g["conv_w"] = d_cw[:3].reshape(3, 1, MIX_W)

    lse_t, delta_t = _attn_stats(s["A"], dA, s["lse"], S)
    dQ, dK, dV = _flash_bwd(s["Q"], s["K"], s["V"], dA, lse_t, delta_t, S)
    (dPa, g["g_cq"], g["g_ckv"]), filled = _mla_bwd_post(
        s["P"], D, S, dQ, dK, dV, s["hq"], s["hkv"], w["g_cq"], w["g_ckv"], w["w_uq"], w["w_ukv"], *rope, l,
        bufs if "w_uq" in bufs else None)
    bufs.update(filled)

    dx, g["g_pre_mix"] = _proj_bwd(dx1, s["x"], w["g_pre_mix"], dPg, dPa, dPm, w["w_in_gates"], w["w_in_mla"], w["w_in_mix"])
    d_w_in = _unpad_w_in(_matmul_tn(s["h"], dPg, "wgrad_in_gates"), _matmul_tn(s["h"], dPa, "wgrad_in_mla"),
                         _matmul_tn(s["h"], dPm, "wgrad_in_mix"))
    g["w_in"] = d_w_in.reshape(D, N_CHIPS, -1).transpose(1, 0, 2)
    for n in ("g_pre_mix", "g_cq", "g_ckv", "pool_scale", "g_sgu_v", "g_post_mix", "g_pre_ffn", "g_post_ffn"):
        g[n] = g[n].reshape(-1)
    return dx, g


SMALL = REPLICATED + ("conv_w",)


def _local_step(x, target, gathered, full):
    n_seq, S, D = x.shape
    rope = _rope_tables(S)
    xs = x.reshape(n_seq * S, D)
    weights, saved = [], []
    for l in range(DEPTH):
        w = _layer_weights(gathered, full, l, D)
        gather_next = l + 1 < DEPTH
        if gather_next:
            w, gathered = lax.optimization_barrier((w, gathered))
        xs, s, gathered = _layer_fwd(xs, w, gathered, l, S, rope, gather_next)
        weights.append(w)
        saved.append(s)
    loss_parts, dx = _loss_grad(xs, target.reshape(n_seq * S, D))
    grads, bufs = [None] * DEPTH, {}
    for l in reversed(range(DEPTH)):
        dx, grads[l] = _layer_bwd(dx, weights[l], gathered, l, saved[l], S, rope, bufs)
    bufs["w_in"] = jnp.stack([grads[l]["w_in"] for l in range(DEPTH)], axis=1)
    small = {n: jnp.stack([grads[l][n] for l in range(DEPTH)]) for n in SMALL}
    return loss_parts, dx.reshape(n_seq, S, D), bufs, small


def _unpad_reduced(n, r):
    L = r.shape[0]
    if n == "w_uq":
        return r.reshape(L, Q_LORA, 2, HEAD_PAD)[..., :QK_NOPE + QK_ROPE].reshape(L, Q_LORA, -1)
    if n == "w_ukv":
        r = r.reshape(L, KV_LORA, 2, 2, HEAD_PAD)[..., :QK_NOPE]
        return jnp.concatenate([r[:, :, 0], r[:, :, 1]], axis=-1).reshape(L, KV_LORA, -1)
    if n == "w_br_a":
        return r.reshape(L, HEADS, HEAD_PAD, -1)[:, :, :V_HEAD].reshape(L, HEADS * V_HEAD, -1)
    return r


def _small_rows(n):
    return -(-n // (8 * 128)) * 8


def _to_small(parts):
    flat = jnp.concatenate([p.reshape(-1) for p in parts])
    rows = _small_rows(flat.shape[0])
    return jnp.pad(flat, (0, rows * 128 - flat.shape[0])).reshape(rows, 128)


def kernel(x, w_in, g_pre_mix, g_cq, g_ckv, w_uq, w_ukv, pool_w, pool_scale, g_sgu_v, sgu_w, sgu_b, conv_w, w_br_a, w_br_b, w_br_c, w_br_d, w_out, g_post_mix, g_pre_ffn, w_ffn_gate, w_ffn_up, w_ffn_down, g_post_ffn, loss_target, m_w_in, m_g_pre_mix, m_g_cq, m_g_ckv, m_w_uq, m_w_ukv, m_pool_w, m_pool_scale, m_g_sgu_v, m_sgu_w, m_sgu_b, m_conv_w, m_w_br_a, m_w_br_b, m_w_br_c, m_w_br_d, m_w_out, m_g_post_mix, m_g_pre_ffn, m_w_ffn_gate, m_w_ffn_up, m_w_ffn_down, m_g_post_ffn, v_w_in, v_g_pre_mix, v_g_cq, v_g_ckv, v_w_uq, v_w_ukv, v_pool_w, v_pool_scale, v_g_sgu_v, v_sgu_w, v_sgu_b, v_conv_w, v_w_br_a, v_w_br_b, v_w_br_c, v_w_br_d, v_w_out, v_g_post_mix, v_g_pre_ffn, v_w_ffn_gate, v_w_ffn_up, v_w_ffn_down, v_g_post_ffn):
    local = dict(locals())
    W = {n: local[n] for n in WEIGHTS}
    M = {n: local["m_" + n] for n in WEIGHTS}
    V = {n: local["v_" + n] for n in WEIGHTS}
    chip = 2 * lax.axis_index("x") + lax.axis_index("y")
    core = lax.axis_index("c")

    gathered = dict(zip(GATHERED, _gather_weights([_place_own(W[n], chip, n) for n in GATHERED], 0)))
    conv_shape = conv_w.shape
    conv_cols = conv_shape[-1]
    conv_full_shape = conv_shape[:-1] + (N_CHIPS * conv_cols,)
    placed = lax.dynamic_update_slice(jnp.zeros(conv_full_shape, F32), conv_w, (0, 0, 0, chip * conv_cols))
    n_conv = int(np.prod(conv_full_shape))
    conv_sum = _all_reduce_small(_to_small([placed]), "gather_conv_w")
    full = {n: W[n] for n in REPLICATED}
    full["conv_w"] = 0.5 * conv_sum.reshape(-1)[:n_conv].reshape(conv_full_shape)

    loss_parts, grad_x, bufs, small = _local_step(x, loss_target, gathered, full)
    loss = lax.psum(jnp.sum(loss_parts), ("x", "y", "c"))

    small_sum = _all_reduce_small(_to_small([small[n] for n in SMALL]), "reduce_small_grads")
    small_grads = dict(zip(SMALL, _unpack(small_sum, [small[n].shape for n in SMALL])))
    small_grads["conv_w"] = lax.dynamic_slice(small_grads["conv_w"], (0, 0, 0, chip * conv_cols), conv_shape)

    Gs = [bufs[n] for n in GATHERED]
    Hs = [_add_halves(g, r, core, n) for n, g, r in zip(GATHERED, Gs, _exchange_halves(Gs))]
    place = [lax.axis_index(a).reshape(1).astype(jnp.int32) for a in ("x", "y", "c")]
    halves = [_sum_slots(h, s, place, n) for n, h, s in zip(GATHERED, Hs, _scatter_partials(Hs))]
    shard_grads = {n: _unpad_reduced(n, r).reshape(W[n].shape) for n, r in zip(GATHERED, _join_halves(halves))}

    out_g, out_d, out_m, out_v = {}, {}, {}, {}
    for n in GATHERED:
        shp = W[n].shape
        flat = lambda a: a.reshape(-1, shp[-1])
        d, m2, v2 = _adamw(flat(W[n]), flat(shard_grads[n]), flat(M[n]), flat(V[n]), "adamw_" + n)
        out_g[n], out_d[n], out_m[n], out_v[n] = shard_grads[n], d.reshape(shp), m2.reshape(shp), v2.reshape(shp)
    rest_shapes = [W[n].shape for n in SMALL]
    d, m2, v2 = _adamw(_to_small([W[n] for n in SMALL]), _to_small([small_grads[n] for n in SMALL]),
                       _to_small([M[n] for n in SMALL]), _to_small([V[n] for n in SMALL]), "adamw_small")
    for n, dd, mm, vv in zip(SMALL, _unpack(d, rest_shapes), _unpack(m2, rest_shapes), _unpack(v2, rest_shapes)):
        out_g[n], out_d[n], out_m[n], out_v[n] = small_grads[n], dd, mm, vv

    return (loss, grad_x, *[out_g[n] for n in WEIGHTS], *[out_d[n] for n in WEIGHTS], *[out_m[n] for n in WEIGHTS],
            *[out_v[n] for n in WEIGHTS])
```

```python
import functools

import numpy as np
import jax
import jax.numpy as jnp
from jax import lax
from jax.experimental import pallas as pl
from jax.experimental.pallas import tpu as pltpu

F32 = jnp.float32
BF16 = jnp.bfloat16

EPS = 1e-6
NEG_INF = -1e30
DEPTH = 4
HEADS = 8
QK_NOPE = 64
QK_ROPE = 32
V_HEAD = 64
HEAD_PAD = 128
Q_LORA = 256
KV_LORA = 128
ROPE_THETA = 10000.0
POOL_WINDOWS = (2, 4, 8, 16)
GROUPS = 4
GROUP_DIM = 64
MIX_W = GROUPS * GROUP_DIM
POOL_HALO = 16
CONV_HALO = 8
SGU_BLOCK = 128
CHUNK = 64
CHUNK_SHIFT = 6
ACT_ROWS = 16
NORM_ROWS = 16
ROW_PARTS = 2
SOFTMAX_ROWS = 32
GROUP_SHIFT = 6
N_BRANCH = 4
MLA_W = HEADS * HEAD_PAD
N_CHIPS = 4
CHIP_HEADS_W = MLA_W // N_CHIPS
CHIP_KV = 2 * CHIP_HEADS_W
ATTN_SCALE = (QK_NOPE + QK_ROPE) ** -0.5
LOG2E = 1.4426950408889634
LN2 = 0.6931471805599453
SEC_MLA = Q_LORA + KV_LORA + HEAD_PAD
SEC_MIX = 6 * MIX_W

ADAM_LR = 0.001
ADAM_B1 = 0.9
ADAM_B2 = 0.999
ADAM_EPS = 1e-08
ADAM_WD = 0.01
ADAM_STEP = 10

VMEM_LIMIT = 56 * 1024 * 1024
MESH = pl.DeviceIdType.MESH


def _tile(n, pref, mult=8):
    t = min(n, pref)
    while t > 0:
        if n % t == 0 and t % mult == 0:
            return t
        t -= 1
    return n


def _params(n_axes):
    return pltpu.CompilerParams(dimension_semantics=("arbitrary",) * n_axes, vmem_limit_bytes=VMEM_LIMIT)


def _dot(a, b):
    return jnp.dot(a, b, preferred_element_type=F32)


def _dot_nt(a, b):
    return lax.dot_general(a, b, (((1,), (1,)), ((), ())), preferred_element_type=F32)


def _dot_tn(a, b):
    return lax.dot_general(a, b, (((0,), (0,)), ((), ())), preferred_element_type=F32)


def _rms_r(x):
    return lax.rsqrt(jnp.mean(x * x, axis=-1, keepdims=True) + EPS)


def _rms_bwd(dy, x, g):
    r = _rms_r(x)
    u = dy * g
    dx = r * u - x * (r * r * r * jnp.mean(u * x, axis=-1, keepdims=True))
    dg = jnp.sum(dy * x * r, axis=0, keepdims=True)
    return dx, dg


def _sigmoid(x):
    return 1.0 / (1.0 + jnp.exp(-x))


def _shift_down(a, k):
    return pltpu.roll(a, k, 0)


def _shift_up(a, k):
    return pltpu.roll(a, a.shape[0] - k, 0)


def _rope(x, c, sa, sb):
    w = x.shape[-1]
    return x * c + pltpu.roll(x, QK_ROPE // 2, 1) * sa + pltpu.roll(x, w - QK_ROPE // 2, 1) * sb


def _rope_t(d, c, sa, sb):
    w = d.shape[-1]
    return d * c + pltpu.roll(d * sa, w - QK_ROPE // 2, 1) + pltpu.roll(d * sb, QK_ROPE // 2, 1)


def _full(shape):
    return pl.BlockSpec(shape, lambda *_: (0,) * len(shape))


def _gather_ride(bufs, l, stage):
    def copies(_, o_refs, send_sems, recv_sems):
        x, y, c, me = _place()
        sends, arrivals = [], []
        for t, o in enumerate(o_refs):
            hr = o.shape[2] // 2
            for d in (1, 2, 3):
                sems = (send_sems.at[3 * t + d - 1], recv_sems.at[3 * t + d - 1])
                mine = o.at[l, me, pl.ds(c * hr, hr), :]
                theirs = o.at[l, me ^ d, pl.ds(c * hr, hr), :]
                other_half = o.at[l, me ^ d, pl.ds((1 - c) * hr, hr), :]
                if stage == "chips":
                    sends.append(_remote(mine, mine, *sems, _chip_device(me ^ d, c)))
                    arrivals.append(_remote(theirs, theirs, *sems, _chip_device(me ^ d, c)))
                else:
                    sends.append(_remote(theirs, theirs, *sems, (x, y, 1 - c)))
                    arrivals.append(_remote(other_half, other_half, *sems, (x, y, 1 - c)))
        return sends, arrivals

    shapes = [jax.ShapeDtypeStruct(b.shape, b.dtype) for b in bufs]
    return dict(ins=list(bufs), outs=shapes, alias=True, copies=copies, n_sems=3 * len(bufs))


def _exchange_ride(Gs, spans):
    def copies(g_refs, o_refs, send_sems, recv_sems):
        x, y, c, _ = _place()
        cps = []
        for t, (g, o) in enumerate(zip(g_refs, o_refs)):
            hr = g.shape[2] // 2
            l0, l1 = spans[t]
            cps.append(_remote(g.at[:, pl.ds(l0, l1 - l0), pl.ds((1 - c) * hr, hr), :], o, send_sems.at[t], recv_sems.at[t],
                               (x, y, 1 - c)))
        return cps, cps

    shapes = [jax.ShapeDtypeStruct((g.shape[0], l1 - l0, g.shape[2] // 2, g.shape[3]), g.dtype) for g, (l0, l1) in zip(Gs, spans)]
    return dict(ins=list(Gs), outs=shapes, alias=False, copies=copies, n_sems=len(Gs))


def _scatter_ride(Hs):
    def copies(h_refs, o_refs, send_sems, recv_sems):
        x, y, c, me = _place()
        sends, arrivals = [], []
        for t, (h, o) in enumerate(zip(h_refs, o_refs)):
            for d in (1, 2, 3):
                sems = (send_sems.at[3 * t + d - 1], recv_sems.at[3 * t + d - 1])
                sends.append(_remote(h.at[me ^ d], o.at[me], *sems, _chip_device(me ^ d, c)))
                arrivals.append(_remote(h.at[me ^ d], o.at[me ^ d], *sems, _chip_device(me ^ d, c)))
        return sends, arrivals

    shapes = [jax.ShapeDtypeStruct(h.shape, h.dtype) for h in Hs]
    return dict(ins=list(Hs), outs=shapes, alias=False, copies=copies, n_sems=3 * len(Hs))


def _ride_specs(ride, n_in, n_out):
    if not ride:
        return [], [], [], [], [], {}
    anywhere = pl.BlockSpec(memory_space=pl.ANY)
    sems = pltpu.SemaphoreType.DMA((ride["n_sems"],))
    alias = {n_in + i: n_out + i for i in range(len(ride["ins"]))} if ride["alias"] else {}
    return [anywhere] * len(ride["ins"]), [anywhere] * len(ride["outs"]), list(ride["outs"]), [sems, sems], ride["ins"], alias


def _ride_split(ride, rest, n_out, n_scratch):
    a = len(ride["ins"]) if ride else 0
    b = a + n_out
    c = b + (len(ride["outs"]) if ride else 0)
    d = c + n_scratch
    riding = (ride, rest[:a], rest[b:c], rest[d:]) if ride else None
    return rest[a:b], rest[c:d], riding


def _ride_start(riding, first):
    if riding:
        ride, in_refs, out_refs, (send_sems, recv_sems) = riding

        @pl.when(first)
        def _():
            for cp in ride["copies"](in_refs, out_refs, send_sems, recv_sems)[0]:
                cp.start()


def _ride_finish(riding, last):
    if riding:
        ride, in_refs, out_refs, (send_sems, recv_sems) = riding

        @pl.when(last)
        def _():
            sends, arrivals = ride["copies"](in_refs, out_refs, send_sems, recv_sems)
            for cp in arrivals:
                cp.wait_recv()
            for cp in sends:
                cp.wait_send()


def _run_ride(ride, name):
    def body(*refs):
        _, _, (_, in_refs, out_refs, (send_sems, recv_sems)) = _ride_split(ride, refs, 0, 0)
        sends, arrivals = ride["copies"](in_refs, out_refs, send_sems, recv_sems)
        for cp in sends:
            cp.start()
        for cp in arrivals:
            cp.wait_recv()
        for cp in sends:
            cp.wait_send()

    in_specs, out_specs, out_shapes, scratch, operands, alias = _ride_specs(ride, 0, 0)
    return list(pl.pallas_call(body, name=name, in_specs=in_specs, out_specs=out_specs, out_shape=out_shapes,
                               input_output_aliases=alias, scratch_shapes=scratch)(*operands))


def _norm_matmul(x, g, w, name):
    T, K = x.shape
    N = w.shape[1]
    tm, tn = _tile(T, 512), _tile(N, 1536, 128)

    def body(x_ref, g_ref, w_ref, o_ref, h_ref):
        @pl.when(pl.program_id(1) == 0)
        def _():
            xv = x_ref[...]
            h_ref[...] = (xv * _rms_r(xv) * g_ref[...]).astype(BF16)

        o_ref[...] = _dot(h_ref[...], w_ref[...])

    return pl.pallas_call(
        body, name=name, grid=(T // tm, N // tn),
        in_specs=[pl.BlockSpec((tm, K), lambda i, j: (i, 0)), _full((1, K)), pl.BlockSpec((K, tn), lambda i, j: (0, j))],
        out_specs=[pl.BlockSpec((tm, tn), lambda i, j: (i, j)), pl.BlockSpec((tm, K), lambda i, j: (i, 0))],
        out_shape=[jax.ShapeDtypeStruct((T, N), F32), jax.ShapeDtypeStruct((T, K), BF16)],
        compiler_params=_params(2),
    )(x, g, w)


def _mla_prep(P, D, g_cq, g_ckv, wuq, wukv, rope_c, rope_sa, rope_sb, S):
    T = P.shape[0]
    tm = _tile(S, 512)
    n_si = S // tm
    base = 4 * D

    def body(cq_ref, ckv_ref, kr_ref, gq_ref, gkv_ref, wq_ref, wkv_ref, c_ref, sa_ref, sb_ref,
             q_ref, k_ref, v_ref, hq_ref, hkv_ref):
        c, sa, sb = c_ref[...], sa_ref[...], sb_ref[...]
        cq = cq_ref[...]
        hq = (cq * _rms_r(cq) * gq_ref[...]).astype(BF16)
        hq_ref[...] = hq
        q = _dot(hq, wq_ref[...])
        q = _rope(q, jnp.tile(c, (1, HEADS)), jnp.tile(sa, (1, HEADS)), jnp.tile(sb, (1, HEADS)))
        q_ref[...] = (q * (ATTN_SCALE * LOG2E)).astype(BF16)
        ckv = ckv_ref[...]
        hkv = (ckv * _rms_r(ckv) * gkv_ref[...]).astype(BF16)
        hkv_ref[...] = hkv
        kv = _dot(hkv, wkv_ref[...])
        kr = _rope(kr_ref[...], c, sa, sb)
        k_nope = jnp.concatenate([kv[:, j * CHIP_KV:j * CHIP_KV + CHIP_HEADS_W] for j in range(N_CHIPS)], axis=1)
        k_ref[...] = (k_nope + jnp.tile(kr, (1, HEADS))).astype(BF16)
        v = jnp.concatenate([kv[:, j * CHIP_KV + CHIP_HEADS_W:(j + 1) * CHIP_KV] for j in range(N_CHIPS)], axis=1)
        ones_lane = (lax.broadcasted_iota(jnp.int32, (1, MLA_W), 1) & (HEAD_PAD - 1)) == V_HEAD
        v_ref[...] = jnp.where(ones_lane, 1.0, v).astype(BF16)

    tab = pl.BlockSpec((tm, HEAD_PAD), lambda i: (i % n_si, 0))
    row = lambda w: pl.BlockSpec((tm, w), lambda i: (i, 0))
    return pl.pallas_call(
        body, name="mla_prep", grid=(T // tm,),
        in_specs=[pl.BlockSpec((tm, Q_LORA), lambda i: (i, base // Q_LORA)),
                  pl.BlockSpec((tm, KV_LORA), lambda i: (i, (base + Q_LORA) // KV_LORA)),
                  pl.BlockSpec((tm, HEAD_PAD), lambda i: (i, (base + Q_LORA + KV_LORA) // HEAD_PAD)),
                  _full((1, Q_LORA)), _full((1, KV_LORA)), _full((Q_LORA, MLA_W)), _full((KV_LORA, 2 * MLA_W)),
                  tab, tab, tab],
        out_specs=[row(MLA_W), row(MLA_W), row(MLA_W), row(Q_LORA), row(KV_LORA)],
        out_shape=[jax.ShapeDtypeStruct((T, MLA_W), BF16)] * 3
        + [jax.ShapeDtypeStruct((T, Q_LORA), BF16), jax.ShapeDtypeStruct((T, KV_LORA), BF16)],
        compiler_params=_params(1),
    )(P, P, P, g_cq, g_ckv, wuq, wukv, rope_c, rope_sa, rope_sb)


def _chunk_mask(tq, tk):
    row = lax.broadcasted_iota(jnp.int32, (tq, tk), 0)
    col = lax.broadcasted_iota(jnp.int32, (tq, tk), 1)
    return (row >> CHUNK_SHIFT) >= (col >> CHUNK_SHIFT)


def _flash_fwd(Q, K, V, S, ride=None):
    T = Q.shape[0]
    n_seq = T // S
    tq = _tile(S, 256, 128)
    nq = S // tq

    def body(q_ref, k_ref, v_ref, *rest):
        (o_ref, lse_ref), (m_s, acc_s, s_s, p_s, a_s), riding = _ride_split(ride, rest, 2, 5)
        _ride_start(riding, (pl.program_id(0) == 0) & (pl.program_id(1) == 0))
        qi = pl.program_id(1)
        m_s[...] = jnp.full(m_s.shape, NEG_INF, F32)
        acc_s[...] = jnp.zeros_like(acc_s)

        def block(kb, masked):
            rows = pl.ds(pl.multiple_of(kb * tq, tq), tq)
            for h in range(HEADS):
                hs = slice(h * HEAD_PAD, (h + 1) * HEAD_PAD)
                s_s[h] = _dot_nt(q_ref[:, hs], k_ref[rows, hs])
            def softmax_head(h):
                for r in range(0, tq, SOFTMAX_ROWS):
                    rs = slice(r, r + SOFTMAX_ROWS)
                    s = s_s[h, rs, :]
                    if masked:
                        row = r + lax.broadcasted_iota(jnp.int32, (SOFTMAX_ROWS, tq), 0)
                        col = lax.broadcasted_iota(jnp.int32, (SOFTMAX_ROWS, tq), 1)
                        s = jnp.where((row >> CHUNK_SHIFT) >= (col >> CHUNK_SHIFT), s, NEG_INF)
                    m_old = m_s[h, rs]
                    m_new = jnp.maximum(m_old, jnp.max(s, axis=-1, keepdims=True))
                    m_s[h, rs] = m_new
                    a_s[h, rs] = jnp.exp2(m_old - m_new)
                    for half in range(tq // HEAD_PAD):
                        cs = slice(half * HEAD_PAD, (half + 1) * HEAD_PAD)
                        p_s[h, rs, cs] = jnp.exp2(s[:, cs] - m_new).astype(BF16)

            for h in range(HEADS):
                softmax_head(h)
            for h in range(HEADS):
                hs = slice(h * HEAD_PAD, (h + 1) * HEAD_PAD)
                acc_s[:, hs] = a_s[h] * acc_s[:, hs] + _dot(p_s[h], v_ref[rows, hs])

        def full_block(kb, carry):
            block(kb, False)
            return carry

        lax.fori_loop(0, qi, full_block, 0)
        block(qi, True)
        lane = lax.broadcasted_iota(jnp.int32, (tq, HEAD_PAD), 1)
        lse_all = jnp.zeros((tq, HEAD_PAD), F32)
        for h in range(HEADS):
            hs = slice(h * HEAD_PAD, (h + 1) * HEAD_PAD)
            acc = acc_s[:, hs]
            l = jnp.sum(jnp.where(lane == V_HEAD, acc, 0.0), axis=-1, keepdims=True)
            o_ref[:, hs] = (acc / l).astype(BF16)
            lse_all = jnp.where(lane == h, m_s[h] + jnp.log2(l), lse_all)
        lse_ref[...] = lse_all
        _ride_finish(riding, (pl.program_id(0) == n_seq - 1) & (pl.program_id(1) == nq - 1))

    ride_in, ride_out, ride_shapes, ride_scratch, ride_args, alias = _ride_specs(ride, 3, 2)
    out = pl.pallas_call(
        body, name="flash_fwd", grid=(n_seq, nq),
        in_specs=[pl.BlockSpec((tq, MLA_W), lambda b, i: (b * nq + i, 0)),
                  pl.BlockSpec((S, MLA_W), lambda b, i: (b, 0)), pl.BlockSpec((S, MLA_W), lambda b, i: (b, 0))] + ride_in,
        out_specs=[pl.BlockSpec((tq, MLA_W), lambda b, i: (b * nq + i, 0)),
                   pl.BlockSpec((tq, HEAD_PAD), lambda b, i: (b * nq + i, 0))] + ride_out,
        out_shape=[jax.ShapeDtypeStruct((T, MLA_W), BF16), jax.ShapeDtypeStruct((T, HEAD_PAD), F32)] + ride_shapes,
        input_output_aliases=alias,
        scratch_shapes=[pltpu.VMEM((HEADS, tq, HEAD_PAD), F32), pltpu.VMEM((tq, MLA_W), F32), pltpu.VMEM((HEADS, tq, tq), F32),
                        pltpu.VMEM((HEADS, tq, tq), BF16), pltpu.VMEM((HEADS, tq, HEAD_PAD), F32)] + ride_scratch,
        compiler_params=_params(2),
    )(Q, K, V, *ride_args)
    return out[0], out[1], list(out[2:])


def _lane_group():
    return lax.broadcasted_iota(jnp.int32, (1, MIX_W), 1) >> GROUP_SHIFT


def _by_group(a0, a1, a2, a3):
    g = _lane_group()
    return jnp.where(g == 0, a0, jnp.where(g == 1, a1, jnp.where(g == 2, a2, a3)))


def _pool_count(si, tc, rows):
    pos = si * tc + lax.broadcasted_iota(jnp.int32, (rows, MIX_W), 0)
    win = _by_group(*POOL_WINDOWS)
    return jnp.minimum(pos + 1, win).astype(F32)


def _pool_fwd(z, z_prev, si, tc):
    ze = jnp.concatenate([z_prev, z], axis=0)
    s1 = ze + _shift_down(ze, 1)
    s2 = s1 + _shift_down(s1, 2)
    s4 = s2 + _shift_down(s2, 4)
    s8 = s4 + _shift_down(s4, 8)
    win_sum = _by_group(s1, s2, s4, s8)[POOL_HALO:]
    return win_sum / _pool_count(si, tc, tc) - z


def _sgu_weights(w_ref):
    row = lax.broadcasted_iota(jnp.int32, (SGU_BLOCK, SGU_BLOCK), 0)
    col = lax.broadcasted_iota(jnp.int32, (SGU_BLOCK, SGU_BLOCK), 1)
    keep = (row >> CHUNK_SHIFT) >= (col >> CHUNK_SHIFT)
    return keep, [jnp.where(keep, w_ref[g], 0.0).astype(BF16) for g in range(GROUPS)]


def _sgu_mix(vn_blk, wm, bias):
    g = _lane_group()
    mixed = bias
    for k in range(GROUPS):
        mixed = mixed + jnp.where(g == k, _dot(wm[k], vn_blk), 0.0)
    return mixed


def _conv_fwd(z, z_prev, w_ref):
    ze = jnp.concatenate([z_prev, z], axis=0)
    y = w_ref[0:1, :] * _shift_down(ze, 2) + w_ref[1:2, :] * _shift_down(ze, 1) + w_ref[2:3, :] * ze
    return y[CONV_HALO:]


def _mix_specs(T, D, tc):
    base = (4 * D + SEC_MLA) // MIX_W
    cur = lambda k: pl.BlockSpec((tc, MIX_W), lambda i: (i, base + k))
    prev = lambda k, halo: pl.BlockSpec((halo, MIX_W), lambda i: (jnp.maximum(i * (tc // halo) - 1, 0), base + k))
    nxt = lambda k, halo: pl.BlockSpec((halo, MIX_W), lambda i: (jnp.minimum((i + 1) * (tc // halo), T // halo - 1), base + k))
    return cur, prev, nxt


def _mixers_fwd(P, D, S, pool_bd, pool_scale, g_v, sgu_w, sgu_bias, conv_w, ride=None):
    T = P.shape[0]
    tc = _tile(S, 512, SGU_BLOCK)
    n_si = S // tc
    cur, prev, _ = _mix_specs(T, D, tc)

    def body(z_ref, zp_ref, u_ref, v_ref, b_ref, c_ref, x_ref, cp_ref, xp_ref,
             pw_ref, ps_ref, gv_ref, sw_ref, sb_ref, cw_ref, *rest):
        (ob_ref, oc_ref, od_ref), _, riding = _ride_split(ride, rest, 3, 0)
        _ride_start(riding, pl.program_id(0) == 0)
        si = pl.program_id(0) % n_si
        first = si == 0
        z = z_ref[...]
        pooled = _pool_fwd(z, jnp.where(first, 0.0, zp_ref[...]), si, tc)
        ob_ref[...] = (_dot(pooled.astype(BF16), pw_ref[...]) * ps_ref[...]).astype(BF16)

        v = v_ref[...]
        vn = (v * _rms_r(v) * gv_ref[...]).astype(BF16)
        _, wm = _sgu_weights(sw_ref)
        for blk in range(tc // SGU_BLOCK):
            rows = slice(blk * SGU_BLOCK, (blk + 1) * SGU_BLOCK)
            oc_ref[rows, :] = (u_ref[rows, :] * _sgu_mix(vn[rows], wm, sb_ref[...])).astype(BF16)

        zc = c_ref[...] * x_ref[...]
        zc_prev = jnp.where(first, 0.0, cp_ref[...] * xp_ref[...])
        od_ref[...] = (b_ref[...] * _conv_fwd(zc, zc_prev, cw_ref)).astype(BF16)
        _ride_finish(riding, pl.program_id(0) == T // tc - 1)

    out = pl.BlockSpec((tc, MIX_W), lambda i: (i, 0))
    ride_in, ride_out, ride_shapes, ride_scratch, ride_args, alias = _ride_specs(ride, 15, 3)
    res = pl.pallas_call(
        body, name="mixers_fwd", grid=(T // tc,),
        in_specs=[cur(0), prev(0, POOL_HALO), cur(1), cur(2), cur(3), cur(4), cur(5), prev(4, CONV_HALO), prev(5, CONV_HALO),
                  _full((MIX_W, MIX_W)), _full((1, MIX_W)), _full((1, MIX_W)), _full((GROUPS, SGU_BLOCK, SGU_BLOCK)),
                  _full((SGU_BLOCK, MIX_W)), _full((CONV_HALO, MIX_W))] + ride_in,
        out_specs=[out, out, out] + ride_out,
        out_shape=[jax.ShapeDtypeStruct((T, MIX_W), BF16)] * 3 + ride_shapes,
        input_output_aliases=alias,
        scratch_shapes=ride_scratch,
        compiler_params=_params(1),
    )(P, P, P, P, P, P, P, P, P, pool_bd, pool_scale, g_v, sgu_w, sgu_bias, conv_w, *ride_args)
    return res[0], res[1], res[2], list(res[3:])


def _merge_fwd(x, P, A, Bm, C, Dv, wa, wb, wc, wd, wout, g_post):
    T, D = x.shape
    tm = _tile(T, 256)

    def body(x_ref, lg_ref, a_ref, b_ref, c_ref, d_ref, wa_ref, wb_ref, wc_ref, wd_ref, wo_ref, g_ref,
             x1_ref, mg_ref, o_ref):
        merged = jnp.zeros((tm, D), F32)
        for k, (br, w) in enumerate(((a_ref, wa_ref), (b_ref, wb_ref), (c_ref, wc_ref), (d_ref, wd_ref))):
            merged = merged + _sigmoid(lg_ref[:, k * D:(k + 1) * D]) * _dot(br[...], w[...])
        mg = merged.astype(BF16)
        mg_ref[...] = mg
        o = _dot(mg, wo_ref[...])
        o_ref[...] = o
        x1_ref[...] = x_ref[...] + o * _rms_r(o) * g_ref[...]

    row = lambda w: pl.BlockSpec((tm, w), lambda i: (i, 0))
    return pl.pallas_call(
        body, name="merge_fwd", grid=(T // tm,),
        in_specs=[row(D), row(4 * D), row(MLA_W), row(MIX_W), row(MIX_W), row(MIX_W),
                  _full((MLA_W, D)), _full((MIX_W, D)), _full((MIX_W, D)), _full((MIX_W, D)), _full((D, D)), _full((1, D))],
        out_specs=[row(D), row(D), row(D)],
        out_shape=[jax.ShapeDtypeStruct((T, D), F32), jax.ShapeDtypeStruct((T, D), BF16), jax.ShapeDtypeStruct((T, D), F32)],
        compiler_params=_params(1),
    )(x, P, A, Bm, C, Dv, wa, wb, wc, wd, wout, g_post)


def _ffn_specs(T, D, Fc, l):
    tm = _tile(T, 512)
    row = pl.BlockSpec((tm, D), lambda i, j: (i, 0))
    col = pl.BlockSpec((None, tm, Fc), lambda i, j: (j, i, 0))
    w_in = pl.BlockSpec((None, None, D, Fc), lambda i, j: (l, j, 0, 0))
    w_out = pl.BlockSpec((None, None, Fc, D), lambda i, j: (l, j, 0, 0))
    return tm, row, col, w_in, w_out


def _ffn_fwd(x1, g_pre, wg, wu, wdn, g_post, l):
    T, D = x1.shape
    nf, Fc = wg.shape[1], wg.shape[3]
    tm, row, col, w_in, w_out = _ffn_specs(T, D, Fc, l)

    def body(x_ref, gp_ref, wg_ref, wu_ref, wd_ref, gq_ref, x2_ref, h_ref, gt_ref, up_ref, f_ref, gt_s, up_s, a_s):
        j = pl.program_id(1)

        @pl.when(j == 0)
        def _():
            for r in range(0, tm, NORM_ROWS):
                rs = slice(r, r + NORM_ROWS)
                xv = x_ref[rs, :]
                h_ref[rs, :] = (xv * _rms_r(xv) * gp_ref[...]).astype(BF16)
            f_ref[...] = jnp.zeros_like(f_ref)

        gt_s[...] = _dot(h_ref[...], wg_ref[...])
        up_s[...] = _dot(h_ref[...], wu_ref[...])
        for r in range(0, tm, ACT_ROWS):
            rs = slice(r, r + ACT_ROWS)
            gt, up = gt_s[rs, :], up_s[rs, :]
            gt_ref[rs, :] = gt.astype(BF16)
            up_ref[rs, :] = up.astype(BF16)
            a_s[rs, :] = (gt * _sigmoid(gt) * up).astype(BF16)
        f_ref[...] += _dot(a_s[...], wd_ref[...])

        @pl.when(j == nf - 1)
        def _():
            for r in range(0, tm, NORM_ROWS):
                rs = slice(r, r + NORM_ROWS)
                f = f_ref[rs, :]
                x2_ref[rs, :] = x_ref[rs, :] + f * _rms_r(f) * gq_ref[...]

    return pl.pallas_call(
        body, name="ffn_fwd", grid=(T // tm, nf),
        in_specs=[row, _full((1, D)), w_in, w_in, w_out, _full((1, D))],
        out_specs=[row, row, col, col, row],
        out_shape=[jax.ShapeDtypeStruct((T, D), F32), jax.ShapeDtypeStruct((T, D), BF16),
                   jax.ShapeDtypeStruct((nf, T, Fc), BF16), jax.ShapeDtypeStruct((nf, T, Fc), BF16),
                   jax.ShapeDtypeStruct((T, D), F32)],
        scratch_shapes=[pltpu.VMEM((tm, Fc), F32), pltpu.VMEM((tm, Fc), F32), pltpu.VMEM((tm, Fc), BF16)],
        compiler_params=_params(2),
    )(x1, g_pre, wg, wu, wdn, g_post)


def _loss_grad(y, target):
    T, D = y.shape
    tm = _tile(T, 512)

    def body(y_ref, t_ref, l_ref, dy_ref):
        @pl.when(pl.program_id(0) == 0)
        def _():
            l_ref[...] = jnp.zeros_like(l_ref)

        d = y_ref[...] - t_ref[...]
        dy_ref[...] = d * (1.0 / D)
        e = jnp.sum((d * d).reshape(tm // 8, 8, D), axis=0)
        part = e[:, 0:128]
        for k in range(1, D // 128):
            part = part + e[:, k * 128:(k + 1) * 128]
        l_ref[...] += part * (0.5 / D)

    row = pl.BlockSpec((tm, D), lambda i: (i, 0))
    return pl.pallas_call(
        body, name="loss_grad", grid=(T // tm,),
        in_specs=[row, row], out_specs=[_full((8, 128)), row],
        out_shape=[jax.ShapeDtypeStruct((8, 128), F32), jax.ShapeDtypeStruct((T, D), F32)],
        compiler_params=_params(1),
    )(y, target)


def _matmul_tn(a, b, name):
    T, M = a.shape
    N = b.shape[1]
    tm, tn, tk = _tile(M, 1024, 128), _tile(N, 1536, 128), _tile(T, 512)

    def body(a_ref, b_ref, o_ref):
        @pl.when(pl.program_id(2) == 0)
        def _():
            o_ref[...] = jnp.zeros_like(o_ref)

        o_ref[...] += _dot_tn(a_ref[...], b_ref[...])

    return pl.pallas_call(
        body, name=name, grid=(M // tm, N // tn, T // tk),
        in_specs=[pl.BlockSpec((tk, tm), lambda i, j, k: (k, i)), pl.BlockSpec((tk, tn), lambda i, j, k: (k, j))],
        out_specs=pl.BlockSpec((tm, tn), lambda i, j, k: (i, j)),
        out_shape=jax.ShapeDtypeStruct((M, N), F32),
        compiler_params=_params(3),
    )(a, b)


def _wgrad_chip(buf, l, n_layers, a, b, rows, cols, a_mode, b_mode, name):
    T = a.shape[-2]
    tk = _tile(T, 1024)

    def spec(mode, width):
        if mode == "all":
            return pl.BlockSpec((tk, width), lambda k, t: (t, 0))
        if mode == "cols":
            return pl.BlockSpec((tk, width), lambda k, t: (t, k))
        return pl.BlockSpec((None, tk, width), lambda k, t: (k, t, 0))

    def body(a_ref, b_ref, *rest):
        o_ref = rest[-1]

        @pl.when(pl.program_id(1) == 0)
        def _():
            o_ref[...] = jnp.zeros_like(o_ref)

        o_ref[...] += _dot_tn(a_ref[...], b_ref[...])

    keep = [] if buf is None else [buf]
    return pl.pallas_call(
        body, name=name, grid=(N_CHIPS, T // tk),
        in_specs=[spec(a_mode, rows), spec(b_mode, cols)] + [pl.BlockSpec(memory_space=pl.ANY)] * len(keep),
        out_specs=pl.BlockSpec((None, None, rows, cols), lambda k, t: (k, l, 0, 0)),
        out_shape=jax.ShapeDtypeStruct((N_CHIPS, n_layers, rows, cols), F32),
        input_output_aliases={2: 0} if keep else {},
        compiler_params=_params(2),
    )(a, b, *keep)


def _ffn_bwd(dx2, x1, f, gt, up, g_pre, wg, wu, wdn, g_post, l, ride=None):
    T, D = x1.shape
    nf, Fc = wg.shape[1], wg.shape[3]
    tm, row, col, w_in, w_out = _ffn_specs(T, D, Fc, l)

    def body(dx2_ref, x1_ref, f_ref, gt_ref, up_ref, gp_ref, wg_ref, wu_ref, wd_ref, gq_ref, *rest):
        (dx1_ref, df_ref, dgt_ref, dup_ref, a_ref, dgp_ref, dgq_ref), (dh_acc, da_s), riding = _ride_split(ride, rest, 7, 2)
        i, j = pl.program_id(0), pl.program_id(1)
        _ride_start(riding, (i == 0) & (j == 0))

        @pl.when((i == 0) & (j == 0))
        def _():
            dgp_ref[...] = jnp.zeros_like(dgp_ref)
            dgq_ref[...] = jnp.zeros_like(dgq_ref)

        @pl.when(j == 0)
        def _():
            dg_sum = jnp.zeros((1, D), F32)
            for r in range(0, tm, NORM_ROWS):
                rs = slice(r, r + NORM_ROWS)
                df, dg = _rms_bwd(dx2_ref[rs, :], f_ref[rs, :], gq_ref[...])
                df_ref[rs, :] = df.astype(BF16)
                dg_sum = dg_sum + dg
            dgq_ref[...] += dg_sum
            dh_acc[...] = jnp.zeros_like(dh_acc)

        parts = [slice(p * tm // ROW_PARTS, (p + 1) * tm // ROW_PARTS) for p in range(ROW_PARTS)]
        for ps in parts:
            da_s[ps, :] = _dot_nt(df_ref[ps, :], wd_ref[...])
        for ps in parts:
            for r in range(ps.start, ps.stop, ACT_ROWS):
                rs = slice(r, r + ACT_ROWS)
                da = da_s[rs, :]
                gt = gt_ref[rs, :].astype(F32)
                up = up_ref[rs, :].astype(F32)
                sig = _sigmoid(gt)
                silu = gt * sig
                dgt_ref[rs, :] = (da * up * (sig * (1.0 + gt * (1.0 - sig)))).astype(BF16)
                dup_ref[rs, :] = (da * silu).astype(BF16)
                a_ref[rs, :] = (silu * up).astype(BF16)
            dh_acc[ps, :] += _dot_nt(dgt_ref[ps, :], wg_ref[...]) + _dot_nt(dup_ref[ps, :], wu_ref[...])

        @pl.when(j == nf - 1)
        def _():
            dg_sum = jnp.zeros((1, D), F32)
            for r in range(0, tm, NORM_ROWS):
                rs = slice(r, r + NORM_ROWS)
                dx, dg = _rms_bwd(dh_acc[rs, :], x1_ref[rs, :], gp_ref[...])
                dx1_ref[rs, :] = dx2_ref[rs, :] + dx
                dg_sum = dg_sum + dg
            dgp_ref[...] += dg_sum

        _ride_finish(riding, (i == T // tm - 1) & (j == nf - 1))

    ride_in, ride_out, ride_shapes, ride_scratch, ride_args, alias = _ride_specs(ride, 10, 7)
    out = pl.pallas_call(
        body, name="ffn_bwd", grid=(T // tm, nf),
        in_specs=[row, row, row, col, col, _full((1, D)), w_in, w_in, w_out, _full((1, D))] + ride_in,
        out_specs=[row, row, col, col, col, _full((1, D)), _full((1, D))] + ride_out,
        out_shape=[jax.ShapeDtypeStruct((T, D), F32), jax.ShapeDtypeStruct((T, D), BF16)]
        + [jax.ShapeDtypeStruct((nf, T, Fc), BF16)] * 3
        + [jax.ShapeDtypeStruct((1, D), F32), jax.ShapeDtypeStruct((1, D), F32)] + ride_shapes,
        input_output_aliases=alias,
        scratch_shapes=[pltpu.VMEM((tm, D), F32), pltpu.VMEM((tm, Fc), F32)] + ride_scratch,
        compiler_params=_params(2),
    )(dx2, x1, f, gt, up, g_pre, wg, wu, wdn, g_post, *ride_args)
    return tuple(out[:7]) + (list(out[7:]),)


def _wgrad_out(l, n_layers, rows, cols):
    spec = pl.BlockSpec((N_CHIPS, None, rows, cols), lambda *_: (0, l, 0, 0))
    return spec, jax.ShapeDtypeStruct((N_CHIPS, n_layers, rows, cols), F32)


def _merge_bwd(dx1, o, merged, P, A, Bm, C, Dv, wa, wb, wc, wd, wout, g_post, l, bufs, ride=None):
    T, D = o.shape
    tm = _tile(T, 256)
    Dc = D // N_CHIPS
    names = ("w_out", "w_br_a", "w_br_b", "w_br_c", "w_br_d")
    keep = [] if bufs is None else [bufs[n] for n in names]

    def body(dx1_ref, o_ref, mg_ref, lg_ref, a_ref, b_ref, c_ref, d_ref, wa_ref, wb_ref, wc_ref, wd_ref, wo_ref, g_ref, *rest):
        outs, _, riding = _ride_split(ride, rest[len(keep):], 11, 0)
        dlg_ref, da_ref, db_ref, dc_ref, dd_ref, dg_ref, go_ref, ga_ref, gb_ref, gc_ref, gd_ref = outs
        _ride_start(riding, pl.program_id(0) == 0)

        @pl.when(pl.program_id(0) == 0)
        def _():
            for r in (dg_ref, go_ref, ga_ref, gb_ref, gc_ref, gd_ref):
                r[...] = jnp.zeros_like(r)

        d_o, dg = _rms_bwd(dx1_ref[...], o_ref[...], g_ref[...])
        dg_ref[...] += dg
        d_o = d_o.astype(BF16)
        for k in range(N_CHIPS):
            go_ref[k] += _dot_tn(mg_ref[:, k * Dc:(k + 1) * Dc], d_o)
        dm = _dot_nt(d_o, wo_ref[...])
        branches = ((a_ref, wa_ref, da_ref, ga_ref), (b_ref, wb_ref, db_ref, gb_ref),
                    (c_ref, wc_ref, dc_ref, gc_ref), (d_ref, wd_ref, dd_ref, gd_ref))
        for j, (br, w, dbr_ref, gw_ref) in enumerate(branches):
            gate = _sigmoid(lg_ref[:, j * D:(j + 1) * D])
            y = _dot(br[...], w[...])
            dlg_ref[:, j * D:(j + 1) * D] = (dm * y * gate * (1.0 - gate)).astype(BF16)
            dy = (dm * gate).astype(BF16)
            dbr_ref[...] = _dot_nt(dy, w[...]).astype(dbr_ref.dtype)
            for k in range(N_CHIPS):
                gw_ref[k] += _dot_tn(br[...], dy[:, k * Dc:(k + 1) * Dc])

        _ride_finish(riding, pl.program_id(0) == T // tm - 1)

    row = lambda w: pl.BlockSpec((tm, w), lambda i: (i, 0))
    wg = [_wgrad_out(l, DEPTH, r, c) for r, c in ((Dc, D), (MLA_W, Dc), (MIX_W, Dc), (MIX_W, Dc), (MIX_W, Dc))]
    n_in = 14
    ride_in, ride_out, ride_shapes, ride_scratch, ride_args, alias = _ride_specs(ride, n_in + len(keep), 11)
    out = pl.pallas_call(
        body, name="merge_bwd", grid=(T // tm,),
        in_specs=[row(D), row(D), row(D), row(4 * D), row(MLA_W), row(MIX_W), row(MIX_W), row(MIX_W),
                  _full((MLA_W, D)), _full((MIX_W, D)), _full((MIX_W, D)), _full((MIX_W, D)), _full((D, D)), _full((1, D))]
        + [pl.BlockSpec(memory_space=pl.ANY)] * len(keep) + ride_in,
        out_specs=[row(4 * D), row(MLA_W), row(MIX_W), row(MIX_W), row(MIX_W), _full((1, D))] + [s for s, _ in wg] + ride_out,
        out_shape=[jax.ShapeDtypeStruct((T, 4 * D), BF16), jax.ShapeDtypeStruct((T, MLA_W), BF16)]
        + [jax.ShapeDtypeStruct((T, MIX_W), F32)] * 3 + [jax.ShapeDtypeStruct((1, D), F32)] + [s for _, s in wg] + ride_shapes,
        input_output_aliases={**{n_in + i: 6 + i for i in range(len(keep))}, **alias},
        scratch_shapes=ride_scratch,
        compiler_params=_params(1),
    )(dx1, o, merged, P, A, Bm, C, Dv, wa, wb, wc, wd, wout, g_post, *keep, *ride_args)
    return out[:6], dict(zip(names, out[6:11])), list(out[11:])


def _mixers_bwd(P, D, S, dBm, dC, dDv, pool_bd, pool_scale, g_v, sgu_w, sgu_bias, conv_w):
    T = P.shape[0]
    tc = _tile(S, 512, SGU_BLOCK)
    n_si = S // tc
    cur, prev, nxt = _mix_specs(T, D, tc)
    n_blk = tc // SGU_BLOCK

    def body(z_ref, zp_ref, u_ref, v_ref, b_ref, c_ref, x_ref, cp_ref, xp_ref, bn_ref,
             dbm_ref, dbmn_ref, dc_ref, ddv_ref, ddvn_ref,
             pw_ref, ps_ref, gv_ref, sw_ref, sb_ref, cw_ref,
             dp_ref, dpw_ref, dps_ref, dgv_ref, dsw_ref, dsb_ref, dcw_ref, dvn_acc):
        si = pl.program_id(0) % n_si
        first, last = si == 0, si == n_si - 1

        @pl.when(pl.program_id(0) == 0)
        def _():
            for r in (dpw_ref, dps_ref, dgv_ref, dsw_ref, dsb_ref, dcw_ref):
                r[...] = jnp.zeros_like(r)

        z = z_ref[...]
        pooled = _pool_fwd(z, jnp.where(first, 0.0, zp_ref[...]), si, tc).astype(BF16)
        dbm = dbm_ref[...]
        dps_ref[...] += jnp.sum(dbm * _dot(pooled, pw_ref[...]), axis=0, keepdims=True)
        dmix = (jnp.concatenate([dbm, jnp.where(last, 0.0, dbmn_ref[...])], axis=0) * ps_ref[...]).astype(BF16)
        dpw_ref[...] += _dot_tn(pooled, dmix[:tc])
        dpool = _dot_nt(dmix, pw_ref[...])
        e = dpool / _pool_count(si, tc, tc + POOL_HALO)
        f1 = e + _shift_up(e, 1)
        f2 = f1 + _shift_up(f1, 2)
        f4 = f2 + _shift_up(f2, 4)
        f8 = f4 + _shift_up(f4, 8)
        dp_ref[:, 0:MIX_W] = (_by_group(f1, f2, f4, f8)[:tc] - dpool[:tc]).astype(BF16)

        v = v_ref[...]
        vn = (v * _rms_r(v) * gv_ref[...]).astype(BF16)
        keep, wm = _sgu_weights(sw_ref)
        g = _lane_group()
        for blk in range(n_blk):
            rows = slice(blk * SGU_BLOCK, (blk + 1) * SGU_BLOCK)
            vb = vn[rows]
            dc = dc_ref[rows, :]
            dp_ref[rows, MIX_W:2 * MIX_W] = (dc * _sgu_mix(vb, wm, sb_ref[...])).astype(BF16)
            dmx = dc * u_ref[rows, :]
            dsb_ref[...] += dmx
            dvn = jnp.zeros((SGU_BLOCK, MIX_W), F32)
            for k in range(GROUPS):
                dmk = jnp.where(g == k, dmx, 0.0).astype(BF16)
                dsw_ref[k] += jnp.where(keep, _dot_nt(dmk, vb), 0.0)
                dvn = dvn + _dot_tn(wm[k], dmk)
            dvn_acc[rows, :] = dvn
        dv, dg = _rms_bwd(dvn_acc[...], v, gv_ref[...])
        dgv_ref[...] += dg
        dp_ref[:, 2 * MIX_W:3 * MIX_W] = dv.astype(BF16)

        cg, xg, bg = c_ref[...], x_ref[...], b_ref[...]
        zc = cg * xg
        ze = jnp.concatenate([jnp.where(first, 0.0, cp_ref[...] * xp_ref[...]), zc], axis=0)
        z1, z2 = _shift_down(ze, 1)[CONV_HALO:], _shift_down(ze, 2)[CONV_HALO:]
        ddv = ddv_ref[...]
        y = cw_ref[0:1, :] * z2 + cw_ref[1:2, :] * z1 + cw_ref[2:3, :] * zc
        dp_ref[:, 3 * MIX_W:4 * MIX_W] = (ddv * y).astype(BF16)
        dy = ddv * bg
        dcw_ref[0:1, :] += jnp.sum(dy * z2, axis=0, keepdims=True)
        dcw_ref[1:2, :] += jnp.sum(dy * z1, axis=0, keepdims=True)
        dcw_ref[2:3, :] += jnp.sum(dy * zc, axis=0, keepdims=True)
        dye = jnp.concatenate([dy, jnp.where(last, 0.0, ddvn_ref[...] * bn_ref[...])], axis=0)
        dz = (cw_ref[2:3, :] * dye + cw_ref[1:2, :] * _shift_up(dye, 1) + cw_ref[0:1, :] * _shift_up(dye, 2))[:tc]
        dp_ref[:, 4 * MIX_W:5 * MIX_W] = (dz * xg).astype(BF16)
        dp_ref[:, 5 * MIX_W:6 * MIX_W] = (dz * cg).astype(BF16)

    grad = lambda halo: pl.BlockSpec((halo, MIX_W), lambda i: (jnp.minimum((i + 1) * (tc // halo), T // halo - 1), 0))
    out = pl.BlockSpec((tc, MIX_W), lambda i: (i, 0))
    return pl.pallas_call(
        body, name="mixers_bwd", grid=(T // tc,),
        in_specs=[cur(0), prev(0, POOL_HALO), cur(1), cur(2), cur(3), cur(4), cur(5), prev(4, CONV_HALO), prev(5, CONV_HALO),
                  nxt(3, CONV_HALO), out, grad(POOL_HALO), out, out, grad(CONV_HALO),
                  _full((MIX_W, MIX_W)), _full((1, MIX_W)), _full((1, MIX_W)), _full((GROUPS, SGU_BLOCK, SGU_BLOCK)),
                  _full((SGU_BLOCK, MIX_W)), _full((CONV_HALO, MIX_W))],
        out_specs=[pl.BlockSpec((tc, SEC_MIX), lambda i: (i, 0)), _full((MIX_W, MIX_W)), _full((1, MIX_W)), _full((1, MIX_W)),
                   _full((GROUPS, SGU_BLOCK, SGU_BLOCK)), _full((SGU_BLOCK, MIX_W)), _full((CONV_HALO, MIX_W))],
        out_shape=[jax.ShapeDtypeStruct((T, SEC_MIX), BF16), jax.ShapeDtypeStruct((MIX_W, MIX_W), F32),
                   jax.ShapeDtypeStruct((1, MIX_W), F32), jax.ShapeDtypeStruct((1, MIX_W), F32),
                   jax.ShapeDtypeStruct((GROUPS, SGU_BLOCK, SGU_BLOCK), F32), jax.ShapeDtypeStruct((SGU_BLOCK, MIX_W), F32),
                   jax.ShapeDtypeStruct((CONV_HALO, MIX_W), F32)],
        scratch_shapes=[pltpu.VMEM((tc, MIX_W), F32)],
        compiler_params=_params(1),
    )(P, P, P, P, P, P, P, P, P, P, dBm, dBm, dC, dDv, dDv, pool_bd, pool_scale, g_v, sgu_w, sgu_bias, conv_w)


def _attn_tile(S):
    return _tile(S, 256, 128)


def _attn_stats(O, dO, lse, S):
    T = O.shape[0]
    tq = _attn_tile(S)

    def body(o_ref, do_ref, lse_ref, lt_ref, dt_ref):
        lane = lax.broadcasted_iota(jnp.int32, (tq, HEAD_PAD), 1)
        delta = jnp.zeros((tq, HEAD_PAD), F32)
        for h in range(HEADS):
            hs = slice(h * HEAD_PAD, (h + 1) * HEAD_PAD)
            s = jnp.sum(o_ref[:, hs].astype(F32) * do_ref[:, hs].astype(F32), axis=-1, keepdims=True)
            delta = jnp.where(lane == h, s, delta)
        delta_t, lse_t = delta.T, lse_ref[...].T
        for h in range(HEADS):
            lt_ref[h, 0] = lse_t[h:h + 1, :]
            dt_ref[h, 0] = delta_t[h:h + 1, :]

    row = lambda w: pl.BlockSpec((tq, w), lambda i: (i, 0))
    out = pl.BlockSpec((HEADS, 1, 1, tq), lambda i: (0, i, 0, 0))
    return pl.pallas_call(
        body, name="attn_stats", grid=(T // tq,), in_specs=[row(MLA_W), row(MLA_W), row(HEAD_PAD)], out_specs=[out, out],
        out_shape=[jax.ShapeDtypeStruct((HEADS, T // tq, 1, tq), F32)] * 2, compiler_params=_params(1),
    )(O, dO, lse)


def _flash_bwd(Q, K, V, dO, lse_t, delta_t, S, ride=None):
    T = Q.shape[0]
    n_seq = T // S
    tq = _attn_tile(S)
    nq = S // tq

    def body(k_ref, v_ref, q_ref, do_ref, lse_ref, dl_ref, *rest):
        (dq_ref, dk_ref, dv_ref), (s_s, dp_s, p_s, ds_s), riding = _ride_split(ride, rest, 3, 4)
        kb = pl.program_id(1)
        _ride_start(riding, (pl.program_id(0) == 0) & (kb == 0))

        @pl.when(kb == 0)
        def _():
            dq_ref[...] = jnp.zeros_like(dq_ref)

        dk_ref[...] = jnp.zeros_like(dk_ref)
        dv_ref[...] = jnp.zeros_like(dv_ref)

        def block(qi, masked):
            rows = pl.ds(pl.multiple_of(qi * tq, tq), tq)
            for h in range(HEADS):
                hs = slice(h * HEAD_PAD, (h + 1) * HEAD_PAD)
                s_s[h] = _dot_nt(k_ref[:, hs], q_ref[rows, hs])
                dp_s[h] = _dot_nt(v_ref[:, hs], do_ref[rows, hs])
            for h in range(HEADS):
                lse_row, dl_row = lse_ref[h, qi], dl_ref[h, qi]
                for r in range(0, tq, SOFTMAX_ROWS):
                    rs = slice(r, r + SOFTMAX_ROWS)
                    s = s_s[h, rs, :]
                    if masked:
                        key = r + lax.broadcasted_iota(jnp.int32, (SOFTMAX_ROWS, tq), 0)
                        query = lax.broadcasted_iota(jnp.int32, (SOFTMAX_ROWS, tq), 1)
                        s = jnp.where((query >> CHUNK_SHIFT) >= (key >> CHUNK_SHIFT), s, NEG_INF)
                    p = jnp.exp2(s - lse_row)
                    p_s[h, rs, :] = p.astype(BF16)
                    ds_s[h, rs, :] = (p * (dp_s[h, rs, :] - dl_row)).astype(BF16)
            for h in range(HEADS):
                hs = slice(h * HEAD_PAD, (h + 1) * HEAD_PAD)
                dv_ref[:, hs] += _dot(p_s[h], do_ref[rows, hs])
                dk_ref[:, hs] += _dot(ds_s[h], q_ref[rows, hs])
                dq_ref[rows, hs] += _dot_tn(ds_s[h], k_ref[:, hs])

        def full_block(qi, carry):
            block(qi, False)
            return carry

        block(kb, True)
        lax.fori_loop(kb + 1, nq, full_block, 0)
        dk_ref[...] = dk_ref[...] * LN2
        _ride_finish(riding, (pl.program_id(0) == n_seq - 1) & (kb == nq - 1))

    tile = pl.BlockSpec((tq, MLA_W), lambda b, i: (b * nq + i, 0))
    seq = pl.BlockSpec((S, MLA_W), lambda b, i: (b, 0))
    stat = pl.BlockSpec((HEADS, nq, 1, tq), lambda b, i: (0, b, 0, 0))
    ride_in, ride_out, ride_shapes, ride_scratch, ride_args, alias = _ride_specs(ride, 6, 3)
    out = pl.pallas_call(
        body, name="flash_bwd", grid=(n_seq, nq),
        in_specs=[tile, tile, seq, seq, stat, stat] + ride_in,
        out_specs=[seq, tile, tile] + ride_out,
        out_shape=[jax.ShapeDtypeStruct((T, MLA_W), F32)] * 3 + ride_shapes,
        input_output_aliases=alias,
        scratch_shapes=[pltpu.VMEM((HEADS, tq, tq), F32), pltpu.VMEM((HEADS, tq, tq), F32),
                        pltpu.VMEM((HEADS, tq, tq), BF16), pltpu.VMEM((HEADS, tq, tq), BF16)] + ride_scratch,
        compiler_params=_params(2),
    )(K, V, Q, dO, lse_t, delta_t, *ride_args)
    return out[0], out[1], out[2], list(out[3:])


def _mla_bwd_post(P, D, S, dQ, dK, dV, hq, hkv, g_cq, g_ckv, wuq, wukv, rope_c, rope_sa, rope_sb, l, bufs):
    T = P.shape[0]
    tm = _tile(S, 512)
    n_si = S // tm
    base = 4 * D
    names = ("w_uq", "w_ukv")

    def body(cq_ref, ckv_ref, dq_ref, dk_ref, dv_ref, hq_ref, hkv_ref, gq_ref, gkv_ref, wq_ref, wkv_ref, c_ref, sa_ref, sb_ref,
             *rest):
        dp_ref, dgq_ref, dgkv_ref, guq_ref, gukv_ref = rest[-5:]

        @pl.when(pl.program_id(0) == 0)
        def _():
            for r in (dgq_ref, dgkv_ref, guq_ref, gukv_ref):
                r[...] = jnp.zeros_like(r)

        c, sa, sb = c_ref[...], sa_ref[...], sb_ref[...]
        dq = _rope_t(dq_ref[...] * ATTN_SCALE, jnp.tile(c, (1, HEADS)), jnp.tile(sa, (1, HEADS)),
                     jnp.tile(sb, (1, HEADS))).astype(BF16)
        dcq, dg = _rms_bwd(_dot_nt(dq, wq_ref[...]), cq_ref[...], gq_ref[...])
        dgq_ref[...] += dg
        dp_ref[:, 0:Q_LORA] = dcq.astype(BF16)

        dk = dk_ref[...]
        dkb, dvb = dk.astype(BF16), dv_ref[...].astype(BF16)
        dkv = jnp.concatenate([p[:, j * CHIP_HEADS_W:(j + 1) * CHIP_HEADS_W] for j in range(N_CHIPS) for p in (dkb, dvb)], axis=1)
        for k in range(N_CHIPS):
            guq_ref[k] += _dot_tn(hq_ref[...], dq[:, k * CHIP_HEADS_W:(k + 1) * CHIP_HEADS_W])
            gukv_ref[k] += _dot_tn(hkv_ref[...], dkv[:, k * CHIP_KV:(k + 1) * CHIP_KV])
        dckv, dg = _rms_bwd(_dot_nt(dkv, wkv_ref[...]), ckv_ref[...], gkv_ref[...])
        dgkv_ref[...] += dg
        dp_ref[:, Q_LORA:Q_LORA + KV_LORA] = dckv.astype(BF16)

        dkr = dk[:, 0:HEAD_PAD]
        for h in range(1, HEADS):
            dkr = dkr + dk[:, h * HEAD_PAD:(h + 1) * HEAD_PAD]
        lane = lax.broadcasted_iota(jnp.int32, (1, HEAD_PAD), 1)
        rope_lanes = (lane >= QK_NOPE) & (lane < QK_NOPE + QK_ROPE)
        dp_ref[:, Q_LORA + KV_LORA:SEC_MLA] = jnp.where(rope_lanes, _rope_t(dkr, c, sa, sb), 0.0).astype(BF16)

    tab = pl.BlockSpec((tm, HEAD_PAD), lambda i: (i % n_si, 0))
    row = lambda w: pl.BlockSpec((tm, w), lambda i: (i, 0))
    wg = [_wgrad_out(l, DEPTH, Q_LORA, CHIP_HEADS_W), _wgrad_out(l, DEPTH, KV_LORA, CHIP_KV)]
    keep = [] if bufs is None else [bufs[n] for n in names]
    n_in = 14
    out = pl.pallas_call(
        body, name="mla_bwd_post", grid=(T // tm,),
        in_specs=[pl.BlockSpec((tm, Q_LORA), lambda i: (i, base // Q_LORA)),
                  pl.BlockSpec((tm, KV_LORA), lambda i: (i, (base + Q_LORA) // KV_LORA)),
                  row(MLA_W), row(MLA_W), row(MLA_W), row(Q_LORA), row(KV_LORA),
                  _full((1, Q_LORA)), _full((1, KV_LORA)), _full((Q_LORA, MLA_W)), _full((KV_LORA, 2 * MLA_W)), tab, tab, tab]
        + [pl.BlockSpec(memory_space=pl.ANY)] * len(keep),
        out_specs=[row(SEC_MLA), _full((1, Q_LORA)), _full((1, KV_LORA))] + [s for s, _ in wg],
        out_shape=[jax.ShapeDtypeStruct((T, SEC_MLA), BF16), jax.ShapeDtypeStruct((1, Q_LORA), F32),
                   jax.ShapeDtypeStruct((1, KV_LORA), F32)] + [s for _, s in wg],
        input_output_aliases={n_in + i: 3 + i for i in range(len(keep))},
        compiler_params=_params(1),
    )(P, P, dQ, dK, dV, hq, hkv, g_cq, g_ckv, wuq, wukv, rope_c, rope_sa, rope_sb, *keep)
    return out[:3], dict(zip(names, out[3:]))


def _proj_bwd(dx1, x, g, dPg, dPa, dPm, w_gates, w_mla, w_mix, ride=None):
    T, D = x.shape
    tm = _tile(T, 256)

    def body(dx1_ref, x_ref, g_ref, dg_ref_in, da_ref, dm_ref, wg_ref, wa_ref, wm_ref, *rest):
        (dx_ref, dg_ref), _, riding = _ride_split(ride, rest, 2, 0)
        _ride_start(riding, pl.program_id(0) == 0)

        @pl.when(pl.program_id(0) == 0)
        def _():
            dg_ref[...] = jnp.zeros_like(dg_ref)

        dh = _dot_nt(dg_ref_in[...], wg_ref[...]) + _dot_nt(da_ref[...], wa_ref[...]) + _dot_nt(dm_ref[...], wm_ref[...])
        dx, dg = _rms_bwd(dh, x_ref[...], g_ref[...])
        dx_ref[...] = dx1_ref[...] + dx
        dg_ref[...] += dg
        _ride_finish(riding, pl.program_id(0) == T // tm - 1)

    row = lambda w: pl.BlockSpec((tm, w), lambda i: (i, 0))
    ride_in, ride_out, ride_shapes, ride_scratch, ride_args, alias = _ride_specs(ride, 9, 2)
    out = pl.pallas_call(
        body, name="proj_bwd", grid=(T // tm,),
        in_specs=[row(D), row(D), _full((1, D)), row(4 * D), row(SEC_MLA), row(SEC_MIX),
                  _full((D, 4 * D)), _full((D, SEC_MLA)), _full((D, SEC_MIX))] + ride_in,
        out_specs=[row(D), _full((1, D))] + ride_out,
        out_shape=[jax.ShapeDtypeStruct((T, D), F32), jax.ShapeDtypeStruct((1, D), F32)] + ride_shapes,
        input_output_aliases=alias, scratch_shapes=ride_scratch,
        compiler_params=_params(1),
    )(dx1, x, g, dPg, dPa, dPm, w_gates, w_mla, w_mix, *ride_args)
    return out[0], out[1], list(out[2:])


def _adamw(w, g, m, v, name):
    R, C = w.shape
    tr = _tile(R, max(8, (1 << 19) // C))

    def body(w_ref, g_ref, m_ref, v_ref, d_ref, mo_ref, vo_ref):
        gv = g_ref[...]
        mn = ADAM_B1 * m_ref[...] + (1.0 - ADAM_B1) * gv
        vn = ADAM_B2 * v_ref[...] + (1.0 - ADAM_B2) * (gv * gv)
        mo_ref[...] = mn
        vo_ref[...] = vn
        m_hat = mn / (1.0 - ADAM_B1 ** ADAM_STEP)
        v_hat = vn / (1.0 - ADAM_B2 ** ADAM_STEP)
        d_ref[...] = -ADAM_LR * (m_hat / (jnp.sqrt(v_hat) + ADAM_EPS) + ADAM_WD * w_ref[...])

    blk = pl.BlockSpec((tr, C), lambda i: (i, 0))
    return pl.pallas_call(
        body, name=name, grid=(R // tr,), in_specs=[blk] * 4, out_specs=[blk] * 3,
        out_shape=[jax.ShapeDtypeStruct((R, C), F32)] * 3, compiler_params=_params(1),
    )(w, g, m, v)


def _rows_tile(rows, cols):
    return _tile(rows, max(16, (1 << 19) // cols), 16)


def _add_halves(G, span, recv, half, name):
    n, L, R, C = G.shape
    l0, nl = span[0], span[1] - span[0]
    hr = R // 2
    tr = _rows_tile(hr, C)
    nb = hr // tr

    def body(half_ref, g_ref, r_ref, o_ref):
        o_ref[...] = (g_ref[...] + r_ref[...]).astype(BF16)

    grid_spec = pltpu.PrefetchScalarGridSpec(
        num_scalar_prefetch=1, grid=(n, nl, nb),
        in_specs=[pl.BlockSpec((None, None, tr, C), lambda k, l, i, h: (k, l0 + l, h[0] * nb + i, 0)),
                  pl.BlockSpec((None, None, tr, C), lambda k, l, i, h: (k, l, i, 0))],
        out_specs=pl.BlockSpec((None, None, tr, C), lambda k, l, i, h: (k, l, i, 0)))
    return pl.pallas_call(
        body, name="rs_add_halves_" + name, grid_spec=grid_spec,
        out_shape=jax.ShapeDtypeStruct((n, nl, hr, C), BF16), compiler_params=_params(3),
    )(half.reshape(1).astype(jnp.int32), G, recv)


def _sum_slots(H, slots, place, name, l0, n_layers, prev=None):
    _, nl, hr, C = slots.shape
    tr = _rows_tile(hr, C)
    nb = hr // tr

    def body(x_ref, y_ref, c_ref, own_ref, s1_ref, s2_ref, s3_ref, *rest):
        o_ref = rest[-1]
        o_ref[...] = ((own_ref[...].astype(F32) + s1_ref[...].astype(F32)) + s2_ref[...].astype(F32)) + s3_ref[...].astype(F32)

    def src(fx, fy):
        def index(l, j, px, py, pc):
            cx = px[0] + fx - 2 * fx * px[0]
            cy = py[0] + fy - 2 * fy * py[0]
            return (2 * cx + cy, l, j, 0)
        return pl.BlockSpec((None, None, tr, C), index)

    keep = [] if prev is None else [prev]
    grid_spec = pltpu.PrefetchScalarGridSpec(
        num_scalar_prefetch=3, grid=(nl, nb),
        in_specs=[src(0, 0), src(0, 1), src(1, 0), src(1, 1)] + [pl.BlockSpec(memory_space=pl.ANY)] * len(keep),
        out_specs=pl.BlockSpec((None, tr, C), lambda l, j, px, py, pc: (l0 + l, pc[0] * nb + j, 0)))
    return pl.pallas_call(
        body, name="rs_sum_slots_" + name, grid_spec=grid_spec,
        out_shape=jax.ShapeDtypeStruct((n_layers, 2 * hr, C), F32),
        input_output_aliases={7: 0} if keep else {}, compiler_params=_params(2),
    )(*place, H, slots, slots, slots, *keep)


HBM = pl.BlockSpec(memory_space=pltpu.HBM)


def _place():
    x, y, c = lax.axis_index("x"), lax.axis_index("y"), lax.axis_index("c")
    return x, y, c, 2 * x + y


def _chip_device(chip, c):
    return (chip // 2, chip % 2, c)


def _remote(src, dst, send_sem, recv_sem, to):
    return pltpu.make_async_remote_copy(src_ref=src, dst_ref=dst, send_sem=send_sem, recv_sem=recv_sem, device_id=to,
                                        device_id_type=MESH)


def _place_own(w, chip, name):
    L, R, C = w.shape
    tr = _rows_tile(R, C)

    def body(p_ref, w_ref, o_ref):
        o_ref[...] = w_ref[...].astype(BF16)

    grid_spec = pltpu.PrefetchScalarGridSpec(
        num_scalar_prefetch=1, grid=(L, R // tr), in_specs=[pl.BlockSpec((None, tr, C), lambda l, j, p: (l, j, 0))],
        out_specs=pl.BlockSpec((None, None, tr, C), lambda l, j, p: (l, p[0], j, 0)))
    return pl.pallas_call(
        body, name="place_" + name, grid_spec=grid_spec,
        out_shape=jax.ShapeDtypeStruct((L, N_CHIPS, R, C), BF16), compiler_params=_params(2),
    )(chip.reshape(1).astype(jnp.int32), w)


def _gather_weights(bufs, l):
    n = len(bufs)

    def body(*refs):
        o_refs = refs[n:2 * n]
        send_sems, recv_sems = refs[2 * n:]
        x, y, c, me = _place()
        sibling = (x, y, 1 - c)

        def copy(t, k, chip, half, to):
            hr = o_refs[t].shape[2] // 2
            block = o_refs[t].at[l, chip, pl.ds(half * hr, hr), :]
            return _remote(block, block, send_sems.at[6 * t + k], recv_sems.at[6 * t + k], to)

        first = [copy(t, d - 1, me, c, _chip_device(me ^ d, c)) for t in range(n) for d in (1, 2, 3)]
        for cp in first:
            cp.start()
        passed = []
        for t in range(n):
            for d in (1, 2, 3):
                copy(t, d - 1, me ^ d, c, sibling).wait_recv()
                passed.append(copy(t, 2 + d, me ^ d, c, sibling))
                passed[-1].start()
        for t in range(n):
            for d in (1, 2, 3):
                copy(t, 2 + d, me ^ d, 1 - c, sibling).wait_recv()
        for cp in first + passed:
            cp.wait_send()

    return pl.pallas_call(
        body, name="gather_weights", in_specs=[HBM] * n, out_specs=[HBM] * n,
        out_shape=[jax.ShapeDtypeStruct(b.shape, b.dtype) for b in bufs],
        input_output_aliases={t: t for t in range(n)},
        scratch_shapes=[pltpu.SemaphoreType.DMA((6 * n,)), pltpu.SemaphoreType.DMA((6 * n,))],
    )(*bufs)


def _join_halves(bufs):
    n = len(bufs)

    def body(*refs):
        o_refs = refs[n:2 * n]
        send_sems, recv_sems = refs[2 * n:]
        x, y, c, _ = _place()

        def half(t, which):
            hr = o_refs[t].shape[1] // 2
            return o_refs[t].at[:, pl.ds(which * hr, hr), :]

        sends = [_remote(half(t, c), half(t, c), send_sems.at[t], recv_sems.at[t], (x, y, 1 - c)) for t in range(n)]
        for cp in sends:
            cp.start()
        for t in range(n):
            _remote(half(t, 1 - c), half(t, 1 - c), send_sems.at[t], recv_sems.at[t], (x, y, 1 - c)).wait_recv()
        for cp in sends:
            cp.wait_send()

    return pl.pallas_call(
        body, name="rs_join_halves", in_specs=[HBM] * n, out_specs=[HBM] * n,
        out_shape=[jax.ShapeDtypeStruct(b.shape, b.dtype) for b in bufs],
        input_output_aliases={t: t for t in range(n)},
        scratch_shapes=[pltpu.SemaphoreType.DMA((n,)), pltpu.SemaphoreType.DMA((n,))],
    )(*bufs)


def _all_reduce_small(v, name):
    R, C = v.shape

    def body(v_ref, o_ref, slots, send_sems, recv_sems):
        x, y, c, _ = _place()
        me = 4 * x + 2 * y + c
        slots[me] = v_ref[...]
        sends = []
        for d in range(1, 8):
            peer = me ^ d
            sends.append(pltpu.make_async_remote_copy(
                src_ref=v_ref, dst_ref=slots.at[me], send_sem=send_sems.at[d - 1], recv_sem=recv_sems.at[d - 1],
                device_id=(peer // 4, (peer // 2) % 2, peer % 2), device_id_type=MESH))
        for cp in sends:
            cp.start()
        for d in range(1, 8):
            peer = me ^ d
            pltpu.make_async_remote_copy(
                src_ref=v_ref, dst_ref=slots.at[peer], send_sem=send_sems.at[d - 1], recv_sem=recv_sems.at[d - 1],
                device_id=(peer // 4, (peer // 2) % 2, peer % 2), device_id_type=MESH).wait_recv()
        for cp in sends:
            cp.wait_send()
        acc = slots[0]
        for k in range(1, 8):
            acc = acc + slots[k]
        o_ref[...] = acc

    vm = pl.BlockSpec(memory_space=pltpu.VMEM)
    return pl.pallas_call(
        body, name=name, in_specs=[vm], out_specs=vm, out_shape=jax.ShapeDtypeStruct((R, C), F32),
        scratch_shapes=[pltpu.VMEM((8, R, C), F32), pltpu.SemaphoreType.DMA((7,)), pltpu.SemaphoreType.DMA((7,))],
    )(v)


SHARDED = ("w_in", "w_uq", "w_ukv", "conv_w", "w_br_a", "w_br_b", "w_br_c", "w_br_d", "w_out", "w_ffn_gate", "w_ffn_up",
           "w_ffn_down")
ROW_SHARDED = ("w_out", "w_ffn_down")
REPLICATED = ("g_pre_mix", "g_cq", "g_ckv", "pool_w", "pool_scale", "g_sgu_v", "sgu_w", "sgu_b", "g_post_mix", "g_pre_ffn",
              "g_post_ffn")
WEIGHTS = ("w_in", "g_pre_mix", "g_cq", "g_ckv", "w_uq", "w_ukv", "pool_w", "pool_scale", "g_sgu_v", "sgu_w", "sgu_b",
           "conv_w", "w_br_a", "w_br_b", "w_br_c", "w_br_d", "w_out", "g_post_mix", "g_pre_ffn", "w_ffn_gate", "w_ffn_up",
           "w_ffn_down", "g_post_ffn")
GATHERED = tuple(n for n in SHARDED if n != "conv_w")


def _unpack(packed, shapes):
    flat = packed.reshape(-1)
    out, o = [], 0
    for s in shapes:
        n = int(np.prod(s))
        out.append(flat[o:o + n].reshape(s))
        o += n
    return out


def _join_cols(g, l):
    return jnp.concatenate([g[l, k] for k in range(N_CHIPS)], axis=1)


def _pad_heads(w, real):
    lead = w.shape[:-1]
    w = w.reshape(lead + (HEADS, real))
    return jnp.pad(w, [(0, 0)] * len(lead) + [(0, 0), (0, HEAD_PAD - real)]).reshape(lead + (MLA_W,))


def _unpad_heads(w, real):
    lead = w.shape[:-1]
    return w.reshape(lead + (HEADS, HEAD_PAD))[..., :real].reshape(lead + (HEADS * real,))


IN_OFFSETS = {"cq": 0, "ckv": Q_LORA, "kr": Q_LORA + KV_LORA, "mix": Q_LORA + KV_LORA + QK_ROPE}
IN_GATES = Q_LORA + KV_LORA + QK_ROPE + SEC_MIX


def _pad_w_in(w):
    K = w.shape[0]
    z = lambda n: jnp.zeros((K, n), w.dtype)
    return jnp.concatenate([w[:, IN_GATES:], w[:, :IN_OFFSETS["kr"]], z(QK_NOPE), w[:, IN_OFFSETS["kr"]:IN_OFFSETS["mix"]],
                            z(HEAD_PAD - QK_NOPE - QK_ROPE), w[:, IN_OFFSETS["mix"]:IN_GATES]], axis=1)


def _unpad_w_in(d_gates, d_mla, d_mix):
    kr = d_mla[:, Q_LORA + KV_LORA + QK_NOPE:Q_LORA + KV_LORA + QK_NOPE + QK_ROPE]
    return jnp.concatenate([d_mla[:, :Q_LORA + KV_LORA], kr, d_mix, d_gates], axis=1)


def _rope_tables(S):
    half = QK_ROPE // 2
    inv = ROPE_THETA ** (-jnp.arange(0, QK_ROPE, 2, dtype=F32) / QK_ROPE)
    ang = jnp.arange(S, dtype=F32)[:, None] * inv[None, :]
    cos, sin = jnp.cos(ang), jnp.sin(ang)
    one, zero = jnp.ones((S, QK_NOPE), F32), jnp.zeros((S, half), F32)
    tail = HEAD_PAD - QK_NOPE - QK_ROPE
    c = jnp.concatenate([one, cos, cos, jnp.ones((S, tail), F32)], axis=1)
    sa = jnp.concatenate([0 * one, zero, sin, jnp.zeros((S, tail), F32)], axis=1)
    sb = jnp.concatenate([0 * one, -sin, zero, jnp.zeros((S, tail), F32)], axis=1)
    return c, sa, sb


def _layer_weights(gathered, full, l, D):
    w = {}
    w_in = _pad_w_in(_join_cols(gathered["w_in"], l))
    w["w_in"] = w_in
    w["w_in_gates"], w["w_in_mla"], w["w_in_mix"] = w_in[:, :4 * D], w_in[:, 4 * D:4 * D + SEC_MLA], w_in[:, 4 * D + SEC_MLA:]
    w["w_uq"] = _pad_heads(_join_cols(gathered["w_uq"], l), QK_NOPE + QK_ROPE)
    ukv = _join_cols(gathered["w_ukv"], l).reshape(KV_LORA, HEADS, QK_NOPE + V_HEAD)
    pad = ((0, 0), (0, 0), (0, HEAD_PAD - QK_NOPE))
    k_pad = jnp.pad(ukv[:, :, :QK_NOPE], pad).reshape(KV_LORA, N_CHIPS, CHIP_HEADS_W)
    v_pad = jnp.pad(ukv[:, :, QK_NOPE:], pad).reshape(KV_LORA, N_CHIPS, CHIP_HEADS_W)
    w["w_ukv"] = jnp.concatenate([k_pad, v_pad], axis=2).reshape(KV_LORA, 2 * MLA_W)
    w["w_br_a"] = jnp.pad(_join_cols(gathered["w_br_a"], l).reshape(HEADS, V_HEAD, D),
                          ((0, 0), (0, HEAD_PAD - V_HEAD), (0, 0))).reshape(MLA_W, D)
    for n in ("w_br_b", "w_br_c", "w_br_d"):
        w[n] = _join_cols(gathered[n], l)
    w["w_out"] = gathered["w_out"][l].reshape(D, D)
    for n in ("g_pre_mix", "g_cq", "g_ckv", "pool_scale", "g_sgu_v", "g_post_mix", "g_pre_ffn", "g_post_ffn"):
        w[n] = full[n][l].reshape(1, -1)
    pw = full["pool_w"][l]
    w["pool_bd"] = jax.scipy.linalg.block_diag(*[pw[g] for g in range(GROUPS)]).astype(BF16)
    w["sgu_w"] = full["sgu_w"][l]
    w["sgu_bias"] = jnp.repeat(full["sgu_b"][l].T, GROUP_DIM, axis=1)
    w["conv_w"] = jnp.pad(full["conv_w"][l].reshape(3, MIX_W), ((0, CONV_HALO - 3), (0, 0)))
    return w


def _layer_fwd(x, w, gathered, l, S, rope, gather_next):
    D = x.shape[1]
    bufs = [gathered[n] for n in GATHERED]
    P, h = _norm_matmul(x, w["g_pre_mix"], w["w_in"], "proj_fwd")
    Q, K, V, hq, hkv = _mla_prep(P, D, w["g_cq"], w["g_ckv"], w["w_uq"], w["w_ukv"], *rope, S)
    A, lse, bufs = _flash_fwd(Q, K, V, S, _gather_ride(bufs, l + 1, "chips") if gather_next else None)
    Bm, C, Dv, bufs = _mixers_fwd(P, D, S, w["pool_bd"], w["pool_scale"], w["g_sgu_v"], w["sgu_w"], w["sgu_bias"], w["conv_w"],
                                  _gather_ride(bufs, l + 1, "cores") if gather_next else None)
    if gather_next:
        gathered = dict(zip(GATHERED, bufs))
    x1, merged, o = _merge_fwd(x, P, A, Bm, C, Dv, w["w_br_a"], w["w_br_b"], w["w_br_c"], w["w_br_d"], w["w_out"], w["g_post_mix"])
    x2, h2, gt, up, f = _ffn_fwd(x1, w["g_pre_ffn"], gathered["w_ffn_gate"], gathered["w_ffn_up"], gathered["w_ffn_down"],
                                 w["g_post_ffn"], l)
    saved = dict(x=x, P=P, h=h, Q=Q, K=K, V=V, hq=hq, hkv=hkv, A=A, lse=lse, Bm=Bm, C=C, Dv=Dv, x1=x1, merged=merged, o=o,
                 h2=h2, gt=gt, up=up, f=f)
    return x2, saved, gathered


RIDE_SETS = (("w_ffn_gate", "w_out", "w_br_a"), ("w_in", "w_br_b", "w_br_c", "w_br_d", "w_uq", "w_ukv"),
             ("w_ffn_up", "w_ffn_down"))


def _layer_bwd(dx2, w, gathered, l, s, S, rope, bufs, early=None):
    D = dx2.shape[1]
    Fc = gathered["w_ffn_gate"].shape[3]
    g = {}

    def wgrad(n, a, b, rows, cols, a_mode, b_mode):
        bufs[n] = _wgrad_chip(bufs.get(n), l, DEPTH, a, b, rows, cols, a_mode, b_mode, "wgrad_" + n)

    def scatter(k):
        return _scatter_ride([sums[n] for n in RIDE_SETS[k]]) if early else None

    sums, slots = {}, {}
    if early:
        done, spans, half = early
        ride = _exchange_ride([done[n] for n in GATHERED], [spans[n] for n in GATHERED])
    dx1, df, dgt, dup, act, g["g_pre_ffn"], g["g_post_ffn"], got = _ffn_bwd(
        dx2, s["x1"], s["f"], s["gt"], s["up"], w["g_pre_ffn"], gathered["w_ffn_gate"], gathered["w_ffn_up"],
        gathered["w_ffn_down"], w["g_post_ffn"], l, ride if early else None)
    if early:
        sums = {n: _add_halves(done[n], spans[n], r, half, n) for n, r in zip(GATHERED, got)}
        sums, (dx1, df, dgt, dup, act) = lax.optimization_barrier((sums, (dx1, df, dgt, dup, act)))
    wgrad("w_ffn_down", act, df, Fc, D, "slab", "all")
    wgrad("w_ffn_gate", s["h2"], dgt, D, Fc, "all", "slab")
    wgrad("w_ffn_up", s["h2"], dup, D, Fc, "all", "slab")

    (dPg, dA, dBm, dC, dDv, g["g_post_mix"]), filled, got = _merge_bwd(
        dx1, s["o"], s["merged"], s["P"], s["A"], s["Bm"], s["C"], s["Dv"], w["w_br_a"], w["w_br_b"], w["w_br_c"], w["w_br_d"],
        w["w_out"], w["g_post_mix"], l, bufs if "w_out" in bufs else None, scatter(0))
    bufs.update(filled)
    slots.update(zip(RIDE_SETS[0], got))

    dPm, d_pool_bd, g_ps, g_gv, g["sgu_w"], d_bias, d_cw = _mixers_bwd(
        s["P"], D, S, dBm, dC, dDv, w["pool_bd"], w["pool_scale"], w["g_sgu_v"], w["sgu_w"], w["sgu_bias"], w["conv_w"])
    g["pool_w"] = jnp.stack([d_pool_bd[k * GROUP_DIM:(k + 1) * GROUP_DIM, k * GROUP_DIM:(k + 1) * GROUP_DIM] for k in range(GROUPS)])
    g["pool_scale"], g["g_sgu_v"] = g_ps, g_gv
    g["sgu_b"] = d_bias.reshape(SGU_BLOCK, GROUPS, GROUP_DIM).sum(-1).T
    g["conv_w"] = d_cw[:3].reshape(3, 1, MIX_W)

    lse_t, delta_t = _attn_stats(s["A"], dA, s["lse"], S)
    dQ, dK, dV, got = _flash_bwd(s["Q"], s["K"], s["V"], dA, lse_t, delta_t, S, scatter(1))
    slots.update(zip(RIDE_SETS[1], got))
    (dPa, g["g_cq"], g["g_ckv"]), filled = _mla_bwd_post(
        s["P"], D, S, dQ, dK, dV, s["hq"], s["hkv"], w["g_cq"], w["g_ckv"], w["w_uq"], w["w_ukv"], *rope, l,
        bufs if "w_uq" in bufs else None)
    bufs.update(filled)

    dx, g["g_pre_mix"], got = _proj_bwd(dx1, s["x"], w["g_pre_mix"], dPg, dPa, dPm, w["w_in_gates"], w["w_in_mla"],
                                        w["w_in_mix"], scatter(2))
    slots.update(zip(RIDE_SETS[2], got))
    d_w_in = _unpad_w_in(_matmul_tn(s["h"], dPg, "wgrad_in_gates"), _matmul_tn(s["h"], dPa, "wgrad_in_mla"),
                         _matmul_tn(s["h"], dPm, "wgrad_in_mix"))
    g["w_in"] = d_w_in.reshape(D, N_CHIPS, -1).transpose(1, 0, 2)
    for n in ("g_pre_mix", "g_cq", "g_ckv", "pool_scale", "g_sgu_v", "g_post_mix", "g_pre_ffn", "g_post_ffn"):
        g[n] = g[n].reshape(-1)
    return dx, g, (sums, slots)


SMALL = REPLICATED + ("conv_w",)


def _local_step(x, target, gathered, full, core):
    n_seq, S, D = x.shape
    rope = _rope_tables(S)
    xs = x.reshape(n_seq * S, D)
    weights, saved = [], []
    for l in range(DEPTH):
        w = _layer_weights(gathered, full, l, D)
        gather_next = l + 1 < DEPTH
        if gather_next:
            w, gathered = lax.optimization_barrier((w, gathered))
        xs, s, gathered = _layer_fwd(xs, w, gathered, l, S, rope, gather_next)
        weights.append(w)
        saved.append(s)
    loss_parts, dx = _loss_grad(xs, target.reshape(n_seq * S, D))
    grads, bufs = [None] * DEPTH, {}
    for l in reversed(range(1, DEPTH)):
        dx, grads[l], _ = _layer_bwd(dx, weights[l], gathered, l, saved[l], S, rope, bufs)
    done = dict(bufs, w_in=jnp.stack([grads[l]["w_in"] for l in range(1, DEPTH)], axis=1))
    spans = dict({n: (1, DEPTH) for n in GATHERED}, w_in=(0, DEPTH - 1))
    dx, grads[0], early = _layer_bwd(dx, weights[0], gathered, 0, saved[0], S, rope, bufs, (done, spans, core))
    last = dict(bufs, w_in=grads[0]["w_in"][:, None])
    small = {n: jnp.stack([grads[l][n] for l in range(DEPTH)]) for n in SMALL}
    return loss_parts, dx.reshape(n_seq, S, D), early, last, small


def _unpad_reduced(n, r):
    L = r.shape[0]
    if n == "w_uq":
        return r.reshape(L, Q_LORA, 2, HEAD_PAD)[..., :QK_NOPE + QK_ROPE].reshape(L, Q_LORA, -1)
    if n == "w_ukv":
        r = r.reshape(L, KV_LORA, 2, 2, HEAD_PAD)[..., :QK_NOPE]
        return jnp.concatenate([r[:, :, 0], r[:, :, 1]], axis=-1).reshape(L, KV_LORA, -1)
    if n == "w_br_a":
        return r.reshape(L, HEADS, HEAD_PAD, -1)[:, :, :V_HEAD].reshape(L, HEADS * V_HEAD, -1)
    return r


def _small_rows(n):
    return -(-n // (8 * 128)) * 8


def _to_small(parts):
    flat = jnp.concatenate([p.reshape(-1) for p in parts])
    rows = _small_rows(flat.shape[0])
    return jnp.pad(flat, (0, rows * 128 - flat.shape[0])).reshape(rows, 128)


def kernel(x, w_in, g_pre_mix, g_cq, g_ckv, w_uq, w_ukv, pool_w, pool_scale, g_sgu_v, sgu_w, sgu_b, conv_w, w_br_a, w_br_b, w_br_c, w_br_d, w_out, g_post_mix, g_pre_ffn, w_ffn_gate, w_ffn_up, w_ffn_down, g_post_ffn, loss_target, m_w_in, m_g_pre_mix, m_g_cq, m_g_ckv, m_w_uq, m_w_ukv, m_pool_w, m_pool_scale, m_g_sgu_v, m_sgu_w, m_sgu_b, m_conv_w, m_w_br_a, m_w_br_b, m_w_br_c, m_w_br_d, m_w_out, m_g_post_mix, m_g_pre_ffn, m_w_ffn_gate, m_w_ffn_up, m_w_ffn_down, m_g_post_ffn, v_w_in, v_g_pre_mix, v_g_cq, v_g_ckv, v_w_uq, v_w_ukv, v_pool_w, v_pool_scale, v_g_sgu_v, v_sgu_w, v_sgu_b, v_conv_w, v_w_br_a, v_w_br_b, v_w_br_c, v_w_br_d, v_w_out, v_g_post_mix, v_g_pre_ffn, v_w_ffn_gate, v_w_ffn_up, v_w_ffn_down, v_g_post_ffn):
    local = dict(locals())
    W = {n: local[n] for n in WEIGHTS}
    M = {n: local["m_" + n] for n in WEIGHTS}
    V = {n: local["v_" + n] for n in WEIGHTS}
    chip = 2 * lax.axis_index("x") + lax.axis_index("y")
    core = lax.axis_index("c")

    gathered = dict(zip(GATHERED, _gather_weights([_place_own(W[n], chip, n) for n in GATHERED], 0)))
    conv_shape = conv_w.shape
    conv_cols = conv_shape[-1]
    conv_full_shape = conv_shape[:-1] + (N_CHIPS * conv_cols,)
    placed = lax.dynamic_update_slice(jnp.zeros(conv_full_shape, F32), conv_w, (0, 0, 0, chip * conv_cols))
    n_conv = int(np.prod(conv_full_shape))
    conv_sum = _all_reduce_small(_to_small([placed]), "gather_conv_w")
    full = {n: W[n] for n in REPLICATED}
    full["conv_w"] = 0.5 * conv_sum.reshape(-1)[:n_conv].reshape(conv_full_shape)

    loss_parts, grad_x, (sums_up, slots_up), last, small = _local_step(x, loss_target, gathered, full, core)
    loss = lax.psum(jnp.sum(loss_parts), ("x", "y", "c"))

    small_sum = _all_reduce_small(_to_small([small[n] for n in SMALL]), "reduce_small_grads")
    small_grads = dict(zip(SMALL, _unpack(small_sum, [small[n].shape for n in SMALL])))
    small_grads["conv_w"] = lax.dynamic_slice(small_grads["conv_w"], (0, 0, 0, chip * conv_cols), conv_shape)

    first = [(0, 1)] * len(GATHERED)
    Gs = [last[n] for n in GATHERED]
    got = _run_ride(_exchange_ride(Gs, first), "rs_exchange_halves")
    sums_0 = [_add_halves(g, (0, 1), r, core, n) for n, g, r in zip(GATHERED, Gs, got)]
    slots_0 = _run_ride(_scatter_ride(sums_0), "rs_scatter_partials")
    place = [lax.axis_index(a).reshape(1).astype(jnp.int32) for a in ("x", "y", "c")]
    halves = []
    for n, h, s in zip(GATHERED, sums_0, slots_0):
        upper = _sum_slots(sums_up[n], slots_up[n], place, n, 1, DEPTH)
        halves.append(_sum_slots(h, s, place, n, 0, DEPTH, prev=upper))
    shard_grads = {n: _unpad_reduced(n, r).reshape(W[n].shape) for n, r in zip(GATHERED, _join_halves(halves))}

    out_g, out_d, out_m, out_v = {}, {}, {}, {}
    for n in GATHERED:
        shp = W[n].shape
        flat = lambda a: a.reshape(-1, shp[-1])
        d, m2, v2 = _adamw(flat(W[n]), flat(shard_grads[n]), flat(M[n]), flat(V[n]), "adamw_" + n)
        out_g[n], out_d[n], out_m[n], out_v[n] = shard_grads[n], d.reshape(shp), m2.reshape(shp), v2.reshape(shp)
    rest_shapes = [W[n].shape for n in SMALL]
    d, m2, v2 = _adamw(_to_small([W[n] for n in SMALL]), _to_small([small_grads[n] for n in SMALL]),
                       _to_small([M[n] for n in SMALL]), _to_small([V[n] for n in SMALL]), "adamw_small")
    for n, dd, mm, vv in zip(SMALL, _unpack(d, rest_shapes), _unpack(m2, rest_shapes), _unpack(v2, rest_shapes)):
        out_g[n], out_d[n], out_m[n], out_v[n] = small_grads[n], dd, mm, vv

    return (loss, grad_x, *[out_g[n] for n in WEIGHTS], *[out_d[n] for n in WEIGHTS], *[out_m[n] for n in WEIGHTS],
            *[out_v[n] for n in WEIGHTS])
```

```python
import functools

import numpy as np
import jax
import jax.numpy as jnp
from jax import lax
from jax.experimental import pallas as pl
from jax.experimental.pallas import tpu as pltpu

F32 = jnp.float32
BF16 = jnp.bfloat16

EPS = 1e-6
NEG_INF = -1e30
DEPTH = 4
HEADS = 8
QK_NOPE = 64
QK_ROPE = 32
V_HEAD = 64
HEAD_PAD = 128
Q_LORA = 256
KV_LORA = 128
ROPE_THETA = 10000.0
POOL_WINDOWS = (2, 4, 8, 16)
GROUPS = 4
GROUP_DIM = 64
MIX_W = GROUPS * GROUP_DIM
POOL_HALO = 16
CONV_HALO = 16
SGU_BLOCK = 128
CHUNK = 64
CHUNK_SHIFT = 6
ACT_ROWS = 16
NORM_ROWS = 16
ROW_PARTS = 2
SOFTMAX_ROWS = 32
GROUP_SHIFT = 6
N_BRANCH = 4
MLA_W = HEADS * HEAD_PAD
N_CHIPS = 4
CHIP_HEADS_W = MLA_W // N_CHIPS
CHIP_KV = 2 * CHIP_HEADS_W
ATTN_SCALE = (QK_NOPE + QK_ROPE) ** -0.5
LOG2E = 1.4426950408889634
LN2 = 0.6931471805599453
SEC_MLA = Q_LORA + KV_LORA + HEAD_PAD
SEC_MIX = 6 * MIX_W

ADAM_LR = 0.001
ADAM_B1 = 0.9
ADAM_B2 = 0.999
ADAM_EPS = 1e-08
ADAM_WD = 0.01
ADAM_STEP = 10

VMEM_LIMIT = 56 * 1024 * 1024
MESH = pl.DeviceIdType.MESH


def _tile(n, pref, mult=8):
    t = min(n, pref)
    while t > 0:
        if n % t == 0 and t % mult == 0:
            return t
        t -= 1
    return n


def _params(n_axes):
    return pltpu.CompilerParams(dimension_semantics=("arbitrary",) * n_axes, vmem_limit_bytes=VMEM_LIMIT)


def _dot(a, b):
    return jnp.dot(a, b, preferred_element_type=F32)


def _dot_nt(a, b):
    return lax.dot_general(a, b, (((1,), (1,)), ((), ())), preferred_element_type=F32)


def _dot_tn(a, b):
    return lax.dot_general(a, b, (((0,), (0,)), ((), ())), preferred_element_type=F32)


def _rms_r(x):
    return lax.rsqrt(jnp.mean(x * x, axis=-1, keepdims=True) + EPS)


def _rms_bwd(dy, x, g):
    r = _rms_r(x)
    u = dy * g
    dx = r * u - x * (r * r * r * jnp.mean(u * x, axis=-1, keepdims=True))
    dg = jnp.sum(dy * x * r, axis=0, keepdims=True)
    return dx, dg


def _sigmoid(x):
    return 1.0 / (1.0 + jnp.exp(-x))


def _shift_down(a, k):
    return pltpu.roll(a, k, 0)


def _shift_up(a, k):
    return pltpu.roll(a, a.shape[0] - k, 0)


def _rope(x, c, sa, sb):
    w = x.shape[-1]
    return x * c + pltpu.roll(x, QK_ROPE // 2, 1) * sa + pltpu.roll(x, w - QK_ROPE // 2, 1) * sb


def _rope_t(d, c, sa, sb):
    w = d.shape[-1]
    return d * c + pltpu.roll(d * sa, w - QK_ROPE // 2, 1) + pltpu.roll(d * sb, QK_ROPE // 2, 1)


def _full(shape):
    return pl.BlockSpec(shape, lambda *_: (0,) * len(shape))


def _gather_ride(bufs, l, stage):
    def copies(_, o_refs, send_sems, recv_sems):
        x, y, c, me = _place()
        sends, arrivals = [], []
        for t, o in enumerate(o_refs):
            hr = o.shape[2] // 2
            for d in (1, 2, 3):
                sems = (send_sems.at[3 * t + d - 1], recv_sems.at[3 * t + d - 1])
                mine = o.at[l, me, pl.ds(c * hr, hr), :]
                theirs = o.at[l, me ^ d, pl.ds(c * hr, hr), :]
                other_half = o.at[l, me ^ d, pl.ds((1 - c) * hr, hr), :]
                if stage == "chips":
                    sends.append(_remote(mine, mine, *sems, _chip_device(me ^ d, c)))
                    arrivals.append(_remote(theirs, theirs, *sems, _chip_device(me ^ d, c)))
                else:
                    sends.append(_remote(theirs, theirs, *sems, (x, y, 1 - c)))
                    arrivals.append(_remote(other_half, other_half, *sems, (x, y, 1 - c)))
        return sends, arrivals

    shapes = [jax.ShapeDtypeStruct(b.shape, b.dtype) for b in bufs]
    return dict(ins=list(bufs), outs=shapes, alias=True, copies=copies, n_sems=3 * len(bufs))


def _exchange_ride(Gs, spans):
    def copies(g_refs, o_refs, send_sems, recv_sems):
        x, y, c, _ = _place()
        cps = []
        for t, (g, o) in enumerate(zip(g_refs, o_refs)):
            hr = g.shape[2] // 2
            l0, l1 = spans[t]
            cps.append(_remote(g.at[:, pl.ds(l0, l1 - l0), pl.ds((1 - c) * hr, hr), :], o, send_sems.at[t], recv_sems.at[t],
                               (x, y, 1 - c)))
        return cps, cps

    shapes = [jax.ShapeDtypeStruct((g.shape[0], l1 - l0, g.shape[2] // 2, g.shape[3]), g.dtype) for g, (l0, l1) in zip(Gs, spans)]
    return dict(ins=list(Gs), outs=shapes, alias=False, copies=copies, n_sems=len(Gs))


def _scatter_ride(Hs):
    def copies(h_refs, o_refs, send_sems, recv_sems):
        x, y, c, me = _place()
        sends, arrivals = [], []
        for t, (h, o) in enumerate(zip(h_refs, o_refs)):
            for d in (1, 2, 3):
                sems = (send_sems.at[3 * t + d - 1], recv_sems.at[3 * t + d - 1])
                sends.append(_remote(h.at[me ^ d], o.at[me], *sems, _chip_device(me ^ d, c)))
                arrivals.append(_remote(h.at[me ^ d], o.at[me ^ d], *sems, _chip_device(me ^ d, c)))
        return sends, arrivals

    shapes = [jax.ShapeDtypeStruct(h.shape, h.dtype) for h in Hs]
    return dict(ins=list(Hs), outs=shapes, alias=False, copies=copies, n_sems=3 * len(Hs))


def _ride_specs(ride, n_in, n_out):
    if not ride:
        return [], [], [], [], [], {}
    anywhere = pl.BlockSpec(memory_space=pl.ANY)
    sems = pltpu.SemaphoreType.DMA((ride["n_sems"],))
    alias = {n_in + i: n_out + i for i in range(len(ride["ins"]))} if ride["alias"] else {}
    return [anywhere] * len(ride["ins"]), [anywhere] * len(ride["outs"]), list(ride["outs"]), [sems, sems], ride["ins"], alias


def _ride_split(ride, rest, n_out, n_scratch):
    a = len(ride["ins"]) if ride else 0
    b = a + n_out
    c = b + (len(ride["outs"]) if ride else 0)
    d = c + n_scratch
    riding = (ride, rest[:a], rest[b:c], rest[d:]) if ride else None
    return rest[a:b], rest[c:d], riding


def _ride_start(riding, first):
    if riding:
        ride, in_refs, out_refs, (send_sems, recv_sems) = riding

        @pl.when(first)
        def _():
            for cp in ride["copies"](in_refs, out_refs, send_sems, recv_sems)[0]:
                cp.start()


def _ride_finish(riding, last):
    if riding:
        ride, in_refs, out_refs, (send_sems, recv_sems) = riding

        @pl.when(last)
        def _():
            sends, arrivals = ride["copies"](in_refs, out_refs, send_sems, recv_sems)
            for cp in arrivals:
                cp.wait_recv()
            for cp in sends:
                cp.wait_send()


def _run_ride(ride, name):
    def body(*refs):
        _, _, (_, in_refs, out_refs, (send_sems, recv_sems)) = _ride_split(ride, refs, 0, 0)
        sends, arrivals = ride["copies"](in_refs, out_refs, send_sems, recv_sems)
        for cp in sends:
            cp.start()
        for cp in arrivals:
            cp.wait_recv()
        for cp in sends:
            cp.wait_send()

    in_specs, out_specs, out_shapes, scratch, operands, alias = _ride_specs(ride, 0, 0)
    return list(pl.pallas_call(body, name=name, in_specs=in_specs, out_specs=out_specs, out_shape=out_shapes,
                               input_output_aliases=alias, scratch_shapes=scratch)(*operands))


def _norm_matmul(x, g, w, name):
    T, K = x.shape
    N = w.shape[1]
    tm, tn = _tile(T, 512), _tile(N, 1536, 128)

    def body(x_ref, g_ref, w_ref, o_ref, h_ref):
        @pl.when(pl.program_id(1) == 0)
        def _():
            xv = x_ref[...]
            h_ref[...] = (xv * _rms_r(xv) * g_ref[...]).astype(BF16)

        o_ref[...] = _dot(h_ref[...], w_ref[...]).astype(BF16)

    return pl.pallas_call(
        body, name=name, grid=(T // tm, N // tn),
        in_specs=[pl.BlockSpec((tm, K), lambda i, j: (i, 0)), _full((1, K)), pl.BlockSpec((K, tn), lambda i, j: (0, j))],
        out_specs=[pl.BlockSpec((tm, tn), lambda i, j: (i, j)), pl.BlockSpec((tm, K), lambda i, j: (i, 0))],
        out_shape=[jax.ShapeDtypeStruct((T, N), BF16), jax.ShapeDtypeStruct((T, K), BF16)],
        compiler_params=_params(2),
    )(x, g, w)


def _mla_prep(P, D, g_cq, g_ckv, wuq, wukv, rope_c, rope_sa, rope_sb, S):
    T = P.shape[0]
    tm = _tile(S, 512)
    n_si = S // tm
    base = 4 * D

    def body(cq_ref, ckv_ref, kr_ref, gq_ref, gkv_ref, wq_ref, wkv_ref, c_ref, sa_ref, sb_ref,
             q_ref, k_ref, v_ref, hq_ref, hkv_ref):
        c, sa, sb = c_ref[...], sa_ref[...], sb_ref[...]
        cq = cq_ref[...].astype(F32)
        hq = (cq * _rms_r(cq) * gq_ref[...]).astype(BF16)
        hq_ref[...] = hq
        q = _dot(hq, wq_ref[...])
        q = _rope(q, jnp.tile(c, (1, HEADS)), jnp.tile(sa, (1, HEADS)), jnp.tile(sb, (1, HEADS)))
        q_ref[...] = (q * (ATTN_SCALE * LOG2E)).astype(BF16)
        ckv = ckv_ref[...].astype(F32)
        hkv = (ckv * _rms_r(ckv) * gkv_ref[...]).astype(BF16)
        hkv_ref[...] = hkv
        kv = _dot(hkv, wkv_ref[...])
        kr = _rope(kr_ref[...].astype(F32), c, sa, sb)
        k_nope = jnp.concatenate([kv[:, j * CHIP_KV:j * CHIP_KV + CHIP_HEADS_W] for j in range(N_CHIPS)], axis=1)
        k_ref[...] = (k_nope + jnp.tile(kr, (1, HEADS))).astype(BF16)
        v = jnp.concatenate([kv[:, j * CHIP_KV + CHIP_HEADS_W:(j + 1) * CHIP_KV] for j in range(N_CHIPS)], axis=1)
        ones_lane = (lax.broadcasted_iota(jnp.int32, (1, MLA_W), 1) & (HEAD_PAD - 1)) == V_HEAD
        v_ref[...] = jnp.where(ones_lane, 1.0, v).astype(BF16)

    tab = pl.BlockSpec((tm, HEAD_PAD), lambda i: (i % n_si, 0))
    row = lambda w: pl.BlockSpec((tm, w), lambda i: (i, 0))
    return pl.pallas_call(
        body, name="mla_prep", grid=(T // tm,),
        in_specs=[pl.BlockSpec((tm, Q_LORA), lambda i: (i, base // Q_LORA)),
                  pl.BlockSpec((tm, KV_LORA), lambda i: (i, (base + Q_LORA) // KV_LORA)),
                  pl.BlockSpec((tm, HEAD_PAD), lambda i: (i, (base + Q_LORA + KV_LORA) // HEAD_PAD)),
                  _full((1, Q_LORA)), _full((1, KV_LORA)), _full((Q_LORA, MLA_W)), _full((KV_LORA, 2 * MLA_W)),
                  tab, tab, tab],
        out_specs=[row(MLA_W), row(MLA_W), row(MLA_W), row(Q_LORA), row(KV_LORA)],
        out_shape=[jax.ShapeDtypeStruct((T, MLA_W), BF16)] * 3
        + [jax.ShapeDtypeStruct((T, Q_LORA), BF16), jax.ShapeDtypeStruct((T, KV_LORA), BF16)],
        compiler_params=_params(1),
    )(P, P, P, g_cq, g_ckv, wuq, wukv, rope_c, rope_sa, rope_sb)


def _chunk_mask(tq, tk):
    row = lax.broadcasted_iota(jnp.int32, (tq, tk), 0)
    col = lax.broadcasted_iota(jnp.int32, (tq, tk), 1)
    return (row >> CHUNK_SHIFT) >= (col >> CHUNK_SHIFT)


def _flash_fwd(Q, K, V, S, ride=None):
    T = Q.shape[0]
    n_seq = T // S
    tq = _tile(S, 256, 128)
    nq = S // tq

    def body(q_ref, k_ref, v_ref, *rest):
        (o_ref, lse_ref), (m_s, acc_s, s_s, p_s, a_s), riding = _ride_split(ride, rest, 2, 5)
        _ride_start(riding, (pl.program_id(0) == 0) & (pl.program_id(1) == 0))
        qi = pl.program_id(1)
        m_s[...] = jnp.full(m_s.shape, NEG_INF, F32)
        acc_s[...] = jnp.zeros_like(acc_s)

        def block(kb, masked):
            rows = pl.ds(pl.multiple_of(kb * tq, tq), tq)
            for h in range(HEADS):
                hs = slice(h * HEAD_PAD, (h + 1) * HEAD_PAD)
                s_s[h] = _dot_nt(q_ref[:, hs], k_ref[rows, hs])
            def softmax_head(h):
                for r in range(0, tq, SOFTMAX_ROWS):
                    rs = slice(r, r + SOFTMAX_ROWS)
                    s = s_s[h, rs, :]
                    if masked:
                        row = r + lax.broadcasted_iota(jnp.int32, (SOFTMAX_ROWS, tq), 0)
                        col = lax.broadcasted_iota(jnp.int32, (SOFTMAX_ROWS, tq), 1)
                        s = jnp.where((row >> CHUNK_SHIFT) >= (col >> CHUNK_SHIFT), s, NEG_INF)
                    m_old = m_s[h, rs]
                    m_new = jnp.maximum(m_old, jnp.max(s, axis=-1, keepdims=True))
                    m_s[h, rs] = m_new
                    a_s[h, rs] = jnp.exp2(m_old - m_new)
                    for half in range(tq // HEAD_PAD):
                        cs = slice(half * HEAD_PAD, (half + 1) * HEAD_PAD)
                        p_s[h, rs, cs] = jnp.exp2(s[:, cs] - m_new).astype(BF16)

            for h in range(HEADS):
                softmax_head(h)
            for h in range(HEADS):
                hs = slice(h * HEAD_PAD, (h + 1) * HEAD_PAD)
                acc_s[:, hs] = a_s[h] * acc_s[:, hs] + _dot(p_s[h], v_ref[rows, hs])

        def full_block(kb, carry):
            block(kb, False)
            return carry

        lax.fori_loop(0, qi, full_block, 0)
        block(qi, True)
        lane = lax.broadcasted_iota(jnp.int32, (tq, HEAD_PAD), 1)
        lse_all = jnp.zeros((tq, HEAD_PAD), F32)
        for h in range(HEADS):
            hs = slice(h * HEAD_PAD, (h + 1) * HEAD_PAD)
            acc = acc_s[:, hs]
            l = jnp.sum(jnp.where(lane == V_HEAD, acc, 0.0), axis=-1, keepdims=True)
            o_ref[:, hs] = (acc / l).astype(BF16)
            lse_all = jnp.where(lane == h, m_s[h] + jnp.log2(l), lse_all)
        lse_ref[...] = lse_all
        _ride_finish(riding, (pl.program_id(0) == n_seq - 1) & (pl.program_id(1) == nq - 1))

    ride_in, ride_out, ride_shapes, ride_scratch, ride_args, alias = _ride_specs(ride, 3, 2)
    out = pl.pallas_call(
        body, name="flash_fwd", grid=(n_seq, nq),
        in_specs=[pl.BlockSpec((tq, MLA_W), lambda b, i: (b * nq + i, 0)),
                  pl.BlockSpec((S, MLA_W), lambda b, i: (b, 0)), pl.BlockSpec((S, MLA_W), lambda b, i: (b, 0))] + ride_in,
        out_specs=[pl.BlockSpec((tq, MLA_W), lambda b, i: (b * nq + i, 0)),
                   pl.BlockSpec((tq, HEAD_PAD), lambda b, i: (b * nq + i, 0))] + ride_out,
        out_shape=[jax.ShapeDtypeStruct((T, MLA_W), BF16), jax.ShapeDtypeStruct((T, HEAD_PAD), F32)] + ride_shapes,
        input_output_aliases=alias,
        scratch_shapes=[pltpu.VMEM((HEADS, tq, HEAD_PAD), F32), pltpu.VMEM((tq, MLA_W), F32), pltpu.VMEM((HEADS, tq, tq), F32),
                        pltpu.VMEM((HEADS, tq, tq), BF16), pltpu.VMEM((HEADS, tq, HEAD_PAD), F32)] + ride_scratch,
        compiler_params=_params(2),
    )(Q, K, V, *ride_args)
    return out[0], out[1], list(out[2:])


def _lane_group():
    return lax.broadcasted_iota(jnp.int32, (1, MIX_W), 1) >> GROUP_SHIFT


def _by_group(a0, a1, a2, a3):
    g = _lane_group()
    return jnp.where(g == 0, a0, jnp.where(g == 1, a1, jnp.where(g == 2, a2, a3)))


def _pool_count(si, tc, rows):
    pos = si * tc + lax.broadcasted_iota(jnp.int32, (rows, MIX_W), 0)
    win = _by_group(*POOL_WINDOWS)
    return jnp.minimum(pos + 1, win).astype(F32)


def _pool_fwd(z, z_prev, si, tc):
    ze = jnp.concatenate([z_prev, z], axis=0)
    s1 = ze + _shift_down(ze, 1)
    s2 = s1 + _shift_down(s1, 2)
    s4 = s2 + _shift_down(s2, 4)
    s8 = s4 + _shift_down(s4, 8)
    win_sum = _by_group(s1, s2, s4, s8)[POOL_HALO:]
    return win_sum / _pool_count(si, tc, tc) - z


def _sgu_weights(w_ref):
    row = lax.broadcasted_iota(jnp.int32, (SGU_BLOCK, SGU_BLOCK), 0)
    col = lax.broadcasted_iota(jnp.int32, (SGU_BLOCK, SGU_BLOCK), 1)
    keep = (row >> CHUNK_SHIFT) >= (col >> CHUNK_SHIFT)
    return keep, [jnp.where(keep, w_ref[g], 0.0).astype(BF16) for g in range(GROUPS)]


def _sgu_mix(vn_blk, wm, bias):
    g = _lane_group()
    mixed = bias
    for k in range(GROUPS):
        mixed = mixed + jnp.where(g == k, _dot(wm[k], vn_blk), 0.0)
    return mixed


def _conv_fwd(z, z_prev, w_ref):
    ze = jnp.concatenate([z_prev, z], axis=0)
    y = w_ref[0:1, :] * _shift_down(ze, 2) + w_ref[1:2, :] * _shift_down(ze, 1) + w_ref[2:3, :] * ze
    return y[CONV_HALO:]


def _mix_specs(T, D, tc):
    base = (4 * D + SEC_MLA) // MIX_W
    cur = lambda k: pl.BlockSpec((tc, MIX_W), lambda i: (i, base + k))
    prev = lambda k, halo: pl.BlockSpec((halo, MIX_W), lambda i: (jnp.maximum(i * (tc // halo) - 1, 0), base + k))
    nxt = lambda k, halo: pl.BlockSpec((halo, MIX_W), lambda i: (jnp.minimum((i + 1) * (tc // halo), T // halo - 1), base + k))
    return cur, prev, nxt


def _mixers_fwd(P, D, S, pool_bd, pool_scale, g_v, sgu_w, sgu_bias, conv_w, ride=None):
    T = P.shape[0]
    tc = _tile(S, 512, SGU_BLOCK)
    n_si = S // tc
    cur, prev, _ = _mix_specs(T, D, tc)

    def body(z_ref, zp_ref, u_ref, v_ref, b_ref, c_ref, x_ref, cp_ref, xp_ref,
             pw_ref, ps_ref, gv_ref, sw_ref, sb_ref, cw_ref, *rest):
        (ob_ref, oc_ref, od_ref), _, riding = _ride_split(ride, rest, 3, 0)
        _ride_start(riding, pl.program_id(0) == 0)
        si = pl.program_id(0) % n_si
        first = si == 0
        z = z_ref[...].astype(F32)
        pooled = _pool_fwd(z, jnp.where(first, 0.0, zp_ref[...].astype(F32)), si, tc)
        ob_ref[...] = (_dot(pooled.astype(BF16), pw_ref[...]) * ps_ref[...]).astype(BF16)

        v = v_ref[...].astype(F32)
        vn = (v * _rms_r(v) * gv_ref[...]).astype(BF16)
        _, wm = _sgu_weights(sw_ref)
        for blk in range(tc // SGU_BLOCK):
            rows = slice(blk * SGU_BLOCK, (blk + 1) * SGU_BLOCK)
            oc_ref[rows, :] = (u_ref[rows, :].astype(F32) * _sgu_mix(vn[rows], wm, sb_ref[...])).astype(BF16)

        zc = c_ref[...].astype(F32) * x_ref[...].astype(F32)
        zc_prev = jnp.where(first, 0.0, cp_ref[...].astype(F32) * xp_ref[...].astype(F32))
        od_ref[...] = (b_ref[...].astype(F32) * _conv_fwd(zc, zc_prev, cw_ref)).astype(BF16)
        _ride_finish(riding, pl.program_id(0) == T // tc - 1)

    out = pl.BlockSpec((tc, MIX_W), lambda i: (i, 0))
    ride_in, ride_out, ride_shapes, ride_scratch, ride_args, alias = _ride_specs(ride, 15, 3)
    res = pl.pallas_call(
        body, name="mixers_fwd", grid=(T // tc,),
        in_specs=[cur(0), prev(0, POOL_HALO), cur(1), cur(2), cur(3), cur(4), cur(5), prev(4, CONV_HALO), prev(5, CONV_HALO),
                  _full((MIX_W, MIX_W)), _full((1, MIX_W)), _full((1, MIX_W)), _full((GROUPS, SGU_BLOCK, SGU_BLOCK)),
                  _full((SGU_BLOCK, MIX_W)), _full((CONV_HALO, MIX_W))] + ride_in,
        out_specs=[out, out, out] + ride_out,
        out_shape=[jax.ShapeDtypeStruct((T, MIX_W), BF16)] * 3 + ride_shapes,
        input_output_aliases=alias,
        scratch_shapes=ride_scratch,
        compiler_params=_params(1),
    )(P, P, P, P, P, P, P, P, P, pool_bd, pool_scale, g_v, sgu_w, sgu_bias, conv_w, *ride_args)
    return res[0], res[1], res[2], list(res[3:])


def _merge_fwd(x, P, A, Bm, C, Dv, wa, wb, wc, wd, wout, g_post):
    T, D = x.shape
    tm = _tile(T, 256)

    def body(x_ref, lg_ref, a_ref, b_ref, c_ref, d_ref, wa_ref, wb_ref, wc_ref, wd_ref, wo_ref, g_ref,
             x1_ref, mg_ref, o_ref):
        merged = jnp.zeros((tm, D), F32)
        for k, (br, w) in enumerate(((a_ref, wa_ref), (b_ref, wb_ref), (c_ref, wc_ref), (d_ref, wd_ref))):
            merged = merged + _sigmoid(lg_ref[:, k * D:(k + 1) * D].astype(F32)) * _dot(br[...], w[...])
        mg = merged.astype(BF16)
        mg_ref[...] = mg
        o = _dot(mg, wo_ref[...])
        o_ref[...] = o
        x1_ref[...] = x_ref[...] + o * _rms_r(o) * g_ref[...]

    row = lambda w: pl.BlockSpec((tm, w), lambda i: (i, 0))
    return pl.pallas_call(
        body, name="merge_fwd", grid=(T // tm,),
        in_specs=[row(D), row(4 * D), row(MLA_W), row(MIX_W), row(MIX_W), row(MIX_W),
                  _full((MLA_W, D)), _full((MIX_W, D)), _full((MIX_W, D)), _full((MIX_W, D)), _full((D, D)), _full((1, D))],
        out_specs=[row(D), row(D), row(D)],
        out_shape=[jax.ShapeDtypeStruct((T, D), F32), jax.ShapeDtypeStruct((T, D), BF16), jax.ShapeDtypeStruct((T, D), F32)],
        compiler_params=_params(1),
    )(x, P, A, Bm, C, Dv, wa, wb, wc, wd, wout, g_post)


def _ffn_specs(T, D, Fc, l, rows=512):
    tm = _tile(T, rows)
    row = pl.BlockSpec((tm, D), lambda i, j: (i, 0))
    col = pl.BlockSpec((None, tm, Fc), lambda i, j: (j, i, 0))
    w_in = pl.BlockSpec((None, None, D, Fc), lambda i, j: (l, j, 0, 0))
    w_out = pl.BlockSpec((None, None, Fc, D), lambda i, j: (l, j, 0, 0))
    return tm, row, col, w_in, w_out


def _ffn_fwd(x1, g_pre, wg, wu, wdn, g_post, l):
    T, D = x1.shape
    nf, Fc = wg.shape[1], wg.shape[3]
    tm, row, col, w_in, w_out = _ffn_specs(T, D, Fc, l, 1024)

    def body(x_ref, gp_ref, wg_ref, wu_ref, wd_ref, gq_ref, x2_ref, h_ref, gt_ref, up_ref, f_ref, gt_s, up_s, a_s):
        j = pl.program_id(1)

        @pl.when(j == 0)
        def _():
            for r in range(0, tm, NORM_ROWS):
                rs = slice(r, r + NORM_ROWS)
                xv = x_ref[rs, :]
                h_ref[rs, :] = (xv * _rms_r(xv) * gp_ref[...]).astype(BF16)
            f_ref[...] = jnp.zeros_like(f_ref)

        gt_s[...] = _dot(h_ref[...], wg_ref[...])
        up_s[...] = _dot(h_ref[...], wu_ref[...])
        for r in range(0, tm, ACT_ROWS):
            rs = slice(r, r + ACT_ROWS)
            gt, up = gt_s[rs, :], up_s[rs, :]
            gt_ref[rs, :] = gt.astype(BF16)
            up_ref[rs, :] = up.astype(BF16)
            a_s[rs, :] = (gt * _sigmoid(gt) * up).astype(BF16)
        f_ref[...] += _dot(a_s[...], wd_ref[...])

        @pl.when(j == nf - 1)
        def _():
            for r in range(0, tm, NORM_ROWS):
                rs = slice(r, r + NORM_ROWS)
                f = f_ref[rs, :]
                x2_ref[rs, :] = x_ref[rs, :] + f * _rms_r(f) * gq_ref[...]

    return pl.pallas_call(
        body, name="ffn_fwd", grid=(T // tm, nf),
        in_specs=[row, _full((1, D)), w_in, w_in, w_out, _full((1, D))],
        out_specs=[row, row, col, col, row],
        out_shape=[jax.ShapeDtypeStruct((T, D), F32), jax.ShapeDtypeStruct((T, D), BF16),
                   jax.ShapeDtypeStruct((nf, T, Fc), BF16), jax.ShapeDtypeStruct((nf, T, Fc), BF16),
                   jax.ShapeDtypeStruct((T, D), F32)],
        scratch_shapes=[pltpu.VMEM((tm, Fc), F32), pltpu.VMEM((tm, Fc), F32), pltpu.VMEM((tm, Fc), BF16)],
        compiler_params=_params(2),
    )(x1, g_pre, wg, wu, wdn, g_post)


def _loss_grad(y, target):
    T, D = y.shape
    tm = _tile(T, 512)

    def body(y_ref, t_ref, l_ref, dy_ref):
        @pl.when(pl.program_id(0) == 0)
        def _():
            l_ref[...] = jnp.zeros_like(l_ref)

        d = y_ref[...] - t_ref[...]
        dy_ref[...] = d * (1.0 / D)
        e = jnp.sum((d * d).reshape(tm // 8, 8, D), axis=0)
        part = e[:, 0:128]
        for k in range(1, D // 128):
            part = part + e[:, k * 128:(k + 1) * 128]
        l_ref[...] += part * (0.5 / D)

    row = pl.BlockSpec((tm, D), lambda i: (i, 0))
    return pl.pallas_call(
        body, name="loss_grad", grid=(T // tm,),
        in_specs=[row, row], out_specs=[_full((8, 128)), row],
        out_shape=[jax.ShapeDtypeStruct((8, 128), F32), jax.ShapeDtypeStruct((T, D), F32)],
        compiler_params=_params(1),
    )(y, target)


def _matmul_tn(a, b, name):
    T, M = a.shape
    N = b.shape[1]
    tm, tn, tk = _tile(M, 1024, 128), _tile(N, 1536, 128), _tile(T, 512)

    def body(a_ref, b_ref, o_ref):
        @pl.when(pl.program_id(2) == 0)
        def _():
            o_ref[...] = jnp.zeros_like(o_ref)

        o_ref[...] += _dot_tn(a_ref[...], b_ref[...])

    return pl.pallas_call(
        body, name=name, grid=(M // tm, N // tn, T // tk),
        in_specs=[pl.BlockSpec((tk, tm), lambda i, j, k: (k, i)), pl.BlockSpec((tk, tn), lambda i, j, k: (k, j))],
        out_specs=pl.BlockSpec((tm, tn), lambda i, j, k: (i, j)),
        out_shape=jax.ShapeDtypeStruct((M, N), F32),
        compiler_params=_params(3),
    )(a, b)


def _wgrad_chip(buf, l, n_layers, a, b, rows, cols, a_mode, b_mode, name):
    T = a.shape[-2]
    tk = _tile(T, 1024)

    def spec(mode, width):
        if mode == "all":
            return pl.BlockSpec((tk, width), lambda k, t: (t, 0))
        if mode == "cols":
            return pl.BlockSpec((tk, width), lambda k, t: (t, k))
        return pl.BlockSpec((None, tk, width), lambda k, t: (k, t, 0))

    def body(a_ref, b_ref, *rest):
        o_ref = rest[-1]

        @pl.when(pl.program_id(1) == 0)
        def _():
            o_ref[...] = jnp.zeros_like(o_ref)

        o_ref[...] += _dot_tn(a_ref[...], b_ref[...])

    keep = [] if buf is None else [buf]
    return pl.pallas_call(
        body, name=name, grid=(N_CHIPS, T // tk),
        in_specs=[spec(a_mode, rows), spec(b_mode, cols)] + [pl.BlockSpec(memory_space=pl.ANY)] * len(keep),
        out_specs=pl.BlockSpec((None, None, rows, cols), lambda k, t: (k, l, 0, 0)),
        out_shape=jax.ShapeDtypeStruct((N_CHIPS, n_layers, rows, cols), F32),
        input_output_aliases={2: 0} if keep else {},
        compiler_params=_params(2),
    )(a, b, *keep)


def _ffn_bwd(dx2, x1, f, gt, up, g_pre, wg, wu, wdn, g_post, l, ride=None):
    T, D = x1.shape
    nf, Fc = wg.shape[1], wg.shape[3]
    tm, row, col, w_in, w_out = _ffn_specs(T, D, Fc, l)

    def body(dx2_ref, x1_ref, f_ref, gt_ref, up_ref, gp_ref, wg_ref, wu_ref, wd_ref, gq_ref, *rest):
        (dx1_ref, df_ref, dgt_ref, dup_ref, a_ref, dgp_ref, dgq_ref), (dh_acc, da_s), riding = _ride_split(ride, rest, 7, 2)
        i, j = pl.program_id(0), pl.program_id(1)
        _ride_start(riding, (i == 0) & (j == 0))

        @pl.when((i == 0) & (j == 0))
        def _():
            dgp_ref[...] = jnp.zeros_like(dgp_ref)
            dgq_ref[...] = jnp.zeros_like(dgq_ref)

        @pl.when(j == 0)
        def _():
            dg_sum = jnp.zeros((1, D), F32)
            for r in range(0, tm, NORM_ROWS):
                rs = slice(r, r + NORM_ROWS)
                df, dg = _rms_bwd(dx2_ref[rs, :], f_ref[rs, :], gq_ref[...])
                df_ref[rs, :] = df.astype(BF16)
                dg_sum = dg_sum + dg
            dgq_ref[...] += dg_sum
            dh_acc[...] = jnp.zeros_like(dh_acc)

        parts = [slice(p * tm // ROW_PARTS, (p + 1) * tm // ROW_PARTS) for p in range(ROW_PARTS)]
        for ps in parts:
            da_s[ps, :] = _dot_nt(df_ref[ps, :], wd_ref[...])
        for ps in parts:
            for r in range(ps.start, ps.stop, ACT_ROWS):
                rs = slice(r, r + ACT_ROWS)
                da = da_s[rs, :]
                gt = gt_ref[rs, :].astype(F32)
                up = up_ref[rs, :].astype(F32)
                sig = _sigmoid(gt)
                silu = gt * sig
                dgt_ref[rs, :] = (da * up * (sig * (1.0 + gt * (1.0 - sig)))).astype(BF16)
                dup_ref[rs, :] = (da * silu).astype(BF16)
                a_ref[rs, :] = (silu * up).astype(BF16)
            dh_acc[ps, :] += _dot_nt(dgt_ref[ps, :], wg_ref[...]) + _dot_nt(dup_ref[ps, :], wu_ref[...])

        @pl.when(j == nf - 1)
        def _():
            dg_sum = jnp.zeros((1, D), F32)
            for r in range(0, tm, NORM_ROWS):
                rs = slice(r, r + NORM_ROWS)
                dx, dg = _rms_bwd(dh_acc[rs, :], x1_ref[rs, :], gp_ref[...])
                dx1_ref[rs, :] = dx2_ref[rs, :] + dx
                dg_sum = dg_sum + dg
            dgp_ref[...] += dg_sum

        _ride_finish(riding, (i == T // tm - 1) & (j == nf - 1))

    ride_in, ride_out, ride_shapes, ride_scratch, ride_args, alias = _ride_specs(ride, 10, 7)
    out = pl.pallas_call(
        body, name="ffn_bwd", grid=(T // tm, nf),
        in_specs=[row, row, row, col, col, _full((1, D)), w_in, w_in, w_out, _full((1, D))] + ride_in,
        out_specs=[row, row, col, col, col, _full((1, D)), _full((1, D))] + ride_out,
        out_shape=[jax.ShapeDtypeStruct((T, D), F32), jax.ShapeDtypeStruct((T, D), BF16)]
        + [jax.ShapeDtypeStruct((nf, T, Fc), BF16)] * 3
        + [jax.ShapeDtypeStruct((1, D), F32), jax.ShapeDtypeStruct((1, D), F32)] + ride_shapes,
        input_output_aliases=alias,
        scratch_shapes=[pltpu.VMEM((tm, D), F32), pltpu.VMEM((tm, Fc), F32)] + ride_scratch,
        compiler_params=_params(2),
    )(dx2, x1, f, gt, up, g_pre, wg, wu, wdn, g_post, *ride_args)
    return tuple(out[:7]) + (list(out[7:]),)


def _wgrad_out(l, n_layers, rows, cols):
    spec = pl.BlockSpec((N_CHIPS, None, rows, cols), lambda *_: (0, l, 0, 0))
    return spec, jax.ShapeDtypeStruct((N_CHIPS, n_layers, rows, cols), F32)


def _merge_bwd(dx1, o, merged, P, A, Bm, C, Dv, wa, wb, wc, wd, wout, g_post, l, bufs, ride=None):
    T, D = o.shape
    tm = _tile(T, 256)
    Dc = D // N_CHIPS
    names = ("w_out", "w_br_a", "w_br_b", "w_br_c", "w_br_d")
    keep = [] if bufs is None else [bufs[n] for n in names]

    def body(dx1_ref, o_ref, mg_ref, lg_ref, a_ref, b_ref, c_ref, d_ref, wa_ref, wb_ref, wc_ref, wd_ref, wo_ref, g_ref, *rest):
        outs, _, riding = _ride_split(ride, rest[len(keep):], 11, 0)
        dlg_ref, da_ref, db_ref, dc_ref, dd_ref, dg_ref, go_ref, ga_ref, gb_ref, gc_ref, gd_ref = outs
        _ride_start(riding, pl.program_id(0) == 0)

        @pl.when(pl.program_id(0) == 0)
        def _():
            for r in (dg_ref, go_ref, ga_ref, gb_ref, gc_ref, gd_ref):
                r[...] = jnp.zeros_like(r)

        d_o, dg = _rms_bwd(dx1_ref[...], o_ref[...], g_ref[...])
        dg_ref[...] += dg
        d_o = d_o.astype(BF16)
        for k in range(N_CHIPS):
            go_ref[k] += _dot_tn(mg_ref[:, k * Dc:(k + 1) * Dc], d_o)
        dm = _dot_nt(d_o, wo_ref[...])
        branches = ((a_ref, wa_ref, da_ref, ga_ref), (b_ref, wb_ref, db_ref, gb_ref),
                    (c_ref, wc_ref, dc_ref, gc_ref), (d_ref, wd_ref, dd_ref, gd_ref))
        for j, (br, w, dbr_ref, gw_ref) in enumerate(branches):
            gate = _sigmoid(lg_ref[:, j * D:(j + 1) * D].astype(F32))
            y = _dot(br[...], w[...])
            dlg_ref[:, j * D:(j + 1) * D] = (dm * y * gate * (1.0 - gate)).astype(BF16)
            dy = (dm * gate).astype(BF16)
            dbr_ref[...] = _dot_nt(dy, w[...]).astype(dbr_ref.dtype)
            for k in range(N_CHIPS):
                gw_ref[k] += _dot_tn(br[...], dy[:, k * Dc:(k + 1) * Dc])

        _ride_finish(riding, pl.program_id(0) == T // tm - 1)

    row = lambda w: pl.BlockSpec((tm, w), lambda i: (i, 0))
    wg = [_wgrad_out(l, DEPTH, r, c) for r, c in ((Dc, D), (MLA_W, Dc), (MIX_W, Dc), (MIX_W, Dc), (MIX_W, Dc))]
    n_in = 14
    ride_in, ride_out, ride_shapes, ride_scratch, ride_args, alias = _ride_specs(ride, n_in + len(keep), 11)
    out = pl.pallas_call(
        body, name="merge_bwd", grid=(T // tm,),
        in_specs=[row(D), row(D), row(D), row(4 * D), row(MLA_W), row(MIX_W), row(MIX_W), row(MIX_W),
                  _full((MLA_W, D)), _full((MIX_W, D)), _full((MIX_W, D)), _full((MIX_W, D)), _full((D, D)), _full((1, D))]
        + [pl.BlockSpec(memory_space=pl.ANY)] * len(keep) + ride_in,
        out_specs=[row(4 * D), row(MLA_W), row(MIX_W), row(MIX_W), row(MIX_W), _full((1, D))] + [s for s, _ in wg] + ride_out,
        out_shape=[jax.ShapeDtypeStruct((T, 4 * D), BF16), jax.ShapeDtypeStruct((T, MLA_W), BF16)]
        + [jax.ShapeDtypeStruct((T, MIX_W), F32)] * 3 + [jax.ShapeDtypeStruct((1, D), F32)] + [s for _, s in wg] + ride_shapes,
        input_output_aliases={**{n_in + i: 6 + i for i in range(len(keep))}, **alias},
        scratch_shapes=ride_scratch,
        compiler_params=_params(1),
    )(dx1, o, merged, P, A, Bm, C, Dv, wa, wb, wc, wd, wout, g_post, *keep, *ride_args)
    return out[:6], dict(zip(names, out[6:11])), list(out[11:])


def _mixers_bwd(P, D, S, dBm, dC, dDv, pool_bd, pool_scale, g_v, sgu_w, sgu_bias, conv_w):
    T = P.shape[0]
    tc = _tile(S, 512, SGU_BLOCK)
    n_si = S // tc
    cur, prev, nxt = _mix_specs(T, D, tc)
    n_blk = tc // SGU_BLOCK

    def body(z_ref, zp_ref, u_ref, v_ref, b_ref, c_ref, x_ref, cp_ref, xp_ref, bn_ref,
             dbm_ref, dbmn_ref, dc_ref, ddv_ref, ddvn_ref,
             pw_ref, ps_ref, gv_ref, sw_ref, sb_ref, cw_ref,
             dp_ref, dpw_ref, dps_ref, dgv_ref, dsw_ref, dsb_ref, dcw_ref, dvn_acc):
        si = pl.program_id(0) % n_si
        first, last = si == 0, si == n_si - 1

        @pl.when(pl.program_id(0) == 0)
        def _():
            for r in (dpw_ref, dps_ref, dgv_ref, dsw_ref, dsb_ref, dcw_ref):
                r[...] = jnp.zeros_like(r)

        z = z_ref[...].astype(F32)
        pooled = _pool_fwd(z, jnp.where(first, 0.0, zp_ref[...].astype(F32)), si, tc).astype(BF16)
        dbm = dbm_ref[...]
        dps_ref[...] += jnp.sum(dbm * _dot(pooled, pw_ref[...]), axis=0, keepdims=True)
        dmix = (jnp.concatenate([dbm, jnp.where(last, 0.0, dbmn_ref[...])], axis=0) * ps_ref[...]).astype(BF16)
        dpw_ref[...] += _dot_tn(pooled, dmix[:tc])
        dpool = _dot_nt(dmix, pw_ref[...])
        e = dpool / _pool_count(si, tc, tc + POOL_HALO)
        f1 = e + _shift_up(e, 1)
        f2 = f1 + _shift_up(f1, 2)
        f4 = f2 + _shift_up(f2, 4)
        f8 = f4 + _shift_up(f4, 8)
        dp_ref[:, 0:MIX_W] = (_by_group(f1, f2, f4, f8)[:tc] - dpool[:tc]).astype(BF16)

        v = v_ref[...].astype(F32)
        vn = (v * _rms_r(v) * gv_ref[...]).astype(BF16)
        keep, wm = _sgu_weights(sw_ref)
        g = _lane_group()
        for blk in range(n_blk):
            rows = slice(blk * SGU_BLOCK, (blk + 1) * SGU_BLOCK)
            vb = vn[rows]
            dc = dc_ref[rows, :]
            dp_ref[rows, MIX_W:2 * MIX_W] = (dc * _sgu_mix(vb, wm, sb_ref[...])).astype(BF16)
            dmx = dc * u_ref[rows, :].astype(F32)
            dsb_ref[...] += dmx
            dvn = jnp.zeros((SGU_BLOCK, MIX_W), F32)
            for k in range(GROUPS):
                dmk = jnp.where(g == k, dmx, 0.0).astype(BF16)
                dsw_ref[k] += jnp.where(keep, _dot_nt(dmk, vb), 0.0)
                dvn = dvn + _dot_tn(wm[k], dmk)
            dvn_acc[rows, :] = dvn
        dv, dg = _rms_bwd(dvn_acc[...], v, gv_ref[...])
        dgv_ref[...] += dg
        dp_ref[:, 2 * MIX_W:3 * MIX_W] = dv.astype(BF16)

        cg, xg, bg = c_ref[...].astype(F32), x_ref[...].astype(F32), b_ref[...].astype(F32)
        zc = cg * xg
        ze = jnp.concatenate([jnp.where(first, 0.0, cp_ref[...].astype(F32) * xp_ref[...].astype(F32)), zc], axis=0)
        z1, z2 = _shift_down(ze, 1)[CONV_HALO:], _shift_down(ze, 2)[CONV_HALO:]
        ddv = ddv_ref[...]
        y = cw_ref[0:1, :] * z2 + cw_ref[1:2, :] * z1 + cw_ref[2:3, :] * zc
        dp_ref[:, 3 * MIX_W:4 * MIX_W] = (ddv * y).astype(BF16)
        dy = ddv * bg
        dcw_ref[0:1, :] += jnp.sum(dy * z2, axis=0, keepdims=True)
        dcw_ref[1:2, :] += jnp.sum(dy * z1, axis=0, keepdims=True)
        dcw_ref[2:3, :] += jnp.sum(dy * zc, axis=0, keepdims=True)
        dye = jnp.concatenate([dy, jnp.where(last, 0.0, ddvn_ref[...] * bn_ref[...].astype(F32))], axis=0)
        dz = (cw_ref[2:3, :] * dye + cw_ref[1:2, :] * _shift_up(dye, 1) + cw_ref[0:1, :] * _shift_up(dye, 2))[:tc]
        dp_ref[:, 4 * MIX_W:5 * MIX_W] = (dz * xg).astype(BF16)
        dp_ref[:, 5 * MIX_W:6 * MIX_W] = (dz * cg).astype(BF16)

    grad = lambda halo: pl.BlockSpec((halo, MIX_W), lambda i: (jnp.minimum((i + 1) * (tc // halo), T // halo - 1), 0))
    out = pl.BlockSpec((tc, MIX_W), lambda i: (i, 0))
    return pl.pallas_call(
        body, name="mixers_bwd", grid=(T // tc,),
        in_specs=[cur(0), prev(0, POOL_HALO), cur(1), cur(2), cur(3), cur(4), cur(5), prev(4, CONV_HALO), prev(5, CONV_HALO),
                  nxt(3, CONV_HALO), out, grad(POOL_HALO), out, out, grad(CONV_HALO),
                  _full((MIX_W, MIX_W)), _full((1, MIX_W)), _full((1, MIX_W)), _full((GROUPS, SGU_BLOCK, SGU_BLOCK)),
                  _full((SGU_BLOCK, MIX_W)), _full((CONV_HALO, MIX_W))],
        out_specs=[pl.BlockSpec((tc, SEC_MIX), lambda i: (i, 0)), _full((MIX_W, MIX_W)), _full((1, MIX_W)), _full((1, MIX_W)),
                   _full((GROUPS, SGU_BLOCK, SGU_BLOCK)), _full((SGU_BLOCK, MIX_W)), _full((CONV_HALO, MIX_W))],
        out_shape=[jax.ShapeDtypeStruct((T, SEC_MIX), BF16), jax.ShapeDtypeStruct((MIX_W, MIX_W), F32),
                   jax.ShapeDtypeStruct((1, MIX_W), F32), jax.ShapeDtypeStruct((1, MIX_W), F32),
                   jax.ShapeDtypeStruct((GROUPS, SGU_BLOCK, SGU_BLOCK), F32), jax.ShapeDtypeStruct((SGU_BLOCK, MIX_W), F32),
                   jax.ShapeDtypeStruct((CONV_HALO, MIX_W), F32)],
        scratch_shapes=[pltpu.VMEM((tc, MIX_W), F32)],
        compiler_params=_params(1),
    )(P, P, P, P, P, P, P, P, P, P, dBm, dBm, dC, dDv, dDv, pool_bd, pool_scale, g_v, sgu_w, sgu_bias, conv_w)


def _attn_tile(S):
    return _tile(S, 256, 128)


def _attn_stats(O, dO, lse, S):
    T = O.shape[0]
    tq = _attn_tile(S)

    def body(o_ref, do_ref, lse_ref, lt_ref, dt_ref):
        lane = lax.broadcasted_iota(jnp.int32, (tq, HEAD_PAD), 1)
        delta = jnp.zeros((tq, HEAD_PAD), F32)
        for h in range(HEADS):
            hs = slice(h * HEAD_PAD, (h + 1) * HEAD_PAD)
            s = jnp.sum(o_ref[:, hs].astype(F32) * do_ref[:, hs].astype(F32), axis=-1, keepdims=True)
            delta = jnp.where(lane == h, s, delta)
        delta_t, lse_t = delta.T, lse_ref[...].T
        for h in range(HEADS):
            lt_ref[h, 0] = lse_t[h:h + 1, :]
            dt_ref[h, 0] = delta_t[h:h + 1, :]

    row = lambda w: pl.BlockSpec((tq, w), lambda i: (i, 0))
    out = pl.BlockSpec((HEADS, 1, 1, tq), lambda i: (0, i, 0, 0))
    return pl.pallas_call(
        body, name="attn_stats", grid=(T // tq,), in_specs=[row(MLA_W), row(MLA_W), row(HEAD_PAD)], out_specs=[out, out],
        out_shape=[jax.ShapeDtypeStruct((HEADS, T // tq, 1, tq), F32)] * 2, compiler_params=_params(1),
    )(O, dO, lse)


def _flash_bwd(Q, K, V, dO, lse_t, delta_t, S, ride=None):
    T = Q.shape[0]
    n_seq = T // S
    tq = _attn_tile(S)
    nq = S // tq

    def body(k_ref, v_ref, q_ref, do_ref, lse_ref, dl_ref, *rest):
        (dq_ref, dk_ref, dv_ref), (s_s, dp_s, p_s, ds_s), riding = _ride_split(ride, rest, 3, 4)
        kb = pl.program_id(1)
        _ride_start(riding, (pl.program_id(0) == 0) & (kb == 0))

        @pl.when(kb == 0)
        def _():
            dq_ref[...] = jnp.zeros_like(dq_ref)

        dk_ref[...] = jnp.zeros_like(dk_ref)
        dv_ref[...] = jnp.zeros_like(dv_ref)

        def block(qi, masked):
            rows = pl.ds(pl.multiple_of(qi * tq, tq), tq)
            for h in range(HEADS):
                hs = slice(h * HEAD_PAD, (h + 1) * HEAD_PAD)
                s_s[h] = _dot_nt(k_ref[:, hs], q_ref[rows, hs])
                dp_s[h] = _dot_nt(v_ref[:, hs], do_ref[rows, hs])
            for h in range(HEADS):
                lse_row, dl_row = lse_ref[h, qi], dl_ref[h, qi]
                for r in range(0, tq, SOFTMAX_ROWS):
                    rs = slice(r, r + SOFTMAX_ROWS)
                    s = s_s[h, rs, :]
                    if masked:
                        key = r + lax.broadcasted_iota(jnp.int32, (SOFTMAX_ROWS, tq), 0)
                        query = lax.broadcasted_iota(jnp.int32, (SOFTMAX_ROWS, tq), 1)
                        s = jnp.where((query >> CHUNK_SHIFT) >= (key >> CHUNK_SHIFT), s, NEG_INF)
                    p = jnp.exp2(s - lse_row)
                    p_s[h, rs, :] = p.astype(BF16)
                    ds_s[h, rs, :] = (p * (dp_s[h, rs, :] - dl_row)).astype(BF16)
            for h in range(HEADS):
                hs = slice(h * HEAD_PAD, (h + 1) * HEAD_PAD)
                dv_ref[:, hs] += _dot(p_s[h], do_ref[rows, hs])
                dk_ref[:, hs] += _dot(ds_s[h], q_ref[rows, hs])
                dq_ref[rows, hs] += _dot_tn(ds_s[h], k_ref[:, hs])

        def full_block(qi, carry):
            block(qi, False)
            return carry

        block(kb, True)
        lax.fori_loop(kb + 1, nq, full_block, 0)
        dk_ref[...] = dk_ref[...] * LN2
        _ride_finish(riding, (pl.program_id(0) == n_seq - 1) & (kb == nq - 1))

    tile = pl.BlockSpec((tq, MLA_W), lambda b, i: (b * nq + i, 0))
    seq = pl.BlockSpec((S, MLA_W), lambda b, i: (b, 0))
    stat = pl.BlockSpec((HEADS, nq, 1, tq), lambda b, i: (0, b, 0, 0))
    ride_in, ride_out, ride_shapes, ride_scratch, ride_args, alias = _ride_specs(ride, 6, 3)
    out = pl.pallas_call(
        body, name="flash_bwd", grid=(n_seq, nq),
        in_specs=[tile, tile, seq, seq, stat, stat] + ride_in,
        out_specs=[seq, tile, tile] + ride_out,
        out_shape=[jax.ShapeDtypeStruct((T, MLA_W), F32)] * 3 + ride_shapes,
        input_output_aliases=alias,
        scratch_shapes=[pltpu.VMEM((HEADS, tq, tq), F32), pltpu.VMEM((HEADS, tq, tq), F32),
                        pltpu.VMEM((HEADS, tq, tq), BF16), pltpu.VMEM((HEADS, tq, tq), BF16)] + ride_scratch,
        compiler_params=_params(2),
    )(K, V, Q, dO, lse_t, delta_t, *ride_args)
    return out[0], out[1], out[2], list(out[3:])


def _mla_bwd_post(P, D, S, dQ, dK, dV, hq, hkv, g_cq, g_ckv, wuq, wukv, rope_c, rope_sa, rope_sb, l, bufs):
    T = P.shape[0]
    tm = _tile(S, 512)
    n_si = S // tm
    base = 4 * D
    names = ("w_uq", "w_ukv")

    def body(cq_ref, ckv_ref, dq_ref, dk_ref, dv_ref, hq_ref, hkv_ref, gq_ref, gkv_ref, wq_ref, wkv_ref, c_ref, sa_ref, sb_ref,
             *rest):
        dp_ref, dgq_ref, dgkv_ref, guq_ref, gukv_ref = rest[-5:]

        @pl.when(pl.program_id(0) == 0)
        def _():
            for r in (dgq_ref, dgkv_ref, guq_ref, gukv_ref):
                r[...] = jnp.zeros_like(r)

        c, sa, sb = c_ref[...], sa_ref[...], sb_ref[...]
        dq = _rope_t(dq_ref[...] * ATTN_SCALE, jnp.tile(c, (1, HEADS)), jnp.tile(sa, (1, HEADS)),
                     jnp.tile(sb, (1, HEADS))).astype(BF16)
        dcq, dg = _rms_bwd(_dot_nt(dq, wq_ref[...]), cq_ref[...].astype(F32), gq_ref[...])
        dgq_ref[...] += dg
        dp_ref[:, 0:Q_LORA] = dcq.astype(BF16)

        dk = dk_ref[...]
        dkb, dvb = dk.astype(BF16), dv_ref[...].astype(BF16)
        dkv = jnp.concatenate([p[:, j * CHIP_HEADS_W:(j + 1) * CHIP_HEADS_W] for j in range(N_CHIPS) for p in (dkb, dvb)], axis=1)
        for k in range(N_CHIPS):
            guq_ref[k] += _dot_tn(hq_ref[...], dq[:, k * CHIP_HEADS_W:(k + 1) * CHIP_HEADS_W])
            gukv_ref[k] += _dot_tn(hkv_ref[...], dkv[:, k * CHIP_KV:(k + 1) * CHIP_KV])
        dckv, dg = _rms_bwd(_dot_nt(dkv, wkv_ref[...]), ckv_ref[...].astype(F32), gkv_ref[...])
        dgkv_ref[...] += dg
        dp_ref[:, Q_LORA:Q_LORA + KV_LORA] = dckv.astype(BF16)

        dkr = dk[:, 0:HEAD_PAD]
        for h in range(1, HEADS):
            dkr = dkr + dk[:, h * HEAD_PAD:(h + 1) * HEAD_PAD]
        lane = lax.broadcasted_iota(jnp.int32, (1, HEAD_PAD), 1)
        rope_lanes = (lane >= QK_NOPE) & (lane < QK_NOPE + QK_ROPE)
        dp_ref[:, Q_LORA + KV_LORA:SEC_MLA] = jnp.where(rope_lanes, _rope_t(dkr, c, sa, sb), 0.0).astype(BF16)

    tab = pl.BlockSpec((tm, HEAD_PAD), lambda i: (i % n_si, 0))
    row = lambda w: pl.BlockSpec((tm, w), lambda i: (i, 0))
    wg = [_wgrad_out(l, DEPTH, Q_LORA, CHIP_HEADS_W), _wgrad_out(l, DEPTH, KV_LORA, CHIP_KV)]
    keep = [] if bufs is None else [bufs[n] for n in names]
    n_in = 14
    out = pl.pallas_call(
        body, name="mla_bwd_post", grid=(T // tm,),
        in_specs=[pl.BlockSpec((tm, Q_LORA), lambda i: (i, base // Q_LORA)),
                  pl.BlockSpec((tm, KV_LORA), lambda i: (i, (base + Q_LORA) // KV_LORA)),
                  row(MLA_W), row(MLA_W), row(MLA_W), row(Q_LORA), row(KV_LORA),
                  _full((1, Q_LORA)), _full((1, KV_LORA)), _full((Q_LORA, MLA_W)), _full((KV_LORA, 2 * MLA_W)), tab, tab, tab]
        + [pl.BlockSpec(memory_space=pl.ANY)] * len(keep),
        out_specs=[row(SEC_MLA), _full((1, Q_LORA)), _full((1, KV_LORA))] + [s for s, _ in wg],
        out_shape=[jax.ShapeDtypeStruct((T, SEC_MLA), BF16), jax.ShapeDtypeStruct((1, Q_LORA), F32),
                   jax.ShapeDtypeStruct((1, KV_LORA), F32)] + [s for _, s in wg],
        input_output_aliases={n_in + i: 3 + i for i in range(len(keep))},
        compiler_params=_params(1),
    )(P, P, dQ, dK, dV, hq, hkv, g_cq, g_ckv, wuq, wukv, rope_c, rope_sa, rope_sb, *keep)
    return out[:3], dict(zip(names, out[3:]))


def _proj_bwd(dx1, x, g, dPg, dPa, dPm, w_gates, w_mla, w_mix, ride=None):
    T, D = x.shape
    tm = _tile(T, 256)

    def body(dx1_ref, x_ref, g_ref, dg_ref_in, da_ref, dm_ref, wg_ref, wa_ref, wm_ref, *rest):
        (dx_ref, dg_ref), _, riding = _ride_split(ride, rest, 2, 0)
        _ride_start(riding, pl.program_id(0) == 0)

        @pl.when(pl.program_id(0) == 0)
        def _():
            dg_ref[...] = jnp.zeros_like(dg_ref)

        dh = _dot_nt(dg_ref_in[...], wg_ref[...]) + _dot_nt(da_ref[...], wa_ref[...]) + _dot_nt(dm_ref[...], wm_ref[...])
        dx, dg = _rms_bwd(dh, x_ref[...], g_ref[...])
        dx_ref[...] = dx1_ref[...] + dx
        dg_ref[...] += dg
        _ride_finish(riding, pl.program_id(0) == T // tm - 1)

    row = lambda w: pl.BlockSpec((tm, w), lambda i: (i, 0))
    ride_in, ride_out, ride_shapes, ride_scratch, ride_args, alias = _ride_specs(ride, 9, 2)
    out = pl.pallas_call(
        body, name="proj_bwd", grid=(T // tm,),
        in_specs=[row(D), row(D), _full((1, D)), row(4 * D), row(SEC_MLA), row(SEC_MIX),
                  _full((D, 4 * D)), _full((D, SEC_MLA)), _full((D, SEC_MIX))] + ride_in,
        out_specs=[row(D), _full((1, D))] + ride_out,
        out_shape=[jax.ShapeDtypeStruct((T, D), F32), jax.ShapeDtypeStruct((1, D), F32)] + ride_shapes,
        input_output_aliases=alias, scratch_shapes=ride_scratch,
        compiler_params=_params(1),
    )(dx1, x, g, dPg, dPa, dPm, w_gates, w_mla, w_mix, *ride_args)
    return out[0], out[1], list(out[2:])


def _adamw(w, g, m, v, name):
    R, C = w.shape
    tr = _tile(R, max(8, (1 << 19) // C))

    def body(w_ref, g_ref, m_ref, v_ref, d_ref, mo_ref, vo_ref):
        gv = g_ref[...]
        mn = ADAM_B1 * m_ref[...] + (1.0 - ADAM_B1) * gv
        vn = ADAM_B2 * v_ref[...] + (1.0 - ADAM_B2) * (gv * gv)
        mo_ref[...] = mn
        vo_ref[...] = vn
        m_hat = mn / (1.0 - ADAM_B1 ** ADAM_STEP)
        v_hat = vn / (1.0 - ADAM_B2 ** ADAM_STEP)
        d_ref[...] = -ADAM_LR * (m_hat / (jnp.sqrt(v_hat) + ADAM_EPS) + ADAM_WD * w_ref[...])

    blk = pl.BlockSpec((tr, C), lambda i: (i, 0))
    return pl.pallas_call(
        body, name=name, grid=(R // tr,), in_specs=[blk] * 4, out_specs=[blk] * 3,
        out_shape=[jax.ShapeDtypeStruct((R, C), F32)] * 3, compiler_params=_params(1),
    )(w, g, m, v)


def _rows_tile(rows, cols):
    return _tile(rows, max(16, (1 << 19) // cols), 16)


def _add_halves(G, span, recv, half, name):
    n, L, R, C = G.shape
    l0, nl = span[0], span[1] - span[0]
    hr = R // 2
    tr = _rows_tile(hr, C)
    nb = hr // tr

    def body(half_ref, g_ref, r_ref, o_ref):
        o_ref[...] = (g_ref[...] + r_ref[...]).astype(BF16)

    grid_spec = pltpu.PrefetchScalarGridSpec(
        num_scalar_prefetch=1, grid=(n, nl, nb),
        in_specs=[pl.BlockSpec((None, None, tr, C), lambda k, l, i, h: (k, l0 + l, h[0] * nb + i, 0)),
                  pl.BlockSpec((None, None, tr, C), lambda k, l, i, h: (k, l, i, 0))],
        out_specs=pl.BlockSpec((None, None, tr, C), lambda k, l, i, h: (k, l, i, 0)))
    return pl.pallas_call(
        body, name="rs_add_halves_" + name, grid_spec=grid_spec,
        out_shape=jax.ShapeDtypeStruct((n, nl, hr, C), BF16), compiler_params=_params(3),
    )(half.reshape(1).astype(jnp.int32), G, recv)


def _sum_slots(H, slots, place, name, l0, n_layers, prev=None):
    _, nl, hr, C = slots.shape
    tr = _rows_tile(hr, C)
    nb = hr // tr

    def body(x_ref, y_ref, c_ref, own_ref, s1_ref, s2_ref, s3_ref, *rest):
        o_ref = rest[-1]
        o_ref[...] = ((own_ref[...].astype(F32) + s1_ref[...].astype(F32)) + s2_ref[...].astype(F32)) + s3_ref[...].astype(F32)

    def src(fx, fy):
        def index(l, j, px, py, pc):
            cx = px[0] + fx - 2 * fx * px[0]
            cy = py[0] + fy - 2 * fy * py[0]
            return (2 * cx + cy, l, j, 0)
        return pl.BlockSpec((None, None, tr, C), index)

    keep = [] if prev is None else [prev]
    grid_spec = pltpu.PrefetchScalarGridSpec(
        num_scalar_prefetch=3, grid=(nl, nb),
        in_specs=[src(0, 0), src(0, 1), src(1, 0), src(1, 1)] + [pl.BlockSpec(memory_space=pl.ANY)] * len(keep),
        out_specs=pl.BlockSpec((None, tr, C), lambda l, j, px, py, pc: (l0 + l, pc[0] * nb + j, 0)))
    return pl.pallas_call(
        body, name="rs_sum_slots_" + name, grid_spec=grid_spec,
        out_shape=jax.ShapeDtypeStruct((n_layers, 2 * hr, C), F32),
        input_output_aliases={7: 0} if keep else {}, compiler_params=_params(2),
    )(*place, H, slots, slots, slots, *keep)


HBM = pl.BlockSpec(memory_space=pltpu.HBM)


def _place():
    x, y, c = lax.axis_index("x"), lax.axis_index("y"), lax.axis_index("c")
    return x, y, c, 2 * x + y


def _chip_device(chip, c):
    return (chip // 2, chip % 2, c)


def _remote(src, dst, send_sem, recv_sem, to):
    return pltpu.make_async_remote_copy(src_ref=src, dst_ref=dst, send_sem=send_sem, recv_sem=recv_sem, device_id=to,
                                        device_id_type=MESH)


def _place_own(w, chip, name):
    L, R, C = w.shape
    tr = _rows_tile(R, C)

    def body(p_ref, w_ref, o_ref):
        o_ref[...] = w_ref[...].astype(BF16)

    grid_spec = pltpu.PrefetchScalarGridSpec(
        num_scalar_prefetch=1, grid=(L, R // tr), in_specs=[pl.BlockSpec((None, tr, C), lambda l, j, p: (l, j, 0))],
        out_specs=pl.BlockSpec((None, None, tr, C), lambda l, j, p: (l, p[0], j, 0)))
    return pl.pallas_call(
        body, name="place_" + name, grid_spec=grid_spec,
        out_shape=jax.ShapeDtypeStruct((L, N_CHIPS, R, C), BF16), compiler_params=_params(2),
    )(chip.reshape(1).astype(jnp.int32), w)


def _gather_weights(bufs, l):
    n = len(bufs)

    def body(*refs):
        o_refs = refs[n:2 * n]
        send_sems, recv_sems = refs[2 * n:]
        x, y, c, me = _place()
        sibling = (x, y, 1 - c)

        def copy(t, k, chip, half, to):
            hr = o_refs[t].shape[2] // 2
            block = o_refs[t].at[l, chip, pl.ds(half * hr, hr), :]
            return _remote(block, block, send_sems.at[6 * t + k], recv_sems.at[6 * t + k], to)

        first = [copy(t, d - 1, me, c, _chip_device(me ^ d, c)) for t in range(n) for d in (1, 2, 3)]
        for cp in first:
            cp.start()
        passed = []
        for t in range(n):
            for d in (1, 2, 3):
                copy(t, d - 1, me ^ d, c, sibling).wait_recv()
                passed.append(copy(t, 2 + d, me ^ d, c, sibling))
                passed[-1].start()
        for t in range(n):
            for d in (1, 2, 3):
                copy(t, 2 + d, me ^ d, 1 - c, sibling).wait_recv()
        for cp in first + passed:
            cp.wait_send()

    return pl.pallas_call(
        body, name="gather_weights", in_specs=[HBM] * n, out_specs=[HBM] * n,
        out_shape=[jax.ShapeDtypeStruct(b.shape, b.dtype) for b in bufs],
        input_output_aliases={t: t for t in range(n)},
        scratch_shapes=[pltpu.SemaphoreType.DMA((6 * n,)), pltpu.SemaphoreType.DMA((6 * n,))],
    )(*bufs)


def _join_halves(bufs):
    n = len(bufs)

    def body(*refs):
        o_refs = refs[n:2 * n]
        send_sems, recv_sems = refs[2 * n:]
        x, y, c, _ = _place()

        def half(t, which):
            hr = o_refs[t].shape[1] // 2
            return o_refs[t].at[:, pl.ds(which * hr, hr), :]

        sends = [_remote(half(t, c), half(t, c), send_sems.at[t], recv_sems.at[t], (x, y, 1 - c)) for t in range(n)]
        for cp in sends:
            cp.start()
        for t in range(n):
            _remote(half(t, 1 - c), half(t, 1 - c), send_sems.at[t], recv_sems.at[t], (x, y, 1 - c)).wait_recv()
        for cp in sends:
            cp.wait_send()

    return pl.pallas_call(
        body, name="rs_join_halves", in_specs=[HBM] * n, out_specs=[HBM] * n,
        out_shape=[jax.ShapeDtypeStruct(b.shape, b.dtype) for b in bufs],
        input_output_aliases={t: t for t in range(n)},
        scratch_shapes=[pltpu.SemaphoreType.DMA((n,)), pltpu.SemaphoreType.DMA((n,))],
    )(*bufs)


def _all_reduce_small(v, name):
    R, C = v.shape

    def body(v_ref, o_ref, slots, send_sems, recv_sems):
        x, y, c, _ = _place()
        me = 4 * x + 2 * y + c
        slots[me] = v_ref[...]
        sends = []
        for d in range(1, 8):
            peer = me ^ d
            sends.append(pltpu.make_async_remote_copy(
                src_ref=v_ref, dst_ref=slots.at[me], send_sem=send_sems.at[d - 1], recv_sem=recv_sems.at[d - 1],
                device_id=(peer // 4, (peer // 2) % 2, peer % 2), device_id_type=MESH))
        for cp in sends:
            cp.start()
        for d in range(1, 8):
            peer = me ^ d
            pltpu.make_async_remote_copy(
                src_ref=v_ref, dst_ref=slots.at[peer], send_sem=send_sems.at[d - 1], recv_sem=recv_sems.at[d - 1],
                device_id=(peer // 4, (peer // 2) % 2, peer % 2), device_id_type=MESH).wait_recv()
        for cp in sends:
            cp.wait_send()
        acc = slots[0]
        for k in range(1, 8):
            acc = acc + slots[k]
        o_ref[...] = acc

    vm = pl.BlockSpec(memory_space=pltpu.VMEM)
    return pl.pallas_call(
        body, name=name, in_specs=[vm], out_specs=vm, out_shape=jax.ShapeDtypeStruct((R, C), F32),
        scratch_shapes=[pltpu.VMEM((8, R, C), F32), pltpu.SemaphoreType.DMA((7,)), pltpu.SemaphoreType.DMA((7,))],
    )(v)


SHARDED = ("w_in", "w_uq", "w_ukv", "conv_w", "w_br_a", "w_br_b", "w_br_c", "w_br_d", "w_out", "w_ffn_gate", "w_ffn_up",
           "w_ffn_down")
ROW_SHARDED = ("w_out", "w_ffn_down")
REPLICATED = ("g_pre_mix", "g_cq", "g_ckv", "pool_w", "pool_scale", "g_sgu_v", "sgu_w", "sgu_b", "g_post_mix", "g_pre_ffn",
              "g_post_ffn")
WEIGHTS = ("w_in", "g_pre_mix", "g_cq", "g_ckv", "w_uq", "w_ukv", "pool_w", "pool_scale", "g_sgu_v", "sgu_w", "sgu_b",
           "conv_w", "w_br_a", "w_br_b", "w_br_c", "w_br_d", "w_out", "g_post_mix", "g_pre_ffn", "w_ffn_gate", "w_ffn_up",
           "w_ffn_down", "g_post_ffn")
GATHERED = tuple(n for n in SHARDED if n != "conv_w")


def _unpack(packed, shapes):
    flat = packed.reshape(-1)
    out, o = [], 0
    for s in shapes:
        n = int(np.prod(s))
        out.append(flat[o:o + n].reshape(s))
        o += n
    return out


def _join_cols(g, l):
    return jnp.concatenate([g[l, k] for k in range(N_CHIPS)], axis=1)


def _pad_heads(w, real):
    lead = w.shape[:-1]
    w = w.reshape(lead + (HEADS, real))
    return jnp.pad(w, [(0, 0)] * len(lead) + [(0, 0), (0, HEAD_PAD - real)]).reshape(lead + (MLA_W,))


def _unpad_heads(w, real):
    lead = w.shape[:-1]
    return w.reshape(lead + (HEADS, HEAD_PAD))[..., :real].reshape(lead + (HEADS * real,))


IN_OFFSETS = {"cq": 0, "ckv": Q_LORA, "kr": Q_LORA + KV_LORA, "mix": Q_LORA + KV_LORA + QK_ROPE}
IN_GATES = Q_LORA + KV_LORA + QK_ROPE + SEC_MIX


def _pad_w_in(w):
    K = w.shape[0]
    z = lambda n: jnp.zeros((K, n), w.dtype)
    return jnp.concatenate([w[:, IN_GATES:], w[:, :IN_OFFSETS["kr"]], z(QK_NOPE), w[:, IN_OFFSETS["kr"]:IN_OFFSETS["mix"]],
                            z(HEAD_PAD - QK_NOPE - QK_ROPE), w[:, IN_OFFSETS["mix"]:IN_GATES]], axis=1)


def _unpad_w_in(d_gates, d_mla, d_mix):
    kr = d_mla[:, Q_LORA + KV_LORA + QK_NOPE:Q_LORA + KV_LORA + QK_NOPE + QK_ROPE]
    return jnp.concatenate([d_mla[:, :Q_LORA + KV_LORA], kr, d_mix, d_gates], axis=1)


def _rope_tables(S):
    half = QK_ROPE // 2
    inv = ROPE_THETA ** (-jnp.arange(0, QK_ROPE, 2, dtype=F32) / QK_ROPE)
    ang = jnp.arange(S, dtype=F32)[:, None] * inv[None, :]
    cos, sin = jnp.cos(ang), jnp.sin(ang)
    one, zero = jnp.ones((S, QK_NOPE), F32), jnp.zeros((S, half), F32)
    tail = HEAD_PAD - QK_NOPE - QK_ROPE
    c = jnp.concatenate([one, cos, cos, jnp.ones((S, tail), F32)], axis=1)
    sa = jnp.concatenate([0 * one, zero, sin, jnp.zeros((S, tail), F32)], axis=1)
    sb = jnp.concatenate([0 * one, -sin, zero, jnp.zeros((S, tail), F32)], axis=1)
    return c, sa, sb


def _layer_weights(gathered, full, l, D):
    w = {}
    w_in = _pad_w_in(_join_cols(gathered["w_in"], l))
    w["w_in"] = w_in
    w["w_in_gates"], w["w_in_mla"], w["w_in_mix"] = w_in[:, :4 * D], w_in[:, 4 * D:4 * D + SEC_MLA], w_in[:, 4 * D + SEC_MLA:]
    w["w_uq"] = _pad_heads(_join_cols(gathered["w_uq"], l), QK_NOPE + QK_ROPE)
    ukv = _join_cols(gathered["w_ukv"], l).reshape(KV_LORA, HEADS, QK_NOPE + V_HEAD)
    pad = ((0, 0), (0, 0), (0, HEAD_PAD - QK_NOPE))
    k_pad = jnp.pad(ukv[:, :, :QK_NOPE], pad).reshape(KV_LORA, N_CHIPS, CHIP_HEADS_W)
    v_pad = jnp.pad(ukv[:, :, QK_NOPE:], pad).reshape(KV_LORA, N_CHIPS, CHIP_HEADS_W)
    w["w_ukv"] = jnp.concatenate([k_pad, v_pad], axis=2).reshape(KV_LORA, 2 * MLA_W)
    w["w_br_a"] = jnp.pad(_join_cols(gathered["w_br_a"], l).reshape(HEADS, V_HEAD, D),
                          ((0, 0), (0, HEAD_PAD - V_HEAD), (0, 0))).reshape(MLA_W, D)
    for n in ("w_br_b", "w_br_c", "w_br_d"):
        w[n] = _join_cols(gathered[n], l)
    w["w_out"] = gathered["w_out"][l].reshape(D, D)
    for n in ("g_pre_mix", "g_cq", "g_ckv", "pool_scale", "g_sgu_v", "g_post_mix", "g_pre_ffn", "g_post_ffn"):
        w[n] = full[n][l].reshape(1, -1)
    pw = full["pool_w"][l]
    w["pool_bd"] = jax.scipy.linalg.block_diag(*[pw[g] for g in range(GROUPS)]).astype(BF16)
    w["sgu_w"] = full["sgu_w"][l]
    w["sgu_bias"] = jnp.repeat(full["sgu_b"][l].T, GROUP_DIM, axis=1)
    w["conv_w"] = jnp.pad(full["conv_w"][l].reshape(3, MIX_W), ((0, CONV_HALO - 3), (0, 0)))
    return w


def _layer_fwd(x, w, gathered, l, S, rope, gather_next):
    D = x.shape[1]
    bufs = [gathered[n] for n in GATHERED]
    P, h = _norm_matmul(x, w["g_pre_mix"], w["w_in"], "proj_fwd")
    Q, K, V, hq, hkv = _mla_prep(P, D, w["g_cq"], w["g_ckv"], w["w_uq"], w["w_ukv"], *rope, S)
    A, lse, bufs = _flash_fwd(Q, K, V, S, _gather_ride(bufs, l + 1, "chips") if gather_next else None)
    Bm, C, Dv, bufs = _mixers_fwd(P, D, S, w["pool_bd"], w["pool_scale"], w["g_sgu_v"], w["sgu_w"], w["sgu_bias"], w["conv_w"],
                                  _gather_ride(bufs, l + 1, "cores") if gather_next else None)
    if gather_next:
        gathered = dict(zip(GATHERED, bufs))
    x1, merged, o = _merge_fwd(x, P, A, Bm, C, Dv, w["w_br_a"], w["w_br_b"], w["w_br_c"], w["w_br_d"], w["w_out"], w["g_post_mix"])
    x2, h2, gt, up, f = _ffn_fwd(x1, w["g_pre_ffn"], gathered["w_ffn_gate"], gathered["w_ffn_up"], gathered["w_ffn_down"],
                                 w["g_post_ffn"], l)
    saved = dict(x=x, P=P, h=h, Q=Q, K=K, V=V, hq=hq, hkv=hkv, A=A, lse=lse, Bm=Bm, C=C, Dv=Dv, x1=x1, merged=merged, o=o,
                 h2=h2, gt=gt, up=up, f=f)
    return x2, saved, gathered


RIDE_SETS = (("w_ffn_gate", "w_out", "w_br_a"), ("w_in", "w_br_b", "w_br_c", "w_br_d", "w_uq", "w_ukv"),
             ("w_ffn_up", "w_ffn_down"))


def _layer_bwd(dx2, w, gathered, l, s, S, rope, bufs, early=None):
    D = dx2.shape[1]
    Fc = gathered["w_ffn_gate"].shape[3]
    g = {}

    def wgrad(n, a, b, rows, cols, a_mode, b_mode):
        bufs[n] = _wgrad_chip(bufs.get(n), l, DEPTH, a, b, rows, cols, a_mode, b_mode, "wgrad_" + n)

    def scatter(k):
        return _scatter_ride([sums[n] for n in RIDE_SETS[k]]) if early else None

    sums, slots = {}, {}
    if early:
        done, spans, half = early
        ride = _exchange_ride([done[n] for n in GATHERED], [spans[n] for n in GATHERED])
    dx1, df, dgt, dup, act, g["g_pre_ffn"], g["g_post_ffn"], got = _ffn_bwd(
        dx2, s["x1"], s["f"], s["gt"], s["up"], w["g_pre_ffn"], gathered["w_ffn_gate"], gathered["w_ffn_up"],
        gathered["w_ffn_down"], w["g_post_ffn"], l, ride if early else None)
    if early:
        sums = {n: _add_halves(done[n], spans[n], r, half, n) for n, r in zip(GATHERED, got)}
        sums, (dx1, df, dgt, dup, act) = lax.optimization_barrier((sums, (dx1, df, dgt, dup, act)))
    wgrad("w_ffn_down", act, df, Fc, D, "slab", "all")
    wgrad("w_ffn_gate", s["h2"], dgt, D, Fc, "all", "slab")
    wgrad("w_ffn_up", s["h2"], dup, D, Fc, "all", "slab")

    (dPg, dA, dBm, dC, dDv, g["g_post_mix"]), filled, got = _merge_bwd(
        dx1, s["o"], s["merged"], s["P"], s["A"], s["Bm"], s["C"], s["Dv"], w["w_br_a"], w["w_br_b"], w["w_br_c"], w["w_br_d"],
        w["w_out"], w["g_post_mix"], l, bufs if "w_out" in bufs else None, scatter(0))
    bufs.update(filled)
    slots.update(zip(RIDE_SETS[0], got))

    dPm, d_pool_bd, g_ps, g_gv, g["sgu_w"], d_bias, d_cw = _mixers_bwd(
        s["P"], D, S, dBm, dC, dDv, w["pool_bd"], w["pool_scale"], w["g_sgu_v"], w["sgu_w"], w["sgu_bias"], w["conv_w"])
    g["pool_w"] = jnp.stack([d_pool_bd[k * GROUP_DIM:(k + 1) * GROUP_DIM, k * GROUP_DIM:(k + 1) * GROUP_DIM] for k in range(GROUPS)])
    g["pool_scale"], g["g_sgu_v"] = g_ps, g_gv
    g["sgu_b"] = d_bias.reshape(SGU_BLOCK, GROUPS, GROUP_DIM).sum(-1).T
    g["conv_w"] = d_cw[:3].reshape(3, 1, MIX_W)

    lse_t, delta_t = _attn_stats(s["A"], dA, s["lse"], S)
    dQ, dK, dV, got = _flash_bwd(s["Q"], s["K"], s["V"], dA, lse_t, delta_t, S, scatter(1))
    slots.update(zip(RIDE_SETS[1], got))
    (dPa, g["g_cq"], g["g_ckv"]), filled = _mla_bwd_post(
        s["P"], D, S, dQ, dK, dV, s["hq"], s["hkv"], w["g_cq"], w["g_ckv"], w["w_uq"], w["w_ukv"], *rope, l,
        bufs if "w_uq" in bufs else None)
    bufs.update(filled)

    dx, g["g_pre_mix"], got = _proj_bwd(dx1, s["x"], w["g_pre_mix"], dPg, dPa, dPm, w["w_in_gates"], w["w_in_mla"],
                                        w["w_in_mix"], scatter(2))
    slots.update(zip(RIDE_SETS[2], got))
    d_w_in = _unpad_w_in(_matmul_tn(s["h"], dPg, "wgrad_in_gates"), _matmul_tn(s["h"], dPa, "wgrad_in_mla"),
                         _matmul_tn(s["h"], dPm, "wgrad_in_mix"))
    g["w_in"] = d_w_in.reshape(D, N_CHIPS, -1).transpose(1, 0, 2)
    for n in ("g_pre_mix", "g_cq", "g_ckv", "pool_scale", "g_sgu_v", "g_post_mix", "g_pre_ffn", "g_post_ffn"):
        g[n] = g[n].reshape(-1)
    return dx, g, (sums, slots)


SMALL = REPLICATED + ("conv_w",)


def _local_step(x, target, gathered, full, core):
    n_seq, S, D = x.shape
    rope = _rope_tables(S)
    xs = x.reshape(n_seq * S, D)
    weights, saved = [], []
    for l in range(DEPTH):
        w = _layer_weights(gathered, full, l, D)
        gather_next = l + 1 < DEPTH
        if gather_next:
            w, gathered = lax.optimization_barrier((w, gathered))
        xs, s, gathered = _layer_fwd(xs, w, gathered, l, S, rope, gather_next)
        weights.append(w)
        saved.append(s)
    loss_parts, dx = _loss_grad(xs, target.reshape(n_seq * S, D))
    grads, bufs = [None] * DEPTH, {}
    for l in reversed(range(1, DEPTH)):
        dx, grads[l], _ = _layer_bwd(dx, weights[l], gathered, l, saved[l], S, rope, bufs)
    done = dict(bufs, w_in=jnp.stack([grads[l]["w_in"] for l in range(1, DEPTH)], axis=1))
    spans = dict({n: (1, DEPTH) for n in GATHERED}, w_in=(0, DEPTH - 1))
    dx, grads[0], early = _layer_bwd(dx, weights[0], gathered, 0, saved[0], S, rope, bufs, (done, spans, core))
    last = dict(bufs, w_in=grads[0]["w_in"][:, None])
    small = {n: jnp.stack([grads[l][n] for l in range(DEPTH)]) for n in SMALL}
    return loss_parts, dx.reshape(n_seq, S, D), early, last, small


def _unpad_reduced(n, r):
    L = r.shape[0]
    if n == "w_uq":
        return r.reshape(L, Q_LORA, 2, HEAD_PAD)[..., :QK_NOPE + QK_ROPE].reshape(L, Q_LORA, -1)
    if n == "w_ukv":
        r = r.reshape(L, KV_LORA, 2, 2, HEAD_PAD)[..., :QK_NOPE]
        return jnp.concatenate([r[:, :, 0], r[:, :, 1]], axis=-1).reshape(L, KV_LORA, -1)
    if n == "w_br_a":
        return r.reshape(L, HEADS, HEAD_PAD, -1)[:, :, :V_HEAD].reshape(L, HEADS * V_HEAD, -1)
    return r


def _small_rows(n):
    return -(-n // (8 * 128)) * 8


def _to_small(parts):
    flat = jnp.concatenate([p.reshape(-1) for p in parts])
    rows = _small_rows(flat.shape[0])
    return jnp.pad(flat, (0, rows * 128 - flat.shape[0])).reshape(rows, 128)


def kernel(x, w_in, g_pre_mix, g_cq, g_ckv, w_uq, w_ukv, pool_w, pool_scale, g_sgu_v, sgu_w, sgu_b, conv_w, w_br_a, w_br_b, w_br_c, w_br_d, w_out, g_post_mix, g_pre_ffn, w_ffn_gate, w_ffn_up, w_ffn_down, g_post_ffn, loss_target, m_w_in, m_g_pre_mix, m_g_cq, m_g_ckv, m_w_uq, m_w_ukv, m_pool_w, m_pool_scale, m_g_sgu_v, m_sgu_w, m_sgu_b, m_conv_w, m_w_br_a, m_w_br_b, m_w_br_c, m_w_br_d, m_w_out, m_g_post_mix, m_g_pre_ffn, m_w_ffn_gate, m_w_ffn_up, m_w_ffn_down, m_g_post_ffn, v_w_in, v_g_pre_mix, v_g_cq, v_g_ckv, v_w_uq, v_w_ukv, v_pool_w, v_pool_scale, v_g_sgu_v, v_sgu_w, v_sgu_b, v_conv_w, v_w_br_a, v_w_br_b, v_w_br_c, v_w_br_d, v_w_out, v_g_post_mix, v_g_pre_ffn, v_w_ffn_gate, v_w_ffn_up, v_w_ffn_down, v_g_post_ffn):
    local = dict(locals())
    W = {n: local[n] for n in WEIGHTS}
    M = {n: local["m_" + n] for n in WEIGHTS}
    V = {n: local["v_" + n] for n in WEIGHTS}
    chip = 2 * lax.axis_index("x") + lax.axis_index("y")
    core = lax.axis_index("c")

    gathered = dict(zip(GATHERED, _gather_weights([_place_own(W[n], chip, n) for n in GATHERED], 0)))
    conv_shape = conv_w.shape
    conv_cols = conv_shape[-1]
    conv_full_shape = conv_shape[:-1] + (N_CHIPS * conv_cols,)
    placed = lax.dynamic_update_slice(jnp.zeros(conv_full_shape, F32), conv_w, (0, 0, 0, chip * conv_cols))
    n_conv = int(np.prod(conv_full_shape))
    conv_sum = _all_reduce_small(_to_small([placed]), "gather_conv_w")
    full = {n: W[n] for n in REPLICATED}
    full["conv_w"] = 0.5 * conv_sum.reshape(-1)[:n_conv].reshape(conv_full_shape)

    loss_parts, grad_x, (sums_up, slots_up), last, small = _local_step(x, loss_target, gathered, full, core)
    loss = lax.psum(jnp.sum(loss_parts), ("x", "y", "c"))

    small_sum = _all_reduce_small(_to_small([small[n] for n in SMALL]), "reduce_small_grads")
    small_grads = dict(zip(SMALL, _unpack(small_sum, [small[n].shape for n in SMALL])))
    small_grads["conv_w"] = lax.dynamic_slice(small_grads["conv_w"], (0, 0, 0, chip * conv_cols), conv_shape)

    first = [(0, 1)] * len(GATHERED)
    Gs = [last[n] for n in GATHERED]
    got = _run_ride(_exchange_ride(Gs, first), "rs_exchange_halves")
    sums_0 = [_add_halves(g, (0, 1), r, core, n) for n, g, r in zip(GATHERED, Gs, got)]
    slots_0 = _run_ride(_scatter_ride(sums_0), "rs_scatter_partials")
    place = [lax.axis_index(a).reshape(1).astype(jnp.int32) for a in ("x", "y", "c")]
    halves = []
    for n, h, s in zip(GATHERED, sums_0, slots_0):
        upper = _sum_slots(sums_up[n], slots_up[n], place, n, 1, DEPTH)
        halves.append(_sum_slots(h, s, place, n, 0, DEPTH, prev=upper))
    shard_grads = {n: _unpad_reduced(n, r).reshape(W[n].shape) for n, r in zip(GATHERED, _join_halves(halves))}

    out_g, out_d, out_m, out_v = {}, {}, {}, {}
    for n in GATHERED:
        shp = W[n].shape
        flat = lambda a: a.reshape(-1, shp[-1])
        d, m2, v2 = _adamw(flat(W[n]), flat(shard_grads[n]), flat(M[n]), flat(V[n]), "adamw_" + n)
        out_g[n], out_d[n], out_m[n], out_v[n] = shard_grads[n], d.reshape(shp), m2.reshape(shp), v2.reshape(shp)
    rest_shapes = [W[n].shape for n in SMALL]
    d, m2, v2 = _adamw(_to_small([W[n] for n in SMALL]), _to_small([small_grads[n] for n in SMALL]),
                       _to_small([M[n] for n in SMALL]), _to_small([V[n] for n in SMALL]), "adamw_small")
    for n, dd, mm, vv in zip(SMALL, _unpack(d, rest_shapes), _unpack(m2, rest_shapes), _unpack(v2, rest_shapes)):
        out_g[n], out_d[n], out_m[n], out_v[n] = small_grads[n], dd, mm, vv

    return (loss, grad_x, *[out_g[n] for n in WEIGHTS], *[out_d[n] for n in WEIGHTS], *[out_m[n] for n in WEIGHTS],
            *[out_v[n] for n in WEIGHTS])
```

```python
import functools

import numpy as np
import jax
import jax.numpy as jnp
from jax import lax
from jax.experimental import pallas as pl
from jax.experimental.pallas import tpu as pltpu

F32 = jnp.float32
BF16 = jnp.bfloat16

EPS = 1e-6
NEG_INF = -1e30
DEPTH = 4
HEADS = 8
QK_NOPE = 64
QK_ROPE = 32
V_HEAD = 64
HEAD_PAD = 128
Q_LORA = 256
KV_LORA = 128
ROPE_THETA = 10000.0
POOL_WINDOWS = (2, 4, 8, 16)
GROUPS = 4
GROUP_DIM = 64
MIX_W = GROUPS * GROUP_DIM
POOL_HALO = 16
CONV_HALO = 16
SGU_BLOCK = 128
CHUNK = 64
CHUNK_SHIFT = 6
ACT_ROWS = 16
NORM_ROWS = 16
ROW_PARTS = 2
SOFTMAX_ROWS = 32
GROUP_SHIFT = 6
N_BRANCH = 4
MLA_W = HEADS * HEAD_PAD
N_CHIPS = 4
CHIP_HEADS_W = MLA_W // N_CHIPS
CHIP_KV = 2 * CHIP_HEADS_W
ATTN_SCALE = (QK_NOPE + QK_ROPE) ** -0.5
LOG2E = 1.4426950408889634
LN2 = 0.6931471805599453
SEC_MLA = Q_LORA + KV_LORA + HEAD_PAD
SEC_MIX = 6 * MIX_W

ADAM_LR = 0.001
ADAM_B1 = 0.9
ADAM_B2 = 0.999
ADAM_EPS = 1e-08
ADAM_WD = 0.01
ADAM_STEP = 10

VMEM_LIMIT = 56 * 1024 * 1024
MESH = pl.DeviceIdType.MESH


def _tile(n, pref, mult=8):
    t = min(n, pref)
    while t > 0:
        if n % t == 0 and t % mult == 0:
            return t
        t -= 1
    return n


def _params(n_axes):
    return pltpu.CompilerParams(dimension_semantics=("arbitrary",) * n_axes, vmem_limit_bytes=VMEM_LIMIT)


def _dot(a, b):
    return jnp.dot(a, b, preferred_element_type=F32)


def _dot_nt(a, b):
    return lax.dot_general(a, b, (((1,), (1,)), ((), ())), preferred_element_type=F32)


def _dot_tn(a, b):
    return lax.dot_general(a, b, (((0,), (0,)), ((), ())), preferred_element_type=F32)


def _rms_r(x):
    return lax.rsqrt(jnp.mean(x * x, axis=-1, keepdims=True) + EPS)


def _rms_bwd(dy, x, g):
    r = _rms_r(x)
    u = dy * g
    dx = r * u - x * (r * r * r * jnp.mean(u * x, axis=-1, keepdims=True))
    dg = jnp.sum(dy * x * r, axis=0, keepdims=True)
    return dx, dg


def _sigmoid(x):
    return 1.0 / (1.0 + jnp.exp(-x))


def _shift_down(a, k):
    return pltpu.roll(a, k, 0)


def _shift_up(a, k):
    return pltpu.roll(a, a.shape[0] - k, 0)


def _rope(x, c, sa, sb):
    w = x.shape[-1]
    return x * c + pltpu.roll(x, QK_ROPE // 2, 1) * sa + pltpu.roll(x, w - QK_ROPE // 2, 1) * sb


def _rope_t(d, c, sa, sb):
    w = d.shape[-1]
    return d * c + pltpu.roll(d * sa, w - QK_ROPE // 2, 1) + pltpu.roll(d * sb, QK_ROPE // 2, 1)


def _full(shape):
    return pl.BlockSpec(shape, lambda *_: (0,) * len(shape))


def _gather_ride(bufs, l, stage):
    def copies(_, o_refs, send_sems, recv_sems):
        x, y, c, me = _place()
        sends, arrivals = [], []
        for t, o in enumerate(o_refs):
            hr = o.shape[2] // 2
            for d in (1, 2, 3):
                sems = (send_sems.at[3 * t + d - 1], recv_sems.at[3 * t + d - 1])
                mine = o.at[l, me, pl.ds(c * hr, hr), :]
                theirs = o.at[l, me ^ d, pl.ds(c * hr, hr), :]
                other_half = o.at[l, me ^ d, pl.ds((1 - c) * hr, hr), :]
                if stage == "chips":
                    sends.append(_remote(mine, mine, *sems, _chip_device(me ^ d, c)))
                    arrivals.append(_remote(theirs, theirs, *sems, _chip_device(me ^ d, c)))
                else:
                    sends.append(_remote(theirs, theirs, *sems, (x, y, 1 - c)))
                    arrivals.append(_remote(other_half, other_half, *sems, (x, y, 1 - c)))
        return sends, arrivals

    shapes = [jax.ShapeDtypeStruct(b.shape, b.dtype) for b in bufs]
    return dict(ins=list(bufs), outs=shapes, alias=True, copies=copies, n_sems=3 * len(bufs))


def _exchange_ride(Gs, spans):
    def copies(g_refs, o_refs, send_sems, recv_sems):
        x, y, c, _ = _place()
        cps = []
        for t, (g, o) in enumerate(zip(g_refs, o_refs)):
            hr = g.shape[2] // 2
            l0, l1 = spans[t]
            cps.append(_remote(g.at[:, pl.ds(l0, l1 - l0), pl.ds((1 - c) * hr, hr), :], o, send_sems.at[t], recv_sems.at[t],
                               (x, y, 1 - c)))
        return cps, cps

    shapes = [jax.ShapeDtypeStruct((g.shape[0], l1 - l0, g.shape[2] // 2, g.shape[3]), g.dtype) for g, (l0, l1) in zip(Gs, spans)]
    return dict(ins=list(Gs), outs=shapes, alias=False, copies=copies, n_sems=len(Gs))


def _scatter_ride(Hs):
    def copies(h_refs, o_refs, send_sems, recv_sems):
        x, y, c, me = _place()
        sends, arrivals = [], []
        for t, (h, o) in enumerate(zip(h_refs, o_refs)):
            for d in (1, 2, 3):
                sems = (send_sems.at[3 * t + d - 1], recv_sems.at[3 * t + d - 1])
                sends.append(_remote(h.at[me ^ d], o.at[me], *sems, _chip_device(me ^ d, c)))
                arrivals.append(_remote(h.at[me ^ d], o.at[me ^ d], *sems, _chip_device(me ^ d, c)))
        return sends, arrivals

    shapes = [jax.ShapeDtypeStruct(h.shape, h.dtype) for h in Hs]
    return dict(ins=list(Hs), outs=shapes, alias=False, copies=copies, n_sems=3 * len(Hs))


def _ride_specs(ride, n_in, n_out):
    if not ride:
        return [], [], [], [], [], {}
    anywhere = pl.BlockSpec(memory_space=pl.ANY)
    sems = pltpu.SemaphoreType.DMA((ride["n_sems"],))
    alias = {n_in + i: n_out + i for i in range(len(ride["ins"]))} if ride["alias"] else {}
    return [anywhere] * len(ride["ins"]), [anywhere] * len(ride["outs"]), list(ride["outs"]), [sems, sems], ride["ins"], alias


def _ride_split(ride, rest, n_out, n_scratch):
    a = len(ride["ins"]) if ride else 0
    b = a + n_out
    c = b + (len(ride["outs"]) if ride else 0)
    d = c + n_scratch
    riding = (ride, rest[:a], rest[b:c], rest[d:]) if ride else None
    return rest[a:b], rest[c:d], riding


def _ride_start(riding, first):
    if riding:
        ride, in_refs, out_refs, (send_sems, recv_sems) = riding

        @pl.when(first)
        def _():
            for cp in ride["copies"](in_refs, out_refs, send_sems, recv_sems)[0]:
                cp.start()


def _ride_finish(riding, last):
    if riding:
        ride, in_refs, out_refs, (send_sems, recv_sems) = riding

        @pl.when(last)
        def _():
            sends, arrivals = ride["copies"](in_refs, out_refs, send_sems, recv_sems)
            for cp in arrivals:
                cp.wait_recv()
            for cp in sends:
                cp.wait_send()


def _run_ride(ride, name):
    def body(*refs):
        _, _, (_, in_refs, out_refs, (send_sems, recv_sems)) = _ride_split(ride, refs, 0, 0)
        sends, arrivals = ride["copies"](in_refs, out_refs, send_sems, recv_sems)
        for cp in sends:
            cp.start()
        for cp in arrivals:
            cp.wait_recv()
        for cp in sends:
            cp.wait_send()

    in_specs, out_specs, out_shapes, scratch, operands, alias = _ride_specs(ride, 0, 0)
    return list(pl.pallas_call(body, name=name, in_specs=in_specs, out_specs=out_specs, out_shape=out_shapes,
                               input_output_aliases=alias, scratch_shapes=scratch)(*operands))


def _norm_matmul(x, g, w, name):
    T, K = x.shape
    N = w.shape[1]
    tm, tn = _tile(T, 512), _tile(N, 1536, 128)

    def body(x_ref, g_ref, w_ref, o_ref, h_ref):
        @pl.when(pl.program_id(1) == 0)
        def _():
            xv = x_ref[...]
            h_ref[...] = (xv * _rms_r(xv) * g_ref[...]).astype(BF16)

        o_ref[...] = _dot(h_ref[...], w_ref[...]).astype(BF16)

    return pl.pallas_call(
        body, name=name, grid=(T // tm, N // tn),
        in_specs=[pl.BlockSpec((tm, K), lambda i, j: (i, 0)), _full((1, K)), pl.BlockSpec((K, tn), lambda i, j: (0, j))],
        out_specs=[pl.BlockSpec((tm, tn), lambda i, j: (i, j)), pl.BlockSpec((tm, K), lambda i, j: (i, 0))],
        out_shape=[jax.ShapeDtypeStruct((T, N), BF16), jax.ShapeDtypeStruct((T, K), BF16)],
        compiler_params=_params(2),
    )(x, g, w)


def _mla_prep(P, D, g_cq, g_ckv, wuq, wukv, rope_c, rope_sa, rope_sb, S):
    T = P.shape[0]
    tm = _tile(S, 512)
    n_si = S // tm
    base = 4 * D

    def body(cq_ref, ckv_ref, kr_ref, gq_ref, gkv_ref, wq_ref, wkv_ref, c_ref, sa_ref, sb_ref,
             q_ref, k_ref, v_ref, hq_ref, hkv_ref):
        c, sa, sb = c_ref[...], sa_ref[...], sb_ref[...]
        cq = cq_ref[...].astype(F32)
        hq = (cq * _rms_r(cq) * gq_ref[...]).astype(BF16)
        hq_ref[...] = hq
        q = _dot(hq, wq_ref[...])
        q = _rope(q, jnp.tile(c, (1, HEADS)), jnp.tile(sa, (1, HEADS)), jnp.tile(sb, (1, HEADS)))
        q_ref[...] = (q * (ATTN_SCALE * LOG2E)).astype(BF16)
        ckv = ckv_ref[...].astype(F32)
        hkv = (ckv * _rms_r(ckv) * gkv_ref[...]).astype(BF16)
        hkv_ref[...] = hkv
        kv = _dot(hkv, wkv_ref[...])
        kr = _rope(kr_ref[...].astype(F32), c, sa, sb)
        k_nope = jnp.concatenate([kv[:, j * CHIP_KV:j * CHIP_KV + CHIP_HEADS_W] for j in range(N_CHIPS)], axis=1)
        k_ref[...] = (k_nope + jnp.tile(kr, (1, HEADS))).astype(BF16)
        v = jnp.concatenate([kv[:, j * CHIP_KV + CHIP_HEADS_W:(j + 1) * CHIP_KV] for j in range(N_CHIPS)], axis=1)
        ones_lane = (lax.broadcasted_iota(jnp.int32, (1, MLA_W), 1) & (HEAD_PAD - 1)) == V_HEAD
        v_ref[...] = jnp.where(ones_lane, 1.0, v).astype(BF16)

    tab = pl.BlockSpec((tm, HEAD_PAD), lambda i: (i % n_si, 0))
    row = lambda w: pl.BlockSpec((tm, w), lambda i: (i, 0))
    return pl.pallas_call(
        body, name="mla_prep", grid=(T // tm,),
        in_specs=[pl.BlockSpec((tm, Q_LORA), lambda i: (i, base // Q_LORA)),
                  pl.BlockSpec((tm, KV_LORA), lambda i: (i, (base + Q_LORA) // KV_LORA)),
                  pl.BlockSpec((tm, HEAD_PAD), lambda i: (i, (base + Q_LORA + KV_LORA) // HEAD_PAD)),
                  _full((1, Q_LORA)), _full((1, KV_LORA)), _full((Q_LORA, MLA_W)), _full((KV_LORA, 2 * MLA_W)),
                  tab, tab, tab],
        out_specs=[row(MLA_W), row(MLA_W), row(MLA_W), row(Q_LORA), row(KV_LORA)],
        out_shape=[jax.ShapeDtypeStruct((T, MLA_W), BF16)] * 3
        + [jax.ShapeDtypeStruct((T, Q_LORA), BF16), jax.ShapeDtypeStruct((T, KV_LORA), BF16)],
        compiler_params=_params(1),
    )(P, P, P, g_cq, g_ckv, wuq, wukv, rope_c, rope_sa, rope_sb)


def _chunk_mask(tq, tk):
    row = lax.broadcasted_iota(jnp.int32, (tq, tk), 0)
    col = lax.broadcasted_iota(jnp.int32, (tq, tk), 1)
    return (row >> CHUNK_SHIFT) >= (col >> CHUNK_SHIFT)


def _flash_fwd(Q, K, V, S, ride=None):
    T = Q.shape[0]
    n_seq = T // S
    tq = _tile(S, 256, 128)
    nq = S // tq

    def body(q_ref, k_ref, v_ref, *rest):
        (o_ref, lse_ref), (m_s, acc_s, s_s, p_s, a_s), riding = _ride_split(ride, rest, 2, 5)
        _ride_start(riding, (pl.program_id(0) == 0) & (pl.program_id(1) == 0))
        qi = pl.program_id(1)
        m_s[...] = jnp.full(m_s.shape, NEG_INF, F32)
        acc_s[...] = jnp.zeros_like(acc_s)

        def block(kb, masked):
            rows = pl.ds(pl.multiple_of(kb * tq, tq), tq)
            for h in range(HEADS):
                hs = slice(h * HEAD_PAD, (h + 1) * HEAD_PAD)
                s_s[h] = _dot_nt(q_ref[:, hs], k_ref[rows, hs])
            def softmax_head(h):
                for r in range(0, tq, SOFTMAX_ROWS):
                    rs = slice(r, r + SOFTMAX_ROWS)
                    s = s_s[h, rs, :]
                    if masked:
                        row = r + lax.broadcasted_iota(jnp.int32, (SOFTMAX_ROWS, tq), 0)
                        col = lax.broadcasted_iota(jnp.int32, (SOFTMAX_ROWS, tq), 1)
                        s = jnp.where((row >> CHUNK_SHIFT) >= (col >> CHUNK_SHIFT), s, NEG_INF)
                    m_old = m_s[h, rs]
                    m_new = jnp.maximum(m_old, jnp.max(s, axis=-1, keepdims=True))
                    m_s[h, rs] = m_new
                    a_s[h, rs] = jnp.exp2(m_old - m_new)
                    for half in range(tq // HEAD_PAD):
                        cs = slice(half * HEAD_PAD, (half + 1) * HEAD_PAD)
                        p_s[h, rs, cs] = jnp.exp2(s[:, cs] - m_new).astype(BF16)

            for h in range(HEADS):
                softmax_head(h)
            for h in range(HEADS):
                hs = slice(h * HEAD_PAD, (h + 1) * HEAD_PAD)
                acc_s[:, hs] = a_s[h] * acc_s[:, hs] + _dot(p_s[h], v_ref[rows, hs])

        def full_block(kb, carry):
            block(kb, False)
            return carry

        lax.fori_loop(0, qi, full_block, 0)
        block(qi, True)
        lane = lax.broadcasted_iota(jnp.int32, (tq, HEAD_PAD), 1)
        lse_all = jnp.zeros((tq, HEAD_PAD), F32)
        for h in range(HEADS):
            hs = slice(h * HEAD_PAD, (h + 1) * HEAD_PAD)
            acc = acc_s[:, hs]
            l = jnp.sum(jnp.where(lane == V_HEAD, acc, 0.0), axis=-1, keepdims=True)
            o_ref[:, hs] = (acc / l).astype(BF16)
            lse_all = jnp.where(lane == h, m_s[h] + jnp.log2(l), lse_all)
        lse_ref[...] = lse_all
        _ride_finish(riding, (pl.program_id(0) == n_seq - 1) & (pl.program_id(1) == nq - 1))

    ride_in, ride_out, ride_shapes, ride_scratch, ride_args, alias = _ride_specs(ride, 3, 2)
    out = pl.pallas_call(
        body, name="flash_fwd", grid=(n_seq, nq),
        in_specs=[pl.BlockSpec((tq, MLA_W), lambda b, i: (b * nq + i, 0)),
                  pl.BlockSpec((S, MLA_W), lambda b, i: (b, 0)), pl.BlockSpec((S, MLA_W), lambda b, i: (b, 0))] + ride_in,
        out_specs=[pl.BlockSpec((tq, MLA_W), lambda b, i: (b * nq + i, 0)),
                   pl.BlockSpec((tq, HEAD_PAD), lambda b, i: (b * nq + i, 0))] + ride_out,
        out_shape=[jax.ShapeDtypeStruct((T, MLA_W), BF16), jax.ShapeDtypeStruct((T, HEAD_PAD), F32)] + ride_shapes,
        input_output_aliases=alias,
        scratch_shapes=[pltpu.VMEM((HEADS, tq, HEAD_PAD), F32), pltpu.VMEM((tq, MLA_W), F32), pltpu.VMEM((HEADS, tq, tq), F32),
                        pltpu.VMEM((HEADS, tq, tq), BF16), pltpu.VMEM((HEADS, tq, HEAD_PAD), F32)] + ride_scratch,
        compiler_params=_params(2),
    )(Q, K, V, *ride_args)
    return out[0], out[1], list(out[2:])


def _lane_group():
    return lax.broadcasted_iota(jnp.int32, (1, MIX_W), 1) >> GROUP_SHIFT


def _by_group(a0, a1, a2, a3):
    g = _lane_group()
    return jnp.where(g == 0, a0, jnp.where(g == 1, a1, jnp.where(g == 2, a2, a3)))


def _pool_count(si, tc, rows):
    pos = si * tc + lax.broadcasted_iota(jnp.int32, (rows, MIX_W), 0)
    win = _by_group(*POOL_WINDOWS)
    return jnp.minimum(pos + 1, win).astype(F32)


def _pool_fwd(z, z_prev, si, tc):
    ze = jnp.concatenate([z_prev, z], axis=0)
    s1 = ze + _shift_down(ze, 1)
    s2 = s1 + _shift_down(s1, 2)
    s4 = s2 + _shift_down(s2, 4)
    s8 = s4 + _shift_down(s4, 8)
    win_sum = _by_group(s1, s2, s4, s8)[POOL_HALO:]
    return win_sum / _pool_count(si, tc, tc) - z


def _sgu_weights(w_ref):
    row = lax.broadcasted_iota(jnp.int32, (SGU_BLOCK, SGU_BLOCK), 0)
    col = lax.broadcasted_iota(jnp.int32, (SGU_BLOCK, SGU_BLOCK), 1)
    keep = (row >> CHUNK_SHIFT) >= (col >> CHUNK_SHIFT)
    return keep, [jnp.where(keep, w_ref[g], 0.0).astype(BF16) for g in range(GROUPS)]


def _sgu_mix(vn_blk, wm, bias):
    g = _lane_group()
    mixed = bias
    for k in range(GROUPS):
        mixed = mixed + jnp.where(g == k, _dot(wm[k], vn_blk), 0.0)
    return mixed


def _conv_fwd(z, z_prev, w_ref):
    ze = jnp.concatenate([z_prev, z], axis=0)
    y = w_ref[0:1, :] * _shift_down(ze, 2) + w_ref[1:2, :] * _shift_down(ze, 1) + w_ref[2:3, :] * ze
    return y[CONV_HALO:]


def _mix_specs(T, D, tc):
    base = (4 * D + SEC_MLA) // MIX_W
    cur = lambda k: pl.BlockSpec((tc, MIX_W), lambda i: (i, base + k))
    prev = lambda k, halo: pl.BlockSpec((halo, MIX_W), lambda i: (jnp.maximum(i * (tc // halo) - 1, 0), base + k))
    nxt = lambda k, halo: pl.BlockSpec((halo, MIX_W), lambda i: (jnp.minimum((i + 1) * (tc // halo), T // halo - 1), base + k))
    return cur, prev, nxt


def _mixers_fwd(P, D, S, pool_bd, pool_scale, g_v, sgu_w, sgu_bias, conv_w, ride=None):
    T = P.shape[0]
    tc = _tile(S, 512, SGU_BLOCK)
    n_si = S // tc
    cur, prev, _ = _mix_specs(T, D, tc)

    def body(z_ref, zp_ref, u_ref, v_ref, b_ref, c_ref, x_ref, cp_ref, xp_ref,
             pw_ref, ps_ref, gv_ref, sw_ref, sb_ref, cw_ref, *rest):
        (ob_ref, oc_ref, od_ref), _, riding = _ride_split(ride, rest, 3, 0)
        _ride_start(riding, pl.program_id(0) == 0)
        si = pl.program_id(0) % n_si
        first = si == 0
        z = z_ref[...].astype(F32)
        pooled = _pool_fwd(z, jnp.where(first, 0.0, zp_ref[...].astype(F32)), si, tc)
        ob_ref[...] = (_dot(pooled.astype(BF16), pw_ref[...]) * ps_ref[...]).astype(BF16)

        v = v_ref[...].astype(F32)
        vn = (v * _rms_r(v) * gv_ref[...]).astype(BF16)
        _, wm = _sgu_weights(sw_ref)
        for blk in range(tc // SGU_BLOCK):
            rows = slice(blk * SGU_BLOCK, (blk + 1) * SGU_BLOCK)
            oc_ref[rows, :] = (u_ref[rows, :].astype(F32) * _sgu_mix(vn[rows], wm, sb_ref[...])).astype(BF16)

        zc = c_ref[...].astype(F32) * x_ref[...].astype(F32)
        zc_prev = jnp.where(first, 0.0, cp_ref[...].astype(F32) * xp_ref[...].astype(F32))
        od_ref[...] = (b_ref[...].astype(F32) * _conv_fwd(zc, zc_prev, cw_ref)).astype(BF16)
        _ride_finish(riding, pl.program_id(0) == T // tc - 1)

    out = pl.BlockSpec((tc, MIX_W), lambda i: (i, 0))
    ride_in, ride_out, ride_shapes, ride_scratch, ride_args, alias = _ride_specs(ride, 15, 3)
    res = pl.pallas_call(
        body, name="mixers_fwd", grid=(T // tc,),
        in_specs=[cur(0), prev(0, POOL_HALO), cur(1), cur(2), cur(3), cur(4), cur(5), prev(4, CONV_HALO), prev(5, CONV_HALO),
                  _full((MIX_W, MIX_W)), _full((1, MIX_W)), _full((1, MIX_W)), _full((GROUPS, SGU_BLOCK, SGU_BLOCK)),
                  _full((SGU_BLOCK, MIX_W)), _full((CONV_HALO, MIX_W))] + ride_in,
        out_specs=[out, out, out] + ride_out,
        out_shape=[jax.ShapeDtypeStruct((T, MIX_W), BF16)] * 3 + ride_shapes,
        input_output_aliases=alias,
        scratch_shapes=ride_scratch,
        compiler_params=_params(1),
    )(P, P, P, P, P, P, P, P, P, pool_bd, pool_scale, g_v, sgu_w, sgu_bias, conv_w, *ride_args)
    return res[0], res[1], res[2], list(res[3:])


def _merge_fwd(x, P, A, Bm, C, Dv, wa, wb, wc, wd, wout, g_post):
    T, D = x.shape
    tm = _tile(T, 256)

    def body(x_ref, lg_ref, a_ref, b_ref, c_ref, d_ref, wa_ref, wb_ref, wc_ref, wd_ref, wo_ref, g_ref,
             x1_ref, mg_ref, o_ref):
        merged = jnp.zeros((tm, D), F32)
        for k, (br, w) in enumerate(((a_ref, wa_ref), (b_ref, wb_ref), (c_ref, wc_ref), (d_ref, wd_ref))):
            merged = merged + _sigmoid(lg_ref[:, k * D:(k + 1) * D].astype(F32)) * _dot(br[...], w[...])
        mg = merged.astype(BF16)
        mg_ref[...] = mg
        o = _dot(mg, wo_ref[...])
        o_ref[...] = o
        x1_ref[...] = x_ref[...] + o * _rms_r(o) * g_ref[...]

    row = lambda w: pl.BlockSpec((tm, w), lambda i: (i, 0))
    return pl.pallas_call(
        body, name="merge_fwd", grid=(T // tm,),
        in_specs=[row(D), row(4 * D), row(MLA_W), row(MIX_W), row(MIX_W), row(MIX_W),
                  _full((MLA_W, D)), _full((MIX_W, D)), _full((MIX_W, D)), _full((MIX_W, D)), _full((D, D)), _full((1, D))],
        out_specs=[row(D), row(D), row(D)],
        out_shape=[jax.ShapeDtypeStruct((T, D), F32), jax.ShapeDtypeStruct((T, D), BF16), jax.ShapeDtypeStruct((T, D), F32)],
        compiler_params=_params(1),
    )(x, P, A, Bm, C, Dv, wa, wb, wc, wd, wout, g_post)


def _ffn_specs(T, D, Fc, l, rows=512):
    tm = _tile(T, rows)
    row = pl.BlockSpec((tm, D), lambda i, j: (i, 0))
    col = pl.BlockSpec((None, tm, Fc), lambda i, j: (j, i, 0))
    w_in = pl.BlockSpec((None, None, D, Fc), lambda i, j: (l, j, 0, 0))
    w_out = pl.BlockSpec((None, None, Fc, D), lambda i, j: (l, j, 0, 0))
    return tm, row, col, w_in, w_out


def _ffn_fwd(x1, g_pre, wg, wu, wdn, g_post, l):
    T, D = x1.shape
    nf, Fc = wg.shape[1], wg.shape[3]
    tm, row, col, w_in, w_out = _ffn_specs(T, D, Fc, l, 1024)

    def body(x_ref, gp_ref, wg_ref, wu_ref, wd_ref, gq_ref, x2_ref, h_ref, gt_ref, up_ref, f_ref, gt_s, up_s, a_s):
        j = pl.program_id(1)

        @pl.when(j == 0)
        def _():
            for r in range(0, tm, NORM_ROWS):
                rs = slice(r, r + NORM_ROWS)
                xv = x_ref[rs, :]
                h_ref[rs, :] = (xv * _rms_r(xv) * gp_ref[...]).astype(BF16)
            f_ref[...] = jnp.zeros_like(f_ref)

        gt_s[...] = _dot(h_ref[...], wg_ref[...])
        up_s[...] = _dot(h_ref[...], wu_ref[...])
        for r in range(0, tm, ACT_ROWS):
            rs = slice(r, r + ACT_ROWS)
            gt, up = gt_s[rs, :], up_s[rs, :]
            gt_ref[rs, :] = gt.astype(BF16)
            up_ref[rs, :] = up.astype(BF16)
            a_s[rs, :] = (gt * _sigmoid(gt) * up).astype(BF16)
        f_ref[...] += _dot(a_s[...], wd_ref[...])

        @pl.when(j == nf - 1)
        def _():
            for r in range(0, tm, NORM_ROWS):
                rs = slice(r, r + NORM_ROWS)
                f = f_ref[rs, :]
                x2_ref[rs, :] = x_ref[rs, :] + f * _rms_r(f) * gq_ref[...]

    return pl.pallas_call(
        body, name="ffn_fwd", grid=(T // tm, nf),
        in_specs=[row, _full((1, D)), w_in, w_in, w_out, _full((1, D))],
        out_specs=[row, row, col, col, row],
        out_shape=[jax.ShapeDtypeStruct((T, D), F32), jax.ShapeDtypeStruct((T, D), BF16),
                   jax.ShapeDtypeStruct((nf, T, Fc), BF16), jax.ShapeDtypeStruct((nf, T, Fc), BF16),
                   jax.ShapeDtypeStruct((T, D), F32)],
        scratch_shapes=[pltpu.VMEM((tm, Fc), F32), pltpu.VMEM((tm, Fc), F32), pltpu.VMEM((tm, Fc), BF16)],
        compiler_params=_params(2),
    )(x1, g_pre, wg, wu, wdn, g_post)


def _loss_grad(y, target):
    T, D = y.shape
    tm = _tile(T, 512)

    def body(y_ref, t_ref, l_ref, dy_ref):
        @pl.when(pl.program_id(0) == 0)
        def _():
            l_ref[...] = jnp.zeros_like(l_ref)

        d = y_ref[...] - t_ref[...]
        dy_ref[...] = d * (1.0 / D)
        e = jnp.sum((d * d).reshape(tm // 8, 8, D), axis=0)
        part = e[:, 0:128]
        for k in range(1, D // 128):
            part = part + e[:, k * 128:(k + 1) * 128]
        l_ref[...] += part * (0.5 / D)

    row = pl.BlockSpec((tm, D), lambda i: (i, 0))
    return pl.pallas_call(
        body, name="loss_grad", grid=(T // tm,),
        in_specs=[row, row], out_specs=[_full((8, 128)), row],
        out_shape=[jax.ShapeDtypeStruct((8, 128), F32), jax.ShapeDtypeStruct((T, D), F32)],
        compiler_params=_params(1),
    )(y, target)


def _matmul_tn(a, b, name):
    T, M = a.shape
    N = b.shape[1]
    tm, tn, tk = _tile(M, 1024, 128), _tile(N, 1536, 128), _tile(T, 512)

    def body(a_ref, b_ref, o_ref):
        @pl.when(pl.program_id(2) == 0)
        def _():
            o_ref[...] = jnp.zeros_like(o_ref)

        o_ref[...] += _dot_tn(a_ref[...], b_ref[...])

    return pl.pallas_call(
        body, name=name, grid=(M // tm, N // tn, T // tk),
        in_specs=[pl.BlockSpec((tk, tm), lambda i, j, k: (k, i)), pl.BlockSpec((tk, tn), lambda i, j, k: (k, j))],
        out_specs=pl.BlockSpec((tm, tn), lambda i, j, k: (i, j)),
        out_shape=jax.ShapeDtypeStruct((M, N), F32),
        compiler_params=_params(3),
    )(a, b)


def _norm_bwd(dy, x, g, name, add=None, parts=None):
    T, D = x.shape
    tm = _tile(T, 256)
    n_parts = 0 if parts is None else parts.shape[0]

    def body(*refs):
        dx_ref, dg_ref = refs[-2:]
        ins = list(refs[:-2])
        dy_ref = ins.pop(0) if parts is None else None
        p_ref = ins.pop(0) if parts is not None else None
        x_ref, g_ref = ins[0], ins[1]
        add_ref = ins[2] if add is not None else None

        @pl.when(pl.program_id(0) == 0)
        def _():
            dg_ref[...] = jnp.zeros_like(dg_ref)

        dg_sum = jnp.zeros((1, D), F32)
        for r in range(0, tm, NORM_ROWS):
            rs = slice(r, r + NORM_ROWS)
            if parts is None:
                dy = dy_ref[rs, :]
            else:
                dy = p_ref[0, rs, :]
                for k in range(1, n_parts):
                    dy = dy + p_ref[k, rs, :]
            dx, dg = _rms_bwd(dy, x_ref[rs, :], g_ref[...])
            dx_ref[rs, :] = (dx if add is None else add_ref[rs, :] + dx).astype(dx_ref.dtype)
            dg_sum = dg_sum + dg
        dg_ref[...] += dg_sum

    row = pl.BlockSpec((tm, D), lambda i: (i, 0))
    first = [row] if parts is None else [pl.BlockSpec((n_parts, tm, D), lambda i: (0, i, 0))]
    operands = [dy if parts is None else parts, x, g] + ([] if add is None else [add])
    return pl.pallas_call(
        body, name=name, grid=(T // tm,),
        in_specs=first + [row, _full((1, D))] + ([] if add is None else [row]),
        out_specs=[row, _full((1, D))],
        out_shape=[jax.ShapeDtypeStruct((T, D), BF16 if add is None else F32), jax.ShapeDtypeStruct((1, D), F32)],
        compiler_params=_params(1),
    )(*operands)


FFN_WEIGHTS = ("w_ffn_gate", "w_ffn_up", "w_ffn_down")


def _ffn_bwd_main(df, gt, up, h2, wg, wu, wdn, l, bufs, ride=None):
    T, D = h2.shape
    nf, Fc = wg.shape[1], wg.shape[3]
    tm = _tile(T, 512)
    nt = T // tm
    keep = [] if bufs is None else [bufs[n] for n in FFN_WEIGHTS]

    def body(df_ref, gt_ref, up_ref, h_ref, wg_ref, wu_ref, wd_ref, *rest):
        (dh_ref, gg_ref, gu_ref, gd_ref), (da_s, dgt_s, dup_s, act_s), riding = _ride_split(ride, rest[len(keep):], 4, 4)
        j, i = pl.program_id(0), pl.program_id(1)
        _ride_start(riding, (j == 0) & (i == 0))

        @pl.when(i == 0)
        def _():
            for r in (gg_ref, gu_ref, gd_ref):
                r[...] = jnp.zeros_like(r)

        parts = [slice(p * tm // ROW_PARTS, (p + 1) * tm // ROW_PARTS) for p in range(ROW_PARTS)]
        for ps in parts:
            da_s[ps, :] = _dot_nt(df_ref[ps, :], wd_ref[...])
        for ps in parts:
            for r in range(ps.start, ps.stop, ACT_ROWS):
                rs = slice(r, r + ACT_ROWS)
                da = da_s[rs, :]
                gt = gt_ref[rs, :].astype(F32)
                u = up_ref[rs, :].astype(F32)
                sig = _sigmoid(gt)
                silu = gt * sig
                dgt_s[rs, :] = (da * u * (sig * (1.0 + gt * (1.0 - sig)))).astype(BF16)
                dup_s[rs, :] = (da * silu).astype(BF16)
                act_s[rs, :] = (silu * u).astype(BF16)
            dh_ref[ps, :] = _dot_nt(dgt_s[ps, :], wg_ref[...]) + _dot_nt(dup_s[ps, :], wu_ref[...])
        gg_ref[...] += _dot_tn(h_ref[...], dgt_s[...])
        gu_ref[...] += _dot_tn(h_ref[...], dup_s[...])
        gd_ref[...] += _dot_tn(act_s[...], df_ref[...])
        _ride_finish(riding, (j == nf - 1) & (i == nt - 1))

    row = pl.BlockSpec((tm, D), lambda j, i: (i, 0))
    col = pl.BlockSpec((None, tm, Fc), lambda j, i: (j, i, 0))
    w_in = pl.BlockSpec((None, None, D, Fc), lambda j, i: (l, j, 0, 0))
    w_out = pl.BlockSpec((None, None, Fc, D), lambda j, i: (l, j, 0, 0))
    g_in = pl.BlockSpec((None, None, D, Fc), lambda j, i: (j, l, 0, 0))
    g_out = pl.BlockSpec((None, None, Fc, D), lambda j, i: (j, l, 0, 0))
    n_in = 7
    ride_in, ride_out, ride_shapes, ride_scratch, ride_args, alias = _ride_specs(ride, n_in + len(keep), 4)
    out = pl.pallas_call(
        body, name="ffn_bwd_main", grid=(nf, nt),
        in_specs=[row, col, col, row, w_in, w_in, w_out] + [pl.BlockSpec(memory_space=pl.ANY)] * len(keep) + ride_in,
        out_specs=[pl.BlockSpec((None, tm, D), lambda j, i: (j, i, 0)), g_in, g_in, g_out] + ride_out,
        out_shape=[jax.ShapeDtypeStruct((nf, T, D), F32), jax.ShapeDtypeStruct((nf, DEPTH, D, Fc), F32),
                   jax.ShapeDtypeStruct((nf, DEPTH, D, Fc), F32), jax.ShapeDtypeStruct((nf, DEPTH, Fc, D), F32)] + ride_shapes,
        input_output_aliases={**{n_in + k: 1 + k for k in range(len(keep))}, **alias},
        scratch_shapes=[pltpu.VMEM((tm, Fc), F32), pltpu.VMEM((tm, Fc), BF16), pltpu.VMEM((tm, Fc), BF16),
                        pltpu.VMEM((tm, Fc), BF16)] + ride_scratch,
        compiler_params=_params(2),
    )(df, gt, up, h2, wg, wu, wdn, *keep, *ride_args)
    return out[0], dict(zip(FFN_WEIGHTS, out[1:4])), list(out[4:])


def _wgrad_out(l, n_layers, rows, cols):
    spec = pl.BlockSpec((N_CHIPS, None, rows, cols), lambda *_: (0, l, 0, 0))
    return spec, jax.ShapeDtypeStruct((N_CHIPS, n_layers, rows, cols), F32)


def _merge_bwd(dx1, o, merged, P, A, Bm, C, Dv, wa, wb, wc, wd, wout, g_post, l, bufs, ride=None):
    T, D = o.shape
    tm = _tile(T, 256)
    Dc = D // N_CHIPS
    names = ("w_out", "w_br_a", "w_br_b", "w_br_c", "w_br_d")
    keep = [] if bufs is None else [bufs[n] for n in names]

    def body(dx1_ref, o_ref, mg_ref, lg_ref, a_ref, b_ref, c_ref, d_ref, wa_ref, wb_ref, wc_ref, wd_ref, wo_ref, g_ref, *rest):
        outs, _, riding = _ride_split(ride, rest[len(keep):], 11, 0)
        dlg_ref, da_ref, db_ref, dc_ref, dd_ref, dg_ref, go_ref, ga_ref, gb_ref, gc_ref, gd_ref = outs
        _ride_start(riding, pl.program_id(0) == 0)

        @pl.when(pl.program_id(0) == 0)
        def _():
            for r in (dg_ref, go_ref, ga_ref, gb_ref, gc_ref, gd_ref):
                r[...] = jnp.zeros_like(r)

        d_o, dg = _rms_bwd(dx1_ref[...], o_ref[...], g_ref[...])
        dg_ref[...] += dg
        d_o = d_o.astype(BF16)
        for k in range(N_CHIPS):
            go_ref[k] += _dot_tn(mg_ref[:, k * Dc:(k + 1) * Dc], d_o)
        dm = _dot_nt(d_o, wo_ref[...])
        branches = ((a_ref, wa_ref, da_ref, ga_ref), (b_ref, wb_ref, db_ref, gb_ref),
                    (c_ref, wc_ref, dc_ref, gc_ref), (d_ref, wd_ref, dd_ref, gd_ref))
        for j, (br, w, dbr_ref, gw_ref) in enumerate(branches):
            gate = _sigmoid(lg_ref[:, j * D:(j + 1) * D].astype(F32))
            y = _dot(br[...], w[...])
            dlg_ref[:, j * D:(j + 1) * D] = (dm * y * gate * (1.0 - gate)).astype(BF16)
            dy = (dm * gate).astype(BF16)
            dbr_ref[...] = _dot_nt(dy, w[...]).astype(dbr_ref.dtype)
            for k in range(N_CHIPS):
                gw_ref[k] += _dot_tn(br[...], dy[:, k * Dc:(k + 1) * Dc])

        _ride_finish(riding, pl.program_id(0) == T // tm - 1)

    row = lambda w: pl.BlockSpec((tm, w), lambda i: (i, 0))
    wg = [_wgrad_out(l, DEPTH, r, c) for r, c in ((Dc, D), (MLA_W, Dc), (MIX_W, Dc), (MIX_W, Dc), (MIX_W, Dc))]
    n_in = 14
    ride_in, ride_out, ride_shapes, ride_scratch, ride_args, alias = _ride_specs(ride, n_in + len(keep), 11)
    out = pl.pallas_call(
        body, name="merge_bwd", grid=(T // tm,),
        in_specs=[row(D), row(D), row(D), row(4 * D), row(MLA_W), row(MIX_W), row(MIX_W), row(MIX_W),
                  _full((MLA_W, D)), _full((MIX_W, D)), _full((MIX_W, D)), _full((MIX_W, D)), _full((D, D)), _full((1, D))]
        + [pl.BlockSpec(memory_space=pl.ANY)] * len(keep) + ride_in,
        out_specs=[row(4 * D), row(MLA_W), row(MIX_W), row(MIX_W), row(MIX_W), _full((1, D))] + [s for s, _ in wg] + ride_out,
        out_shape=[jax.ShapeDtypeStruct((T, 4 * D), BF16), jax.ShapeDtypeStruct((T, MLA_W), BF16)]
        + [jax.ShapeDtypeStruct((T, MIX_W), F32)] * 3 + [jax.ShapeDtypeStruct((1, D), F32)] + [s for _, s in wg] + ride_shapes,
        input_output_aliases={**{n_in + i: 6 + i for i in range(len(keep))}, **alias},
        scratch_shapes=ride_scratch,
        compiler_params=_params(1),
    )(dx1, o, merged, P, A, Bm, C, Dv, wa, wb, wc, wd, wout, g_post, *keep, *ride_args)
    return out[:6], dict(zip(names, out[6:11])), list(out[11:])


def _mixers_bwd(P, D, S, dBm, dC, dDv, pool_bd, pool_scale, g_v, sgu_w, sgu_bias, conv_w):
    T = P.shape[0]
    tc = _tile(S, 512, SGU_BLOCK)
    n_si = S // tc
    cur, prev, nxt = _mix_specs(T, D, tc)
    n_blk = tc // SGU_BLOCK

    def body(z_ref, zp_ref, u_ref, v_ref, b_ref, c_ref, x_ref, cp_ref, xp_ref, bn_ref,
             dbm_ref, dbmn_ref, dc_ref, ddv_ref, ddvn_ref,
             pw_ref, ps_ref, gv_ref, sw_ref, sb_ref, cw_ref,
             dp_ref, dpw_ref, dps_ref, dgv_ref, dsw_ref, dsb_ref, dcw_ref, dvn_acc):
        si = pl.program_id(0) % n_si
        first, last = si == 0, si == n_si - 1

        @pl.when(pl.program_id(0) == 0)
        def _():
            for r in (dpw_ref, dps_ref, dgv_ref, dsw_ref, dsb_ref, dcw_ref):
                r[...] = jnp.zeros_like(r)

        z = z_ref[...].astype(F32)
        pooled = _pool_fwd(z, jnp.where(first, 0.0, zp_ref[...].astype(F32)), si, tc).astype(BF16)
        dbm = dbm_ref[...]
        dps_ref[...] += jnp.sum(dbm * _dot(pooled, pw_ref[...]), axis=0, keepdims=True)
        dmix = (jnp.concatenate([dbm, jnp.where(last, 0.0, dbmn_ref[...])], axis=0) * ps_ref[...]).astype(BF16)
        dpw_ref[...] += _dot_tn(pooled, dmix[:tc])
        dpool = _dot_nt(dmix, pw_ref[...])
        e = dpool / _pool_count(si, tc, tc + POOL_HALO)
        f1 = e + _shift_up(e, 1)
        f2 = f1 + _shift_up(f1, 2)
        f4 = f2 + _shift_up(f2, 4)
        f8 = f4 + _shift_up(f4, 8)
        dp_ref[:, 0:MIX_W] = (_by_group(f1, f2, f4, f8)[:tc] - dpool[:tc]).astype(BF16)

        v = v_ref[...].astype(F32)
        vn = (v * _rms_r(v) * gv_ref[...]).astype(BF16)
        keep, wm = _sgu_weights(sw_ref)
        g = _lane_group()
        for blk in range(n_blk):
            rows = slice(blk * SGU_BLOCK, (blk + 1) * SGU_BLOCK)
            vb = vn[rows]
            dc = dc_ref[rows, :]
            dp_ref[rows, MIX_W:2 * MIX_W] = (dc * _sgu_mix(vb, wm, sb_ref[...])).astype(BF16)
            dmx = dc * u_ref[rows, :].astype(F32)
            dsb_ref[...] += dmx
            dvn = jnp.zeros((SGU_BLOCK, MIX_W), F32)
            for k in range(GROUPS):
                dmk = jnp.where(g == k, dmx, 0.0).astype(BF16)
                dsw_ref[k] += jnp.where(keep, _dot_nt(dmk, vb), 0.0)
                dvn = dvn + _dot_tn(wm[k], dmk)
            dvn_acc[rows, :] = dvn
        dv, dg = _rms_bwd(dvn_acc[...], v, gv_ref[...])
        dgv_ref[...] += dg
        dp_ref[:, 2 * MIX_W:3 * MIX_W] = dv.astype(BF16)

        cg, xg, bg = c_ref[...].astype(F32), x_ref[...].astype(F32), b_ref[...].astype(F32)
        zc = cg * xg
        ze = jnp.concatenate([jnp.where(first, 0.0, cp_ref[...].astype(F32) * xp_ref[...].astype(F32)), zc], axis=0)
        z1, z2 = _shift_down(ze, 1)[CONV_HALO:], _shift_down(ze, 2)[CONV_HALO:]
        ddv = ddv_ref[...]
        y = cw_ref[0:1, :] * z2 + cw_ref[1:2, :] * z1 + cw_ref[2:3, :] * zc
        dp_ref[:, 3 * MIX_W:4 * MIX_W] = (ddv * y).astype(BF16)
        dy = ddv * bg
        dcw_ref[0:1, :] += jnp.sum(dy * z2, axis=0, keepdims=True)
        dcw_ref[1:2, :] += jnp.sum(dy * z1, axis=0, keepdims=True)
        dcw_ref[2:3, :] += jnp.sum(dy * zc, axis=0, keepdims=True)
        dye = jnp.concatenate([dy, jnp.where(last, 0.0, ddvn_ref[...] * bn_ref[...].astype(F32))], axis=0)
        dz = (cw_ref[2:3, :] * dye + cw_ref[1:2, :] * _shift_up(dye, 1) + cw_ref[0:1, :] * _shift_up(dye, 2))[:tc]
        dp_ref[:, 4 * MIX_W:5 * MIX_W] = (dz * xg).astype(BF16)
        dp_ref[:, 5 * MIX_W:6 * MIX_W] = (dz * cg).astype(BF16)

    grad = lambda halo: pl.BlockSpec((halo, MIX_W), lambda i: (jnp.minimum((i + 1) * (tc // halo), T // halo - 1), 0))
    out = pl.BlockSpec((tc, MIX_W), lambda i: (i, 0))
    return pl.pallas_call(
        body, name="mixers_bwd", grid=(T // tc,),
        in_specs=[cur(0), prev(0, POOL_HALO), cur(1), cur(2), cur(3), cur(4), cur(5), prev(4, CONV_HALO), prev(5, CONV_HALO),
                  nxt(3, CONV_HALO), out, grad(POOL_HALO), out, out, grad(CONV_HALO),
                  _full((MIX_W, MIX_W)), _full((1, MIX_W)), _full((1, MIX_W)), _full((GROUPS, SGU_BLOCK, SGU_BLOCK)),
                  _full((SGU_BLOCK, MIX_W)), _full((CONV_HALO, MIX_W))],
        out_specs=[pl.BlockSpec((tc, SEC_MIX), lambda i: (i, 0)), _full((MIX_W, MIX_W)), _full((1, MIX_W)), _full((1, MIX_W)),
                   _full((GROUPS, SGU_BLOCK, SGU_BLOCK)), _full((SGU_BLOCK, MIX_W)), _full((CONV_HALO, MIX_W))],
        out_shape=[jax.ShapeDtypeStruct((T, SEC_MIX), BF16), jax.ShapeDtypeStruct((MIX_W, MIX_W), F32),
                   jax.ShapeDtypeStruct((1, MIX_W), F32), jax.ShapeDtypeStruct((1, MIX_W), F32),
                   jax.ShapeDtypeStruct((GROUPS, SGU_BLOCK, SGU_BLOCK), F32), jax.ShapeDtypeStruct((SGU_BLOCK, MIX_W), F32),
                   jax.ShapeDtypeStruct((CONV_HALO, MIX_W), F32)],
        scratch_shapes=[pltpu.VMEM((tc, MIX_W), F32)],
        compiler_params=_params(1),
    )(P, P, P, P, P, P, P, P, P, P, dBm, dBm, dC, dDv, dDv, pool_bd, pool_scale, g_v, sgu_w, sgu_bias, conv_w)


def _attn_tile(S):
    return _tile(S, 256, 128)


def _attn_stats(O, dO, lse, S):
    T = O.shape[0]
    tq = _attn_tile(S)

    def body(o_ref, do_ref, lse_ref, lt_ref, dt_ref):
        lane = lax.broadcasted_iota(jnp.int32, (tq, HEAD_PAD), 1)
        delta = jnp.zeros((tq, HEAD_PAD), F32)
        for h in range(HEADS):
            hs = slice(h * HEAD_PAD, (h + 1) * HEAD_PAD)
            s = jnp.sum(o_ref[:, hs].astype(F32) * do_ref[:, hs].astype(F32), axis=-1, keepdims=True)
            delta = jnp.where(lane == h, s, delta)
        delta_t, lse_t = delta.T, lse_ref[...].T
        for h in range(HEADS):
            lt_ref[h, 0] = lse_t[h:h + 1, :]
            dt_ref[h, 0] = delta_t[h:h + 1, :]

    row = lambda w: pl.BlockSpec((tq, w), lambda i: (i, 0))
    out = pl.BlockSpec((HEADS, 1, 1, tq), lambda i: (0, i, 0, 0))
    return pl.pallas_call(
        body, name="attn_stats", grid=(T // tq,), in_specs=[row(MLA_W), row(MLA_W), row(HEAD_PAD)], out_specs=[out, out],
        out_shape=[jax.ShapeDtypeStruct((HEADS, T // tq, 1, tq), F32)] * 2, compiler_params=_params(1),
    )(O, dO, lse)


def _flash_bwd(Q, K, V, dO, lse_t, delta_t, S, ride=None):
    T = Q.shape[0]
    n_seq = T // S
    tq = _attn_tile(S)
    nq = S // tq

    def body(k_ref, v_ref, q_ref, do_ref, lse_ref, dl_ref, *rest):
        (dq_ref, dk_ref, dv_ref), (s_s, dp_s, p_s, ds_s), riding = _ride_split(ride, rest, 3, 4)
        kb = pl.program_id(1)
        _ride_start(riding, (pl.program_id(0) == 0) & (kb == 0))

        @pl.when(kb == 0)
        def _():
            dq_ref[...] = jnp.zeros_like(dq_ref)

        dk_ref[...] = jnp.zeros_like(dk_ref)
        dv_ref[...] = jnp.zeros_like(dv_ref)

        def block(qi, masked):
            rows = pl.ds(pl.multiple_of(qi * tq, tq), tq)
            for h in range(HEADS):
                hs = slice(h * HEAD_PAD, (h + 1) * HEAD_PAD)
                s_s[h] = _dot_nt(k_ref[:, hs], q_ref[rows, hs])
                dp_s[h] = _dot_nt(v_ref[:, hs], do_ref[rows, hs])
            for h in range(HEADS):
                lse_row, dl_row = lse_ref[h, qi], dl_ref[h, qi]
                for r in range(0, tq, SOFTMAX_ROWS):
                    rs = slice(r, r + SOFTMAX_ROWS)
                    s = s_s[h, rs, :]
                    if masked:
                        key = r + lax.broadcasted_iota(jnp.int32, (SOFTMAX_ROWS, tq), 0)
                        query = lax.broadcasted_iota(jnp.int32, (SOFTMAX_ROWS, tq), 1)
                        s = jnp.where((query >> CHUNK_SHIFT) >= (key >> CHUNK_SHIFT), s, NEG_INF)
                    p = jnp.exp2(s - lse_row)
                    p_s[h, rs, :] = p.astype(BF16)
                    ds_s[h, rs, :] = (p * (dp_s[h, rs, :] - dl_row)).astype(BF16)
            for h in range(HEADS):
                hs = slice(h * HEAD_PAD, (h + 1) * HEAD_PAD)
                dv_ref[:, hs] += _dot(p_s[h], do_ref[rows, hs])
                dk_ref[:, hs] += _dot(ds_s[h], q_ref[rows, hs])
                dq_ref[rows, hs] += _dot_tn(ds_s[h], k_ref[:, hs])

        def full_block(qi, carry):
            block(qi, False)
            return carry

        block(kb, True)
        lax.fori_loop(kb + 1, nq, full_block, 0)
        dk_ref[...] = dk_ref[...] * LN2
        _ride_finish(riding, (pl.program_id(0) == n_seq - 1) & (kb == nq - 1))

    tile = pl.BlockSpec((tq, MLA_W), lambda b, i: (b * nq + i, 0))
    seq = pl.BlockSpec((S, MLA_W), lambda b, i: (b, 0))
    stat = pl.BlockSpec((HEADS, nq, 1, tq), lambda b, i: (0, b, 0, 0))
    ride_in, ride_out, ride_shapes, ride_scratch, ride_args, alias = _ride_specs(ride, 6, 3)
    out = pl.pallas_call(
        body, name="flash_bwd", grid=(n_seq, nq),
        in_specs=[tile, tile, seq, seq, stat, stat] + ride_in,
        out_specs=[seq, tile, tile] + ride_out,
        out_shape=[jax.ShapeDtypeStruct((T, MLA_W), F32)] * 3 + ride_shapes,
        input_output_aliases=alias,
        scratch_shapes=[pltpu.VMEM((HEADS, tq, tq), F32), pltpu.VMEM((HEADS, tq, tq), F32),
                        pltpu.VMEM((HEADS, tq, tq), BF16), pltpu.VMEM((HEADS, tq, tq), BF16)] + ride_scratch,
        compiler_params=_params(2),
    )(K, V, Q, dO, lse_t, delta_t, *ride_args)
    return out[0], out[1], out[2], list(out[3:])


def _mla_bwd_post(P, D, S, dQ, dK, dV, hq, hkv, g_cq, g_ckv, wuq, wukv, rope_c, rope_sa, rope_sb, l, bufs):
    T = P.shape[0]
    tm = _tile(S, 512)
    n_si = S // tm
    base = 4 * D
    names = ("w_uq", "w_ukv")

    def body(cq_ref, ckv_ref, dq_ref, dk_ref, dv_ref, hq_ref, hkv_ref, gq_ref, gkv_ref, wq_ref, wkv_ref, c_ref, sa_ref, sb_ref,
             *rest):
        dp_ref, dgq_ref, dgkv_ref, guq_ref, gukv_ref = rest[-5:]

        @pl.when(pl.program_id(0) == 0)
        def _():
            for r in (dgq_ref, dgkv_ref, guq_ref, gukv_ref):
                r[...] = jnp.zeros_like(r)

        c, sa, sb = c_ref[...], sa_ref[...], sb_ref[...]
        dq = _rope_t(dq_ref[...] * ATTN_SCALE, jnp.tile(c, (1, HEADS)), jnp.tile(sa, (1, HEADS)),
                     jnp.tile(sb, (1, HEADS))).astype(BF16)
        dcq, dg = _rms_bwd(_dot_nt(dq, wq_ref[...]), cq_ref[...].astype(F32), gq_ref[...])
        dgq_ref[...] += dg
        dp_ref[:, 0:Q_LORA] = dcq.astype(BF16)

        dk = dk_ref[...]
        dkb, dvb = dk.astype(BF16), dv_ref[...].astype(BF16)
        dkv = jnp.concatenate([p[:, j * CHIP_HEADS_W:(j + 1) * CHIP_HEADS_W] for j in range(N_CHIPS) for p in (dkb, dvb)], axis=1)
        for k in range(N_CHIPS):
            guq_ref[k] += _dot_tn(hq_ref[...], dq[:, k * CHIP_HEADS_W:(k + 1) * CHIP_HEADS_W])
            gukv_ref[k] += _dot_tn(hkv_ref[...], dkv[:, k * CHIP_KV:(k + 1) * CHIP_KV])
        dckv, dg = _rms_bwd(_dot_nt(dkv, wkv_ref[...]), ckv_ref[...].astype(F32), gkv_ref[...])
        dgkv_ref[...] += dg
        dp_ref[:, Q_LORA:Q_LORA + KV_LORA] = dckv.astype(BF16)

        dkr = dk[:, 0:HEAD_PAD]
        for h in range(1, HEADS):
            dkr = dkr + dk[:, h * HEAD_PAD:(h + 1) * HEAD_PAD]
        lane = lax.broadcasted_iota(jnp.int32, (1, HEAD_PAD), 1)
        rope_lanes = (lane >= QK_NOPE) & (lane < QK_NOPE + QK_ROPE)
        dp_ref[:, Q_LORA + KV_LORA:SEC_MLA] = jnp.where(rope_lanes, _rope_t(dkr, c, sa, sb), 0.0).astype(BF16)

    tab = pl.BlockSpec((tm, HEAD_PAD), lambda i: (i % n_si, 0))
    row = lambda w: pl.BlockSpec((tm, w), lambda i: (i, 0))
    wg = [_wgrad_out(l, DEPTH, Q_LORA, CHIP_HEADS_W), _wgrad_out(l, DEPTH, KV_LORA, CHIP_KV)]
    keep = [] if bufs is None else [bufs[n] for n in names]
    n_in = 14
    out = pl.pallas_call(
        body, name="mla_bwd_post", grid=(T // tm,),
        in_specs=[pl.BlockSpec((tm, Q_LORA), lambda i: (i, base // Q_LORA)),
                  pl.BlockSpec((tm, KV_LORA), lambda i: (i, (base + Q_LORA) // KV_LORA)),
                  row(MLA_W), row(MLA_W), row(MLA_W), row(Q_LORA), row(KV_LORA),
                  _full((1, Q_LORA)), _full((1, KV_LORA)), _full((Q_LORA, MLA_W)), _full((KV_LORA, 2 * MLA_W)), tab, tab, tab]
        + [pl.BlockSpec(memory_space=pl.ANY)] * len(keep),
        out_specs=[row(SEC_MLA), _full((1, Q_LORA)), _full((1, KV_LORA))] + [s for s, _ in wg],
        out_shape=[jax.ShapeDtypeStruct((T, SEC_MLA), BF16), jax.ShapeDtypeStruct((1, Q_LORA), F32),
                   jax.ShapeDtypeStruct((1, KV_LORA), F32)] + [s for _, s in wg],
        input_output_aliases={n_in + i: 3 + i for i in range(len(keep))},
        compiler_params=_params(1),
    )(P, P, dQ, dK, dV, hq, hkv, g_cq, g_ckv, wuq, wukv, rope_c, rope_sa, rope_sb, *keep)
    return out[:3], dict(zip(names, out[3:]))


def _proj_bwd(dx1, x, g, dPg, dPa, dPm, w_gates, w_mla, w_mix, ride=None):
    T, D = x.shape
    tm = _tile(T, 256)

    def body(dx1_ref, x_ref, g_ref, dg_ref_in, da_ref, dm_ref, wg_ref, wa_ref, wm_ref, *rest):
        (dx_ref, dg_ref), _, riding = _ride_split(ride, rest, 2, 0)
        _ride_start(riding, pl.program_id(0) == 0)

        @pl.when(pl.program_id(0) == 0)
        def _():
            dg_ref[...] = jnp.zeros_like(dg_ref)

        dh = _dot_nt(dg_ref_in[...], wg_ref[...]) + _dot_nt(da_ref[...], wa_ref[...]) + _dot_nt(dm_ref[...], wm_ref[...])
        dx, dg = _rms_bwd(dh, x_ref[...], g_ref[...])
        dx_ref[...] = dx1_ref[...] + dx
        dg_ref[...] += dg
        _ride_finish(riding, pl.program_id(0) == T // tm - 1)

    row = lambda w: pl.BlockSpec((tm, w), lambda i: (i, 0))
    ride_in, ride_out, ride_shapes, ride_scratch, ride_args, alias = _ride_specs(ride, 9, 2)
    out = pl.pallas_call(
        body, name="proj_bwd", grid=(T // tm,),
        in_specs=[row(D), row(D), _full((1, D)), row(4 * D), row(SEC_MLA), row(SEC_MIX),
                  _full((D, 4 * D)), _full((D, SEC_MLA)), _full((D, SEC_MIX))] + ride_in,
        out_specs=[row(D), _full((1, D))] + ride_out,
        out_shape=[jax.ShapeDtypeStruct((T, D), F32), jax.ShapeDtypeStruct((1, D), F32)] + ride_shapes,
        input_output_aliases=alias, scratch_shapes=ride_scratch,
        compiler_params=_params(1),
    )(dx1, x, g, dPg, dPa, dPm, w_gates, w_mla, w_mix, *ride_args)
    return out[0], out[1], list(out[2:])


def _adamw(w, g, m, v, name):
    R, C = w.shape
    tr = _tile(R, max(8, (1 << 19) // C))

    def body(w_ref, g_ref, m_ref, v_ref, d_ref, mo_ref, vo_ref):
        gv = g_ref[...]
        mn = ADAM_B1 * m_ref[...] + (1.0 - ADAM_B1) * gv
        vn = ADAM_B2 * v_ref[...] + (1.0 - ADAM_B2) * (gv * gv)
        mo_ref[...] = mn
        vo_ref[...] = vn
        m_hat = mn / (1.0 - ADAM_B1 ** ADAM_STEP)
        v_hat = vn / (1.0 - ADAM_B2 ** ADAM_STEP)
        d_ref[...] = -ADAM_LR * (m_hat / (jnp.sqrt(v_hat) + ADAM_EPS) + ADAM_WD * w_ref[...])

    blk = pl.BlockSpec((tr, C), lambda i: (i, 0))
    return pl.pallas_call(
        body, name=name, grid=(R // tr,), in_specs=[blk] * 4, out_specs=[blk] * 3,
        out_shape=[jax.ShapeDtypeStruct((R, C), F32)] * 3, compiler_params=_params(1),
    )(w, g, m, v)


def _rows_tile(rows, cols):
    return _tile(rows, max(16, (1 << 19) // cols), 16)


def _add_halves(G, span, recv, half, name):
    n, L, R, C = G.shape
    l0, nl = span[0], span[1] - span[0]
    hr = R // 2
    tr = _rows_tile(hr, C)
    nb = hr // tr

    def body(half_ref, g_ref, r_ref, o_ref):
        o_ref[...] = (g_ref[...] + r_ref[...]).astype(BF16)

    grid_spec = pltpu.PrefetchScalarGridSpec(
        num_scalar_prefetch=1, grid=(n, nl, nb),
        in_specs=[pl.BlockSpec((None, None, tr, C), lambda k, l, i, h: (k, l0 + l, h[0] * nb + i, 0)),
                  pl.BlockSpec((None, None, tr, C), lambda k, l, i, h: (k, l, i, 0))],
        out_specs=pl.BlockSpec((None, None, tr, C), lambda k, l, i, h: (k, l, i, 0)))
    return pl.pallas_call(
        body, name="rs_add_halves_" + name, grid_spec=grid_spec,
        out_shape=jax.ShapeDtypeStruct((n, nl, hr, C), BF16), compiler_params=_params(3),
    )(half.reshape(1).astype(jnp.int32), G, recv)


def _sum_slots(H, slots, place, name, l0, n_layers, prev=None):
    _, nl, hr, C = slots.shape
    tr = _rows_tile(hr, C)
    nb = hr // tr

    def body(x_ref, y_ref, c_ref, own_ref, s1_ref, s2_ref, s3_ref, *rest):
        o_ref = rest[-1]
        o_ref[...] = ((own_ref[...].astype(F32) + s1_ref[...].astype(F32)) + s2_ref[...].astype(F32)) + s3_ref[...].astype(F32)

    def src(fx, fy):
        def index(l, j, px, py, pc):
            cx = px[0] + fx - 2 * fx * px[0]
            cy = py[0] + fy - 2 * fy * py[0]
            return (2 * cx + cy, l, j, 0)
        return pl.BlockSpec((None, None, tr, C), index)

    keep = [] if prev is None else [prev]
    grid_spec = pltpu.PrefetchScalarGridSpec(
        num_scalar_prefetch=3, grid=(nl, nb),
        in_specs=[src(0, 0), src(0, 1), src(1, 0), src(1, 1)] + [pl.BlockSpec(memory_space=pl.ANY)] * len(keep),
        out_specs=pl.BlockSpec((None, tr, C), lambda l, j, px, py, pc: (l0 + l, pc[0] * nb + j, 0)))
    return pl.pallas_call(
        body, name="rs_sum_slots_" + name, grid_spec=grid_spec,
        out_shape=jax.ShapeDtypeStruct((n_layers, 2 * hr, C), F32),
        input_output_aliases={7: 0} if keep else {}, compiler_params=_params(2),
    )(*place, H, slots, slots, slots, *keep)


HBM = pl.BlockSpec(memory_space=pltpu.HBM)


def _place():
    x, y, c = lax.axis_index("x"), lax.axis_index("y"), lax.axis_index("c")
    return x, y, c, 2 * x + y


def _chip_device(chip, c):
    return (chip // 2, chip % 2, c)


def _remote(src, dst, send_sem, recv_sem, to):
    return pltpu.make_async_remote_copy(src_ref=src, dst_ref=dst, send_sem=send_sem, recv_sem=recv_sem, device_id=to,
                                        device_id_type=MESH)


def _place_own(w, chip, name):
    L, R, C = w.shape
    tr = _rows_tile(R, C)

    def body(p_ref, w_ref, o_ref):
        o_ref[...] = w_ref[...].astype(BF16)

    grid_spec = pltpu.PrefetchScalarGridSpec(
        num_scalar_prefetch=1, grid=(L, R // tr), in_specs=[pl.BlockSpec((None, tr, C), lambda l, j, p: (l, j, 0))],
        out_specs=pl.BlockSpec((None, None, tr, C), lambda l, j, p: (l, p[0], j, 0)))
    return pl.pallas_call(
        body, name="place_" + name, grid_spec=grid_spec,
        out_shape=jax.ShapeDtypeStruct((L, N_CHIPS, R, C), BF16), compiler_params=_params(2),
    )(chip.reshape(1).astype(jnp.int32), w)


def _gather_weights(bufs, l):
    n = len(bufs)

    def body(*refs):
        o_refs = refs[n:2 * n]
        send_sems, recv_sems = refs[2 * n:]
        x, y, c, me = _place()
        sibling = (x, y, 1 - c)

        def copy(t, k, chip, half, to):
            hr = o_refs[t].shape[2] // 2
            block = o_refs[t].at[l, chip, pl.ds(half * hr, hr), :]
            return _remote(block, block, send_sems.at[6 * t + k], recv_sems.at[6 * t + k], to)

        first = [copy(t, d - 1, me, c, _chip_device(me ^ d, c)) for t in range(n) for d in (1, 2, 3)]
        for cp in first:
            cp.start()
        passed = []
        for t in range(n):
            for d in (1, 2, 3):
                copy(t, d - 1, me ^ d, c, sibling).wait_recv()
                passed.append(copy(t, 2 + d, me ^ d, c, sibling))
                passed[-1].start()
        for t in range(n):
            for d in (1, 2, 3):
                copy(t, 2 + d, me ^ d, 1 - c, sibling).wait_recv()
        for cp in first + passed:
            cp.wait_send()

    return pl.pallas_call(
        body, name="gather_weights", in_specs=[HBM] * n, out_specs=[HBM] * n,
        out_shape=[jax.ShapeDtypeStruct(b.shape, b.dtype) for b in bufs],
        input_output_aliases={t: t for t in range(n)},
        scratch_shapes=[pltpu.SemaphoreType.DMA((6 * n,)), pltpu.SemaphoreType.DMA((6 * n,))],
    )(*bufs)


def _join_halves(bufs):
    n = len(bufs)

    def body(*refs):
        o_refs = refs[n:2 * n]
        send_sems, recv_sems = refs[2 * n:]
        x, y, c, _ = _place()

        def half(t, which):
            hr = o_refs[t].shape[1] // 2
            return o_refs[t].at[:, pl.ds(which * hr, hr), :]

        sends = [_remote(half(t, c), half(t, c), send_sems.at[t], recv_sems.at[t], (x, y, 1 - c)) for t in range(n)]
        for cp in sends:
            cp.start()
        for t in range(n):
            _remote(half(t, 1 - c), half(t, 1 - c), send_sems.at[t], recv_sems.at[t], (x, y, 1 - c)).wait_recv()
        for cp in sends:
            cp.wait_send()

    return pl.pallas_call(
        body, name="rs_join_halves", in_specs=[HBM] * n, out_specs=[HBM] * n,
        out_shape=[jax.ShapeDtypeStruct(b.shape, b.dtype) for b in bufs],
        input_output_aliases={t: t for t in range(n)},
        scratch_shapes=[pltpu.SemaphoreType.DMA((n,)), pltpu.SemaphoreType.DMA((n,))],
    )(*bufs)


def _all_reduce_small(v, name):
    R, C = v.shape

    def body(v_ref, o_ref, slots, send_sems, recv_sems):
        x, y, c, _ = _place()
        me = 4 * x + 2 * y + c
        slots[me] = v_ref[...]
        sends = []
        for d in range(1, 8):
            peer = me ^ d
            sends.append(pltpu.make_async_remote_copy(
                src_ref=v_ref, dst_ref=slots.at[me], send_sem=send_sems.at[d - 1], recv_sem=recv_sems.at[d - 1],
                device_id=(peer // 4, (peer // 2) % 2, peer % 2), device_id_type=MESH))
        for cp in sends:
            cp.start()
        for d in range(1, 8):
            peer = me ^ d
            pltpu.make_async_remote_copy(
                src_ref=v_ref, dst_ref=slots.at[peer], send_sem=send_sems.at[d - 1], recv_sem=recv_sems.at[d - 1],
                device_id=(peer // 4, (peer // 2) % 2, peer % 2), device_id_type=MESH).wait_recv()
        for cp in sends:
            cp.wait_send()
        acc = slots[0]
        for k in range(1, 8):
            acc = acc + slots[k]
        o_ref[...] = acc

    vm = pl.BlockSpec(memory_space=pltpu.VMEM)
    return pl.pallas_call(
        body, name=name, in_specs=[vm], out_specs=vm, out_shape=jax.ShapeDtypeStruct((R, C), F32),
        scratch_shapes=[pltpu.VMEM((8, R, C), F32), pltpu.SemaphoreType.DMA((7,)), pltpu.SemaphoreType.DMA((7,))],
    )(v)


SHARDED = ("w_in", "w_uq", "w_ukv", "conv_w", "w_br_a", "w_br_b", "w_br_c", "w_br_d", "w_out", "w_ffn_gate", "w_ffn_up",
           "w_ffn_down")
ROW_SHARDED = ("w_out", "w_ffn_down")
REPLICATED = ("g_pre_mix", "g_cq", "g_ckv", "pool_w", "pool_scale", "g_sgu_v", "sgu_w", "sgu_b", "g_post_mix", "g_pre_ffn",
              "g_post_ffn")
WEIGHTS = ("w_in", "g_pre_mix", "g_cq", "g_ckv", "w_uq", "w_ukv", "pool_w", "pool_scale", "g_sgu_v", "sgu_w", "sgu_b",
           "conv_w", "w_br_a", "w_br_b", "w_br_c", "w_br_d", "w_out", "g_post_mix", "g_pre_ffn", "w_ffn_gate", "w_ffn_up",
           "w_ffn_down", "g_post_ffn")
GATHERED = tuple(n for n in SHARDED if n != "conv_w")


def _unpack(packed, shapes):
    flat = packed.reshape(-1)
    out, o = [], 0
    for s in shapes:
        n = int(np.prod(s))
        out.append(flat[o:o + n].reshape(s))
        o += n
    return out


def _join_cols(g, l):
    return jnp.concatenate([g[l, k] for k in range(N_CHIPS)], axis=1)


def _pad_heads(w, real):
    lead = w.shape[:-1]
    w = w.reshape(lead + (HEADS, real))
    return jnp.pad(w, [(0, 0)] * len(lead) + [(0, 0), (0, HEAD_PAD - real)]).reshape(lead + (MLA_W,))


def _unpad_heads(w, real):
    lead = w.shape[:-1]
    return w.reshape(lead + (HEADS, HEAD_PAD))[..., :real].reshape(lead + (HEADS * real,))


IN_OFFSETS = {"cq": 0, "ckv": Q_LORA, "kr": Q_LORA + KV_LORA, "mix": Q_LORA + KV_LORA + QK_ROPE}
IN_GATES = Q_LORA + KV_LORA + QK_ROPE + SEC_MIX


def _pad_w_in(w):
    K = w.shape[0]
    z = lambda n: jnp.zeros((K, n), w.dtype)
    return jnp.concatenate([w[:, IN_GATES:], w[:, :IN_OFFSETS["kr"]], z(QK_NOPE), w[:, IN_OFFSETS["kr"]:IN_OFFSETS["mix"]],
                            z(HEAD_PAD - QK_NOPE - QK_ROPE), w[:, IN_OFFSETS["mix"]:IN_GATES]], axis=1)


def _unpad_w_in(d_gates, d_mla, d_mix):
    kr = d_mla[:, Q_LORA + KV_LORA + QK_NOPE:Q_LORA + KV_LORA + QK_NOPE + QK_ROPE]
    return jnp.concatenate([d_mla[:, :Q_LORA + KV_LORA], kr, d_mix, d_gates], axis=1)


def _rope_tables(S):
    half = QK_ROPE // 2
    inv = ROPE_THETA ** (-jnp.arange(0, QK_ROPE, 2, dtype=F32) / QK_ROPE)
    ang = jnp.arange(S, dtype=F32)[:, None] * inv[None, :]
    cos, sin = jnp.cos(ang), jnp.sin(ang)
    one, zero = jnp.ones((S, QK_NOPE), F32), jnp.zeros((S, half), F32)
    tail = HEAD_PAD - QK_NOPE - QK_ROPE
    c = jnp.concatenate([one, cos, cos, jnp.ones((S, tail), F32)], axis=1)
    sa = jnp.concatenate([0 * one, zero, sin, jnp.zeros((S, tail), F32)], axis=1)
    sb = jnp.concatenate([0 * one, -sin, zero, jnp.zeros((S, tail), F32)], axis=1)
    return c, sa, sb


def _layer_weights(gathered, full, l, D):
    w = {}
    w_in = _pad_w_in(_join_cols(gathered["w_in"], l))
    w["w_in"] = w_in
    w["w_in_gates"], w["w_in_mla"], w["w_in_mix"] = w_in[:, :4 * D], w_in[:, 4 * D:4 * D + SEC_MLA], w_in[:, 4 * D + SEC_MLA:]
    w["w_uq"] = _pad_heads(_join_cols(gathered["w_uq"], l), QK_NOPE + QK_ROPE)
    ukv = _join_cols(gathered["w_ukv"], l).reshape(KV_LORA, HEADS, QK_NOPE + V_HEAD)
    pad = ((0, 0), (0, 0), (0, HEAD_PAD - QK_NOPE))
    k_pad = jnp.pad(ukv[:, :, :QK_NOPE], pad).reshape(KV_LORA, N_CHIPS, CHIP_HEADS_W)
    v_pad = jnp.pad(ukv[:, :, QK_NOPE:], pad).reshape(KV_LORA, N_CHIPS, CHIP_HEADS_W)
    w["w_ukv"] = jnp.concatenate([k_pad, v_pad], axis=2).reshape(KV_LORA, 2 * MLA_W)
    w["w_br_a"] = jnp.pad(_join_cols(gathered["w_br_a"], l).reshape(HEADS, V_HEAD, D),
                          ((0, 0), (0, HEAD_PAD - V_HEAD), (0, 0))).reshape(MLA_W, D)
    for n in ("w_br_b", "w_br_c", "w_br_d"):
        w[n] = _join_cols(gathered[n], l)
    w["w_out"] = gathered["w_out"][l].reshape(D, D)
    for n in ("g_pre_mix", "g_cq", "g_ckv", "pool_scale", "g_sgu_v", "g_post_mix", "g_pre_ffn", "g_post_ffn"):
        w[n] = full[n][l].reshape(1, -1)
    pw = full["pool_w"][l]
    w["pool_bd"] = jax.scipy.linalg.block_diag(*[pw[g] for g in range(GROUPS)]).astype(BF16)
    w["sgu_w"] = full["sgu_w"][l]
    w["sgu_bias"] = jnp.repeat(full["sgu_b"][l].T, GROUP_DIM, axis=1)
    w["conv_w"] = jnp.pad(full["conv_w"][l].reshape(3, MIX_W), ((0, CONV_HALO - 3), (0, 0)))
    return w


def _layer_fwd(x, w, gathered, l, S, rope, gather_next):
    D = x.shape[1]
    bufs = [gathered[n] for n in GATHERED]
    P, h = _norm_matmul(x, w["g_pre_mix"], w["w_in"], "proj_fwd")
    Q, K, V, hq, hkv = _mla_prep(P, D, w["g_cq"], w["g_ckv"], w["w_uq"], w["w_ukv"], *rope, S)
    A, lse, bufs = _flash_fwd(Q, K, V, S, _gather_ride(bufs, l + 1, "chips") if gather_next else None)
    Bm, C, Dv, bufs = _mixers_fwd(P, D, S, w["pool_bd"], w["pool_scale"], w["g_sgu_v"], w["sgu_w"], w["sgu_bias"], w["conv_w"],
                                  _gather_ride(bufs, l + 1, "cores") if gather_next else None)
    if gather_next:
        gathered = dict(zip(GATHERED, bufs))
    x1, merged, o = _merge_fwd(x, P, A, Bm, C, Dv, w["w_br_a"], w["w_br_b"], w["w_br_c"], w["w_br_d"], w["w_out"], w["g_post_mix"])
    x2, h2, gt, up, f = _ffn_fwd(x1, w["g_pre_ffn"], gathered["w_ffn_gate"], gathered["w_ffn_up"], gathered["w_ffn_down"],
                                 w["g_post_ffn"], l)
    saved = dict(x=x, P=P, h=h, Q=Q, K=K, V=V, hq=hq, hkv=hkv, A=A, lse=lse, Bm=Bm, C=C, Dv=Dv, x1=x1, merged=merged, o=o,
                 h2=h2, gt=gt, up=up, f=f)
    return x2, saved, gathered


RIDE_SETS = (("w_ffn_gate", "w_out", "w_br_a"), ("w_in", "w_br_b", "w_br_c", "w_br_d", "w_uq", "w_ukv"),
             ("w_ffn_up", "w_ffn_down"))


def _layer_bwd(dx2, w, gathered, l, s, S, rope, bufs, early=None, send_ffn=False):
    D = dx2.shape[1]
    g = {}
    others = [n for n in GATHERED if n not in FFN_WEIGHTS]

    def scatter(k):
        return _scatter_ride([sums[n] for n in RIDE_SETS[k]]) if early else None

    sums, slots = {}, {}
    if early:
        done, spans, half, got_ffn = early
        ride = _exchange_ride([done[n] for n in others], [spans[n] for n in others])
    df, g["g_post_ffn"] = _norm_bwd(dx2, s["f"], w["g_post_ffn"], "ffn_bwd_pre")
    dh_parts, filled, got = _ffn_bwd_main(df, s["gt"], s["up"], s["h2"], gathered["w_ffn_gate"], gathered["w_ffn_up"],
                                          gathered["w_ffn_down"], l, bufs if "w_ffn_gate" in bufs else None,
                                          ride if early else None)
    bufs.update(filled)
    dx1, g["g_pre_ffn"] = _norm_bwd(None, s["x1"], w["g_pre_ffn"], "ffn_bwd_post", add=dx2, parts=dh_parts)
    if early:
        sums = {n: _add_halves(done[n], spans[n], r, half, n) for n, r in zip(others, got)}
        sums.update({n: _add_halves(filled[n], spans[n], got_ffn[n], half, n) for n in FFN_WEIGHTS})
        sums, dx1 = lax.optimization_barrier((sums, dx1))

    (dPg, dA, dBm, dC, dDv, g["g_post_mix"]), filled, got = _merge_bwd(
        dx1, s["o"], s["merged"], s["P"], s["A"], s["Bm"], s["C"], s["Dv"], w["w_br_a"], w["w_br_b"], w["w_br_c"], w["w_br_d"],
        w["w_out"], w["g_post_mix"], l, bufs if "w_out" in bufs else None, scatter(0))
    bufs.update(filled)
    slots.update(zip(RIDE_SETS[0], got))

    dPm, d_pool_bd, g_ps, g_gv, g["sgu_w"], d_bias, d_cw = _mixers_bwd(
        s["P"], D, S, dBm, dC, dDv, w["pool_bd"], w["pool_scale"], w["g_sgu_v"], w["sgu_w"], w["sgu_bias"], w["conv_w"])
    g["pool_w"] = jnp.stack([d_pool_bd[k * GROUP_DIM:(k + 1) * GROUP_DIM, k * GROUP_DIM:(k + 1) * GROUP_DIM] for k in range(GROUPS)])
    g["pool_scale"], g["g_sgu_v"] = g_ps, g_gv
    g["sgu_b"] = d_bias.reshape(SGU_BLOCK, GROUPS, GROUP_DIM).sum(-1).T
    g["conv_w"] = d_cw[:3].reshape(3, 1, MIX_W)

    lse_t, delta_t = _attn_stats(s["A"], dA, s["lse"], S)
    dQ, dK, dV, got = _flash_bwd(s["Q"], s["K"], s["V"], dA, lse_t, delta_t, S, scatter(1))
    slots.update(zip(RIDE_SETS[1], got))
    (dPa, g["g_cq"], g["g_ckv"]), filled = _mla_bwd_post(
        s["P"], D, S, dQ, dK, dV, s["hq"], s["hkv"], w["g_cq"], w["g_ckv"], w["w_uq"], w["w_ukv"], *rope, l,
        bufs if "w_uq" in bufs else None)
    bufs.update(filled)

    ride = scatter(2)
    if send_ffn:
        ride = _exchange_ride([bufs[n] for n in FFN_WEIGHTS], [(l, DEPTH)] * len(FFN_WEIGHTS))
    dx, g["g_pre_mix"], got = _proj_bwd(dx1, s["x"], w["g_pre_mix"], dPg, dPa, dPm, w["w_in_gates"], w["w_in_mla"],
                                        w["w_in_mix"], ride)
    if send_ffn:
        slots = dict(zip(FFN_WEIGHTS, got))
    else:
        slots.update(zip(RIDE_SETS[2], got))
    d_w_in = _unpad_w_in(_matmul_tn(s["h"], dPg, "wgrad_in_gates"), _matmul_tn(s["h"], dPa, "wgrad_in_mla"),
                         _matmul_tn(s["h"], dPm, "wgrad_in_mix"))
    g["w_in"] = d_w_in.reshape(D, N_CHIPS, -1).transpose(1, 0, 2)
    for n in ("g_pre_mix", "g_cq", "g_ckv", "pool_scale", "g_sgu_v", "g_post_mix", "g_pre_ffn", "g_post_ffn"):
        g[n] = g[n].reshape(-1)
    return dx, g, (sums, slots)


SMALL = REPLICATED + ("conv_w",)


def _local_step(x, target, gathered, full, core):
    n_seq, S, D = x.shape
    rope = _rope_tables(S)
    xs = x.reshape(n_seq * S, D)
    weights, saved = [], []
    for l in range(DEPTH):
        w = _layer_weights(gathered, full, l, D)
        gather_next = l + 1 < DEPTH
        if gather_next:
            w, gathered = lax.optimization_barrier((w, gathered))
        xs, s, gathered = _layer_fwd(xs, w, gathered, l, S, rope, gather_next)
        weights.append(w)
        saved.append(s)
    loss_parts, dx = _loss_grad(xs, target.reshape(n_seq * S, D))
    grads, bufs = [None] * DEPTH, {}
    for l in reversed(range(1, DEPTH)):
        dx, grads[l], (_, got_ffn) = _layer_bwd(dx, weights[l], gathered, l, saved[l], S, rope, bufs, send_ffn=l == 1)
    done = dict(bufs, w_in=jnp.stack([grads[l]["w_in"] for l in range(1, DEPTH)], axis=1))
    spans = dict({n: (1, DEPTH) for n in GATHERED}, w_in=(0, DEPTH - 1))
    dx, grads[0], early = _layer_bwd(dx, weights[0], gathered, 0, saved[0], S, rope, bufs, (done, spans, core, got_ffn))
    last = dict(bufs, w_in=grads[0]["w_in"][:, None])
    small = {n: jnp.stack([grads[l][n] for l in range(DEPTH)]) for n in SMALL}
    return loss_parts, dx.reshape(n_seq, S, D), early, last, small


def _unpad_reduced(n, r):
    L = r.shape[0]
    if n == "w_uq":
        return r.reshape(L, Q_LORA, 2, HEAD_PAD)[..., :QK_NOPE + QK_ROPE].reshape(L, Q_LORA, -1)
    if n == "w_ukv":
        r = r.reshape(L, KV_LORA, 2, 2, HEAD_PAD)[..., :QK_NOPE]
        return jnp.concatenate([r[:, :, 0], r[:, :, 1]], axis=-1).reshape(L, KV_LORA, -1)
    if n == "w_br_a":
        return r.reshape(L, HEADS, HEAD_PAD, -1)[:, :, :V_HEAD].reshape(L, HEADS * V_HEAD, -1)
    return r


def _small_rows(n):
    return -(-n // (8 * 128)) * 8


def _to_small(parts):
    flat = jnp.concatenate([p.reshape(-1) for p in parts])
    rows = _small_rows(flat.shape[0])
    return jnp.pad(flat, (0, rows * 128 - flat.shape[0])).reshape(rows, 128)


def kernel(x, w_in, g_pre_mix, g_cq, g_ckv, w_uq, w_ukv, pool_w, pool_scale, g_sgu_v, sgu_w, sgu_b, conv_w, w_br_a, w_br_b, w_br_c, w_br_d, w_out, g_post_mix, g_pre_ffn, w_ffn_gate, w_ffn_up, w_ffn_down, g_post_ffn, loss_target, m_w_in, m_g_pre_mix, m_g_cq, m_g_ckv, m_w_uq, m_w_ukv, m_pool_w, m_pool_scale, m_g_sgu_v, m_sgu_w, m_sgu_b, m_conv_w, m_w_br_a, m_w_br_b, m_w_br_c, m_w_br_d, m_w_out, m_g_post_mix, m_g_pre_ffn, m_w_ffn_gate, m_w_ffn_up, m_w_ffn_down, m_g_post_ffn, v_w_in, v_g_pre_mix, v_g_cq, v_g_ckv, v_w_uq, v_w_ukv, v_pool_w, v_pool_scale, v_g_sgu_v, v_sgu_w, v_sgu_b, v_conv_w, v_w_br_a, v_w_br_b, v_w_br_c, v_w_br_d, v_w_out, v_g_post_mix, v_g_pre_ffn, v_w_ffn_gate, v_w_ffn_up, v_w_ffn_down, v_g_post_ffn):
    local = dict(locals())
    W = {n: local[n] for n in WEIGHTS}
    M = {n: local["m_" + n] for n in WEIGHTS}
    V = {n: local["v_" + n] for n in WEIGHTS}
    chip = 2 * lax.axis_index("x") + lax.axis_index("y")
    core = lax.axis_index("c")

    gathered = dict(zip(GATHERED, _gather_weights([_place_own(W[n], chip, n) for n in GATHERED], 0)))
    conv_shape = conv_w.shape
    conv_cols = conv_shape[-1]
    conv_full_shape = conv_shape[:-1] + (N_CHIPS * conv_cols,)
    placed = lax.dynamic_update_slice(jnp.zeros(conv_full_shape, F32), conv_w, (0, 0, 0, chip * conv_cols))
    n_conv = int(np.prod(conv_full_shape))
    conv_sum = _all_reduce_small(_to_small([placed]), "gather_conv_w")
    full = {n: W[n] for n in REPLICATED}
    full["conv_w"] = 0.5 * conv_sum.reshape(-1)[:n_conv].reshape(conv_full_shape)

    loss_parts, grad_x, (sums_up, slots_up), last, small = _local_step(x, loss_target, gathered, full, core)
    loss = lax.psum(jnp.sum(loss_parts), ("x", "y", "c"))

    small_sum = _all_reduce_small(_to_small([small[n] for n in SMALL]), "reduce_small_grads")
    small_grads = dict(zip(SMALL, _unpack(small_sum, [small[n].shape for n in SMALL])))
    small_grads["conv_w"] = lax.dynamic_slice(small_grads["conv_w"], (0, 0, 0, chip * conv_cols), conv_shape)

    first = [(0, 1)] * len(GATHERED)
    Gs = [last[n] for n in GATHERED]
    got = _run_ride(_exchange_ride(Gs, first), "rs_exchange_halves")
    sums_0 = [_add_halves(g, (0, 1), r, core, n) for n, g, r in zip(GATHERED, Gs, got)]
    slots_0 = _run_ride(_scatter_ride(sums_0), "rs_scatter_partials")
    place = [lax.axis_index(a).reshape(1).astype(jnp.int32) for a in ("x", "y", "c")]
    halves = []
    for n, h, s in zip(GATHERED, sums_0, slots_0):
        upper = _sum_slots(sums_up[n], slots_up[n], place, n, 1, DEPTH)
        halves.append(_sum_slots(h, s, place, n, 0, DEPTH, prev=upper))
    shard_grads = {n: _unpad_reduced(n, r).reshape(W[n].shape) for n, r in zip(GATHERED, _join_halves(halves))}

    out_g, out_d, out_m, out_v = {}, {}, {}, {}
    for n in GATHERED:
        shp = W[n].shape
        flat = lambda a: a.reshape(-1, shp[-1])
        d, m2, v2 = _adamw(flat(W[n]), flat(shard_grads[n]), flat(M[n]), flat(V[n]), "adamw_" + n)
        out_g[n], out_d[n], out_m[n], out_v[n] = shard_grads[n], d.reshape(shp), m2.reshape(shp), v2.reshape(shp)
    rest_shapes = [W[n].shape for n in SMALL]
    d, m2, v2 = _adamw(_to_small([W[n] for n in SMALL]), _to_small([small_grads[n] for n in SMALL]),
                       _to_small([M[n] for n in SMALL]), _to_small([V[n] for n in SMALL]), "adamw_small")
    for n, dd, mm, vv in zip(SMALL, _unpack(d, rest_shapes), _unpack(m2, rest_shapes), _unpack(v2, rest_shapes)):
        out_g[n], out_d[n], out_m[n], out_v[n] = small_grads[n], dd, mm, vv

    return (loss, grad_x, *[out_g[n] for n in WEIGHTS], *[out_d[n] for n in WEIGHTS], *[out_m[n] for n in WEIGHTS],
            *[out_v[n] for n in WEIGHTS])
```

```python
import functools

import numpy as np
import jax
import jax.numpy as jnp
from jax import lax
from jax.experimental import pallas as pl
from jax.experimental.pallas import tpu as pltpu

F32 = jnp.float32
BF16 = jnp.bfloat16

EPS = 1e-6
NEG_INF = -1e30
DEPTH = 4
HEADS = 8
QK_NOPE = 64
QK_ROPE = 32
V_HEAD = 64
HEAD_PAD = 128
Q_LORA = 256
KV_LORA = 128
ROPE_THETA = 10000.0
POOL_WINDOWS = (2, 4, 8, 16)
GROUPS = 4
GROUP_DIM = 64
MIX_W = GROUPS * GROUP_DIM
POOL_HALO = 16
CONV_HALO = 16
SGU_BLOCK = 128
CHUNK = 64
CHUNK_SHIFT = 6
ACT_ROWS = 16
NORM_ROWS = 16
ROW_PARTS = 2
SOFTMAX_ROWS = 32
GROUP_SHIFT = 6
N_BRANCH = 4
MLA_W = HEADS * HEAD_PAD
N_CHIPS = 4
CHIP_HEADS_W = MLA_W // N_CHIPS
CHIP_KV = 2 * CHIP_HEADS_W
ATTN_SCALE = (QK_NOPE + QK_ROPE) ** -0.5
LOG2E = 1.4426950408889634
LN2 = 0.6931471805599453
SEC_MLA = Q_LORA + KV_LORA + HEAD_PAD
SEC_MIX = 6 * MIX_W

ADAM_LR = 0.001
ADAM_B1 = 0.9
ADAM_B2 = 0.999
ADAM_EPS = 1e-08
ADAM_WD = 0.01
ADAM_STEP = 10

VMEM_LIMIT = 56 * 1024 * 1024
MESH = pl.DeviceIdType.MESH


def _tile(n, pref, mult=8):
    t = min(n, pref)
    while t > 0:
        if n % t == 0 and t % mult == 0:
            return t
        t -= 1
    return n


def _params(n_axes):
    return pltpu.CompilerParams(dimension_semantics=("arbitrary",) * n_axes, vmem_limit_bytes=VMEM_LIMIT)


def _dot(a, b):
    return jnp.dot(a, b, preferred_element_type=F32)


def _dot_nt(a, b):
    return lax.dot_general(a, b, (((1,), (1,)), ((), ())), preferred_element_type=F32)


def _dot_tn(a, b):
    return lax.dot_general(a, b, (((0,), (0,)), ((), ())), preferred_element_type=F32)


def _rms_r(x):
    return lax.rsqrt(jnp.mean(x * x, axis=-1, keepdims=True) + EPS)


def _rms_bwd(dy, x, g):
    r = _rms_r(x)
    u = dy * g
    dx = r * u - x * (r * r * r * jnp.mean(u * x, axis=-1, keepdims=True))
    dg = jnp.sum(dy * x * r, axis=0, keepdims=True)
    return dx, dg


def _sigmoid(x):
    return 1.0 / (1.0 + jnp.exp(-x))


def _shift_down(a, k):
    return pltpu.roll(a, k, 0)


def _shift_up(a, k):
    return pltpu.roll(a, a.shape[0] - k, 0)


def _rope(x, c, sa, sb):
    w = x.shape[-1]
    return x * c + pltpu.roll(x, QK_ROPE // 2, 1) * sa + pltpu.roll(x, w - QK_ROPE // 2, 1) * sb


def _rope_t(d, c, sa, sb):
    w = d.shape[-1]
    return d * c + pltpu.roll(d * sa, w - QK_ROPE // 2, 1) + pltpu.roll(d * sb, QK_ROPE // 2, 1)


def _full(shape):
    return pl.BlockSpec(shape, lambda *_: (0,) * len(shape))


def _gather_ride(bufs, l, stage):
    def copies(_, o_refs, send_sems, recv_sems):
        x, y, c, me = _place()
        sends, arrivals = [], []
        for t, o in enumerate(o_refs):
            hr = o.shape[2] // 2
            for d in (1, 2, 3):
                sems = (send_sems.at[3 * t + d - 1], recv_sems.at[3 * t + d - 1])
                mine = o.at[l, me, pl.ds(c * hr, hr), :]
                theirs = o.at[l, me ^ d, pl.ds(c * hr, hr), :]
                other_half = o.at[l, me ^ d, pl.ds((1 - c) * hr, hr), :]
                if stage == "chips":
                    sends.append(_remote(mine, mine, *sems, _chip_device(me ^ d, c)))
                    arrivals.append(_remote(theirs, theirs, *sems, _chip_device(me ^ d, c)))
                else:
                    sends.append(_remote(theirs, theirs, *sems, (x, y, 1 - c)))
                    arrivals.append(_remote(other_half, other_half, *sems, (x, y, 1 - c)))
        return sends, arrivals

    shapes = [jax.ShapeDtypeStruct(b.shape, b.dtype) for b in bufs]
    return dict(ins=list(bufs), outs=shapes, alias=True, copies=copies, n_sems=3 * len(bufs))


def _exchange_ride(Gs, spans):
    def copies(g_refs, o_refs, send_sems, recv_sems):
        x, y, c, _ = _place()
        cps = []
        for t, (g, o) in enumerate(zip(g_refs, o_refs)):
            hr = g.shape[2] // 2
            l0, l1 = spans[t]
            cps.append(_remote(g.at[:, pl.ds(l0, l1 - l0), pl.ds((1 - c) * hr, hr), :], o, send_sems.at[t], recv_sems.at[t],
                               (x, y, 1 - c)))
        return cps, cps

    shapes = [jax.ShapeDtypeStruct((g.shape[0], l1 - l0, g.shape[2] // 2, g.shape[3]), g.dtype) for g, (l0, l1) in zip(Gs, spans)]
    return dict(ins=list(Gs), outs=shapes, alias=False, copies=copies, n_sems=len(Gs))


def _scatter_ride(Hs):
    def copies(h_refs, o_refs, send_sems, recv_sems):
        x, y, c, me = _place()
        sends, arrivals = [], []
        for t, (h, o) in enumerate(zip(h_refs, o_refs)):
            for d in (1, 2, 3):
                sems = (send_sems.at[3 * t + d - 1], recv_sems.at[3 * t + d - 1])
                sends.append(_remote(h.at[me ^ d], o.at[me], *sems, _chip_device(me ^ d, c)))
                arrivals.append(_remote(h.at[me ^ d], o.at[me ^ d], *sems, _chip_device(me ^ d, c)))
        return sends, arrivals

    shapes = [jax.ShapeDtypeStruct(h.shape, h.dtype) for h in Hs]
    return dict(ins=list(Hs), outs=shapes, alias=False, copies=copies, n_sems=3 * len(Hs))


def _ride_specs(ride, n_in, n_out):
    if not ride:
        return [], [], [], [], [], {}
    anywhere = pl.BlockSpec(memory_space=pl.ANY)
    sems = pltpu.SemaphoreType.DMA((ride["n_sems"],))
    alias = {n_in + i: n_out + i for i in range(len(ride["ins"]))} if ride["alias"] else {}
    return [anywhere] * len(ride["ins"]), [anywhere] * len(ride["outs"]), list(ride["outs"]), [sems, sems], ride["ins"], alias


def _ride_split(ride, rest, n_out, n_scratch):
    a = len(ride["ins"]) if ride else 0
    b = a + n_out
    c = b + (len(ride["outs"]) if ride else 0)
    d = c + n_scratch
    riding = (ride, rest[:a], rest[b:c], rest[d:]) if ride else None
    return rest[a:b], rest[c:d], riding


def _ride_start(riding, first):
    if riding:
        ride, in_refs, out_refs, (send_sems, recv_sems) = riding

        @pl.when(first)
        def _():
            for cp in ride["copies"](in_refs, out_refs, send_sems, recv_sems)[0]:
                cp.start()


def _ride_finish(riding, last):
    if riding:
        ride, in_refs, out_refs, (send_sems, recv_sems) = riding

        @pl.when(last)
        def _():
            sends, arrivals = ride["copies"](in_refs, out_refs, send_sems, recv_sems)
            for cp in arrivals:
                cp.wait_recv()
            for cp in sends:
                cp.wait_send()


def _run_ride(ride, name):
    def body(*refs):
        _, _, (_, in_refs, out_refs, (send_sems, recv_sems)) = _ride_split(ride, refs, 0, 0)
        sends, arrivals = ride["copies"](in_refs, out_refs, send_sems, recv_sems)
        for cp in sends:
            cp.start()
        for cp in arrivals:
            cp.wait_recv()
        for cp in sends:
            cp.wait_send()

    in_specs, out_specs, out_shapes, scratch, operands, alias = _ride_specs(ride, 0, 0)
    return list(pl.pallas_call(body, name=name, in_specs=in_specs, out_specs=out_specs, out_shape=out_shapes,
                               input_output_aliases=alias, scratch_shapes=scratch)(*operands))


def _norm_matmul(x, g, w, name):
    T, K = x.shape
    N = w.shape[1]
    tm, tn = _tile(T, 512), _tile(N, 1536, 128)

    def body(x_ref, g_ref, w_ref, o_ref, h_ref):
        @pl.when(pl.program_id(1) == 0)
        def _():
            xv = x_ref[...]
            h_ref[...] = (xv * _rms_r(xv) * g_ref[...]).astype(BF16)

        o_ref[...] = _dot(h_ref[...], w_ref[...]).astype(BF16)

    return pl.pallas_call(
        body, name=name, grid=(T // tm, N // tn),
        in_specs=[pl.BlockSpec((tm, K), lambda i, j: (i, 0)), _full((1, K)), pl.BlockSpec((K, tn), lambda i, j: (0, j))],
        out_specs=[pl.BlockSpec((tm, tn), lambda i, j: (i, j)), pl.BlockSpec((tm, K), lambda i, j: (i, 0))],
        out_shape=[jax.ShapeDtypeStruct((T, N), BF16), jax.ShapeDtypeStruct((T, K), BF16)],
        compiler_params=_params(2),
    )(x, g, w)


def _mla_prep(P, D, g_cq, g_ckv, wuq, wukv, rope_c, rope_sa, rope_sb, S):
    T = P.shape[0]
    tm = _tile(S, 512)
    n_si = S // tm
    base = 4 * D

    def body(cq_ref, ckv_ref, kr_ref, gq_ref, gkv_ref, wq_ref, wkv_ref, c_ref, sa_ref, sb_ref,
             q_ref, k_ref, v_ref, hq_ref, hkv_ref):
        c, sa, sb = c_ref[...], sa_ref[...], sb_ref[...]
        cq = cq_ref[...].astype(F32)
        hq = (cq * _rms_r(cq) * gq_ref[...]).astype(BF16)
        hq_ref[...] = hq
        q = _dot(hq, wq_ref[...])
        q = _rope(q, jnp.tile(c, (1, HEADS)), jnp.tile(sa, (1, HEADS)), jnp.tile(sb, (1, HEADS)))
        q_ref[...] = (q * (ATTN_SCALE * LOG2E)).astype(BF16)
        ckv = ckv_ref[...].astype(F32)
        hkv = (ckv * _rms_r(ckv) * gkv_ref[...]).astype(BF16)
        hkv_ref[...] = hkv
        kv = _dot(hkv, wkv_ref[...])
        kr = _rope(kr_ref[...].astype(F32), c, sa, sb)
        k_nope = jnp.concatenate([kv[:, j * CHIP_KV:j * CHIP_KV + CHIP_HEADS_W] for j in range(N_CHIPS)], axis=1)
        k_ref[...] = (k_nope + jnp.tile(kr, (1, HEADS))).astype(BF16)
        v = jnp.concatenate([kv[:, j * CHIP_KV + CHIP_HEADS_W:(j + 1) * CHIP_KV] for j in range(N_CHIPS)], axis=1)
        ones_lane = (lax.broadcasted_iota(jnp.int32, (1, MLA_W), 1) & (HEAD_PAD - 1)) == V_HEAD
        v_ref[...] = jnp.where(ones_lane, 1.0, v).astype(BF16)

    tab = pl.BlockSpec((tm, HEAD_PAD), lambda i: (i % n_si, 0))
    row = lambda w: pl.BlockSpec((tm, w), lambda i: (i, 0))
    return pl.pallas_call(
        body, name="mla_prep", grid=(T // tm,),
        in_specs=[pl.BlockSpec((tm, Q_LORA), lambda i: (i, base // Q_LORA)),
                  pl.BlockSpec((tm, KV_LORA), lambda i: (i, (base + Q_LORA) // KV_LORA)),
                  pl.BlockSpec((tm, HEAD_PAD), lambda i: (i, (base + Q_LORA + KV_LORA) // HEAD_PAD)),
                  _full((1, Q_LORA)), _full((1, KV_LORA)), _full((Q_LORA, MLA_W)), _full((KV_LORA, 2 * MLA_W)),
                  tab, tab, tab],
        out_specs=[row(MLA_W), row(MLA_W), row(MLA_W), row(Q_LORA), row(KV_LORA)],
        out_shape=[jax.ShapeDtypeStruct((T, MLA_W), BF16)] * 3
        + [jax.ShapeDtypeStruct((T, Q_LORA), BF16), jax.ShapeDtypeStruct((T, KV_LORA), BF16)],
        compiler_params=_params(1),
    )(P, P, P, g_cq, g_ckv, wuq, wukv, rope_c, rope_sa, rope_sb)


def _chunk_mask(tq, tk):
    row = lax.broadcasted_iota(jnp.int32, (tq, tk), 0)
    col = lax.broadcasted_iota(jnp.int32, (tq, tk), 1)
    return (row >> CHUNK_SHIFT) >= (col >> CHUNK_SHIFT)


def _rows_to_lanes(stats, out_ref):
    t = stats.T
    for h in range(HEADS):
        out_ref[h, 0] = t[h:h + 1, :]


def _flash_fwd(Q, K, V, S, ride=None):
    T = Q.shape[0]
    n_seq = T // S
    tq = _tile(S, 256, 128)
    nq = S // tq

    def body(q_ref, k_ref, v_ref, *rest):
        (o_ref, lse_ref), (m_s, acc_s, s_s, p_s, a_s), riding = _ride_split(ride, rest, 2, 5)
        _ride_start(riding, (pl.program_id(0) == 0) & (pl.program_id(1) == 0))
        qi = pl.program_id(1)
        m_s[...] = jnp.full(m_s.shape, NEG_INF, F32)
        acc_s[...] = jnp.zeros_like(acc_s)

        def block(kb, masked):
            rows = pl.ds(pl.multiple_of(kb * tq, tq), tq)
            for h in range(HEADS):
                hs = slice(h * HEAD_PAD, (h + 1) * HEAD_PAD)
                s_s[h] = _dot_nt(q_ref[:, hs], k_ref[rows, hs])
            def softmax_head(h):
                for r in range(0, tq, SOFTMAX_ROWS):
                    rs = slice(r, r + SOFTMAX_ROWS)
                    s = s_s[h, rs, :]
                    if masked:
                        row = r + lax.broadcasted_iota(jnp.int32, (SOFTMAX_ROWS, tq), 0)
                        col = lax.broadcasted_iota(jnp.int32, (SOFTMAX_ROWS, tq), 1)
                        s = jnp.where((row >> CHUNK_SHIFT) >= (col >> CHUNK_SHIFT), s, NEG_INF)
                    m_old = m_s[h, rs]
                    m_new = jnp.maximum(m_old, jnp.max(s, axis=-1, keepdims=True))
                    m_s[h, rs] = m_new
                    a_s[h, rs] = jnp.exp2(m_old - m_new)
                    for half in range(tq // HEAD_PAD):
                        cs = slice(half * HEAD_PAD, (half + 1) * HEAD_PAD)
                        p_s[h, rs, cs] = jnp.exp2(s[:, cs] - m_new).astype(BF16)

            for h in range(HEADS):
                softmax_head(h)
            for h in range(HEADS):
                hs = slice(h * HEAD_PAD, (h + 1) * HEAD_PAD)
                acc_s[:, hs] = a_s[h] * acc_s[:, hs] + _dot(p_s[h], v_ref[rows, hs])

        def full_block(kb, carry):
            block(kb, False)
            return carry

        lax.fori_loop(0, qi, full_block, 0)
        block(qi, True)
        lane = lax.broadcasted_iota(jnp.int32, (tq, HEAD_PAD), 1)
        lse_all = jnp.zeros((tq, HEAD_PAD), F32)
        for h in range(HEADS):
            hs = slice(h * HEAD_PAD, (h + 1) * HEAD_PAD)
            acc = acc_s[:, hs]
            l = jnp.sum(jnp.where(lane == V_HEAD, acc, 0.0), axis=-1, keepdims=True)
            o_ref[:, hs] = (acc / l).astype(BF16)
            lse_all = jnp.where(lane == h, m_s[h] + jnp.log2(l), lse_all)
        _rows_to_lanes(lse_all, lse_ref)
        _ride_finish(riding, (pl.program_id(0) == n_seq - 1) & (pl.program_id(1) == nq - 1))

    ride_in, ride_out, ride_shapes, ride_scratch, ride_args, alias = _ride_specs(ride, 3, 2)
    out = pl.pallas_call(
        body, name="flash_fwd", grid=(n_seq, nq),
        in_specs=[pl.BlockSpec((tq, MLA_W), lambda b, i: (b * nq + i, 0)),
                  pl.BlockSpec((S, MLA_W), lambda b, i: (b, 0)), pl.BlockSpec((S, MLA_W), lambda b, i: (b, 0))] + ride_in,
        out_specs=[pl.BlockSpec((tq, MLA_W), lambda b, i: (b * nq + i, 0)),
                   pl.BlockSpec((HEADS, 1, 1, tq), lambda b, i: (0, b * nq + i, 0, 0))] + ride_out,
        out_shape=[jax.ShapeDtypeStruct((T, MLA_W), BF16), jax.ShapeDtypeStruct((HEADS, T // tq, 1, tq), F32)] + ride_shapes,
        input_output_aliases=alias,
        scratch_shapes=[pltpu.VMEM((HEADS, tq, HEAD_PAD), F32), pltpu.VMEM((tq, MLA_W), F32), pltpu.VMEM((HEADS, tq, tq), F32),
                        pltpu.VMEM((HEADS, tq, tq), BF16), pltpu.VMEM((HEADS, tq, HEAD_PAD), F32)] + ride_scratch,
        compiler_params=_params(2),
    )(Q, K, V, *ride_args)
    return out[0], out[1], list(out[2:])


def _lane_group():
    return lax.broadcasted_iota(jnp.int32, (1, MIX_W), 1) >> GROUP_SHIFT


def _by_group(a0, a1, a2, a3):
    g = _lane_group()
    return jnp.where(g == 0, a0, jnp.where(g == 1, a1, jnp.where(g == 2, a2, a3)))


def _pool_count(si, tc, rows):
    pos = si * tc + lax.broadcasted_iota(jnp.int32, (rows, MIX_W), 0)
    win = _by_group(*POOL_WINDOWS)
    return jnp.minimum(pos + 1, win).astype(F32)


def _pool_fwd(z, z_prev, si, tc):
    ze = jnp.concatenate([z_prev, z], axis=0)
    s1 = ze + _shift_down(ze, 1)
    s2 = s1 + _shift_down(s1, 2)
    s4 = s2 + _shift_down(s2, 4)
    s8 = s4 + _shift_down(s4, 8)
    win_sum = _by_group(s1, s2, s4, s8)[POOL_HALO:]
    return win_sum / _pool_count(si, tc, tc) - z


def _sgu_weights(w_ref):
    row = lax.broadcasted_iota(jnp.int32, (SGU_BLOCK, SGU_BLOCK), 0)
    col = lax.broadcasted_iota(jnp.int32, (SGU_BLOCK, SGU_BLOCK), 1)
    keep = (row >> CHUNK_SHIFT) >= (col >> CHUNK_SHIFT)
    return keep, [jnp.where(keep, w_ref[g], 0.0).astype(BF16) for g in range(GROUPS)]


def _sgu_mix(vn_blk, wm, bias):
    g = _lane_group()
    mixed = bias
    for k in range(GROUPS):
        mixed = mixed + jnp.where(g == k, _dot(wm[k], vn_blk), 0.0)
    return mixed


def _conv_fwd(z, z_prev, w_ref):
    ze = jnp.concatenate([z_prev, z], axis=0)
    y = w_ref[0:1, :] * _shift_down(ze, 2) + w_ref[1:2, :] * _shift_down(ze, 1) + w_ref[2:3, :] * ze
    return y[CONV_HALO:]


def _mix_specs(T, D, tc):
    base = (4 * D + SEC_MLA) // MIX_W
    cur = lambda k: pl.BlockSpec((tc, MIX_W), lambda i: (i, base + k))
    prev = lambda k, halo: pl.BlockSpec((halo, MIX_W), lambda i: (jnp.maximum(i * (tc // halo) - 1, 0), base + k))
    nxt = lambda k, halo: pl.BlockSpec((halo, MIX_W), lambda i: (jnp.minimum((i + 1) * (tc // halo), T // halo - 1), base + k))
    return cur, prev, nxt


def _mixers_fwd(P, D, S, pool_bd, pool_scale, g_v, sgu_w, sgu_bias, conv_w, ride=None):
    T = P.shape[0]
    tc = _tile(S, 512, SGU_BLOCK)
    n_si = S // tc
    cur, prev, _ = _mix_specs(T, D, tc)

    def body(z_ref, zp_ref, u_ref, v_ref, b_ref, c_ref, x_ref, cp_ref, xp_ref,
             pw_ref, ps_ref, gv_ref, sw_ref, sb_ref, cw_ref, *rest):
        (ob_ref, oc_ref, od_ref), _, riding = _ride_split(ride, rest, 3, 0)
        _ride_start(riding, pl.program_id(0) == 0)
        si = pl.program_id(0) % n_si
        first = si == 0
        z = z_ref[...].astype(F32)
        pooled = _pool_fwd(z, jnp.where(first, 0.0, zp_ref[...].astype(F32)), si, tc)
        ob_ref[...] = (_dot(pooled.astype(BF16), pw_ref[...]) * ps_ref[...]).astype(BF16)

        v = v_ref[...].astype(F32)
        vn = (v * _rms_r(v) * gv_ref[...]).astype(BF16)
        _, wm = _sgu_weights(sw_ref)
        for blk in range(tc // SGU_BLOCK):
            rows = slice(blk * SGU_BLOCK, (blk + 1) * SGU_BLOCK)
            oc_ref[rows, :] = (u_ref[rows, :].astype(F32) * _sgu_mix(vn[rows], wm, sb_ref[...])).astype(BF16)

        zc = c_ref[...].astype(F32) * x_ref[...].astype(F32)
        zc_prev = jnp.where(first, 0.0, cp_ref[...].astype(F32) * xp_ref[...].astype(F32))
        od_ref[...] = (b_ref[...].astype(F32) * _conv_fwd(zc, zc_prev, cw_ref)).astype(BF16)
        _ride_finish(riding, pl.program_id(0) == T // tc - 1)

    out = pl.BlockSpec((tc, MIX_W), lambda i: (i, 0))
    ride_in, ride_out, ride_shapes, ride_scratch, ride_args, alias = _ride_specs(ride, 15, 3)
    res = pl.pallas_call(
        body, name="mixers_fwd", grid=(T // tc,),
        in_specs=[cur(0), prev(0, POOL_HALO), cur(1), cur(2), cur(3), cur(4), cur(5), prev(4, CONV_HALO), prev(5, CONV_HALO),
                  _full((MIX_W, MIX_W)), _full((1, MIX_W)), _full((1, MIX_W)), _full((GROUPS, SGU_BLOCK, SGU_BLOCK)),
                  _full((SGU_BLOCK, MIX_W)), _full((CONV_HALO, MIX_W))] + ride_in,
        out_specs=[out, out, out] + ride_out,
        out_shape=[jax.ShapeDtypeStruct((T, MIX_W), BF16)] * 3 + ride_shapes,
        input_output_aliases=alias,
        scratch_shapes=ride_scratch,
        compiler_params=_params(1),
    )(P, P, P, P, P, P, P, P, P, pool_bd, pool_scale, g_v, sgu_w, sgu_bias, conv_w, *ride_args)
    return res[0], res[1], res[2], list(res[3:])


def _merge_fwd(x, P, A, Bm, C, Dv, wa, wb, wc, wd, wout, g_post):
    T, D = x.shape
    tm = _tile(T, 256)

    def body(x_ref, lg_ref, a_ref, b_ref, c_ref, d_ref, wa_ref, wb_ref, wc_ref, wd_ref, wo_ref, g_ref,
             x1_ref, mg_ref, o_ref):
        merged = jnp.zeros((tm, D), F32)
        for k, (br, w) in enumerate(((a_ref, wa_ref), (b_ref, wb_ref), (c_ref, wc_ref), (d_ref, wd_ref))):
            merged = merged + _sigmoid(lg_ref[:, k * D:(k + 1) * D].astype(F32)) * _dot(br[...], w[...])
        mg = merged.astype(BF16)
        mg_ref[...] = mg
        o = _dot(mg, wo_ref[...])
        o_ref[...] = o
        x1_ref[...] = x_ref[...] + o * _rms_r(o) * g_ref[...]

    row = lambda w: pl.BlockSpec((tm, w), lambda i: (i, 0))
    return pl.pallas_call(
        body, name="merge_fwd", grid=(T // tm,),
        in_specs=[row(D), row(4 * D), row(MLA_W), row(MIX_W), row(MIX_W), row(MIX_W),
                  _full((MLA_W, D)), _full((MIX_W, D)), _full((MIX_W, D)), _full((MIX_W, D)), _full((D, D)), _full((1, D))],
        out_specs=[row(D), row(D), row(D)],
        out_shape=[jax.ShapeDtypeStruct((T, D), F32), jax.ShapeDtypeStruct((T, D), BF16), jax.ShapeDtypeStruct((T, D), F32)],
        compiler_params=_params(1),
    )(x, P, A, Bm, C, Dv, wa, wb, wc, wd, wout, g_post)


def _ffn_specs(T, D, Fc, l, rows=512):
    tm = _tile(T, rows)
    row = pl.BlockSpec((tm, D), lambda i, j: (i, 0))
    col = pl.BlockSpec((None, tm, Fc), lambda i, j: (j, i, 0))
    w_in = pl.BlockSpec((None, None, D, Fc), lambda i, j: (l, j, 0, 0))
    w_out = pl.BlockSpec((None, None, Fc, D), lambda i, j: (l, j, 0, 0))
    return tm, row, col, w_in, w_out


def _ffn_fwd(x1, g_pre, wg, wu, wdn, g_post, l):
    T, D = x1.shape
    nf, Fc = wg.shape[1], wg.shape[3]
    tm, row, col, w_in, w_out = _ffn_specs(T, D, Fc, l, 1024)

    def body(x_ref, gp_ref, wg_ref, wu_ref, wd_ref, gq_ref, x2_ref, h_ref, gt_ref, up_ref, f_ref, gt_s, up_s, a_s):
        j = pl.program_id(1)

        @pl.when(j == 0)
        def _():
            for r in range(0, tm, NORM_ROWS):
                rs = slice(r, r + NORM_ROWS)
                xv = x_ref[rs, :]
                h_ref[rs, :] = (xv * _rms_r(xv) * gp_ref[...]).astype(BF16)
            f_ref[...] = jnp.zeros_like(f_ref)

        gt_s[...] = _dot(h_ref[...], wg_ref[...])
        up_s[...] = _dot(h_ref[...], wu_ref[...])
        for r in range(0, tm, ACT_ROWS):
            rs = slice(r, r + ACT_ROWS)
            gt, up = gt_s[rs, :], up_s[rs, :]
            gt_ref[rs, :] = gt.astype(BF16)
            up_ref[rs, :] = up.astype(BF16)
            a_s[rs, :] = (gt * _sigmoid(gt) * up).astype(BF16)
        f_ref[...] += _dot(a_s[...], wd_ref[...])

        @pl.when(j == nf - 1)
        def _():
            for r in range(0, tm, NORM_ROWS):
                rs = slice(r, r + NORM_ROWS)
                f = f_ref[rs, :]
                x2_ref[rs, :] = x_ref[rs, :] + f * _rms_r(f) * gq_ref[...]

    return pl.pallas_call(
        body, name="ffn_fwd", grid=(T // tm, nf),
        in_specs=[row, _full((1, D)), w_in, w_in, w_out, _full((1, D))],
        out_specs=[row, row, col, col, row],
        out_shape=[jax.ShapeDtypeStruct((T, D), F32), jax.ShapeDtypeStruct((T, D), BF16),
                   jax.ShapeDtypeStruct((nf, T, Fc), BF16), jax.ShapeDtypeStruct((nf, T, Fc), BF16),
                   jax.ShapeDtypeStruct((T, D), F32)],
        scratch_shapes=[pltpu.VMEM((tm, Fc), F32), pltpu.VMEM((tm, Fc), F32), pltpu.VMEM((tm, Fc), BF16)],
        compiler_params=_params(2),
    )(x1, g_pre, wg, wu, wdn, g_post)


def _loss_grad(y, target):
    T, D = y.shape
    tm = _tile(T, 512)

    def body(y_ref, t_ref, l_ref, dy_ref):
        @pl.when(pl.program_id(0) == 0)
        def _():
            l_ref[...] = jnp.zeros_like(l_ref)

        d = y_ref[...] - t_ref[...]
        dy_ref[...] = d * (1.0 / D)
        e = jnp.sum((d * d).reshape(tm // 8, 8, D), axis=0)
        part = e[:, 0:128]
        for k in range(1, D // 128):
            part = part + e[:, k * 128:(k + 1) * 128]
        l_ref[...] += part * (0.5 / D)

    row = pl.BlockSpec((tm, D), lambda i: (i, 0))
    return pl.pallas_call(
        body, name="loss_grad", grid=(T // tm,),
        in_specs=[row, row], out_specs=[_full((8, 128)), row],
        out_shape=[jax.ShapeDtypeStruct((8, 128), F32), jax.ShapeDtypeStruct((T, D), F32)],
        compiler_params=_params(1),
    )(y, target)


def _matmul_tn(a, b, name):
    T, M = a.shape
    N = b.shape[1]
    tm, tn, tk = _tile(M, 1024, 128), _tile(N, 1536, 128), _tile(T, 512)

    def body(a_ref, b_ref, o_ref):
        @pl.when(pl.program_id(2) == 0)
        def _():
            o_ref[...] = jnp.zeros_like(o_ref)

        o_ref[...] += _dot_tn(a_ref[...], b_ref[...])

    return pl.pallas_call(
        body, name=name, grid=(M // tm, N // tn, T // tk),
        in_specs=[pl.BlockSpec((tk, tm), lambda i, j, k: (k, i)), pl.BlockSpec((tk, tn), lambda i, j, k: (k, j))],
        out_specs=pl.BlockSpec((tm, tn), lambda i, j, k: (i, j)),
        out_shape=jax.ShapeDtypeStruct((M, N), F32),
        compiler_params=_params(3),
    )(a, b)


def _norm_bwd(dy, x, g, name, add=None, parts=None):
    T, D = x.shape
    tm = _tile(T, 256)
    n_parts = 0 if parts is None else parts.shape[0]

    def body(*refs):
        dx_ref, dg_ref = refs[-2:]
        ins = list(refs[:-2])
        dy_ref = ins.pop(0) if parts is None else None
        p_ref = ins.pop(0) if parts is not None else None
        x_ref, g_ref = ins[0], ins[1]
        add_ref = ins[2] if add is not None else None

        @pl.when(pl.program_id(0) == 0)
        def _():
            dg_ref[...] = jnp.zeros_like(dg_ref)

        dg_sum = jnp.zeros((1, D), F32)
        for r in range(0, tm, NORM_ROWS):
            rs = slice(r, r + NORM_ROWS)
            if parts is None:
                dy = dy_ref[rs, :]
            else:
                dy = p_ref[0, rs, :].astype(F32)
                for k in range(1, n_parts):
                    dy = dy + p_ref[k, rs, :].astype(F32)
            dx, dg = _rms_bwd(dy, x_ref[rs, :], g_ref[...])
            dx_ref[rs, :] = (dx if add is None else add_ref[rs, :] + dx).astype(dx_ref.dtype)
            dg_sum = dg_sum + dg
        dg_ref[...] += dg_sum

    row = pl.BlockSpec((tm, D), lambda i: (i, 0))
    first = [row] if parts is None else [pl.BlockSpec((n_parts, tm, D), lambda i: (0, i, 0))]
    operands = [dy if parts is None else parts, x, g] + ([] if add is None else [add])
    return pl.pallas_call(
        body, name=name, grid=(T // tm,),
        in_specs=first + [row, _full((1, D))] + ([] if add is None else [row]),
        out_specs=[row, _full((1, D))],
        out_shape=[jax.ShapeDtypeStruct((T, D), BF16 if add is None else F32), jax.ShapeDtypeStruct((1, D), F32)],
        compiler_params=_params(1),
    )(*operands)


FFN_WEIGHTS = ("w_ffn_gate", "w_ffn_up", "w_ffn_down")


def _ffn_bwd_main(df, gt, up, h2, wg, wu, wdn, l, bufs, ride=None):
    T, D = h2.shape
    nf, Fc = wg.shape[1], wg.shape[3]
    tm = _tile(T, 512)
    nt = T // tm
    keep = [] if bufs is None else [bufs[n] for n in FFN_WEIGHTS]

    def body(df_ref, gt_ref, up_ref, h_ref, wg_ref, wu_ref, wd_ref, *rest):
        (dh_ref, gg_ref, gu_ref, gd_ref), (da_s, dgt_s, dup_s, act_s), riding = _ride_split(ride, rest[len(keep):], 4, 4)
        j, i = pl.program_id(0), pl.program_id(1)
        _ride_start(riding, (j == 0) & (i == 0))

        @pl.when(i == 0)
        def _():
            for r in (gg_ref, gu_ref, gd_ref):
                r[...] = jnp.zeros_like(r)

        parts = [slice(p * tm // ROW_PARTS, (p + 1) * tm // ROW_PARTS) for p in range(ROW_PARTS)]
        for ps in parts:
            da_s[ps, :] = _dot_nt(df_ref[ps, :], wd_ref[...])
        for ps in parts:
            for r in range(ps.start, ps.stop, ACT_ROWS):
                rs = slice(r, r + ACT_ROWS)
                da = da_s[rs, :]
                gt = gt_ref[rs, :].astype(F32)
                u = up_ref[rs, :].astype(F32)
                sig = _sigmoid(gt)
                silu = gt * sig
                dgt_s[rs, :] = (da * u * (sig * (1.0 + gt * (1.0 - sig)))).astype(BF16)
                dup_s[rs, :] = (da * silu).astype(BF16)
                act_s[rs, :] = (silu * u).astype(BF16)
            dh_ref[ps, :] = (_dot_nt(dgt_s[ps, :], wg_ref[...]) + _dot_nt(dup_s[ps, :], wu_ref[...])).astype(BF16)
        gg_ref[...] += _dot_tn(h_ref[...], dgt_s[...])
        gu_ref[...] += _dot_tn(h_ref[...], dup_s[...])
        gd_ref[...] += _dot_tn(act_s[...], df_ref[...])
        _ride_finish(riding, (j == nf - 1) & (i == nt - 1))

    row = pl.BlockSpec((tm, D), lambda j, i: (i, 0))
    col = pl.BlockSpec((None, tm, Fc), lambda j, i: (j, i, 0))
    w_in = pl.BlockSpec((None, None, D, Fc), lambda j, i: (l, j, 0, 0))
    w_out = pl.BlockSpec((None, None, Fc, D), lambda j, i: (l, j, 0, 0))
    g_in = pl.BlockSpec((None, None, D, Fc), lambda j, i: (j, l, 0, 0))
    g_out = pl.BlockSpec((None, None, Fc, D), lambda j, i: (j, l, 0, 0))
    n_in = 7
    ride_in, ride_out, ride_shapes, ride_scratch, ride_args, alias = _ride_specs(ride, n_in + len(keep), 4)
    out = pl.pallas_call(
        body, name="ffn_bwd_main", grid=(nf, nt),
        in_specs=[row, col, col, row, w_in, w_in, w_out] + [pl.BlockSpec(memory_space=pl.ANY)] * len(keep) + ride_in,
        out_specs=[pl.BlockSpec((None, tm, D), lambda j, i: (j, i, 0)), g_in, g_in, g_out] + ride_out,
        out_shape=[jax.ShapeDtypeStruct((nf, T, D), BF16), jax.ShapeDtypeStruct((nf, DEPTH, D, Fc), F32),
                   jax.ShapeDtypeStruct((nf, DEPTH, D, Fc), F32), jax.ShapeDtypeStruct((nf, DEPTH, Fc, D), F32)] + ride_shapes,
        input_output_aliases={**{n_in + k: 1 + k for k in range(len(keep))}, **alias},
        scratch_shapes=[pltpu.VMEM((tm, Fc), F32), pltpu.VMEM((tm, Fc), BF16), pltpu.VMEM((tm, Fc), BF16),
                        pltpu.VMEM((tm, Fc), BF16)] + ride_scratch,
        compiler_params=_params(2),
    )(df, gt, up, h2, wg, wu, wdn, *keep, *ride_args)
    return out[0], dict(zip(FFN_WEIGHTS, out[1:4])), list(out[4:])


def _wgrad_out(l, n_layers, rows, cols):
    spec = pl.BlockSpec((N_CHIPS, None, rows, cols), lambda *_: (0, l, 0, 0))
    return spec, jax.ShapeDtypeStruct((N_CHIPS, n_layers, rows, cols), F32)


def _merge_bwd(dx1, o, merged, P, A, Bm, C, Dv, wa, wb, wc, wd, wout, g_post, l, bufs, ride=None):
    T, D = o.shape
    tm = _tile(T, 256)
    Dc = D // N_CHIPS
    names = ("w_out", "w_br_a", "w_br_b", "w_br_c", "w_br_d")
    keep = [] if bufs is None else [bufs[n] for n in names]

    def body(dx1_ref, o_ref, mg_ref, lg_ref, a_ref, b_ref, c_ref, d_ref, wa_ref, wb_ref, wc_ref, wd_ref, wo_ref, g_ref, *rest):
        outs, _, riding = _ride_split(ride, rest[len(keep):], 12, 0)
        dlg_ref, da_ref, db_ref, dc_ref, dd_ref, dg_ref, dt_ref, go_ref, ga_ref, gb_ref, gc_ref, gd_ref = outs
        _ride_start(riding, pl.program_id(0) == 0)

        @pl.when(pl.program_id(0) == 0)
        def _():
            for r in (dg_ref, go_ref, ga_ref, gb_ref, gc_ref, gd_ref):
                r[...] = jnp.zeros_like(r)

        d_o, dg = _rms_bwd(dx1_ref[...], o_ref[...], g_ref[...])
        dg_ref[...] += dg
        d_o = d_o.astype(BF16)
        for k in range(N_CHIPS):
            go_ref[k] += _dot_tn(mg_ref[:, k * Dc:(k + 1) * Dc], d_o)
        dm = _dot_nt(d_o, wo_ref[...])
        branches = ((a_ref, wa_ref, da_ref, ga_ref), (b_ref, wb_ref, db_ref, gb_ref),
                    (c_ref, wc_ref, dc_ref, gc_ref), (d_ref, wd_ref, dd_ref, gd_ref))
        for j, (br, w, dbr_ref, gw_ref) in enumerate(branches):
            gate = _sigmoid(lg_ref[:, j * D:(j + 1) * D].astype(F32))
            y = _dot(br[...], w[...])
            dlg_ref[:, j * D:(j + 1) * D] = (dm * y * gate * (1.0 - gate)).astype(BF16)
            dy = (dm * gate).astype(BF16)
            d_in = _dot_nt(dy, w[...]).astype(dbr_ref.dtype)
            dbr_ref[...] = d_in
            if j == 0:
                prod = d_in.astype(F32) * br[...].astype(F32)
                lane = lax.broadcasted_iota(jnp.int32, (tm, HEAD_PAD), 1)
                delta = jnp.zeros((tm, HEAD_PAD), F32)
                for h in range(HEADS):
                    row_sum = jnp.sum(prod[:, h * HEAD_PAD:(h + 1) * HEAD_PAD], axis=-1, keepdims=True)
                    delta = jnp.where(lane == h, row_sum, delta)
                _rows_to_lanes(delta, dt_ref)
            for k in range(N_CHIPS):
                gw_ref[k] += _dot_tn(br[...], dy[:, k * Dc:(k + 1) * Dc])

        _ride_finish(riding, pl.program_id(0) == T // tm - 1)

    row = lambda w: pl.BlockSpec((tm, w), lambda i: (i, 0))
    wg = [_wgrad_out(l, DEPTH, r, c) for r, c in ((Dc, D), (MLA_W, Dc), (MIX_W, Dc), (MIX_W, Dc), (MIX_W, Dc))]
    n_in = 14
    ride_in, ride_out, ride_shapes, ride_scratch, ride_args, alias = _ride_specs(ride, n_in + len(keep), 12)
    stat = pl.BlockSpec((HEADS, 1, 1, tm), lambda i: (0, i, 0, 0))
    out = pl.pallas_call(
        body, name="merge_bwd", grid=(T // tm,),
        in_specs=[row(D), row(D), row(D), row(4 * D), row(MLA_W), row(MIX_W), row(MIX_W), row(MIX_W),
                  _full((MLA_W, D)), _full((MIX_W, D)), _full((MIX_W, D)), _full((MIX_W, D)), _full((D, D)), _full((1, D))]
        + [pl.BlockSpec(memory_space=pl.ANY)] * len(keep) + ride_in,
        out_specs=[row(4 * D), row(MLA_W), row(MIX_W), row(MIX_W), row(MIX_W), _full((1, D)), stat] + [s for s, _ in wg]
        + ride_out,
        out_shape=[jax.ShapeDtypeStruct((T, 4 * D), BF16), jax.ShapeDtypeStruct((T, MLA_W), BF16)]
        + [jax.ShapeDtypeStruct((T, MIX_W), F32)] * 3 + [jax.ShapeDtypeStruct((1, D), F32),
                                                         jax.ShapeDtypeStruct((HEADS, T // tm, 1, tm), F32)]
        + [s for _, s in wg] + ride_shapes,
        input_output_aliases={**{n_in + i: 7 + i for i in range(len(keep))}, **alias},
        scratch_shapes=ride_scratch,
        compiler_params=_params(1),
    )(dx1, o, merged, P, A, Bm, C, Dv, wa, wb, wc, wd, wout, g_post, *keep, *ride_args)
    return out[:7], dict(zip(names, out[7:12])), list(out[12:])


def _mixers_bwd(P, D, S, dBm, dC, dDv, pool_bd, pool_scale, g_v, sgu_w, sgu_bias, conv_w):
    T = P.shape[0]
    tc = _tile(S, 512, SGU_BLOCK)
    n_si = S // tc
    cur, prev, nxt = _mix_specs(T, D, tc)
    n_blk = tc // SGU_BLOCK

    def body(z_ref, zp_ref, u_ref, v_ref, b_ref, c_ref, x_ref, cp_ref, xp_ref, bn_ref,
             dbm_ref, dbmn_ref, dc_ref, ddv_ref, ddvn_ref,
             pw_ref, ps_ref, gv_ref, sw_ref, sb_ref, cw_ref,
             dp_ref, dpw_ref, dps_ref, dgv_ref, dsw_ref, dsb_ref, dcw_ref, dvn_acc):
        si = pl.program_id(0) % n_si
        first, last = si == 0, si == n_si - 1

        @pl.when(pl.program_id(0) == 0)
        def _():
            for r in (dpw_ref, dps_ref, dgv_ref, dsw_ref, dsb_ref, dcw_ref):
                r[...] = jnp.zeros_like(r)

        z = z_ref[...].astype(F32)
        pooled = _pool_fwd(z, jnp.where(first, 0.0, zp_ref[...].astype(F32)), si, tc).astype(BF16)
        dbm = dbm_ref[...]
        dps_ref[...] += jnp.sum(dbm * _dot(pooled, pw_ref[...]), axis=0, keepdims=True)
        dmix = (jnp.concatenate([dbm, jnp.where(last, 0.0, dbmn_ref[...])], axis=0) * ps_ref[...]).astype(BF16)
        dpw_ref[...] += _dot_tn(pooled, dmix[:tc])
        dpool = _dot_nt(dmix, pw_ref[...])
        e = dpool / _pool_count(si, tc, tc + POOL_HALO)
        f1 = e + _shift_up(e, 1)
        f2 = f1 + _shift_up(f1, 2)
        f4 = f2 + _shift_up(f2, 4)
        f8 = f4 + _shift_up(f4, 8)
        dp_ref[:, 0:MIX_W] = (_by_group(f1, f2, f4, f8)[:tc] - dpool[:tc]).astype(BF16)

        v = v_ref[...].astype(F32)
        vn = (v * _rms_r(v) * gv_ref[...]).astype(BF16)
        keep, wm = _sgu_weights(sw_ref)
        g = _lane_group()
        for blk in range(n_blk):
            rows = slice(blk * SGU_BLOCK, (blk + 1) * SGU_BLOCK)
            vb = vn[rows]
            dc = dc_ref[rows, :]
            dp_ref[rows, MIX_W:2 * MIX_W] = (dc * _sgu_mix(vb, wm, sb_ref[...])).astype(BF16)
            dmx = dc * u_ref[rows, :].astype(F32)
            dsb_ref[...] += dmx
            dvn = jnp.zeros((SGU_BLOCK, MIX_W), F32)
            for k in range(GROUPS):
                dmk = jnp.where(g == k, dmx, 0.0).astype(BF16)
                dsw_ref[k] += jnp.where(keep, _dot_nt(dmk, vb), 0.0)
                dvn = dvn + _dot_tn(wm[k], dmk)
            dvn_acc[rows, :] = dvn
        dv, dg = _rms_bwd(dvn_acc[...], v, gv_ref[...])
        dgv_ref[...] += dg
        dp_ref[:, 2 * MIX_W:3 * MIX_W] = dv.astype(BF16)

        cg, xg, bg = c_ref[...].astype(F32), x_ref[...].astype(F32), b_ref[...].astype(F32)
        zc = cg * xg
        ze = jnp.concatenate([jnp.where(first, 0.0, cp_ref[...].astype(F32) * xp_ref[...].astype(F32)), zc], axis=0)
        z1, z2 = _shift_down(ze, 1)[CONV_HALO:], _shift_down(ze, 2)[CONV_HALO:]
        ddv = ddv_ref[...]
        y = cw_ref[0:1, :] * z2 + cw_ref[1:2, :] * z1 + cw_ref[2:3, :] * zc
        dp_ref[:, 3 * MIX_W:4 * MIX_W] = (ddv * y).astype(BF16)
        dy = ddv * bg
        dcw_ref[0:1, :] += jnp.sum(dy * z2, axis=0, keepdims=True)
        dcw_ref[1:2, :] += jnp.sum(dy * z1, axis=0, keepdims=True)
        dcw_ref[2:3, :] += jnp.sum(dy * zc, axis=0, keepdims=True)
        dye = jnp.concatenate([dy, jnp.where(last, 0.0, ddvn_ref[...] * bn_ref[...].astype(F32))], axis=0)
        dz = (cw_ref[2:3, :] * dye + cw_ref[1:2, :] * _shift_up(dye, 1) + cw_ref[0:1, :] * _shift_up(dye, 2))[:tc]
        dp_ref[:, 4 * MIX_W:5 * MIX_W] = (dz * xg).astype(BF16)
        dp_ref[:, 5 * MIX_W:6 * MIX_W] = (dz * cg).astype(BF16)

    grad = lambda halo: pl.BlockSpec((halo, MIX_W), lambda i: (jnp.minimum((i + 1) * (tc // halo), T // halo - 1), 0))
    out = pl.BlockSpec((tc, MIX_W), lambda i: (i, 0))
    return pl.pallas_call(
        body, name="mixers_bwd", grid=(T // tc,),
        in_specs=[cur(0), prev(0, POOL_HALO), cur(1), cur(2), cur(3), cur(4), cur(5), prev(4, CONV_HALO), prev(5, CONV_HALO),
                  nxt(3, CONV_HALO), out, grad(POOL_HALO), out, out, grad(CONV_HALO),
                  _full((MIX_W, MIX_W)), _full((1, MIX_W)), _full((1, MIX_W)), _full((GROUPS, SGU_BLOCK, SGU_BLOCK)),
                  _full((SGU_BLOCK, MIX_W)), _full((CONV_HALO, MIX_W))],
        out_specs=[pl.BlockSpec((tc, SEC_MIX), lambda i: (i, 0)), _full((MIX_W, MIX_W)), _full((1, MIX_W)), _full((1, MIX_W)),
                   _full((GROUPS, SGU_BLOCK, SGU_BLOCK)), _full((SGU_BLOCK, MIX_W)), _full((CONV_HALO, MIX_W))],
        out_shape=[jax.ShapeDtypeStruct((T, SEC_MIX), BF16), jax.ShapeDtypeStruct((MIX_W, MIX_W), F32),
                   jax.ShapeDtypeStruct((1, MIX_W), F32), jax.ShapeDtypeStruct((1, MIX_W), F32),
                   jax.ShapeDtypeStruct((GROUPS, SGU_BLOCK, SGU_BLOCK), F32), jax.ShapeDtypeStruct((SGU_BLOCK, MIX_W), F32),
                   jax.ShapeDtypeStruct((CONV_HALO, MIX_W), F32)],
        scratch_shapes=[pltpu.VMEM((tc, MIX_W), F32)],
        compiler_params=_params(1),
    )(P, P, P, P, P, P, P, P, P, P, dBm, dBm, dC, dDv, dDv, pool_bd, pool_scale, g_v, sgu_w, sgu_bias, conv_w)


def _attn_tile(S):
    return _tile(S, 256, 128)


def _flash_bwd(Q, K, V, dO, lse_t, delta_t, S, ride=None):
    T = Q.shape[0]
    n_seq = T // S
    tq = _attn_tile(S)
    nq = S // tq

    def body(k_ref, v_ref, q_ref, do_ref, lse_ref, dl_ref, *rest):
        (dq_ref, dk_ref, dv_ref), (s_s, dp_s, p_s, ds_s), riding = _ride_split(ride, rest, 3, 4)
        kb = pl.program_id(1)
        _ride_start(riding, (pl.program_id(0) == 0) & (kb == 0))

        @pl.when(kb == 0)
        def _():
            dq_ref[...] = jnp.zeros_like(dq_ref)

        dk_ref[...] = jnp.zeros_like(dk_ref)
        dv_ref[...] = jnp.zeros_like(dv_ref)

        def block(qi, masked):
            rows = pl.ds(pl.multiple_of(qi * tq, tq), tq)
            for h in range(HEADS):
                hs = slice(h * HEAD_PAD, (h + 1) * HEAD_PAD)
                s_s[h] = _dot_nt(k_ref[:, hs], q_ref[rows, hs])
                dp_s[h] = _dot_nt(v_ref[:, hs], do_ref[rows, hs])
            for h in range(HEADS):
                lse_row, dl_row = lse_ref[h, qi], dl_ref[h, qi]
                for r in range(0, tq, SOFTMAX_ROWS):
                    rs = slice(r, r + SOFTMAX_ROWS)
                    s = s_s[h, rs, :]
                    if masked:
                        key = r + lax.broadcasted_iota(jnp.int32, (SOFTMAX_ROWS, tq), 0)
                        query = lax.broadcasted_iota(jnp.int32, (SOFTMAX_ROWS, tq), 1)
                        s = jnp.where((query >> CHUNK_SHIFT) >= (key >> CHUNK_SHIFT), s, NEG_INF)
                    p = jnp.exp2(s - lse_row)
                    p_s[h, rs, :] = p.astype(BF16)
                    ds_s[h, rs, :] = (p * (dp_s[h, rs, :] - dl_row)).astype(BF16)
            for h in range(HEADS):
                hs = slice(h * HEAD_PAD, (h + 1) * HEAD_PAD)
                dv_ref[:, hs] += _dot(p_s[h], do_ref[rows, hs])
                dk_ref[:, hs] += _dot(ds_s[h], q_ref[rows, hs])
                dq_ref[rows, hs] += _dot_tn(ds_s[h], k_ref[:, hs])

        def full_block(qi, carry):
            block(qi, False)
            return carry

        block(kb, True)
        lax.fori_loop(kb + 1, nq, full_block, 0)
        dk_ref[...] = dk_ref[...] * LN2
        _ride_finish(riding, (pl.program_id(0) == n_seq - 1) & (kb == nq - 1))

    tile = pl.BlockSpec((tq, MLA_W), lambda b, i: (b * nq + i, 0))
    seq = pl.BlockSpec((S, MLA_W), lambda b, i: (b, 0))
    stat = pl.BlockSpec((HEADS, nq, 1, tq), lambda b, i: (0, b, 0, 0))
    ride_in, ride_out, ride_shapes, ride_scratch, ride_args, alias = _ride_specs(ride, 6, 3)
    out = pl.pallas_call(
        body, name="flash_bwd", grid=(n_seq, nq),
        in_specs=[tile, tile, seq, seq, stat, stat] + ride_in,
        out_specs=[seq, tile, tile] + ride_out,
        out_shape=[jax.ShapeDtypeStruct((T, MLA_W), F32)] * 3 + ride_shapes,
        input_output_aliases=alias,
        scratch_shapes=[pltpu.VMEM((HEADS, tq, tq), F32), pltpu.VMEM((HEADS, tq, tq), F32),
                        pltpu.VMEM((HEADS, tq, tq), BF16), pltpu.VMEM((HEADS, tq, tq), BF16)] + ride_scratch,
        compiler_params=_params(2),
    )(K, V, Q, dO, lse_t, delta_t, *ride_args)
    return out[0], out[1], out[2], list(out[3:])


def _mla_bwd_post(P, D, S, dQ, dK, dV, hq, hkv, g_cq, g_ckv, wuq, wukv, rope_c, rope_sa, rope_sb, l, bufs):
    T = P.shape[0]
    tm = _tile(S, 512)
    n_si = S // tm
    base = 4 * D
    names = ("w_uq", "w_ukv")

    def body(cq_ref, ckv_ref, dq_ref, dk_ref, dv_ref, hq_ref, hkv_ref, gq_ref, gkv_ref, wq_ref, wkv_ref, c_ref, sa_ref, sb_ref,
             *rest):
        dp_ref, dgq_ref, dgkv_ref, guq_ref, gukv_ref = rest[-5:]

        @pl.when(pl.program_id(0) == 0)
        def _():
            for r in (dgq_ref, dgkv_ref, guq_ref, gukv_ref):
                r[...] = jnp.zeros_like(r)

        c, sa, sb = c_ref[...], sa_ref[...], sb_ref[...]
        dq = _rope_t(dq_ref[...] * ATTN_SCALE, jnp.tile(c, (1, HEADS)), jnp.tile(sa, (1, HEADS)),
                     jnp.tile(sb, (1, HEADS))).astype(BF16)
        dcq, dg = _rms_bwd(_dot_nt(dq, wq_ref[...]), cq_ref[...].astype(F32), gq_ref[...])
        dgq_ref[...] += dg
        dp_ref[:, 0:Q_LORA] = dcq.astype(BF16)

        dk = dk_ref[...]
        dkb, dvb = dk.astype(BF16), dv_ref[...].astype(BF16)
        dkv = jnp.concatenate([p[:, j * CHIP_HEADS_W:(j + 1) * CHIP_HEADS_W] for j in range(N_CHIPS) for p in (dkb, dvb)], axis=1)
        for k in range(N_CHIPS):
            guq_ref[k] += _dot_tn(hq_ref[...], dq[:, k * CHIP_HEADS_W:(k + 1) * CHIP_HEADS_W])
            gukv_ref[k] += _dot_tn(hkv_ref[...], dkv[:, k * CHIP_KV:(k + 1) * CHIP_KV])
        dckv, dg = _rms_bwd(_dot_nt(dkv, wkv_ref[...]), ckv_ref[...].astype(F32), gkv_ref[...])
        dgkv_ref[...] += dg
        dp_ref[:, Q_LORA:Q_LORA + KV_LORA] = dckv.astype(BF16)

        dkr = dk[:, 0:HEAD_PAD]
        for h in range(1, HEADS):
            dkr = dkr + dk[:, h * HEAD_PAD:(h + 1) * HEAD_PAD]
        lane = lax.broadcasted_iota(jnp.int32, (1, HEAD_PAD), 1)
        rope_lanes = (lane >= QK_NOPE) & (lane < QK_NOPE + QK_ROPE)
        dp_ref[:, Q_LORA + KV_LORA:SEC_MLA] = jnp.where(rope_lanes, _rope_t(dkr, c, sa, sb), 0.0).astype(BF16)

    tab = pl.BlockSpec((tm, HEAD_PAD), lambda i: (i % n_si, 0))
    row = lambda w: pl.BlockSpec((tm, w), lambda i: (i, 0))
    wg = [_wgrad_out(l, DEPTH, Q_LORA, CHIP_HEADS_W), _wgrad_out(l, DEPTH, KV_LORA, CHIP_KV)]
    keep = [] if bufs is None else [bufs[n] for n in names]
    n_in = 14
    out = pl.pallas_call(
        body, name="mla_bwd_post", grid=(T // tm,),
        in_specs=[pl.BlockSpec((tm, Q_LORA), lambda i: (i, base // Q_LORA)),
                  pl.BlockSpec((tm, KV_LORA), lambda i: (i, (base + Q_LORA) // KV_LORA)),
                  row(MLA_W), row(MLA_W), row(MLA_W), row(Q_LORA), row(KV_LORA),
                  _full((1, Q_LORA)), _full((1, KV_LORA)), _full((Q_LORA, MLA_W)), _full((KV_LORA, 2 * MLA_W)), tab, tab, tab]
        + [pl.BlockSpec(memory_space=pl.ANY)] * len(keep),
        out_specs=[row(SEC_MLA), _full((1, Q_LORA)), _full((1, KV_LORA))] + [s for s, _ in wg],
        out_shape=[jax.ShapeDtypeStruct((T, SEC_MLA), BF16), jax.ShapeDtypeStruct((1, Q_LORA), F32),
                   jax.ShapeDtypeStruct((1, KV_LORA), F32)] + [s for _, s in wg],
        input_output_aliases={n_in + i: 3 + i for i in range(len(keep))},
        compiler_params=_params(1),
    )(P, P, dQ, dK, dV, hq, hkv, g_cq, g_ckv, wuq, wukv, rope_c, rope_sa, rope_sb, *keep)
    return out[:3], dict(zip(names, out[3:]))


def _proj_bwd(dx1, x, g, dPg, dPa, dPm, w_gates, w_mla, w_mix, ride=None):
    T, D = x.shape
    tm = _tile(T, 256)

    def body(dx1_ref, x_ref, g_ref, dg_ref_in, da_ref, dm_ref, wg_ref, wa_ref, wm_ref, *rest):
        (dx_ref, dg_ref), _, riding = _ride_split(ride, rest, 2, 0)
        _ride_start(riding, pl.program_id(0) == 0)

        @pl.when(pl.program_id(0) == 0)
        def _():
            dg_ref[...] = jnp.zeros_like(dg_ref)

        dh = _dot_nt(dg_ref_in[...], wg_ref[...]) + _dot_nt(da_ref[...], wa_ref[...]) + _dot_nt(dm_ref[...], wm_ref[...])
        dx, dg = _rms_bwd(dh, x_ref[...], g_ref[...])
        dx_ref[...] = dx1_ref[...] + dx
        dg_ref[...] += dg
        _ride_finish(riding, pl.program_id(0) == T // tm - 1)

    row = lambda w: pl.BlockSpec((tm, w), lambda i: (i, 0))
    ride_in, ride_out, ride_shapes, ride_scratch, ride_args, alias = _ride_specs(ride, 9, 2)
    out = pl.pallas_call(
        body, name="proj_bwd", grid=(T // tm,),
        in_specs=[row(D), row(D), _full((1, D)), row(4 * D), row(SEC_MLA), row(SEC_MIX),
                  _full((D, 4 * D)), _full((D, SEC_MLA)), _full((D, SEC_MIX))] + ride_in,
        out_specs=[row(D), _full((1, D))] + ride_out,
        out_shape=[jax.ShapeDtypeStruct((T, D), F32), jax.ShapeDtypeStruct((1, D), F32)] + ride_shapes,
        input_output_aliases=alias, scratch_shapes=ride_scratch,
        compiler_params=_params(1),
    )(dx1, x, g, dPg, dPa, dPm, w_gates, w_mla, w_mix, *ride_args)
    return out[0], out[1], list(out[2:])


def _adamw(w, g, m, v, name):
    R, C = w.shape
    tr = _tile(R, max(8, (1 << 19) // C))

    def body(w_ref, g_ref, m_ref, v_ref, d_ref, mo_ref, vo_ref):
        gv = g_ref[...]
        mn = ADAM_B1 * m_ref[...] + (1.0 - ADAM_B1) * gv
        vn = ADAM_B2 * v_ref[...] + (1.0 - ADAM_B2) * (gv * gv)
        mo_ref[...] = mn
        vo_ref[...] = vn
        m_hat = mn / (1.0 - ADAM_B1 ** ADAM_STEP)
        v_hat = vn / (1.0 - ADAM_B2 ** ADAM_STEP)
        d_ref[...] = -ADAM_LR * (m_hat / (jnp.sqrt(v_hat) + ADAM_EPS) + ADAM_WD * w_ref[...])

    blk = pl.BlockSpec((tr, C), lambda i: (i, 0))
    return pl.pallas_call(
        body, name=name, grid=(R // tr,), in_specs=[blk] * 4, out_specs=[blk] * 3,
        out_shape=[jax.ShapeDtypeStruct((R, C), F32)] * 3, compiler_params=_params(1),
    )(w, g, m, v)


def _rows_tile(rows, cols):
    return _tile(rows, max(16, (1 << 19) // cols), 16)


def _add_halves(G, span, recv, half, name):
    n, L, R, C = G.shape
    l0, nl = span[0], span[1] - span[0]
    hr = R // 2
    tr = _rows_tile(hr, C)
    nb = hr // tr

    def body(half_ref, g_ref, r_ref, o_ref):
        o_ref[...] = (g_ref[...] + r_ref[...]).astype(BF16)

    grid_spec = pltpu.PrefetchScalarGridSpec(
        num_scalar_prefetch=1, grid=(n, nl, nb),
        in_specs=[pl.BlockSpec((None, None, tr, C), lambda k, l, i, h: (k, l0 + l, h[0] * nb + i, 0)),
                  pl.BlockSpec((None, None, tr, C), lambda k, l, i, h: (k, l, i, 0))],
        out_specs=pl.BlockSpec((None, None, tr, C), lambda k, l, i, h: (k, l, i, 0)))
    return pl.pallas_call(
        body, name="rs_add_halves_" + name, grid_spec=grid_spec,
        out_shape=jax.ShapeDtypeStruct((n, nl, hr, C), BF16), compiler_params=_params(3),
    )(half.reshape(1).astype(jnp.int32), G, recv)


def _sum_slots(H, slots, place, name, l0, n_layers, prev=None):
    _, nl, hr, C = slots.shape
    tr = _rows_tile(hr, C)
    nb = hr // tr

    def body(x_ref, y_ref, c_ref, own_ref, s1_ref, s2_ref, s3_ref, *rest):
        o_ref = rest[-1]
        o_ref[...] = ((own_ref[...].astype(F32) + s1_ref[...].astype(F32)) + s2_ref[...].astype(F32)) + s3_ref[...].astype(F32)

    def src(fx, fy):
        def index(l, j, px, py, pc):
            cx = px[0] + fx - 2 * fx * px[0]
            cy = py[0] + fy - 2 * fy * py[0]
            return (2 * cx + cy, l, j, 0)
        return pl.BlockSpec((None, None, tr, C), index)

    keep = [] if prev is None else [prev]
    grid_spec = pltpu.PrefetchScalarGridSpec(
        num_scalar_prefetch=3, grid=(nl, nb),
        in_specs=[src(0, 0), src(0, 1), src(1, 0), src(1, 1)] + [pl.BlockSpec(memory_space=pl.ANY)] * len(keep),
        out_specs=pl.BlockSpec((None, tr, C), lambda l, j, px, py, pc: (l0 + l, pc[0] * nb + j, 0)))
    return pl.pallas_call(
        body, name="rs_sum_slots_" + name, grid_spec=grid_spec,
        out_shape=jax.ShapeDtypeStruct((n_layers, 2 * hr, C), F32),
        input_output_aliases={7: 0} if keep else {}, compiler_params=_params(2),
    )(*place, H, slots, slots, slots, *keep)


HBM = pl.BlockSpec(memory_space=pltpu.HBM)


def _place():
    x, y, c = lax.axis_index("x"), lax.axis_index("y"), lax.axis_index("c")
    return x, y, c, 2 * x + y


def _chip_device(chip, c):
    return (chip // 2, chip % 2, c)


def _remote(src, dst, send_sem, recv_sem, to):
    return pltpu.make_async_remote_copy(src_ref=src, dst_ref=dst, send_sem=send_sem, recv_sem=recv_sem, device_id=to,
                                        device_id_type=MESH)


def _place_own(w, chip, name):
    L, R, C = w.shape
    tr = _rows_tile(R, C)

    def body(p_ref, w_ref, o_ref):
        o_ref[...] = w_ref[...].astype(BF16)

    grid_spec = pltpu.PrefetchScalarGridSpec(
        num_scalar_prefetch=1, grid=(L, R // tr), in_specs=[pl.BlockSpec((None, tr, C), lambda l, j, p: (l, j, 0))],
        out_specs=pl.BlockSpec((None, None, tr, C), lambda l, j, p: (l, p[0], j, 0)))
    return pl.pallas_call(
        body, name="place_" + name, grid_spec=grid_spec,
        out_shape=jax.ShapeDtypeStruct((L, N_CHIPS, R, C), BF16), compiler_params=_params(2),
    )(chip.reshape(1).astype(jnp.int32), w)


def _gather_weights(bufs, l):
    n = len(bufs)

    def body(*refs):
        o_refs = refs[n:2 * n]
        send_sems, recv_sems = refs[2 * n:]
        x, y, c, me = _place()
        sibling = (x, y, 1 - c)

        def copy(t, k, chip, half, to):
            hr = o_refs[t].shape[2] // 2
            block = o_refs[t].at[l, chip, pl.ds(half * hr, hr), :]
            return _remote(block, block, send_sems.at[6 * t + k], recv_sems.at[6 * t + k], to)

        first = [copy(t, d - 1, me, c, _chip_device(me ^ d, c)) for t in range(n) for d in (1, 2, 3)]
        for cp in first:
            cp.start()
        passed = []
        for t in range(n):
            for d in (1, 2, 3):
                copy(t, d - 1, me ^ d, c, sibling).wait_recv()
                passed.append(copy(t, 2 + d, me ^ d, c, sibling))
                passed[-1].start()
        for t in range(n):
            for d in (1, 2, 3):
                copy(t, 2 + d, me ^ d, 1 - c, sibling).wait_recv()
        for cp in first + passed:
            cp.wait_send()

    return pl.pallas_call(
        body, name="gather_weights", in_specs=[HBM] * n, out_specs=[HBM] * n,
        out_shape=[jax.ShapeDtypeStruct(b.shape, b.dtype) for b in bufs],
        input_output_aliases={t: t for t in range(n)},
        scratch_shapes=[pltpu.SemaphoreType.DMA((6 * n,)), pltpu.SemaphoreType.DMA((6 * n,))],
    )(*bufs)


def _join_halves(bufs):
    n = len(bufs)

    def body(*refs):
        o_refs = refs[n:2 * n]
        send_sems, recv_sems = refs[2 * n:]
        x, y, c, _ = _place()

        def half(t, which):
            hr = o_refs[t].shape[1] // 2
            return o_refs[t].at[:, pl.ds(which * hr, hr), :]

        sends = [_remote(half(t, c), half(t, c), send_sems.at[t], recv_sems.at[t], (x, y, 1 - c)) for t in range(n)]
        for cp in sends:
            cp.start()
        for t in range(n):
            _remote(half(t, 1 - c), half(t, 1 - c), send_sems.at[t], recv_sems.at[t], (x, y, 1 - c)).wait_recv()
        for cp in sends:
            cp.wait_send()

    return pl.pallas_call(
        body, name="rs_join_halves", in_specs=[HBM] * n, out_specs=[HBM] * n,
        out_shape=[jax.ShapeDtypeStruct(b.shape, b.dtype) for b in bufs],
        input_output_aliases={t: t for t in range(n)},
        scratch_shapes=[pltpu.SemaphoreType.DMA((n,)), pltpu.SemaphoreType.DMA((n,))],
    )(*bufs)


def _all_reduce_small(v, name):
    R, C = v.shape

    def body(v_ref, o_ref, slots, send_sems, recv_sems):
        x, y, c, _ = _place()
        me = 4 * x + 2 * y + c
        slots[me] = v_ref[...]
        sends = []
        for d in range(1, 8):
            peer = me ^ d
            sends.append(pltpu.make_async_remote_copy(
                src_ref=v_ref, dst_ref=slots.at[me], send_sem=send_sems.at[d - 1], recv_sem=recv_sems.at[d - 1],
                device_id=(peer // 4, (peer // 2) % 2, peer % 2), device_id_type=MESH))
        for cp in sends:
            cp.start()
        for d in range(1, 8):
            peer = me ^ d
            pltpu.make_async_remote_copy(
                src_ref=v_ref, dst_ref=slots.at[peer], send_sem=send_sems.at[d - 1], recv_sem=recv_sems.at[d - 1],
                device_id=(peer // 4, (peer // 2) % 2, peer % 2), device_id_type=MESH).wait_recv()
        for cp in sends:
            cp.wait_send()
        acc = slots[0]
        for k in range(1, 8):
            acc = acc + slots[k]
        o_ref[...] = acc

    vm = pl.BlockSpec(memory_space=pltpu.VMEM)
    return pl.pallas_call(
        body, name=name, in_specs=[vm], out_specs=vm, out_shape=jax.ShapeDtypeStruct((R, C), F32),
        scratch_shapes=[pltpu.VMEM((8, R, C), F32), pltpu.SemaphoreType.DMA((7,)), pltpu.SemaphoreType.DMA((7,))],
    )(v)


SHARDED = ("w_in", "w_uq", "w_ukv", "conv_w", "w_br_a", "w_br_b", "w_br_c", "w_br_d", "w_out", "w_ffn_gate", "w_ffn_up",
           "w_ffn_down")
ROW_SHARDED = ("w_out", "w_ffn_down")
REPLICATED = ("g_pre_mix", "g_cq", "g_ckv", "pool_w", "pool_scale", "g_sgu_v", "sgu_w", "sgu_b", "g_post_mix", "g_pre_ffn",
              "g_post_ffn")
WEIGHTS = ("w_in", "g_pre_mix", "g_cq", "g_ckv", "w_uq", "w_ukv", "pool_w", "pool_scale", "g_sgu_v", "sgu_w", "sgu_b",
           "conv_w", "w_br_a", "w_br_b", "w_br_c", "w_br_d", "w_out", "g_post_mix", "g_pre_ffn", "w_ffn_gate", "w_ffn_up",
           "w_ffn_down", "g_post_ffn")
GATHERED = tuple(n for n in SHARDED if n != "conv_w")


def _unpack(packed, shapes):
    flat = packed.reshape(-1)
    out, o = [], 0
    for s in shapes:
        n = int(np.prod(s))
        out.append(flat[o:o + n].reshape(s))
        o += n
    return out


def _join_cols(g, l):
    return jnp.concatenate([g[l, k] for k in range(N_CHIPS)], axis=1)


def _pad_heads(w, real):
    lead = w.shape[:-1]
    w = w.reshape(lead + (HEADS, real))
    return jnp.pad(w, [(0, 0)] * len(lead) + [(0, 0), (0, HEAD_PAD - real)]).reshape(lead + (MLA_W,))


def _unpad_heads(w, real):
    lead = w.shape[:-1]
    return w.reshape(lead + (HEADS, HEAD_PAD))[..., :real].reshape(lead + (HEADS * real,))


IN_OFFSETS = {"cq": 0, "ckv": Q_LORA, "kr": Q_LORA + KV_LORA, "mix": Q_LORA + KV_LORA + QK_ROPE}
IN_GATES = Q_LORA + KV_LORA + QK_ROPE + SEC_MIX


def _pad_w_in(w):
    K = w.shape[0]
    z = lambda n: jnp.zeros((K, n), w.dtype)
    return jnp.concatenate([w[:, IN_GATES:], w[:, :IN_OFFSETS["kr"]], z(QK_NOPE), w[:, IN_OFFSETS["kr"]:IN_OFFSETS["mix"]],
                            z(HEAD_PAD - QK_NOPE - QK_ROPE), w[:, IN_OFFSETS["mix"]:IN_GATES]], axis=1)


def _unpad_w_in(d_gates, d_mla, d_mix):
    kr = d_mla[:, Q_LORA + KV_LORA + QK_NOPE:Q_LORA + KV_LORA + QK_NOPE + QK_ROPE]
    return jnp.concatenate([d_mla[:, :Q_LORA + KV_LORA], kr, d_mix, d_gates], axis=1)


def _rope_tables(S):
    half = QK_ROPE // 2
    inv = ROPE_THETA ** (-jnp.arange(0, QK_ROPE, 2, dtype=F32) / QK_ROPE)
    ang = jnp.arange(S, dtype=F32)[:, None] * inv[None, :]
    cos, sin = jnp.cos(ang), jnp.sin(ang)
    one, zero = jnp.ones((S, QK_NOPE), F32), jnp.zeros((S, half), F32)
    tail = HEAD_PAD - QK_NOPE - QK_ROPE
    c = jnp.concatenate([one, cos, cos, jnp.ones((S, tail), F32)], axis=1)
    sa = jnp.concatenate([0 * one, zero, sin, jnp.zeros((S, tail), F32)], axis=1)
    sb = jnp.concatenate([0 * one, -sin, zero, jnp.zeros((S, tail), F32)], axis=1)
    return c, sa, sb


def _layer_weights(gathered, full, l, D):
    w = {}
    w_in = _pad_w_in(_join_cols(gathered["w_in"], l))
    w["w_in"] = w_in
    w["w_in_gates"], w["w_in_mla"], w["w_in_mix"] = w_in[:, :4 * D], w_in[:, 4 * D:4 * D + SEC_MLA], w_in[:, 4 * D + SEC_MLA:]
    w["w_uq"] = _pad_heads(_join_cols(gathered["w_uq"], l), QK_NOPE + QK_ROPE)
    ukv = _join_cols(gathered["w_ukv"], l).reshape(KV_LORA, HEADS, QK_NOPE + V_HEAD)
    pad = ((0, 0), (0, 0), (0, HEAD_PAD - QK_NOPE))
    k_pad = jnp.pad(ukv[:, :, :QK_NOPE], pad).reshape(KV_LORA, N_CHIPS, CHIP_HEADS_W)
    v_pad = jnp.pad(ukv[:, :, QK_NOPE:], pad).reshape(KV_LORA, N_CHIPS, CHIP_HEADS_W)
    w["w_ukv"] = jnp.concatenate([k_pad, v_pad], axis=2).reshape(KV_LORA, 2 * MLA_W)
    w["w_br_a"] = jnp.pad(_join_cols(gathered["w_br_a"], l).reshape(HEADS, V_HEAD, D),
                          ((0, 0), (0, HEAD_PAD - V_HEAD), (0, 0))).reshape(MLA_W, D)
    for n in ("w_br_b", "w_br_c", "w_br_d"):
        w[n] = _join_cols(gathered[n], l)
    w["w_out"] = gathered["w_out"][l].reshape(D, D)
    for n in ("g_pre_mix", "g_cq", "g_ckv", "pool_scale", "g_sgu_v", "g_post_mix", "g_pre_ffn", "g_post_ffn"):
        w[n] = full[n][l].reshape(1, -1)
    pw = full["pool_w"][l]
    w["pool_bd"] = jax.scipy.linalg.block_diag(*[pw[g] for g in range(GROUPS)]).astype(BF16)
    w["sgu_w"] = full["sgu_w"][l]
    w["sgu_bias"] = jnp.repeat(full["sgu_b"][l].T, GROUP_DIM, axis=1)
    w["conv_w"] = jnp.pad(full["conv_w"][l].reshape(3, MIX_W), ((0, CONV_HALO - 3), (0, 0)))
    return w


def _layer_fwd(x, w, gathered, l, S, rope, gather_next):
    D = x.shape[1]
    bufs = [gathered[n] for n in GATHERED]
    P, h = _norm_matmul(x, w["g_pre_mix"], w["w_in"], "proj_fwd")
    Q, K, V, hq, hkv = _mla_prep(P, D, w["g_cq"], w["g_ckv"], w["w_uq"], w["w_ukv"], *rope, S)
    A, lse, bufs = _flash_fwd(Q, K, V, S, _gather_ride(bufs, l + 1, "chips") if gather_next else None)
    Bm, C, Dv, bufs = _mixers_fwd(P, D, S, w["pool_bd"], w["pool_scale"], w["g_sgu_v"], w["sgu_w"], w["sgu_bias"], w["conv_w"],
                                  _gather_ride(bufs, l + 1, "cores") if gather_next else None)
    if gather_next:
        gathered = dict(zip(GATHERED, bufs))
    x1, merged, o = _merge_fwd(x, P, A, Bm, C, Dv, w["w_br_a"], w["w_br_b"], w["w_br_c"], w["w_br_d"], w["w_out"], w["g_post_mix"])
    x2, h2, gt, up, f = _ffn_fwd(x1, w["g_pre_ffn"], gathered["w_ffn_gate"], gathered["w_ffn_up"], gathered["w_ffn_down"],
                                 w["g_post_ffn"], l)
    saved = dict(x=x, P=P, h=h, Q=Q, K=K, V=V, hq=hq, hkv=hkv, A=A, lse=lse, Bm=Bm, C=C, Dv=Dv, x1=x1, merged=merged, o=o,
                 h2=h2, gt=gt, up=up, f=f)
    return x2, saved, gathered


RIDE_SETS = (("w_ffn_gate", "w_out", "w_br_a"), ("w_in", "w_br_b", "w_br_c", "w_br_d", "w_uq", "w_ukv"),
             ("w_ffn_up", "w_ffn_down"))


def _layer_bwd(dx2, w, gathered, l, s, S, rope, bufs, early=None, send_ffn=False):
    D = dx2.shape[1]
    g = {}
    others = [n for n in GATHERED if n not in FFN_WEIGHTS]

    def scatter(k):
        return _scatter_ride([sums[n] for n in RIDE_SETS[k]]) if early else None

    sums, slots = {}, {}
    if early:
        done, spans, half, got_ffn = early
        ride = _exchange_ride([done[n] for n in others], [spans[n] for n in others])
    df, g["g_post_ffn"] = _norm_bwd(dx2, s["f"], w["g_post_ffn"], "ffn_bwd_pre")
    dh_parts, filled, got = _ffn_bwd_main(df, s["gt"], s["up"], s["h2"], gathered["w_ffn_gate"], gathered["w_ffn_up"],
                                          gathered["w_ffn_down"], l, bufs if "w_ffn_gate" in bufs else None,
                                          ride if early else None)
    bufs.update(filled)
    dx1, g["g_pre_ffn"] = _norm_bwd(None, s["x1"], w["g_pre_ffn"], "ffn_bwd_post", add=dx2, parts=dh_parts)
    if early:
        sums = {n: _add_halves(done[n], spans[n], r, half, n) for n, r in zip(others, got)}
        sums.update({n: _add_halves(filled[n], spans[n], got_ffn[n], half, n) for n in FFN_WEIGHTS})
        sums, dx1 = lax.optimization_barrier((sums, dx1))

    (dPg, dA, dBm, dC, dDv, g["g_post_mix"], delta_t), filled, got = _merge_bwd(
        dx1, s["o"], s["merged"], s["P"], s["A"], s["Bm"], s["C"], s["Dv"], w["w_br_a"], w["w_br_b"], w["w_br_c"], w["w_br_d"],
        w["w_out"], w["g_post_mix"], l, bufs if "w_out" in bufs else None, scatter(0))
    bufs.update(filled)
    slots.update(zip(RIDE_SETS[0], got))

    dPm, d_pool_bd, g_ps, g_gv, g["sgu_w"], d_bias, d_cw = _mixers_bwd(
        s["P"], D, S, dBm, dC, dDv, w["pool_bd"], w["pool_scale"], w["g_sgu_v"], w["sgu_w"], w["sgu_bias"], w["conv_w"])
    g["pool_w"] = jnp.stack([d_pool_bd[k * GROUP_DIM:(k + 1) * GROUP_DIM, k * GROUP_DIM:(k + 1) * GROUP_DIM] for k in range(GROUPS)])
    g["pool_scale"], g["g_sgu_v"] = g_ps, g_gv
    g["sgu_b"] = d_bias.reshape(SGU_BLOCK, GROUPS, GROUP_DIM).sum(-1).T
    g["conv_w"] = d_cw[:3].reshape(3, 1, MIX_W)

    dQ, dK, dV, got = _flash_bwd(s["Q"], s["K"], s["V"], dA, s["lse"], delta_t, S, scatter(1))
    slots.update(zip(RIDE_SETS[1], got))
    (dPa, g["g_cq"], g["g_ckv"]), filled = _mla_bwd_post(
        s["P"], D, S, dQ, dK, dV, s["hq"], s["hkv"], w["g_cq"], w["g_ckv"], w["w_uq"], w["w_ukv"], *rope, l,
        bufs if "w_uq" in bufs else None)
    bufs.update(filled)

    ride = scatter(2)
    if send_ffn:
        ride = _exchange_ride([bufs[n] for n in FFN_WEIGHTS], [(l, DEPTH)] * len(FFN_WEIGHTS))
    dx, g["g_pre_mix"], got = _proj_bwd(dx1, s["x"], w["g_pre_mix"], dPg, dPa, dPm, w["w_in_gates"], w["w_in_mla"],
                                        w["w_in_mix"], ride)
    if send_ffn:
        slots = dict(zip(FFN_WEIGHTS, got))
    else:
        slots.update(zip(RIDE_SETS[2], got))
    d_w_in = _unpad_w_in(_matmul_tn(s["h"], dPg, "wgrad_in_gates"), _matmul_tn(s["h"], dPa, "wgrad_in_mla"),
                         _matmul_tn(s["h"], dPm, "wgrad_in_mix"))
    g["w_in"] = d_w_in.reshape(D, N_CHIPS, -1).transpose(1, 0, 2)
    for n in ("g_pre_mix", "g_cq", "g_ckv", "pool_scale", "g_sgu_v", "g_post_mix", "g_pre_ffn", "g_post_ffn"):
        g[n] = g[n].reshape(-1)
    return dx, g, (sums, slots)


SMALL = REPLICATED + ("conv_w",)


def _local_step(x, target, gathered, full, core):
    n_seq, S, D = x.shape
    rope = _rope_tables(S)
    xs = x.reshape(n_seq * S, D)
    weights, saved = [], []
    for l in range(DEPTH):
        w = _layer_weights(gathered, full, l, D)
        gather_next = l + 1 < DEPTH
        if gather_next:
            w, gathered = lax.optimization_barrier((w, gathered))
        xs, s, gathered = _layer_fwd(xs, w, gathered, l, S, rope, gather_next)
        weights.append(w)
        saved.append(s)
    loss_parts, dx = _loss_grad(xs, target.reshape(n_seq * S, D))
    grads, bufs = [None] * DEPTH, {}
    for l in reversed(range(1, DEPTH)):
        dx, grads[l], (_, got_ffn) = _layer_bwd(dx, weights[l], gathered, l, saved[l], S, rope, bufs, send_ffn=l == 1)
    done = dict(bufs, w_in=jnp.stack([grads[l]["w_in"] for l in range(1, DEPTH)], axis=1))
    spans = dict({n: (1, DEPTH) for n in GATHERED}, w_in=(0, DEPTH - 1))
    dx, grads[0], early = _layer_bwd(dx, weights[0], gathered, 0, saved[0], S, rope, bufs, (done, spans, core, got_ffn))
    last = dict(bufs, w_in=grads[0]["w_in"][:, None])
    small = {n: jnp.stack([grads[l][n] for l in range(DEPTH)]) for n in SMALL}
    return loss_parts, dx.reshape(n_seq, S, D), early, last, small


def _unpad_reduced(n, r):
    L = r.shape[0]
    if n == "w_uq":
        return r.reshape(L, Q_LORA, 2, HEAD_PAD)[..., :QK_NOPE + QK_ROPE].reshape(L, Q_LORA, -1)
    if n == "w_ukv":
        r = r.reshape(L, KV_LORA, 2, 2, HEAD_PAD)[..., :QK_NOPE]
        return jnp.concatenate([r[:, :, 0], r[:, :, 1]], axis=-1).reshape(L, KV_LORA, -1)
    if n == "w_br_a":
        return r.reshape(L, HEADS, HEAD_PAD, -1)[:, :, :V_HEAD].reshape(L, HEADS * V_HEAD, -1)
    return r


def _small_rows(n):
    return -(-n // (8 * 128)) * 8


def _to_small(parts):
    flat = jnp.concatenate([p.reshape(-1) for p in parts])
    rows = _small_rows(flat.shape[0])
    return jnp.pad(flat, (0, rows * 128 - flat.shape[0])).reshape(rows, 128)


def kernel(x, w_in, g_pre_mix, g_cq, g_ckv, w_uq, w_ukv, pool_w, pool_scale, g_sgu_v, sgu_w, sgu_b, conv_w, w_br_a, w_br_b, w_br_c, w_br_d, w_out, g_post_mix, g_pre_ffn, w_ffn_gate, w_ffn_up, w_ffn_down, g_post_ffn, loss_target, m_w_in, m_g_pre_mix, m_g_cq, m_g_ckv, m_w_uq, m_w_ukv, m_pool_w, m_pool_scale, m_g_sgu_v, m_sgu_w, m_sgu_b, m_conv_w, m_w_br_a, m_w_br_b, m_w_br_c, m_w_br_d, m_w_out, m_g_post_mix, m_g_pre_ffn, m_w_ffn_gate, m_w_ffn_up, m_w_ffn_down, m_g_post_ffn, v_w_in, v_g_pre_mix, v_g_cq, v_g_ckv, v_w_uq, v_w_ukv, v_pool_w, v_pool_scale, v_g_sgu_v, v_sgu_w, v_sgu_b, v_conv_w, v_w_br_a, v_w_br_b, v_w_br_c, v_w_br_d, v_w_out, v_g_post_mix, v_g_pre_ffn, v_w_ffn_gate, v_w_ffn_up, v_w_ffn_down, v_g_post_ffn):
    local = dict(locals())
    W = {n: local[n] for n in WEIGHTS}
    M = {n: local["m_" + n] for n in WEIGHTS}
    V = {n: local["v_" + n] for n in WEIGHTS}
    chip = 2 * lax.axis_index("x") + lax.axis_index("y")
    core = lax.axis_index("c")

    gathered = dict(zip(GATHERED, _gather_weights([_place_own(W[n], chip, n) for n in GATHERED], 0)))
    conv_shape = conv_w.shape
    conv_cols = conv_shape[-1]
    conv_full_shape = conv_shape[:-1] + (N_CHIPS * conv_cols,)
    placed = lax.dynamic_update_slice(jnp.zeros(conv_full_shape, F32), conv_w, (0, 0, 0, chip * conv_cols))
    n_conv = int(np.prod(conv_full_shape))
    conv_sum = _all_reduce_small(_to_small([placed]), "gather_conv_w")
    full = {n: W[n] for n in REPLICATED}
    full["conv_w"] = 0.5 * conv_sum.reshape(-1)[:n_conv].reshape(conv_full_shape)

    loss_parts, grad_x, (sums_up, slots_up), last, small = _local_step(x, loss_target, gathered, full, core)
    loss = lax.psum(jnp.sum(loss_parts), ("x", "y", "c"))

    small_sum = _all_reduce_small(_to_small([small[n] for n in SMALL]), "reduce_small_grads")
    small_grads = dict(zip(SMALL, _unpack(small_sum, [small[n].shape for n in SMALL])))
    small_grads["conv_w"] = lax.dynamic_slice(small_grads["conv_w"], (0, 0, 0, chip * conv_cols), conv_shape)

    first = [(0, 1)] * len(GATHERED)
    Gs = [last[n] for n in GATHERED]
    got = _run_ride(_exchange_ride(Gs, first), "rs_exchange_halves")
    sums_0 = [_add_halves(g, (0, 1), r, core, n) for n, g, r in zip(GATHERED, Gs, got)]
    slots_0 = _run_ride(_scatter_ride(sums_0), "rs_scatter_partials")
    place = [lax.axis_index(a).reshape(1).astype(jnp.int32) for a in ("x", "y", "c")]
    halves = []
    for n, h, s in zip(GATHERED, sums_0, slots_0):
        upper = _sum_slots(sums_up[n], slots_up[n], place, n, 1, DEPTH)
        halves.append(_sum_slots(h, s, place, n, 0, DEPTH, prev=upper))
    shard_grads = {n: _unpad_reduced(n, r).reshape(W[n].shape) for n, r in zip(GATHERED, _join_halves(halves))}

    out_g, out_d, out_m, out_v = {}, {}, {}, {}
    for n in GATHERED:
        shp = W[n].shape
        flat = lambda a: a.reshape(-1, shp[-1])
        d, m2, v2 = _adamw(flat(W[n]), flat(shard_grads[n]), flat(M[n]), flat(V[n]), "adamw_" + n)
        out_g[n], out_d[n], out_m[n], out_v[n] = shard_grads[n], d.reshape(shp), m2.reshape(shp), v2.reshape(shp)
    rest_shapes = [W[n].shape for n in SMALL]
    d, m2, v2 = _adamw(_to_small([W[n] for n in SMALL]), _to_small([small_grads[n] for n in SMALL]),
                       _to_small([M[n] for n in SMALL]), _to_small([V[n] for n in SMALL]), "adamw_small")
    for n, dd, mm, vv in zip(SMALL, _unpack(d, rest_shapes), _unpack(m2, rest_shapes), _unpack(v2, rest_shapes)):
        out_g[n], out_d[n], out_m[n], out_v[n] = small_grads[n], dd, mm, vv

    return (loss, grad_x, *[out_g[n] for n in WEIGHTS], *[out_d[n] for n in WEIGHTS], *[out_m[n] for n in WEIGHTS],
            *[out_v[n] for n in WEIGHTS])
```

```python
import functools

import numpy as np
import jax
import jax.numpy as jnp
from jax import lax
from jax.experimental import pallas as pl
from jax.experimental.pallas import tpu as pltpu

F32 = jnp.float32
BF16 = jnp.bfloat16

EPS = 1e-6
NEG_INF = -1e30
DEPTH = 4
HEADS = 8
QK_NOPE = 64
QK_ROPE = 32
V_HEAD = 64
HEAD_PAD = 128
Q_LORA = 256
KV_LORA = 128
ROPE_THETA = 10000.0
POOL_WINDOWS = (2, 4, 8, 16)
GROUPS = 4
GROUP_DIM = 64
MIX_W = GROUPS * GROUP_DIM
POOL_HALO = 16
CONV_HALO = 16
SGU_BLOCK = 128
CHUNK = 64
CHUNK_SHIFT = 6
ACT_ROWS = 16
NORM_ROWS = 16
ROW_PARTS = 2
SOFTMAX_ROWS = 32
GROUP_SHIFT = 6
N_BRANCH = 4
MLA_W = HEADS * HEAD_PAD
N_CHIPS = 4
CHIP_HEADS_W = MLA_W // N_CHIPS
CHIP_KV = 2 * CHIP_HEADS_W
ATTN_SCALE = (QK_NOPE + QK_ROPE) ** -0.5
LOG2E = 1.4426950408889634
LN2 = 0.6931471805599453
SEC_MLA = Q_LORA + KV_LORA + HEAD_PAD
SEC_MIX = 6 * MIX_W

ADAM_LR = 0.001
ADAM_B1 = 0.9
ADAM_B2 = 0.999
ADAM_EPS = 1e-08
ADAM_WD = 0.01
ADAM_STEP = 10

VMEM_LIMIT = 56 * 1024 * 1024
MESH = pl.DeviceIdType.MESH


def _tile(n, pref, mult=8):
    t = min(n, pref)
    while t > 0:
        if n % t == 0 and t % mult == 0:
            return t
        t -= 1
    return n


def _params(n_axes):
    return pltpu.CompilerParams(dimension_semantics=("arbitrary",) * n_axes, vmem_limit_bytes=VMEM_LIMIT)


def _dot(a, b):
    return jnp.dot(a, b, preferred_element_type=F32)


def _dot_nt(a, b):
    return lax.dot_general(a, b, (((1,), (1,)), ((), ())), preferred_element_type=F32)


def _dot_tn(a, b):
    return lax.dot_general(a, b, (((0,), (0,)), ((), ())), preferred_element_type=F32)


def _rms_r(x):
    return lax.rsqrt(jnp.mean(x * x, axis=-1, keepdims=True) + EPS)


def _rms_bwd(dy, x, g):
    r = _rms_r(x)
    u = dy * g
    dx = r * u - x * (r * r * r * jnp.mean(u * x, axis=-1, keepdims=True))
    dg = jnp.sum(dy * x * r, axis=0, keepdims=True)
    return dx, dg


def _sigmoid(x):
    return 1.0 / (1.0 + jnp.exp(-x))


def _shift_down(a, k):
    return pltpu.roll(a, k, 0)


def _shift_up(a, k):
    return pltpu.roll(a, a.shape[0] - k, 0)


def _rope(x, c, sa, sb):
    w = x.shape[-1]
    return x * c + pltpu.roll(x, QK_ROPE // 2, 1) * sa + pltpu.roll(x, w - QK_ROPE // 2, 1) * sb


def _rope_t(d, c, sa, sb):
    w = d.shape[-1]
    return d * c + pltpu.roll(d * sa, w - QK_ROPE // 2, 1) + pltpu.roll(d * sb, QK_ROPE // 2, 1)


def _full(shape):
    return pl.BlockSpec(shape, lambda *_: (0,) * len(shape))


def _gather_ride(bufs, l, stage):
    def copies(_, o_refs, send_sems, recv_sems):
        x, y, c, me = _place()
        sends, arrivals = [], []
        for t, o in enumerate(o_refs):
            hr = o.shape[2] // 2
            for d in (1, 2, 3):
                sems = (send_sems.at[3 * t + d - 1], recv_sems.at[3 * t + d - 1])
                mine = o.at[l, me, pl.ds(c * hr, hr), :]
                theirs = o.at[l, me ^ d, pl.ds(c * hr, hr), :]
                other_half = o.at[l, me ^ d, pl.ds((1 - c) * hr, hr), :]
                if stage == "chips":
                    sends.append(_remote(mine, mine, *sems, _chip_device(me ^ d, c)))
                    arrivals.append(_remote(theirs, theirs, *sems, _chip_device(me ^ d, c)))
                else:
                    sends.append(_remote(theirs, theirs, *sems, (x, y, 1 - c)))
                    arrivals.append(_remote(other_half, other_half, *sems, (x, y, 1 - c)))
        return sends, arrivals

    shapes = [jax.ShapeDtypeStruct(b.shape, b.dtype) for b in bufs]
    return dict(ins=list(bufs), outs=shapes, alias=True, copies=copies, n_sems=3 * len(bufs))


def _exchange_ride(Gs, spans):
    def copies(g_refs, o_refs, send_sems, recv_sems):
        x, y, c, _ = _place()
        cps = []
        for t, (g, o) in enumerate(zip(g_refs, o_refs)):
            hr = g.shape[2] // 2
            l0, l1 = spans[t]
            cps.append(_remote(g.at[:, pl.ds(l0, l1 - l0), pl.ds((1 - c) * hr, hr), :], o, send_sems.at[t], recv_sems.at[t],
                               (x, y, 1 - c)))
        return cps, cps

    shapes = [jax.ShapeDtypeStruct((g.shape[0], l1 - l0, g.shape[2] // 2, g.shape[3]), g.dtype) for g, (l0, l1) in zip(Gs, spans)]
    return dict(ins=list(Gs), outs=shapes, alias=False, copies=copies, n_sems=len(Gs))


def _scatter_ride(Hs):
    def copies(h_refs, o_refs, send_sems, recv_sems):
        x, y, c, me = _place()
        sends, arrivals = [], []
        for t, (h, o) in enumerate(zip(h_refs, o_refs)):
            for d in (1, 2, 3):
                sems = (send_sems.at[3 * t + d - 1], recv_sems.at[3 * t + d - 1])
                sends.append(_remote(h.at[me ^ d], o.at[me], *sems, _chip_device(me ^ d, c)))
                arrivals.append(_remote(h.at[me ^ d], o.at[me ^ d], *sems, _chip_device(me ^ d, c)))
        return sends, arrivals

    shapes = [jax.ShapeDtypeStruct(h.shape, h.dtype) for h in Hs]
    return dict(ins=list(Hs), outs=shapes, alias=False, copies=copies, n_sems=3 * len(Hs))


def _ride_specs(ride, n_in, n_out):
    if not ride:
        return [], [], [], [], [], {}
    anywhere = pl.BlockSpec(memory_space=pl.ANY)
    sems = pltpu.SemaphoreType.DMA((ride["n_sems"],))
    alias = {n_in + i: n_out + i for i in range(len(ride["ins"]))} if ride["alias"] else {}
    return [anywhere] * len(ride["ins"]), [anywhere] * len(ride["outs"]), list(ride["outs"]), [sems, sems], ride["ins"], alias


def _ride_split(ride, rest, n_out, n_scratch):
    a = len(ride["ins"]) if ride else 0
    b = a + n_out
    c = b + (len(ride["outs"]) if ride else 0)
    d = c + n_scratch
    riding = (ride, rest[:a], rest[b:c], rest[d:]) if ride else None
    return rest[a:b], rest[c:d], riding


def _ride_start(riding, first):
    if riding:
        ride, in_refs, out_refs, (send_sems, recv_sems) = riding

        @pl.when(first)
        def _():
            for cp in ride["copies"](in_refs, out_refs, send_sems, recv_sems)[0]:
                cp.start()


def _ride_finish(riding, last):
    if riding:
        ride, in_refs, out_refs, (send_sems, recv_sems) = riding

        @pl.when(last)
        def _():
            sends, arrivals = ride["copies"](in_refs, out_refs, send_sems, recv_sems)
            for cp in arrivals:
                cp.wait_recv()
            for cp in sends:
                cp.wait_send()


def _run_ride(ride, name):
    def body(*refs):
        _, _, (_, in_refs, out_refs, (send_sems, recv_sems)) = _ride_split(ride, refs, 0, 0)
        sends, arrivals = ride["copies"](in_refs, out_refs, send_sems, recv_sems)
        for cp in sends:
            cp.start()
        for cp in arrivals:
            cp.wait_recv()
        for cp in sends:
            cp.wait_send()

    in_specs, out_specs, out_shapes, scratch, operands, alias = _ride_specs(ride, 0, 0)
    return list(pl.pallas_call(body, name=name, in_specs=in_specs, out_specs=out_specs, out_shape=out_shapes,
                               input_output_aliases=alias, scratch_shapes=scratch)(*operands))


def _norm_matmul(x, g, w, name):
    T, K = x.shape
    N = w.shape[1]
    tm, tn = _tile(T, 1024), _tile(N, 1536, 128)

    def body(x_ref, g_ref, w_ref, o_ref, h_ref):
        @pl.when(pl.program_id(1) == 0)
        def _():
            xv = x_ref[...]
            h_ref[...] = (xv * _rms_r(xv) * g_ref[...]).astype(BF16)

        o_ref[...] = _dot(h_ref[...], w_ref[...]).astype(BF16)

    return pl.pallas_call(
        body, name=name, grid=(T // tm, N // tn),
        in_specs=[pl.BlockSpec((tm, K), lambda i, j: (i, 0)), _full((1, K)), pl.BlockSpec((K, tn), lambda i, j: (0, j))],
        out_specs=[pl.BlockSpec((tm, tn), lambda i, j: (i, j)), pl.BlockSpec((tm, K), lambda i, j: (i, 0))],
        out_shape=[jax.ShapeDtypeStruct((T, N), BF16), jax.ShapeDtypeStruct((T, K), BF16)],
        compiler_params=_params(2),
    )(x, g, w)


def _mla_prep(P, D, g_cq, g_ckv, wuq, wukv, rope_c, rope_sa, rope_sb, S):
    T = P.shape[0]
    tm = _tile(S, 512)
    n_si = S // tm
    base = 4 * D

    def body(cq_ref, ckv_ref, kr_ref, gq_ref, gkv_ref, wq_ref, wkv_ref, c_ref, sa_ref, sb_ref,
             q_ref, k_ref, v_ref, hq_ref, hkv_ref):
        c, sa, sb = c_ref[...], sa_ref[...], sb_ref[...]
        cq = cq_ref[...].astype(F32)
        hq = (cq * _rms_r(cq) * gq_ref[...]).astype(BF16)
        hq_ref[...] = hq
        q = _dot(hq, wq_ref[...])
        q = _rope(q, jnp.tile(c, (1, HEADS)), jnp.tile(sa, (1, HEADS)), jnp.tile(sb, (1, HEADS)))
        q_ref[...] = (q * (ATTN_SCALE * LOG2E)).astype(BF16)
        ckv = ckv_ref[...].astype(F32)
        hkv = (ckv * _rms_r(ckv) * gkv_ref[...]).astype(BF16)
        hkv_ref[...] = hkv
        kv = _dot(hkv, wkv_ref[...])
        kr = _rope(kr_ref[...].astype(F32), c, sa, sb)
        k_nope = jnp.concatenate([kv[:, j * CHIP_KV:j * CHIP_KV + CHIP_HEADS_W] for j in range(N_CHIPS)], axis=1)
        k_ref[...] = (k_nope + jnp.tile(kr, (1, HEADS))).astype(BF16)
        v = jnp.concatenate([kv[:, j * CHIP_KV + CHIP_HEADS_W:(j + 1) * CHIP_KV] for j in range(N_CHIPS)], axis=1)
        ones_lane = (lax.broadcasted_iota(jnp.int32, (1, MLA_W), 1) & (HEAD_PAD - 1)) == V_HEAD
        v_ref[...] = jnp.where(ones_lane, 1.0, v).astype(BF16)

    tab = pl.BlockSpec((tm, HEAD_PAD), lambda i: (i % n_si, 0))
    row = lambda w: pl.BlockSpec((tm, w), lambda i: (i, 0))
    return pl.pallas_call(
        body, name="mla_prep", grid=(T // tm,),
        in_specs=[pl.BlockSpec((tm, Q_LORA), lambda i: (i, base // Q_LORA)),
                  pl.BlockSpec((tm, KV_LORA), lambda i: (i, (base + Q_LORA) // KV_LORA)),
                  pl.BlockSpec((tm, HEAD_PAD), lambda i: (i, (base + Q_LORA + KV_LORA) // HEAD_PAD)),
                  _full((1, Q_LORA)), _full((1, KV_LORA)), _full((Q_LORA, MLA_W)), _full((KV_LORA, 2 * MLA_W)),
                  tab, tab, tab],
        out_specs=[row(MLA_W), row(MLA_W), row(MLA_W), row(Q_LORA), row(KV_LORA)],
        out_shape=[jax.ShapeDtypeStruct((T, MLA_W), BF16)] * 3
        + [jax.ShapeDtypeStruct((T, Q_LORA), BF16), jax.ShapeDtypeStruct((T, KV_LORA), BF16)],
        compiler_params=_params(1),
    )(P, P, P, g_cq, g_ckv, wuq, wukv, rope_c, rope_sa, rope_sb)


def _chunk_mask(tq, tk):
    row = lax.broadcasted_iota(jnp.int32, (tq, tk), 0)
    col = lax.broadcasted_iota(jnp.int32, (tq, tk), 1)
    return (row >> CHUNK_SHIFT) >= (col >> CHUNK_SHIFT)


def _rows_to_lanes(stats, out_ref):
    t = stats.T
    for h in range(HEADS):
        out_ref[h, 0] = t[h:h + 1, :]


def _flash_fwd(Q, K, V, S, ride=None):
    T = Q.shape[0]
    n_seq = T // S
    tq = _tile(S, 256, 128)
    nq = S // tq

    def body(q_ref, k_ref, v_ref, *rest):
        (o_ref, lse_ref), (m_s, acc_s, s_s, p_s, a_s), riding = _ride_split(ride, rest, 2, 5)
        _ride_start(riding, (pl.program_id(0) == 0) & (pl.program_id(1) == 0))
        qi = pl.program_id(1)
        m_s[...] = jnp.full(m_s.shape, NEG_INF, F32)
        acc_s[...] = jnp.zeros_like(acc_s)

        def block(kb, masked):
            rows = pl.ds(pl.multiple_of(kb * tq, tq), tq)
            for h in range(HEADS):
                hs = slice(h * HEAD_PAD, (h + 1) * HEAD_PAD)
                s_s[h] = _dot_nt(q_ref[:, hs], k_ref[rows, hs])
            def softmax_head(h):
                for r in range(0, tq, SOFTMAX_ROWS):
                    rs = slice(r, r + SOFTMAX_ROWS)
                    s = s_s[h, rs, :]
                    if masked:
                        row = r + lax.broadcasted_iota(jnp.int32, (SOFTMAX_ROWS, tq), 0)
                        col = lax.broadcasted_iota(jnp.int32, (SOFTMAX_ROWS, tq), 1)
                        s = jnp.where((row >> CHUNK_SHIFT) >= (col >> CHUNK_SHIFT), s, NEG_INF)
                    m_old = m_s[h, rs]
                    m_new = jnp.maximum(m_old, jnp.max(s, axis=-1, keepdims=True))
                    m_s[h, rs] = m_new
                    a_s[h, rs] = jnp.exp2(m_old - m_new)
                    for half in range(tq // HEAD_PAD):
                        cs = slice(half * HEAD_PAD, (half + 1) * HEAD_PAD)
                        p_s[h, rs, cs] = jnp.exp2(s[:, cs] - m_new).astype(BF16)

            for h in range(HEADS):
                softmax_head(h)
            for h in range(HEADS):
                hs = slice(h * HEAD_PAD, (h + 1) * HEAD_PAD)
                acc_s[:, hs] = a_s[h] * acc_s[:, hs] + _dot(p_s[h], v_ref[rows, hs])

        def full_block(kb, carry):
            block(kb, False)
            return carry

        lax.fori_loop(0, qi, full_block, 0)
        block(qi, True)
        lane = lax.broadcasted_iota(jnp.int32, (tq, HEAD_PAD), 1)
        lse_all = jnp.zeros((tq, HEAD_PAD), F32)
        for h in range(HEADS):
            hs = slice(h * HEAD_PAD, (h + 1) * HEAD_PAD)
            acc = acc_s[:, hs]
            l = jnp.sum(jnp.where(lane == V_HEAD, acc, 0.0), axis=-1, keepdims=True)
            o_ref[:, hs] = (acc / l).astype(BF16)
            lse_all = jnp.where(lane == h, m_s[h] + jnp.log2(l), lse_all)
        _rows_to_lanes(lse_all, lse_ref)
        _ride_finish(riding, (pl.program_id(0) == n_seq - 1) & (pl.program_id(1) == nq - 1))

    ride_in, ride_out, ride_shapes, ride_scratch, ride_args, alias = _ride_specs(ride, 3, 2)
    out = pl.pallas_call(
        body, name="flash_fwd", grid=(n_seq, nq),
        in_specs=[pl.BlockSpec((tq, MLA_W), lambda b, i: (b * nq + i, 0)),
                  pl.BlockSpec((S, MLA_W), lambda b, i: (b, 0)), pl.BlockSpec((S, MLA_W), lambda b, i: (b, 0))] + ride_in,
        out_specs=[pl.BlockSpec((tq, MLA_W), lambda b, i: (b * nq + i, 0)),
                   pl.BlockSpec((HEADS, 1, 1, tq), lambda b, i: (0, b * nq + i, 0, 0))] + ride_out,
        out_shape=[jax.ShapeDtypeStruct((T, MLA_W), BF16), jax.ShapeDtypeStruct((HEADS, T // tq, 1, tq), F32)] + ride_shapes,
        input_output_aliases=alias,
        scratch_shapes=[pltpu.VMEM((HEADS, tq, HEAD_PAD), F32), pltpu.VMEM((tq, MLA_W), F32), pltpu.VMEM((HEADS, tq, tq), F32),
                        pltpu.VMEM((HEADS, tq, tq), BF16), pltpu.VMEM((HEADS, tq, HEAD_PAD), F32)] + ride_scratch,
        compiler_params=_params(2),
    )(Q, K, V, *ride_args)
    return out[0], out[1], list(out[2:])


def _lane_group():
    return lax.broadcasted_iota(jnp.int32, (1, MIX_W), 1) >> GROUP_SHIFT


def _by_group(a0, a1, a2, a3):
    g = _lane_group()
    return jnp.where(g == 0, a0, jnp.where(g == 1, a1, jnp.where(g == 2, a2, a3)))


def _pool_count(si, tc, rows):
    pos = si * tc + lax.broadcasted_iota(jnp.int32, (rows, MIX_W), 0)
    win = _by_group(*POOL_WINDOWS)
    return jnp.minimum(pos + 1, win).astype(F32)


def _pool_fwd(z, z_prev, si, tc):
    ze = jnp.concatenate([z_prev, z], axis=0)
    s1 = ze + _shift_down(ze, 1)
    s2 = s1 + _shift_down(s1, 2)
    s4 = s2 + _shift_down(s2, 4)
    s8 = s4 + _shift_down(s4, 8)
    win_sum = _by_group(s1, s2, s4, s8)[POOL_HALO:]
    return win_sum / _pool_count(si, tc, tc) - z


def _sgu_weights(w_ref):
    row = lax.broadcasted_iota(jnp.int32, (SGU_BLOCK, SGU_BLOCK), 0)
    col = lax.broadcasted_iota(jnp.int32, (SGU_BLOCK, SGU_BLOCK), 1)
    keep = (row >> CHUNK_SHIFT) >= (col >> CHUNK_SHIFT)
    return keep, [jnp.where(keep, w_ref[g], 0.0).astype(BF16) for g in range(GROUPS)]


def _sgu_mix(vn_blk, wm, bias):
    g = _lane_group()
    mixed = bias
    for k in range(GROUPS):
        mixed = mixed + jnp.where(g == k, _dot(wm[k], vn_blk), 0.0)
    return mixed


def _conv_fwd(z, z_prev, w_ref):
    ze = jnp.concatenate([z_prev, z], axis=0)
    y = w_ref[0:1, :] * _shift_down(ze, 2) + w_ref[1:2, :] * _shift_down(ze, 1) + w_ref[2:3, :] * ze
    return y[CONV_HALO:]


def _mix_specs(T, D, tc):
    base = (4 * D + SEC_MLA) // MIX_W
    cur = lambda k: pl.BlockSpec((tc, MIX_W), lambda i: (i, base + k))
    prev = lambda k, halo: pl.BlockSpec((halo, MIX_W), lambda i: (jnp.maximum(i * (tc // halo) - 1, 0), base + k))
    nxt = lambda k, halo: pl.BlockSpec((halo, MIX_W), lambda i: (jnp.minimum((i + 1) * (tc // halo), T // halo - 1), base + k))
    return cur, prev, nxt


def _mixers_fwd(P, D, S, pool_bd, pool_scale, g_v, sgu_w, sgu_bias, conv_w, ride=None):
    T = P.shape[0]
    tc = _tile(S, 512, SGU_BLOCK)
    n_si = S // tc
    cur, prev, _ = _mix_specs(T, D, tc)

    def body(z_ref, zp_ref, u_ref, v_ref, b_ref, c_ref, x_ref, cp_ref, xp_ref,
             pw_ref, ps_ref, gv_ref, sw_ref, sb_ref, cw_ref, *rest):
        (ob_ref, oc_ref, od_ref), _, riding = _ride_split(ride, rest, 3, 0)
        _ride_start(riding, pl.program_id(0) == 0)
        si = pl.program_id(0) % n_si
        first = si == 0
        z = z_ref[...].astype(F32)
        pooled = _pool_fwd(z, jnp.where(first, 0.0, zp_ref[...].astype(F32)), si, tc)
        ob_ref[...] = (_dot(pooled.astype(BF16), pw_ref[...]) * ps_ref[...]).astype(BF16)

        v = v_ref[...].astype(F32)
        vn = (v * _rms_r(v) * gv_ref[...]).astype(BF16)
        _, wm = _sgu_weights(sw_ref)
        for blk in range(tc // SGU_BLOCK):
            rows = slice(blk * SGU_BLOCK, (blk + 1) * SGU_BLOCK)
            oc_ref[rows, :] = (u_ref[rows, :].astype(F32) * _sgu_mix(vn[rows], wm, sb_ref[...])).astype(BF16)

        zc = c_ref[...].astype(F32) * x_ref[...].astype(F32)
        zc_prev = jnp.where(first, 0.0, cp_ref[...].astype(F32) * xp_ref[...].astype(F32))
        od_ref[...] = (b_ref[...].astype(F32) * _conv_fwd(zc, zc_prev, cw_ref)).astype(BF16)
        _ride_finish(riding, pl.program_id(0) == T // tc - 1)

    out = pl.BlockSpec((tc, MIX_W), lambda i: (i, 0))
    ride_in, ride_out, ride_shapes, ride_scratch, ride_args, alias = _ride_specs(ride, 15, 3)
    res = pl.pallas_call(
        body, name="mixers_fwd", grid=(T // tc,),
        in_specs=[cur(0), prev(0, POOL_HALO), cur(1), cur(2), cur(3), cur(4), cur(5), prev(4, CONV_HALO), prev(5, CONV_HALO),
                  _full((MIX_W, MIX_W)), _full((1, MIX_W)), _full((1, MIX_W)), _full((GROUPS, SGU_BLOCK, SGU_BLOCK)),
                  _full((SGU_BLOCK, MIX_W)), _full((CONV_HALO, MIX_W))] + ride_in,
        out_specs=[out, out, out] + ride_out,
        out_shape=[jax.ShapeDtypeStruct((T, MIX_W), BF16)] * 3 + ride_shapes,
        input_output_aliases=alias,
        scratch_shapes=ride_scratch,
        compiler_params=_params(1),
    )(P, P, P, P, P, P, P, P, P, pool_bd, pool_scale, g_v, sgu_w, sgu_bias, conv_w, *ride_args)
    return res[0], res[1], res[2], list(res[3:])


def _merge_fwd(x, P, A, Bm, C, Dv, wa, wb, wc, wd, wout, g_post):
    T, D = x.shape
    tm = _tile(T, 256)

    def body(x_ref, lg_ref, a_ref, b_ref, c_ref, d_ref, wa_ref, wb_ref, wc_ref, wd_ref, wo_ref, g_ref,
             x1_ref, mg_ref, o_ref):
        merged = jnp.zeros((tm, D), F32)
        for k, (br, w) in enumerate(((a_ref, wa_ref), (b_ref, wb_ref), (c_ref, wc_ref), (d_ref, wd_ref))):
            merged = merged + _sigmoid(lg_ref[:, k * D:(k + 1) * D].astype(F32)) * _dot(br[...], w[...])
        mg = merged.astype(BF16)
        mg_ref[...] = mg
        o = _dot(mg, wo_ref[...])
        o_ref[...] = o
        x1_ref[...] = x_ref[...] + o * _rms_r(o) * g_ref[...]

    row = lambda w: pl.BlockSpec((tm, w), lambda i: (i, 0))
    return pl.pallas_call(
        body, name="merge_fwd", grid=(T // tm,),
        in_specs=[row(D), row(4 * D), row(MLA_W), row(MIX_W), row(MIX_W), row(MIX_W),
                  _full((MLA_W, D)), _full((MIX_W, D)), _full((MIX_W, D)), _full((MIX_W, D)), _full((D, D)), _full((1, D))],
        out_specs=[row(D), row(D), row(D)],
        out_shape=[jax.ShapeDtypeStruct((T, D), F32), jax.ShapeDtypeStruct((T, D), BF16), jax.ShapeDtypeStruct((T, D), F32)],
        compiler_params=_params(1),
    )(x, P, A, Bm, C, Dv, wa, wb, wc, wd, wout, g_post)


def _ffn_specs(T, D, Fc, l, rows=512):
    tm = _tile(T, rows)
    row = pl.BlockSpec((tm, D), lambda i, j: (i, 0))
    col = pl.BlockSpec((None, tm, Fc), lambda i, j: (j, i, 0))
    w_in = pl.BlockSpec((None, None, D, Fc), lambda i, j: (l, j, 0, 0))
    w_out = pl.BlockSpec((None, None, Fc, D), lambda i, j: (l, j, 0, 0))
    return tm, row, col, w_in, w_out


def _ffn_fwd(x1, g_pre, wg, wu, wdn, g_post, l):
    T, D = x1.shape
    nf, Fc = wg.shape[1], wg.shape[3]
    tm, row, col, w_in, w_out = _ffn_specs(T, D, Fc, l, 1024)

    def body(x_ref, gp_ref, wg_ref, wu_ref, wd_ref, gq_ref, x2_ref, h_ref, gt_ref, up_ref, f_ref, gt_s, up_s, a_s):
        j = pl.program_id(1)

        @pl.when(j == 0)
        def _():
            for r in range(0, tm, NORM_ROWS):
                rs = slice(r, r + NORM_ROWS)
                xv = x_ref[rs, :]
                h_ref[rs, :] = (xv * _rms_r(xv) * gp_ref[...]).astype(BF16)
            f_ref[...] = jnp.zeros_like(f_ref)

        gt_s[...] = _dot(h_ref[...], wg_ref[...])
        up_s[...] = _dot(h_ref[...], wu_ref[...])
        for r in range(0, tm, ACT_ROWS):
            rs = slice(r, r + ACT_ROWS)
            gt, up = gt_s[rs, :], up_s[rs, :]
            gt_ref[rs, :] = gt.astype(BF16)
            up_ref[rs, :] = up.astype(BF16)
            a_s[rs, :] = (gt * _sigmoid(gt) * up).astype(BF16)
        f_ref[...] += _dot(a_s[...], wd_ref[...])

        @pl.when(j == nf - 1)
        def _():
            for r in range(0, tm, NORM_ROWS):
                rs = slice(r, r + NORM_ROWS)
                f = f_ref[rs, :]
                x2_ref[rs, :] = x_ref[rs, :] + f * _rms_r(f) * gq_ref[...]

    return pl.pallas_call(
        body, name="ffn_fwd", grid=(T // tm, nf),
        in_specs=[row, _full((1, D)), w_in, w_in, w_out, _full((1, D))],
        out_specs=[row, row, col, col, row],
        out_shape=[jax.ShapeDtypeStruct((T, D), F32), jax.ShapeDtypeStruct((T, D), BF16),
                   jax.ShapeDtypeStruct((nf, T, Fc), BF16), jax.ShapeDtypeStruct((nf, T, Fc), BF16),
                   jax.ShapeDtypeStruct((T, D), F32)],
        scratch_shapes=[pltpu.VMEM((tm, Fc), F32), pltpu.VMEM((tm, Fc), F32), pltpu.VMEM((tm, Fc), BF16)],
        compiler_params=_params(2),
    )(x1, g_pre, wg, wu, wdn, g_post)


def _loss_grad(y, target):
    T, D = y.shape
    tm = _tile(T, 512)

    def body(y_ref, t_ref, l_ref, dy_ref):
        @pl.when(pl.program_id(0) == 0)
        def _():
            l_ref[...] = jnp.zeros_like(l_ref)

        d = y_ref[...] - t_ref[...]
        dy_ref[...] = d * (1.0 / D)
        e = jnp.sum((d * d).reshape(tm // 8, 8, D), axis=0)
        part = e[:, 0:128]
        for k in range(1, D // 128):
            part = part + e[:, k * 128:(k + 1) * 128]
        l_ref[...] += part * (0.5 / D)

    row = pl.BlockSpec((tm, D), lambda i: (i, 0))
    return pl.pallas_call(
        body, name="loss_grad", grid=(T // tm,),
        in_specs=[row, row], out_specs=[_full((8, 128)), row],
        out_shape=[jax.ShapeDtypeStruct((8, 128), F32), jax.ShapeDtypeStruct((T, D), F32)],
        compiler_params=_params(1),
    )(y, target)


def _matmul_tn(a, b, name):
    T, M = a.shape
    N = b.shape[1]
    tm, tn, tk = _tile(M, 1024, 128), _tile(N, 1536, 128), _tile(T, 1024)

    def body(a_ref, b_ref, o_ref):
        @pl.when(pl.program_id(2) == 0)
        def _():
            o_ref[...] = jnp.zeros_like(o_ref)

        o_ref[...] += _dot_tn(a_ref[...], b_ref[...])

    return pl.pallas_call(
        body, name=name, grid=(M // tm, N // tn, T // tk),
        in_specs=[pl.BlockSpec((tk, tm), lambda i, j, k: (k, i)), pl.BlockSpec((tk, tn), lambda i, j, k: (k, j))],
        out_specs=pl.BlockSpec((tm, tn), lambda i, j, k: (i, j)),
        out_shape=jax.ShapeDtypeStruct((M, N), F32),
        compiler_params=_params(3),
    )(a, b)


def _norm_bwd(dy, x, g, name, add=None, parts=None):
    T, D = x.shape
    tm = _tile(T, 256)
    n_parts = 0 if parts is None else parts.shape[0]

    def body(*refs):
        dx_ref, dg_ref = refs[-2:]
        ins = list(refs[:-2])
        dy_ref = ins.pop(0) if parts is None else None
        p_ref = ins.pop(0) if parts is not None else None
        x_ref, g_ref = ins[0], ins[1]
        add_ref = ins[2] if add is not None else None

        @pl.when(pl.program_id(0) == 0)
        def _():
            dg_ref[...] = jnp.zeros_like(dg_ref)

        dg_sum = jnp.zeros((1, D), F32)
        for r in range(0, tm, NORM_ROWS):
            rs = slice(r, r + NORM_ROWS)
            if parts is None:
                dy = dy_ref[rs, :]
            else:
                dy = p_ref[0, rs, :].astype(F32)
                for k in range(1, n_parts):
                    dy = dy + p_ref[k, rs, :].astype(F32)
            dx, dg = _rms_bwd(dy, x_ref[rs, :], g_ref[...])
            dx_ref[rs, :] = (dx if add is None else add_ref[rs, :] + dx).astype(dx_ref.dtype)
            dg_sum = dg_sum + dg
        dg_ref[...] += dg_sum

    row = pl.BlockSpec((tm, D), lambda i: (i, 0))
    first = [row] if parts is None else [pl.BlockSpec((n_parts, tm, D), lambda i: (0, i, 0))]
    operands = [dy if parts is None else parts, x, g] + ([] if add is None else [add])
    return pl.pallas_call(
        body, name=name, grid=(T // tm,),
        in_specs=first + [row, _full((1, D))] + ([] if add is None else [row]),
        out_specs=[row, _full((1, D))],
        out_shape=[jax.ShapeDtypeStruct((T, D), BF16 if add is None else F32), jax.ShapeDtypeStruct((1, D), F32)],
        compiler_params=_params(1),
    )(*operands)


FFN_WEIGHTS = ("w_ffn_gate", "w_ffn_up", "w_ffn_down")


def _ffn_bwd_main(df, gt, up, h2, wg, wu, wdn, l, bufs, ride=None):
    T, D = h2.shape
    nf, Fc = wg.shape[1], wg.shape[3]
    tm = _tile(T, 512)
    nt = T // tm
    keep = [] if bufs is None else [bufs[n] for n in FFN_WEIGHTS]

    def body(df_ref, gt_ref, up_ref, h_ref, wg_ref, wu_ref, wd_ref, *rest):
        (dh_ref, gg_ref, gu_ref, gd_ref), (da_s, dgt_s, dup_s, act_s), riding = _ride_split(ride, rest[len(keep):], 4, 4)
        j, i = pl.program_id(0), pl.program_id(1)
        _ride_start(riding, (j == 0) & (i == 0))

        @pl.when(i == 0)
        def _():
            for r in (gg_ref, gu_ref, gd_ref):
                r[...] = jnp.zeros_like(r)

        parts = [slice(p * tm // ROW_PARTS, (p + 1) * tm // ROW_PARTS) for p in range(ROW_PARTS)]
        for ps in parts:
            da_s[ps, :] = _dot_nt(df_ref[ps, :], wd_ref[...])
        for ps in parts:
            for r in range(ps.start, ps.stop, ACT_ROWS):
                rs = slice(r, r + ACT_ROWS)
                da = da_s[rs, :]
                gt = gt_ref[rs, :].astype(F32)
                u = up_ref[rs, :].astype(F32)
                sig = _sigmoid(gt)
                silu = gt * sig
                dgt_s[rs, :] = (da * u * (sig * (1.0 + gt * (1.0 - sig)))).astype(BF16)
                dup_s[rs, :] = (da * silu).astype(BF16)
                act_s[rs, :] = (silu * u).astype(BF16)
            dh_ref[ps, :] = (_dot_nt(dgt_s[ps, :], wg_ref[...]) + _dot_nt(dup_s[ps, :], wu_ref[...])).astype(BF16)
        gg_ref[...] += _dot_tn(h_ref[...], dgt_s[...])
        gu_ref[...] += _dot_tn(h_ref[...], dup_s[...])
        gd_ref[...] += _dot_tn(act_s[...], df_ref[...])
        _ride_finish(riding, (j == nf - 1) & (i == nt - 1))

    row = pl.BlockSpec((tm, D), lambda j, i: (i, 0))
    col = pl.BlockSpec((None, tm, Fc), lambda j, i: (j, i, 0))
    w_in = pl.BlockSpec((None, None, D, Fc), lambda j, i: (l, j, 0, 0))
    w_out = pl.BlockSpec((None, None, Fc, D), lambda j, i: (l, j, 0, 0))
    g_in = pl.BlockSpec((None, None, D, Fc), lambda j, i: (j, l, 0, 0))
    g_out = pl.BlockSpec((None, None, Fc, D), lambda j, i: (j, l, 0, 0))
    n_in = 7
    ride_in, ride_out, ride_shapes, ride_scratch, ride_args, alias = _ride_specs(ride, n_in + len(keep), 4)
    out = pl.pallas_call(
        body, name="ffn_bwd_main", grid=(nf, nt),
        in_specs=[row, col, col, row, w_in, w_in, w_out] + [pl.BlockSpec(memory_space=pl.ANY)] * len(keep) + ride_in,
        out_specs=[pl.BlockSpec((None, tm, D), lambda j, i: (j, i, 0)), g_in, g_in, g_out] + ride_out,
        out_shape=[jax.ShapeDtypeStruct((nf, T, D), BF16), jax.ShapeDtypeStruct((nf, DEPTH, D, Fc), F32),
                   jax.ShapeDtypeStruct((nf, DEPTH, D, Fc), F32), jax.ShapeDtypeStruct((nf, DEPTH, Fc, D), F32)] + ride_shapes,
        input_output_aliases={**{n_in + k: 1 + k for k in range(len(keep))}, **alias},
        scratch_shapes=[pltpu.VMEM((tm, Fc), F32), pltpu.VMEM((tm, Fc), BF16), pltpu.VMEM((tm, Fc), BF16),
                        pltpu.VMEM((tm, Fc), BF16)] + ride_scratch,
        compiler_params=_params(2),
    )(df, gt, up, h2, wg, wu, wdn, *keep, *ride_args)
    return out[0], dict(zip(FFN_WEIGHTS, out[1:4])), list(out[4:])


def _wgrad_out(l, n_layers, rows, cols):
    spec = pl.BlockSpec((N_CHIPS, None, rows, cols), lambda *_: (0, l, 0, 0))
    return spec, jax.ShapeDtypeStruct((N_CHIPS, n_layers, rows, cols), F32)


def _merge_bwd(dx1, o, merged, P, A, Bm, C, Dv, wa, wb, wc, wd, wout, g_post, l, bufs, ride=None):
    T, D = o.shape
    tm = _tile(T, 256)
    Dc = D // N_CHIPS
    names = ("w_out", "w_br_a", "w_br_b", "w_br_c", "w_br_d")
    keep = [] if bufs is None else [bufs[n] for n in names]

    def body(dx1_ref, o_ref, mg_ref, lg_ref, a_ref, b_ref, c_ref, d_ref, wa_ref, wb_ref, wc_ref, wd_ref, wo_ref, g_ref, *rest):
        outs, _, riding = _ride_split(ride, rest[len(keep):], 12, 0)
        dlg_ref, da_ref, db_ref, dc_ref, dd_ref, dg_ref, dt_ref, go_ref, ga_ref, gb_ref, gc_ref, gd_ref = outs
        _ride_start(riding, pl.program_id(0) == 0)

        @pl.when(pl.program_id(0) == 0)
        def _():
            for r in (dg_ref, go_ref, ga_ref, gb_ref, gc_ref, gd_ref):
                r[...] = jnp.zeros_like(r)

        d_o, dg = _rms_bwd(dx1_ref[...], o_ref[...], g_ref[...])
        dg_ref[...] += dg
        d_o = d_o.astype(BF16)
        for k in range(N_CHIPS):
            go_ref[k] += _dot_tn(mg_ref[:, k * Dc:(k + 1) * Dc], d_o)
        dm = _dot_nt(d_o, wo_ref[...])
        branches = ((a_ref, wa_ref, da_ref, ga_ref), (b_ref, wb_ref, db_ref, gb_ref),
                    (c_ref, wc_ref, dc_ref, gc_ref), (d_ref, wd_ref, dd_ref, gd_ref))
        for j, (br, w, dbr_ref, gw_ref) in enumerate(branches):
            gate = _sigmoid(lg_ref[:, j * D:(j + 1) * D].astype(F32))
            y = _dot(br[...], w[...])
            dlg_ref[:, j * D:(j + 1) * D] = (dm * y * gate * (1.0 - gate)).astype(BF16)
            dy = (dm * gate).astype(BF16)
            d_in = _dot_nt(dy, w[...]).astype(dbr_ref.dtype)
            dbr_ref[...] = d_in
            if j == 0:
                prod = d_in.astype(F32) * br[...].astype(F32)
                lane = lax.broadcasted_iota(jnp.int32, (tm, HEAD_PAD), 1)
                delta = jnp.zeros((tm, HEAD_PAD), F32)
                for h in range(HEADS):
                    row_sum = jnp.sum(prod[:, h * HEAD_PAD:(h + 1) * HEAD_PAD], axis=-1, keepdims=True)
                    delta = jnp.where(lane == h, row_sum, delta)
                _rows_to_lanes(delta, dt_ref)
            for k in range(N_CHIPS):
                gw_ref[k] += _dot_tn(br[...], dy[:, k * Dc:(k + 1) * Dc])

        _ride_finish(riding, pl.program_id(0) == T // tm - 1)

    row = lambda w: pl.BlockSpec((tm, w), lambda i: (i, 0))
    wg = [_wgrad_out(l, DEPTH, r, c) for r, c in ((Dc, D), (MLA_W, Dc), (MIX_W, Dc), (MIX_W, Dc), (MIX_W, Dc))]
    n_in = 14
    ride_in, ride_out, ride_shapes, ride_scratch, ride_args, alias = _ride_specs(ride, n_in + len(keep), 12)
    stat = pl.BlockSpec((HEADS, 1, 1, tm), lambda i: (0, i, 0, 0))
    out = pl.pallas_call(
        body, name="merge_bwd", grid=(T // tm,),
        in_specs=[row(D), row(D), row(D), row(4 * D), row(MLA_W), row(MIX_W), row(MIX_W), row(MIX_W),
                  _full((MLA_W, D)), _full((MIX_W, D)), _full((MIX_W, D)), _full((MIX_W, D)), _full((D, D)), _full((1, D))]
        + [pl.BlockSpec(memory_space=pl.ANY)] * len(keep) + ride_in,
        out_specs=[row(4 * D), row(MLA_W), row(MIX_W), row(MIX_W), row(MIX_W), _full((1, D)), stat] + [s for s, _ in wg]
        + ride_out,
        out_shape=[jax.ShapeDtypeStruct((T, 4 * D), BF16), jax.ShapeDtypeStruct((T, MLA_W), BF16)]
        + [jax.ShapeDtypeStruct((T, MIX_W), F32)] * 3 + [jax.ShapeDtypeStruct((1, D), F32),
                                                         jax.ShapeDtypeStruct((HEADS, T // tm, 1, tm), F32)]
        + [s for _, s in wg] + ride_shapes,
        input_output_aliases={**{n_in + i: 7 + i for i in range(len(keep))}, **alias},
        scratch_shapes=ride_scratch,
        compiler_params=_params(1),
    )(dx1, o, merged, P, A, Bm, C, Dv, wa, wb, wc, wd, wout, g_post, *keep, *ride_args)
    return out[:7], dict(zip(names, out[7:12])), list(out[12:])


def _mixers_bwd(P, D, S, dBm, dC, dDv, pool_bd, pool_scale, g_v, sgu_w, sgu_bias, conv_w):
    T = P.shape[0]
    tc = _tile(S, 512, SGU_BLOCK)
    n_si = S // tc
    cur, prev, nxt = _mix_specs(T, D, tc)
    n_blk = tc // SGU_BLOCK

    def body(z_ref, zp_ref, u_ref, v_ref, b_ref, c_ref, x_ref, cp_ref, xp_ref, bn_ref,
             dbm_ref, dbmn_ref, dc_ref, ddv_ref, ddvn_ref,
             pw_ref, ps_ref, gv_ref, sw_ref, sb_ref, cw_ref,
             dp_ref, dpw_ref, dps_ref, dgv_ref, dsw_ref, dsb_ref, dcw_ref, dvn_acc):
        si = pl.program_id(0) % n_si
        first, last = si == 0, si == n_si - 1

        @pl.when(pl.program_id(0) == 0)
        def _():
            for r in (dpw_ref, dps_ref, dgv_ref, dsw_ref, dsb_ref, dcw_ref):
                r[...] = jnp.zeros_like(r)

        z = z_ref[...].astype(F32)
        pooled = _pool_fwd(z, jnp.where(first, 0.0, zp_ref[...].astype(F32)), si, tc).astype(BF16)
        dbm = dbm_ref[...]
        dps_ref[...] += jnp.sum(dbm * _dot(pooled, pw_ref[...]), axis=0, keepdims=True)
        dmix = (jnp.concatenate([dbm, jnp.where(last, 0.0, dbmn_ref[...])], axis=0) * ps_ref[...]).astype(BF16)
        dpw_ref[...] += _dot_tn(pooled, dmix[:tc])
        dpool = _dot_nt(dmix, pw_ref[...])
        e = dpool / _pool_count(si, tc, tc + POOL_HALO)
        f1 = e + _shift_up(e, 1)
        f2 = f1 + _shift_up(f1, 2)
        f4 = f2 + _shift_up(f2, 4)
        f8 = f4 + _shift_up(f4, 8)
        dp_ref[:, 0:MIX_W] = (_by_group(f1, f2, f4, f8)[:tc] - dpool[:tc]).astype(BF16)

        v = v_ref[...].astype(F32)
        vn = (v * _rms_r(v) * gv_ref[...]).astype(BF16)
        keep, wm = _sgu_weights(sw_ref)
        g = _lane_group()
        for blk in range(n_blk):
            rows = slice(blk * SGU_BLOCK, (blk + 1) * SGU_BLOCK)
            vb = vn[rows]
            dc = dc_ref[rows, :]
            dp_ref[rows, MIX_W:2 * MIX_W] = (dc * _sgu_mix(vb, wm, sb_ref[...])).astype(BF16)
            dmx = dc * u_ref[rows, :].astype(F32)
            dsb_ref[...] += dmx
            dvn = jnp.zeros((SGU_BLOCK, MIX_W), F32)
            for k in range(GROUPS):
                dmk = jnp.where(g == k, dmx, 0.0).astype(BF16)
                dsw_ref[k] += jnp.where(keep, _dot_nt(dmk, vb), 0.0)
                dvn = dvn + _dot_tn(wm[k], dmk)
            dvn_acc[rows, :] = dvn
        dv, dg = _rms_bwd(dvn_acc[...], v, gv_ref[...])
        dgv_ref[...] += dg
        dp_ref[:, 2 * MIX_W:3 * MIX_W] = dv.astype(BF16)

        cg, xg, bg = c_ref[...].astype(F32), x_ref[...].astype(F32), b_ref[...].astype(F32)
        zc = cg * xg
        ze = jnp.concatenate([jnp.where(first, 0.0, cp_ref[...].astype(F32) * xp_ref[...].astype(F32)), zc], axis=0)
        z1, z2 = _shift_down(ze, 1)[CONV_HALO:], _shift_down(ze, 2)[CONV_HALO:]
        ddv = ddv_ref[...]
        y = cw_ref[0:1, :] * z2 + cw_ref[1:2, :] * z1 + cw_ref[2:3, :] * zc
        dp_ref[:, 3 * MIX_W:4 * MIX_W] = (ddv * y).astype(BF16)
        dy = ddv * bg
        dcw_ref[0:1, :] += jnp.sum(dy * z2, axis=0, keepdims=True)
        dcw_ref[1:2, :] += jnp.sum(dy * z1, axis=0, keepdims=True)
        dcw_ref[2:3, :] += jnp.sum(dy * zc, axis=0, keepdims=True)
        dye = jnp.concatenate([dy, jnp.where(last, 0.0, ddvn_ref[...] * bn_ref[...].astype(F32))], axis=0)
        dz = (cw_ref[2:3, :] * dye + cw_ref[1:2, :] * _shift_up(dye, 1) + cw_ref[0:1, :] * _shift_up(dye, 2))[:tc]
        dp_ref[:, 4 * MIX_W:5 * MIX_W] = (dz * xg).astype(BF16)
        dp_ref[:, 5 * MIX_W:6 * MIX_W] = (dz * cg).astype(BF16)

    grad = lambda halo: pl.BlockSpec((halo, MIX_W), lambda i: (jnp.minimum((i + 1) * (tc // halo), T // halo - 1), 0))
    out = pl.BlockSpec((tc, MIX_W), lambda i: (i, 0))
    return pl.pallas_call(
        body, name="mixers_bwd", grid=(T // tc,),
        in_specs=[cur(0), prev(0, POOL_HALO), cur(1), cur(2), cur(3), cur(4), cur(5), prev(4, CONV_HALO), prev(5, CONV_HALO),
                  nxt(3, CONV_HALO), out, grad(POOL_HALO), out, out, grad(CONV_HALO),
                  _full((MIX_W, MIX_W)), _full((1, MIX_W)), _full((1, MIX_W)), _full((GROUPS, SGU_BLOCK, SGU_BLOCK)),
                  _full((SGU_BLOCK, MIX_W)), _full((CONV_HALO, MIX_W))],
        out_specs=[pl.BlockSpec((tc, SEC_MIX), lambda i: (i, 0)), _full((MIX_W, MIX_W)), _full((1, MIX_W)), _full((1, MIX_W)),
                   _full((GROUPS, SGU_BLOCK, SGU_BLOCK)), _full((SGU_BLOCK, MIX_W)), _full((CONV_HALO, MIX_W))],
        out_shape=[jax.ShapeDtypeStruct((T, SEC_MIX), BF16), jax.ShapeDtypeStruct((MIX_W, MIX_W), F32),
                   jax.ShapeDtypeStruct((1, MIX_W), F32), jax.ShapeDtypeStruct((1, MIX_W), F32),
                   jax.ShapeDtypeStruct((GROUPS, SGU_BLOCK, SGU_BLOCK), F32), jax.ShapeDtypeStruct((SGU_BLOCK, MIX_W), F32),
                   jax.ShapeDtypeStruct((CONV_HALO, MIX_W), F32)],
        scratch_shapes=[pltpu.VMEM((tc, MIX_W), F32)],
        compiler_params=_params(1),
    )(P, P, P, P, P, P, P, P, P, P, dBm, dBm, dC, dDv, dDv, pool_bd, pool_scale, g_v, sgu_w, sgu_bias, conv_w)


def _attn_tile(S):
    return _tile(S, 256, 128)


def _flash_bwd(Q, K, V, dO, lse_t, delta_t, S, ride=None):
    T = Q.shape[0]
    n_seq = T // S
    tq = _attn_tile(S)
    nq = S // tq

    def body(k_ref, v_ref, q_ref, do_ref, lse_ref, dl_ref, *rest):
        (dq_ref, dk_ref, dv_ref), (s_s, dp_s, p_s, ds_s), riding = _ride_split(ride, rest, 3, 4)
        kb = pl.program_id(1)
        _ride_start(riding, (pl.program_id(0) == 0) & (kb == 0))

        @pl.when(kb == 0)
        def _():
            dq_ref[...] = jnp.zeros_like(dq_ref)

        dk_ref[...] = jnp.zeros_like(dk_ref)
        dv_ref[...] = jnp.zeros_like(dv_ref)

        def block(qi, masked):
            rows = pl.ds(pl.multiple_of(qi * tq, tq), tq)
            for h in range(HEADS):
                hs = slice(h * HEAD_PAD, (h + 1) * HEAD_PAD)
                s_s[h] = _dot_nt(k_ref[:, hs], q_ref[rows, hs])
                dp_s[h] = _dot_nt(v_ref[:, hs], do_ref[rows, hs])
            for h in range(HEADS):
                lse_row, dl_row = lse_ref[h, qi], dl_ref[h, qi]
                for r in range(0, tq, SOFTMAX_ROWS):
                    rs = slice(r, r + SOFTMAX_ROWS)
                    s = s_s[h, rs, :]
                    if masked:
                        key = r + lax.broadcasted_iota(jnp.int32, (SOFTMAX_ROWS, tq), 0)
                        query = lax.broadcasted_iota(jnp.int32, (SOFTMAX_ROWS, tq), 1)
                        s = jnp.where((query >> CHUNK_SHIFT) >= (key >> CHUNK_SHIFT), s, NEG_INF)
                    p = jnp.exp2(s - lse_row)
                    p_s[h, rs, :] = p.astype(BF16)
                    ds_s[h, rs, :] = (p * (dp_s[h, rs, :] - dl_row)).astype(BF16)
            for h in range(HEADS):
                hs = slice(h * HEAD_PAD, (h + 1) * HEAD_PAD)
                dv_ref[:, hs] += _dot(p_s[h], do_ref[rows, hs])
                dk_ref[:, hs] += _dot(ds_s[h], q_ref[rows, hs])
                dq_ref[rows, hs] += _dot_tn(ds_s[h], k_ref[:, hs])

        def full_block(qi, carry):
            block(qi, False)
            return carry

        block(kb, True)
        lax.fori_loop(kb + 1, nq, full_block, 0)
        dk_ref[...] = dk_ref[...] * LN2
        _ride_finish(riding, (pl.program_id(0) == n_seq - 1) & (kb == nq - 1))

    tile = pl.BlockSpec((tq, MLA_W), lambda b, i: (b * nq + i, 0))
    seq = pl.BlockSpec((S, MLA_W), lambda b, i: (b, 0))
    stat = pl.BlockSpec((HEADS, nq, 1, tq), lambda b, i: (0, b, 0, 0))
    ride_in, ride_out, ride_shapes, ride_scratch, ride_args, alias = _ride_specs(ride, 6, 3)
    out = pl.pallas_call(
        body, name="flash_bwd", grid=(n_seq, nq),
        in_specs=[tile, tile, seq, seq, stat, stat] + ride_in,
        out_specs=[seq, tile, tile] + ride_out,
        out_shape=[jax.ShapeDtypeStruct((T, MLA_W), F32)] * 3 + ride_shapes,
        input_output_aliases=alias,
        scratch_shapes=[pltpu.VMEM((HEADS, tq, tq), F32), pltpu.VMEM((HEADS, tq, tq), F32),
                        pltpu.VMEM((HEADS, tq, tq), BF16), pltpu.VMEM((HEADS, tq, tq), BF16)] + ride_scratch,
        compiler_params=_params(2),
    )(K, V, Q, dO, lse_t, delta_t, *ride_args)
    return out[0], out[1], out[2], list(out[3:])


def _mla_bwd_post(P, D, S, dQ, dK, dV, hq, hkv, g_cq, g_ckv, wuq, wukv, rope_c, rope_sa, rope_sb, l, bufs):
    T = P.shape[0]
    tm = _tile(S, 512)
    n_si = S // tm
    base = 4 * D
    names = ("w_uq", "w_ukv")

    def body(cq_ref, ckv_ref, dq_ref, dk_ref, dv_ref, hq_ref, hkv_ref, gq_ref, gkv_ref, wq_ref, wkv_ref, c_ref, sa_ref, sb_ref,
             *rest):
        dp_ref, dgq_ref, dgkv_ref, guq_ref, gukv_ref = rest[-5:]

        @pl.when(pl.program_id(0) == 0)
        def _():
            for r in (dgq_ref, dgkv_ref, guq_ref, gukv_ref):
                r[...] = jnp.zeros_like(r)

        c, sa, sb = c_ref[...], sa_ref[...], sb_ref[...]
        dq = _rope_t(dq_ref[...] * ATTN_SCALE, jnp.tile(c, (1, HEADS)), jnp.tile(sa, (1, HEADS)),
                     jnp.tile(sb, (1, HEADS))).astype(BF16)
        dcq, dg = _rms_bwd(_dot_nt(dq, wq_ref[...]), cq_ref[...].astype(F32), gq_ref[...])
        dgq_ref[...] += dg
        dp_ref[:, 0:Q_LORA] = dcq.astype(BF16)

        dk = dk_ref[...]
        dkb, dvb = dk.astype(BF16), dv_ref[...].astype(BF16)
        dkv = jnp.concatenate([p[:, j * CHIP_HEADS_W:(j + 1) * CHIP_HEADS_W] for j in range(N_CHIPS) for p in (dkb, dvb)], axis=1)
        for k in range(N_CHIPS):
            guq_ref[k] += _dot_tn(hq_ref[...], dq[:, k * CHIP_HEADS_W:(k + 1) * CHIP_HEADS_W])
            gukv_ref[k] += _dot_tn(hkv_ref[...], dkv[:, k * CHIP_KV:(k + 1) * CHIP_KV])
        dckv, dg = _rms_bwd(_dot_nt(dkv, wkv_ref[...]), ckv_ref[...].astype(F32), gkv_ref[...])
        dgkv_ref[...] += dg
        dp_ref[:, Q_LORA:Q_LORA + KV_LORA] = dckv.astype(BF16)

        dkr = dk[:, 0:HEAD_PAD]
        for h in range(1, HEADS):
            dkr = dkr + dk[:, h * HEAD_PAD:(h + 1) * HEAD_PAD]
        lane = lax.broadcasted_iota(jnp.int32, (1, HEAD_PAD), 1)
        rope_lanes = (lane >= QK_NOPE) & (lane < QK_NOPE + QK_ROPE)
        dp_ref[:, Q_LORA + KV_LORA:SEC_MLA] = jnp.where(rope_lanes, _rope_t(dkr, c, sa, sb), 0.0).astype(BF16)

    tab = pl.BlockSpec((tm, HEAD_PAD), lambda i: (i % n_si, 0))
    row = lambda w: pl.BlockSpec((tm, w), lambda i: (i, 0))
    wg = [_wgrad_out(l, DEPTH, Q_LORA, CHIP_HEADS_W), _wgrad_out(l, DEPTH, KV_LORA, CHIP_KV)]
    keep = [] if bufs is None else [bufs[n] for n in names]
    n_in = 14
    out = pl.pallas_call(
        body, name="mla_bwd_post", grid=(T // tm,),
        in_specs=[pl.BlockSpec((tm, Q_LORA), lambda i: (i, base // Q_LORA)),
                  pl.BlockSpec((tm, KV_LORA), lambda i: (i, (base + Q_LORA) // KV_LORA)),
                  row(MLA_W), row(MLA_W), row(MLA_W), row(Q_LORA), row(KV_LORA),
                  _full((1, Q_LORA)), _full((1, KV_LORA)), _full((Q_LORA, MLA_W)), _full((KV_LORA, 2 * MLA_W)), tab, tab, tab]
        + [pl.BlockSpec(memory_space=pl.ANY)] * len(keep),
        out_specs=[row(SEC_MLA), _full((1, Q_LORA)), _full((1, KV_LORA))] + [s for s, _ in wg],
        out_shape=[jax.ShapeDtypeStruct((T, SEC_MLA), BF16), jax.ShapeDtypeStruct((1, Q_LORA), F32),
                   jax.ShapeDtypeStruct((1, KV_LORA), F32)] + [s for _, s in wg],
        input_output_aliases={n_in + i: 3 + i for i in range(len(keep))},
        compiler_params=_params(1),
    )(P, P, dQ, dK, dV, hq, hkv, g_cq, g_ckv, wuq, wukv, rope_c, rope_sa, rope_sb, *keep)
    return out[:3], dict(zip(names, out[3:]))


def _proj_bwd(dx1, x, g, dPg, dPa, dPm, w_gates, w_mla, w_mix, ride=None):
    T, D = x.shape
    tm = _tile(T, 256)

    def body(dx1_ref, x_ref, g_ref, dg_ref_in, da_ref, dm_ref, wg_ref, wa_ref, wm_ref, *rest):
        (dx_ref, dg_ref), _, riding = _ride_split(ride, rest, 2, 0)
        _ride_start(riding, pl.program_id(0) == 0)

        @pl.when(pl.program_id(0) == 0)
        def _():
            dg_ref[...] = jnp.zeros_like(dg_ref)

        dh = _dot_nt(dg_ref_in[...], wg_ref[...]) + _dot_nt(da_ref[...], wa_ref[...]) + _dot_nt(dm_ref[...], wm_ref[...])
        dx, dg = _rms_bwd(dh, x_ref[...], g_ref[...])
        dx_ref[...] = dx1_ref[...] + dx
        dg_ref[...] += dg
        _ride_finish(riding, pl.program_id(0) == T // tm - 1)

    row = lambda w: pl.BlockSpec((tm, w), lambda i: (i, 0))
    ride_in, ride_out, ride_shapes, ride_scratch, ride_args, alias = _ride_specs(ride, 9, 2)
    out = pl.pallas_call(
        body, name="proj_bwd", grid=(T // tm,),
        in_specs=[row(D), row(D), _full((1, D)), row(4 * D), row(SEC_MLA), row(SEC_MIX),
                  _full((D, 4 * D)), _full((D, SEC_MLA)), _full((D, SEC_MIX))] + ride_in,
        out_specs=[row(D), _full((1, D))] + ride_out,
        out_shape=[jax.ShapeDtypeStruct((T, D), F32), jax.ShapeDtypeStruct((1, D), F32)] + ride_shapes,
        input_output_aliases=alias, scratch_shapes=ride_scratch,
        compiler_params=_params(1),
    )(dx1, x, g, dPg, dPa, dPm, w_gates, w_mla, w_mix, *ride_args)
    return out[0], out[1], list(out[2:])


def _adamw(w, g, m, v, name):
    R, C = w.shape
    tr = _tile(R, max(8, (1 << 19) // C))

    def body(w_ref, g_ref, m_ref, v_ref, d_ref, mo_ref, vo_ref):
        gv = g_ref[...]
        mn = ADAM_B1 * m_ref[...] + (1.0 - ADAM_B1) * gv
        vn = ADAM_B2 * v_ref[...] + (1.0 - ADAM_B2) * (gv * gv)
        mo_ref[...] = mn
        vo_ref[...] = vn
        m_hat = mn / (1.0 - ADAM_B1 ** ADAM_STEP)
        v_hat = vn / (1.0 - ADAM_B2 ** ADAM_STEP)
        d_ref[...] = -ADAM_LR * (m_hat / (jnp.sqrt(v_hat) + ADAM_EPS) + ADAM_WD * w_ref[...])

    blk = pl.BlockSpec((tr, C), lambda i: (i, 0))
    return pl.pallas_call(
        body, name=name, grid=(R // tr,), in_specs=[blk] * 4, out_specs=[blk] * 3,
        out_shape=[jax.ShapeDtypeStruct((R, C), F32)] * 3, compiler_params=_params(1),
    )(w, g, m, v)


def _rows_tile(rows, cols):
    return _tile(rows, max(16, (1 << 19) // cols), 16)


def _add_halves(G, span, recv, half, name):
    n, L, R, C = G.shape
    l0, nl = span[0], span[1] - span[0]
    hr = R // 2
    tr = _rows_tile(hr, C)
    nb = hr // tr

    def body(half_ref, g_ref, r_ref, o_ref):
        o_ref[...] = (g_ref[...] + r_ref[...]).astype(BF16)

    grid_spec = pltpu.PrefetchScalarGridSpec(
        num_scalar_prefetch=1, grid=(n, nl, nb),
        in_specs=[pl.BlockSpec((None, None, tr, C), lambda k, l, i, h: (k, l0 + l, h[0] * nb + i, 0)),
                  pl.BlockSpec((None, None, tr, C), lambda k, l, i, h: (k, l, i, 0))],
        out_specs=pl.BlockSpec((None, None, tr, C), lambda k, l, i, h: (k, l, i, 0)))
    return pl.pallas_call(
        body, name="rs_add_halves_" + name, grid_spec=grid_spec,
        out_shape=jax.ShapeDtypeStruct((n, nl, hr, C), BF16), compiler_params=_params(3),
    )(half.reshape(1).astype(jnp.int32), G, recv)


def _sum_slots(H, slots, place, name, l0, n_layers, prev=None):
    _, nl, hr, C = slots.shape
    tr = _rows_tile(hr, C)
    nb = hr // tr

    def body(x_ref, y_ref, c_ref, own_ref, s1_ref, s2_ref, s3_ref, *rest):
        o_ref = rest[-1]
        o_ref[...] = ((own_ref[...].astype(F32) + s1_ref[...].astype(F32)) + s2_ref[...].astype(F32)) + s3_ref[...].astype(F32)

    def src(fx, fy):
        def index(l, j, px, py, pc):
            cx = px[0] + fx - 2 * fx * px[0]
            cy = py[0] + fy - 2 * fy * py[0]
            return (2 * cx + cy, l, j, 0)
        return pl.BlockSpec((None, None, tr, C), index)

    keep = [] if prev is None else [prev]
    grid_spec = pltpu.PrefetchScalarGridSpec(
        num_scalar_prefetch=3, grid=(nl, nb),
        in_specs=[src(0, 0), src(0, 1), src(1, 0), src(1, 1)] + [pl.BlockSpec(memory_space=pl.ANY)] * len(keep),
        out_specs=pl.BlockSpec((None, tr, C), lambda l, j, px, py, pc: (l0 + l, pc[0] * nb + j, 0)))
    return pl.pallas_call(
        body, name="rs_sum_slots_" + name, grid_spec=grid_spec,
        out_shape=jax.ShapeDtypeStruct((n_layers, 2 * hr, C), F32),
        input_output_aliases={7: 0} if keep else {}, compiler_params=_params(2),
    )(*place, H, slots, slots, slots, *keep)


HBM = pl.BlockSpec(memory_space=pltpu.HBM)


def _place():
    x, y, c = lax.axis_index("x"), lax.axis_index("y"), lax.axis_index("c")
    return x, y, c, 2 * x + y


def _chip_device(chip, c):
    return (chip // 2, chip % 2, c)


def _remote(src, dst, send_sem, recv_sem, to):
    return pltpu.make_async_remote_copy(src_ref=src, dst_ref=dst, send_sem=send_sem, recv_sem=recv_sem, device_id=to,
                                        device_id_type=MESH)


def _place_own(w, chip, name):
    L, R, C = w.shape
    tr = _rows_tile(R, C)

    def body(p_ref, w_ref, o_ref):
        o_ref[...] = w_ref[...].astype(BF16)

    grid_spec = pltpu.PrefetchScalarGridSpec(
        num_scalar_prefetch=1, grid=(L, R // tr), in_specs=[pl.BlockSpec((None, tr, C), lambda l, j, p: (l, j, 0))],
        out_specs=pl.BlockSpec((None, None, tr, C), lambda l, j, p: (l, p[0], j, 0)))
    return pl.pallas_call(
        body, name="place_" + name, grid_spec=grid_spec,
        out_shape=jax.ShapeDtypeStruct((L, N_CHIPS, R, C), BF16), compiler_params=_params(2),
    )(chip.reshape(1).astype(jnp.int32), w)


def _gather_weights(bufs, l):
    n = len(bufs)

    def body(*refs):
        o_refs = refs[n:2 * n]
        send_sems, recv_sems = refs[2 * n:]
        x, y, c, me = _place()
        sibling = (x, y, 1 - c)

        def copy(t, k, chip, half, to):
            hr = o_refs[t].shape[2] // 2
            block = o_refs[t].at[l, chip, pl.ds(half * hr, hr), :]
            return _remote(block, block, send_sems.at[6 * t + k], recv_sems.at[6 * t + k], to)

        first = [copy(t, d - 1, me, c, _chip_device(me ^ d, c)) for t in range(n) for d in (1, 2, 3)]
        for cp in first:
            cp.start()
        passed = []
        for t in range(n):
            for d in (1, 2, 3):
                copy(t, d - 1, me ^ d, c, sibling).wait_recv()
                passed.append(copy(t, 2 + d, me ^ d, c, sibling))
                passed[-1].start()
        for t in range(n):
            for d in (1, 2, 3):
                copy(t, 2 + d, me ^ d, 1 - c, sibling).wait_recv()
        for cp in first + passed:
            cp.wait_send()

    return pl.pallas_call(
        body, name="gather_weights", in_specs=[HBM] * n, out_specs=[HBM] * n,
        out_shape=[jax.ShapeDtypeStruct(b.shape, b.dtype) for b in bufs],
        input_output_aliases={t: t for t in range(n)},
        scratch_shapes=[pltpu.SemaphoreType.DMA((6 * n,)), pltpu.SemaphoreType.DMA((6 * n,))],
    )(*bufs)


def _join_halves(bufs):
    n = len(bufs)

    def body(*refs):
        o_refs = refs[n:2 * n]
        send_sems, recv_sems = refs[2 * n:]
        x, y, c, _ = _place()

        def half(t, which):
            hr = o_refs[t].shape[1] // 2
            return o_refs[t].at[:, pl.ds(which * hr, hr), :]

        sends = [_remote(half(t, c), half(t, c), send_sems.at[t], recv_sems.at[t], (x, y, 1 - c)) for t in range(n)]
        for cp in sends:
            cp.start()
        for t in range(n):
            _remote(half(t, 1 - c), half(t, 1 - c), send_sems.at[t], recv_sems.at[t], (x, y, 1 - c)).wait_recv()
        for cp in sends:
            cp.wait_send()

    return pl.pallas_call(
        body, name="rs_join_halves", in_specs=[HBM] * n, out_specs=[HBM] * n,
        out_shape=[jax.ShapeDtypeStruct(b.shape, b.dtype) for b in bufs],
        input_output_aliases={t: t for t in range(n)},
        scratch_shapes=[pltpu.SemaphoreType.DMA((n,)), pltpu.SemaphoreType.DMA((n,))],
    )(*bufs)


def _all_reduce_small(v, name):
    R, C = v.shape

    def body(v_ref, o_ref, slots, send_sems, recv_sems):
        x, y, c, _ = _place()
        me = 4 * x + 2 * y + c
        slots[me] = v_ref[...]
        sends = []
        for d in range(1, 8):
            peer = me ^ d
            sends.append(pltpu.make_async_remote_copy(
                src_ref=v_ref, dst_ref=slots.at[me], send_sem=send_sems.at[d - 1], recv_sem=recv_sems.at[d - 1],
                device_id=(peer // 4, (peer // 2) % 2, peer % 2), device_id_type=MESH))
        for cp in sends:
            cp.start()
        for d in range(1, 8):
            peer = me ^ d
            pltpu.make_async_remote_copy(
                src_ref=v_ref, dst_ref=slots.at[peer], send_sem=send_sems.at[d - 1], recv_sem=recv_sems.at[d - 1],
                device_id=(peer // 4, (peer // 2) % 2, peer % 2), device_id_type=MESH).wait_recv()
        for cp in sends:
            cp.wait_send()
        acc = slots[0]
        for k in range(1, 8):
            acc = acc + slots[k]
        o_ref[...] = acc

    vm = pl.BlockSpec(memory_space=pltpu.VMEM)
    return pl.pallas_call(
        body, name=name, in_specs=[vm], out_specs=vm, out_shape=jax.ShapeDtypeStruct((R, C), F32),
        scratch_shapes=[pltpu.VMEM((8, R, C), F32), pltpu.SemaphoreType.DMA((7,)), pltpu.SemaphoreType.DMA((7,))],
    )(v)


SHARDED = ("w_in", "w_uq", "w_ukv", "conv_w", "w_br_a", "w_br_b", "w_br_c", "w_br_d", "w_out", "w_ffn_gate", "w_ffn_up",
           "w_ffn_down")
ROW_SHARDED = ("w_out", "w_ffn_down")
REPLICATED = ("g_pre_mix", "g_cq", "g_ckv", "pool_w", "pool_scale", "g_sgu_v", "sgu_w", "sgu_b", "g_post_mix", "g_pre_ffn",
              "g_post_ffn")
WEIGHTS = ("w_in", "g_pre_mix", "g_cq", "g_ckv", "w_uq", "w_ukv", "pool_w", "pool_scale", "g_sgu_v", "sgu_w", "sgu_b",
           "conv_w", "w_br_a", "w_br_b", "w_br_c", "w_br_d", "w_out", "g_post_mix", "g_pre_ffn", "w_ffn_gate", "w_ffn_up",
           "w_ffn_down", "g_post_ffn")
GATHERED = tuple(n for n in SHARDED if n != "conv_w")


def _unpack(packed, shapes):
    flat = packed.reshape(-1)
    out, o = [], 0
    for s in shapes:
        n = int(np.prod(s))
        out.append(flat[o:o + n].reshape(s))
        o += n
    return out


def _join_cols(g, l):
    return jnp.concatenate([g[l, k] for k in range(N_CHIPS)], axis=1)


def _pad_heads(w, real):
    lead = w.shape[:-1]
    w = w.reshape(lead + (HEADS, real))
    return jnp.pad(w, [(0, 0)] * len(lead) + [(0, 0), (0, HEAD_PAD - real)]).reshape(lead + (MLA_W,))


def _unpad_heads(w, real):
    lead = w.shape[:-1]
    return w.reshape(lead + (HEADS, HEAD_PAD))[..., :real].reshape(lead + (HEADS * real,))


IN_OFFSETS = {"cq": 0, "ckv": Q_LORA, "kr": Q_LORA + KV_LORA, "mix": Q_LORA + KV_LORA + QK_ROPE}
IN_GATES = Q_LORA + KV_LORA + QK_ROPE + SEC_MIX


def _pad_w_in(w):
    K = w.shape[0]
    z = lambda n: jnp.zeros((K, n), w.dtype)
    return jnp.concatenate([w[:, IN_GATES:], w[:, :IN_OFFSETS["kr"]], z(QK_NOPE), w[:, IN_OFFSETS["kr"]:IN_OFFSETS["mix"]],
                            z(HEAD_PAD - QK_NOPE - QK_ROPE), w[:, IN_OFFSETS["mix"]:IN_GATES]], axis=1)


def _unpad_w_in(d_gates, d_mla, d_mix):
    kr = d_mla[:, Q_LORA + KV_LORA + QK_NOPE:Q_LORA + KV_LORA + QK_NOPE + QK_ROPE]
    return jnp.concatenate([d_mla[:, :Q_LORA + KV_LORA], kr, d_mix, d_gates], axis=1)


def _rope_tables(S):
    half = QK_ROPE // 2
    inv = ROPE_THETA ** (-jnp.arange(0, QK_ROPE, 2, dtype=F32) / QK_ROPE)
    ang = jnp.arange(S, dtype=F32)[:, None] * inv[None, :]
    cos, sin = jnp.cos(ang), jnp.sin(ang)
    one, zero = jnp.ones((S, QK_NOPE), F32), jnp.zeros((S, half), F32)
    tail = HEAD_PAD - QK_NOPE - QK_ROPE
    c = jnp.concatenate([one, cos, cos, jnp.ones((S, tail), F32)], axis=1)
    sa = jnp.concatenate([0 * one, zero, sin, jnp.zeros((S, tail), F32)], axis=1)
    sb = jnp.concatenate([0 * one, -sin, zero, jnp.zeros((S, tail), F32)], axis=1)
    return c, sa, sb


def _layer_weights(gathered, full, l, D):
    w = {}
    w_in = _pad_w_in(_join_cols(gathered["w_in"], l))
    w["w_in"] = w_in
    w["w_in_gates"], w["w_in_mla"], w["w_in_mix"] = w_in[:, :4 * D], w_in[:, 4 * D:4 * D + SEC_MLA], w_in[:, 4 * D + SEC_MLA:]
    w["w_uq"] = _pad_heads(_join_cols(gathered["w_uq"], l), QK_NOPE + QK_ROPE)
    ukv = _join_cols(gathered["w_ukv"], l).reshape(KV_LORA, HEADS, QK_NOPE + V_HEAD)
    pad = ((0, 0), (0, 0), (0, HEAD_PAD - QK_NOPE))
    k_pad = jnp.pad(ukv[:, :, :QK_NOPE], pad).reshape(KV_LORA, N_CHIPS, CHIP_HEADS_W)
    v_pad = jnp.pad(ukv[:, :, QK_NOPE:], pad).reshape(KV_LORA, N_CHIPS, CHIP_HEADS_W)
    w["w_ukv"] = jnp.concatenate([k_pad, v_pad], axis=2).reshape(KV_LORA, 2 * MLA_W)
    w["w_br_a"] = jnp.pad(_join_cols(gathered["w_br_a"], l).reshape(HEADS, V_HEAD, D),
                          ((0, 0), (0, HEAD_PAD - V_HEAD), (0, 0))).reshape(MLA_W, D)
    for n in ("w_br_b", "w_br_c", "w_br_d"):
        w[n] = _join_cols(gathered[n], l)
    w["w_out"] = gathered["w_out"][l].reshape(D, D)
    for n in ("g_pre_mix", "g_cq", "g_ckv", "pool_scale", "g_sgu_v", "g_post_mix", "g_pre_ffn", "g_post_ffn"):
        w[n] = full[n][l].reshape(1, -1)
    pw = full["pool_w"][l]
    w["pool_bd"] = jax.scipy.linalg.block_diag(*[pw[g] for g in range(GROUPS)]).astype(BF16)
    w["sgu_w"] = full["sgu_w"][l]
    w["sgu_bias"] = jnp.repeat(full["sgu_b"][l].T, GROUP_DIM, axis=1)
    w["conv_w"] = jnp.pad(full["conv_w"][l].reshape(3, MIX_W), ((0, CONV_HALO - 3), (0, 0)))
    return w


def _layer_fwd(x, w, gathered, l, S, rope, gather_next):
    D = x.shape[1]
    bufs = [gathered[n] for n in GATHERED]
    P, h = _norm_matmul(x, w["g_pre_mix"], w["w_in"], "proj_fwd")
    Q, K, V, hq, hkv = _mla_prep(P, D, w["g_cq"], w["g_ckv"], w["w_uq"], w["w_ukv"], *rope, S)
    A, lse, bufs = _flash_fwd(Q, K, V, S, _gather_ride(bufs, l + 1, "chips") if gather_next else None)
    Bm, C, Dv, bufs = _mixers_fwd(P, D, S, w["pool_bd"], w["pool_scale"], w["g_sgu_v"], w["sgu_w"], w["sgu_bias"], w["conv_w"],
                                  _gather_ride(bufs, l + 1, "cores") if gather_next else None)
    if gather_next:
        gathered = dict(zip(GATHERED, bufs))
    x1, merged, o = _merge_fwd(x, P, A, Bm, C, Dv, w["w_br_a"], w["w_br_b"], w["w_br_c"], w["w_br_d"], w["w_out"], w["g_post_mix"])
    x2, h2, gt, up, f = _ffn_fwd(x1, w["g_pre_ffn"], gathered["w_ffn_gate"], gathered["w_ffn_up"], gathered["w_ffn_down"],
                                 w["g_post_ffn"], l)
    saved = dict(x=x, P=P, h=h, Q=Q, K=K, V=V, hq=hq, hkv=hkv, A=A, lse=lse, Bm=Bm, C=C, Dv=Dv, x1=x1, merged=merged, o=o,
                 h2=h2, gt=gt, up=up, f=f)
    return x2, saved, gathered


RIDE_SETS = (("w_ffn_gate", "w_out", "w_br_a"), ("w_in", "w_br_b", "w_br_c", "w_br_d", "w_uq", "w_ukv"),
             ("w_ffn_up", "w_ffn_down"))


def _layer_bwd(dx2, w, gathered, l, s, S, rope, bufs, early=None, send_ffn=False):
    D = dx2.shape[1]
    g = {}
    others = [n for n in GATHERED if n not in FFN_WEIGHTS]

    def scatter(k):
        return _scatter_ride([sums[n] for n in RIDE_SETS[k]]) if early else None

    sums, slots = {}, {}
    if early:
        done, spans, half, got_ffn = early
        ride = _exchange_ride([done[n] for n in others], [spans[n] for n in others])
    df, g["g_post_ffn"] = _norm_bwd(dx2, s["f"], w["g_post_ffn"], "ffn_bwd_pre")
    dh_parts, filled, got = _ffn_bwd_main(df, s["gt"], s["up"], s["h2"], gathered["w_ffn_gate"], gathered["w_ffn_up"],
                                          gathered["w_ffn_down"], l, bufs if "w_ffn_gate" in bufs else None,
                                          ride if early else None)
    bufs.update(filled)
    dx1, g["g_pre_ffn"] = _norm_bwd(None, s["x1"], w["g_pre_ffn"], "ffn_bwd_post", add=dx2, parts=dh_parts)
    if early:
        sums = {n: _add_halves(done[n], spans[n], r, half, n) for n, r in zip(others, got)}
        sums.update({n: _add_halves(filled[n], spans[n], got_ffn[n], half, n) for n in FFN_WEIGHTS})
        sums, dx1 = lax.optimization_barrier((sums, dx1))

    (dPg, dA, dBm, dC, dDv, g["g_post_mix"], delta_t), filled, got = _merge_bwd(
        dx1, s["o"], s["merged"], s["P"], s["A"], s["Bm"], s["C"], s["Dv"], w["w_br_a"], w["w_br_b"], w["w_br_c"], w["w_br_d"],
        w["w_out"], w["g_post_mix"], l, bufs if "w_out" in bufs else None, scatter(0))
    bufs.update(filled)
    slots.update(zip(RIDE_SETS[0], got))

    dPm, d_pool_bd, g_ps, g_gv, g["sgu_w"], d_bias, d_cw = _mixers_bwd(
        s["P"], D, S, dBm, dC, dDv, w["pool_bd"], w["pool_scale"], w["g_sgu_v"], w["sgu_w"], w["sgu_bias"], w["conv_w"])
    g["pool_w"] = jnp.stack([d_pool_bd[k * GROUP_DIM:(k + 1) * GROUP_DIM, k * GROUP_DIM:(k + 1) * GROUP_DIM] for k in range(GROUPS)])
    g["pool_scale"], g["g_sgu_v"] = g_ps, g_gv
    g["sgu_b"] = d_bias.reshape(SGU_BLOCK, GROUPS, GROUP_DIM).sum(-1).T
    g["conv_w"] = d_cw[:3].reshape(3, 1, MIX_W)

    dQ, dK, dV, got = _flash_bwd(s["Q"], s["K"], s["V"], dA, s["lse"], delta_t, S, scatter(1))
    slots.update(zip(RIDE_SETS[1], got))
    (dPa, g["g_cq"], g["g_ckv"]), filled = _mla_bwd_post(
        s["P"], D, S, dQ, dK, dV, s["hq"], s["hkv"], w["g_cq"], w["g_ckv"], w["w_uq"], w["w_ukv"], *rope, l,
        bufs if "w_uq" in bufs else None)
    bufs.update(filled)

    ride = scatter(2)
    if send_ffn:
        ride = _exchange_ride([bufs[n] for n in FFN_WEIGHTS], [(l, DEPTH)] * len(FFN_WEIGHTS))
    dx, g["g_pre_mix"], got = _proj_bwd(dx1, s["x"], w["g_pre_mix"], dPg, dPa, dPm, w["w_in_gates"], w["w_in_mla"],
                                        w["w_in_mix"], ride)
    if send_ffn:
        slots = dict(zip(FFN_WEIGHTS, got))
    else:
        slots.update(zip(RIDE_SETS[2], got))
    d_w_in = _unpad_w_in(_matmul_tn(s["h"], dPg, "wgrad_in_gates"), _matmul_tn(s["h"], dPa, "wgrad_in_mla"),
                         _matmul_tn(s["h"], dPm, "wgrad_in_mix"))
    g["w_in"] = d_w_in.reshape(D, N_CHIPS, -1).transpose(1, 0, 2)
    for n in ("g_pre_mix", "g_cq", "g_ckv", "pool_scale", "g_sgu_v", "g_post_mix", "g_pre_ffn", "g_post_ffn"):
        g[n] = g[n].reshape(-1)
    return dx, g, (sums, slots)


SMALL = REPLICATED + ("conv_w",)


def _local_step(x, target, gathered, full, core):
    n_seq, S, D = x.shape
    rope = _rope_tables(S)
    xs = x.reshape(n_seq * S, D)
    weights, saved = [], []
    for l in range(DEPTH):
        w = _layer_weights(gathered, full, l, D)
        gather_next = l + 1 < DEPTH
        if gather_next:
            w, gathered = lax.optimization_barrier((w, gathered))
        xs, s, gathered = _layer_fwd(xs, w, gathered, l, S, rope, gather_next)
        weights.append(w)
        saved.append(s)
    loss_parts, dx = _loss_grad(xs, target.reshape(n_seq * S, D))
    grads, bufs = [None] * DEPTH, {}
    for l in reversed(range(1, DEPTH)):
        dx, grads[l], (_, got_ffn) = _layer_bwd(dx, weights[l], gathered, l, saved[l], S, rope, bufs, send_ffn=l == 1)
    done = dict(bufs, w_in=jnp.stack([grads[l]["w_in"] for l in range(1, DEPTH)], axis=1))
    spans = dict({n: (1, DEPTH) for n in GATHERED}, w_in=(0, DEPTH - 1))
    dx, grads[0], early = _layer_bwd(dx, weights[0], gathered, 0, saved[0], S, rope, bufs, (done, spans, core, got_ffn))
    last = dict(bufs, w_in=grads[0]["w_in"][:, None])
    small = {n: jnp.stack([grads[l][n] for l in range(DEPTH)]) for n in SMALL}
    return loss_parts, dx.reshape(n_seq, S, D), early, last, small


def _unpad_reduced(n, r):
    L = r.shape[0]
    if n == "w_uq":
        return r.reshape(L, Q_LORA, 2, HEAD_PAD)[..., :QK_NOPE + QK_ROPE].reshape(L, Q_LORA, -1)
    if n == "w_ukv":
        r = r.reshape(L, KV_LORA, 2, 2, HEAD_PAD)[..., :QK_NOPE]
        return jnp.concatenate([r[:, :, 0], r[:, :, 1]], axis=-1).reshape(L, KV_LORA, -1)
    if n == "w_br_a":
        return r.reshape(L, HEADS, HEAD_PAD, -1)[:, :, :V_HEAD].reshape(L, HEADS * V_HEAD, -1)
    return r


def _small_rows(n):
    return -(-n // (8 * 128)) * 8


def _to_small(parts):
    flat = jnp.concatenate([p.reshape(-1) for p in parts])
    rows = _small_rows(flat.shape[0])
    return jnp.pad(flat, (0, rows * 128 - flat.shape[0])).reshape(rows, 128)


def kernel(x, w_in, g_pre_mix, g_cq, g_ckv, w_uq, w_ukv, pool_w, pool_scale, g_sgu_v, sgu_w, sgu_b, conv_w, w_br_a, w_br_b, w_br_c, w_br_d, w_out, g_post_mix, g_pre_ffn, w_ffn_gate, w_ffn_up, w_ffn_down, g_post_ffn, loss_target, m_w_in, m_g_pre_mix, m_g_cq, m_g_ckv, m_w_uq, m_w_ukv, m_pool_w, m_pool_scale, m_g_sgu_v, m_sgu_w, m_sgu_b, m_conv_w, m_w_br_a, m_w_br_b, m_w_br_c, m_w_br_d, m_w_out, m_g_post_mix, m_g_pre_ffn, m_w_ffn_gate, m_w_ffn_up, m_w_ffn_down, m_g_post_ffn, v_w_in, v_g_pre_mix, v_g_cq, v_g_ckv, v_w_uq, v_w_ukv, v_pool_w, v_pool_scale, v_g_sgu_v, v_sgu_w, v_sgu_b, v_conv_w, v_w_br_a, v_w_br_b, v_w_br_c, v_w_br_d, v_w_out, v_g_post_mix, v_g_pre_ffn, v_w_ffn_gate, v_w_ffn_up, v_w_ffn_down, v_g_post_ffn):
    local = dict(locals())
    W = {n: local[n] for n in WEIGHTS}
    M = {n: local["m_" + n] for n in WEIGHTS}
    V = {n: local["v_" + n] for n in WEIGHTS}
    chip = 2 * lax.axis_index("x") + lax.axis_index("y")
    core = lax.axis_index("c")

    gathered = dict(zip(GATHERED, _gather_weights([_place_own(W[n], chip, n) for n in GATHERED], 0)))
    conv_shape = conv_w.shape
    conv_cols = conv_shape[-1]
    conv_full_shape = conv_shape[:-1] + (N_CHIPS * conv_cols,)
    placed = lax.dynamic_update_slice(jnp.zeros(conv_full_shape, F32), conv_w, (0, 0, 0, chip * conv_cols))
    n_conv = int(np.prod(conv_full_shape))
    conv_sum = _all_reduce_small(_to_small([placed]), "gather_conv_w")
    full = {n: W[n] for n in REPLICATED}
    full["conv_w"] = 0.5 * conv_sum.reshape(-1)[:n_conv].reshape(conv_full_shape)

    loss_parts, grad_x, (sums_up, slots_up), last, small = _local_step(x, loss_target, gathered, full, core)
    loss = lax.psum(jnp.sum(loss_parts), ("x", "y", "c"))

    small_sum = _all_reduce_small(_to_small([small[n] for n in SMALL]), "reduce_small_grads")
    small_grads = dict(zip(SMALL, _unpack(small_sum, [small[n].shape for n in SMALL])))
    small_grads["conv_w"] = lax.dynamic_slice(small_grads["conv_w"], (0, 0, 0, chip * conv_cols), conv_shape)

    first = [(0, 1)] * len(GATHERED)
    Gs = [last[n] for n in GATHERED]
    got = _run_ride(_exchange_ride(Gs, first), "rs_exchange_halves")
    sums_0 = [_add_halves(g, (0, 1), r, core, n) for n, g, r in zip(GATHERED, Gs, got)]
    slots_0 = _run_ride(_scatter_ride(sums_0), "rs_scatter_partials")
    place = [lax.axis_index(a).reshape(1).astype(jnp.int32) for a in ("x", "y", "c")]
    halves = []
    for n, h, s in zip(GATHERED, sums_0, slots_0):
        upper = _sum_slots(sums_up[n], slots_up[n], place, n, 1, DEPTH)
        halves.append(_sum_slots(h, s, place, n, 0, DEPTH, prev=upper))
    shard_grads = {n: _unpad_reduced(n, r).reshape(W[n].shape) for n, r in zip(GATHERED, _join_halves(halves))}

    out_g, out_d, out_m, out_v = {}, {}, {}, {}
    for n in GATHERED:
        shp = W[n].shape
        flat = lambda a: a.reshape(-1, shp[-1])
        d, m2, v2 = _adamw(flat(W[n]), flat(shard_grads[n]), flat(M[n]), flat(V[n]), "adamw_" + n)
        out_g[n], out_d[n], out_m[n], out_v[n] = shard_grads[n], d.reshape(shp), m2.reshape(shp), v2.reshape(shp)
    rest_shapes = [W[n].shape for n in SMALL]
    d, m2, v2 = _adamw(_to_small([W[n] for n in SMALL]), _to_small([small_grads[n] for n in SMALL]),
                       _to_small([M[n] for n in SMALL]), _to_small([V[n] for n in SMALL]), "adamw_small")
    for n, dd, mm, vv in zip(SMALL, _unpack(d, rest_shapes), _unpack(m2, rest_shapes), _unpack(v2, rest_shapes)):
        out_g[n], out_d[n], out_m[n], out_v[n] = small_grads[n], dd, mm, vv

    return (loss, grad_x, *[out_g[n] for n in WEIGHTS], *[out_d[n] for n in WEIGHTS], *[out_m[n] for n in WEIGHTS],
            *[out_v[n] for n in WEIGHTS])
```

```python
import functools

import numpy as np
import jax
import jax.numpy as jnp
from jax import lax
from jax.experimental import pallas as pl
from jax.experimental.pallas import tpu as pltpu

F32 = jnp.float32
BF16 = jnp.bfloat16

EPS = 1e-6
NEG_INF = -1e30
DEPTH = 4
HEADS = 8
QK_NOPE = 64
QK_ROPE = 32
V_HEAD = 64
HEAD_PAD = 128
Q_LORA = 256
KV_LORA = 128
ROPE_THETA = 10000.0
POOL_WINDOWS = (2, 4, 8, 16)
GROUPS = 4
GROUP_DIM = 64
MIX_W = GROUPS * GROUP_DIM
POOL_HALO = 16
CONV_HALO = 16
SGU_BLOCK = 128
CHUNK = 64
CHUNK_SHIFT = 6
ACT_ROWS = 16
NORM_ROWS = 16
ROW_PARTS = 2
SOFTMAX_ROWS = 32
GROUP_SHIFT = 6
N_BRANCH = 4
MLA_W = HEADS * HEAD_PAD
N_CHIPS = 4
CHIP_HEADS_W = MLA_W // N_CHIPS
CHIP_KV = 2 * CHIP_HEADS_W
ATTN_SCALE = (QK_NOPE + QK_ROPE) ** -0.5
LOG2E = 1.4426950408889634
LN2 = 0.6931471805599453
SEC_MLA = Q_LORA + KV_LORA + HEAD_PAD
SEC_MIX = 6 * MIX_W

ADAM_LR = 0.001
ADAM_B1 = 0.9
ADAM_B2 = 0.999
ADAM_EPS = 1e-08
ADAM_WD = 0.01
ADAM_STEP = 10

VMEM_LIMIT = 56 * 1024 * 1024
MESH = pl.DeviceIdType.MESH


def _tile(n, pref, mult=8):
    t = min(n, pref)
    while t > 0:
        if n % t == 0 and t % mult == 0:
            return t
        t -= 1
    return n


def _params(n_axes):
    return pltpu.CompilerParams(dimension_semantics=("arbitrary",) * n_axes, vmem_limit_bytes=VMEM_LIMIT)


def _dot(a, b):
    return jnp.dot(a, b, preferred_element_type=F32)


def _dot_nt(a, b):
    return lax.dot_general(a, b, (((1,), (1,)), ((), ())), preferred_element_type=F32)


def _dot_tn(a, b):
    return lax.dot_general(a, b, (((0,), (0,)), ((), ())), preferred_element_type=F32)


def _rms_r(x):
    return lax.rsqrt(jnp.mean(x * x, axis=-1, keepdims=True) + EPS)


def _rms_bwd(dy, x, g):
    r = _rms_r(x)
    u = dy * g
    dx = r * u - x * (r * r * r * jnp.mean(u * x, axis=-1, keepdims=True))
    dg = jnp.sum(dy * x * r, axis=0, keepdims=True)
    return dx, dg


def _sigmoid(x):
    return 1.0 / (1.0 + jnp.exp(-x))


def _shift_down(a, k):
    return pltpu.roll(a, k, 0)


def _shift_up(a, k):
    return pltpu.roll(a, a.shape[0] - k, 0)


def _rope(x, c, sa, sb):
    w = x.shape[-1]
    return x * c + pltpu.roll(x, QK_ROPE // 2, 1) * sa + pltpu.roll(x, w - QK_ROPE // 2, 1) * sb


def _rope_t(d, c, sa, sb):
    w = d.shape[-1]
    return d * c + pltpu.roll(d * sa, w - QK_ROPE // 2, 1) + pltpu.roll(d * sb, QK_ROPE // 2, 1)


def _full(shape):
    return pl.BlockSpec(shape, lambda *_: (0,) * len(shape))


def _gather_ride(bufs, l, stage):
    def copies(_, o_refs, send_sems, recv_sems):
        x, y, c, me = _place()
        sends, arrivals = [], []
        for t, o in enumerate(o_refs):
            hr = o.shape[2] // 2
            for d in (1, 2, 3):
                sems = (send_sems.at[3 * t + d - 1], recv_sems.at[3 * t + d - 1])
                mine = o.at[l, me, pl.ds(c * hr, hr), :]
                theirs = o.at[l, me ^ d, pl.ds(c * hr, hr), :]
                other_half = o.at[l, me ^ d, pl.ds((1 - c) * hr, hr), :]
                if stage == "chips":
                    sends.append(_remote(mine, mine, *sems, _chip_device(me ^ d, c)))
                    arrivals.append(_remote(theirs, theirs, *sems, _chip_device(me ^ d, c)))
                else:
                    sends.append(_remote(theirs, theirs, *sems, (x, y, 1 - c)))
                    arrivals.append(_remote(other_half, other_half, *sems, (x, y, 1 - c)))
        return sends, arrivals

    shapes = [jax.ShapeDtypeStruct(b.shape, b.dtype) for b in bufs]
    return dict(ins=list(bufs), outs=shapes, alias=True, copies=copies, n_sems=3 * len(bufs))


def _exchange_ride(Gs, spans):
    def copies(g_refs, o_refs, send_sems, recv_sems):
        x, y, c, _ = _place()
        cps = []
        for t, (g, o) in enumerate(zip(g_refs, o_refs)):
            hr = g.shape[2] // 2
            l0, l1 = spans[t]
            cps.append(_remote(g.at[:, pl.ds(l0, l1 - l0), pl.ds((1 - c) * hr, hr), :], o, send_sems.at[t], recv_sems.at[t],
                               (x, y, 1 - c)))
        return cps, cps

    shapes = [jax.ShapeDtypeStruct((g.shape[0], l1 - l0, g.shape[2] // 2, g.shape[3]), g.dtype) for g, (l0, l1) in zip(Gs, spans)]
    return dict(ins=list(Gs), outs=shapes, alias=False, copies=copies, n_sems=len(Gs))


def _scatter_ride(Hs):
    def copies(h_refs, o_refs, send_sems, recv_sems):
        x, y, c, me = _place()
        sends, arrivals = [], []
        for t, (h, o) in enumerate(zip(h_refs, o_refs)):
            for d in (1, 2, 3):
                sems = (send_sems.at[3 * t + d - 1], recv_sems.at[3 * t + d - 1])
                sends.append(_remote(h.at[me ^ d], o.at[me], *sems, _chip_device(me ^ d, c)))
                arrivals.append(_remote(h.at[me ^ d], o.at[me ^ d], *sems, _chip_device(me ^ d, c)))
        return sends, arrivals

    shapes = [jax.ShapeDtypeStruct(h.shape, h.dtype) for h in Hs]
    return dict(ins=list(Hs), outs=shapes, alias=False, copies=copies, n_sems=3 * len(Hs))


def _ride_specs(ride, n_in, n_out):
    if not ride:
        return [], [], [], [], [], {}
    anywhere = pl.BlockSpec(memory_space=pl.ANY)
    sems = pltpu.SemaphoreType.DMA((ride["n_sems"],))
    alias = {n_in + i: n_out + i for i in range(len(ride["ins"]))} if ride["alias"] else {}
    return [anywhere] * len(ride["ins"]), [anywhere] * len(ride["outs"]), list(ride["outs"]), [sems, sems], ride["ins"], alias


def _ride_split(ride, rest, n_out, n_scratch):
    a = len(ride["ins"]) if ride else 0
    b = a + n_out
    c = b + (len(ride["outs"]) if ride else 0)
    d = c + n_scratch
    riding = (ride, rest[:a], rest[b:c], rest[d:]) if ride else None
    return rest[a:b], rest[c:d], riding


def _ride_start(riding, first):
    if riding:
        ride, in_refs, out_refs, (send_sems, recv_sems) = riding

        @pl.when(first)
        def _():
            for cp in ride["copies"](in_refs, out_refs, send_sems, recv_sems)[0]:
                cp.start()


def _ride_finish(riding, last):
    if riding:
        ride, in_refs, out_refs, (send_sems, recv_sems) = riding

        @pl.when(last)
        def _():
            sends, arrivals = ride["copies"](in_refs, out_refs, send_sems, recv_sems)
            for cp in arrivals:
                cp.wait_recv()
            for cp in sends:
                cp.wait_send()


def _run_ride(ride, name):
    def body(*refs):
        _, _, (_, in_refs, out_refs, (send_sems, recv_sems)) = _ride_split(ride, refs, 0, 0)
        sends, arrivals = ride["copies"](in_refs, out_refs, send_sems, recv_sems)
        for cp in sends:
            cp.start()
        for cp in arrivals:
            cp.wait_recv()
        for cp in sends:
            cp.wait_send()

    in_specs, out_specs, out_shapes, scratch, operands, alias = _ride_specs(ride, 0, 0)
    return list(pl.pallas_call(body, name=name, in_specs=in_specs, out_specs=out_specs, out_shape=out_shapes,
                               input_output_aliases=alias, scratch_shapes=scratch)(*operands))


def _norm_matmul(x, g, w, name):
    T, K = x.shape
    N = w.shape[1]
    tm, tn = _tile(T, 2048), _tile(N, 1536, 128)

    def body(x_ref, g_ref, w_ref, o_ref, h_ref):
        @pl.when(pl.program_id(1) == 0)
        def _():
            xv = x_ref[...]
            h_ref[...] = (xv * _rms_r(xv) * g_ref[...]).astype(BF16)

        o_ref[...] = _dot(h_ref[...], w_ref[...]).astype(BF16)

    return pl.pallas_call(
        body, name=name, grid=(T // tm, N // tn),
        in_specs=[pl.BlockSpec((tm, K), lambda i, j: (i, 0)), _full((1, K)), pl.BlockSpec((K, tn), lambda i, j: (0, j))],
        out_specs=[pl.BlockSpec((tm, tn), lambda i, j: (i, j)), pl.BlockSpec((tm, K), lambda i, j: (i, 0))],
        out_shape=[jax.ShapeDtypeStruct((T, N), BF16), jax.ShapeDtypeStruct((T, K), BF16)],
        compiler_params=_params(2),
    )(x, g, w)


def _mla_prep(P, D, g_cq, g_ckv, wuq, wukv, rope_c, rope_sa, rope_sb, S):
    T = P.shape[0]
    tm = _tile(S, 512)
    n_si = S // tm
    base = 4 * D

    def body(cq_ref, ckv_ref, kr_ref, gq_ref, gkv_ref, wq_ref, wkv_ref, c_ref, sa_ref, sb_ref,
             q_ref, k_ref, v_ref, hq_ref, hkv_ref):
        c, sa, sb = c_ref[...], sa_ref[...], sb_ref[...]
        cq = cq_ref[...].astype(F32)
        hq = (cq * _rms_r(cq) * gq_ref[...]).astype(BF16)
        hq_ref[...] = hq
        q = _dot(hq, wq_ref[...])
        q = _rope(q, jnp.tile(c, (1, HEADS)), jnp.tile(sa, (1, HEADS)), jnp.tile(sb, (1, HEADS)))
        q_ref[...] = (q * (ATTN_SCALE * LOG2E)).astype(BF16)
        ckv = ckv_ref[...].astype(F32)
        hkv = (ckv * _rms_r(ckv) * gkv_ref[...]).astype(BF16)
        hkv_ref[...] = hkv
        kv = _dot(hkv, wkv_ref[...])
        kr = _rope(kr_ref[...].astype(F32), c, sa, sb)
        k_nope = jnp.concatenate([kv[:, j * CHIP_KV:j * CHIP_KV + CHIP_HEADS_W] for j in range(N_CHIPS)], axis=1)
        k_ref[...] = (k_nope + jnp.tile(kr, (1, HEADS))).astype(BF16)
        v = jnp.concatenate([kv[:, j * CHIP_KV + CHIP_HEADS_W:(j + 1) * CHIP_KV] for j in range(N_CHIPS)], axis=1)
        ones_lane = (lax.broadcasted_iota(jnp.int32, (1, MLA_W), 1) & (HEAD_PAD - 1)) == V_HEAD
        v_ref[...] = jnp.where(ones_lane, 1.0, v).astype(BF16)

    tab = pl.BlockSpec((tm, HEAD_PAD), lambda i: (i % n_si, 0))
    row = lambda w: pl.BlockSpec((tm, w), lambda i: (i, 0))
    return pl.pallas_call(
        body, name="mla_prep", grid=(T // tm,),
        in_specs=[pl.BlockSpec((tm, Q_LORA), lambda i: (i, base // Q_LORA)),
                  pl.BlockSpec((tm, KV_LORA), lambda i: (i, (base + Q_LORA) // KV_LORA)),
                  pl.BlockSpec((tm, HEAD_PAD), lambda i: (i, (base + Q_LORA + KV_LORA) // HEAD_PAD)),
                  _full((1, Q_LORA)), _full((1, KV_LORA)), _full((Q_LORA, MLA_W)), _full((KV_LORA, 2 * MLA_W)),
                  tab, tab, tab],
        out_specs=[row(MLA_W), row(MLA_W), row(MLA_W), row(Q_LORA), row(KV_LORA)],
        out_shape=[jax.ShapeDtypeStruct((T, MLA_W), BF16)] * 3
        + [jax.ShapeDtypeStruct((T, Q_LORA), BF16), jax.ShapeDtypeStruct((T, KV_LORA), BF16)],
        compiler_params=_params(1),
    )(P, P, P, g_cq, g_ckv, wuq, wukv, rope_c, rope_sa, rope_sb)


def _chunk_mask(tq, tk):
    row = lax.broadcasted_iota(jnp.int32, (tq, tk), 0)
    col = lax.broadcasted_iota(jnp.int32, (tq, tk), 1)
    return (row >> CHUNK_SHIFT) >= (col >> CHUNK_SHIFT)


def _rows_to_lanes(stats, out_ref):
    t = stats.T
    for h in range(HEADS):
        out_ref[h, 0] = t[h:h + 1, :]


def _flash_fwd(Q, K, V, S, ride=None):
    T = Q.shape[0]
    n_seq = T // S
    tq = _tile(S, 256, 128)
    nq = S // tq

    def body(q_ref, k_ref, v_ref, *rest):
        (o_ref, lse_ref), (m_s, acc_s, s_s, p_s, a_s), riding = _ride_split(ride, rest, 2, 5)
        _ride_start(riding, (pl.program_id(0) == 0) & (pl.program_id(1) == 0))
        qi = pl.program_id(1)
        m_s[...] = jnp.full(m_s.shape, NEG_INF, F32)
        acc_s[...] = jnp.zeros_like(acc_s)

        def block(kb, masked):
            rows = pl.ds(pl.multiple_of(kb * tq, tq), tq)
            for h in range(HEADS):
                hs = slice(h * HEAD_PAD, (h + 1) * HEAD_PAD)
                s_s[h] = _dot_nt(q_ref[:, hs], k_ref[rows, hs])
            def softmax_head(h):
                for r in range(0, tq, SOFTMAX_ROWS):
                    rs = slice(r, r + SOFTMAX_ROWS)
                    s = s_s[h, rs, :]
                    if masked:
                        row = r + lax.broadcasted_iota(jnp.int32, (SOFTMAX_ROWS, tq), 0)
                        col = lax.broadcasted_iota(jnp.int32, (SOFTMAX_ROWS, tq), 1)
                        s = jnp.where((row >> CHUNK_SHIFT) >= (col >> CHUNK_SHIFT), s, NEG_INF)
                    m_old = m_s[h, rs]
                    m_new = jnp.maximum(m_old, jnp.max(s, axis=-1, keepdims=True))
                    m_s[h, rs] = m_new
                    a_s[h, rs] = jnp.exp2(m_old - m_new)
                    for half in range(tq // HEAD_PAD):
                        cs = slice(half * HEAD_PAD, (half + 1) * HEAD_PAD)
                        p_s[h, rs, cs] = jnp.exp2(s[:, cs] - m_new).astype(BF16)

            for h in range(HEADS):
                softmax_head(h)
            for h in range(HEADS):
                hs = slice(h * HEAD_PAD, (h + 1) * HEAD_PAD)
                acc_s[:, hs] = a_s[h] * acc_s[:, hs] + _dot(p_s[h], v_ref[rows, hs])

        def full_block(kb, carry):
            block(kb, False)
            return carry

        lax.fori_loop(0, qi, full_block, 0)
        block(qi, True)
        lane = lax.broadcasted_iota(jnp.int32, (tq, HEAD_PAD), 1)
        lse_all = jnp.zeros((tq, HEAD_PAD), F32)
        for h in range(HEADS):
            hs = slice(h * HEAD_PAD, (h + 1) * HEAD_PAD)
            acc = acc_s[:, hs]
            l = jnp.sum(jnp.where(lane == V_HEAD, acc, 0.0), axis=-1, keepdims=True)
            o_ref[:, hs] = (acc / l).astype(BF16)
            lse_all = jnp.where(lane == h, m_s[h] + jnp.log2(l), lse_all)
        _rows_to_lanes(lse_all, lse_ref)
        _ride_finish(riding, (pl.program_id(0) == n_seq - 1) & (pl.program_id(1) == nq - 1))

    ride_in, ride_out, ride_shapes, ride_scratch, ride_args, alias = _ride_specs(ride, 3, 2)
    out = pl.pallas_call(
        body, name="flash_fwd", grid=(n_seq, nq),
        in_specs=[pl.BlockSpec((tq, MLA_W), lambda b, i: (b * nq + i, 0)),
                  pl.BlockSpec((S, MLA_W), lambda b, i: (b, 0)), pl.BlockSpec((S, MLA_W), lambda b, i: (b, 0))] + ride_in,
        out_specs=[pl.BlockSpec((tq, MLA_W), lambda b, i: (b * nq + i, 0)),
                   pl.BlockSpec((HEADS, 1, 1, tq), lambda b, i: (0, b * nq + i, 0, 0))] + ride_out,
        out_shape=[jax.ShapeDtypeStruct((T, MLA_W), BF16), jax.ShapeDtypeStruct((HEADS, T // tq, 1, tq), F32)] + ride_shapes,
        input_output_aliases=alias,
        scratch_shapes=[pltpu.VMEM((HEADS, tq, HEAD_PAD), F32), pltpu.VMEM((tq, MLA_W), F32), pltpu.VMEM((HEADS, tq, tq), F32),
                        pltpu.VMEM((HEADS, tq, tq), BF16), pltpu.VMEM((HEADS, tq, HEAD_PAD), F32)] + ride_scratch,
        compiler_params=_params(2),
    )(Q, K, V, *ride_args)
    return out[0], out[1], list(out[2:])


def _lane_group():
    return lax.broadcasted_iota(jnp.int32, (1, MIX_W), 1) >> GROUP_SHIFT


def _by_group(a0, a1, a2, a3):
    g = _lane_group()
    return jnp.where(g == 0, a0, jnp.where(g == 1, a1, jnp.where(g == 2, a2, a3)))


def _pool_count(si, tc, rows):
    pos = si * tc + lax.broadcasted_iota(jnp.int32, (rows, MIX_W), 0)
    win = _by_group(*POOL_WINDOWS)
    return jnp.minimum(pos + 1, win).astype(F32)


def _pool_fwd(z, z_prev, si, tc):
    ze = jnp.concatenate([z_prev, z], axis=0)
    s1 = ze + _shift_down(ze, 1)
    s2 = s1 + _shift_down(s1, 2)
    s4 = s2 + _shift_down(s2, 4)
    s8 = s4 + _shift_down(s4, 8)
    win_sum = _by_group(s1, s2, s4, s8)[POOL_HALO:]
    return win_sum / _pool_count(si, tc, tc) - z


def _sgu_weights(w_ref):
    row = lax.broadcasted_iota(jnp.int32, (SGU_BLOCK, SGU_BLOCK), 0)
    col = lax.broadcasted_iota(jnp.int32, (SGU_BLOCK, SGU_BLOCK), 1)
    keep = (row >> CHUNK_SHIFT) >= (col >> CHUNK_SHIFT)
    return keep, [jnp.where(keep, w_ref[g], 0.0).astype(BF16) for g in range(GROUPS)]


def _sgu_mix(vn_blk, wm, bias):
    g = _lane_group()
    mixed = bias
    for k in range(GROUPS):
        mixed = mixed + jnp.where(g == k, _dot(wm[k], vn_blk), 0.0)
    return mixed


def _conv_fwd(z, z_prev, w_ref):
    ze = jnp.concatenate([z_prev, z], axis=0)
    y = w_ref[0:1, :] * _shift_down(ze, 2) + w_ref[1:2, :] * _shift_down(ze, 1) + w_ref[2:3, :] * ze
    return y[CONV_HALO:]


def _mix_specs(T, D, tc):
    base = (4 * D + SEC_MLA) // MIX_W
    cur = lambda k: pl.BlockSpec((tc, MIX_W), lambda i: (i, base + k))
    prev = lambda k, halo: pl.BlockSpec((halo, MIX_W), lambda i: (jnp.maximum(i * (tc // halo) - 1, 0), base + k))
    nxt = lambda k, halo: pl.BlockSpec((halo, MIX_W), lambda i: (jnp.minimum((i + 1) * (tc // halo), T // halo - 1), base + k))
    return cur, prev, nxt


def _mixers_fwd(P, D, S, pool_bd, pool_scale, g_v, sgu_w, sgu_bias, conv_w, ride=None):
    T = P.shape[0]
    tc = _tile(S, 512, SGU_BLOCK)
    n_si = S // tc
    cur, prev, _ = _mix_specs(T, D, tc)

    def body(z_ref, zp_ref, u_ref, v_ref, b_ref, c_ref, x_ref, cp_ref, xp_ref,
             pw_ref, ps_ref, gv_ref, sw_ref, sb_ref, cw_ref, *rest):
        (ob_ref, oc_ref, od_ref), _, riding = _ride_split(ride, rest, 3, 0)
        _ride_start(riding, pl.program_id(0) == 0)
        si = pl.program_id(0) % n_si
        first = si == 0
        z = z_ref[...].astype(F32)
        pooled = _pool_fwd(z, jnp.where(first, 0.0, zp_ref[...].astype(F32)), si, tc)
        ob_ref[...] = (_dot(pooled.astype(BF16), pw_ref[...]) * ps_ref[...]).astype(BF16)

        v = v_ref[...].astype(F32)
        vn = (v * _rms_r(v) * gv_ref[...]).astype(BF16)
        _, wm = _sgu_weights(sw_ref)
        for blk in range(tc // SGU_BLOCK):
            rows = slice(blk * SGU_BLOCK, (blk + 1) * SGU_BLOCK)
            oc_ref[rows, :] = (u_ref[rows, :].astype(F32) * _sgu_mix(vn[rows], wm, sb_ref[...])).astype(BF16)

        zc = c_ref[...].astype(F32) * x_ref[...].astype(F32)
        zc_prev = jnp.where(first, 0.0, cp_ref[...].astype(F32) * xp_ref[...].astype(F32))
        od_ref[...] = (b_ref[...].astype(F32) * _conv_fwd(zc, zc_prev, cw_ref)).astype(BF16)
        _ride_finish(riding, pl.program_id(0) == T // tc - 1)

    out = pl.BlockSpec((tc, MIX_W), lambda i: (i, 0))
    ride_in, ride_out, ride_shapes, ride_scratch, ride_args, alias = _ride_specs(ride, 15, 3)
    res = pl.pallas_call(
        body, name="mixers_fwd", grid=(T // tc,),
        in_specs=[cur(0), prev(0, POOL_HALO), cur(1), cur(2), cur(3), cur(4), cur(5), prev(4, CONV_HALO), prev(5, CONV_HALO),
                  _full((MIX_W, MIX_W)), _full((1, MIX_W)), _full((1, MIX_W)), _full((GROUPS, SGU_BLOCK, SGU_BLOCK)),
                  _full((SGU_BLOCK, MIX_W)), _full((CONV_HALO, MIX_W))] + ride_in,
        out_specs=[out, out, out] + ride_out,
        out_shape=[jax.ShapeDtypeStruct((T, MIX_W), BF16)] * 3 + ride_shapes,
        input_output_aliases=alias,
        scratch_shapes=ride_scratch,
        compiler_params=_params(1),
    )(P, P, P, P, P, P, P, P, P, pool_bd, pool_scale, g_v, sgu_w, sgu_bias, conv_w, *ride_args)
    return res[0], res[1], res[2], list(res[3:])


def _merge_fwd(x, P, A, Bm, C, Dv, wa, wb, wc, wd, wout, g_post):
    T, D = x.shape
    tm = _tile(T, 512)

    def body(x_ref, lg_ref, a_ref, b_ref, c_ref, d_ref, wa_ref, wb_ref, wc_ref, wd_ref, wo_ref, g_ref,
             x1_ref, mg_ref, o_ref):
        merged = jnp.zeros((tm, D), F32)
        for k, (br, w) in enumerate(((a_ref, wa_ref), (b_ref, wb_ref), (c_ref, wc_ref), (d_ref, wd_ref))):
            merged = merged + _sigmoid(lg_ref[:, k * D:(k + 1) * D].astype(F32)) * _dot(br[...], w[...])
        mg = merged.astype(BF16)
        mg_ref[...] = mg
        o = _dot(mg, wo_ref[...])
        o_ref[...] = o
        x1_ref[...] = x_ref[...] + o * _rms_r(o) * g_ref[...]

    row = lambda w: pl.BlockSpec((tm, w), lambda i: (i, 0))
    return pl.pallas_call(
        body, name="merge_fwd", grid=(T // tm,),
        in_specs=[row(D), row(4 * D), row(MLA_W), row(MIX_W), row(MIX_W), row(MIX_W),
                  _full((MLA_W, D)), _full((MIX_W, D)), _full((MIX_W, D)), _full((MIX_W, D)), _full((D, D)), _full((1, D))],
        out_specs=[row(D), row(D), row(D)],
        out_shape=[jax.ShapeDtypeStruct((T, D), F32), jax.ShapeDtypeStruct((T, D), BF16), jax.ShapeDtypeStruct((T, D), F32)],
        compiler_params=_params(1),
    )(x, P, A, Bm, C, Dv, wa, wb, wc, wd, wout, g_post)


def _ffn_specs(T, D, Fc, l, rows=512):
    tm = _tile(T, rows)
    row = pl.BlockSpec((tm, D), lambda i, j: (i, 0))
    col = pl.BlockSpec((None, tm, Fc), lambda i, j: (j, i, 0))
    w_in = pl.BlockSpec((None, None, D, Fc), lambda i, j: (l, j, 0, 0))
    w_out = pl.BlockSpec((None, None, Fc, D), lambda i, j: (l, j, 0, 0))
    return tm, row, col, w_in, w_out


def _ffn_fwd(x1, g_pre, wg, wu, wdn, g_post, l):
    T, D = x1.shape
    nf, Fc = wg.shape[1], wg.shape[3]
    tm, row, col, w_in, w_out = _ffn_specs(T, D, Fc, l, 1024)

    def body(x_ref, gp_ref, wg_ref, wu_ref, wd_ref, gq_ref, x2_ref, h_ref, gt_ref, up_ref, f_ref, gt_s, up_s, a_s):
        j = pl.program_id(1)

        @pl.when(j == 0)
        def _():
            for r in range(0, tm, NORM_ROWS):
                rs = slice(r, r + NORM_ROWS)
                xv = x_ref[rs, :]
                h_ref[rs, :] = (xv * _rms_r(xv) * gp_ref[...]).astype(BF16)
            f_ref[...] = jnp.zeros_like(f_ref)

        gt_s[...] = _dot(h_ref[...], wg_ref[...])
        up_s[...] = _dot(h_ref[...], wu_ref[...])
        for r in range(0, tm, ACT_ROWS):
            rs = slice(r, r + ACT_ROWS)
            gt, up = gt_s[rs, :], up_s[rs, :]
            gt_ref[rs, :] = gt.astype(BF16)
            up_ref[rs, :] = up.astype(BF16)
            a_s[rs, :] = (gt * _sigmoid(gt) * up).astype(BF16)
        f_ref[...] += _dot(a_s[...], wd_ref[...])

        @pl.when(j == nf - 1)
        def _():
            for r in range(0, tm, NORM_ROWS):
                rs = slice(r, r + NORM_ROWS)
                f = f_ref[rs, :]
                x2_ref[rs, :] = x_ref[rs, :] + f * _rms_r(f) * gq_ref[...]

    return pl.pallas_call(
        body, name="ffn_fwd", grid=(T // tm, nf),
        in_specs=[row, _full((1, D)), w_in, w_in, w_out, _full((1, D))],
        out_specs=[row, row, col, col, row],
        out_shape=[jax.ShapeDtypeStruct((T, D), F32), jax.ShapeDtypeStruct((T, D), BF16),
                   jax.ShapeDtypeStruct((nf, T, Fc), BF16), jax.ShapeDtypeStruct((nf, T, Fc), BF16),
                   jax.ShapeDtypeStruct((T, D), F32)],
        scratch_shapes=[pltpu.VMEM((tm, Fc), F32), pltpu.VMEM((tm, Fc), F32), pltpu.VMEM((tm, Fc), BF16)],
        compiler_params=_params(2),
    )(x1, g_pre, wg, wu, wdn, g_post)


def _loss_grad(y, target):
    T, D = y.shape
    tm = _tile(T, 512)

    def body(y_ref, t_ref, l_ref, dy_ref):
        @pl.when(pl.program_id(0) == 0)
        def _():
            l_ref[...] = jnp.zeros_like(l_ref)

        d = y_ref[...] - t_ref[...]
        dy_ref[...] = d * (1.0 / D)
        e = jnp.sum((d * d).reshape(tm // 8, 8, D), axis=0)
        part = e[:, 0:128]
        for k in range(1, D // 128):
            part = part + e[:, k * 128:(k + 1) * 128]
        l_ref[...] += part * (0.5 / D)

    row = pl.BlockSpec((tm, D), lambda i: (i, 0))
    return pl.pallas_call(
        body, name="loss_grad", grid=(T // tm,),
        in_specs=[row, row], out_specs=[_full((8, 128)), row],
        out_shape=[jax.ShapeDtypeStruct((8, 128), F32), jax.ShapeDtypeStruct((T, D), F32)],
        compiler_params=_params(1),
    )(y, target)


def _matmul_tn(a, b, name):
    T, M = a.shape
    N = b.shape[1]
    tm, tn, tk = _tile(M, 1024, 128), _tile(N, 1536, 128), _tile(T, 1024)

    def body(a_ref, b_ref, o_ref):
        @pl.when(pl.program_id(2) == 0)
        def _():
            o_ref[...] = jnp.zeros_like(o_ref)

        o_ref[...] += _dot_tn(a_ref[...], b_ref[...])

    return pl.pallas_call(
        body, name=name, grid=(M // tm, N // tn, T // tk),
        in_specs=[pl.BlockSpec((tk, tm), lambda i, j, k: (k, i)), pl.BlockSpec((tk, tn), lambda i, j, k: (k, j))],
        out_specs=pl.BlockSpec((tm, tn), lambda i, j, k: (i, j)),
        out_shape=jax.ShapeDtypeStruct((M, N), F32),
        compiler_params=_params(3),
    )(a, b)


def _norm_bwd(dy, x, g, name, add=None, parts=None):
    T, D = x.shape
    tm = _tile(T, 256)
    n_parts = 0 if parts is None else parts.shape[0]

    def body(*refs):
        dx_ref, dg_ref = refs[-2:]
        ins = list(refs[:-2])
        dy_ref = ins.pop(0) if parts is None else None
        p_ref = ins.pop(0) if parts is not None else None
        x_ref, g_ref = ins[0], ins[1]
        add_ref = ins[2] if add is not None else None

        @pl.when(pl.program_id(0) == 0)
        def _():
            dg_ref[...] = jnp.zeros_like(dg_ref)

        dg_sum = jnp.zeros((1, D), F32)
        for r in range(0, tm, NORM_ROWS):
            rs = slice(r, r + NORM_ROWS)
            if parts is None:
                dy = dy_ref[rs, :]
            else:
                dy = p_ref[0, rs, :].astype(F32)
                for k in range(1, n_parts):
                    dy = dy + p_ref[k, rs, :].astype(F32)
            dx, dg = _rms_bwd(dy, x_ref[rs, :], g_ref[...])
            dx_ref[rs, :] = (dx if add is None else add_ref[rs, :] + dx).astype(dx_ref.dtype)
            dg_sum = dg_sum + dg
        dg_ref[...] += dg_sum

    row = pl.BlockSpec((tm, D), lambda i: (i, 0))
    first = [row] if parts is None else [pl.BlockSpec((n_parts, tm, D), lambda i: (0, i, 0))]
    operands = [dy if parts is None else parts, x, g] + ([] if add is None else [add])
    return pl.pallas_call(
        body, name=name, grid=(T // tm,),
        in_specs=first + [row, _full((1, D))] + ([] if add is None else [row]),
        out_specs=[row, _full((1, D))],
        out_shape=[jax.ShapeDtypeStruct((T, D), BF16 if add is None else F32), jax.ShapeDtypeStruct((1, D), F32)],
        compiler_params=_params(1),
    )(*operands)


FFN_WEIGHTS = ("w_ffn_gate", "w_ffn_up", "w_ffn_down")


def _ffn_bwd_main(df, gt, up, h2, wg, wu, wdn, l, bufs, ride=None):
    T, D = h2.shape
    nf, Fc = wg.shape[1], wg.shape[3]
    tm = _tile(T, 512)
    nt = T // tm
    keep = [] if bufs is None else [bufs[n] for n in FFN_WEIGHTS]

    def body(df_ref, gt_ref, up_ref, h_ref, wg_ref, wu_ref, wd_ref, *rest):
        (dh_ref, gg_ref, gu_ref, gd_ref), (da_s, dgt_s, dup_s, act_s), riding = _ride_split(ride, rest[len(keep):], 4, 4)
        j, i = pl.program_id(0), pl.program_id(1)
        _ride_start(riding, (j == 0) & (i == 0))

        @pl.when(i == 0)
        def _():
            for r in (gg_ref, gu_ref, gd_ref):
                r[...] = jnp.zeros_like(r)

        parts = [slice(p * tm // ROW_PARTS, (p + 1) * tm // ROW_PARTS) for p in range(ROW_PARTS)]
        for ps in parts:
            da_s[ps, :] = _dot_nt(df_ref[ps, :], wd_ref[...])
        for ps in parts:
            for r in range(ps.start, ps.stop, ACT_ROWS):
                rs = slice(r, r + ACT_ROWS)
                da = da_s[rs, :]
                gt = gt_ref[rs, :].astype(F32)
                u = up_ref[rs, :].astype(F32)
                sig = _sigmoid(gt)
                silu = gt * sig
                dgt_s[rs, :] = (da * u * (sig * (1.0 + gt * (1.0 - sig)))).astype(BF16)
                dup_s[rs, :] = (da * silu).astype(BF16)
                act_s[rs, :] = (silu * u).astype(BF16)
            dh_ref[ps, :] = (_dot_nt(dgt_s[ps, :], wg_ref[...]) + _dot_nt(dup_s[ps, :], wu_ref[...])).astype(BF16)
        gg_ref[...] += _dot_tn(h_ref[...], dgt_s[...])
        gu_ref[...] += _dot_tn(h_ref[...], dup_s[...])
        gd_ref[...] += _dot_tn(act_s[...], df_ref[...])
        _ride_finish(riding, (j == nf - 1) & (i == nt - 1))

    row = pl.BlockSpec((tm, D), lambda j, i: (i, 0))
    col = pl.BlockSpec((None, tm, Fc), lambda j, i: (j, i, 0))
    w_in = pl.BlockSpec((None, None, D, Fc), lambda j, i: (l, j, 0, 0))
    w_out = pl.BlockSpec((None, None, Fc, D), lambda j, i: (l, j, 0, 0))
    g_in = pl.BlockSpec((None, None, D, Fc), lambda j, i: (j, l, 0, 0))
    g_out = pl.BlockSpec((None, None, Fc, D), lambda j, i: (j, l, 0, 0))
    n_in = 7
    ride_in, ride_out, ride_shapes, ride_scratch, ride_args, alias = _ride_specs(ride, n_in + len(keep), 4)
    out = pl.pallas_call(
        body, name="ffn_bwd_main", grid=(nf, nt),
        in_specs=[row, col, col, row, w_in, w_in, w_out] + [pl.BlockSpec(memory_space=pl.ANY)] * len(keep) + ride_in,
        out_specs=[pl.BlockSpec((None, tm, D), lambda j, i: (j, i, 0)), g_in, g_in, g_out] + ride_out,
        out_shape=[jax.ShapeDtypeStruct((nf, T, D), BF16), jax.ShapeDtypeStruct((nf, DEPTH, D, Fc), F32),
                   jax.ShapeDtypeStruct((nf, DEPTH, D, Fc), F32), jax.ShapeDtypeStruct((nf, DEPTH, Fc, D), F32)] + ride_shapes,
        input_output_aliases={**{n_in + k: 1 + k for k in range(len(keep))}, **alias},
        scratch_shapes=[pltpu.VMEM((tm, Fc), F32), pltpu.VMEM((tm, Fc), BF16), pltpu.VMEM((tm, Fc), BF16),
                        pltpu.VMEM((tm, Fc), BF16)] + ride_scratch,
        compiler_params=_params(2),
    )(df, gt, up, h2, wg, wu, wdn, *keep, *ride_args)
    return out[0], dict(zip(FFN_WEIGHTS, out[1:4])), list(out[4:])


def _wgrad_out(l, n_layers, rows, cols):
    spec = pl.BlockSpec((N_CHIPS, None, rows, cols), lambda *_: (0, l, 0, 0))
    return spec, jax.ShapeDtypeStruct((N_CHIPS, n_layers, rows, cols), F32)


def _merge_bwd(dx1, o, merged, P, A, Bm, C, Dv, wa, wb, wc, wd, wout, g_post, l, bufs, ride=None):
    T, D = o.shape
    tm = _tile(T, 256)
    Dc = D // N_CHIPS
    names = ("w_out", "w_br_a", "w_br_b", "w_br_c", "w_br_d")
    keep = [] if bufs is None else [bufs[n] for n in names]

    def body(dx1_ref, o_ref, mg_ref, lg_ref, a_ref, b_ref, c_ref, d_ref, wa_ref, wb_ref, wc_ref, wd_ref, wo_ref, g_ref, *rest):
        outs, _, riding = _ride_split(ride, rest[len(keep):], 12, 0)
        dlg_ref, da_ref, db_ref, dc_ref, dd_ref, dg_ref, dt_ref, go_ref, ga_ref, gb_ref, gc_ref, gd_ref = outs
        _ride_start(riding, pl.program_id(0) == 0)

        @pl.when(pl.program_id(0) == 0)
        def _():
            for r in (dg_ref, go_ref, ga_ref, gb_ref, gc_ref, gd_ref):
                r[...] = jnp.zeros_like(r)

        d_o, dg = _rms_bwd(dx1_ref[...], o_ref[...], g_ref[...])
        dg_ref[...] += dg
        d_o = d_o.astype(BF16)
        for k in range(N_CHIPS):
            go_ref[k] += _dot_tn(mg_ref[:, k * Dc:(k + 1) * Dc], d_o)
        dm = _dot_nt(d_o, wo_ref[...])
        branches = ((a_ref, wa_ref, da_ref, ga_ref), (b_ref, wb_ref, db_ref, gb_ref),
                    (c_ref, wc_ref, dc_ref, gc_ref), (d_ref, wd_ref, dd_ref, gd_ref))
        for j, (br, w, dbr_ref, gw_ref) in enumerate(branches):
            gate = _sigmoid(lg_ref[:, j * D:(j + 1) * D].astype(F32))
            y = _dot(br[...], w[...])
            dlg_ref[:, j * D:(j + 1) * D] = (dm * y * gate * (1.0 - gate)).astype(BF16)
            dy = (dm * gate).astype(BF16)
            d_in = _dot_nt(dy, w[...]).astype(dbr_ref.dtype)
            dbr_ref[...] = d_in
            if j == 0:
                prod = d_in.astype(F32) * br[...].astype(F32)
                lane = lax.broadcasted_iota(jnp.int32, (tm, HEAD_PAD), 1)
                delta = jnp.zeros((tm, HEAD_PAD), F32)
                for h in range(HEADS):
                    row_sum = jnp.sum(prod[:, h * HEAD_PAD:(h + 1) * HEAD_PAD], axis=-1, keepdims=True)
                    delta = jnp.where(lane == h, row_sum, delta)
                _rows_to_lanes(delta, dt_ref)
            for k in range(N_CHIPS):
                gw_ref[k] += _dot_tn(br[...], dy[:, k * Dc:(k + 1) * Dc])

        _ride_finish(riding, pl.program_id(0) == T // tm - 1)

    row = lambda w: pl.BlockSpec((tm, w), lambda i: (i, 0))
    wg = [_wgrad_out(l, DEPTH, r, c) for r, c in ((Dc, D), (MLA_W, Dc), (MIX_W, Dc), (MIX_W, Dc), (MIX_W, Dc))]
    n_in = 14
    ride_in, ride_out, ride_shapes, ride_scratch, ride_args, alias = _ride_specs(ride, n_in + len(keep), 12)
    stat = pl.BlockSpec((HEADS, 1, 1, tm), lambda i: (0, i, 0, 0))
    out = pl.pallas_call(
        body, name="merge_bwd", grid=(T // tm,),
        in_specs=[row(D), row(D), row(D), row(4 * D), row(MLA_W), row(MIX_W), row(MIX_W), row(MIX_W),
                  _full((MLA_W, D)), _full((MIX_W, D)), _full((MIX_W, D)), _full((MIX_W, D)), _full((D, D)), _full((1, D))]
        + [pl.BlockSpec(memory_space=pl.ANY)] * len(keep) + ride_in,
        out_specs=[row(4 * D), row(MLA_W), row(MIX_W), row(MIX_W), row(MIX_W), _full((1, D)), stat] + [s for s, _ in wg]
        + ride_out,
        out_shape=[jax.ShapeDtypeStruct((T, 4 * D), BF16), jax.ShapeDtypeStruct((T, MLA_W), BF16)]
        + [jax.ShapeDtypeStruct((T, MIX_W), F32)] * 3 + [jax.ShapeDtypeStruct((1, D), F32),
                                                         jax.ShapeDtypeStruct((HEADS, T // tm, 1, tm), F32)]
        + [s for _, s in wg] + ride_shapes,
        input_output_aliases={**{n_in + i: 7 + i for i in range(len(keep))}, **alias},
        scratch_shapes=ride_scratch,
        compiler_params=_params(1),
    )(dx1, o, merged, P, A, Bm, C, Dv, wa, wb, wc, wd, wout, g_post, *keep, *ride_args)
    return out[:7], dict(zip(names, out[7:12])), list(out[12:])


def _mixers_bwd(P, D, S, dBm, dC, dDv, pool_bd, pool_scale, g_v, sgu_w, sgu_bias, conv_w):
    T = P.shape[0]
    tc = _tile(S, 512, SGU_BLOCK)
    n_si = S // tc
    cur, prev, nxt = _mix_specs(T, D, tc)
    n_blk = tc // SGU_BLOCK

    def body(z_ref, zp_ref, u_ref, v_ref, b_ref, c_ref, x_ref, cp_ref, xp_ref, bn_ref,
             dbm_ref, dbmn_ref, dc_ref, ddv_ref, ddvn_ref,
             pw_ref, ps_ref, gv_ref, sw_ref, sb_ref, cw_ref,
             dp_ref, dpw_ref, dps_ref, dgv_ref, dsw_ref, dsb_ref, dcw_ref, dvn_acc):
        si = pl.program_id(0) % n_si
        first, last = si == 0, si == n_si - 1

        @pl.when(pl.program_id(0) == 0)
        def _():
            for r in (dpw_ref, dps_ref, dgv_ref, dsw_ref, dsb_ref, dcw_ref):
                r[...] = jnp.zeros_like(r)

        z = z_ref[...].astype(F32)
        pooled = _pool_fwd(z, jnp.where(first, 0.0, zp_ref[...].astype(F32)), si, tc).astype(BF16)
        dbm = dbm_ref[...]
        dps_ref[...] += jnp.sum(dbm * _dot(pooled, pw_ref[...]), axis=0, keepdims=True)
        dmix = (jnp.concatenate([dbm, jnp.where(last, 0.0, dbmn_ref[...])], axis=0) * ps_ref[...]).astype(BF16)
        dpw_ref[...] += _dot_tn(pooled, dmix[:tc])
        dpool = _dot_nt(dmix, pw_ref[...])
        e = dpool / _pool_count(si, tc, tc + POOL_HALO)
        f1 = e + _shift_up(e, 1)
        f2 = f1 + _shift_up(f1, 2)
        f4 = f2 + _shift_up(f2, 4)
        f8 = f4 + _shift_up(f4, 8)
        dp_ref[:, 0:MIX_W] = (_by_group(f1, f2, f4, f8)[:tc] - dpool[:tc]).astype(BF16)

        v = v_ref[...].astype(F32)
        vn = (v * _rms_r(v) * gv_ref[...]).astype(BF16)
        keep, wm = _sgu_weights(sw_ref)
        g = _lane_group()
        for blk in range(n_blk):
            rows = slice(blk * SGU_BLOCK, (blk + 1) * SGU_BLOCK)
            vb = vn[rows]
            dc = dc_ref[rows, :]
            dp_ref[rows, MIX_W:2 * MIX_W] = (dc * _sgu_mix(vb, wm, sb_ref[...])).astype(BF16)
            dmx = dc * u_ref[rows, :].astype(F32)
            dsb_ref[...] += dmx
            dvn = jnp.zeros((SGU_BLOCK, MIX_W), F32)
            for k in range(GROUPS):
                dmk = jnp.where(g == k, dmx, 0.0).astype(BF16)
                dsw_ref[k] += jnp.where(keep, _dot_nt(dmk, vb), 0.0)
                dvn = dvn + _dot_tn(wm[k], dmk)
            dvn_acc[rows, :] = dvn
        dv, dg = _rms_bwd(dvn_acc[...], v, gv_ref[...])
        dgv_ref[...] += dg
        dp_ref[:, 2 * MIX_W:3 * MIX_W] = dv.astype(BF16)

        cg, xg, bg = c_ref[...].astype(F32), x_ref[...].astype(F32), b_ref[...].astype(F32)
        zc = cg * xg
        ze = jnp.concatenate([jnp.where(first, 0.0, cp_ref[...].astype(F32) * xp_ref[...].astype(F32)), zc], axis=0)
        z1, z2 = _shift_down(ze, 1)[CONV_HALO:], _shift_down(ze, 2)[CONV_HALO:]
        ddv = ddv_ref[...]
        y = cw_ref[0:1, :] * z2 + cw_ref[1:2, :] * z1 + cw_ref[2:3, :] * zc
        dp_ref[:, 3 * MIX_W:4 * MIX_W] = (ddv * y).astype(BF16)
        dy = ddv * bg
        dcw_ref[0:1, :] += jnp.sum(dy * z2, axis=0, keepdims=True)
        dcw_ref[1:2, :] += jnp.sum(dy * z1, axis=0, keepdims=True)
        dcw_ref[2:3, :] += jnp.sum(dy * zc, axis=0, keepdims=True)
        dye = jnp.concatenate([dy, jnp.where(last, 0.0, ddvn_ref[...] * bn_ref[...].astype(F32))], axis=0)
        dz = (cw_ref[2:3, :] * dye + cw_ref[1:2, :] * _shift_up(dye, 1) + cw_ref[0:1, :] * _shift_up(dye, 2))[:tc]
        dp_ref[:, 4 * MIX_W:5 * MIX_W] = (dz * xg).astype(BF16)
        dp_ref[:, 5 * MIX_W:6 * MIX_W] = (dz * cg).astype(BF16)

    grad = lambda halo: pl.BlockSpec((halo, MIX_W), lambda i: (jnp.minimum((i + 1) * (tc // halo), T // halo - 1), 0))
    out = pl.BlockSpec((tc, MIX_W), lambda i: (i, 0))
    return pl.pallas_call(
        body, name="mixers_bwd", grid=(T // tc,),
        in_specs=[cur(0), prev(0, POOL_HALO), cur(1), cur(2), cur(3), cur(4), cur(5), prev(4, CONV_HALO), prev(5, CONV_HALO),
                  nxt(3, CONV_HALO), out, grad(POOL_HALO), out, out, grad(CONV_HALO),
                  _full((MIX_W, MIX_W)), _full((1, MIX_W)), _full((1, MIX_W)), _full((GROUPS, SGU_BLOCK, SGU_BLOCK)),
                  _full((SGU_BLOCK, MIX_W)), _full((CONV_HALO, MIX_W))],
        out_specs=[pl.BlockSpec((tc, SEC_MIX), lambda i: (i, 0)), _full((MIX_W, MIX_W)), _full((1, MIX_W)), _full((1, MIX_W)),
                   _full((GROUPS, SGU_BLOCK, SGU_BLOCK)), _full((SGU_BLOCK, MIX_W)), _full((CONV_HALO, MIX_W))],
        out_shape=[jax.ShapeDtypeStruct((T, SEC_MIX), BF16), jax.ShapeDtypeStruct((MIX_W, MIX_W), F32),
                   jax.ShapeDtypeStruct((1, MIX_W), F32), jax.ShapeDtypeStruct((1, MIX_W), F32),
                   jax.ShapeDtypeStruct((GROUPS, SGU_BLOCK, SGU_BLOCK), F32), jax.ShapeDtypeStruct((SGU_BLOCK, MIX_W), F32),
                   jax.ShapeDtypeStruct((CONV_HALO, MIX_W), F32)],
        scratch_shapes=[pltpu.VMEM((tc, MIX_W), F32)],
        compiler_params=_params(1),
    )(P, P, P, P, P, P, P, P, P, P, dBm, dBm, dC, dDv, dDv, pool_bd, pool_scale, g_v, sgu_w, sgu_bias, conv_w)


def _attn_tile(S):
    return _tile(S, 256, 128)


def _flash_bwd(Q, K, V, dO, lse_t, delta_t, S, ride=None):
    T = Q.shape[0]
    n_seq = T // S
    tq = _attn_tile(S)
    nq = S // tq

    def body(k_ref, v_ref, q_ref, do_ref, lse_ref, dl_ref, *rest):
        (dq_ref, dk_ref, dv_ref), (s_s, dp_s, p_s, ds_s), riding = _ride_split(ride, rest, 3, 4)
        kb = pl.program_id(1)
        _ride_start(riding, (pl.program_id(0) == 0) & (kb == 0))

        @pl.when(kb == 0)
        def _():
            dq_ref[...] = jnp.zeros_like(dq_ref)

        dk_ref[...] = jnp.zeros_like(dk_ref)
        dv_ref[...] = jnp.zeros_like(dv_ref)

        def block(qi, masked):
            rows = pl.ds(pl.multiple_of(qi * tq, tq), tq)
            for h in range(HEADS):
                hs = slice(h * HEAD_PAD, (h + 1) * HEAD_PAD)
                s_s[h] = _dot_nt(k_ref[:, hs], q_ref[rows, hs])
                dp_s[h] = _dot_nt(v_ref[:, hs], do_ref[rows, hs])
            for h in range(HEADS):
                lse_row, dl_row = lse_ref[h, qi], dl_ref[h, qi]
                for r in range(0, tq, SOFTMAX_ROWS):
                    rs = slice(r, r + SOFTMAX_ROWS)
                    s = s_s[h, rs, :]
                    if masked:
                        key = r + lax.broadcasted_iota(jnp.int32, (SOFTMAX_ROWS, tq), 0)
                        query = lax.broadcasted_iota(jnp.int32, (SOFTMAX_ROWS, tq), 1)
                        s = jnp.where((query >> CHUNK_SHIFT) >= (key >> CHUNK_SHIFT), s, NEG_INF)
                    p = jnp.exp2(s - lse_row)
                    p_s[h, rs, :] = p.astype(BF16)
                    ds_s[h, rs, :] = (p * (dp_s[h, rs, :] - dl_row)).astype(BF16)
            for h in range(HEADS):
                hs = slice(h * HEAD_PAD, (h + 1) * HEAD_PAD)
                dv_ref[:, hs] += _dot(p_s[h], do_ref[rows, hs])
                dk_ref[:, hs] += _dot(ds_s[h], q_ref[rows, hs])
                dq_ref[rows, hs] += _dot_tn(ds_s[h], k_ref[:, hs])

        def full_block(qi, carry):
            block(qi, False)
            return carry

        block(kb, True)
        lax.fori_loop(kb + 1, nq, full_block, 0)
        dk_ref[...] = dk_ref[...] * LN2
        _ride_finish(riding, (pl.program_id(0) == n_seq - 1) & (kb == nq - 1))

    tile = pl.BlockSpec((tq, MLA_W), lambda b, i: (b * nq + i, 0))
    seq = pl.BlockSpec((S, MLA_W), lambda b, i: (b, 0))
    stat = pl.BlockSpec((HEADS, nq, 1, tq), lambda b, i: (0, b, 0, 0))
    ride_in, ride_out, ride_shapes, ride_scratch, ride_args, alias = _ride_specs(ride, 6, 3)
    out = pl.pallas_call(
        body, name="flash_bwd", grid=(n_seq, nq),
        in_specs=[tile, tile, seq, seq, stat, stat] + ride_in,
        out_specs=[seq, tile, tile] + ride_out,
        out_shape=[jax.ShapeDtypeStruct((T, MLA_W), F32)] * 3 + ride_shapes,
        input_output_aliases=alias,
        scratch_shapes=[pltpu.VMEM((HEADS, tq, tq), F32), pltpu.VMEM((HEADS, tq, tq), F32),
                        pltpu.VMEM((HEADS, tq, tq), BF16), pltpu.VMEM((HEADS, tq, tq), BF16)] + ride_scratch,
        compiler_params=_params(2),
    )(K, V, Q, dO, lse_t, delta_t, *ride_args)
    return out[0], out[1], out[2], list(out[3:])


def _mla_bwd_post(P, D, S, dQ, dK, dV, hq, hkv, g_cq, g_ckv, wuq, wukv, rope_c, rope_sa, rope_sb, l, bufs):
    T = P.shape[0]
    tm = _tile(S, 512)
    n_si = S // tm
    base = 4 * D
    names = ("w_uq", "w_ukv")

    def body(cq_ref, ckv_ref, dq_ref, dk_ref, dv_ref, hq_ref, hkv_ref, gq_ref, gkv_ref, wq_ref, wkv_ref, c_ref, sa_ref, sb_ref,
             *rest):
        dp_ref, dgq_ref, dgkv_ref, guq_ref, gukv_ref = rest[-5:]

        @pl.when(pl.program_id(0) == 0)
        def _():
            for r in (dgq_ref, dgkv_ref, guq_ref, gukv_ref):
                r[...] = jnp.zeros_like(r)

        c, sa, sb = c_ref[...], sa_ref[...], sb_ref[...]
        dq = _rope_t(dq_ref[...] * ATTN_SCALE, jnp.tile(c, (1, HEADS)), jnp.tile(sa, (1, HEADS)),
                     jnp.tile(sb, (1, HEADS))).astype(BF16)
        dcq, dg = _rms_bwd(_dot_nt(dq, wq_ref[...]), cq_ref[...].astype(F32), gq_ref[...])
        dgq_ref[...] += dg
        dp_ref[:, 0:Q_LORA] = dcq.astype(BF16)

        dk = dk_ref[...]
        dkb, dvb = dk.astype(BF16), dv_ref[...].astype(BF16)
        dkv = jnp.concatenate([p[:, j * CHIP_HEADS_W:(j + 1) * CHIP_HEADS_W] for j in range(N_CHIPS) for p in (dkb, dvb)], axis=1)
        for k in range(N_CHIPS):
            guq_ref[k] += _dot_tn(hq_ref[...], dq[:, k * CHIP_HEADS_W:(k + 1) * CHIP_HEADS_W])
            gukv_ref[k] += _dot_tn(hkv_ref[...], dkv[:, k * CHIP_KV:(k + 1) * CHIP_KV])
        dckv, dg = _rms_bwd(_dot_nt(dkv, wkv_ref[...]), ckv_ref[...].astype(F32), gkv_ref[...])
        dgkv_ref[...] += dg
        dp_ref[:, Q_LORA:Q_LORA + KV_LORA] = dckv.astype(BF16)

        dkr = dk[:, 0:HEAD_PAD]
        for h in range(1, HEADS):
            dkr = dkr + dk[:, h * HEAD_PAD:(h + 1) * HEAD_PAD]
        lane = lax.broadcasted_iota(jnp.int32, (1, HEAD_PAD), 1)
        rope_lanes = (lane >= QK_NOPE) & (lane < QK_NOPE + QK_ROPE)
        dp_ref[:, Q_LORA + KV_LORA:SEC_MLA] = jnp.where(rope_lanes, _rope_t(dkr, c, sa, sb), 0.0).astype(BF16)

    tab = pl.BlockSpec((tm, HEAD_PAD), lambda i: (i % n_si, 0))
    row = lambda w: pl.BlockSpec((tm, w), lambda i: (i, 0))
    wg = [_wgrad_out(l, DEPTH, Q_LORA, CHIP_HEADS_W), _wgrad_out(l, DEPTH, KV_LORA, CHIP_KV)]
    keep = [] if bufs is None else [bufs[n] for n in names]
    n_in = 14
    out = pl.pallas_call(
        body, name="mla_bwd_post", grid=(T // tm,),
        in_specs=[pl.BlockSpec((tm, Q_LORA), lambda i: (i, base // Q_LORA)),
                  pl.BlockSpec((tm, KV_LORA), lambda i: (i, (base + Q_LORA) // KV_LORA)),
                  row(MLA_W), row(MLA_W), row(MLA_W), row(Q_LORA), row(KV_LORA),
                  _full((1, Q_LORA)), _full((1, KV_LORA)), _full((Q_LORA, MLA_W)), _full((KV_LORA, 2 * MLA_W)), tab, tab, tab]
        + [pl.BlockSpec(memory_space=pl.ANY)] * len(keep),
        out_specs=[row(SEC_MLA), _full((1, Q_LORA)), _full((1, KV_LORA))] + [s for s, _ in wg],
        out_shape=[jax.ShapeDtypeStruct((T, SEC_MLA), BF16), jax.ShapeDtypeStruct((1, Q_LORA), F32),
                   jax.ShapeDtypeStruct((1, KV_LORA), F32)] + [s for _, s in wg],
        input_output_aliases={n_in + i: 3 + i for i in range(len(keep))},
        compiler_params=_params(1),
    )(P, P, dQ, dK, dV, hq, hkv, g_cq, g_ckv, wuq, wukv, rope_c, rope_sa, rope_sb, *keep)
    return out[:3], dict(zip(names, out[3:]))


def _proj_bwd(dx1, x, g, dPg, dPa, dPm, w_gates, w_mla, w_mix, ride=None):
    T, D = x.shape
    tm = _tile(T, 256)

    def body(dx1_ref, x_ref, g_ref, dg_ref_in, da_ref, dm_ref, wg_ref, wa_ref, wm_ref, *rest):
        (dx_ref, dg_ref), _, riding = _ride_split(ride, rest, 2, 0)
        _ride_start(riding, pl.program_id(0) == 0)

        @pl.when(pl.program_id(0) == 0)
        def _():
            dg_ref[...] = jnp.zeros_like(dg_ref)

        dh = _dot_nt(dg_ref_in[...], wg_ref[...]) + _dot_nt(da_ref[...], wa_ref[...]) + _dot_nt(dm_ref[...], wm_ref[...])
        dx, dg = _rms_bwd(dh, x_ref[...], g_ref[...])
        dx_ref[...] = dx1_ref[...] + dx
        dg_ref[...] += dg
        _ride_finish(riding, pl.program_id(0) == T // tm - 1)

    row = lambda w: pl.BlockSpec((tm, w), lambda i: (i, 0))
    ride_in, ride_out, ride_shapes, ride_scratch, ride_args, alias = _ride_specs(ride, 9, 2)
    out = pl.pallas_call(
        body, name="proj_bwd", grid=(T // tm,),
        in_specs=[row(D), row(D), _full((1, D)), row(4 * D), row(SEC_MLA), row(SEC_MIX),
                  _full((D, 4 * D)), _full((D, SEC_MLA)), _full((D, SEC_MIX))] + ride_in,
        out_specs=[row(D), _full((1, D))] + ride_out,
        out_shape=[jax.ShapeDtypeStruct((T, D), F32), jax.ShapeDtypeStruct((1, D), F32)] + ride_shapes,
        input_output_aliases=alias, scratch_shapes=ride_scratch,
        compiler_params=_params(1),
    )(dx1, x, g, dPg, dPa, dPm, w_gates, w_mla, w_mix, *ride_args)
    return out[0], out[1], list(out[2:])


def _adamw(w, g, m, v, name):
    R, C = w.shape
    tr = _tile(R, max(8, (1 << 19) // C))

    def body(w_ref, g_ref, m_ref, v_ref, d_ref, mo_ref, vo_ref):
        gv = g_ref[...]
        mn = ADAM_B1 * m_ref[...] + (1.0 - ADAM_B1) * gv
        vn = ADAM_B2 * v_ref[...] + (1.0 - ADAM_B2) * (gv * gv)
        mo_ref[...] = mn
        vo_ref[...] = vn
        m_hat = mn / (1.0 - ADAM_B1 ** ADAM_STEP)
        v_hat = vn / (1.0 - ADAM_B2 ** ADAM_STEP)
        d_ref[...] = -ADAM_LR * (m_hat / (jnp.sqrt(v_hat) + ADAM_EPS) + ADAM_WD * w_ref[...])

    blk = pl.BlockSpec((tr, C), lambda i: (i, 0))
    return pl.pallas_call(
        body, name=name, grid=(R // tr,), in_specs=[blk] * 4, out_specs=[blk] * 3,
        out_shape=[jax.ShapeDtypeStruct((R, C), F32)] * 3, compiler_params=_params(1),
    )(w, g, m, v)


def _rows_tile(rows, cols):
    return _tile(rows, max(16, (1 << 19) // cols), 16)


def _add_halves(G, span, recv, half, name):
    n, L, R, C = G.shape
    l0, nl = span[0], span[1] - span[0]
    hr = R // 2
    tr = _rows_tile(hr, C)
    nb = hr // tr

    def body(half_ref, g_ref, r_ref, o_ref):
        o_ref[...] = (g_ref[...] + r_ref[...]).astype(BF16)

    grid_spec = pltpu.PrefetchScalarGridSpec(
        num_scalar_prefetch=1, grid=(n, nl, nb),
        in_specs=[pl.BlockSpec((None, None, tr, C), lambda k, l, i, h: (k, l0 + l, h[0] * nb + i, 0)),
                  pl.BlockSpec((None, None, tr, C), lambda k, l, i, h: (k, l, i, 0))],
        out_specs=pl.BlockSpec((None, None, tr, C), lambda k, l, i, h: (k, l, i, 0)))
    return pl.pallas_call(
        body, name="rs_add_halves_" + name, grid_spec=grid_spec,
        out_shape=jax.ShapeDtypeStruct((n, nl, hr, C), BF16), compiler_params=_params(3),
    )(half.reshape(1).astype(jnp.int32), G, recv)


def _sum_slots(H, slots, place, name, l0, n_layers, prev=None):
    _, nl, hr, C = slots.shape
    tr = _rows_tile(hr, C)
    nb = hr // tr

    def body(x_ref, y_ref, c_ref, own_ref, s1_ref, s2_ref, s3_ref, *rest):
        o_ref = rest[-1]
        o_ref[...] = ((own_ref[...].astype(F32) + s1_ref[...].astype(F32)) + s2_ref[...].astype(F32)) + s3_ref[...].astype(F32)

    def src(fx, fy):
        def index(l, j, px, py, pc):
            cx = px[0] + fx - 2 * fx * px[0]
            cy = py[0] + fy - 2 * fy * py[0]
            return (2 * cx + cy, l, j, 0)
        return pl.BlockSpec((None, None, tr, C), index)

    keep = [] if prev is None else [prev]
    grid_spec = pltpu.PrefetchScalarGridSpec(
        num_scalar_prefetch=3, grid=(nl, nb),
        in_specs=[src(0, 0), src(0, 1), src(1, 0), src(1, 1)] + [pl.BlockSpec(memory_space=pl.ANY)] * len(keep),
        out_specs=pl.BlockSpec((None, tr, C), lambda l, j, px, py, pc: (l0 + l, pc[0] * nb + j, 0)))
    return pl.pallas_call(
        body, name="rs_sum_slots_" + name, grid_spec=grid_spec,
        out_shape=jax.ShapeDtypeStruct((n_layers, 2 * hr, C), F32),
        input_output_aliases={7: 0} if keep else {}, compiler_params=_params(2),
    )(*place, H, slots, slots, slots, *keep)


HBM = pl.BlockSpec(memory_space=pltpu.HBM)


def _place():
    x, y, c = lax.axis_index("x"), lax.axis_index("y"), lax.axis_index("c")
    return x, y, c, 2 * x + y


def _chip_device(chip, c):
    return (chip // 2, chip % 2, c)


def _remote(src, dst, send_sem, recv_sem, to):
    return pltpu.make_async_remote_copy(src_ref=src, dst_ref=dst, send_sem=send_sem, recv_sem=recv_sem, device_id=to,
                                        device_id_type=MESH)


def _place_own(w, chip, name):
    L, R, C = w.shape
    tr = _rows_tile(R, C)

    def body(p_ref, w_ref, o_ref):
        o_ref[...] = w_ref[...].astype(BF16)

    grid_spec = pltpu.PrefetchScalarGridSpec(
        num_scalar_prefetch=1, grid=(L, R // tr), in_specs=[pl.BlockSpec((None, tr, C), lambda l, j, p: (l, j, 0))],
        out_specs=pl.BlockSpec((None, None, tr, C), lambda l, j, p: (l, p[0], j, 0)))
    return pl.pallas_call(
        body, name="place_" + name, grid_spec=grid_spec,
        out_shape=jax.ShapeDtypeStruct((L, N_CHIPS, R, C), BF16), compiler_params=_params(2),
    )(chip.reshape(1).astype(jnp.int32), w)


def _gather_weights(bufs, l):
    n = len(bufs)

    def body(*refs):
        o_refs = refs[n:2 * n]
        send_sems, recv_sems = refs[2 * n:]
        x, y, c, me = _place()
        sibling = (x, y, 1 - c)

        def copy(t, k, chip, half, to):
            hr = o_refs[t].shape[2] // 2
            block = o_refs[t].at[l, chip, pl.ds(half * hr, hr), :]
            return _remote(block, block, send_sems.at[6 * t + k], recv_sems.at[6 * t + k], to)

        first = [copy(t, d - 1, me, c, _chip_device(me ^ d, c)) for t in range(n) for d in (1, 2, 3)]
        for cp in first:
            cp.start()
        passed = []
        for t in range(n):
            for d in (1, 2, 3):
                copy(t, d - 1, me ^ d, c, sibling).wait_recv()
                passed.append(copy(t, 2 + d, me ^ d, c, sibling))
                passed[-1].start()
        for t in range(n):
            for d in (1, 2, 3):
                copy(t, 2 + d, me ^ d, 1 - c, sibling).wait_recv()
        for cp in first + passed:
            cp.wait_send()

    return pl.pallas_call(
        body, name="gather_weights", in_specs=[HBM] * n, out_specs=[HBM] * n,
        out_shape=[jax.ShapeDtypeStruct(b.shape, b.dtype) for b in bufs],
        input_output_aliases={t: t for t in range(n)},
        scratch_shapes=[pltpu.SemaphoreType.DMA((6 * n,)), pltpu.SemaphoreType.DMA((6 * n,))],
    )(*bufs)


def _join_halves(bufs):
    n = len(bufs)

    def body(*refs):
        o_refs = refs[n:2 * n]
        send_sems, recv_sems = refs[2 * n:]
        x, y, c, _ = _place()

        def half(t, which):
            hr = o_refs[t].shape[1] // 2
            return o_refs[t].at[:, pl.ds(which * hr, hr), :]

        sends = [_remote(half(t, c), half(t, c), send_sems.at[t], recv_sems.at[t], (x, y, 1 - c)) for t in range(n)]
        for cp in sends:
            cp.start()
        for t in range(n):
            _remote(half(t, 1 - c), half(t, 1 - c), send_sems.at[t], recv_sems.at[t], (x, y, 1 - c)).wait_recv()
        for cp in sends:
            cp.wait_send()

    return pl.pallas_call(
        body, name="rs_join_halves", in_specs=[HBM] * n, out_specs=[HBM] * n,
        out_shape=[jax.ShapeDtypeStruct(b.shape, b.dtype) for b in bufs],
        input_output_aliases={t: t for t in range(n)},
        scratch_shapes=[pltpu.SemaphoreType.DMA((n,)), pltpu.SemaphoreType.DMA((n,))],
    )(*bufs)


def _all_reduce_small(v, name):
    R, C = v.shape

    def body(v_ref, o_ref, slots, send_sems, recv_sems):
        x, y, c, _ = _place()
        me = 4 * x + 2 * y + c
        slots[me] = v_ref[...]
        sends = []
        for d in range(1, 8):
            peer = me ^ d
            sends.append(pltpu.make_async_remote_copy(
                src_ref=v_ref, dst_ref=slots.at[me], send_sem=send_sems.at[d - 1], recv_sem=recv_sems.at[d - 1],
                device_id=(peer // 4, (peer // 2) % 2, peer % 2), device_id_type=MESH))
        for cp in sends:
            cp.start()
        for d in range(1, 8):
            peer = me ^ d
            pltpu.make_async_remote_copy(
                src_ref=v_ref, dst_ref=slots.at[peer], send_sem=send_sems.at[d - 1], recv_sem=recv_sems.at[d - 1],
                device_id=(peer // 4, (peer // 2) % 2, peer % 2), device_id_type=MESH).wait_recv()
        for cp in sends:
            cp.wait_send()
        acc = slots[0]
        for k in range(1, 8):
            acc = acc + slots[k]
        o_ref[...] = acc

    vm = pl.BlockSpec(memory_space=pltpu.VMEM)
    return pl.pallas_call(
        body, name=name, in_specs=[vm], out_specs=vm, out_shape=jax.ShapeDtypeStruct((R, C), F32),
        scratch_shapes=[pltpu.VMEM((8, R, C), F32), pltpu.SemaphoreType.DMA((7,)), pltpu.SemaphoreType.DMA((7,))],
    )(v)


SHARDED = ("w_in", "w_uq", "w_ukv", "conv_w", "w_br_a", "w_br_b", "w_br_c", "w_br_d", "w_out", "w_ffn_gate", "w_ffn_up",
           "w_ffn_down")
ROW_SHARDED = ("w_out", "w_ffn_down")
REPLICATED = ("g_pre_mix", "g_cq", "g_ckv", "pool_w", "pool_scale", "g_sgu_v", "sgu_w", "sgu_b", "g_post_mix", "g_pre_ffn",
              "g_post_ffn")
WEIGHTS = ("w_in", "g_pre_mix", "g_cq", "g_ckv", "w_uq", "w_ukv", "pool_w", "pool_scale", "g_sgu_v", "sgu_w", "sgu_b",
           "conv_w", "w_br_a", "w_br_b", "w_br_c", "w_br_d", "w_out", "g_post_mix", "g_pre_ffn", "w_ffn_gate", "w_ffn_up",
           "w_ffn_down", "g_post_ffn")
GATHERED = tuple(n for n in SHARDED if n != "conv_w")


def _unpack(packed, shapes):
    flat = packed.reshape(-1)
    out, o = [], 0
    for s in shapes:
        n = int(np.prod(s))
        out.append(flat[o:o + n].reshape(s))
        o += n
    return out


def _join_cols(g, l):
    return jnp.concatenate([g[l, k] for k in range(N_CHIPS)], axis=1)


def _pad_heads(w, real):
    lead = w.shape[:-1]
    w = w.reshape(lead + (HEADS, real))
    return jnp.pad(w, [(0, 0)] * len(lead) + [(0, 0), (0, HEAD_PAD - real)]).reshape(lead + (MLA_W,))


def _unpad_heads(w, real):
    lead = w.shape[:-1]
    return w.reshape(lead + (HEADS, HEAD_PAD))[..., :real].reshape(lead + (HEADS * real,))


IN_OFFSETS = {"cq": 0, "ckv": Q_LORA, "kr": Q_LORA + KV_LORA, "mix": Q_LORA + KV_LORA + QK_ROPE}
IN_GATES = Q_LORA + KV_LORA + QK_ROPE + SEC_MIX


def _pad_w_in(w):
    K = w.shape[0]
    z = lambda n: jnp.zeros((K, n), w.dtype)
    return jnp.concatenate([w[:, IN_GATES:], w[:, :IN_OFFSETS["kr"]], z(QK_NOPE), w[:, IN_OFFSETS["kr"]:IN_OFFSETS["mix"]],
                            z(HEAD_PAD - QK_NOPE - QK_ROPE), w[:, IN_OFFSETS["mix"]:IN_GATES]], axis=1)


def _unpad_w_in(d_gates, d_mla, d_mix):
    kr = d_mla[:, Q_LORA + KV_LORA + QK_NOPE:Q_LORA + KV_LORA + QK_NOPE + QK_ROPE]
    return jnp.concatenate([d_mla[:, :Q_LORA + KV_LORA], kr, d_mix, d_gates], axis=1)


def _rope_tables(S):
    half = QK_ROPE // 2
    inv = ROPE_THETA ** (-jnp.arange(0, QK_ROPE, 2, dtype=F32) / QK_ROPE)
    ang = jnp.arange(S, dtype=F32)[:, None] * inv[None, :]
    cos, sin = jnp.cos(ang), jnp.sin(ang)
    one, zero = jnp.ones((S, QK_NOPE), F32), jnp.zeros((S, half), F32)
    tail = HEAD_PAD - QK_NOPE - QK_ROPE
    c = jnp.concatenate([one, cos, cos, jnp.ones((S, tail), F32)], axis=1)
    sa = jnp.concatenate([0 * one, zero, sin, jnp.zeros((S, tail), F32)], axis=1)
    sb = jnp.concatenate([0 * one, -sin, zero, jnp.zeros((S, tail), F32)], axis=1)
    return c, sa, sb


def _layer_weights(gathered, full, l, D):
    w = {}
    w_in = _pad_w_in(_join_cols(gathered["w_in"], l))
    w["w_in"] = w_in
    w["w_in_gates"], w["w_in_mla"], w["w_in_mix"] = w_in[:, :4 * D], w_in[:, 4 * D:4 * D + SEC_MLA], w_in[:, 4 * D + SEC_MLA:]
    w["w_uq"] = _pad_heads(_join_cols(gathered["w_uq"], l), QK_NOPE + QK_ROPE)
    ukv = _join_cols(gathered["w_ukv"], l).reshape(KV_LORA, HEADS, QK_NOPE + V_HEAD)
    pad = ((0, 0), (0, 0), (0, HEAD_PAD - QK_NOPE))
    k_pad = jnp.pad(ukv[:, :, :QK_NOPE], pad).reshape(KV_LORA, N_CHIPS, CHIP_HEADS_W)
    v_pad = jnp.pad(ukv[:, :, QK_NOPE:], pad).reshape(KV_LORA, N_CHIPS, CHIP_HEADS_W)
    w["w_ukv"] = jnp.concatenate([k_pad, v_pad], axis=2).reshape(KV_LORA, 2 * MLA_W)
    w["w_br_a"] = jnp.pad(_join_cols(gathered["w_br_a"], l).reshape(HEADS, V_HEAD, D),
                          ((0, 0), (0, HEAD_PAD - V_HEAD), (0, 0))).reshape(MLA_W, D)
    for n in ("w_br_b", "w_br_c", "w_br_d"):
        w[n] = _join_cols(gathered[n], l)
    w["w_out"] = gathered["w_out"][l].reshape(D, D)
    for n in ("g_pre_mix", "g_cq", "g_ckv", "pool_scale", "g_sgu_v", "g_post_mix", "g_pre_ffn", "g_post_ffn"):
        w[n] = full[n][l].reshape(1, -1)
    pw = full["pool_w"][l]
    w["pool_bd"] = jax.scipy.linalg.block_diag(*[pw[g] for g in range(GROUPS)]).astype(BF16)
    w["sgu_w"] = full["sgu_w"][l]
    w["sgu_bias"] = jnp.repeat(full["sgu_b"][l].T, GROUP_DIM, axis=1)
    w["conv_w"] = jnp.pad(full["conv_w"][l].reshape(3, MIX_W), ((0, CONV_HALO - 3), (0, 0)))
    return w


def _layer_fwd(x, w, gathered, l, S, rope, gather_next):
    D = x.shape[1]
    bufs = [gathered[n] for n in GATHERED]
    P, h = _norm_matmul(x, w["g_pre_mix"], w["w_in"], "proj_fwd")
    Q, K, V, hq, hkv = _mla_prep(P, D, w["g_cq"], w["g_ckv"], w["w_uq"], w["w_ukv"], *rope, S)
    A, lse, bufs = _flash_fwd(Q, K, V, S, _gather_ride(bufs, l + 1, "chips") if gather_next else None)
    Bm, C, Dv, bufs = _mixers_fwd(P, D, S, w["pool_bd"], w["pool_scale"], w["g_sgu_v"], w["sgu_w"], w["sgu_bias"], w["conv_w"],
                                  _gather_ride(bufs, l + 1, "cores") if gather_next else None)
    if gather_next:
        gathered = dict(zip(GATHERED, bufs))
    x1, merged, o = _merge_fwd(x, P, A, Bm, C, Dv, w["w_br_a"], w["w_br_b"], w["w_br_c"], w["w_br_d"], w["w_out"], w["g_post_mix"])
    x2, h2, gt, up, f = _ffn_fwd(x1, w["g_pre_ffn"], gathered["w_ffn_gate"], gathered["w_ffn_up"], gathered["w_ffn_down"],
                                 w["g_post_ffn"], l)
    saved = dict(x=x, P=P, h=h, Q=Q, K=K, V=V, hq=hq, hkv=hkv, A=A, lse=lse, Bm=Bm, C=C, Dv=Dv, x1=x1, merged=merged, o=o,
                 h2=h2, gt=gt, up=up, f=f)
    return x2, saved, gathered


RIDE_SETS = (("w_ffn_gate", "w_out", "w_br_a"), ("w_in", "w_br_b", "w_br_c", "w_br_d", "w_uq", "w_ukv"),
             ("w_ffn_up", "w_ffn_down"))


def _layer_bwd(dx2, w, gathered, l, s, S, rope, bufs, early=None, send_ffn=False):
    D = dx2.shape[1]
    g = {}
    others = [n for n in GATHERED if n not in FFN_WEIGHTS]

    def scatter(k):
        return _scatter_ride([sums[n] for n in RIDE_SETS[k]]) if early else None

    sums, slots = {}, {}
    if early:
        done, spans, half, got_ffn = early
        ride = _exchange_ride([done[n] for n in others], [spans[n] for n in others])
    df, g["g_post_ffn"] = _norm_bwd(dx2, s["f"], w["g_post_ffn"], "ffn_bwd_pre")
    dh_parts, filled, got = _ffn_bwd_main(df, s["gt"], s["up"], s["h2"], gathered["w_ffn_gate"], gathered["w_ffn_up"],
                                          gathered["w_ffn_down"], l, bufs if "w_ffn_gate" in bufs else None,
                                          ride if early else None)
    bufs.update(filled)
    dx1, g["g_pre_ffn"] = _norm_bwd(None, s["x1"], w["g_pre_ffn"], "ffn_bwd_post", add=dx2, parts=dh_parts)
    if early:
        sums = {n: _add_halves(done[n], spans[n], r, half, n) for n, r in zip(others, got)}
        sums.update({n: _add_halves(filled[n], spans[n], got_ffn[n], half, n) for n in FFN_WEIGHTS})
        sums, dx1 = lax.optimization_barrier((sums, dx1))

    (dPg, dA, dBm, dC, dDv, g["g_post_mix"], delta_t), filled, got = _merge_bwd(
        dx1, s["o"], s["merged"], s["P"], s["A"], s["Bm"], s["C"], s["Dv"], w["w_br_a"], w["w_br_b"], w["w_br_c"], w["w_br_d"],
        w["w_out"], w["g_post_mix"], l, bufs if "w_out" in bufs else None, scatter(0))
    bufs.update(filled)
    slots.update(zip(RIDE_SETS[0], got))

    dPm, d_pool_bd, g_ps, g_gv, g["sgu_w"], d_bias, d_cw = _mixers_bwd(
        s["P"], D, S, dBm, dC, dDv, w["pool_bd"], w["pool_scale"], w["g_sgu_v"], w["sgu_w"], w["sgu_bias"], w["conv_w"])
    g["pool_w"] = jnp.stack([d_pool_bd[k * GROUP_DIM:(k + 1) * GROUP_DIM, k * GROUP_DIM:(k + 1) * GROUP_DIM] for k in range(GROUPS)])
    g["pool_scale"], g["g_sgu_v"] = g_ps, g_gv
    g["sgu_b"] = d_bias.reshape(SGU_BLOCK, GROUPS, GROUP_DIM).sum(-1).T
    g["conv_w"] = d_cw[:3].reshape(3, 1, MIX_W)

    dQ, dK, dV, got = _flash_bwd(s["Q"], s["K"], s["V"], dA, s["lse"], delta_t, S, scatter(1))
    slots.update(zip(RIDE_SETS[1], got))
    (dPa, g["g_cq"], g["g_ckv"]), filled = _mla_bwd_post(
        s["P"], D, S, dQ, dK, dV, s["hq"], s["hkv"], w["g_cq"], w["g_ckv"], w["w_uq"], w["w_ukv"], *rope, l,
        bufs if "w_uq" in bufs else None)
    bufs.update(filled)

    ride = scatter(2)
    if send_ffn:
        ride = _exchange_ride([bufs[n] for n in FFN_WEIGHTS], [(l, DEPTH)] * len(FFN_WEIGHTS))
    dx, g["g_pre_mix"], got = _proj_bwd(dx1, s["x"], w["g_pre_mix"], dPg, dPa, dPm, w["w_in_gates"], w["w_in_mla"],
                                        w["w_in_mix"], ride)
    if send_ffn:
        slots = dict(zip(FFN_WEIGHTS, got))
    else:
        slots.update(zip(RIDE_SETS[2], got))
    d_w_in = _unpad_w_in(_matmul_tn(s["h"], dPg, "wgrad_in_gates"), _matmul_tn(s["h"], dPa, "wgrad_in_mla"),
                         _matmul_tn(s["h"], dPm, "wgrad_in_mix"))
    g["w_in"] = d_w_in.reshape(D, N_CHIPS, -1).transpose(1, 0, 2)
    for n in ("g_pre_mix", "g_cq", "g_ckv", "pool_scale", "g_sgu_v", "g_post_mix", "g_pre_ffn", "g_post_ffn"):
        g[n] = g[n].reshape(-1)
    return dx, g, (sums, slots)


SMALL = REPLICATED + ("conv_w",)


def _local_step(x, target, gathered, full, core):
    n_seq, S, D = x.shape
    rope = _rope_tables(S)
    xs = x.reshape(n_seq * S, D)
    weights, saved = [], []
    for l in range(DEPTH):
        w = _layer_weights(gathered, full, l, D)
        gather_next = l + 1 < DEPTH
        if gather_next:
            w, gathered = lax.optimization_barrier((w, gathered))
        xs, s, gathered = _layer_fwd(xs, w, gathered, l, S, rope, gather_next)
        weights.append(w)
        saved.append(s)
    loss_parts, dx = _loss_grad(xs, target.reshape(n_seq * S, D))
    grads, bufs = [None] * DEPTH, {}
    for l in reversed(range(1, DEPTH)):
        dx, grads[l], (_, got_ffn) = _layer_bwd(dx, weights[l], gathered, l, saved[l], S, rope, bufs, send_ffn=l == 1)
    done = dict(bufs, w_in=jnp.stack([grads[l]["w_in"] for l in range(1, DEPTH)], axis=1))
    spans = dict({n: (1, DEPTH) for n in GATHERED}, w_in=(0, DEPTH - 1))
    dx, grads[0], early = _layer_bwd(dx, weights[0], gathered, 0, saved[0], S, rope, bufs, (done, spans, core, got_ffn))
    last = dict(bufs, w_in=grads[0]["w_in"][:, None])
    small = {n: jnp.stack([grads[l][n] for l in range(DEPTH)]) for n in SMALL}
    return loss_parts, dx.reshape(n_seq, S, D), early, last, small


def _unpad_reduced(n, r):
    L = r.shape[0]
    if n == "w_uq":
        return r.reshape(L, Q_LORA, 2, HEAD_PAD)[..., :QK_NOPE + QK_ROPE].reshape(L, Q_LORA, -1)
    if n == "w_ukv":
        r = r.reshape(L, KV_LORA, 2, 2, HEAD_PAD)[..., :QK_NOPE]
        return jnp.concatenate([r[:, :, 0], r[:, :, 1]], axis=-1).reshape(L, KV_LORA, -1)
    if n == "w_br_a":
        return r.reshape(L, HEADS, HEAD_PAD, -1)[:, :, :V_HEAD].reshape(L, HEADS * V_HEAD, -1)
    return r


def _small_rows(n):
    return -(-n // (8 * 128)) * 8


def _to_small(parts):
    flat = jnp.concatenate([p.reshape(-1) for p in parts])
    rows = _small_rows(flat.shape[0])
    return jnp.pad(flat, (0, rows * 128 - flat.shape[0])).reshape(rows, 128)


def kernel(x, w_in, g_pre_mix, g_cq, g_ckv, w_uq, w_ukv, pool_w, pool_scale, g_sgu_v, sgu_w, sgu_b, conv_w, w_br_a, w_br_b, w_br_c, w_br_d, w_out, g_post_mix, g_pre_ffn, w_ffn_gate, w_ffn_up, w_ffn_down, g_post_ffn, loss_target, m_w_in, m_g_pre_mix, m_g_cq, m_g_ckv, m_w_uq, m_w_ukv, m_pool_w, m_pool_scale, m_g_sgu_v, m_sgu_w, m_sgu_b, m_conv_w, m_w_br_a, m_w_br_b, m_w_br_c, m_w_br_d, m_w_out, m_g_post_mix, m_g_pre_ffn, m_w_ffn_gate, m_w_ffn_up, m_w_ffn_down, m_g_post_ffn, v_w_in, v_g_pre_mix, v_g_cq, v_g_ckv, v_w_uq, v_w_ukv, v_pool_w, v_pool_scale, v_g_sgu_v, v_sgu_w, v_sgu_b, v_conv_w, v_w_br_a, v_w_br_b, v_w_br_c, v_w_br_d, v_w_out, v_g_post_mix, v_g_pre_ffn, v_w_ffn_gate, v_w_ffn_up, v_w_ffn_down, v_g_post_ffn):
    local = dict(locals())
    W = {n: local[n] for n in WEIGHTS}
    M = {n: local["m_" + n] for n in WEIGHTS}
    V = {n: local["v_" + n] for n in WEIGHTS}
    chip = 2 * lax.axis_index("x") + lax.axis_index("y")
    core = lax.axis_index("c")

    gathered = dict(zip(GATHERED, _gather_weights([_place_own(W[n], chip, n) for n in GATHERED], 0)))
    conv_shape = conv_w.shape
    conv_cols = conv_shape[-1]
    conv_full_shape = conv_shape[:-1] + (N_CHIPS * conv_cols,)
    placed = lax.dynamic_update_slice(jnp.zeros(conv_full_shape, F32), conv_w, (0, 0, 0, chip * conv_cols))
    n_conv = int(np.prod(conv_full_shape))
    conv_sum = _all_reduce_small(_to_small([placed]), "gather_conv_w")
    full = {n: W[n] for n in REPLICATED}
    full["conv_w"] = 0.5 * conv_sum.reshape(-1)[:n_conv].reshape(conv_full_shape)

    loss_parts, grad_x, (sums_up, slots_up), last, small = _local_step(x, loss_target, gathered, full, core)
    loss = lax.psum(jnp.sum(loss_parts), ("x", "y", "c"))

    small_sum = _all_reduce_small(_to_small([small[n] for n in SMALL]), "reduce_small_grads")
    small_grads = dict(zip(SMALL, _unpack(small_sum, [small[n].shape for n in SMALL])))
    small_grads["conv_w"] = lax.dynamic_slice(small_grads["conv_w"], (0, 0, 0, chip * conv_cols), conv_shape)

    first = [(0, 1)] * len(GATHERED)
    Gs = [last[n] for n in GATHERED]
    got = _run_ride(_exchange_ride(Gs, first), "rs_exchange_halves")
    sums_0 = [_add_halves(g, (0, 1), r, core, n) for n, g, r in zip(GATHERED, Gs, got)]
    slots_0 = _run_ride(_scatter_ride(sums_0), "rs_scatter_partials")
    place = [lax.axis_index(a).reshape(1).astype(jnp.int32) for a in ("x", "y", "c")]
    halves = []
    for n, h, s in zip(GATHERED, sums_0, slots_0):
        upper = _sum_slots(sums_up[n], slots_up[n], place, n, 1, DEPTH)
        halves.append(_sum_slots(h, s, place, n, 0, DEPTH, prev=upper))
    shard_grads = {n: _unpad_reduced(n, r).reshape(W[n].shape) for n, r in zip(GATHERED, _join_halves(halves))}

    out_g, out_d, out_m, out_v = {}, {}, {}, {}
    for n in GATHERED:
        shp = W[n].shape
        flat = lambda a: a.reshape(-1, shp[-1])
        d, m2, v2 = _adamw(flat(W[n]), flat(shard_grads[n]), flat(M[n]), flat(V[n]), "adamw_" + n)
        out_g[n], out_d[n], out_m[n], out_v[n] = shard_grads[n], d.reshape(shp), m2.reshape(shp), v2.reshape(shp)
    rest_shapes = [W[n].shape for n in SMALL]
    d, m2, v2 = _adamw(_to_small([W[n] for n in SMALL]), _to_small([small_grads[n] for n in SMALL]),
                       _to_small([M[n] for n in SMALL]), _to_small([V[n] for n in SMALL]), "adamw_small")
    for n, dd, mm, vv in zip(SMALL, _unpack(d, rest_shapes), _unpack(m2, rest_shapes), _unpack(v2, rest_shapes)):
        out_g[n], out_d[n], out_m[n], out_v[n] = small_grads[n], dd, mm, vv

    return (loss, grad_x, *[out_g[n] for n in WEIGHTS], *[out_d[n] for n in WEIGHTS], *[out_m[n] for n in WEIGHTS],
            *[out_v[n] for n in WEIGHTS])
```

```python
import functools

import numpy as np
import jax
import jax.numpy as jnp
from jax import lax
from jax.experimental import pallas as pl
from jax.experimental.pallas import tpu as pltpu

F32 = jnp.float32
BF16 = jnp.bfloat16

EPS = 1e-6
NEG_INF = -1e30
DEPTH = 4
HEADS = 8
QK_NOPE = 64
QK_ROPE = 32
V_HEAD = 64
HEAD_PAD = 128
Q_LORA = 256
KV_LORA = 128
ROPE_THETA = 10000.0
POOL_WINDOWS = (2, 4, 8, 16)
GROUPS = 4
GROUP_DIM = 64
MIX_W = GROUPS * GROUP_DIM
POOL_HALO = 16
CONV_HALO = 16
SGU_BLOCK = 128
CHUNK = 64
CHUNK_SHIFT = 6
ACT_ROWS = 16
NORM_ROWS = 16
ROW_PARTS = 2
SOFTMAX_ROWS = 32
GROUP_SHIFT = 6
N_BRANCH = 4
MLA_W = HEADS * HEAD_PAD
N_CHIPS = 4
CHIP_HEADS_W = MLA_W // N_CHIPS
CHIP_KV = 2 * CHIP_HEADS_W
ATTN_SCALE = (QK_NOPE + QK_ROPE) ** -0.5
LOG2E = 1.4426950408889634
LN2 = 0.6931471805599453
SEC_MLA = Q_LORA + KV_LORA + HEAD_PAD
SEC_MIX = 6 * MIX_W

ADAM_LR = 0.001
ADAM_B1 = 0.9
ADAM_B2 = 0.999
ADAM_EPS = 1e-08
ADAM_WD = 0.01
ADAM_STEP = 10

VMEM_LIMIT = 56 * 1024 * 1024
MESH = pl.DeviceIdType.MESH


def _tile(n, pref, mult=8):
    t = min(n, pref)
    while t > 0:
        if n % t == 0 and t % mult == 0:
            return t
        t -= 1
    return n


def _params(n_axes):
    return pltpu.CompilerParams(dimension_semantics=("arbitrary",) * n_axes, vmem_limit_bytes=VMEM_LIMIT)


def _dot(a, b):
    return jnp.dot(a, b, preferred_element_type=F32)


def _dot_nt(a, b):
    return lax.dot_general(a, b, (((1,), (1,)), ((), ())), preferred_element_type=F32)


def _dot_tn(a, b):
    return lax.dot_general(a, b, (((0,), (0,)), ((), ())), preferred_element_type=F32)


def _rms_r(x):
    return lax.rsqrt(jnp.mean(x * x, axis=-1, keepdims=True) + EPS)


def _rms_bwd(dy, x, g):
    r = _rms_r(x)
    u = dy * g
    dx = r * u - x * (r * r * r * jnp.mean(u * x, axis=-1, keepdims=True))
    dg = jnp.sum(dy * x * r, axis=0, keepdims=True)
    return dx, dg


def _sigmoid(x):
    return 1.0 / (1.0 + jnp.exp(-x))


def _shift_down(a, k):
    return pltpu.roll(a, k, 0)


def _shift_up(a, k):
    return pltpu.roll(a, a.shape[0] - k, 0)


def _rope(x, c, sa, sb):
    w = x.shape[-1]
    return x * c + pltpu.roll(x, QK_ROPE // 2, 1) * sa + pltpu.roll(x, w - QK_ROPE // 2, 1) * sb


def _rope_t(d, c, sa, sb):
    w = d.shape[-1]
    return d * c + pltpu.roll(d * sa, w - QK_ROPE // 2, 1) + pltpu.roll(d * sb, QK_ROPE // 2, 1)


def _full(shape):
    return pl.BlockSpec(shape, lambda *_: (0,) * len(shape))


def _gather_ride(bufs, l, stage):
    def copies(_, o_refs, send_sems, recv_sems):
        x, y, c, me = _place()
        sends, arrivals = [], []
        for t, o in enumerate(o_refs):
            hr = o.shape[2] // 2
            for d in (1, 2, 3):
                sems = (send_sems.at[3 * t + d - 1], recv_sems.at[3 * t + d - 1])
                mine = o.at[l, me, pl.ds(c * hr, hr), :]
                theirs = o.at[l, me ^ d, pl.ds(c * hr, hr), :]
                other_half = o.at[l, me ^ d, pl.ds((1 - c) * hr, hr), :]
                if stage == "chips":
                    sends.append(_remote(mine, mine, *sems, _chip_device(me ^ d, c)))
                    arrivals.append(_remote(theirs, theirs, *sems, _chip_device(me ^ d, c)))
                else:
                    sends.append(_remote(theirs, theirs, *sems, (x, y, 1 - c)))
                    arrivals.append(_remote(other_half, other_half, *sems, (x, y, 1 - c)))
        return sends, arrivals

    shapes = [jax.ShapeDtypeStruct(b.shape, b.dtype) for b in bufs]
    return dict(ins=list(bufs), outs=shapes, alias=True, copies=copies, n_sems=3 * len(bufs))


def _exchange_ride(Gs, spans):
    def copies(g_refs, o_refs, send_sems, recv_sems):
        x, y, c, _ = _place()
        cps = []
        for t, (g, o) in enumerate(zip(g_refs, o_refs)):
            hr = g.shape[2] // 2
            l0, l1 = spans[t]
            cps.append(_remote(g.at[:, pl.ds(l0, l1 - l0), pl.ds((1 - c) * hr, hr), :], o, send_sems.at[t], recv_sems.at[t],
                               (x, y, 1 - c)))
        return cps, cps

    shapes = [jax.ShapeDtypeStruct((g.shape[0], l1 - l0, g.shape[2] // 2, g.shape[3]), g.dtype) for g, (l0, l1) in zip(Gs, spans)]
    return dict(ins=list(Gs), outs=shapes, alias=False, copies=copies, n_sems=len(Gs))


def _scatter_ride(Hs):
    def copies(h_refs, o_refs, send_sems, recv_sems):
        x, y, c, me = _place()
        sends, arrivals = [], []
        for t, (h, o) in enumerate(zip(h_refs, o_refs)):
            for d in (1, 2, 3):
                sems = (send_sems.at[3 * t + d - 1], recv_sems.at[3 * t + d - 1])
                sends.append(_remote(h.at[me ^ d], o.at[me], *sems, _chip_device(me ^ d, c)))
                arrivals.append(_remote(h.at[me ^ d], o.at[me ^ d], *sems, _chip_device(me ^ d, c)))
        return sends, arrivals

    shapes = [jax.ShapeDtypeStruct(h.shape, h.dtype) for h in Hs]
    return dict(ins=list(Hs), outs=shapes, alias=False, copies=copies, n_sems=3 * len(Hs))


def _ride_specs(ride, n_in, n_out):
    if not ride:
        return [], [], [], [], [], {}
    anywhere = pl.BlockSpec(memory_space=pl.ANY)
    sems = pltpu.SemaphoreType.DMA((ride["n_sems"],))
    alias = {n_in + i: n_out + i for i in range(len(ride["ins"]))} if ride["alias"] else {}
    return [anywhere] * len(ride["ins"]), [anywhere] * len(ride["outs"]), list(ride["outs"]), [sems, sems], ride["ins"], alias


def _ride_split(ride, rest, n_out, n_scratch):
    a = len(ride["ins"]) if ride else 0
    b = a + n_out
    c = b + (len(ride["outs"]) if ride else 0)
    d = c + n_scratch
    riding = (ride, rest[:a], rest[b:c], rest[d:]) if ride else None
    return rest[a:b], rest[c:d], riding


def _ride_start(riding, first):
    if riding:
        ride, in_refs, out_refs, (send_sems, recv_sems) = riding

        @pl.when(first)
        def _():
            for cp in ride["copies"](in_refs, out_refs, send_sems, recv_sems)[0]:
                cp.start()


def _ride_finish(riding, last):
    if riding:
        ride, in_refs, out_refs, (send_sems, recv_sems) = riding

        @pl.when(last)
        def _():
            sends, arrivals = ride["copies"](in_refs, out_refs, send_sems, recv_sems)
            for cp in arrivals:
                cp.wait_recv()
            for cp in sends:
                cp.wait_send()


def _run_ride(ride, name):
    def body(*refs):
        _, _, (_, in_refs, out_refs, (send_sems, recv_sems)) = _ride_split(ride, refs, 0, 0)
        sends, arrivals = ride["copies"](in_refs, out_refs, send_sems, recv_sems)
        for cp in sends:
            cp.start()
        for cp in arrivals:
            cp.wait_recv()
        for cp in sends:
            cp.wait_send()

    in_specs, out_specs, out_shapes, scratch, operands, alias = _ride_specs(ride, 0, 0)
    return list(pl.pallas_call(body, name=name, in_specs=in_specs, out_specs=out_specs, out_shape=out_shapes,
                               input_output_aliases=alias, scratch_shapes=scratch)(*operands))


def _norm_matmul(x, g, w, name):
    T, K = x.shape
    N = w.shape[1]
    tm, tn = _tile(T, 2048), _tile(N, 1536, 128)

    def body(x_ref, g_ref, w_ref, o_ref, h_ref):
        @pl.when(pl.program_id(1) == 0)
        def _():
            xv = x_ref[...]
            h_ref[...] = (xv * _rms_r(xv) * g_ref[...]).astype(BF16)

        o_ref[...] = _dot(h_ref[...], w_ref[...]).astype(BF16)

    return pl.pallas_call(
        body, name=name, grid=(T // tm, N // tn),
        in_specs=[pl.BlockSpec((tm, K), lambda i, j: (i, 0)), _full((1, K)), pl.BlockSpec((K, tn), lambda i, j: (0, j))],
        out_specs=[pl.BlockSpec((tm, tn), lambda i, j: (i, j)), pl.BlockSpec((tm, K), lambda i, j: (i, 0))],
        out_shape=[jax.ShapeDtypeStruct((T, N), BF16), jax.ShapeDtypeStruct((T, K), BF16)],
        compiler_params=_params(2),
    )(x, g, w)


def _mla_prep(P, D, g_cq, g_ckv, wuq, wukv, rope_c, rope_sa, rope_sb, S):
    T = P.shape[0]
    tm = _tile(S, 512)
    n_si = S // tm
    base = 4 * D

    def body(cq_ref, ckv_ref, kr_ref, gq_ref, gkv_ref, wq_ref, wkv_ref, c_ref, sa_ref, sb_ref,
             q_ref, k_ref, v_ref, hq_ref, hkv_ref):
        c, sa, sb = c_ref[...], sa_ref[...], sb_ref[...]
        cq = cq_ref[...].astype(F32)
        hq = (cq * _rms_r(cq) * gq_ref[...]).astype(BF16)
        hq_ref[...] = hq
        q = _dot(hq, wq_ref[...])
        q = _rope(q, jnp.tile(c, (1, HEADS)), jnp.tile(sa, (1, HEADS)), jnp.tile(sb, (1, HEADS)))
        q_ref[...] = (q * (ATTN_SCALE * LOG2E)).astype(BF16)
        ckv = ckv_ref[...].astype(F32)
        hkv = (ckv * _rms_r(ckv) * gkv_ref[...]).astype(BF16)
        hkv_ref[...] = hkv
        kv = _dot(hkv, wkv_ref[...])
        kr = _rope(kr_ref[...].astype(F32), c, sa, sb)
        k_nope = jnp.concatenate([kv[:, j * CHIP_KV:j * CHIP_KV + CHIP_HEADS_W] for j in range(N_CHIPS)], axis=1)
        k_ref[...] = (k_nope + jnp.tile(kr, (1, HEADS))).astype(BF16)
        v = jnp.concatenate([kv[:, j * CHIP_KV + CHIP_HEADS_W:(j + 1) * CHIP_KV] for j in range(N_CHIPS)], axis=1)
        ones_lane = (lax.broadcasted_iota(jnp.int32, (1, MLA_W), 1) & (HEAD_PAD - 1)) == V_HEAD
        v_ref[...] = jnp.where(ones_lane, 1.0, v).astype(BF16)

    tab = pl.BlockSpec((tm, HEAD_PAD), lambda i: (i % n_si, 0))
    row = lambda w: pl.BlockSpec((tm, w), lambda i: (i, 0))
    return pl.pallas_call(
        body, name="mla_prep", grid=(T // tm,),
        in_specs=[pl.BlockSpec((tm, Q_LORA), lambda i: (i, base // Q_LORA)),
                  pl.BlockSpec((tm, KV_LORA), lambda i: (i, (base + Q_LORA) // KV_LORA)),
                  pl.BlockSpec((tm, HEAD_PAD), lambda i: (i, (base + Q_LORA + KV_LORA) // HEAD_PAD)),
                  _full((1, Q_LORA)), _full((1, KV_LORA)), _full((Q_LORA, MLA_W)), _full((KV_LORA, 2 * MLA_W)),
                  tab, tab, tab],
        out_specs=[row(MLA_W), row(MLA_W), row(MLA_W), row(Q_LORA), row(KV_LORA)],
        out_shape=[jax.ShapeDtypeStruct((T, MLA_W), BF16)] * 3
        + [jax.ShapeDtypeStruct((T, Q_LORA), BF16), jax.ShapeDtypeStruct((T, KV_LORA), BF16)],
        compiler_params=_params(1),
    )(P, P, P, g_cq, g_ckv, wuq, wukv, rope_c, rope_sa, rope_sb)


def _chunk_mask(tq, tk):
    row = lax.broadcasted_iota(jnp.int32, (tq, tk), 0)
    col = lax.broadcasted_iota(jnp.int32, (tq, tk), 1)
    return (row >> CHUNK_SHIFT) >= (col >> CHUNK_SHIFT)


def _rows_to_lanes(stats, out_ref):
    t = stats.T
    for h in range(HEADS):
        out_ref[h, 0] = t[h:h + 1, :]


def _flash_fwd(Q, K, V, S, ride=None):
    T = Q.shape[0]
    n_seq = T // S
    tq = _tile(S, 256, 128)
    nq = S // tq

    def body(q_ref, k_ref, v_ref, *rest):
        (o_ref, lse_ref), (m_s, acc_s, s_s, p_s, a_s), riding = _ride_split(ride, rest, 2, 5)
        _ride_start(riding, (pl.program_id(0) == 0) & (pl.program_id(1) == 0))
        qi = pl.program_id(1)
        m_s[...] = jnp.full(m_s.shape, NEG_INF, F32)
        acc_s[...] = jnp.zeros_like(acc_s)

        def block(kb, masked):
            rows = pl.ds(pl.multiple_of(kb * tq, tq), tq)
            for h in range(HEADS):
                hs = slice(h * HEAD_PAD, (h + 1) * HEAD_PAD)
                s_s[h] = _dot_nt(q_ref[:, hs], k_ref[rows, hs])
            def softmax_head(h):
                for r in range(0, tq, SOFTMAX_ROWS):
                    rs = slice(r, r + SOFTMAX_ROWS)
                    s = s_s[h, rs, :]
                    if masked:
                        row = r + lax.broadcasted_iota(jnp.int32, (SOFTMAX_ROWS, tq), 0)
                        col = lax.broadcasted_iota(jnp.int32, (SOFTMAX_ROWS, tq), 1)
                        s = jnp.where((row >> CHUNK_SHIFT) >= (col >> CHUNK_SHIFT), s, NEG_INF)
                    m_old = m_s[h, rs]
                    m_new = jnp.maximum(m_old, jnp.max(s, axis=-1, keepdims=True))
                    m_s[h, rs] = m_new
                    a_s[h, rs] = jnp.exp2(m_old - m_new)
                    for half in range(tq // HEAD_PAD):
                        cs = slice(half * HEAD_PAD, (half + 1) * HEAD_PAD)
                        p_s[h, rs, cs] = jnp.exp2(s[:, cs] - m_new).astype(BF16)

            for h in range(HEADS):
                softmax_head(h)
            for h in range(HEADS):
                hs = slice(h * HEAD_PAD, (h + 1) * HEAD_PAD)
                acc_s[:, hs] = a_s[h] * acc_s[:, hs] + _dot(p_s[h], v_ref[rows, hs])

        def full_block(kb, carry):
            block(kb, False)
            return carry

        lax.fori_loop(0, qi, full_block, 0)
        block(qi, True)
        lane = lax.broadcasted_iota(jnp.int32, (tq, HEAD_PAD), 1)
        lse_all = jnp.zeros((tq, HEAD_PAD), F32)
        for h in range(HEADS):
            hs = slice(h * HEAD_PAD, (h + 1) * HEAD_PAD)
            acc = acc_s[:, hs]
            l = jnp.sum(jnp.where(lane == V_HEAD, acc, 0.0), axis=-1, keepdims=True)
            o_ref[:, hs] = (acc / l).astype(BF16)
            lse_all = jnp.where(lane == h, m_s[h] + jnp.log2(l), lse_all)
        _rows_to_lanes(lse_all, lse_ref)
        _ride_finish(riding, (pl.program_id(0) == n_seq - 1) & (pl.program_id(1) == nq - 1))

    ride_in, ride_out, ride_shapes, ride_scratch, ride_args, alias = _ride_specs(ride, 3, 2)
    out = pl.pallas_call(
        body, name="flash_fwd", grid=(n_seq, nq),
        in_specs=[pl.BlockSpec((tq, MLA_W), lambda b, i: (b * nq + i, 0)),
                  pl.BlockSpec((S, MLA_W), lambda b, i: (b, 0)), pl.BlockSpec((S, MLA_W), lambda b, i: (b, 0))] + ride_in,
        out_specs=[pl.BlockSpec((tq, MLA_W), lambda b, i: (b * nq + i, 0)),
                   pl.BlockSpec((HEADS, 1, 1, tq), lambda b, i: (0, b * nq + i, 0, 0))] + ride_out,
        out_shape=[jax.ShapeDtypeStruct((T, MLA_W), BF16), jax.ShapeDtypeStruct((HEADS, T // tq, 1, tq), F32)] + ride_shapes,
        input_output_aliases=alias,
        scratch_shapes=[pltpu.VMEM((HEADS, tq, HEAD_PAD), F32), pltpu.VMEM((tq, MLA_W), F32), pltpu.VMEM((HEADS, tq, tq), F32),
                        pltpu.VMEM((HEADS, tq, tq), BF16), pltpu.VMEM((HEADS, tq, HEAD_PAD), F32)] + ride_scratch,
        compiler_params=_params(2),
    )(Q, K, V, *ride_args)
    return out[0], out[1], list(out[2:])


def _lane_group():
    return lax.broadcasted_iota(jnp.int32, (1, MIX_W), 1) >> GROUP_SHIFT


def _by_group(a0, a1, a2, a3):
    g = _lane_group()
    return jnp.where(g == 0, a0, jnp.where(g == 1, a1, jnp.where(g == 2, a2, a3)))


def _pool_count(si, tc, rows):
    pos = si * tc + lax.broadcasted_iota(jnp.int32, (rows, MIX_W), 0)
    win = _by_group(*POOL_WINDOWS)
    return jnp.minimum(pos + 1, win).astype(F32)


def _pool_fwd(z, z_prev, si, tc):
    ze = jnp.concatenate([z_prev, z], axis=0)
    s1 = ze + _shift_down(ze, 1)
    s2 = s1 + _shift_down(s1, 2)
    s4 = s2 + _shift_down(s2, 4)
    s8 = s4 + _shift_down(s4, 8)
    win_sum = _by_group(s1, s2, s4, s8)[POOL_HALO:]
    return win_sum / _pool_count(si, tc, tc) - z


def _sgu_weights(w_ref):
    row = lax.broadcasted_iota(jnp.int32, (SGU_BLOCK, SGU_BLOCK), 0)
    col = lax.broadcasted_iota(jnp.int32, (SGU_BLOCK, SGU_BLOCK), 1)
    keep = (row >> CHUNK_SHIFT) >= (col >> CHUNK_SHIFT)
    return keep, [jnp.where(keep, w_ref[g], 0.0).astype(BF16) for g in range(GROUPS)]


def _sgu_mix(vn_blk, wm, bias):
    g = _lane_group()
    mixed = bias
    for k in range(GROUPS):
        mixed = mixed + jnp.where(g == k, _dot(wm[k], vn_blk), 0.0)
    return mixed


def _conv_fwd(z, z_prev, w_ref):
    ze = jnp.concatenate([z_prev, z], axis=0)
    y = w_ref[0:1, :] * _shift_down(ze, 2) + w_ref[1:2, :] * _shift_down(ze, 1) + w_ref[2:3, :] * ze
    return y[CONV_HALO:]


def _mix_specs(T, D, tc):
    base = (4 * D + SEC_MLA) // MIX_W
    cur = lambda k: pl.BlockSpec((tc, MIX_W), lambda i: (i, base + k))
    prev = lambda k, halo: pl.BlockSpec((halo, MIX_W), lambda i: (jnp.maximum(i * (tc // halo) - 1, 0), base + k))
    nxt = lambda k, halo: pl.BlockSpec((halo, MIX_W), lambda i: (jnp.minimum((i + 1) * (tc // halo), T // halo - 1), base + k))
    return cur, prev, nxt


def _mixers_fwd(P, D, S, pool_bd, pool_scale, g_v, sgu_w, sgu_bias, conv_w, ride=None):
    T = P.shape[0]
    tc = _tile(S, 512, SGU_BLOCK)
    n_si = S // tc
    cur, prev, _ = _mix_specs(T, D, tc)

    def body(z_ref, zp_ref, u_ref, v_ref, b_ref, c_ref, x_ref, cp_ref, xp_ref,
             pw_ref, ps_ref, gv_ref, sw_ref, sb_ref, cw_ref, *rest):
        (ob_ref, oc_ref, od_ref), _, riding = _ride_split(ride, rest, 3, 0)
        _ride_start(riding, pl.program_id(0) == 0)
        si = pl.program_id(0) % n_si
        first = si == 0
        z = z_ref[...].astype(F32)
        pooled = _pool_fwd(z, jnp.where(first, 0.0, zp_ref[...].astype(F32)), si, tc)
        ob_ref[...] = (_dot(pooled.astype(BF16), pw_ref[...]) * ps_ref[...]).astype(BF16)

        v = v_ref[...].astype(F32)
        vn = (v * _rms_r(v) * gv_ref[...]).astype(BF16)
        _, wm = _sgu_weights(sw_ref)
        for blk in range(tc // SGU_BLOCK):
            rows = slice(blk * SGU_BLOCK, (blk + 1) * SGU_BLOCK)
            oc_ref[rows, :] = (u_ref[rows, :].astype(F32) * _sgu_mix(vn[rows], wm, sb_ref[...])).astype(BF16)

        zc = c_ref[...].astype(F32) * x_ref[...].astype(F32)
        zc_prev = jnp.where(first, 0.0, cp_ref[...].astype(F32) * xp_ref[...].astype(F32))
        od_ref[...] = (b_ref[...].astype(F32) * _conv_fwd(zc, zc_prev, cw_ref)).astype(BF16)
        _ride_finish(riding, pl.program_id(0) == T // tc - 1)

    out = pl.BlockSpec((tc, MIX_W), lambda i: (i, 0))
    ride_in, ride_out, ride_shapes, ride_scratch, ride_args, alias = _ride_specs(ride, 15, 3)
    res = pl.pallas_call(
        body, name="mixers_fwd", grid=(T // tc,),
        in_specs=[cur(0), prev(0, POOL_HALO), cur(1), cur(2), cur(3), cur(4), cur(5), prev(4, CONV_HALO), prev(5, CONV_HALO),
                  _full((MIX_W, MIX_W)), _full((1, MIX_W)), _full((1, MIX_W)), _full((GROUPS, SGU_BLOCK, SGU_BLOCK)),
                  _full((SGU_BLOCK, MIX_W)), _full((CONV_HALO, MIX_W))] + ride_in,
        out_specs=[out, out, out] + ride_out,
        out_shape=[jax.ShapeDtypeStruct((T, MIX_W), BF16)] * 3 + ride_shapes,
        input_output_aliases=alias,
        scratch_shapes=ride_scratch,
        compiler_params=_params(1),
    )(P, P, P, P, P, P, P, P, P, pool_bd, pool_scale, g_v, sgu_w, sgu_bias, conv_w, *ride_args)
    return res[0], res[1], res[2], list(res[3:])


def _merge_fwd(x, P, A, Bm, C, Dv, wa, wb, wc, wd, wout, g_post):
    T, D = x.shape
    tm = _tile(T, 512)

    def body(x_ref, lg_ref, a_ref, b_ref, c_ref, d_ref, wa_ref, wb_ref, wc_ref, wd_ref, wo_ref, g_ref,
             x1_ref, mg_ref, o_ref):
        merged = jnp.zeros((tm, D), F32)
        for k, (br, w) in enumerate(((a_ref, wa_ref), (b_ref, wb_ref), (c_ref, wc_ref), (d_ref, wd_ref))):
            merged = merged + _sigmoid(lg_ref[:, k * D:(k + 1) * D].astype(F32)) * _dot(br[...], w[...])
        mg = merged.astype(BF16)
        mg_ref[...] = mg
        o = _dot(mg, wo_ref[...])
        o_ref[...] = o
        x1_ref[...] = x_ref[...] + o * _rms_r(o) * g_ref[...]

    row = lambda w: pl.BlockSpec((tm, w), lambda i: (i, 0))
    return pl.pallas_call(
        body, name="merge_fwd", grid=(T // tm,),
        in_specs=[row(D), row(4 * D), row(MLA_W), row(MIX_W), row(MIX_W), row(MIX_W),
                  _full((MLA_W, D)), _full((MIX_W, D)), _full((MIX_W, D)), _full((MIX_W, D)), _full((D, D)), _full((1, D))],
        out_specs=[row(D), row(D), row(D)],
        out_shape=[jax.ShapeDtypeStruct((T, D), F32), jax.ShapeDtypeStruct((T, D), BF16), jax.ShapeDtypeStruct((T, D), F32)],
        compiler_params=_params(1),
    )(x, P, A, Bm, C, Dv, wa, wb, wc, wd, wout, g_post)


def _ffn_specs(T, D, Fc, l, rows=512):
    tm = _tile(T, rows)
    row = pl.BlockSpec((tm, D), lambda i, j: (i, 0))
    col = pl.BlockSpec((None, tm, Fc), lambda i, j: (j, i, 0))
    w_in = pl.BlockSpec((None, None, D, Fc), lambda i, j: (l, j, 0, 0))
    w_out = pl.BlockSpec((None, None, Fc, D), lambda i, j: (l, j, 0, 0))
    return tm, row, col, w_in, w_out


def _ffn_fwd(x1, g_pre, wg, wu, wdn, g_post, l):
    T, D = x1.shape
    nf, Fc = wg.shape[1], wg.shape[3]
    tm, row, col, w_in, w_out = _ffn_specs(T, D, Fc, l, 1024)

    def body(x_ref, gp_ref, wg_ref, wu_ref, wd_ref, gq_ref, x2_ref, h_ref, gt_ref, up_ref, f_ref, gt_s, up_s, a_s):
        j = pl.program_id(1)

        @pl.when(j == 0)
        def _():
            for r in range(0, tm, NORM_ROWS):
                rs = slice(r, r + NORM_ROWS)
                xv = x_ref[rs, :]
                h_ref[rs, :] = (xv * _rms_r(xv) * gp_ref[...]).astype(BF16)
            f_ref[...] = jnp.zeros_like(f_ref)

        gt_s[...] = _dot(h_ref[...], wg_ref[...])
        up_s[...] = _dot(h_ref[...], wu_ref[...])
        for r in range(0, tm, ACT_ROWS):
            rs = slice(r, r + ACT_ROWS)
            gt, up = gt_s[rs, :], up_s[rs, :]
            gt_ref[rs, :] = gt.astype(BF16)
            up_ref[rs, :] = up.astype(BF16)
            a_s[rs, :] = (gt * _sigmoid(gt) * up).astype(BF16)
        f_ref[...] += _dot(a_s[...], wd_ref[...])

        @pl.when(j == nf - 1)
        def _():
            for r in range(0, tm, NORM_ROWS):
                rs = slice(r, r + NORM_ROWS)
                f = f_ref[rs, :]
                x2_ref[rs, :] = x_ref[rs, :] + f * _rms_r(f) * gq_ref[...]

    return pl.pallas_call(
        body, name="ffn_fwd", grid=(T // tm, nf),
        in_specs=[row, _full((1, D)), w_in, w_in, w_out, _full((1, D))],
        out_specs=[row, row, col, col, row],
        out_shape=[jax.ShapeDtypeStruct((T, D), F32), jax.ShapeDtypeStruct((T, D), BF16),
                   jax.ShapeDtypeStruct((nf, T, Fc), BF16), jax.ShapeDtypeStruct((nf, T, Fc), BF16),
                   jax.ShapeDtypeStruct((T, D), F32)],
        scratch_shapes=[pltpu.VMEM((tm, Fc), F32), pltpu.VMEM((tm, Fc), F32), pltpu.VMEM((tm, Fc), BF16)],
        compiler_params=_params(2),
    )(x1, g_pre, wg, wu, wdn, g_post)


def _loss_grad(y, target):
    T, D = y.shape
    tm = _tile(T, 512)

    def body(y_ref, t_ref, l_ref, dy_ref):
        @pl.when(pl.program_id(0) == 0)
        def _():
            l_ref[...] = jnp.zeros_like(l_ref)

        d = y_ref[...] - t_ref[...]
        dy_ref[...] = d * (1.0 / D)
        e = jnp.sum((d * d).reshape(tm // 8, 8, D), axis=0)
        part = e[:, 0:128]
        for k in range(1, D // 128):
            part = part + e[:, k * 128:(k + 1) * 128]
        l_ref[...] += part * (0.5 / D)

    row = pl.BlockSpec((tm, D), lambda i: (i, 0))
    return pl.pallas_call(
        body, name="loss_grad", grid=(T // tm,),
        in_specs=[row, row], out_specs=[_full((8, 128)), row],
        out_shape=[jax.ShapeDtypeStruct((8, 128), F32), jax.ShapeDtypeStruct((T, D), F32)],
        compiler_params=_params(1),
    )(y, target)


def _matmul_tn(a, b, name):
    T, M = a.shape
    N = b.shape[1]
    tm, tn, tk = _tile(M, 1024, 128), _tile(N, 1536, 128), _tile(T, 1024)

    def body(a_ref, b_ref, o_ref):
        @pl.when(pl.program_id(2) == 0)
        def _():
            o_ref[...] = jnp.zeros_like(o_ref)

        o_ref[...] += _dot_tn(a_ref[...], b_ref[...])

    return pl.pallas_call(
        body, name=name, grid=(M // tm, N // tn, T // tk),
        in_specs=[pl.BlockSpec((tk, tm), lambda i, j, k: (k, i)), pl.BlockSpec((tk, tn), lambda i, j, k: (k, j))],
        out_specs=pl.BlockSpec((tm, tn), lambda i, j, k: (i, j)),
        out_shape=jax.ShapeDtypeStruct((M, N), F32),
        compiler_params=_params(3),
    )(a, b)


def _norm_bwd(dy, x, g, name, add=None, parts=None):
    T, D = x.shape
    tm = _tile(T, 256)
    n_parts = 0 if parts is None else parts.shape[0]

    def body(*refs):
        dx_ref, dg_ref = refs[-2:]
        ins = list(refs[:-2])
        dy_ref = ins.pop(0) if parts is None else None
        p_ref = ins.pop(0) if parts is not None else None
        x_ref, g_ref = ins[0], ins[1]
        add_ref = ins[2] if add is not None else None

        @pl.when(pl.program_id(0) == 0)
        def _():
            dg_ref[...] = jnp.zeros_like(dg_ref)

        dg_sum = jnp.zeros((1, D), F32)
        for r in range(0, tm, NORM_ROWS):
            rs = slice(r, r + NORM_ROWS)
            if parts is None:
                dy = dy_ref[rs, :]
            else:
                dy = p_ref[0, rs, :].astype(F32)
                for k in range(1, n_parts):
                    dy = dy + p_ref[k, rs, :].astype(F32)
            dx, dg = _rms_bwd(dy, x_ref[rs, :], g_ref[...])
            dx_ref[rs, :] = (dx if add is None else add_ref[rs, :] + dx).astype(dx_ref.dtype)
            dg_sum = dg_sum + dg
        dg_ref[...] += dg_sum

    row = pl.BlockSpec((tm, D), lambda i: (i, 0))
    first = [row] if parts is None else [pl.BlockSpec((n_parts, tm, D), lambda i: (0, i, 0))]
    operands = [dy if parts is None else parts, x, g] + ([] if add is None else [add])
    return pl.pallas_call(
        body, name=name, grid=(T // tm,),
        in_specs=first + [row, _full((1, D))] + ([] if add is None else [row]),
        out_specs=[row, _full((1, D))],
        out_shape=[jax.ShapeDtypeStruct((T, D), BF16 if add is None else F32), jax.ShapeDtypeStruct((1, D), F32)],
        compiler_params=_params(1),
    )(*operands)


FFN_WEIGHTS = ("w_ffn_gate", "w_ffn_up", "w_ffn_down")


def _ffn_bwd_main(df, gt, up, h2, wg, wu, wdn, l, bufs, ride=None):
    T, D = h2.shape
    nf, Fc = wg.shape[1], wg.shape[3]
    tm = _tile(T, 512)
    nt = T // tm
    keep = [] if bufs is None else [bufs[n] for n in FFN_WEIGHTS]

    def body(df_ref, gt_ref, up_ref, h_ref, wg_ref, wu_ref, wd_ref, *rest):
        (dh_ref, gg_ref, gu_ref, gd_ref), (da_s, dgt_s, dup_s, act_s), riding = _ride_split(ride, rest[len(keep):], 4, 4)
        j, i = pl.program_id(0), pl.program_id(1)
        _ride_start(riding, (j == 0) & (i == 0))

        @pl.when(i == 0)
        def _():
            for r in (gg_ref, gu_ref, gd_ref):
                r[...] = jnp.zeros_like(r)

        parts = [slice(p * tm // ROW_PARTS, (p + 1) * tm // ROW_PARTS) for p in range(ROW_PARTS)]
        for ps in parts:
            da_s[ps, :] = _dot_nt(df_ref[ps, :], wd_ref[...])
        for ps in parts:
            for r in range(ps.start, ps.stop, ACT_ROWS):
                rs = slice(r, r + ACT_ROWS)
                da = da_s[rs, :]
                gt = gt_ref[rs, :].astype(F32)
                u = up_ref[rs, :].astype(F32)
                sig = _sigmoid(gt)
                silu = gt * sig
                dgt_s[rs, :] = (da * u * (sig * (1.0 + gt * (1.0 - sig)))).astype(BF16)
                dup_s[rs, :] = (da * silu).astype(BF16)
                act_s[rs, :] = (silu * u).astype(BF16)
            dh_ref[ps, :] = (_dot_nt(dgt_s[ps, :], wg_ref[...]) + _dot_nt(dup_s[ps, :], wu_ref[...])).astype(BF16)
        gg_ref[...] += _dot_tn(h_ref[...], dgt_s[...])
        gu_ref[...] += _dot_tn(h_ref[...], dup_s[...])
        gd_ref[...] += _dot_tn(act_s[...], df_ref[...])
        _ride_finish(riding, (j == nf - 1) & (i == nt - 1))

    row = pl.BlockSpec((tm, D), lambda j, i: (i, 0))
    col = pl.BlockSpec((None, tm, Fc), lambda j, i: (j, i, 0))
    w_in = pl.BlockSpec((None, None, D, Fc), lambda j, i: (l, j, 0, 0))
    w_out = pl.BlockSpec((None, None, Fc, D), lambda j, i: (l, j, 0, 0))
    g_in = pl.BlockSpec((None, None, D, Fc), lambda j, i: (j, l, 0, 0))
    g_out = pl.BlockSpec((None, None, Fc, D), lambda j, i: (j, l, 0, 0))
    n_in = 7
    ride_in, ride_out, ride_shapes, ride_scratch, ride_args, alias = _ride_specs(ride, n_in + len(keep), 4)
    out = pl.pallas_call(
        body, name="ffn_bwd_main", grid=(nf, nt),
        in_specs=[row, col, col, row, w_in, w_in, w_out] + [pl.BlockSpec(memory_space=pl.ANY)] * len(keep) + ride_in,
        out_specs=[pl.BlockSpec((None, tm, D), lambda j, i: (j, i, 0)), g_in, g_in, g_out] + ride_out,
        out_shape=[jax.ShapeDtypeStruct((nf, T, D), BF16), jax.ShapeDtypeStruct((nf, DEPTH, D, Fc), F32),
                   jax.ShapeDtypeStruct((nf, DEPTH, D, Fc), F32), jax.ShapeDtypeStruct((nf, DEPTH, Fc, D), F32)] + ride_shapes,
        input_output_aliases={**{n_in + k: 1 + k for k in range(len(keep))}, **alias},
        scratch_shapes=[pltpu.VMEM((tm, Fc), F32), pltpu.VMEM((tm, Fc), BF16), pltpu.VMEM((tm, Fc), BF16),
                        pltpu.VMEM((tm, Fc), BF16)] + ride_scratch,
        compiler_params=_params(2),
    )(df, gt, up, h2, wg, wu, wdn, *keep, *ride_args)
    return out[0], dict(zip(FFN_WEIGHTS, out[1:4])), list(out[4:])


def _wgrad_out(l, n_layers, rows, cols):
    spec = pl.BlockSpec((N_CHIPS, None, rows, cols), lambda *_: (0, l, 0, 0))
    return spec, jax.ShapeDtypeStruct((N_CHIPS, n_layers, rows, cols), F32)


def _merge_bwd(dx1, o, merged, P, A, Bm, C, Dv, wa, wb, wc, wd, wout, g_post, l, bufs, ride=None):
    T, D = o.shape
    tm = _tile(T, 256)
    Dc = D // N_CHIPS
    names = ("w_out", "w_br_a", "w_br_b", "w_br_c", "w_br_d")
    keep = [] if bufs is None else [bufs[n] for n in names]

    def body(dx1_ref, o_ref, mg_ref, lg_ref, a_ref, b_ref, c_ref, d_ref, wa_ref, wb_ref, wc_ref, wd_ref, wo_ref, g_ref, *rest):
        outs, _, riding = _ride_split(ride, rest[len(keep):], 12, 0)
        dlg_ref, da_ref, db_ref, dc_ref, dd_ref, dg_ref, dt_ref, go_ref, ga_ref, gb_ref, gc_ref, gd_ref = outs
        _ride_start(riding, pl.program_id(0) == 0)

        @pl.when(pl.program_id(0) == 0)
        def _():
            for r in (dg_ref, go_ref, ga_ref, gb_ref, gc_ref, gd_ref):
                r[...] = jnp.zeros_like(r)

        d_o, dg = _rms_bwd(dx1_ref[...], o_ref[...], g_ref[...])
        dg_ref[...] += dg
        d_o = d_o.astype(BF16)
        for k in range(N_CHIPS):
            go_ref[k] += _dot_tn(mg_ref[:, k * Dc:(k + 1) * Dc], d_o)
        dm = _dot_nt(d_o, wo_ref[...])
        branches = ((a_ref, wa_ref, da_ref, ga_ref), (b_ref, wb_ref, db_ref, gb_ref),
                    (c_ref, wc_ref, dc_ref, gc_ref), (d_ref, wd_ref, dd_ref, gd_ref))
        for j, (br, w, dbr_ref, gw_ref) in enumerate(branches):
            gate = _sigmoid(lg_ref[:, j * D:(j + 1) * D].astype(F32))
            y = _dot(br[...], w[...])
            dlg_ref[:, j * D:(j + 1) * D] = (dm * y * gate * (1.0 - gate)).astype(BF16)
            dy = (dm * gate).astype(BF16)
            d_in = _dot_nt(dy, w[...]).astype(dbr_ref.dtype)
            dbr_ref[...] = d_in
            if j == 0:
                prod = d_in.astype(F32) * br[...].astype(F32)
                lane = lax.broadcasted_iota(jnp.int32, (tm, HEAD_PAD), 1)
                delta = jnp.zeros((tm, HEAD_PAD), F32)
                for h in range(HEADS):
                    row_sum = jnp.sum(prod[:, h * HEAD_PAD:(h + 1) * HEAD_PAD], axis=-1, keepdims=True)
                    delta = jnp.where(lane == h, row_sum, delta)
                _rows_to_lanes(delta, dt_ref)
            for k in range(N_CHIPS):
                gw_ref[k] += _dot_tn(br[...], dy[:, k * Dc:(k + 1) * Dc])

        _ride_finish(riding, pl.program_id(0) == T // tm - 1)

    row = lambda w: pl.BlockSpec((tm, w), lambda i: (i, 0))
    wg = [_wgrad_out(l, DEPTH, r, c) for r, c in ((Dc, D), (MLA_W, Dc), (MIX_W, Dc), (MIX_W, Dc), (MIX_W, Dc))]
    n_in = 14
    ride_in, ride_out, ride_shapes, ride_scratch, ride_args, alias = _ride_specs(ride, n_in + len(keep), 12)
    stat = pl.BlockSpec((HEADS, 1, 1, tm), lambda i: (0, i, 0, 0))
    out = pl.pallas_call(
        body, name="merge_bwd", grid=(T // tm,),
        in_specs=[row(D), row(D), row(D), row(4 * D), row(MLA_W), row(MIX_W), row(MIX_W), row(MIX_W),
                  _full((MLA_W, D)), _full((MIX_W, D)), _full((MIX_W, D)), _full((MIX_W, D)), _full((D, D)), _full((1, D))]
        + [pl.BlockSpec(memory_space=pl.ANY)] * len(keep) + ride_in,
        out_specs=[row(4 * D), row(MLA_W), row(MIX_W), row(MIX_W), row(MIX_W), _full((1, D)), stat] + [s for s, _ in wg]
        + ride_out,
        out_shape=[jax.ShapeDtypeStruct((T, 4 * D), BF16), jax.ShapeDtypeStruct((T, MLA_W), BF16)]
        + [jax.ShapeDtypeStruct((T, MIX_W), F32)] * 3 + [jax.ShapeDtypeStruct((1, D), F32),
                                                         jax.ShapeDtypeStruct((HEADS, T // tm, 1, tm), F32)]
        + [s for _, s in wg] + ride_shapes,
        input_output_aliases={**{n_in + i: 7 + i for i in range(len(keep))}, **alias},
        scratch_shapes=ride_scratch,
        compiler_params=_params(1),
    )(dx1, o, merged, P, A, Bm, C, Dv, wa, wb, wc, wd, wout, g_post, *keep, *ride_args)
    return out[:7], dict(zip(names, out[7:12])), list(out[12:])


def _mixers_bwd(P, D, S, dBm, dC, dDv, pool_bd, pool_scale, g_v, sgu_w, sgu_bias, conv_w):
    T = P.shape[0]
    tc = _tile(S, 512, SGU_BLOCK)
    n_si = S // tc
    cur, prev, nxt = _mix_specs(T, D, tc)
    n_blk = tc // SGU_BLOCK

    def body(z_ref, zp_ref, u_ref, v_ref, b_ref, c_ref, x_ref, cp_ref, xp_ref, bn_ref,
             dbm_ref, dbmn_ref, dc_ref, ddv_ref, ddvn_ref,
             pw_ref, ps_ref, gv_ref, sw_ref, sb_ref, cw_ref,
             dp_ref, dpw_ref, dps_ref, dgv_ref, dsw_ref, dsb_ref, dcw_ref, dvn_acc):
        si = pl.program_id(0) % n_si
        first, last = si == 0, si == n_si - 1

        @pl.when(pl.program_id(0) == 0)
        def _():
            for r in (dpw_ref, dps_ref, dgv_ref, dsw_ref, dsb_ref, dcw_ref):
                r[...] = jnp.zeros_like(r)

        z = z_ref[...].astype(F32)
        pooled = _pool_fwd(z, jnp.where(first, 0.0, zp_ref[...].astype(F32)), si, tc).astype(BF16)
        dbm = dbm_ref[...]
        dps_ref[...] += jnp.sum(dbm * _dot(pooled, pw_ref[...]), axis=0, keepdims=True)
        dmix = (jnp.concatenate([dbm, jnp.where(last, 0.0, dbmn_ref[...])], axis=0) * ps_ref[...]).astype(BF16)
        dpw_ref[...] += _dot_tn(pooled, dmix[:tc])
        dpool = _dot_nt(dmix, pw_ref[...])
        e = dpool / _pool_count(si, tc, tc + POOL_HALO)
        f1 = e + _shift_up(e, 1)
        f2 = f1 + _shift_up(f1, 2)
        f4 = f2 + _shift_up(f2, 4)
        f8 = f4 + _shift_up(f4, 8)
        dp_ref[:, 0:MIX_W] = (_by_group(f1, f2, f4, f8)[:tc] - dpool[:tc]).astype(BF16)

        v = v_ref[...].astype(F32)
        vn = (v * _rms_r(v) * gv_ref[...]).astype(BF16)
        keep, wm = _sgu_weights(sw_ref)
        g = _lane_group()
        for blk in range(n_blk):
            rows = slice(blk * SGU_BLOCK, (blk + 1) * SGU_BLOCK)
            vb = vn[rows]
            dc = dc_ref[rows, :]
            dp_ref[rows, MIX_W:2 * MIX_W] = (dc * _sgu_mix(vb, wm, sb_ref[...])).astype(BF16)
            dmx = dc * u_ref[rows, :].astype(F32)
            dsb_ref[...] += dmx
            dvn = jnp.zeros((SGU_BLOCK, MIX_W), F32)
            for k in range(GROUPS):
                dmk = jnp.where(g == k, dmx, 0.0).astype(BF16)
                dsw_ref[k] += jnp.where(keep, _dot_nt(dmk, vb), 0.0)
                dvn = dvn + _dot_tn(wm[k], dmk)
            dvn_acc[rows, :] = dvn
        dv, dg = _rms_bwd(dvn_acc[...], v, gv_ref[...])
        dgv_ref[...] += dg
        dp_ref[:, 2 * MIX_W:3 * MIX_W] = dv.astype(BF16)

        cg, xg, bg = c_ref[...].astype(F32), x_ref[...].astype(F32), b_ref[...].astype(F32)
        zc = cg * xg
        ze = jnp.concatenate([jnp.where(first, 0.0, cp_ref[...].astype(F32) * xp_ref[...].astype(F32)), zc], axis=0)
        z1, z2 = _shift_down(ze, 1)[CONV_HALO:], _shift_down(ze, 2)[CONV_HALO:]
        ddv = ddv_ref[...]
        y = cw_ref[0:1, :] * z2 + cw_ref[1:2, :] * z1 + cw_ref[2:3, :] * zc
        dp_ref[:, 3 * MIX_W:4 * MIX_W] = (ddv * y).astype(BF16)
        dy = ddv * bg
        dcw_ref[0:1, :] += jnp.sum(dy * z2, axis=0, keepdims=True)
        dcw_ref[1:2, :] += jnp.sum(dy * z1, axis=0, keepdims=True)
        dcw_ref[2:3, :] += jnp.sum(dy * zc, axis=0, keepdims=True)
        dye = jnp.concatenate([dy, jnp.where(last, 0.0, ddvn_ref[...] * bn_ref[...].astype(F32))], axis=0)
        dz = (cw_ref[2:3, :] * dye + cw_ref[1:2, :] * _shift_up(dye, 1) + cw_ref[0:1, :] * _shift_up(dye, 2))[:tc]
        dp_ref[:, 4 * MIX_W:5 * MIX_W] = (dz * xg).astype(BF16)
        dp_ref[:, 5 * MIX_W:6 * MIX_W] = (dz * cg).astype(BF16)

    grad = lambda halo: pl.BlockSpec((halo, MIX_W), lambda i: (jnp.minimum((i + 1) * (tc // halo), T // halo - 1), 0))
    out = pl.BlockSpec((tc, MIX_W), lambda i: (i, 0))
    return pl.pallas_call(
        body, name="mixers_bwd", grid=(T // tc,),
        in_specs=[cur(0), prev(0, POOL_HALO), cur(1), cur(2), cur(3), cur(4), cur(5), prev(4, CONV_HALO), prev(5, CONV_HALO),
                  nxt(3, CONV_HALO), out, grad(POOL_HALO), out, out, grad(CONV_HALO),
                  _full((MIX_W, MIX_W)), _full((1, MIX_W)), _full((1, MIX_W)), _full((GROUPS, SGU_BLOCK, SGU_BLOCK)),
                  _full((SGU_BLOCK, MIX_W)), _full((CONV_HALO, MIX_W))],
        out_specs=[pl.BlockSpec((tc, SEC_MIX), lambda i: (i, 0)), _full((MIX_W, MIX_W)), _full((1, MIX_W)), _full((1, MIX_W)),
                   _full((GROUPS, SGU_BLOCK, SGU_BLOCK)), _full((SGU_BLOCK, MIX_W)), _full((CONV_HALO, MIX_W))],
        out_shape=[jax.ShapeDtypeStruct((T, SEC_MIX), BF16), jax.ShapeDtypeStruct((MIX_W, MIX_W), F32),
                   jax.ShapeDtypeStruct((1, MIX_W), F32), jax.ShapeDtypeStruct((1, MIX_W), F32),
                   jax.ShapeDtypeStruct((GROUPS, SGU_BLOCK, SGU_BLOCK), F32), jax.ShapeDtypeStruct((SGU_BLOCK, MIX_W), F32),
                   jax.ShapeDtypeStruct((CONV_HALO, MIX_W), F32)],
        scratch_shapes=[pltpu.VMEM((tc, MIX_W), F32)],
        compiler_params=_params(1),
    )(P, P, P, P, P, P, P, P, P, P, dBm, dBm, dC, dDv, dDv, pool_bd, pool_scale, g_v, sgu_w, sgu_bias, conv_w)


def _attn_tile(S):
    return _tile(S, 256, 128)


def _flash_bwd(Q, K, V, dO, lse_t, delta_t, S, ride=None):
    T = Q.shape[0]
    n_seq = T // S
    tq = _attn_tile(S)
    nq = S // tq

    def body(k_ref, v_ref, q_ref, do_ref, lse_ref, dl_ref, *rest):
        (dq_ref, dk_ref, dv_ref), (s_s, dp_s, p_s, ds_s), riding = _ride_split(ride, rest, 3, 4)
        kb = pl.program_id(1)
        _ride_start(riding, (pl.program_id(0) == 0) & (kb == 0))

        @pl.when(kb == 0)
        def _():
            dq_ref[...] = jnp.zeros_like(dq_ref)

        dk_ref[...] = jnp.zeros_like(dk_ref)
        dv_ref[...] = jnp.zeros_like(dv_ref)

        def block(qi, masked):
            rows = pl.ds(pl.multiple_of(qi * tq, tq), tq)
            for h in range(HEADS):
                hs = slice(h * HEAD_PAD, (h + 1) * HEAD_PAD)
                s_s[h] = _dot_nt(k_ref[:, hs], q_ref[rows, hs])
                dp_s[h] = _dot_nt(v_ref[:, hs], do_ref[rows, hs])
            for h in range(HEADS):
                lse_row, dl_row = lse_ref[h, qi], dl_ref[h, qi]
                for r in range(0, tq, SOFTMAX_ROWS):
                    rs = slice(r, r + SOFTMAX_ROWS)
                    s = s_s[h, rs, :]
                    if masked:
                        key = r + lax.broadcasted_iota(jnp.int32, (SOFTMAX_ROWS, tq), 0)
                        query = lax.broadcasted_iota(jnp.int32, (SOFTMAX_ROWS, tq), 1)
                        s = jnp.where((query >> CHUNK_SHIFT) >= (key >> CHUNK_SHIFT), s, NEG_INF)
                    p = jnp.exp2(s - lse_row)
                    p_s[h, rs, :] = p.astype(BF16)
                    ds_s[h, rs, :] = (p * (dp_s[h, rs, :] - dl_row)).astype(BF16)
            for h in range(HEADS):
                hs = slice(h * HEAD_PAD, (h + 1) * HEAD_PAD)
                dv_ref[:, hs] += _dot(p_s[h], do_ref[rows, hs])
                dk_ref[:, hs] += _dot(ds_s[h], q_ref[rows, hs])
                dq_ref[rows, hs] += _dot_tn(ds_s[h], k_ref[:, hs])

        def full_block(qi, carry):
            block(qi, False)
            return carry

        block(kb, True)
        lax.fori_loop(kb + 1, nq, full_block, 0)
        dk_ref[...] = dk_ref[...] * LN2
        _ride_finish(riding, (pl.program_id(0) == n_seq - 1) & (kb == nq - 1))

    tile = pl.BlockSpec((tq, MLA_W), lambda b, i: (b * nq + i, 0))
    seq = pl.BlockSpec((S, MLA_W), lambda b, i: (b, 0))
    stat = pl.BlockSpec((HEADS, nq, 1, tq), lambda b, i: (0, b, 0, 0))
    ride_in, ride_out, ride_shapes, ride_scratch, ride_args, alias = _ride_specs(ride, 6, 3)
    out = pl.pallas_call(
        body, name="flash_bwd", grid=(n_seq, nq),
        in_specs=[tile, tile, seq, seq, stat, stat] + ride_in,
        out_specs=[seq, tile, tile] + ride_out,
        out_shape=[jax.ShapeDtypeStruct((T, MLA_W), F32)] * 3 + ride_shapes,
        input_output_aliases=alias,
        scratch_shapes=[pltpu.VMEM((HEADS, tq, tq), F32), pltpu.VMEM((HEADS, tq, tq), F32),
                        pltpu.VMEM((HEADS, tq, tq), BF16), pltpu.VMEM((HEADS, tq, tq), BF16)] + ride_scratch,
        compiler_params=_params(2),
    )(K, V, Q, dO, lse_t, delta_t, *ride_args)
    return out[0], out[1], out[2], list(out[3:])


def _mla_bwd_post(P, D, S, dQ, dK, dV, hq, hkv, g_cq, g_ckv, wuq, wukv, rope_c, rope_sa, rope_sb, l, bufs):
    T = P.shape[0]
    tm = _tile(S, 512)
    n_si = S // tm
    base = 4 * D
    names = ("w_uq", "w_ukv")

    def body(cq_ref, ckv_ref, dq_ref, dk_ref, dv_ref, hq_ref, hkv_ref, gq_ref, gkv_ref, wq_ref, wkv_ref, c_ref, sa_ref, sb_ref,
             *rest):
        dp_ref, dgq_ref, dgkv_ref, guq_ref, gukv_ref = rest[-5:]

        @pl.when(pl.program_id(0) == 0)
        def _():
            for r in (dgq_ref, dgkv_ref, guq_ref, gukv_ref):
                r[...] = jnp.zeros_like(r)

        c, sa, sb = c_ref[...], sa_ref[...], sb_ref[...]
        dq = _rope_t(dq_ref[...] * ATTN_SCALE, jnp.tile(c, (1, HEADS)), jnp.tile(sa, (1, HEADS)),
                     jnp.tile(sb, (1, HEADS))).astype(BF16)
        dcq, dg = _rms_bwd(_dot_nt(dq, wq_ref[...]), cq_ref[...].astype(F32), gq_ref[...])
        dgq_ref[...] += dg
        dp_ref[:, 0:Q_LORA] = dcq.astype(BF16)

        dk = dk_ref[...]
        dkb, dvb = dk.astype(BF16), dv_ref[...].astype(BF16)
        dkv = jnp.concatenate([p[:, j * CHIP_HEADS_W:(j + 1) * CHIP_HEADS_W] for j in range(N_CHIPS) for p in (dkb, dvb)], axis=1)
        for k in range(N_CHIPS):
            guq_ref[k] += _dot_tn(hq_ref[...], dq[:, k * CHIP_HEADS_W:(k + 1) * CHIP_HEADS_W])
            gukv_ref[k] += _dot_tn(hkv_ref[...], dkv[:, k * CHIP_KV:(k + 1) * CHIP_KV])
        dckv, dg = _rms_bwd(_dot_nt(dkv, wkv_ref[...]), ckv_ref[...].astype(F32), gkv_ref[...])
        dgkv_ref[...] += dg
        dp_ref[:, Q_LORA:Q_LORA + KV_LORA] = dckv.astype(BF16)

        dkr = dk[:, 0:HEAD_PAD]
        for h in range(1, HEADS):
            dkr = dkr + dk[:, h * HEAD_PAD:(h + 1) * HEAD_PAD]
        lane = lax.broadcasted_iota(jnp.int32, (1, HEAD_PAD), 1)
        rope_lanes = (lane >= QK_NOPE) & (lane < QK_NOPE + QK_ROPE)
        dp_ref[:, Q_LORA + KV_LORA:SEC_MLA] = jnp.where(rope_lanes, _rope_t(dkr, c, sa, sb), 0.0).astype(BF16)

    tab = pl.BlockSpec((tm, HEAD_PAD), lambda i: (i % n_si, 0))
    row = lambda w: pl.BlockSpec((tm, w), lambda i: (i, 0))
    wg = [_wgrad_out(l, DEPTH, Q_LORA, CHIP_HEADS_W), _wgrad_out(l, DEPTH, KV_LORA, CHIP_KV)]
    keep = [] if bufs is None else [bufs[n] for n in names]
    n_in = 14
    out = pl.pallas_call(
        body, name="mla_bwd_post", grid=(T // tm,),
        in_specs=[pl.BlockSpec((tm, Q_LORA), lambda i: (i, base // Q_LORA)),
                  pl.BlockSpec((tm, KV_LORA), lambda i: (i, (base + Q_LORA) // KV_LORA)),
                  row(MLA_W), row(MLA_W), row(MLA_W), row(Q_LORA), row(KV_LORA),
                  _full((1, Q_LORA)), _full((1, KV_LORA)), _full((Q_LORA, MLA_W)), _full((KV_LORA, 2 * MLA_W)), tab, tab, tab]
        + [pl.BlockSpec(memory_space=pl.ANY)] * len(keep),
        out_specs=[row(SEC_MLA), _full((1, Q_LORA)), _full((1, KV_LORA))] + [s for s, _ in wg],
        out_shape=[jax.ShapeDtypeStruct((T, SEC_MLA), BF16), jax.ShapeDtypeStruct((1, Q_LORA), F32),
                   jax.ShapeDtypeStruct((1, KV_LORA), F32)] + [s for _, s in wg],
        input_output_aliases={n_in + i: 3 + i for i in range(len(keep))},
        compiler_params=_params(1),
    )(P, P, dQ, dK, dV, hq, hkv, g_cq, g_ckv, wuq, wukv, rope_c, rope_sa, rope_sb, *keep)
    return out[:3], dict(zip(names, out[3:]))


def _proj_bwd(dx1, x, g, dPg, dPa, dPm, w_gates, w_mla, w_mix, ride=None):
    T, D = x.shape
    tm = _tile(T, 512)

    def body(dx1_ref, x_ref, g_ref, dg_ref_in, da_ref, dm_ref, wg_ref, wa_ref, wm_ref, *rest):
        (dx_ref, dg_ref), _, riding = _ride_split(ride, rest, 2, 0)
        _ride_start(riding, pl.program_id(0) == 0)

        @pl.when(pl.program_id(0) == 0)
        def _():
            dg_ref[...] = jnp.zeros_like(dg_ref)

        dh = _dot_nt(dg_ref_in[...], wg_ref[...]) + _dot_nt(da_ref[...], wa_ref[...]) + _dot_nt(dm_ref[...], wm_ref[...])
        dx, dg = _rms_bwd(dh, x_ref[...], g_ref[...])
        dx_ref[...] = dx1_ref[...] + dx
        dg_ref[...] += dg
        _ride_finish(riding, pl.program_id(0) == T // tm - 1)

    row = lambda w: pl.BlockSpec((tm, w), lambda i: (i, 0))
    ride_in, ride_out, ride_shapes, ride_scratch, ride_args, alias = _ride_specs(ride, 9, 2)
    out = pl.pallas_call(
        body, name="proj_bwd", grid=(T // tm,),
        in_specs=[row(D), row(D), _full((1, D)), row(4 * D), row(SEC_MLA), row(SEC_MIX),
                  _full((D, 4 * D)), _full((D, SEC_MLA)), _full((D, SEC_MIX))] + ride_in,
        out_specs=[row(D), _full((1, D))] + ride_out,
        out_shape=[jax.ShapeDtypeStruct((T, D), F32), jax.ShapeDtypeStruct((1, D), F32)] + ride_shapes,
        input_output_aliases=alias, scratch_shapes=ride_scratch,
        compiler_params=_params(1),
    )(dx1, x, g, dPg, dPa, dPm, w_gates, w_mla, w_mix, *ride_args)
    return out[0], out[1], list(out[2:])


def _adamw(w, g, m, v, name):
    R, C = w.shape
    tr = _tile(R, max(8, (1 << 19) // C))

    def body(w_ref, g_ref, m_ref, v_ref, d_ref, mo_ref, vo_ref):
        gv = g_ref[...]
        mn = ADAM_B1 * m_ref[...] + (1.0 - ADAM_B1) * gv
        vn = ADAM_B2 * v_ref[...] + (1.0 - ADAM_B2) * (gv * gv)
        mo_ref[...] = mn
        vo_ref[...] = vn
        m_hat = mn / (1.0 - ADAM_B1 ** ADAM_STEP)
        v_hat = vn / (1.0 - ADAM_B2 ** ADAM_STEP)
        d_ref[...] = -ADAM_LR * (m_hat / (jnp.sqrt(v_hat) + ADAM_EPS) + ADAM_WD * w_ref[...])

    blk = pl.BlockSpec((tr, C), lambda i: (i, 0))
    return pl.pallas_call(
        body, name=name, grid=(R // tr,), in_specs=[blk] * 4, out_specs=[blk] * 3,
        out_shape=[jax.ShapeDtypeStruct((R, C), F32)] * 3, compiler_params=_params(1),
    )(w, g, m, v)


def _rows_tile(rows, cols):
    return _tile(rows, max(16, (1 << 19) // cols), 16)


def _add_halves(G, span, recv, half, name):
    n, L, R, C = G.shape
    l0, nl = span[0], span[1] - span[0]
    hr = R // 2
    tr = _rows_tile(hr, C)
    nb = hr // tr

    def body(half_ref, g_ref, r_ref, o_ref):
        o_ref[...] = (g_ref[...] + r_ref[...]).astype(BF16)

    grid_spec = pltpu.PrefetchScalarGridSpec(
        num_scalar_prefetch=1, grid=(n, nl, nb),
        in_specs=[pl.BlockSpec((None, None, tr, C), lambda k, l, i, h: (k, l0 + l, h[0] * nb + i, 0)),
                  pl.BlockSpec((None, None, tr, C), lambda k, l, i, h: (k, l, i, 0))],
        out_specs=pl.BlockSpec((None, None, tr, C), lambda k, l, i, h: (k, l, i, 0)))
    return pl.pallas_call(
        body, name="rs_add_halves_" + name, grid_spec=grid_spec,
        out_shape=jax.ShapeDtypeStruct((n, nl, hr, C), BF16), compiler_params=_params(3),
    )(half.reshape(1).astype(jnp.int32), G, recv)


def _sum_slots(H, slots, place, name, l0, n_layers, prev=None):
    _, nl, hr, C = slots.shape
    tr = _rows_tile(hr, C)
    nb = hr // tr

    def body(x_ref, y_ref, c_ref, own_ref, s1_ref, s2_ref, s3_ref, *rest):
        o_ref = rest[-1]
        o_ref[...] = ((own_ref[...].astype(F32) + s1_ref[...].astype(F32)) + s2_ref[...].astype(F32)) + s3_ref[...].astype(F32)

    def src(fx, fy):
        def index(l, j, px, py, pc):
            cx = px[0] + fx - 2 * fx * px[0]
            cy = py[0] + fy - 2 * fy * py[0]
            return (2 * cx + cy, l, j, 0)
        return pl.BlockSpec((None, None, tr, C), index)

    keep = [] if prev is None else [prev]
    grid_spec = pltpu.PrefetchScalarGridSpec(
        num_scalar_prefetch=3, grid=(nl, nb),
        in_specs=[src(0, 0), src(0, 1), src(1, 0), src(1, 1)] + [pl.BlockSpec(memory_space=pl.ANY)] * len(keep),
        out_specs=pl.BlockSpec((None, tr, C), lambda l, j, px, py, pc: (l0 + l, pc[0] * nb + j, 0)))
    return pl.pallas_call(
        body, name="rs_sum_slots_" + name, grid_spec=grid_spec,
        out_shape=jax.ShapeDtypeStruct((n_layers, 2 * hr, C), F32),
        input_output_aliases={7: 0} if keep else {}, compiler_params=_params(2),
    )(*place, H, slots, slots, slots, *keep)


HBM = pl.BlockSpec(memory_space=pltpu.HBM)


def _place():
    x, y, c = lax.axis_index("x"), lax.axis_index("y"), lax.axis_index("c")
    return x, y, c, 2 * x + y


def _chip_device(chip, c):
    return (chip // 2, chip % 2, c)


def _remote(src, dst, send_sem, recv_sem, to):
    return pltpu.make_async_remote_copy(src_ref=src, dst_ref=dst, send_sem=send_sem, recv_sem=recv_sem, device_id=to,
                                        device_id_type=MESH)


def _place_own(w, chip, name):
    L, R, C = w.shape
    tr = _rows_tile(R, C)

    def body(p_ref, w_ref, o_ref):
        o_ref[...] = w_ref[...].astype(BF16)

    grid_spec = pltpu.PrefetchScalarGridSpec(
        num_scalar_prefetch=1, grid=(L, R // tr), in_specs=[pl.BlockSpec((None, tr, C), lambda l, j, p: (l, j, 0))],
        out_specs=pl.BlockSpec((None, None, tr, C), lambda l, j, p: (l, p[0], j, 0)))
    return pl.pallas_call(
        body, name="place_" + name, grid_spec=grid_spec,
        out_shape=jax.ShapeDtypeStruct((L, N_CHIPS, R, C), BF16), compiler_params=_params(2),
    )(chip.reshape(1).astype(jnp.int32), w)


def _gather_weights(bufs, l):
    n = len(bufs)

    def body(*refs):
        o_refs = refs[n:2 * n]
        send_sems, recv_sems = refs[2 * n:]
        x, y, c, me = _place()
        sibling = (x, y, 1 - c)

        def copy(t, k, chip, half, to):
            hr = o_refs[t].shape[2] // 2
            block = o_refs[t].at[l, chip, pl.ds(half * hr, hr), :]
            return _remote(block, block, send_sems.at[6 * t + k], recv_sems.at[6 * t + k], to)

        first = [copy(t, d - 1, me, c, _chip_device(me ^ d, c)) for t in range(n) for d in (1, 2, 3)]
        for cp in first:
            cp.start()
        passed = []
        for t in range(n):
            for d in (1, 2, 3):
                copy(t, d - 1, me ^ d, c, sibling).wait_recv()
                passed.append(copy(t, 2 + d, me ^ d, c, sibling))
                passed[-1].start()
        for t in range(n):
            for d in (1, 2, 3):
                copy(t, 2 + d, me ^ d, 1 - c, sibling).wait_recv()
        for cp in first + passed:
            cp.wait_send()

    return pl.pallas_call(
        body, name="gather_weights", in_specs=[HBM] * n, out_specs=[HBM] * n,
        out_shape=[jax.ShapeDtypeStruct(b.shape, b.dtype) for b in bufs],
        input_output_aliases={t: t for t in range(n)},
        scratch_shapes=[pltpu.SemaphoreType.DMA((6 * n,)), pltpu.SemaphoreType.DMA((6 * n,))],
    )(*bufs)


def _join_halves(bufs):
    n = len(bufs)

    def body(*refs):
        o_refs = refs[n:2 * n]
        send_sems, recv_sems = refs[2 * n:]
        x, y, c, _ = _place()

        def half(t, which):
            hr = o_refs[t].shape[1] // 2
            return o_refs[t].at[:, pl.ds(which * hr, hr), :]

        sends = [_remote(half(t, c), half(t, c), send_sems.at[t], recv_sems.at[t], (x, y, 1 - c)) for t in range(n)]
        for cp in sends:
            cp.start()
        for t in range(n):
            _remote(half(t, 1 - c), half(t, 1 - c), send_sems.at[t], recv_sems.at[t], (x, y, 1 - c)).wait_recv()
        for cp in sends:
            cp.wait_send()

    return pl.pallas_call(
        body, name="rs_join_halves", in_specs=[HBM] * n, out_specs=[HBM] * n,
        out_shape=[jax.ShapeDtypeStruct(b.shape, b.dtype) for b in bufs],
        input_output_aliases={t: t for t in range(n)},
        scratch_shapes=[pltpu.SemaphoreType.DMA((n,)), pltpu.SemaphoreType.DMA((n,))],
    )(*bufs)


def _all_reduce_small(v, name):
    R, C = v.shape

    def body(v_ref, o_ref, slots, send_sems, recv_sems):
        x, y, c, _ = _place()
        me = 4 * x + 2 * y + c
        slots[me] = v_ref[...]
        sends = []
        for d in range(1, 8):
            peer = me ^ d
            sends.append(pltpu.make_async_remote_copy(
                src_ref=v_ref, dst_ref=slots.at[me], send_sem=send_sems.at[d - 1], recv_sem=recv_sems.at[d - 1],
                device_id=(peer // 4, (peer // 2) % 2, peer % 2), device_id_type=MESH))
        for cp in sends:
            cp.start()
        for d in range(1, 8):
            peer = me ^ d
            pltpu.make_async_remote_copy(
                src_ref=v_ref, dst_ref=slots.at[peer], send_sem=send_sems.at[d - 1], recv_sem=recv_sems.at[d - 1],
                device_id=(peer // 4, (peer // 2) % 2, peer % 2), device_id_type=MESH).wait_recv()
        for cp in sends:
            cp.wait_send()
        acc = slots[0]
        for k in range(1, 8):
            acc = acc + slots[k]
        o_ref[...] = acc

    vm = pl.BlockSpec(memory_space=pltpu.VMEM)
    return pl.pallas_call(
        body, name=name, in_specs=[vm], out_specs=vm, out_shape=jax.ShapeDtypeStruct((R, C), F32),
        scratch_shapes=[pltpu.VMEM((8, R, C), F32), pltpu.SemaphoreType.DMA((7,)), pltpu.SemaphoreType.DMA((7,))],
    )(v)


SHARDED = ("w_in", "w_uq", "w_ukv", "conv_w", "w_br_a", "w_br_b", "w_br_c", "w_br_d", "w_out", "w_ffn_gate", "w_ffn_up",
           "w_ffn_down")
ROW_SHARDED = ("w_out", "w_ffn_down")
REPLICATED = ("g_pre_mix", "g_cq", "g_ckv", "pool_w", "pool_scale", "g_sgu_v", "sgu_w", "sgu_b", "g_post_mix", "g_pre_ffn",
              "g_post_ffn")
WEIGHTS = ("w_in", "g_pre_mix", "g_cq", "g_ckv", "w_uq", "w_ukv", "pool_w", "pool_scale", "g_sgu_v", "sgu_w", "sgu_b",
           "conv_w", "w_br_a", "w_br_b", "w_br_c", "w_br_d", "w_out", "g_post_mix", "g_pre_ffn", "w_ffn_gate", "w_ffn_up",
           "w_ffn_down", "g_post_ffn")
GATHERED = tuple(n for n in SHARDED if n != "conv_w")


def _unpack(packed, shapes):
    flat = packed.reshape(-1)
    out, o = [], 0
    for s in shapes:
        n = int(np.prod(s))
        out.append(flat[o:o + n].reshape(s))
        o += n
    return out


def _join_cols(g, l):
    return jnp.concatenate([g[l, k] for k in range(N_CHIPS)], axis=1)


def _pad_heads(w, real):
    lead = w.shape[:-1]
    w = w.reshape(lead + (HEADS, real))
    return jnp.pad(w, [(0, 0)] * len(lead) + [(0, 0), (0, HEAD_PAD - real)]).reshape(lead + (MLA_W,))


def _unpad_heads(w, real):
    lead = w.shape[:-1]
    return w.reshape(lead + (HEADS, HEAD_PAD))[..., :real].reshape(lead + (HEADS * real,))


IN_OFFSETS = {"cq": 0, "ckv": Q_LORA, "kr": Q_LORA + KV_LORA, "mix": Q_LORA + KV_LORA + QK_ROPE}
IN_GATES = Q_LORA + KV_LORA + QK_ROPE + SEC_MIX


def _pad_w_in(w):
    K = w.shape[0]
    z = lambda n: jnp.zeros((K, n), w.dtype)
    return jnp.concatenate([w[:, IN_GATES:], w[:, :IN_OFFSETS["kr"]], z(QK_NOPE), w[:, IN_OFFSETS["kr"]:IN_OFFSETS["mix"]],
                            z(HEAD_PAD - QK_NOPE - QK_ROPE), w[:, IN_OFFSETS["mix"]:IN_GATES]], axis=1)


def _unpad_w_in(d_gates, d_mla, d_mix):
    kr = d_mla[:, Q_LORA + KV_LORA + QK_NOPE:Q_LORA + KV_LORA + QK_NOPE + QK_ROPE]
    return jnp.concatenate([d_mla[:, :Q_LORA + KV_LORA], kr, d_mix, d_gates], axis=1)


def _rope_tables(S):
    half = QK_ROPE // 2
    inv = ROPE_THETA ** (-jnp.arange(0, QK_ROPE, 2, dtype=F32) / QK_ROPE)
    ang = jnp.arange(S, dtype=F32)[:, None] * inv[None, :]
    cos, sin = jnp.cos(ang), jnp.sin(ang)
    one, zero = jnp.ones((S, QK_NOPE), F32), jnp.zeros((S, half), F32)
    tail = HEAD_PAD - QK_NOPE - QK_ROPE
    c = jnp.concatenate([one, cos, cos, jnp.ones((S, tail), F32)], axis=1)
    sa = jnp.concatenate([0 * one, zero, sin, jnp.zeros((S, tail), F32)], axis=1)
    sb = jnp.concatenate([0 * one, -sin, zero, jnp.zeros((S, tail), F32)], axis=1)
    return c, sa, sb


def _layer_weights(gathered, full, l, D):
    w = {}
    w_in = _pad_w_in(_join_cols(gathered["w_in"], l))
    w["w_in"] = w_in
    w["w_in_gates"], w["w_in_mla"], w["w_in_mix"] = w_in[:, :4 * D], w_in[:, 4 * D:4 * D + SEC_MLA], w_in[:, 4 * D + SEC_MLA:]
    w["w_uq"] = _pad_heads(_join_cols(gathered["w_uq"], l), QK_NOPE + QK_ROPE)
    ukv = _join_cols(gathered["w_ukv"], l).reshape(KV_LORA, HEADS, QK_NOPE + V_HEAD)
    pad = ((0, 0), (0, 0), (0, HEAD_PAD - QK_NOPE))
    k_pad = jnp.pad(ukv[:, :, :QK_NOPE], pad).reshape(KV_LORA, N_CHIPS, CHIP_HEADS_W)
    v_pad = jnp.pad(ukv[:, :, QK_NOPE:], pad).reshape(KV_LORA, N_CHIPS, CHIP_HEADS_W)
    w["w_ukv"] = jnp.concatenate([k_pad, v_pad], axis=2).reshape(KV_LORA, 2 * MLA_W)
    w["w_br_a"] = jnp.pad(_join_cols(gathered["w_br_a"], l).reshape(HEADS, V_HEAD, D),
                          ((0, 0), (0, HEAD_PAD - V_HEAD), (0, 0))).reshape(MLA_W, D)
    for n in ("w_br_b", "w_br_c", "w_br_d"):
        w[n] = _join_cols(gathered[n], l)
    w["w_out"] = gathered["w_out"][l].reshape(D, D)
    for n in ("g_pre_mix", "g_cq", "g_ckv", "pool_scale", "g_sgu_v", "g_post_mix", "g_pre_ffn", "g_post_ffn"):
        w[n] = full[n][l].reshape(1, -1)
    pw = full["pool_w"][l]
    w["pool_bd"] = jax.scipy.linalg.block_diag(*[pw[g] for g in range(GROUPS)]).astype(BF16)
    w["sgu_w"] = full["sgu_w"][l]
    w["sgu_bias"] = jnp.repeat(full["sgu_b"][l].T, GROUP_DIM, axis=1)
    w["conv_w"] = jnp.pad(full["conv_w"][l].reshape(3, MIX_W), ((0, CONV_HALO - 3), (0, 0)))
    return w


def _layer_fwd(x, w, gathered, l, S, rope, gather_next):
    D = x.shape[1]
    bufs = [gathered[n] for n in GATHERED]
    P, h = _norm_matmul(x, w["g_pre_mix"], w["w_in"], "proj_fwd")
    Q, K, V, hq, hkv = _mla_prep(P, D, w["g_cq"], w["g_ckv"], w["w_uq"], w["w_ukv"], *rope, S)
    A, lse, bufs = _flash_fwd(Q, K, V, S, _gather_ride(bufs, l + 1, "chips") if gather_next else None)
    Bm, C, Dv, bufs = _mixers_fwd(P, D, S, w["pool_bd"], w["pool_scale"], w["g_sgu_v"], w["sgu_w"], w["sgu_bias"], w["conv_w"],
                                  _gather_ride(bufs, l + 1, "cores") if gather_next else None)
    if gather_next:
        gathered = dict(zip(GATHERED, bufs))
    x1, merged, o = _merge_fwd(x, P, A, Bm, C, Dv, w["w_br_a"], w["w_br_b"], w["w_br_c"], w["w_br_d"], w["w_out"], w["g_post_mix"])
    x2, h2, gt, up, f = _ffn_fwd(x1, w["g_pre_ffn"], gathered["w_ffn_gate"], gathered["w_ffn_up"], gathered["w_ffn_down"],
                                 w["g_post_ffn"], l)
    saved = dict(x=x, P=P, h=h, Q=Q, K=K, V=V, hq=hq, hkv=hkv, A=A, lse=lse, Bm=Bm, C=C, Dv=Dv, x1=x1, merged=merged, o=o,
                 h2=h2, gt=gt, up=up, f=f)
    return x2, saved, gathered


RIDE_SETS = (("w_ffn_gate", "w_out", "w_br_a"), ("w_in", "w_br_b", "w_br_c", "w_br_d", "w_uq", "w_ukv"),
             ("w_ffn_up", "w_ffn_down"))


def _layer_bwd(dx2, w, gathered, l, s, S, rope, bufs, early=None, send_ffn=False):
    D = dx2.shape[1]
    g = {}
    others = [n for n in GATHERED if n not in FFN_WEIGHTS]

    def scatter(k):
        return _scatter_ride([sums[n] for n in RIDE_SETS[k]]) if early else None

    sums, slots = {}, {}
    if early:
        done, spans, half, got_ffn = early
        ride = _exchange_ride([done[n] for n in others], [spans[n] for n in others])
    df, g["g_post_ffn"] = _norm_bwd(dx2, s["f"], w["g_post_ffn"], "ffn_bwd_pre")
    dh_parts, filled, got = _ffn_bwd_main(df, s["gt"], s["up"], s["h2"], gathered["w_ffn_gate"], gathered["w_ffn_up"],
                                          gathered["w_ffn_down"], l, bufs if "w_ffn_gate" in bufs else None,
                                          ride if early else None)
    bufs.update(filled)
    dx1, g["g_pre_ffn"] = _norm_bwd(None, s["x1"], w["g_pre_ffn"], "ffn_bwd_post", add=dx2, parts=dh_parts)
    if early:
        sums = {n: _add_halves(done[n], spans[n], r, half, n) for n, r in zip(others, got)}
        sums.update({n: _add_halves(filled[n], spans[n], got_ffn[n], half, n) for n in FFN_WEIGHTS})
        sums, dx1 = lax.optimization_barrier((sums, dx1))

    (dPg, dA, dBm, dC, dDv, g["g_post_mix"], delta_t), filled, got = _merge_bwd(
        dx1, s["o"], s["merged"], s["P"], s["A"], s["Bm"], s["C"], s["Dv"], w["w_br_a"], w["w_br_b"], w["w_br_c"], w["w_br_d"],
        w["w_out"], w["g_post_mix"], l, bufs if "w_out" in bufs else None, scatter(0))
    bufs.update(filled)
    slots.update(zip(RIDE_SETS[0], got))

    dPm, d_pool_bd, g_ps, g_gv, g["sgu_w"], d_bias, d_cw = _mixers_bwd(
        s["P"], D, S, dBm, dC, dDv, w["pool_bd"], w["pool_scale"], w["g_sgu_v"], w["sgu_w"], w["sgu_bias"], w["conv_w"])
    g["pool_w"] = jnp.stack([d_pool_bd[k * GROUP_DIM:(k + 1) * GROUP_DIM, k * GROUP_DIM:(k + 1) * GROUP_DIM] for k in range(GROUPS)])
    g["pool_scale"], g["g_sgu_v"] = g_ps, g_gv
    g["sgu_b"] = d_bias.reshape(SGU_BLOCK, GROUPS, GROUP_DIM).sum(-1).T
    g["conv_w"] = d_cw[:3].reshape(3, 1, MIX_W)

    dQ, dK, dV, got = _flash_bwd(s["Q"], s["K"], s["V"], dA, s["lse"], delta_t, S, scatter(1))
    slots.update(zip(RIDE_SETS[1], got))
    (dPa, g["g_cq"], g["g_ckv"]), filled = _mla_bwd_post(
        s["P"], D, S, dQ, dK, dV, s["hq"], s["hkv"], w["g_cq"], w["g_ckv"], w["w_uq"], w["w_ukv"], *rope, l,
        bufs if "w_uq" in bufs else None)
    bufs.update(filled)

    ride = scatter(2)
    if send_ffn:
        ride = _exchange_ride([bufs[n] for n in FFN_WEIGHTS], [(l, DEPTH)] * len(FFN_WEIGHTS))
    dx, g["g_pre_mix"], got = _proj_bwd(dx1, s["x"], w["g_pre_mix"], dPg, dPa, dPm, w["w_in_gates"], w["w_in_mla"],
                                        w["w_in_mix"], ride)
    if send_ffn:
        slots = dict(zip(FFN_WEIGHTS, got))
    else:
        slots.update(zip(RIDE_SETS[2], got))
    d_w_in = _unpad_w_in(_matmul_tn(s["h"], dPg, "wgrad_in_gates"), _matmul_tn(s["h"], dPa, "wgrad_in_mla"),
                         _matmul_tn(s["h"], dPm, "wgrad_in_mix"))
    g["w_in"] = d_w_in.reshape(D, N_CHIPS, -1).transpose(1, 0, 2)
    for n in ("g_pre_mix", "g_cq", "g_ckv", "pool_scale", "g_sgu_v", "g_post_mix", "g_pre_ffn", "g_post_ffn"):
        g[n] = g[n].reshape(-1)
    return dx, g, (sums, slots)


SMALL = REPLICATED + ("conv_w",)


def _local_step(x, target, gathered, full, core):
    n_seq, S, D = x.shape
    rope = _rope_tables(S)
    xs = x.reshape(n_seq * S, D)
    weights, saved = [], []
    for l in range(DEPTH):
        w = _layer_weights(gathered, full, l, D)
        gather_next = l + 1 < DEPTH
        if gather_next:
            w, gathered = lax.optimization_barrier((w, gathered))
        xs, s, gathered = _layer_fwd(xs, w, gathered, l, S, rope, gather_next)
        weights.append(w)
        saved.append(s)
    loss_parts, dx = _loss_grad(xs, target.reshape(n_seq * S, D))
    grads, bufs = [None] * DEPTH, {}
    for l in reversed(range(1, DEPTH)):
        dx, grads[l], (_, got_ffn) = _layer_bwd(dx, weights[l], gathered, l, saved[l], S, rope, bufs, send_ffn=l == 1)
    done = dict(bufs, w_in=jnp.stack([grads[l]["w_in"] for l in range(1, DEPTH)], axis=1))
    spans = dict({n: (1, DEPTH) for n in GATHERED}, w_in=(0, DEPTH - 1))
    dx, grads[0], early = _layer_bwd(dx, weights[0], gathered, 0, saved[0], S, rope, bufs, (done, spans, core, got_ffn))
    last = dict(bufs, w_in=grads[0]["w_in"][:, None])
    small = {n: jnp.stack([grads[l][n] for l in range(DEPTH)]) for n in SMALL}
    return loss_parts, dx.reshape(n_seq, S, D), early, last, small


def _unpad_reduced(n, r):
    L = r.shape[0]
    if n == "w_uq":
        return r.reshape(L, Q_LORA, 2, HEAD_PAD)[..., :QK_NOPE + QK_ROPE].reshape(L, Q_LORA, -1)
    if n == "w_ukv":
        r = r.reshape(L, KV_LORA, 2, 2, HEAD_PAD)[..., :QK_NOPE]
        return jnp.concatenate([r[:, :, 0], r[:, :, 1]], axis=-1).reshape(L, KV_LORA, -1)
    if n == "w_br_a":
        return r.reshape(L, HEADS, HEAD_PAD, -1)[:, :, :V_HEAD].reshape(L, HEADS * V_HEAD, -1)
    return r


def _small_rows(n):
    return -(-n // (8 * 128)) * 8


def _to_small(parts):
    flat = jnp.concatenate([p.reshape(-1) for p in parts])
    rows = _small_rows(flat.shape[0])
    return jnp.pad(flat, (0, rows * 128 - flat.shape[0])).reshape(rows, 128)


def kernel(x, w_in, g_pre_mix, g_cq, g_ckv, w_uq, w_ukv, pool_w, pool_scale, g_sgu_v, sgu_w, sgu_b, conv_w, w_br_a, w_br_b, w_br_c, w_br_d, w_out, g_post_mix, g_pre_ffn, w_ffn_gate, w_ffn_up, w_ffn_down, g_post_ffn, loss_target, m_w_in, m_g_pre_mix, m_g_cq, m_g_ckv, m_w_uq, m_w_ukv, m_pool_w, m_pool_scale, m_g_sgu_v, m_sgu_w, m_sgu_b, m_conv_w, m_w_br_a, m_w_br_b, m_w_br_c, m_w_br_d, m_w_out, m_g_post_mix, m_g_pre_ffn, m_w_ffn_gate, m_w_ffn_up, m_w_ffn_down, m_g_post_ffn, v_w_in, v_g_pre_mix, v_g_cq, v_g_ckv, v_w_uq, v_w_ukv, v_pool_w, v_pool_scale, v_g_sgu_v, v_sgu_w, v_sgu_b, v_conv_w, v_w_br_a, v_w_br_b, v_w_br_c, v_w_br_d, v_w_out, v_g_post_mix, v_g_pre_ffn, v_w_ffn_gate, v_w_ffn_up, v_w_ffn_down, v_g_post_ffn):
    local = dict(locals())
    W = {n: local[n] for n in WEIGHTS}
    M = {n: local["m_" + n] for n in WEIGHTS}
    V = {n: local["v_" + n] for n in WEIGHTS}
    chip = 2 * lax.axis_index("x") + lax.axis_index("y")
    core = lax.axis_index("c")

    gathered = dict(zip(GATHERED, _gather_weights([_place_own(W[n], chip, n) for n in GATHERED], 0)))
    conv_shape = conv_w.shape
    conv_cols = conv_shape[-1]
    conv_full_shape = conv_shape[:-1] + (N_CHIPS * conv_cols,)
    placed = lax.dynamic_update_slice(jnp.zeros(conv_full_shape, F32), conv_w, (0, 0, 0, chip * conv_cols))
    n_conv = int(np.prod(conv_full_shape))
    conv_sum = _all_reduce_small(_to_small([placed]), "gather_conv_w")
    full = {n: W[n] for n in REPLICATED}
    full["conv_w"] = 0.5 * conv_sum.reshape(-1)[:n_conv].reshape(conv_full_shape)

    loss_parts, grad_x, (sums_up, slots_up), last, small = _local_step(x, loss_target, gathered, full, core)
    loss = lax.psum(jnp.sum(loss_parts), ("x", "y", "c"))

    small_sum = _all_reduce_small(_to_small([small[n] for n in SMALL]), "reduce_small_grads")
    small_grads = dict(zip(SMALL, _unpack(small_sum, [small[n].shape for n in SMALL])))
    small_grads["conv_w"] = lax.dynamic_slice(small_grads["conv_w"], (0, 0, 0, chip * conv_cols), conv_shape)

    first = [(0, 1)] * len(GATHERED)
    Gs = [last[n] for n in GATHERED]
    got = _run_ride(_exchange_ride(Gs, first), "rs_exchange_halves")
    sums_0 = [_add_halves(g, (0, 1), r, core, n) for n, g, r in zip(GATHERED, Gs, got)]
    slots_0 = _run_ride(_scatter_ride(sums_0), "rs_scatter_partials")
    place = [lax.axis_index(a).reshape(1).astype(jnp.int32) for a in ("x", "y", "c")]
    halves = []
    for n, h, s in zip(GATHERED, sums_0, slots_0):
        upper = _sum_slots(sums_up[n], slots_up[n], place, n, 1, DEPTH)
        halves.append(_sum_slots(h, s, place, n, 0, DEPTH, prev=upper))
    shard_grads = {n: _unpad_reduced(n, r).reshape(W[n].shape) for n, r in zip(GATHERED, _join_halves(halves))}

    out_g, out_d, out_m, out_v = {}, {}, {}, {}
    for n in GATHERED:
        shp = W[n].shape
        flat = lambda a: a.reshape(-1, shp[-1])
        d, m2, v2 = _adamw(flat(W[n]), flat(shard_grads[n]), flat(M[n]), flat(V[n]), "adamw_" + n)
        out_g[n], out_d[n], out_m[n], out_v[n] = shard_grads[n], d.reshape(shp), m2.reshape(shp), v2.reshape(shp)
    rest_shapes = [W[n].shape for n in SMALL]
    d, m2, v2 = _adamw(_to_small([W[n] for n in SMALL]), _to_small([small_grads[n] for n in SMALL]),
                       _to_small([M[n] for n in SMALL]), _to_small([V[n] for n in SMALL]), "adamw_small")
    for n, dd, mm, vv in zip(SMALL, _unpack(d, rest_shapes), _unpack(m2, rest_shapes), _unpack(v2, rest_shapes)):
        out_g[n], out_d[n], out_m[n], out_v[n] = small_grads[n], dd, mm, vv

    return (loss, grad_x, *[out_g[n] for n in WEIGHTS], *[out_d[n] for n in WEIGHTS], *[out_m[n] for n in WEIGHTS],
            *[out_v[n] for n in WEIGHTS])
```

```python
import functools

import numpy as np
import jax
import jax.numpy as jnp
from jax import lax
from jax.experimental import pallas as pl
from jax.experimental.pallas import tpu as pltpu

F32 = jnp.float32
BF16 = jnp.bfloat16

EPS = 1e-6
NEG_INF = -1e30
DEPTH = 4
HEADS = 8
QK_NOPE = 64
QK_ROPE = 32
V_HEAD = 64
HEAD_PAD = 128
Q_LORA = 256
KV_LORA = 128
ROPE_THETA = 10000.0
POOL_WINDOWS = (2, 4, 8, 16)
GROUPS = 4
GROUP_DIM = 64
MIX_W = GROUPS * GROUP_DIM
POOL_HALO = 16
CONV_HALO = 16
SGU_BLOCK = 128
CHUNK = 64
CHUNK_SHIFT = 6
ACT_ROWS = 16
NORM_ROWS = 16
ROW_PARTS = 2
SOFTMAX_ROWS = 32
GROUP_SHIFT = 6
N_BRANCH = 4
MLA_W = HEADS * HEAD_PAD
N_CHIPS = 4
CHIP_HEADS_W = MLA_W // N_CHIPS
CHIP_KV = 2 * CHIP_HEADS_W
ATTN_SCALE = (QK_NOPE + QK_ROPE) ** -0.5
LOG2E = 1.4426950408889634
LN2 = 0.6931471805599453
SEC_MLA = Q_LORA + KV_LORA + HEAD_PAD
SEC_MIX = 6 * MIX_W

ADAM_LR = 0.001
ADAM_B1 = 0.9
ADAM_B2 = 0.999
ADAM_EPS = 1e-08
ADAM_WD = 0.01
ADAM_STEP = 10

VMEM_LIMIT = 56 * 1024 * 1024
MESH = pl.DeviceIdType.MESH


def _tile(n, pref, mult=8):
    t = min(n, pref)
    while t > 0:
        if n % t == 0 and t % mult == 0:
            return t
        t -= 1
    return n


def _params(n_axes):
    return pltpu.CompilerParams(dimension_semantics=("arbitrary",) * n_axes, vmem_limit_bytes=VMEM_LIMIT)


def _dot(a, b):
    return jnp.dot(a, b, preferred_element_type=F32)


def _dot_nt(a, b):
    return lax.dot_general(a, b, (((1,), (1,)), ((), ())), preferred_element_type=F32)


def _dot_tn(a, b):
    return lax.dot_general(a, b, (((0,), (0,)), ((), ())), preferred_element_type=F32)


def _rms_r(x):
    return lax.rsqrt(jnp.mean(x * x, axis=-1, keepdims=True) + EPS)


def _rms_bwd(dy, x, g):
    r = _rms_r(x)
    u = dy * g
    dx = r * u - x * (r * r * r * jnp.mean(u * x, axis=-1, keepdims=True))
    dg = jnp.sum(dy * x * r, axis=0, keepdims=True)
    return dx, dg


def _sigmoid(x):
    return 1.0 / (1.0 + jnp.exp(-x))


def _shift_down(a, k):
    return pltpu.roll(a, k, 0)


def _shift_up(a, k):
    return pltpu.roll(a, a.shape[0] - k, 0)


def _rope(x, c, sa, sb):
    w = x.shape[-1]
    return x * c + pltpu.roll(x, QK_ROPE // 2, 1) * sa + pltpu.roll(x, w - QK_ROPE // 2, 1) * sb


def _rope_t(d, c, sa, sb):
    w = d.shape[-1]
    return d * c + pltpu.roll(d * sa, w - QK_ROPE // 2, 1) + pltpu.roll(d * sb, QK_ROPE // 2, 1)


def _full(shape):
    return pl.BlockSpec(shape, lambda *_: (0,) * len(shape))


def _gather_ride(bufs, l, stage):
    def copies(_, o_refs, send_sems, recv_sems):
        x, y, c, me = _place()
        sends, arrivals = [], []
        for t, o in enumerate(o_refs):
            hr = o.shape[2] // 2
            for d in (1, 2, 3):
                sems = (send_sems.at[3 * t + d - 1], recv_sems.at[3 * t + d - 1])
                mine = o.at[l, me, pl.ds(c * hr, hr), :]
                theirs = o.at[l, me ^ d, pl.ds(c * hr, hr), :]
                other_half = o.at[l, me ^ d, pl.ds((1 - c) * hr, hr), :]
                if stage == "chips":
                    sends.append(_remote(mine, mine, *sems, _chip_device(me ^ d, c)))
                    arrivals.append(_remote(theirs, theirs, *sems, _chip_device(me ^ d, c)))
                else:
                    sends.append(_remote(theirs, theirs, *sems, (x, y, 1 - c)))
                    arrivals.append(_remote(other_half, other_half, *sems, (x, y, 1 - c)))
        return sends, arrivals

    shapes = [jax.ShapeDtypeStruct(b.shape, b.dtype) for b in bufs]
    return dict(ins=list(bufs), outs=shapes, alias=True, copies=copies, n_sems=3 * len(bufs))


def _exchange_ride(Gs, spans):
    def copies(g_refs, o_refs, send_sems, recv_sems):
        x, y, c, _ = _place()
        cps = []
        for t, (g, o) in enumerate(zip(g_refs, o_refs)):
            hr = g.shape[2] // 2
            l0, l1 = spans[t]
            cps.append(_remote(g.at[:, pl.ds(l0, l1 - l0), pl.ds((1 - c) * hr, hr), :], o, send_sems.at[t], recv_sems.at[t],
                               (x, y, 1 - c)))
        return cps, cps

    shapes = [jax.ShapeDtypeStruct((g.shape[0], l1 - l0, g.shape[2] // 2, g.shape[3]), g.dtype) for g, (l0, l1) in zip(Gs, spans)]
    return dict(ins=list(Gs), outs=shapes, alias=False, copies=copies, n_sems=len(Gs))


def _scatter_ride(Hs):
    def copies(h_refs, o_refs, send_sems, recv_sems):
        x, y, c, me = _place()
        sends, arrivals = [], []
        for t, (h, o) in enumerate(zip(h_refs, o_refs)):
            for d in (1, 2, 3):
                sems = (send_sems.at[3 * t + d - 1], recv_sems.at[3 * t + d - 1])
                sends.append(_remote(h.at[me ^ d], o.at[me], *sems, _chip_device(me ^ d, c)))
                arrivals.append(_remote(h.at[me ^ d], o.at[me ^ d], *sems, _chip_device(me ^ d, c)))
        return sends, arrivals

    shapes = [jax.ShapeDtypeStruct(h.shape, h.dtype) for h in Hs]
    return dict(ins=list(Hs), outs=shapes, alias=False, copies=copies, n_sems=3 * len(Hs))


def _ride_specs(ride, n_in, n_out):
    if not ride:
        return [], [], [], [], [], {}
    anywhere = pl.BlockSpec(memory_space=pl.ANY)
    sems = pltpu.SemaphoreType.DMA((ride["n_sems"],))
    alias = {n_in + i: n_out + i for i in range(len(ride["ins"]))} if ride["alias"] else {}
    return [anywhere] * len(ride["ins"]), [anywhere] * len(ride["outs"]), list(ride["outs"]), [sems, sems], ride["ins"], alias


def _ride_split(ride, rest, n_out, n_scratch):
    a = len(ride["ins"]) if ride else 0
    b = a + n_out
    c = b + (len(ride["outs"]) if ride else 0)
    d = c + n_scratch
    riding = (ride, rest[:a], rest[b:c], rest[d:]) if ride else None
    return rest[a:b], rest[c:d], riding


def _ride_start(riding, first):
    if riding:
        ride, in_refs, out_refs, (send_sems, recv_sems) = riding

        @pl.when(first)
        def _():
            for cp in ride["copies"](in_refs, out_refs, send_sems, recv_sems)[0]:
                cp.start()


def _ride_finish(riding, last):
    if riding:
        ride, in_refs, out_refs, (send_sems, recv_sems) = riding

        @pl.when(last)
        def _():
            sends, arrivals = ride["copies"](in_refs, out_refs, send_sems, recv_sems)
            for cp in arrivals:
                cp.wait_recv()
            for cp in sends:
                cp.wait_send()


def _run_ride(ride, name):
    def body(*refs):
        _, _, (_, in_refs, out_refs, (send_sems, recv_sems)) = _ride_split(ride, refs, 0, 0)
        sends, arrivals = ride["copies"](in_refs, out_refs, send_sems, recv_sems)
        for cp in sends:
            cp.start()
        for cp in arrivals:
            cp.wait_recv()
        for cp in sends:
            cp.wait_send()

    in_specs, out_specs, out_shapes, scratch, operands, alias = _ride_specs(ride, 0, 0)
    return list(pl.pallas_call(body, name=name, in_specs=in_specs, out_specs=out_specs, out_shape=out_shapes,
                               input_output_aliases=alias, scratch_shapes=scratch)(*operands))


def _norm_matmul(x, g, w, name):
    T, K = x.shape
    N = w.shape[1]
    tm, tn = _tile(T, 2048), _tile(N, 1536, 128)

    def body(x_ref, g_ref, w_ref, o_ref, h_ref):
        @pl.when(pl.program_id(1) == 0)
        def _():
            xv = x_ref[...]
            h_ref[...] = (xv * _rms_r(xv) * g_ref[...]).astype(BF16)

        o_ref[...] = _dot(h_ref[...], w_ref[...]).astype(BF16)

    return pl.pallas_call(
        body, name=name, grid=(T // tm, N // tn),
        in_specs=[pl.BlockSpec((tm, K), lambda i, j: (i, 0)), _full((1, K)), pl.BlockSpec((K, tn), lambda i, j: (0, j))],
        out_specs=[pl.BlockSpec((tm, tn), lambda i, j: (i, j)), pl.BlockSpec((tm, K), lambda i, j: (i, 0))],
        out_shape=[jax.ShapeDtypeStruct((T, N), BF16), jax.ShapeDtypeStruct((T, K), BF16)],
        compiler_params=_params(2),
    )(x, g, w)


def _mla_prep(P, D, g_cq, g_ckv, wuq, wukv, rope_c, rope_sa, rope_sb, S):
    T = P.shape[0]
    tm = _tile(S, 512)
    n_si = S // tm
    base = 4 * D

    def body(cq_ref, ckv_ref, kr_ref, gq_ref, gkv_ref, wq_ref, wkv_ref, c_ref, sa_ref, sb_ref,
             q_ref, k_ref, v_ref, hq_ref, hkv_ref):
        c, sa, sb = c_ref[...], sa_ref[...], sb_ref[...]
        cq = cq_ref[...].astype(F32)
        hq = (cq * _rms_r(cq) * gq_ref[...]).astype(BF16)
        hq_ref[...] = hq
        q = _dot(hq, wq_ref[...])
        q = _rope(q, jnp.tile(c, (1, HEADS)), jnp.tile(sa, (1, HEADS)), jnp.tile(sb, (1, HEADS)))
        q_ref[...] = (q * (ATTN_SCALE * LOG2E)).astype(BF16)
        ckv = ckv_ref[...].astype(F32)
        hkv = (ckv * _rms_r(ckv) * gkv_ref[...]).astype(BF16)
        hkv_ref[...] = hkv
        kv = _dot(hkv, wkv_ref[...])
        kr = _rope(kr_ref[...].astype(F32), c, sa, sb)
        k_nope = jnp.concatenate([kv[:, j * CHIP_KV:j * CHIP_KV + CHIP_HEADS_W] for j in range(N_CHIPS)], axis=1)
        k_ref[...] = (k_nope + jnp.tile(kr, (1, HEADS))).astype(BF16)
        v = jnp.concatenate([kv[:, j * CHIP_KV + CHIP_HEADS_W:(j + 1) * CHIP_KV] for j in range(N_CHIPS)], axis=1)
        ones_lane = (lax.broadcasted_iota(jnp.int32, (1, MLA_W), 1) & (HEAD_PAD - 1)) == V_HEAD
        v_ref[...] = jnp.where(ones_lane, 1.0, v).astype(BF16)

    tab = pl.BlockSpec((tm, HEAD_PAD), lambda i: (i % n_si, 0))
    row = lambda w: pl.BlockSpec((tm, w), lambda i: (i, 0))
    return pl.pallas_call(
        body, name="mla_prep", grid=(T // tm,),
        in_specs=[pl.BlockSpec((tm, Q_LORA), lambda i: (i, base // Q_LORA)),
                  pl.BlockSpec((tm, KV_LORA), lambda i: (i, (base + Q_LORA) // KV_LORA)),
                  pl.BlockSpec((tm, HEAD_PAD), lambda i: (i, (base + Q_LORA + KV_LORA) // HEAD_PAD)),
                  _full((1, Q_LORA)), _full((1, KV_LORA)), _full((Q_LORA, MLA_W)), _full((KV_LORA, 2 * MLA_W)),
                  tab, tab, tab],
        out_specs=[row(MLA_W), row(MLA_W), row(MLA_W), row(Q_LORA), row(KV_LORA)],
        out_shape=[jax.ShapeDtypeStruct((T, MLA_W), BF16)] * 3
        + [jax.ShapeDtypeStruct((T, Q_LORA), BF16), jax.ShapeDtypeStruct((T, KV_LORA), BF16)],
        compiler_params=_params(1),
    )(P, P, P, g_cq, g_ckv, wuq, wukv, rope_c, rope_sa, rope_sb)


def _chunk_mask(tq, tk):
    row = lax.broadcasted_iota(jnp.int32, (tq, tk), 0)
    col = lax.broadcasted_iota(jnp.int32, (tq, tk), 1)
    return (row >> CHUNK_SHIFT) >= (col >> CHUNK_SHIFT)


def _rows_to_lanes(stats, out_ref):
    t = stats.T
    for h in range(HEADS):
        out_ref[h, 0] = t[h:h + 1, :]


def _flash_fwd(Q, K, V, S, ride=None):
    T = Q.shape[0]
    n_seq = T // S
    tq = _tile(S, 256, 128)
    nq = S // tq

    def body(q_ref, k_ref, v_ref, *rest):
        (o_ref, lse_ref), (m_s, acc_s, s_s, p_s, a_s), riding = _ride_split(ride, rest, 2, 5)
        _ride_start(riding, (pl.program_id(0) == 0) & (pl.program_id(1) == 0))
        qi = pl.program_id(1)
        m_s[...] = jnp.full(m_s.shape, NEG_INF, F32)
        acc_s[...] = jnp.zeros_like(acc_s)

        def block(kb, masked):
            rows = pl.ds(pl.multiple_of(kb * tq, tq), tq)
            for h in range(HEADS):
                hs = slice(h * HEAD_PAD, (h + 1) * HEAD_PAD)
                s_s[h] = _dot_nt(q_ref[:, hs], k_ref[rows, hs])
            def softmax_head(h):
                for r in range(0, tq, SOFTMAX_ROWS):
                    rs = slice(r, r + SOFTMAX_ROWS)
                    s = s_s[h, rs, :]
                    if masked:
                        row = r + lax.broadcasted_iota(jnp.int32, (SOFTMAX_ROWS, tq), 0)
                        col = lax.broadcasted_iota(jnp.int32, (SOFTMAX_ROWS, tq), 1)
                        s = jnp.where((row >> CHUNK_SHIFT) >= (col >> CHUNK_SHIFT), s, NEG_INF)
                    m_old = m_s[h, rs]
                    m_new = jnp.maximum(m_old, jnp.max(s, axis=-1, keepdims=True))
                    m_s[h, rs] = m_new
                    a_s[h, rs] = jnp.exp2(m_old - m_new)
                    for half in range(tq // HEAD_PAD):
                        cs = slice(half * HEAD_PAD, (half + 1) * HEAD_PAD)
                        p_s[h, rs, cs] = jnp.exp2(s[:, cs] - m_new).astype(BF16)

            for h in range(HEADS):
                softmax_head(h)
            for h in range(HEADS):
                hs = slice(h * HEAD_PAD, (h + 1) * HEAD_PAD)
                acc_s[:, hs] = a_s[h] * acc_s[:, hs] + _dot(p_s[h], v_ref[rows, hs])

        def full_block(kb, carry):
            block(kb, False)
            return carry

        lax.fori_loop(0, qi, full_block, 0)
        block(qi, True)
        lane = lax.broadcasted_iota(jnp.int32, (tq, HEAD_PAD), 1)
        lse_all = jnp.zeros((tq, HEAD_PAD), F32)
        for h in range(HEADS):
            hs = slice(h * HEAD_PAD, (h + 1) * HEAD_PAD)
            acc = acc_s[:, hs]
            l = jnp.sum(jnp.where(lane == V_HEAD, acc, 0.0), axis=-1, keepdims=True)
            o_ref[:, hs] = (acc / l).astype(BF16)
            lse_all = jnp.where(lane == h, m_s[h] + jnp.log2(l), lse_all)
        _rows_to_lanes(lse_all, lse_ref)
        _ride_finish(riding, (pl.program_id(0) == n_seq - 1) & (pl.program_id(1) == nq - 1))

    ride_in, ride_out, ride_shapes, ride_scratch, ride_args, alias = _ride_specs(ride, 3, 2)
    out = pl.pallas_call(
        body, name="flash_fwd", grid=(n_seq, nq),
        in_specs=[pl.BlockSpec((tq, MLA_W), lambda b, i: (b * nq + i, 0)),
                  pl.BlockSpec((S, MLA_W), lambda b, i: (b, 0)), pl.BlockSpec((S, MLA_W), lambda b, i: (b, 0))] + ride_in,
        out_specs=[pl.BlockSpec((tq, MLA_W), lambda b, i: (b * nq + i, 0)),
                   pl.BlockSpec((HEADS, 1, 1, tq), lambda b, i: (0, b * nq + i, 0, 0))] + ride_out,
        out_shape=[jax.ShapeDtypeStruct((T, MLA_W), BF16), jax.ShapeDtypeStruct((HEADS, T // tq, 1, tq), F32)] + ride_shapes,
        input_output_aliases=alias,
        scratch_shapes=[pltpu.VMEM((HEADS, tq, HEAD_PAD), F32), pltpu.VMEM((tq, MLA_W), F32), pltpu.VMEM((HEADS, tq, tq), F32),
                        pltpu.VMEM((HEADS, tq, tq), BF16), pltpu.VMEM((HEADS, tq, HEAD_PAD), F32)] + ride_scratch,
        compiler_params=_params(2),
    )(Q, K, V, *ride_args)
    return out[0], out[1], list(out[2:])


def _lane_group():
    return lax.broadcasted_iota(jnp.int32, (1, MIX_W), 1) >> GROUP_SHIFT


def _by_group(a0, a1, a2, a3):
    g = _lane_group()
    return jnp.where(g == 0, a0, jnp.where(g == 1, a1, jnp.where(g == 2, a2, a3)))


def _pool_count(si, tc, rows):
    pos = si * tc + lax.broadcasted_iota(jnp.int32, (rows, MIX_W), 0)
    win = _by_group(*POOL_WINDOWS)
    return jnp.minimum(pos + 1, win).astype(F32)


def _pool_fwd(z, z_prev, si, tc):
    ze = jnp.concatenate([z_prev, z], axis=0)
    s1 = ze + _shift_down(ze, 1)
    s2 = s1 + _shift_down(s1, 2)
    s4 = s2 + _shift_down(s2, 4)
    s8 = s4 + _shift_down(s4, 8)
    win_sum = _by_group(s1, s2, s4, s8)[POOL_HALO:]
    return win_sum / _pool_count(si, tc, tc) - z


def _sgu_weights(w_ref):
    row = lax.broadcasted_iota(jnp.int32, (SGU_BLOCK, SGU_BLOCK), 0)
    col = lax.broadcasted_iota(jnp.int32, (SGU_BLOCK, SGU_BLOCK), 1)
    keep = (row >> CHUNK_SHIFT) >= (col >> CHUNK_SHIFT)
    return keep, [jnp.where(keep, w_ref[g], 0.0).astype(BF16) for g in range(GROUPS)]


def _sgu_mix(vn_blk, wm, bias):
    g = _lane_group()
    mixed = bias
    for k in range(GROUPS):
        mixed = mixed + jnp.where(g == k, _dot(wm[k], vn_blk), 0.0)
    return mixed


def _conv_fwd(z, z_prev, w_ref):
    ze = jnp.concatenate([z_prev, z], axis=0)
    y = w_ref[0:1, :] * _shift_down(ze, 2) + w_ref[1:2, :] * _shift_down(ze, 1) + w_ref[2:3, :] * ze
    return y[CONV_HALO:]


def _mix_specs(T, D, tc):
    base = (4 * D + SEC_MLA) // MIX_W
    cur = lambda k: pl.BlockSpec((tc, MIX_W), lambda i: (i, base + k))
    prev = lambda k, halo: pl.BlockSpec((halo, MIX_W), lambda i: (jnp.maximum(i * (tc // halo) - 1, 0), base + k))
    nxt = lambda k, halo: pl.BlockSpec((halo, MIX_W), lambda i: (jnp.minimum((i + 1) * (tc // halo), T // halo - 1), base + k))
    return cur, prev, nxt


def _mixers_fwd(P, D, S, pool_bd, pool_scale, g_v, sgu_w, sgu_bias, conv_w, ride=None):
    T = P.shape[0]
    tc = _tile(S, 512, SGU_BLOCK)
    n_si = S // tc
    cur, prev, _ = _mix_specs(T, D, tc)

    def body(z_ref, zp_ref, u_ref, v_ref, b_ref, c_ref, x_ref, cp_ref, xp_ref,
             pw_ref, ps_ref, gv_ref, sw_ref, sb_ref, cw_ref, *rest):
        (ob_ref, oc_ref, od_ref), _, riding = _ride_split(ride, rest, 3, 0)
        _ride_start(riding, pl.program_id(0) == 0)
        si = pl.program_id(0) % n_si
        first = si == 0
        z = z_ref[...].astype(F32)
        pooled = _pool_fwd(z, jnp.where(first, 0.0, zp_ref[...].astype(F32)), si, tc)
        ob_ref[...] = (_dot(pooled.astype(BF16), pw_ref[...]) * ps_ref[...]).astype(BF16)

        v = v_ref[...].astype(F32)
        vn = (v * _rms_r(v) * gv_ref[...]).astype(BF16)
        _, wm = _sgu_weights(sw_ref)
        for blk in range(tc // SGU_BLOCK):
            rows = slice(blk * SGU_BLOCK, (blk + 1) * SGU_BLOCK)
            oc_ref[rows, :] = (u_ref[rows, :].astype(F32) * _sgu_mix(vn[rows], wm, sb_ref[...])).astype(BF16)

        zc = c_ref[...].astype(F32) * x_ref[...].astype(F32)
        zc_prev = jnp.where(first, 0.0, cp_ref[...].astype(F32) * xp_ref[...].astype(F32))
        od_ref[...] = (b_ref[...].astype(F32) * _conv_fwd(zc, zc_prev, cw_ref)).astype(BF16)
        _ride_finish(riding, pl.program_id(0) == T // tc - 1)

    out = pl.BlockSpec((tc, MIX_W), lambda i: (i, 0))
    ride_in, ride_out, ride_shapes, ride_scratch, ride_args, alias = _ride_specs(ride, 15, 3)
    res = pl.pallas_call(
        body, name="mixers_fwd", grid=(T // tc,),
        in_specs=[cur(0), prev(0, POOL_HALO), cur(1), cur(2), cur(3), cur(4), cur(5), prev(4, CONV_HALO), prev(5, CONV_HALO),
                  _full((MIX_W, MIX_W)), _full((1, MIX_W)), _full((1, MIX_W)), _full((GROUPS, SGU_BLOCK, SGU_BLOCK)),
                  _full((SGU_BLOCK, MIX_W)), _full((CONV_HALO, MIX_W))] + ride_in,
        out_specs=[out, out, out] + ride_out,
        out_shape=[jax.ShapeDtypeStruct((T, MIX_W), BF16)] * 3 + ride_shapes,
        input_output_aliases=alias,
        scratch_shapes=ride_scratch,
        compiler_params=_params(1),
    )(P, P, P, P, P, P, P, P, P, pool_bd, pool_scale, g_v, sgu_w, sgu_bias, conv_w, *ride_args)
    return res[0], res[1], res[2], list(res[3:])


def _merge_fwd(x, P, A, Bm, C, Dv, wa, wb, wc, wd, wout, g_post):
    T, D = x.shape
    tm = _tile(T, 512)

    def body(x_ref, lg_ref, a_ref, b_ref, c_ref, d_ref, wa_ref, wb_ref, wc_ref, wd_ref, wo_ref, g_ref,
             x1_ref, mg_ref, o_ref):
        merged = jnp.zeros((tm, D), F32)
        for k, (br, w) in enumerate(((a_ref, wa_ref), (b_ref, wb_ref), (c_ref, wc_ref), (d_ref, wd_ref))):
            merged = merged + _sigmoid(lg_ref[:, k * D:(k + 1) * D].astype(F32)) * _dot(br[...], w[...])
        mg = merged.astype(BF16)
        mg_ref[...] = mg
        o = _dot(mg, wo_ref[...])
        o_ref[...] = o
        x1_ref[...] = x_ref[...] + o * _rms_r(o) * g_ref[...]

    row = lambda w: pl.BlockSpec((tm, w), lambda i: (i, 0))
    return pl.pallas_call(
        body, name="merge_fwd", grid=(T // tm,),
        in_specs=[row(D), row(4 * D), row(MLA_W), row(MIX_W), row(MIX_W), row(MIX_W),
                  _full((MLA_W, D)), _full((MIX_W, D)), _full((MIX_W, D)), _full((MIX_W, D)), _full((D, D)), _full((1, D))],
        out_specs=[row(D), row(D), row(D)],
        out_shape=[jax.ShapeDtypeStruct((T, D), F32), jax.ShapeDtypeStruct((T, D), BF16), jax.ShapeDtypeStruct((T, D), F32)],
        compiler_params=_params(1),
    )(x, P, A, Bm, C, Dv, wa, wb, wc, wd, wout, g_post)


def _ffn_specs(T, D, Fc, l, rows=512):
    tm = _tile(T, rows)
    row = pl.BlockSpec((tm, D), lambda i, j: (i, 0))
    col = pl.BlockSpec((None, tm, Fc), lambda i, j: (j, i, 0))
    w_in = pl.BlockSpec((None, None, D, Fc), lambda i, j: (l, j, 0, 0))
    w_out = pl.BlockSpec((None, None, Fc, D), lambda i, j: (l, j, 0, 0))
    return tm, row, col, w_in, w_out


def _ffn_fwd(x1, g_pre, wg, wu, wdn, g_post, l):
    T, D = x1.shape
    nf, Fc = wg.shape[1], wg.shape[3]
    tm, row, col, w_in, w_out = _ffn_specs(T, D, Fc, l, 1024)

    def body(x_ref, gp_ref, wg_ref, wu_ref, wd_ref, gq_ref, x2_ref, h_ref, gt_ref, up_ref, f_ref, gt_s, up_s, a_s):
        j = pl.program_id(1)

        @pl.when(j == 0)
        def _():
            for r in range(0, tm, NORM_ROWS):
                rs = slice(r, r + NORM_ROWS)
                xv = x_ref[rs, :]
                h_ref[rs, :] = (xv * _rms_r(xv) * gp_ref[...]).astype(BF16)
            f_ref[...] = jnp.zeros_like(f_ref)

        gt_s[...] = _dot(h_ref[...], wg_ref[...])
        up_s[...] = _dot(h_ref[...], wu_ref[...])
        for r in range(0, tm, ACT_ROWS):
            rs = slice(r, r + ACT_ROWS)
            gt, up = gt_s[rs, :], up_s[rs, :]
            gt_ref[rs, :] = gt.astype(BF16)
            up_ref[rs, :] = up.astype(BF16)
            a_s[rs, :] = (gt * _sigmoid(gt) * up).astype(BF16)
        f_ref[...] += _dot(a_s[...], wd_ref[...])

        @pl.when(j == nf - 1)
        def _():
            for r in range(0, tm, NORM_ROWS):
                rs = slice(r, r + NORM_ROWS)
                f = f_ref[rs, :]
                x2_ref[rs, :] = x_ref[rs, :] + f * _rms_r(f) * gq_ref[...]

    return pl.pallas_call(
        body, name="ffn_fwd", grid=(T // tm, nf),
        in_specs=[row, _full((1, D)), w_in, w_in, w_out, _full((1, D))],
        out_specs=[row, row, col, col, row],
        out_shape=[jax.ShapeDtypeStruct((T, D), F32), jax.ShapeDtypeStruct((T, D), BF16),
                   jax.ShapeDtypeStruct((nf, T, Fc), BF16), jax.ShapeDtypeStruct((nf, T, Fc), BF16),
                   jax.ShapeDtypeStruct((T, D), F32)],
        scratch_shapes=[pltpu.VMEM((tm, Fc), F32), pltpu.VMEM((tm, Fc), F32), pltpu.VMEM((tm, Fc), BF16)],
        compiler_params=_params(2),
    )(x1, g_pre, wg, wu, wdn, g_post)


def _loss_grad(y, target):
    T, D = y.shape
    tm = _tile(T, 512)

    def body(y_ref, t_ref, l_ref, dy_ref):
        @pl.when(pl.program_id(0) == 0)
        def _():
            l_ref[...] = jnp.zeros_like(l_ref)

        d = y_ref[...] - t_ref[...]
        dy_ref[...] = d * (1.0 / D)
        e = jnp.sum((d * d).reshape(tm // 8, 8, D), axis=0)
        part = e[:, 0:128]
        for k in range(1, D // 128):
            part = part + e[:, k * 128:(k + 1) * 128]
        l_ref[...] += part * (0.5 / D)

    row = pl.BlockSpec((tm, D), lambda i: (i, 0))
    return pl.pallas_call(
        body, name="loss_grad", grid=(T // tm,),
        in_specs=[row, row], out_specs=[_full((8, 128)), row],
        out_shape=[jax.ShapeDtypeStruct((8, 128), F32), jax.ShapeDtypeStruct((T, D), F32)],
        compiler_params=_params(1),
    )(y, target)


def _matmul_tn(a, b, name):
    T, M = a.shape
    N = b.shape[1]
    tm, tn, tk = _tile(M, 1024, 128), _tile(N, 1536, 128), _tile(T, 1024)

    def body(a_ref, b_ref, o_ref):
        @pl.when(pl.program_id(2) == 0)
        def _():
            o_ref[...] = jnp.zeros_like(o_ref)

        o_ref[...] += _dot_tn(a_ref[...], b_ref[...])

    return pl.pallas_call(
        body, name=name, grid=(M // tm, N // tn, T // tk),
        in_specs=[pl.BlockSpec((tk, tm), lambda i, j, k: (k, i)), pl.BlockSpec((tk, tn), lambda i, j, k: (k, j))],
        out_specs=pl.BlockSpec((tm, tn), lambda i, j, k: (i, j)),
        out_shape=jax.ShapeDtypeStruct((M, N), F32),
        compiler_params=_params(3),
    )(a, b)


def _norm_bwd(dy, x, g, name, add=None, parts=None):
    T, D = x.shape
    tm = _tile(T, 512)
    n_parts = 0 if parts is None else parts.shape[0]

    def body(*refs):
        dx_ref, dg_ref = refs[-2:]
        ins = list(refs[:-2])
        dy_ref = ins.pop(0) if parts is None else None
        p_ref = ins.pop(0) if parts is not None else None
        x_ref, g_ref = ins[0], ins[1]
        add_ref = ins[2] if add is not None else None

        @pl.when(pl.program_id(0) == 0)
        def _():
            dg_ref[...] = jnp.zeros_like(dg_ref)

        dg_sum = jnp.zeros((1, D), F32)
        for r in range(0, tm, NORM_ROWS):
            rs = slice(r, r + NORM_ROWS)
            if parts is None:
                dy = dy_ref[rs, :]
            else:
                dy = p_ref[0, rs, :].astype(F32)
                for k in range(1, n_parts):
                    dy = dy + p_ref[k, rs, :].astype(F32)
            dx, dg = _rms_bwd(dy, x_ref[rs, :], g_ref[...])
            dx_ref[rs, :] = (dx if add is None else add_ref[rs, :] + dx).astype(dx_ref.dtype)
            dg_sum = dg_sum + dg
        dg_ref[...] += dg_sum

    row = pl.BlockSpec((tm, D), lambda i: (i, 0))
    first = [row] if parts is None else [pl.BlockSpec((n_parts, tm, D), lambda i: (0, i, 0))]
    operands = [dy if parts is None else parts, x, g] + ([] if add is None else [add])
    return pl.pallas_call(
        body, name=name, grid=(T // tm,),
        in_specs=first + [row, _full((1, D))] + ([] if add is None else [row]),
        out_specs=[row, _full((1, D))],
        out_shape=[jax.ShapeDtypeStruct((T, D), BF16 if add is None else F32), jax.ShapeDtypeStruct((1, D), F32)],
        compiler_params=_params(1),
    )(*operands)


FFN_WEIGHTS = ("w_ffn_gate", "w_ffn_up", "w_ffn_down")


def _ffn_bwd_main(df, gt, up, h2, wg, wu, wdn, l, bufs, ride=None):
    T, D = h2.shape
    nf, Fc = wg.shape[1], wg.shape[3]
    tm = _tile(T, 512)
    nt = T // tm
    keep = [] if bufs is None else [bufs[n] for n in FFN_WEIGHTS]

    def body(df_ref, gt_ref, up_ref, h_ref, wg_ref, wu_ref, wd_ref, *rest):
        (dh_ref, gg_ref, gu_ref, gd_ref), (da_s, dgt_s, dup_s, act_s), riding = _ride_split(ride, rest[len(keep):], 4, 4)
        j, i = pl.program_id(0), pl.program_id(1)
        _ride_start(riding, (j == 0) & (i == 0))

        @pl.when(i == 0)
        def _():
            for r in (gg_ref, gu_ref, gd_ref):
                r[...] = jnp.zeros_like(r)

        parts = [slice(p * tm // ROW_PARTS, (p + 1) * tm // ROW_PARTS) for p in range(ROW_PARTS)]
        for ps in parts:
            da_s[ps, :] = _dot_nt(df_ref[ps, :], wd_ref[...])
        for ps in parts:
            for r in range(ps.start, ps.stop, ACT_ROWS):
                rs = slice(r, r + ACT_ROWS)
                da = da_s[rs, :]
                gt = gt_ref[rs, :].astype(F32)
                u = up_ref[rs, :].astype(F32)
                sig = _sigmoid(gt)
                silu = gt * sig
                dgt_s[rs, :] = (da * u * (sig * (1.0 + gt * (1.0 - sig)))).astype(BF16)
                dup_s[rs, :] = (da * silu).astype(BF16)
                act_s[rs, :] = (silu * u).astype(BF16)
            dh_ref[ps, :] = (_dot_nt(dgt_s[ps, :], wg_ref[...]) + _dot_nt(dup_s[ps, :], wu_ref[...])).astype(BF16)
        gg_ref[...] += _dot_tn(h_ref[...], dgt_s[...])
        gu_ref[...] += _dot_tn(h_ref[...], dup_s[...])
        gd_ref[...] += _dot_tn(act_s[...], df_ref[...])
        _ride_finish(riding, (j == nf - 1) & (i == nt - 1))

    row = pl.BlockSpec((tm, D), lambda j, i: (i, 0))
    col = pl.BlockSpec((None, tm, Fc), lambda j, i: (j, i, 0))
    w_in = pl.BlockSpec((None, None, D, Fc), lambda j, i: (l, j, 0, 0))
    w_out = pl.BlockSpec((None, None, Fc, D), lambda j, i: (l, j, 0, 0))
    g_in = pl.BlockSpec((None, None, D, Fc), lambda j, i: (j, l, 0, 0))
    g_out = pl.BlockSpec((None, None, Fc, D), lambda j, i: (j, l, 0, 0))
    n_in = 7
    ride_in, ride_out, ride_shapes, ride_scratch, ride_args, alias = _ride_specs(ride, n_in + len(keep), 4)
    out = pl.pallas_call(
        body, name="ffn_bwd_main", grid=(nf, nt),
        in_specs=[row, col, col, row, w_in, w_in, w_out] + [pl.BlockSpec(memory_space=pl.ANY)] * len(keep) + ride_in,
        out_specs=[pl.BlockSpec((None, tm, D), lambda j, i: (j, i, 0)), g_in, g_in, g_out] + ride_out,
        out_shape=[jax.ShapeDtypeStruct((nf, T, D), BF16), jax.ShapeDtypeStruct((nf, DEPTH, D, Fc), F32),
                   jax.ShapeDtypeStruct((nf, DEPTH, D, Fc), F32), jax.ShapeDtypeStruct((nf, DEPTH, Fc, D), F32)] + ride_shapes,
        input_output_aliases={**{n_in + k: 1 + k for k in range(len(keep))}, **alias},
        scratch_shapes=[pltpu.VMEM((tm, Fc), F32), pltpu.VMEM((tm, Fc), BF16), pltpu.VMEM((tm, Fc), BF16),
                        pltpu.VMEM((tm, Fc), BF16)] + ride_scratch,
        compiler_params=_params(2),
    )(df, gt, up, h2, wg, wu, wdn, *keep, *ride_args)
    return out[0], dict(zip(FFN_WEIGHTS, out[1:4])), list(out[4:])


def _wgrad_out(l, n_layers, rows, cols):
    spec = pl.BlockSpec((N_CHIPS, None, rows, cols), lambda *_: (0, l, 0, 0))
    return spec, jax.ShapeDtypeStruct((N_CHIPS, n_layers, rows, cols), F32)


def _merge_bwd(dx1, o, merged, P, A, Bm, C, Dv, wa, wb, wc, wd, wout, g_post, l, bufs, ride=None):
    T, D = o.shape
    tm = _tile(T, 256)
    Dc = D // N_CHIPS
    names = ("w_out", "w_br_a", "w_br_b", "w_br_c", "w_br_d")
    keep = [] if bufs is None else [bufs[n] for n in names]

    def body(dx1_ref, o_ref, mg_ref, lg_ref, a_ref, b_ref, c_ref, d_ref, wa_ref, wb_ref, wc_ref, wd_ref, wo_ref, g_ref, *rest):
        outs, _, riding = _ride_split(ride, rest[len(keep):], 12, 0)
        dlg_ref, da_ref, db_ref, dc_ref, dd_ref, dg_ref, dt_ref, go_ref, ga_ref, gb_ref, gc_ref, gd_ref = outs
        _ride_start(riding, pl.program_id(0) == 0)

        @pl.when(pl.program_id(0) == 0)
        def _():
            for r in (dg_ref, go_ref, ga_ref, gb_ref, gc_ref, gd_ref):
                r[...] = jnp.zeros_like(r)

        d_o, dg = _rms_bwd(dx1_ref[...], o_ref[...], g_ref[...])
        dg_ref[...] += dg
        d_o = d_o.astype(BF16)
        for k in range(N_CHIPS):
            go_ref[k] += _dot_tn(mg_ref[:, k * Dc:(k + 1) * Dc], d_o)
        dm = _dot_nt(d_o, wo_ref[...])
        branches = ((a_ref, wa_ref, da_ref, ga_ref), (b_ref, wb_ref, db_ref, gb_ref),
                    (c_ref, wc_ref, dc_ref, gc_ref), (d_ref, wd_ref, dd_ref, gd_ref))
        for j, (br, w, dbr_ref, gw_ref) in enumerate(branches):
            gate = _sigmoid(lg_ref[:, j * D:(j + 1) * D].astype(F32))
            y = _dot(br[...], w[...])
            dlg_ref[:, j * D:(j + 1) * D] = (dm * y * gate * (1.0 - gate)).astype(BF16)
            dy = (dm * gate).astype(BF16)
            d_in = _dot_nt(dy, w[...]).astype(dbr_ref.dtype)
            dbr_ref[...] = d_in
            if j == 0:
                prod = d_in.astype(F32) * br[...].astype(F32)
                lane = lax.broadcasted_iota(jnp.int32, (tm, HEAD_PAD), 1)
                delta = jnp.zeros((tm, HEAD_PAD), F32)
                for h in range(HEADS):
                    row_sum = jnp.sum(prod[:, h * HEAD_PAD:(h + 1) * HEAD_PAD], axis=-1, keepdims=True)
                    delta = jnp.where(lane == h, row_sum, delta)
                _rows_to_lanes(delta, dt_ref)
            for k in range(N_CHIPS):
                gw_ref[k] += _dot_tn(br[...], dy[:, k * Dc:(k + 1) * Dc])

        _ride_finish(riding, pl.program_id(0) == T // tm - 1)

    row = lambda w: pl.BlockSpec((tm, w), lambda i: (i, 0))
    wg = [_wgrad_out(l, DEPTH, r, c) for r, c in ((Dc, D), (MLA_W, Dc), (MIX_W, Dc), (MIX_W, Dc), (MIX_W, Dc))]
    n_in = 14
    ride_in, ride_out, ride_shapes, ride_scratch, ride_args, alias = _ride_specs(ride, n_in + len(keep), 12)
    stat = pl.BlockSpec((HEADS, 1, 1, tm), lambda i: (0, i, 0, 0))
    out = pl.pallas_call(
        body, name="merge_bwd", grid=(T // tm,),
        in_specs=[row(D), row(D), row(D), row(4 * D), row(MLA_W), row(MIX_W), row(MIX_W), row(MIX_W),
                  _full((MLA_W, D)), _full((MIX_W, D)), _full((MIX_W, D)), _full((MIX_W, D)), _full((D, D)), _full((1, D))]
        + [pl.BlockSpec(memory_space=pl.ANY)] * len(keep) + ride_in,
        out_specs=[row(4 * D), row(MLA_W), row(MIX_W), row(MIX_W), row(MIX_W), _full((1, D)), stat] + [s for s, _ in wg]
        + ride_out,
        out_shape=[jax.ShapeDtypeStruct((T, 4 * D), BF16), jax.ShapeDtypeStruct((T, MLA_W), BF16)]
        + [jax.ShapeDtypeStruct((T, MIX_W), F32)] * 3 + [jax.ShapeDtypeStruct((1, D), F32),
                                                         jax.ShapeDtypeStruct((HEADS, T // tm, 1, tm), F32)]
        + [s for _, s in wg] + ride_shapes,
        input_output_aliases={**{n_in + i: 7 + i for i in range(len(keep))}, **alias},
        scratch_shapes=ride_scratch,
        compiler_params=_params(1),
    )(dx1, o, merged, P, A, Bm, C, Dv, wa, wb, wc, wd, wout, g_post, *keep, *ride_args)
    return out[:7], dict(zip(names, out[7:12])), list(out[12:])


def _mixers_bwd(P, D, S, dBm, dC, dDv, pool_bd, pool_scale, g_v, sgu_w, sgu_bias, conv_w):
    T = P.shape[0]
    tc = _tile(S, 512, SGU_BLOCK)
    n_si = S // tc
    cur, prev, nxt = _mix_specs(T, D, tc)
    n_blk = tc // SGU_BLOCK

    def body(z_ref, zp_ref, u_ref, v_ref, b_ref, c_ref, x_ref, cp_ref, xp_ref, bn_ref,
             dbm_ref, dbmn_ref, dc_ref, ddv_ref, ddvn_ref,
             pw_ref, ps_ref, gv_ref, sw_ref, sb_ref, cw_ref,
             dp_ref, dpw_ref, dps_ref, dgv_ref, dsw_ref, dsb_ref, dcw_ref, dvn_acc):
        si = pl.program_id(0) % n_si
        first, last = si == 0, si == n_si - 1

        @pl.when(pl.program_id(0) == 0)
        def _():
            for r in (dpw_ref, dps_ref, dgv_ref, dsw_ref, dsb_ref, dcw_ref):
                r[...] = jnp.zeros_like(r)

        z = z_ref[...].astype(F32)
        pooled = _pool_fwd(z, jnp.where(first, 0.0, zp_ref[...].astype(F32)), si, tc).astype(BF16)
        dbm = dbm_ref[...]
        dps_ref[...] += jnp.sum(dbm * _dot(pooled, pw_ref[...]), axis=0, keepdims=True)
        dmix = (jnp.concatenate([dbm, jnp.where(last, 0.0, dbmn_ref[...])], axis=0) * ps_ref[...]).astype(BF16)
        dpw_ref[...] += _dot_tn(pooled, dmix[:tc])
        dpool = _dot_nt(dmix, pw_ref[...])
        e = dpool / _pool_count(si, tc, tc + POOL_HALO)
        f1 = e + _shift_up(e, 1)
        f2 = f1 + _shift_up(f1, 2)
        f4 = f2 + _shift_up(f2, 4)
        f8 = f4 + _shift_up(f4, 8)
        dp_ref[:, 0:MIX_W] = (_by_group(f1, f2, f4, f8)[:tc] - dpool[:tc]).astype(BF16)

        v = v_ref[...].astype(F32)
        vn = (v * _rms_r(v) * gv_ref[...]).astype(BF16)
        keep, wm = _sgu_weights(sw_ref)
        g = _lane_group()
        for blk in range(n_blk):
            rows = slice(blk * SGU_BLOCK, (blk + 1) * SGU_BLOCK)
            vb = vn[rows]
            dc = dc_ref[rows, :]
            dp_ref[rows, MIX_W:2 * MIX_W] = (dc * _sgu_mix(vb, wm, sb_ref[...])).astype(BF16)
            dmx = dc * u_ref[rows, :].astype(F32)
            dsb_ref[...] += dmx
            dvn = jnp.zeros((SGU_BLOCK, MIX_W), F32)
            for k in range(GROUPS):
                dmk = jnp.where(g == k, dmx, 0.0).astype(BF16)
                dsw_ref[k] += jnp.where(keep, _dot_nt(dmk, vb), 0.0)
                dvn = dvn + _dot_tn(wm[k], dmk)
            dvn_acc[rows, :] = dvn
        dv, dg = _rms_bwd(dvn_acc[...], v, gv_ref[...])
        dgv_ref[...] += dg
        dp_ref[:, 2 * MIX_W:3 * MIX_W] = dv.astype(BF16)

        cg, xg, bg = c_ref[...].astype(F32), x_ref[...].astype(F32), b_ref[...].astype(F32)
        zc = cg * xg
        ze = jnp.concatenate([jnp.where(first, 0.0, cp_ref[...].astype(F32) * xp_ref[...].astype(F32)), zc], axis=0)
        z1, z2 = _shift_down(ze, 1)[CONV_HALO:], _shift_down(ze, 2)[CONV_HALO:]
        ddv = ddv_ref[...]
        y = cw_ref[0:1, :] * z2 + cw_ref[1:2, :] * z1 + cw_ref[2:3, :] * zc
        dp_ref[:, 3 * MIX_W:4 * MIX_W] = (ddv * y).astype(BF16)
        dy = ddv * bg
        dcw_ref[0:1, :] += jnp.sum(dy * z2, axis=0, keepdims=True)
        dcw_ref[1:2, :] += jnp.sum(dy * z1, axis=0, keepdims=True)
        dcw_ref[2:3, :] += jnp.sum(dy * zc, axis=0, keepdims=True)
        dye = jnp.concatenate([dy, jnp.where(last, 0.0, ddvn_ref[...] * bn_ref[...].astype(F32))], axis=0)
        dz = (cw_ref[2:3, :] * dye + cw_ref[1:2, :] * _shift_up(dye, 1) + cw_ref[0:1, :] * _shift_up(dye, 2))[:tc]
        dp_ref[:, 4 * MIX_W:5 * MIX_W] = (dz * xg).astype(BF16)
        dp_ref[:, 5 * MIX_W:6 * MIX_W] = (dz * cg).astype(BF16)

    grad = lambda halo: pl.BlockSpec((halo, MIX_W), lambda i: (jnp.minimum((i + 1) * (tc // halo), T // halo - 1), 0))
    out = pl.BlockSpec((tc, MIX_W), lambda i: (i, 0))
    return pl.pallas_call(
        body, name="mixers_bwd", grid=(T // tc,),
        in_specs=[cur(0), prev(0, POOL_HALO), cur(1), cur(2), cur(3), cur(4), cur(5), prev(4, CONV_HALO), prev(5, CONV_HALO),
                  nxt(3, CONV_HALO), out, grad(POOL_HALO), out, out, grad(CONV_HALO),
                  _full((MIX_W, MIX_W)), _full((1, MIX_W)), _full((1, MIX_W)), _full((GROUPS, SGU_BLOCK, SGU_BLOCK)),
                  _full((SGU_BLOCK, MIX_W)), _full((CONV_HALO, MIX_W))],
        out_specs=[pl.BlockSpec((tc, SEC_MIX), lambda i: (i, 0)), _full((MIX_W, MIX_W)), _full((1, MIX_W)), _full((1, MIX_W)),
                   _full((GROUPS, SGU_BLOCK, SGU_BLOCK)), _full((SGU_BLOCK, MIX_W)), _full((CONV_HALO, MIX_W))],
        out_shape=[jax.ShapeDtypeStruct((T, SEC_MIX), BF16), jax.ShapeDtypeStruct((MIX_W, MIX_W), F32),
                   jax.ShapeDtypeStruct((1, MIX_W), F32), jax.ShapeDtypeStruct((1, MIX_W), F32),
                   jax.ShapeDtypeStruct((GROUPS, SGU_BLOCK, SGU_BLOCK), F32), jax.ShapeDtypeStruct((SGU_BLOCK, MIX_W), F32),
                   jax.ShapeDtypeStruct((CONV_HALO, MIX_W), F32)],
        scratch_shapes=[pltpu.VMEM((tc, MIX_W), F32)],
        compiler_params=_params(1),
    )(P, P, P, P, P, P, P, P, P, P, dBm, dBm, dC, dDv, dDv, pool_bd, pool_scale, g_v, sgu_w, sgu_bias, conv_w)


def _attn_tile(S):
    return _tile(S, 256, 128)


def _flash_bwd(Q, K, V, dO, lse_t, delta_t, S, ride=None):
    T = Q.shape[0]
    n_seq = T // S
    tq = _attn_tile(S)
    nq = S // tq

    def body(k_ref, v_ref, q_ref, do_ref, lse_ref, dl_ref, *rest):
        (dq_ref, dk_ref, dv_ref), (s_s, dp_s, p_s, ds_s), riding = _ride_split(ride, rest, 3, 4)
        kb = pl.program_id(1)
        _ride_start(riding, (pl.program_id(0) == 0) & (kb == 0))

        @pl.when(kb == 0)
        def _():
            dq_ref[...] = jnp.zeros_like(dq_ref)

        dk_ref[...] = jnp.zeros_like(dk_ref)
        dv_ref[...] = jnp.zeros_like(dv_ref)

        def block(qi, masked):
            rows = pl.ds(pl.multiple_of(qi * tq, tq), tq)
            for h in range(HEADS):
                hs = slice(h * HEAD_PAD, (h + 1) * HEAD_PAD)
                s_s[h] = _dot_nt(k_ref[:, hs], q_ref[rows, hs])
                dp_s[h] = _dot_nt(v_ref[:, hs], do_ref[rows, hs])
            for h in range(HEADS):
                lse_row, dl_row = lse_ref[h, qi], dl_ref[h, qi]
                for r in range(0, tq, SOFTMAX_ROWS):
                    rs = slice(r, r + SOFTMAX_ROWS)
                    s = s_s[h, rs, :]
                    if masked:
                        key = r + lax.broadcasted_iota(jnp.int32, (SOFTMAX_ROWS, tq), 0)
                        query = lax.broadcasted_iota(jnp.int32, (SOFTMAX_ROWS, tq), 1)
                        s = jnp.where((query >> CHUNK_SHIFT) >= (key >> CHUNK_SHIFT), s, NEG_INF)
                    p = jnp.exp2(s - lse_row)
                    p_s[h, rs, :] = p.astype(BF16)
                    ds_s[h, rs, :] = (p * (dp_s[h, rs, :] - dl_row)).astype(BF16)
            for h in range(HEADS):
                hs = slice(h * HEAD_PAD, (h + 1) * HEAD_PAD)
                dv_ref[:, hs] += _dot(p_s[h], do_ref[rows, hs])
                dk_ref[:, hs] += _dot(ds_s[h], q_ref[rows, hs])
                dq_ref[rows, hs] += _dot_tn(ds_s[h], k_ref[:, hs])

        def full_block(qi, carry):
            block(qi, False)
            return carry

        block(kb, True)
        lax.fori_loop(kb + 1, nq, full_block, 0)
        dk_ref[...] = dk_ref[...] * LN2
        _ride_finish(riding, (pl.program_id(0) == n_seq - 1) & (kb == nq - 1))

    tile = pl.BlockSpec((tq, MLA_W), lambda b, i: (b * nq + i, 0))
    seq = pl.BlockSpec((S, MLA_W), lambda b, i: (b, 0))
    stat = pl.BlockSpec((HEADS, nq, 1, tq), lambda b, i: (0, b, 0, 0))
    ride_in, ride_out, ride_shapes, ride_scratch, ride_args, alias = _ride_specs(ride, 6, 3)
    out = pl.pallas_call(
        body, name="flash_bwd", grid=(n_seq, nq),
        in_specs=[tile, tile, seq, seq, stat, stat] + ride_in,
        out_specs=[seq, tile, tile] + ride_out,
        out_shape=[jax.ShapeDtypeStruct((T, MLA_W), F32)] * 3 + ride_shapes,
        input_output_aliases=alias,
        scratch_shapes=[pltpu.VMEM((HEADS, tq, tq), F32), pltpu.VMEM((HEADS, tq, tq), F32),
                        pltpu.VMEM((HEADS, tq, tq), BF16), pltpu.VMEM((HEADS, tq, tq), BF16)] + ride_scratch,
        compiler_params=_params(2),
    )(K, V, Q, dO, lse_t, delta_t, *ride_args)
    return out[0], out[1], out[2], list(out[3:])


def _mla_bwd_post(P, D, S, dQ, dK, dV, hq, hkv, g_cq, g_ckv, wuq, wukv, rope_c, rope_sa, rope_sb, l, bufs):
    T = P.shape[0]
    tm = _tile(S, 512)
    n_si = S // tm
    base = 4 * D
    names = ("w_uq", "w_ukv")

    def body(cq_ref, ckv_ref, dq_ref, dk_ref, dv_ref, hq_ref, hkv_ref, gq_ref, gkv_ref, wq_ref, wkv_ref, c_ref, sa_ref, sb_ref,
             *rest):
        dp_ref, dgq_ref, dgkv_ref, guq_ref, gukv_ref = rest[-5:]

        @pl.when(pl.program_id(0) == 0)
        def _():
            for r in (dgq_ref, dgkv_ref, guq_ref, gukv_ref):
                r[...] = jnp.zeros_like(r)

        c, sa, sb = c_ref[...], sa_ref[...], sb_ref[...]
        dq = _rope_t(dq_ref[...] * ATTN_SCALE, jnp.tile(c, (1, HEADS)), jnp.tile(sa, (1, HEADS)),
                     jnp.tile(sb, (1, HEADS))).astype(BF16)
        dcq, dg = _rms_bwd(_dot_nt(dq, wq_ref[...]), cq_ref[...].astype(F32), gq_ref[...])
        dgq_ref[...] += dg
        dp_ref[:, 0:Q_LORA] = dcq.astype(BF16)

        dk = dk_ref[...]
        dkb, dvb = dk.astype(BF16), dv_ref[...].astype(BF16)
        dkv = jnp.concatenate([p[:, j * CHIP_HEADS_W:(j + 1) * CHIP_HEADS_W] for j in range(N_CHIPS) for p in (dkb, dvb)], axis=1)
        for k in range(N_CHIPS):
            guq_ref[k] += _dot_tn(hq_ref[...], dq[:, k * CHIP_HEADS_W:(k + 1) * CHIP_HEADS_W])
            gukv_ref[k] += _dot_tn(hkv_ref[...], dkv[:, k * CHIP_KV:(k + 1) * CHIP_KV])
        dckv, dg = _rms_bwd(_dot_nt(dkv, wkv_ref[...]), ckv_ref[...].astype(F32), gkv_ref[...])
        dgkv_ref[...] += dg
        dp_ref[:, Q_LORA:Q_LORA + KV_LORA] = dckv.astype(BF16)

        dkr = dk[:, 0:HEAD_PAD]
        for h in range(1, HEADS):
            dkr = dkr + dk[:, h * HEAD_PAD:(h + 1) * HEAD_PAD]
        lane = lax.broadcasted_iota(jnp.int32, (1, HEAD_PAD), 1)
        rope_lanes = (lane >= QK_NOPE) & (lane < QK_NOPE + QK_ROPE)
        dp_ref[:, Q_LORA + KV_LORA:SEC_MLA] = jnp.where(rope_lanes, _rope_t(dkr, c, sa, sb), 0.0).astype(BF16)

    tab = pl.BlockSpec((tm, HEAD_PAD), lambda i: (i % n_si, 0))
    row = lambda w: pl.BlockSpec((tm, w), lambda i: (i, 0))
    wg = [_wgrad_out(l, DEPTH, Q_LORA, CHIP_HEADS_W), _wgrad_out(l, DEPTH, KV_LORA, CHIP_KV)]
    keep = [] if bufs is None else [bufs[n] for n in names]
    n_in = 14
    out = pl.pallas_call(
        body, name="mla_bwd_post", grid=(T // tm,),
        in_specs=[pl.BlockSpec((tm, Q_LORA), lambda i: (i, base // Q_LORA)),
                  pl.BlockSpec((tm, KV_LORA), lambda i: (i, (base + Q_LORA) // KV_LORA)),
                  row(MLA_W), row(MLA_W), row(MLA_W), row(Q_LORA), row(KV_LORA),
                  _full((1, Q_LORA)), _full((1, KV_LORA)), _full((Q_LORA, MLA_W)), _full((KV_LORA, 2 * MLA_W)), tab, tab, tab]
        + [pl.BlockSpec(memory_space=pl.ANY)] * len(keep),
        out_specs=[row(SEC_MLA), _full((1, Q_LORA)), _full((1, KV_LORA))] + [s for s, _ in wg],
        out_shape=[jax.ShapeDtypeStruct((T, SEC_MLA), BF16), jax.ShapeDtypeStruct((1, Q_LORA), F32),
                   jax.ShapeDtypeStruct((1, KV_LORA), F32)] + [s for _, s in wg],
        input_output_aliases={n_in + i: 3 + i for i in range(len(keep))},
        compiler_params=_params(1),
    )(P, P, dQ, dK, dV, hq, hkv, g_cq, g_ckv, wuq, wukv, rope_c, rope_sa, rope_sb, *keep)
    return out[:3], dict(zip(names, out[3:]))


def _proj_bwd(dx1, x, g, dPg, dPa, dPm, w_gates, w_mla, w_mix, ride=None):
    T, D = x.shape
    tm = _tile(T, 512)

    def body(dx1_ref, x_ref, g_ref, dg_ref_in, da_ref, dm_ref, wg_ref, wa_ref, wm_ref, *rest):
        (dx_ref, dg_ref), _, riding = _ride_split(ride, rest, 2, 0)
        _ride_start(riding, pl.program_id(0) == 0)

        @pl.when(pl.program_id(0) == 0)
        def _():
            dg_ref[...] = jnp.zeros_like(dg_ref)

        dh = _dot_nt(dg_ref_in[...], wg_ref[...]) + _dot_nt(da_ref[...], wa_ref[...]) + _dot_nt(dm_ref[...], wm_ref[...])
        dx, dg = _rms_bwd(dh, x_ref[...], g_ref[...])
        dx_ref[...] = dx1_ref[...] + dx
        dg_ref[...] += dg
        _ride_finish(riding, pl.program_id(0) == T // tm - 1)

    row = lambda w: pl.BlockSpec((tm, w), lambda i: (i, 0))
    ride_in, ride_out, ride_shapes, ride_scratch, ride_args, alias = _ride_specs(ride, 9, 2)
    out = pl.pallas_call(
        body, name="proj_bwd", grid=(T // tm,),
        in_specs=[row(D), row(D), _full((1, D)), row(4 * D), row(SEC_MLA), row(SEC_MIX),
                  _full((D, 4 * D)), _full((D, SEC_MLA)), _full((D, SEC_MIX))] + ride_in,
        out_specs=[row(D), _full((1, D))] + ride_out,
        out_shape=[jax.ShapeDtypeStruct((T, D), F32), jax.ShapeDtypeStruct((1, D), F32)] + ride_shapes,
        input_output_aliases=alias, scratch_shapes=ride_scratch,
        compiler_params=_params(1),
    )(dx1, x, g, dPg, dPa, dPm, w_gates, w_mla, w_mix, *ride_args)
    return out[0], out[1], list(out[2:])


def _adamw(w, g, m, v, name):
    R, C = w.shape
    tr = _tile(R, max(8, (1 << 19) // C))

    def body(w_ref, g_ref, m_ref, v_ref, d_ref, mo_ref, vo_ref):
        gv = g_ref[...]
        mn = ADAM_B1 * m_ref[...] + (1.0 - ADAM_B1) * gv
        vn = ADAM_B2 * v_ref[...] + (1.0 - ADAM_B2) * (gv * gv)
        mo_ref[...] = mn
        vo_ref[...] = vn
        m_hat = mn / (1.0 - ADAM_B1 ** ADAM_STEP)
        v_hat = vn / (1.0 - ADAM_B2 ** ADAM_STEP)
        d_ref[...] = -ADAM_LR * (m_hat / (jnp.sqrt(v_hat) + ADAM_EPS) + ADAM_WD * w_ref[...])

    blk = pl.BlockSpec((tr, C), lambda i: (i, 0))
    return pl.pallas_call(
        body, name=name, grid=(R // tr,), in_specs=[blk] * 4, out_specs=[blk] * 3,
        out_shape=[jax.ShapeDtypeStruct((R, C), F32)] * 3, compiler_params=_params(1),
    )(w, g, m, v)


def _rows_tile(rows, cols):
    return _tile(rows, max(16, (1 << 19) // cols), 16)


def _add_halves(G, span, recv, half, name):
    n, L, R, C = G.shape
    l0, nl = span[0], span[1] - span[0]
    hr = R // 2
    tr = _rows_tile(hr, C)
    nb = hr // tr

    def body(half_ref, g_ref, r_ref, o_ref):
        o_ref[...] = (g_ref[...] + r_ref[...]).astype(BF16)

    grid_spec = pltpu.PrefetchScalarGridSpec(
        num_scalar_prefetch=1, grid=(n, nl, nb),
        in_specs=[pl.BlockSpec((None, None, tr, C), lambda k, l, i, h: (k, l0 + l, h[0] * nb + i, 0)),
                  pl.BlockSpec((None, None, tr, C), lambda k, l, i, h: (k, l, i, 0))],
        out_specs=pl.BlockSpec((None, None, tr, C), lambda k, l, i, h: (k, l, i, 0)))
    return pl.pallas_call(
        body, name="rs_add_halves_" + name, grid_spec=grid_spec,
        out_shape=jax.ShapeDtypeStruct((n, nl, hr, C), BF16), compiler_params=_params(3),
    )(half.reshape(1).astype(jnp.int32), G, recv)


def _sum_slots(H, slots, place, name, l0, n_layers, prev=None):
    _, nl, hr, C = slots.shape
    tr = _rows_tile(hr, C)
    nb = hr // tr

    def body(x_ref, y_ref, c_ref, own_ref, s1_ref, s2_ref, s3_ref, *rest):
        o_ref = rest[-1]
        o_ref[...] = ((own_ref[...].astype(F32) + s1_ref[...].astype(F32)) + s2_ref[...].astype(F32)) + s3_ref[...].astype(F32)

    def src(fx, fy):
        def index(l, j, px, py, pc):
            cx = px[0] + fx - 2 * fx * px[0]
            cy = py[0] + fy - 2 * fy * py[0]
            return (2 * cx + cy, l, j, 0)
        return pl.BlockSpec((None, None, tr, C), index)

    keep = [] if prev is None else [prev]
    grid_spec = pltpu.PrefetchScalarGridSpec(
        num_scalar_prefetch=3, grid=(nl, nb),
        in_specs=[src(0, 0), src(0, 1), src(1, 0), src(1, 1)] + [pl.BlockSpec(memory_space=pl.ANY)] * len(keep),
        out_specs=pl.BlockSpec((None, tr, C), lambda l, j, px, py, pc: (l0 + l, pc[0] * nb + j, 0)))
    return pl.pallas_call(
        body, name="rs_sum_slots_" + name, grid_spec=grid_spec,
        out_shape=jax.ShapeDtypeStruct((n_layers, 2 * hr, C), F32),
        input_output_aliases={7: 0} if keep else {}, compiler_params=_params(2),
    )(*place, H, slots, slots, slots, *keep)


HBM = pl.BlockSpec(memory_space=pltpu.HBM)


def _place():
    x, y, c = lax.axis_index("x"), lax.axis_index("y"), lax.axis_index("c")
    return x, y, c, 2 * x + y


def _chip_device(chip, c):
    return (chip // 2, chip % 2, c)


def _remote(src, dst, send_sem, recv_sem, to):
    return pltpu.make_async_remote_copy(src_ref=src, dst_ref=dst, send_sem=send_sem, recv_sem=recv_sem, device_id=to,
                                        device_id_type=MESH)


def _place_own(w, chip, name):
    L, R, C = w.shape
    tr = _rows_tile(R, C)

    def body(p_ref, w_ref, o_ref):
        o_ref[...] = w_ref[...].astype(BF16)

    grid_spec = pltpu.PrefetchScalarGridSpec(
        num_scalar_prefetch=1, grid=(L, R // tr), in_specs=[pl.BlockSpec((None, tr, C), lambda l, j, p: (l, j, 0))],
        out_specs=pl.BlockSpec((None, None, tr, C), lambda l, j, p: (l, p[0], j, 0)))
    return pl.pallas_call(
        body, name="place_" + name, grid_spec=grid_spec,
        out_shape=jax.ShapeDtypeStruct((L, N_CHIPS, R, C), BF16), compiler_params=_params(2),
    )(chip.reshape(1).astype(jnp.int32), w)


def _gather_weights(bufs, l):
    n = len(bufs)

    def body(*refs):
        o_refs = refs[n:2 * n]
        send_sems, recv_sems = refs[2 * n:]
        x, y, c, me = _place()
        sibling = (x, y, 1 - c)

        def copy(t, k, chip, half, to):
            hr = o_refs[t].shape[2] // 2
            block = o_refs[t].at[l, chip, pl.ds(half * hr, hr), :]
            return _remote(block, block, send_sems.at[6 * t + k], recv_sems.at[6 * t + k], to)

        first = [copy(t, d - 1, me, c, _chip_device(me ^ d, c)) for t in range(n) for d in (1, 2, 3)]
        for cp in first:
            cp.start()
        passed = []
        for t in range(n):
            for d in (1, 2, 3):
                copy(t, d - 1, me ^ d, c, sibling).wait_recv()
                passed.append(copy(t, 2 + d, me ^ d, c, sibling))
                passed[-1].start()
        for t in range(n):
            for d in (1, 2, 3):
                copy(t, 2 + d, me ^ d, 1 - c, sibling).wait_recv()
        for cp in first + passed:
            cp.wait_send()

    return pl.pallas_call(
        body, name="gather_weights", in_specs=[HBM] * n, out_specs=[HBM] * n,
        out_shape=[jax.ShapeDtypeStruct(b.shape, b.dtype) for b in bufs],
        input_output_aliases={t: t for t in range(n)},
        scratch_shapes=[pltpu.SemaphoreType.DMA((6 * n,)), pltpu.SemaphoreType.DMA((6 * n,))],
    )(*bufs)


def _join_halves(bufs):
    n = len(bufs)

    def body(*refs):
        o_refs = refs[n:2 * n]
        send_sems, recv_sems = refs[2 * n:]
        x, y, c, _ = _place()

        def half(t, which):
            hr = o_refs[t].shape[1] // 2
            return o_refs[t].at[:, pl.ds(which * hr, hr), :]

        sends = [_remote(half(t, c), half(t, c), send_sems.at[t], recv_sems.at[t], (x, y, 1 - c)) for t in range(n)]
        for cp in sends:
            cp.start()
        for t in range(n):
            _remote(half(t, 1 - c), half(t, 1 - c), send_sems.at[t], recv_sems.at[t], (x, y, 1 - c)).wait_recv()
        for cp in sends:
            cp.wait_send()

    return pl.pallas_call(
        body, name="rs_join_halves", in_specs=[HBM] * n, out_specs=[HBM] * n,
        out_shape=[jax.ShapeDtypeStruct(b.shape, b.dtype) for b in bufs],
        input_output_aliases={t: t for t in range(n)},
        scratch_shapes=[pltpu.SemaphoreType.DMA((n,)), pltpu.SemaphoreType.DMA((n,))],
    )(*bufs)


def _all_reduce_small(v, name):
    R, C = v.shape

    def body(v_ref, o_ref, slots, send_sems, recv_sems):
        x, y, c, _ = _place()
        me = 4 * x + 2 * y + c
        slots[me] = v_ref[...]
        sends = []
        for d in range(1, 8):
            peer = me ^ d
            sends.append(pltpu.make_async_remote_copy(
                src_ref=v_ref, dst_ref=slots.at[me], send_sem=send_sems.at[d - 1], recv_sem=recv_sems.at[d - 1],
                device_id=(peer // 4, (peer // 2) % 2, peer % 2), device_id_type=MESH))
        for cp in sends:
            cp.start()
        for d in range(1, 8):
            peer = me ^ d
            pltpu.make_async_remote_copy(
                src_ref=v_ref, dst_ref=slots.at[peer], send_sem=send_sems.at[d - 1], recv_sem=recv_sems.at[d - 1],
                device_id=(peer // 4, (peer // 2) % 2, peer % 2), device_id_type=MESH).wait_recv()
        for cp in sends:
            cp.wait_send()
        acc = slots[0]
        for k in range(1, 8):
            acc = acc + slots[k]
        o_ref[...] = acc

    vm = pl.BlockSpec(memory_space=pltpu.VMEM)
    return pl.pallas_call(
        body, name=name, in_specs=[vm], out_specs=vm, out_shape=jax.ShapeDtypeStruct((R, C), F32),
        scratch_shapes=[pltpu.VMEM((8, R, C), F32), pltpu.SemaphoreType.DMA((7,)), pltpu.SemaphoreType.DMA((7,))],
    )(v)


SHARDED = ("w_in", "w_uq", "w_ukv", "conv_w", "w_br_a", "w_br_b", "w_br_c", "w_br_d", "w_out", "w_ffn_gate", "w_ffn_up",
           "w_ffn_down")
ROW_SHARDED = ("w_out", "w_ffn_down")
REPLICATED = ("g_pre_mix", "g_cq", "g_ckv", "pool_w", "pool_scale", "g_sgu_v", "sgu_w", "sgu_b", "g_post_mix", "g_pre_ffn",
              "g_post_ffn")
WEIGHTS = ("w_in", "g_pre_mix", "g_cq", "g_ckv", "w_uq", "w_ukv", "pool_w", "pool_scale", "g_sgu_v", "sgu_w", "sgu_b",
           "conv_w", "w_br_a", "w_br_b", "w_br_c", "w_br_d", "w_out", "g_post_mix", "g_pre_ffn", "w_ffn_gate", "w_ffn_up",
           "w_ffn_down", "g_post_ffn")
GATHERED = tuple(n for n in SHARDED if n != "conv_w")


def _unpack(packed, shapes):
    flat = packed.reshape(-1)
    out, o = [], 0
    for s in shapes:
        n = int(np.prod(s))
        out.append(flat[o:o + n].reshape(s))
        o += n
    return out


def _join_cols(g, l):
    return jnp.concatenate([g[l, k] for k in range(N_CHIPS)], axis=1)


def _pad_heads(w, real):
    lead = w.shape[:-1]
    w = w.reshape(lead + (HEADS, real))
    return jnp.pad(w, [(0, 0)] * len(lead) + [(0, 0), (0, HEAD_PAD - real)]).reshape(lead + (MLA_W,))


def _unpad_heads(w, real):
    lead = w.shape[:-1]
    return w.reshape(lead + (HEADS, HEAD_PAD))[..., :real].reshape(lead + (HEADS * real,))


IN_OFFSETS = {"cq": 0, "ckv": Q_LORA, "kr": Q_LORA + KV_LORA, "mix": Q_LORA + KV_LORA + QK_ROPE}
IN_GATES = Q_LORA + KV_LORA + QK_ROPE + SEC_MIX


def _pad_w_in(w):
    K = w.shape[0]
    z = lambda n: jnp.zeros((K, n), w.dtype)
    return jnp.concatenate([w[:, IN_GATES:], w[:, :IN_OFFSETS["kr"]], z(QK_NOPE), w[:, IN_OFFSETS["kr"]:IN_OFFSETS["mix"]],
                            z(HEAD_PAD - QK_NOPE - QK_ROPE), w[:, IN_OFFSETS["mix"]:IN_GATES]], axis=1)


def _unpad_w_in(d_gates, d_mla, d_mix):
    kr = d_mla[:, Q_LORA + KV_LORA + QK_NOPE:Q_LORA + KV_LORA + QK_NOPE + QK_ROPE]
    return jnp.concatenate([d_mla[:, :Q_LORA + KV_LORA], kr, d_mix, d_gates], axis=1)


def _rope_tables(S):
    half = QK_ROPE // 2
    inv = ROPE_THETA ** (-jnp.arange(0, QK_ROPE, 2, dtype=F32) / QK_ROPE)
    ang = jnp.arange(S, dtype=F32)[:, None] * inv[None, :]
    cos, sin = jnp.cos(ang), jnp.sin(ang)
    one, zero = jnp.ones((S, QK_NOPE), F32), jnp.zeros((S, half), F32)
    tail = HEAD_PAD - QK_NOPE - QK_ROPE
    c = jnp.concatenate([one, cos, cos, jnp.ones((S, tail), F32)], axis=1)
    sa = jnp.concatenate([0 * one, zero, sin, jnp.zeros((S, tail), F32)], axis=1)
    sb = jnp.concatenate([0 * one, -sin, zero, jnp.zeros((S, tail), F32)], axis=1)
    return c, sa, sb


def _layer_weights(gathered, full, l, D):
    w = {}
    w_in = _pad_w_in(_join_cols(gathered["w_in"], l))
    w["w_in"] = w_in
    w["w_in_gates"], w["w_in_mla"], w["w_in_mix"] = w_in[:, :4 * D], w_in[:, 4 * D:4 * D + SEC_MLA], w_in[:, 4 * D + SEC_MLA:]
    w["w_uq"] = _pad_heads(_join_cols(gathered["w_uq"], l), QK_NOPE + QK_ROPE)
    ukv = _join_cols(gathered["w_ukv"], l).reshape(KV_LORA, HEADS, QK_NOPE + V_HEAD)
    pad = ((0, 0), (0, 0), (0, HEAD_PAD - QK_NOPE))
    k_pad = jnp.pad(ukv[:, :, :QK_NOPE], pad).reshape(KV_LORA, N_CHIPS, CHIP_HEADS_W)
    v_pad = jnp.pad(ukv[:, :, QK_NOPE:], pad).reshape(KV_LORA, N_CHIPS, CHIP_HEADS_W)
    w["w_ukv"] = jnp.concatenate([k_pad, v_pad], axis=2).reshape(KV_LORA, 2 * MLA_W)
    w["w_br_a"] = jnp.pad(_join_cols(gathered["w_br_a"], l).reshape(HEADS, V_HEAD, D),
                          ((0, 0), (0, HEAD_PAD - V_HEAD), (0, 0))).reshape(MLA_W, D)
    for n in ("w_br_b", "w_br_c", "w_br_d"):
        w[n] = _join_cols(gathered[n], l)
    w["w_out"] = gathered["w_out"][l].reshape(D, D)
    for n in ("g_pre_mix", "g_cq", "g_ckv", "pool_scale", "g_sgu_v", "g_post_mix", "g_pre_ffn", "g_post_ffn"):
        w[n] = full[n][l].reshape(1, -1)
    pw = full["pool_w"][l]
    w["pool_bd"] = jax.scipy.linalg.block_diag(*[pw[g] for g in range(GROUPS)]).astype(BF16)
    w["sgu_w"] = full["sgu_w"][l]
    w["sgu_bias"] = jnp.repeat(full["sgu_b"][l].T, GROUP_DIM, axis=1)
    w["conv_w"] = jnp.pad(full["conv_w"][l].reshape(3, MIX_W), ((0, CONV_HALO - 3), (0, 0)))
    return w


def _layer_fwd(x, w, gathered, l, S, rope, gather_next):
    D = x.shape[1]
    bufs = [gathered[n] for n in GATHERED]
    P, h = _norm_matmul(x, w["g_pre_mix"], w["w_in"], "proj_fwd")
    Q, K, V, hq, hkv = _mla_prep(P, D, w["g_cq"], w["g_ckv"], w["w_uq"], w["w_ukv"], *rope, S)
    A, lse, bufs = _flash_fwd(Q, K, V, S, _gather_ride(bufs, l + 1, "chips") if gather_next else None)
    Bm, C, Dv, bufs = _mixers_fwd(P, D, S, w["pool_bd"], w["pool_scale"], w["g_sgu_v"], w["sgu_w"], w["sgu_bias"], w["conv_w"],
                                  _gather_ride(bufs, l + 1, "cores") if gather_next else None)
    if gather_next:
        gathered = dict(zip(GATHERED, bufs))
    x1, merged, o = _merge_fwd(x, P, A, Bm, C, Dv, w["w_br_a"], w["w_br_b"], w["w_br_c"], w["w_br_d"], w["w_out"], w["g_post_mix"])
    x2, h2, gt, up, f = _ffn_fwd(x1, w["g_pre_ffn"], gathered["w_ffn_gate"], gathered["w_ffn_up"], gathered["w_ffn_down"],
                                 w["g_post_ffn"], l)
    saved = dict(x=x, P=P, h=h, Q=Q, K=K, V=V, hq=hq, hkv=hkv, A=A, lse=lse, Bm=Bm, C=C, Dv=Dv, x1=x1, merged=merged, o=o,
                 h2=h2, gt=gt, up=up, f=f)
    return x2, saved, gathered


RIDE_SETS = (("w_ffn_gate", "w_out", "w_br_a"), ("w_in", "w_br_b", "w_br_c", "w_br_d", "w_uq", "w_ukv"),
             ("w_ffn_up", "w_ffn_down"))


def _layer_bwd(dx2, w, gathered, l, s, S, rope, bufs, early=None, send_ffn=False):
    D = dx2.shape[1]
    g = {}
    others = [n for n in GATHERED if n not in FFN_WEIGHTS]

    def scatter(k):
        return _scatter_ride([sums[n] for n in RIDE_SETS[k]]) if early else None

    sums, slots = {}, {}
    if early:
        done, spans, half, got_ffn = early
        ride = _exchange_ride([done[n] for n in others], [spans[n] for n in others])
    df, g["g_post_ffn"] = _norm_bwd(dx2, s["f"], w["g_post_ffn"], "ffn_bwd_pre")
    dh_parts, filled, got = _ffn_bwd_main(df, s["gt"], s["up"], s["h2"], gathered["w_ffn_gate"], gathered["w_ffn_up"],
                                          gathered["w_ffn_down"], l, bufs if "w_ffn_gate" in bufs else None,
                                          ride if early else None)
    bufs.update(filled)
    dx1, g["g_pre_ffn"] = _norm_bwd(None, s["x1"], w["g_pre_ffn"], "ffn_bwd_post", add=dx2, parts=dh_parts)
    if early:
        sums = {n: _add_halves(done[n], spans[n], r, half, n) for n, r in zip(others, got)}
        sums.update({n: _add_halves(filled[n], spans[n], got_ffn[n], half, n) for n in FFN_WEIGHTS})
        sums, dx1 = lax.optimization_barrier((sums, dx1))

    (dPg, dA, dBm, dC, dDv, g["g_post_mix"], delta_t), filled, got = _merge_bwd(
        dx1, s["o"], s["merged"], s["P"], s["A"], s["Bm"], s["C"], s["Dv"], w["w_br_a"], w["w_br_b"], w["w_br_c"], w["w_br_d"],
        w["w_out"], w["g_post_mix"], l, bufs if "w_out" in bufs else None, scatter(0))
    bufs.update(filled)
    slots.update(zip(RIDE_SETS[0], got))

    dPm, d_pool_bd, g_ps, g_gv, g["sgu_w"], d_bias, d_cw = _mixers_bwd(
        s["P"], D, S, dBm, dC, dDv, w["pool_bd"], w["pool_scale"], w["g_sgu_v"], w["sgu_w"], w["sgu_bias"], w["conv_w"])
    g["pool_w"] = jnp.stack([d_pool_bd[k * GROUP_DIM:(k + 1) * GROUP_DIM, k * GROUP_DIM:(k + 1) * GROUP_DIM] for k in range(GROUPS)])
    g["pool_scale"], g["g_sgu_v"] = g_ps, g_gv
    g["sgu_b"] = d_bias.reshape(SGU_BLOCK, GROUPS, GROUP_DIM).sum(-1).T
    g["conv_w"] = d_cw[:3].reshape(3, 1, MIX_W)

    dQ, dK, dV, got = _flash_bwd(s["Q"], s["K"], s["V"], dA, s["lse"], delta_t, S, scatter(1))
    slots.update(zip(RIDE_SETS[1], got))
    (dPa, g["g_cq"], g["g_ckv"]), filled = _mla_bwd_post(
        s["P"], D, S, dQ, dK, dV, s["hq"], s["hkv"], w["g_cq"], w["g_ckv"], w["w_uq"], w["w_ukv"], *rope, l,
        bufs if "w_uq" in bufs else None)
    bufs.update(filled)

    ride = scatter(2)
    if send_ffn:
        ride = _exchange_ride([bufs[n] for n in FFN_WEIGHTS], [(l, DEPTH)] * len(FFN_WEIGHTS))
    dx, g["g_pre_mix"], got = _proj_bwd(dx1, s["x"], w["g_pre_mix"], dPg, dPa, dPm, w["w_in_gates"], w["w_in_mla"],
                                        w["w_in_mix"], ride)
    if send_ffn:
        slots = dict(zip(FFN_WEIGHTS, got))
    else:
        slots.update(zip(RIDE_SETS[2], got))
    d_w_in = _unpad_w_in(_matmul_tn(s["h"], dPg, "wgrad_in_gates"), _matmul_tn(s["h"], dPa, "wgrad_in_mla"),
                         _matmul_tn(s["h"], dPm, "wgrad_in_mix"))
    g["w_in"] = d_w_in.reshape(D, N_CHIPS, -1).transpose(1, 0, 2)
    for n in ("g_pre_mix", "g_cq", "g_ckv", "pool_scale", "g_sgu_v", "g_post_mix", "g_pre_ffn", "g_post_ffn"):
        g[n] = g[n].reshape(-1)
    return dx, g, (sums, slots)


SMALL = REPLICATED + ("conv_w",)


def _local_step(x, target, gathered, full, core):
    n_seq, S, D = x.shape
    rope = _rope_tables(S)
    xs = x.reshape(n_seq * S, D)
    weights, saved = [], []
    for l in range(DEPTH):
        w = _layer_weights(gathered, full, l, D)
        gather_next = l + 1 < DEPTH
        if gather_next:
            w, gathered = lax.optimization_barrier((w, gathered))
        xs, s, gathered = _layer_fwd(xs, w, gathered, l, S, rope, gather_next)
        weights.append(w)
        saved.append(s)
    loss_parts, dx = _loss_grad(xs, target.reshape(n_seq * S, D))
    grads, bufs = [None] * DEPTH, {}
    for l in reversed(range(1, DEPTH)):
        dx, grads[l], (_, got_ffn) = _layer_bwd(dx, weights[l], gathered, l, saved[l], S, rope, bufs, send_ffn=l == 1)
    done = dict(bufs, w_in=jnp.stack([grads[l]["w_in"] for l in range(1, DEPTH)], axis=1))
    spans = dict({n: (1, DEPTH) for n in GATHERED}, w_in=(0, DEPTH - 1))
    dx, grads[0], early = _layer_bwd(dx, weights[0], gathered, 0, saved[0], S, rope, bufs, (done, spans, core, got_ffn))
    last = dict(bufs, w_in=grads[0]["w_in"][:, None])
    small = {n: jnp.stack([grads[l][n] for l in range(DEPTH)]) for n in SMALL}
    return loss_parts, dx.reshape(n_seq, S, D), early, last, small


def _unpad_reduced(n, r):
    L = r.shape[0]
    if n == "w_uq":
        return r.reshape(L, Q_LORA, 2, HEAD_PAD)[..., :QK_NOPE + QK_ROPE].reshape(L, Q_LORA, -1)
    if n == "w_ukv":
        r = r.reshape(L, KV_LORA, 2, 2, HEAD_PAD)[..., :QK_NOPE]
        return jnp.concatenate([r[:, :, 0], r[:, :, 1]], axis=-1).reshape(L, KV_LORA, -1)
    if n == "w_br_a":
        return r.reshape(L, HEADS, HEAD_PAD, -1)[:, :, :V_HEAD].reshape(L, HEADS * V_HEAD, -1)
    return r


def _small_rows(n):
    return -(-n // (8 * 128)) * 8


def _to_small(parts):
    flat = jnp.concatenate([p.reshape(-1) for p in parts])
    rows = _small_rows(flat.shape[0])
    return jnp.pad(flat, (0, rows * 128 - flat.shape[0])).reshape(rows, 128)


def kernel(x, w_in, g_pre_mix, g_cq, g_ckv, w_uq, w_ukv, pool_w, pool_scale, g_sgu_v, sgu_w, sgu_b, conv_w, w_br_a, w_br_b, w_br_c, w_br_d, w_out, g_post_mix, g_pre_ffn, w_ffn_gate, w_ffn_up, w_ffn_down, g_post_ffn, loss_target, m_w_in, m_g_pre_mix, m_g_cq, m_g_ckv, m_w_uq, m_w_ukv, m_pool_w, m_pool_scale, m_g_sgu_v, m_sgu_w, m_sgu_b, m_conv_w, m_w_br_a, m_w_br_b, m_w_br_c, m_w_br_d, m_w_out, m_g_post_mix, m_g_pre_ffn, m_w_ffn_gate, m_w_ffn_up, m_w_ffn_down, m_g_post_ffn, v_w_in, v_g_pre_mix, v_g_cq, v_g_ckv, v_w_uq, v_w_ukv, v_pool_w, v_pool_scale, v_g_sgu_v, v_sgu_w, v_sgu_b, v_conv_w, v_w_br_a, v_w_br_b, v_w_br_c, v_w_br_d, v_w_out, v_g_post_mix, v_g_pre_ffn, v_w_ffn_gate, v_w_ffn_up, v_w_ffn_down, v_g_post_ffn):
    local = dict(locals())
    W = {n: local[n] for n in WEIGHTS}
    M = {n: local["m_" + n] for n in WEIGHTS}
    V = {n: local["v_" + n] for n in WEIGHTS}
    chip = 2 * lax.axis_index("x") + lax.axis_index("y")
    core = lax.axis_index("c")

    gathered = dict(zip(GATHERED, _gather_weights([_place_own(W[n], chip, n) for n in GATHERED], 0)))
    conv_shape = conv_w.shape
    conv_cols = conv_shape[-1]
    conv_full_shape = conv_shape[:-1] + (N_CHIPS * conv_cols,)
    placed = lax.dynamic_update_slice(jnp.zeros(conv_full_shape, F32), conv_w, (0, 0, 0, chip * conv_cols))
    n_conv = int(np.prod(conv_full_shape))
    conv_sum = _all_reduce_small(_to_small([placed]), "gather_conv_w")
    full = {n: W[n] for n in REPLICATED}
    full["conv_w"] = 0.5 * conv_sum.reshape(-1)[:n_conv].reshape(conv_full_shape)

    loss_parts, grad_x, (sums_up, slots_up), last, small = _local_step(x, loss_target, gathered, full, core)
    loss = lax.psum(jnp.sum(loss_parts), ("x", "y", "c"))

    small_sum = _all_reduce_small(_to_small([small[n] for n in SMALL]), "reduce_small_grads")
    small_grads = dict(zip(SMALL, _unpack(small_sum, [small[n].shape for n in SMALL])))
    small_grads["conv_w"] = lax.dynamic_slice(small_grads["conv_w"], (0, 0, 0, chip * conv_cols), conv_shape)

    first = [(0, 1)] * len(GATHERED)
    Gs = [last[n] for n in GATHERED]
    got = _run_ride(_exchange_ride(Gs, first), "rs_exchange_halves")
    sums_0 = [_add_halves(g, (0, 1), r, core, n) for n, g, r in zip(GATHERED, Gs, got)]
    slots_0 = _run_ride(_scatter_ride(sums_0), "rs_scatter_partials")
    place = [lax.axis_index(a).reshape(1).astype(jnp.int32) for a in ("x", "y", "c")]
    halves = []
    for n, h, s in zip(GATHERED, sums_0, slots_0):
        upper = _sum_slots(sums_up[n], slots_up[n], place, n, 1, DEPTH)
        halves.append(_sum_slots(h, s, place, n, 0, DEPTH, prev=upper))
    shard_grads = {n: _unpad_reduced(n, r).reshape(W[n].shape) for n, r in zip(GATHERED, _join_halves(halves))}

    out_g, out_d, out_m, out_v = {}, {}, {}, {}
    for n in GATHERED:
        shp = W[n].shape
        flat = lambda a: a.reshape(-1, shp[-1])
        d, m2, v2 = _adamw(flat(W[n]), flat(shard_grads[n]), flat(M[n]), flat(V[n]), "adamw_" + n)
        out_g[n], out_d[n], out_m[n], out_v[n] = shard_grads[n], d.reshape(shp), m2.reshape(shp), v2.reshape(shp)
    rest_shapes = [W[n].shape for n in SMALL]
    d, m2, v2 = _adamw(_to_small([W[n] for n in SMALL]), _to_small([small_grads[n] for n in SMALL]),
                       _to_small([M[n] for n in SMALL]), _to_small([V[n] for n in SMALL]), "adamw_small")
    for n, dd, mm, vv in zip(SMALL, _unpack(d, rest_shapes), _unpack(m2, rest_shapes), _unpack(v2, rest_shapes)):
        out_g[n], out_d[n], out_m[n], out_v[n] = small_grads[n], dd, mm, vv

    return (loss, grad_x, *[out_g[n] for n in WEIGHTS], *[out_d[n] for n in WEIGHTS], *[out_m[n] for n in WEIGHTS],
            *[out_v[n] for n in WEIGHTS])
```

```python
import functools

import numpy as np
import jax
import jax.numpy as jnp
from jax import lax
from jax.experimental import pallas as pl
from jax.experimental.pallas import tpu as pltpu

F32 = jnp.float32
BF16 = jnp.bfloat16

EPS = 1e-6
NEG_INF = -1e30
DEPTH = 4
HEADS = 8
QK_NOPE = 64
QK_ROPE = 32
V_HEAD = 64
HEAD_PAD = 128
Q_LORA = 256
KV_LORA = 128
ROPE_THETA = 10000.0
POOL_WINDOWS = (2, 4, 8, 16)
GROUPS = 4
GROUP_DIM = 64
MIX_W = GROUPS * GROUP_DIM
POOL_HALO = 16
CONV_HALO = 16
SGU_BLOCK = 128
CHUNK = 64
CHUNK_SHIFT = 6
ACT_ROWS = 16
NORM_ROWS = 16
ROW_PARTS = 2
SOFTMAX_ROWS = 32
GROUP_SHIFT = 6
N_BRANCH = 4
MLA_W = HEADS * HEAD_PAD
N_CHIPS = 4
CHIP_HEADS_W = MLA_W // N_CHIPS
CHIP_KV = 2 * CHIP_HEADS_W
ATTN_SCALE = (QK_NOPE + QK_ROPE) ** -0.5
LOG2E = 1.4426950408889634
LN2 = 0.6931471805599453
SEC_MLA = Q_LORA + KV_LORA + HEAD_PAD
SEC_MIX = 6 * MIX_W

ADAM_LR = 0.001
ADAM_B1 = 0.9
ADAM_B2 = 0.999
ADAM_EPS = 1e-08
ADAM_WD = 0.01
ADAM_STEP = 10

VMEM_LIMIT = 56 * 1024 * 1024
MESH = pl.DeviceIdType.MESH


def _tile(n, pref, mult=8):
    t = min(n, pref)
    while t > 0:
        if n % t == 0 and t % mult == 0:
            return t
        t -= 1
    return n


def _params(n_axes):
    return pltpu.CompilerParams(dimension_semantics=("arbitrary",) * n_axes, vmem_limit_bytes=VMEM_LIMIT)


def _dot(a, b):
    return jnp.dot(a, b, preferred_element_type=F32)


def _dot_nt(a, b):
    return lax.dot_general(a, b, (((1,), (1,)), ((), ())), preferred_element_type=F32)


def _dot_tn(a, b):
    return lax.dot_general(a, b, (((0,), (0,)), ((), ())), preferred_element_type=F32)


def _rms_r(x):
    return lax.rsqrt(jnp.mean(x * x, axis=-1, keepdims=True) + EPS)


def _rms_bwd(dy, x, g):
    r = _rms_r(x)
    u = dy * g
    dx = r * u - x * (r * r * r * jnp.mean(u * x, axis=-1, keepdims=True))
    dg = jnp.sum(dy * x * r, axis=0, keepdims=True)
    return dx, dg


def _sigmoid(x):
    return 1.0 / (1.0 + jnp.exp(-x))


def _shift_down(a, k):
    return pltpu.roll(a, k, 0)


def _shift_up(a, k):
    return pltpu.roll(a, a.shape[0] - k, 0)


def _rope(x, c, sa, sb):
    w = x.shape[-1]
    return x * c + pltpu.roll(x, QK_ROPE // 2, 1) * sa + pltpu.roll(x, w - QK_ROPE // 2, 1) * sb


def _rope_t(d, c, sa, sb):
    w = d.shape[-1]
    return d * c + pltpu.roll(d * sa, w - QK_ROPE // 2, 1) + pltpu.roll(d * sb, QK_ROPE // 2, 1)


def _full(shape):
    return pl.BlockSpec(shape, lambda *_: (0,) * len(shape))


def _gather_ride(bufs, l, stage):
    def copies(_, o_refs, send_sems, recv_sems):
        x, y, c, me = _place()
        sends, arrivals = [], []
        for t, o in enumerate(o_refs):
            hr = o.shape[2] // 2
            for d in (1, 2, 3):
                sems = (send_sems.at[3 * t + d - 1], recv_sems.at[3 * t + d - 1])
                mine = o.at[l, me, pl.ds(c * hr, hr), :]
                theirs = o.at[l, me ^ d, pl.ds(c * hr, hr), :]
                other_half = o.at[l, me ^ d, pl.ds((1 - c) * hr, hr), :]
                if stage == "chips":
                    sends.append(_remote(mine, mine, *sems, _chip_device(me ^ d, c)))
                    arrivals.append(_remote(theirs, theirs, *sems, _chip_device(me ^ d, c)))
                else:
                    sends.append(_remote(theirs, theirs, *sems, (x, y, 1 - c)))
                    arrivals.append(_remote(other_half, other_half, *sems, (x, y, 1 - c)))
        return sends, arrivals

    shapes = [jax.ShapeDtypeStruct(b.shape, b.dtype) for b in bufs]
    return dict(ins=list(bufs), outs=shapes, alias=True, copies=copies, n_sems=3 * len(bufs))


def _exchange_ride(Gs, spans):
    def copies(g_refs, o_refs, send_sems, recv_sems):
        x, y, c, _ = _place()
        cps = []
        for t, (g, o) in enumerate(zip(g_refs, o_refs)):
            hr = g.shape[2] // 2
            l0, l1 = spans[t]
            cps.append(_remote(g.at[:, pl.ds(l0, l1 - l0), pl.ds((1 - c) * hr, hr), :], o, send_sems.at[t], recv_sems.at[t],
                               (x, y, 1 - c)))
        return cps, cps

    shapes = [jax.ShapeDtypeStruct((g.shape[0], l1 - l0, g.shape[2] // 2, g.shape[3]), g.dtype) for g, (l0, l1) in zip(Gs, spans)]
    return dict(ins=list(Gs), outs=shapes, alias=False, copies=copies, n_sems=len(Gs))


def _scatter_ride(Hs):
    def copies(h_refs, o_refs, send_sems, recv_sems):
        x, y, c, me = _place()
        sends, arrivals = [], []
        for t, (h, o) in enumerate(zip(h_refs, o_refs)):
            for d in (1, 2, 3):
                sems = (send_sems.at[3 * t + d - 1], recv_sems.at[3 * t + d - 1])
                sends.append(_remote(h.at[me ^ d], o.at[me], *sems, _chip_device(me ^ d, c)))
                arrivals.append(_remote(h.at[me ^ d], o.at[me ^ d], *sems, _chip_device(me ^ d, c)))
        return sends, arrivals

    shapes = [jax.ShapeDtypeStruct(h.shape, h.dtype) for h in Hs]
    return dict(ins=list(Hs), outs=shapes, alias=False, copies=copies, n_sems=3 * len(Hs))


def _ride_specs(ride, n_in, n_out):
    if not ride:
        return [], [], [], [], [], {}
    anywhere = pl.BlockSpec(memory_space=pl.ANY)
    sems = pltpu.SemaphoreType.DMA((ride["n_sems"],))
    alias = {n_in + i: n_out + i for i in range(len(ride["ins"]))} if ride["alias"] else {}
    return [anywhere] * len(ride["ins"]), [anywhere] * len(ride["outs"]), list(ride["outs"]), [sems, sems], ride["ins"], alias


def _ride_split(ride, rest, n_out, n_scratch):
    a = len(ride["ins"]) if ride else 0
    b = a + n_out
    c = b + (len(ride["outs"]) if ride else 0)
    d = c + n_scratch
    riding = (ride, rest[:a], rest[b:c], rest[d:]) if ride else None
    return rest[a:b], rest[c:d], riding


def _ride_start(riding, first):
    if riding:
        ride, in_refs, out_refs, (send_sems, recv_sems) = riding

        @pl.when(first)
        def _():
            for cp in ride["copies"](in_refs, out_refs, send_sems, recv_sems)[0]:
                cp.start()


def _ride_finish(riding, last):
    if riding:
        ride, in_refs, out_refs, (send_sems, recv_sems) = riding

        @pl.when(last)
        def _():
            sends, arrivals = ride["copies"](in_refs, out_refs, send_sems, recv_sems)
            for cp in arrivals:
                cp.wait_recv()
            for cp in sends:
                cp.wait_send()


def _run_ride(ride, name):
    def body(*refs):
        _, _, (_, in_refs, out_refs, (send_sems, recv_sems)) = _ride_split(ride, refs, 0, 0)
        sends, arrivals = ride["copies"](in_refs, out_refs, send_sems, recv_sems)
        for cp in sends:
            cp.start()
        for cp in arrivals:
            cp.wait_recv()
        for cp in sends:
            cp.wait_send()

    in_specs, out_specs, out_shapes, scratch, operands, alias = _ride_specs(ride, 0, 0)
    return list(pl.pallas_call(body, name=name, in_specs=in_specs, out_specs=out_specs, out_shape=out_shapes,
                               input_output_aliases=alias, scratch_shapes=scratch)(*operands))


def _norm_matmul(x, g, w, name):
    T, K = x.shape
    N = w.shape[1]
    tm, tn = _tile(T, 2048), _tile(N, 1536, 128)

    def body(x_ref, g_ref, w_ref, o_ref, h_ref):
        @pl.when(pl.program_id(1) == 0)
        def _():
            xv = x_ref[...]
            h_ref[...] = (xv * _rms_r(xv) * g_ref[...]).astype(BF16)

        o_ref[...] = _dot(h_ref[...], w_ref[...]).astype(BF16)

    return pl.pallas_call(
        body, name=name, grid=(T // tm, N // tn),
        in_specs=[pl.BlockSpec((tm, K), lambda i, j: (i, 0)), _full((1, K)), pl.BlockSpec((K, tn), lambda i, j: (0, j))],
        out_specs=[pl.BlockSpec((tm, tn), lambda i, j: (i, j)), pl.BlockSpec((tm, K), lambda i, j: (i, 0))],
        out_shape=[jax.ShapeDtypeStruct((T, N), BF16), jax.ShapeDtypeStruct((T, K), BF16)],
        compiler_params=_params(2),
    )(x, g, w)


def _mla_prep(P, D, g_cq, g_ckv, wuq, wukv, rope_c, rope_sa, rope_sb, S):
    T = P.shape[0]
    tm = _tile(S, 512)
    n_si = S // tm
    base = 4 * D

    def body(cq_ref, ckv_ref, kr_ref, gq_ref, gkv_ref, wq_ref, wkv_ref, c_ref, sa_ref, sb_ref,
             q_ref, k_ref, v_ref, hq_ref, hkv_ref):
        c, sa, sb = c_ref[...], sa_ref[...], sb_ref[...]
        cq = cq_ref[...].astype(F32)
        hq = (cq * _rms_r(cq) * gq_ref[...]).astype(BF16)
        hq_ref[...] = hq
        q = _dot(hq, wq_ref[...])
        q = _rope(q, jnp.tile(c, (1, HEADS)), jnp.tile(sa, (1, HEADS)), jnp.tile(sb, (1, HEADS)))
        q_ref[...] = (q * (ATTN_SCALE * LOG2E)).astype(BF16)
        ckv = ckv_ref[...].astype(F32)
        hkv = (ckv * _rms_r(ckv) * gkv_ref[...]).astype(BF16)
        hkv_ref[...] = hkv
        kv = _dot(hkv, wkv_ref[...])
        kr = _rope(kr_ref[...].astype(F32), c, sa, sb)
        k_nope = jnp.concatenate([kv[:, j * CHIP_KV:j * CHIP_KV + CHIP_HEADS_W] for j in range(N_CHIPS)], axis=1)
        k_ref[...] = (k_nope + jnp.tile(kr, (1, HEADS))).astype(BF16)
        v = jnp.concatenate([kv[:, j * CHIP_KV + CHIP_HEADS_W:(j + 1) * CHIP_KV] for j in range(N_CHIPS)], axis=1)
        ones_lane = (lax.broadcasted_iota(jnp.int32, (1, MLA_W), 1) & (HEAD_PAD - 1)) == V_HEAD
        v_ref[...] = jnp.where(ones_lane, 1.0, v).astype(BF16)

    tab = pl.BlockSpec((tm, HEAD_PAD), lambda i: (i % n_si, 0))
    row = lambda w: pl.BlockSpec((tm, w), lambda i: (i, 0))
    return pl.pallas_call(
        body, name="mla_prep", grid=(T // tm,),
        in_specs=[pl.BlockSpec((tm, Q_LORA), lambda i: (i, base // Q_LORA)),
                  pl.BlockSpec((tm, KV_LORA), lambda i: (i, (base + Q_LORA) // KV_LORA)),
                  pl.BlockSpec((tm, HEAD_PAD), lambda i: (i, (base + Q_LORA + KV_LORA) // HEAD_PAD)),
                  _full((1, Q_LORA)), _full((1, KV_LORA)), _full((Q_LORA, MLA_W)), _full((KV_LORA, 2 * MLA_W)),
                  tab, tab, tab],
        out_specs=[row(MLA_W), row(MLA_W), row(MLA_W), row(Q_LORA), row(KV_LORA)],
        out_shape=[jax.ShapeDtypeStruct((T, MLA_W), BF16)] * 3
        + [jax.ShapeDtypeStruct((T, Q_LORA), BF16), jax.ShapeDtypeStruct((T, KV_LORA), BF16)],
        compiler_params=_params(1),
    )(P, P, P, g_cq, g_ckv, wuq, wukv, rope_c, rope_sa, rope_sb)


def _chunk_mask(tq, tk):
    row = lax.broadcasted_iota(jnp.int32, (tq, tk), 0)
    col = lax.broadcasted_iota(jnp.int32, (tq, tk), 1)
    return (row >> CHUNK_SHIFT) >= (col >> CHUNK_SHIFT)


def _rows_to_lanes(stats, out_ref):
    t = stats.T
    for h in range(HEADS):
        out_ref[h, 0] = t[h:h + 1, :]


def _flash_fwd(Q, K, V, S, ride=None):
    T = Q.shape[0]
    n_seq = T // S
    tq = _tile(S, 256, 128)
    nq = S // tq

    def body(q_ref, k_ref, v_ref, *rest):
        (o_ref, lse_ref), (m_s, acc_s, s_s, p_s, a_s), riding = _ride_split(ride, rest, 2, 5)
        _ride_start(riding, (pl.program_id(0) == 0) & (pl.program_id(1) == 0))
        qi = pl.program_id(1)
        m_s[...] = jnp.full(m_s.shape, NEG_INF, F32)
        acc_s[...] = jnp.zeros_like(acc_s)

        def block(kb, masked):
            rows = pl.ds(pl.multiple_of(kb * tq, tq), tq)
            for h in range(HEADS):
                hs = slice(h * HEAD_PAD, (h + 1) * HEAD_PAD)
                s_s[h] = _dot_nt(q_ref[:, hs], k_ref[rows, hs])
            def softmax_head(h):
                for r in range(0, tq, SOFTMAX_ROWS):
                    rs = slice(r, r + SOFTMAX_ROWS)
                    s = s_s[h, rs, :]
                    if masked:
                        row = r + lax.broadcasted_iota(jnp.int32, (SOFTMAX_ROWS, tq), 0)
                        col = lax.broadcasted_iota(jnp.int32, (SOFTMAX_ROWS, tq), 1)
                        s = jnp.where((row >> CHUNK_SHIFT) >= (col >> CHUNK_SHIFT), s, NEG_INF)
                    m_old = m_s[h, rs]
                    m_new = jnp.maximum(m_old, jnp.max(s, axis=-1, keepdims=True))
                    m_s[h, rs] = m_new
                    a_s[h, rs] = jnp.exp2(m_old - m_new)
                    for half in range(tq // HEAD_PAD):
                        cs = slice(half * HEAD_PAD, (half + 1) * HEAD_PAD)
                        p_s[h, rs, cs] = jnp.exp2(s[:, cs] - m_new).astype(BF16)

            for h in range(HEADS):
                softmax_head(h)
            for h in range(HEADS):
                hs = slice(h * HEAD_PAD, (h + 1) * HEAD_PAD)
                acc_s[:, hs] = a_s[h] * acc_s[:, hs] + _dot(p_s[h], v_ref[rows, hs])

        def full_block(kb, carry):
            block(kb, False)
            return carry

        lax.fori_loop(0, qi, full_block, 0)
        block(qi, True)
        lane = lax.broadcasted_iota(jnp.int32, (tq, HEAD_PAD), 1)
        lse_all = jnp.zeros((tq, HEAD_PAD), F32)
        for h in range(HEADS):
            hs = slice(h * HEAD_PAD, (h + 1) * HEAD_PAD)
            acc = acc_s[:, hs]
            l = jnp.sum(jnp.where(lane == V_HEAD, acc, 0.0), axis=-1, keepdims=True)
            o_ref[:, hs] = (acc / l).astype(BF16)
            lse_all = jnp.where(lane == h, m_s[h] + jnp.log2(l), lse_all)
        _rows_to_lanes(lse_all, lse_ref)
        _ride_finish(riding, (pl.program_id(0) == n_seq - 1) & (pl.program_id(1) == nq - 1))

    ride_in, ride_out, ride_shapes, ride_scratch, ride_args, alias = _ride_specs(ride, 3, 2)
    out = pl.pallas_call(
        body, name="flash_fwd", grid=(n_seq, nq),
        in_specs=[pl.BlockSpec((tq, MLA_W), lambda b, i: (b * nq + i, 0)),
                  pl.BlockSpec((S, MLA_W), lambda b, i: (b, 0)), pl.BlockSpec((S, MLA_W), lambda b, i: (b, 0))] + ride_in,
        out_specs=[pl.BlockSpec((tq, MLA_W), lambda b, i: (b * nq + i, 0)),
                   pl.BlockSpec((HEADS, 1, 1, tq), lambda b, i: (0, b * nq + i, 0, 0))] + ride_out,
        out_shape=[jax.ShapeDtypeStruct((T, MLA_W), BF16), jax.ShapeDtypeStruct((HEADS, T // tq, 1, tq), F32)] + ride_shapes,
        input_output_aliases=alias,
        scratch_shapes=[pltpu.VMEM((HEADS, tq, HEAD_PAD), F32), pltpu.VMEM((tq, MLA_W), F32), pltpu.VMEM((HEADS, tq, tq), F32),
                        pltpu.VMEM((HEADS, tq, tq), BF16), pltpu.VMEM((HEADS, tq, HEAD_PAD), F32)] + ride_scratch,
        compiler_params=_params(2),
    )(Q, K, V, *ride_args)
    return out[0], out[1], list(out[2:])


def _lane_group():
    return lax.broadcasted_iota(jnp.int32, (1, MIX_W), 1) >> GROUP_SHIFT


def _by_group(a0, a1, a2, a3):
    g = _lane_group()
    return jnp.where(g == 0, a0, jnp.where(g == 1, a1, jnp.where(g == 2, a2, a3)))


def _pool_count(si, tc, rows):
    pos = si * tc + lax.broadcasted_iota(jnp.int32, (rows, MIX_W), 0)
    win = _by_group(*POOL_WINDOWS)
    return jnp.minimum(pos + 1, win).astype(F32)


def _pool_fwd(z, z_prev, si, tc):
    ze = jnp.concatenate([z_prev, z], axis=0)
    s1 = ze + _shift_down(ze, 1)
    s2 = s1 + _shift_down(s1, 2)
    s4 = s2 + _shift_down(s2, 4)
    s8 = s4 + _shift_down(s4, 8)
    win_sum = _by_group(s1, s2, s4, s8)[POOL_HALO:]
    return win_sum / _pool_count(si, tc, tc) - z


def _sgu_weights(w_ref):
    row = lax.broadcasted_iota(jnp.int32, (SGU_BLOCK, SGU_BLOCK), 0)
    col = lax.broadcasted_iota(jnp.int32, (SGU_BLOCK, SGU_BLOCK), 1)
    keep = (row >> CHUNK_SHIFT) >= (col >> CHUNK_SHIFT)
    return keep, [jnp.where(keep, w_ref[g], 0.0).astype(BF16) for g in range(GROUPS)]


def _sgu_mix(vn_blk, wm, bias):
    g = _lane_group()
    mixed = bias
    for k in range(GROUPS):
        mixed = mixed + jnp.where(g == k, _dot(wm[k], vn_blk), 0.0)
    return mixed


def _conv_fwd(z, z_prev, w_ref):
    ze = jnp.concatenate([z_prev, z], axis=0)
    y = w_ref[0:1, :] * _shift_down(ze, 2) + w_ref[1:2, :] * _shift_down(ze, 1) + w_ref[2:3, :] * ze
    return y[CONV_HALO:]


def _mix_specs(T, D, tc):
    base = (4 * D + SEC_MLA) // MIX_W
    cur = lambda k: pl.BlockSpec((tc, MIX_W), lambda i: (i, base + k))
    prev = lambda k, halo: pl.BlockSpec((halo, MIX_W), lambda i: (jnp.maximum(i * (tc // halo) - 1, 0), base + k))
    nxt = lambda k, halo: pl.BlockSpec((halo, MIX_W), lambda i: (jnp.minimum((i + 1) * (tc // halo), T // halo - 1), base + k))
    return cur, prev, nxt


def _mixers_fwd(P, D, S, pool_bd, pool_scale, g_v, sgu_w, sgu_bias, conv_w, ride=None):
    T = P.shape[0]
    tc = _tile(S, 512, SGU_BLOCK)
    n_si = S // tc
    cur, prev, _ = _mix_specs(T, D, tc)

    def body(z_ref, zp_ref, u_ref, v_ref, b_ref, c_ref, x_ref, cp_ref, xp_ref,
             pw_ref, ps_ref, gv_ref, sw_ref, sb_ref, cw_ref, *rest):
        (ob_ref, oc_ref, od_ref), _, riding = _ride_split(ride, rest, 3, 0)
        _ride_start(riding, pl.program_id(0) == 0)
        si = pl.program_id(0) % n_si
        first = si == 0
        z = z_ref[...].astype(F32)
        pooled = _pool_fwd(z, jnp.where(first, 0.0, zp_ref[...].astype(F32)), si, tc)
        ob_ref[...] = (_dot(pooled.astype(BF16), pw_ref[...]) * ps_ref[...]).astype(BF16)

        v = v_ref[...].astype(F32)
        vn = (v * _rms_r(v) * gv_ref[...]).astype(BF16)
        _, wm = _sgu_weights(sw_ref)
        for blk in range(tc // SGU_BLOCK):
            rows = slice(blk * SGU_BLOCK, (blk + 1) * SGU_BLOCK)
            oc_ref[rows, :] = (u_ref[rows, :].astype(F32) * _sgu_mix(vn[rows], wm, sb_ref[...])).astype(BF16)

        zc = c_ref[...].astype(F32) * x_ref[...].astype(F32)
        zc_prev = jnp.where(first, 0.0, cp_ref[...].astype(F32) * xp_ref[...].astype(F32))
        od_ref[...] = (b_ref[...].astype(F32) * _conv_fwd(zc, zc_prev, cw_ref)).astype(BF16)
        _ride_finish(riding, pl.program_id(0) == T // tc - 1)

    out = pl.BlockSpec((tc, MIX_W), lambda i: (i, 0))
    ride_in, ride_out, ride_shapes, ride_scratch, ride_args, alias = _ride_specs(ride, 15, 3)
    res = pl.pallas_call(
        body, name="mixers_fwd", grid=(T // tc,),
        in_specs=[cur(0), prev(0, POOL_HALO), cur(1), cur(2), cur(3), cur(4), cur(5), prev(4, CONV_HALO), prev(5, CONV_HALO),
                  _full((MIX_W, MIX_W)), _full((1, MIX_W)), _full((1, MIX_W)), _full((GROUPS, SGU_BLOCK, SGU_BLOCK)),
                  _full((SGU_BLOCK, MIX_W)), _full((CONV_HALO, MIX_W))] + ride_in,
        out_specs=[out, out, out] + ride_out,
        out_shape=[jax.ShapeDtypeStruct((T, MIX_W), BF16)] * 3 + ride_shapes,
        input_output_aliases=alias,
        scratch_shapes=ride_scratch,
        compiler_params=_params(1),
    )(P, P, P, P, P, P, P, P, P, pool_bd, pool_scale, g_v, sgu_w, sgu_bias, conv_w, *ride_args)
    return res[0], res[1], res[2], list(res[3:])


def _merge_fwd(x, P, A, Bm, C, Dv, wa, wb, wc, wd, wout, g_post):
    T, D = x.shape
    tm = _tile(T, 512)

    def body(x_ref, lg_ref, a_ref, b_ref, c_ref, d_ref, wa_ref, wb_ref, wc_ref, wd_ref, wo_ref, g_ref,
             x1_ref, mg_ref, o_ref):
        merged = jnp.zeros((tm, D), F32)
        for k, (br, w) in enumerate(((a_ref, wa_ref), (b_ref, wb_ref), (c_ref, wc_ref), (d_ref, wd_ref))):
            merged = merged + _sigmoid(lg_ref[:, k * D:(k + 1) * D].astype(F32)) * _dot(br[...], w[...])
        mg = merged.astype(BF16)
        mg_ref[...] = mg
        o = _dot(mg, wo_ref[...])
        o_ref[...] = o
        x1_ref[...] = x_ref[...] + o * _rms_r(o) * g_ref[...]

    row = lambda w: pl.BlockSpec((tm, w), lambda i: (i, 0))
    return pl.pallas_call(
        body, name="merge_fwd", grid=(T // tm,),
        in_specs=[row(D), row(4 * D), row(MLA_W), row(MIX_W), row(MIX_W), row(MIX_W),
                  _full((MLA_W, D)), _full((MIX_W, D)), _full((MIX_W, D)), _full((MIX_W, D)), _full((D, D)), _full((1, D))],
        out_specs=[row(D), row(D), row(D)],
        out_shape=[jax.ShapeDtypeStruct((T, D), F32), jax.ShapeDtypeStruct((T, D), BF16), jax.ShapeDtypeStruct((T, D), F32)],
        compiler_params=_params(1),
    )(x, P, A, Bm, C, Dv, wa, wb, wc, wd, wout, g_post)


def _ffn_specs(T, D, Fc, l, rows=512):
    tm = _tile(T, rows)
    row = pl.BlockSpec((tm, D), lambda i, j: (i, 0))
    col = pl.BlockSpec((None, tm, Fc), lambda i, j: (j, i, 0))
    w_in = pl.BlockSpec((None, None, D, Fc), lambda i, j: (l, j, 0, 0))
    w_out = pl.BlockSpec((None, None, Fc, D), lambda i, j: (l, j, 0, 0))
    return tm, row, col, w_in, w_out


def _ffn_fwd(x1, g_pre, wg, wu, wdn, g_post, l):
    T, D = x1.shape
    nf, Fc = wg.shape[1], wg.shape[3]
    tm, row, col, w_in, w_out = _ffn_specs(T, D, Fc, l, 1024)

    def body(x_ref, gp_ref, wg_ref, wu_ref, wd_ref, gq_ref, x2_ref, h_ref, gt_ref, up_ref, f_ref, gt_s, up_s, a_s):
        j = pl.program_id(1)

        @pl.when(j == 0)
        def _():
            for r in range(0, tm, NORM_ROWS):
                rs = slice(r, r + NORM_ROWS)
                xv = x_ref[rs, :]
                h_ref[rs, :] = (xv * _rms_r(xv) * gp_ref[...]).astype(BF16)
            f_ref[...] = jnp.zeros_like(f_ref)

        gt_s[...] = _dot(h_ref[...], wg_ref[...])
        up_s[...] = _dot(h_ref[...], wu_ref[...])
        for r in range(0, tm, ACT_ROWS):
            rs = slice(r, r + ACT_ROWS)
            gt, up = gt_s[rs, :], up_s[rs, :]
            gt_ref[rs, :] = gt.astype(BF16)
            up_ref[rs, :] = up.astype(BF16)
            a_s[rs, :] = (gt * _sigmoid(gt) * up).astype(BF16)
        f_ref[...] += _dot(a_s[...], wd_ref[...])

        @pl.when(j == nf - 1)
        def _():
            for r in range(0, tm, NORM_ROWS):
                rs = slice(r, r + NORM_ROWS)
                f = f_ref[rs, :]
                x2_ref[rs, :] = x_ref[rs, :] + f * _rms_r(f) * gq_ref[...]

    return pl.pallas_call(
        body, name="ffn_fwd", grid=(T // tm, nf),
        in_specs=[row, _full((1, D)), w_in, w_in, w_out, _full((1, D))],
        out_specs=[row, row, col, col, row],
        out_shape=[jax.ShapeDtypeStruct((T, D), F32), jax.ShapeDtypeStruct((T, D), BF16),
                   jax.ShapeDtypeStruct((nf, T, Fc), BF16), jax.ShapeDtypeStruct((nf, T, Fc), BF16),
                   jax.ShapeDtypeStruct((T, D), F32)],
        scratch_shapes=[pltpu.VMEM((tm, Fc), F32), pltpu.VMEM((tm, Fc), F32), pltpu.VMEM((tm, Fc), BF16)],
        compiler_params=_params(2),
    )(x1, g_pre, wg, wu, wdn, g_post)


def _loss_grad(y, target):
    T, D = y.shape
    tm = _tile(T, 512)

    def body(y_ref, t_ref, l_ref, dy_ref):
        @pl.when(pl.program_id(0) == 0)
        def _():
            l_ref[...] = jnp.zeros_like(l_ref)

        d = y_ref[...] - t_ref[...]
        dy_ref[...] = d * (1.0 / D)
        e = jnp.sum((d * d).reshape(tm // 8, 8, D), axis=0)
        part = e[:, 0:128]
        for k in range(1, D // 128):
            part = part + e[:, k * 128:(k + 1) * 128]
        l_ref[...] += part * (0.5 / D)

    row = pl.BlockSpec((tm, D), lambda i: (i, 0))
    return pl.pallas_call(
        body, name="loss_grad", grid=(T // tm,),
        in_specs=[row, row], out_specs=[_full((8, 128)), row],
        out_shape=[jax.ShapeDtypeStruct((8, 128), F32), jax.ShapeDtypeStruct((T, D), F32)],
        compiler_params=_params(1),
    )(y, target)


def _matmul_tn(a, b, name):
    T, M = a.shape
    N = b.shape[1]
    tm, tn, tk = _tile(M, 1024, 128), _tile(N, 1536, 128), _tile(T, 1024)

    def body(a_ref, b_ref, o_ref):
        @pl.when(pl.program_id(2) == 0)
        def _():
            o_ref[...] = jnp.zeros_like(o_ref)

        o_ref[...] += _dot_tn(a_ref[...], b_ref[...])

    return pl.pallas_call(
        body, name=name, grid=(M // tm, N // tn, T // tk),
        in_specs=[pl.BlockSpec((tk, tm), lambda i, j, k: (k, i)), pl.BlockSpec((tk, tn), lambda i, j, k: (k, j))],
        out_specs=pl.BlockSpec((tm, tn), lambda i, j, k: (i, j)),
        out_shape=jax.ShapeDtypeStruct((M, N), F32),
        compiler_params=_params(3),
    )(a, b)


def _norm_bwd(dy, x, g, name, add=None, parts=None):
    T, D = x.shape
    tm = _tile(T, 512)
    n_parts = 0 if parts is None else parts.shape[0]

    def body(*refs):
        dx_ref, dg_ref = refs[-2:]
        ins = list(refs[:-2])
        dy_ref = ins.pop(0) if parts is None else None
        p_ref = ins.pop(0) if parts is not None else None
        x_ref, g_ref = ins[0], ins[1]
        add_ref = ins[2] if add is not None else None

        @pl.when(pl.program_id(0) == 0)
        def _():
            dg_ref[...] = jnp.zeros_like(dg_ref)

        dg_sum = jnp.zeros((1, D), F32)
        for r in range(0, tm, NORM_ROWS):
            rs = slice(r, r + NORM_ROWS)
            if parts is None:
                dy = dy_ref[rs, :]
            else:
                dy = p_ref[0, rs, :].astype(F32)
                for k in range(1, n_parts):
                    dy = dy + p_ref[k, rs, :].astype(F32)
            dx, dg = _rms_bwd(dy, x_ref[rs, :], g_ref[...])
            dx_ref[rs, :] = (dx if add is None else add_ref[rs, :] + dx).astype(dx_ref.dtype)
            dg_sum = dg_sum + dg
        dg_ref[...] += dg_sum

    row = pl.BlockSpec((tm, D), lambda i: (i, 0))
    first = [row] if parts is None else [pl.BlockSpec((n_parts, tm, D), lambda i: (0, i, 0))]
    operands = [dy if parts is None else parts, x, g] + ([] if add is None else [add])
    return pl.pallas_call(
        body, name=name, grid=(T // tm,),
        in_specs=first + [row, _full((1, D))] + ([] if add is None else [row]),
        out_specs=[row, _full((1, D))],
        out_shape=[jax.ShapeDtypeStruct((T, D), BF16 if add is None else F32), jax.ShapeDtypeStruct((1, D), F32)],
        compiler_params=_params(1),
    )(*operands)


FFN_WEIGHTS = ("w_ffn_gate", "w_ffn_up", "w_ffn_down")


def _ffn_bwd_main(df, gt, up, h2, wg, wu, wdn, l, bufs, ride=None):
    T, D = h2.shape
    nf, Fc = wg.shape[1], wg.shape[3]
    tm = _tile(T, 512)
    nt = T // tm
    keep = [] if bufs is None else [bufs[n] for n in FFN_WEIGHTS]

    def body(df_ref, gt_ref, up_ref, h_ref, wg_ref, wu_ref, wd_ref, *rest):
        (dh_ref, gg_ref, gu_ref, gd_ref), (da_s, dgt_s, dup_s, act_s), riding = _ride_split(ride, rest[len(keep):], 4, 4)
        j, i = pl.program_id(0), pl.program_id(1)
        _ride_start(riding, (j == 0) & (i == 0))

        @pl.when(i == 0)
        def _():
            for r in (gg_ref, gu_ref, gd_ref):
                r[...] = jnp.zeros_like(r)

        parts = [slice(p * tm // ROW_PARTS, (p + 1) * tm // ROW_PARTS) for p in range(ROW_PARTS)]
        for ps in parts:
            da_s[ps, :] = _dot_nt(df_ref[ps, :], wd_ref[...])
        for ps in parts:
            for r in range(ps.start, ps.stop, ACT_ROWS):
                rs = slice(r, r + ACT_ROWS)
                da = da_s[rs, :]
                gt = gt_ref[rs, :].astype(F32)
                u = up_ref[rs, :].astype(F32)
                sig = _sigmoid(gt)
                silu = gt * sig
                dgt_s[rs, :] = (da * u * (sig * (1.0 + gt * (1.0 - sig)))).astype(BF16)
                dup_s[rs, :] = (da * silu).astype(BF16)
                act_s[rs, :] = (silu * u).astype(BF16)
            dh_ref[ps, :] = (_dot_nt(dgt_s[ps, :], wg_ref[...]) + _dot_nt(dup_s[ps, :], wu_ref[...])).astype(BF16)
        gg_ref[...] += _dot_tn(h_ref[...], dgt_s[...])
        gu_ref[...] += _dot_tn(h_ref[...], dup_s[...])
        gd_ref[...] += _dot_tn(act_s[...], df_ref[...])
        _ride_finish(riding, (j == nf - 1) & (i == nt - 1))

    row = pl.BlockSpec((tm, D), lambda j, i: (i, 0))
    col = pl.BlockSpec((None, tm, Fc), lambda j, i: (j, i, 0))
    w_in = pl.BlockSpec((None, None, D, Fc), lambda j, i: (l, j, 0, 0))
    w_out = pl.BlockSpec((None, None, Fc, D), lambda j, i: (l, j, 0, 0))
    g_in = pl.BlockSpec((None, None, D, Fc), lambda j, i: (j, l, 0, 0))
    g_out = pl.BlockSpec((None, None, Fc, D), lambda j, i: (j, l, 0, 0))
    n_in = 7
    ride_in, ride_out, ride_shapes, ride_scratch, ride_args, alias = _ride_specs(ride, n_in + len(keep), 4)
    out = pl.pallas_call(
        body, name="ffn_bwd_main", grid=(nf, nt),
        in_specs=[row, col, col, row, w_in, w_in, w_out] + [pl.BlockSpec(memory_space=pl.ANY)] * len(keep) + ride_in,
        out_specs=[pl.BlockSpec((None, tm, D), lambda j, i: (j, i, 0)), g_in, g_in, g_out] + ride_out,
        out_shape=[jax.ShapeDtypeStruct((nf, T, D), BF16), jax.ShapeDtypeStruct((nf, DEPTH, D, Fc), F32),
                   jax.ShapeDtypeStruct((nf, DEPTH, D, Fc), F32), jax.ShapeDtypeStruct((nf, DEPTH, Fc, D), F32)] + ride_shapes,
        input_output_aliases={**{n_in + k: 1 + k for k in range(len(keep))}, **alias},
        scratch_shapes=[pltpu.VMEM((tm, Fc), F32), pltpu.VMEM((tm, Fc), BF16), pltpu.VMEM((tm, Fc), BF16),
                        pltpu.VMEM((tm, Fc), BF16)] + ride_scratch,
        compiler_params=_params(2),
    )(df, gt, up, h2, wg, wu, wdn, *keep, *ride_args)
    return out[0], dict(zip(FFN_WEIGHTS, out[1:4])), list(out[4:])


def _wgrad_out(l, n_layers, rows, cols):
    spec = pl.BlockSpec((N_CHIPS, None, rows, cols), lambda *_: (0, l, 0, 0))
    return spec, jax.ShapeDtypeStruct((N_CHIPS, n_layers, rows, cols), F32)


def _merge_bwd(dx1, o, merged, P, A, Bm, C, Dv, wa, wb, wc, wd, wout, g_post, l, bufs, ride=None):
    T, D = o.shape
    tm = _tile(T, 256)
    Dc = D // N_CHIPS
    names = ("w_out", "w_br_a", "w_br_b", "w_br_c", "w_br_d")
    keep = [] if bufs is None else [bufs[n] for n in names]

    def body(dx1_ref, o_ref, mg_ref, lg_ref, a_ref, b_ref, c_ref, d_ref, wa_ref, wb_ref, wc_ref, wd_ref, wo_ref, g_ref, *rest):
        outs, _, riding = _ride_split(ride, rest[len(keep):], 12, 0)
        dlg_ref, da_ref, db_ref, dc_ref, dd_ref, dg_ref, dt_ref, go_ref, ga_ref, gb_ref, gc_ref, gd_ref = outs
        _ride_start(riding, pl.program_id(0) == 0)

        @pl.when(pl.program_id(0) == 0)
        def _():
            for r in (dg_ref, go_ref, ga_ref, gb_ref, gc_ref, gd_ref):
                r[...] = jnp.zeros_like(r)

        d_o, dg = _rms_bwd(dx1_ref[...], o_ref[...], g_ref[...])
        dg_ref[...] += dg
        d_o = d_o.astype(BF16)
        for k in range(N_CHIPS):
            go_ref[k] += _dot_tn(mg_ref[:, k * Dc:(k + 1) * Dc], d_o)
        dm = _dot_nt(d_o, wo_ref[...])
        branches = ((a_ref, wa_ref, da_ref, ga_ref), (b_ref, wb_ref, db_ref, gb_ref),
                    (c_ref, wc_ref, dc_ref, gc_ref), (d_ref, wd_ref, dd_ref, gd_ref))
        for j, (br, w, dbr_ref, gw_ref) in enumerate(branches):
            gate = _sigmoid(lg_ref[:, j * D:(j + 1) * D].astype(F32))
            y = _dot(br[...], w[...])
            dlg_ref[:, j * D:(j + 1) * D] = (dm * y * gate * (1.0 - gate)).astype(BF16)
            dy = (dm * gate).astype(BF16)
            d_in = _dot_nt(dy, w[...]).astype(dbr_ref.dtype)
            dbr_ref[...] = d_in
            if j == 0:
                prod = d_in.astype(F32) * br[...].astype(F32)
                lane = lax.broadcasted_iota(jnp.int32, (tm, HEAD_PAD), 1)
                delta = jnp.zeros((tm, HEAD_PAD), F32)
                for h in range(HEADS):
                    row_sum = jnp.sum(prod[:, h * HEAD_PAD:(h + 1) * HEAD_PAD], axis=-1, keepdims=True)
                    delta = jnp.where(lane == h, row_sum, delta)
                _rows_to_lanes(delta, dt_ref)
            for k in range(N_CHIPS):
                gw_ref[k] += _dot_tn(br[...], dy[:, k * Dc:(k + 1) * Dc])

        _ride_finish(riding, pl.program_id(0) == T // tm - 1)

    row = lambda w: pl.BlockSpec((tm, w), lambda i: (i, 0))
    wg = [_wgrad_out(l, DEPTH, r, c) for r, c in ((Dc, D), (MLA_W, Dc), (MIX_W, Dc), (MIX_W, Dc), (MIX_W, Dc))]
    n_in = 14
    ride_in, ride_out, ride_shapes, ride_scratch, ride_args, alias = _ride_specs(ride, n_in + len(keep), 12)
    stat = pl.BlockSpec((HEADS, 1, 1, tm), lambda i: (0, i, 0, 0))
    out = pl.pallas_call(
        body, name="merge_bwd", grid=(T // tm,),
        in_specs=[row(D), row(D), row(D), row(4 * D), row(MLA_W), row(MIX_W), row(MIX_W), row(MIX_W),
                  _full((MLA_W, D)), _full((MIX_W, D)), _full((MIX_W, D)), _full((MIX_W, D)), _full((D, D)), _full((1, D))]
        + [pl.BlockSpec(memory_space=pl.ANY)] * len(keep) + ride_in,
        out_specs=[row(4 * D), row(MLA_W), row(MIX_W), row(MIX_W), row(MIX_W), _full((1, D)), stat] + [s for s, _ in wg]
        + ride_out,
        out_shape=[jax.ShapeDtypeStruct((T, 4 * D), BF16), jax.ShapeDtypeStruct((T, MLA_W), BF16)]
        + [jax.ShapeDtypeStruct((T, MIX_W), F32)] * 3 + [jax.ShapeDtypeStruct((1, D), F32),
                                                         jax.ShapeDtypeStruct((HEADS, T // tm, 1, tm), F32)]
        + [s for _, s in wg] + ride_shapes,
        input_output_aliases={**{n_in + i: 7 + i for i in range(len(keep))}, **alias},
        scratch_shapes=ride_scratch,
        compiler_params=_params(1),
    )(dx1, o, merged, P, A, Bm, C, Dv, wa, wb, wc, wd, wout, g_post, *keep, *ride_args)
    return out[:7], dict(zip(names, out[7:12])), list(out[12:])


def _mixers_bwd(P, D, S, dBm, dC, dDv, pool_bd, pool_scale, g_v, sgu_w, sgu_bias, conv_w):
    T = P.shape[0]
    tc = _tile(S, 512, SGU_BLOCK)
    n_si = S // tc
    cur, prev, nxt = _mix_specs(T, D, tc)
    n_blk = tc // SGU_BLOCK

    def body(z_ref, zp_ref, u_ref, v_ref, b_ref, c_ref, x_ref, cp_ref, xp_ref, bn_ref,
             dbm_ref, dbmn_ref, dc_ref, ddv_ref, ddvn_ref,
             pw_ref, ps_ref, gv_ref, sw_ref, sb_ref, cw_ref,
             dp_ref, dpw_ref, dps_ref, dgv_ref, dsw_ref, dsb_ref, dcw_ref, dvn_acc):
        si = pl.program_id(0) % n_si
        first, last = si == 0, si == n_si - 1

        @pl.when(pl.program_id(0) == 0)
        def _():
            for r in (dpw_ref, dps_ref, dgv_ref, dsw_ref, dsb_ref, dcw_ref):
                r[...] = jnp.zeros_like(r)

        z = z_ref[...].astype(F32)
        pooled = _pool_fwd(z, jnp.where(first, 0.0, zp_ref[...].astype(F32)), si, tc).astype(BF16)
        dbm = dbm_ref[...]
        dps_ref[...] += jnp.sum(dbm * _dot(pooled, pw_ref[...]), axis=0, keepdims=True)
        dmix = (jnp.concatenate([dbm, jnp.where(last, 0.0, dbmn_ref[...])], axis=0) * ps_ref[...]).astype(BF16)
        dpw_ref[...] += _dot_tn(pooled, dmix[:tc])
        dpool = _dot_nt(dmix, pw_ref[...])
        e = dpool / _pool_count(si, tc, tc + POOL_HALO)
        f1 = e + _shift_up(e, 1)
        f2 = f1 + _shift_up(f1, 2)
        f4 = f2 + _shift_up(f2, 4)
        f8 = f4 + _shift_up(f4, 8)
        dp_ref[:, 0:MIX_W] = (_by_group(f1, f2, f4, f8)[:tc] - dpool[:tc]).astype(BF16)

        v = v_ref[...].astype(F32)
        vn = (v * _rms_r(v) * gv_ref[...]).astype(BF16)
        keep, wm = _sgu_weights(sw_ref)
        g = _lane_group()
        for blk in range(n_blk):
            rows = slice(blk * SGU_BLOCK, (blk + 1) * SGU_BLOCK)
            vb = vn[rows]
            dc = dc_ref[rows, :]
            dp_ref[rows, MIX_W:2 * MIX_W] = (dc * _sgu_mix(vb, wm, sb_ref[...])).astype(BF16)
            dmx = dc * u_ref[rows, :].astype(F32)
            dsb_ref[...] += dmx
            dvn = jnp.zeros((SGU_BLOCK, MIX_W), F32)
            for k in range(GROUPS):
                dmk = jnp.where(g == k, dmx, 0.0).astype(BF16)
                dsw_ref[k] += jnp.where(keep, _dot_nt(dmk, vb), 0.0)
                dvn = dvn + _dot_tn(wm[k], dmk)
            dvn_acc[rows, :] = dvn
        dv, dg = _rms_bwd(dvn_acc[...], v, gv_ref[...])
        dgv_ref[...] += dg
        dp_ref[:, 2 * MIX_W:3 * MIX_W] = dv.astype(BF16)

        cg, xg, bg = c_ref[...].astype(F32), x_ref[...].astype(F32), b_ref[...].astype(F32)
        zc = cg * xg
        ze = jnp.concatenate([jnp.where(first, 0.0, cp_ref[...].astype(F32) * xp_ref[...].astype(F32)), zc], axis=0)
        z1, z2 = _shift_down(ze, 1)[CONV_HALO:], _shift_down(ze, 2)[CONV_HALO:]
        ddv = ddv_ref[...]
        y = cw_ref[0:1, :] * z2 + cw_ref[1:2, :] * z1 + cw_ref[2:3, :] * zc
        dp_ref[:, 3 * MIX_W:4 * MIX_W] = (ddv * y).astype(BF16)
        dy = ddv * bg
        dcw_ref[0:1, :] += jnp.sum(dy * z2, axis=0, keepdims=True)
        dcw_ref[1:2, :] += jnp.sum(dy * z1, axis=0, keepdims=True)
        dcw_ref[2:3, :] += jnp.sum(dy * zc, axis=0, keepdims=True)
        dye = jnp.concatenate([dy, jnp.where(last, 0.0, ddvn_ref[...] * bn_ref[...].astype(F32))], axis=0)
        dz = (cw_ref[2:3, :] * dye + cw_ref[1:2, :] * _shift_up(dye, 1) + cw_ref[0:1, :] * _shift_up(dye, 2))[:tc]
        dp_ref[:, 4 * MIX_W:5 * MIX_W] = (dz * xg).astype(BF16)
        dp_ref[:, 5 * MIX_W:6 * MIX_W] = (dz * cg).astype(BF16)

    grad = lambda halo: pl.BlockSpec((halo, MIX_W), lambda i: (jnp.minimum((i + 1) * (tc // halo), T // halo - 1), 0))
    out = pl.BlockSpec((tc, MIX_W), lambda i: (i, 0))
    return pl.pallas_call(
        body, name="mixers_bwd", grid=(T // tc,),
        in_specs=[cur(0), prev(0, POOL_HALO), cur(1), cur(2), cur(3), cur(4), cur(5), prev(4, CONV_HALO), prev(5, CONV_HALO),
                  nxt(3, CONV_HALO), out, grad(POOL_HALO), out, out, grad(CONV_HALO),
                  _full((MIX_W, MIX_W)), _full((1, MIX_W)), _full((1, MIX_W)), _full((GROUPS, SGU_BLOCK, SGU_BLOCK)),
                  _full((SGU_BLOCK, MIX_W)), _full((CONV_HALO, MIX_W))],
        out_specs=[pl.BlockSpec((tc, SEC_MIX), lambda i: (i, 0)), _full((MIX_W, MIX_W)), _full((1, MIX_W)), _full((1, MIX_W)),
                   _full((GROUPS, SGU_BLOCK, SGU_BLOCK)), _full((SGU_BLOCK, MIX_W)), _full((CONV_HALO, MIX_W))],
        out_shape=[jax.ShapeDtypeStruct((T, SEC_MIX), BF16), jax.ShapeDtypeStruct((MIX_W, MIX_W), F32),
                   jax.ShapeDtypeStruct((1, MIX_W), F32), jax.ShapeDtypeStruct((1, MIX_W), F32),
                   jax.ShapeDtypeStruct((GROUPS, SGU_BLOCK, SGU_BLOCK), F32), jax.ShapeDtypeStruct((SGU_BLOCK, MIX_W), F32),
                   jax.ShapeDtypeStruct((CONV_HALO, MIX_W), F32)],
        scratch_shapes=[pltpu.VMEM((tc, MIX_W), F32)],
        compiler_params=_params(1),
    )(P, P, P, P, P, P, P, P, P, P, dBm, dBm, dC, dDv, dDv, pool_bd, pool_scale, g_v, sgu_w, sgu_bias, conv_w)


def _attn_tile(S):
    return _tile(S, 256, 128)


def _flash_bwd(Q, K, V, dO, lse_t, delta_t, S, ride=None):
    T = Q.shape[0]
    n_seq = T // S
    tq = _attn_tile(S)
    nq = S // tq

    def body(k_ref, v_ref, q_ref, do_ref, lse_ref, dl_ref, *rest):
        (dq_ref, dk_ref, dv_ref), (s_s, dp_s, p_s, ds_s), riding = _ride_split(ride, rest, 3, 4)
        kb = pl.program_id(1)
        _ride_start(riding, (pl.program_id(0) == 0) & (kb == 0))

        @pl.when(kb == 0)
        def _():
            dq_ref[...] = jnp.zeros_like(dq_ref)

        dk_ref[...] = jnp.zeros_like(dk_ref)
        dv_ref[...] = jnp.zeros_like(dv_ref)

        def block(qi, masked):
            rows = pl.ds(pl.multiple_of(qi * tq, tq), tq)
            for h in range(HEADS):
                hs = slice(h * HEAD_PAD, (h + 1) * HEAD_PAD)
                s_s[h] = _dot_nt(k_ref[:, hs], q_ref[rows, hs])
                dp_s[h] = _dot_nt(v_ref[:, hs], do_ref[rows, hs])
            for h in range(HEADS):
                lse_row, dl_row = lse_ref[h, qi], dl_ref[h, qi]
                for r in range(0, tq, SOFTMAX_ROWS):
                    rs = slice(r, r + SOFTMAX_ROWS)
                    s = s_s[h, rs, :]
                    if masked:
                        key = r + lax.broadcasted_iota(jnp.int32, (SOFTMAX_ROWS, tq), 0)
                        query = lax.broadcasted_iota(jnp.int32, (SOFTMAX_ROWS, tq), 1)
                        s = jnp.where((query >> CHUNK_SHIFT) >= (key >> CHUNK_SHIFT), s, NEG_INF)
                    p = jnp.exp2(s - lse_row)
                    p_s[h, rs, :] = p.astype(BF16)
                    ds_s[h, rs, :] = (p * (dp_s[h, rs, :] - dl_row)).astype(BF16)
            for h in range(HEADS):
                hs = slice(h * HEAD_PAD, (h + 1) * HEAD_PAD)
                dv_ref[:, hs] += _dot(p_s[h], do_ref[rows, hs])
                dk_ref[:, hs] += _dot(ds_s[h], q_ref[rows, hs])
                dq_ref[rows, hs] += _dot_tn(ds_s[h], k_ref[:, hs])

        def full_block(qi, carry):
            block(qi, False)
            return carry

        block(kb, True)
        lax.fori_loop(kb + 1, nq, full_block, 0)
        dk_ref[...] = dk_ref[...] * LN2
        _ride_finish(riding, (pl.program_id(0) == n_seq - 1) & (kb == nq - 1))

    tile = pl.BlockSpec((tq, MLA_W), lambda b, i: (b * nq + i, 0))
    seq = pl.BlockSpec((S, MLA_W), lambda b, i: (b, 0))
    stat = pl.BlockSpec((HEADS, nq, 1, tq), lambda b, i: (0, b, 0, 0))
    ride_in, ride_out, ride_shapes, ride_scratch, ride_args, alias = _ride_specs(ride, 6, 3)
    out = pl.pallas_call(
        body, name="flash_bwd", grid=(n_seq, nq),
        in_specs=[tile, tile, seq, seq, stat, stat] + ride_in,
        out_specs=[seq, tile, tile] + ride_out,
        out_shape=[jax.ShapeDtypeStruct((T, MLA_W), F32)] * 3 + ride_shapes,
        input_output_aliases=alias,
        scratch_shapes=[pltpu.VMEM((HEADS, tq, tq), F32), pltpu.VMEM((HEADS, tq, tq), F32),
                        pltpu.VMEM((HEADS, tq, tq), BF16), pltpu.VMEM((HEADS, tq, tq), BF16)] + ride_scratch,
        compiler_params=_params(2),
    )(K, V, Q, dO, lse_t, delta_t, *ride_args)
    return out[0], out[1], out[2], list(out[3:])


def _mla_bwd_post(P, D, S, dQ, dK, dV, hq, hkv, g_cq, g_ckv, wuq, wukv, rope_c, rope_sa, rope_sb, l, bufs):
    T = P.shape[0]
    tm = _tile(S, 512)
    n_si = S // tm
    base = 4 * D
    names = ("w_uq", "w_ukv")

    def body(cq_ref, ckv_ref, dq_ref, dk_ref, dv_ref, hq_ref, hkv_ref, gq_ref, gkv_ref, wq_ref, wkv_ref, c_ref, sa_ref, sb_ref,
             *rest):
        dp_ref, dgq_ref, dgkv_ref, guq_ref, gukv_ref = rest[-5:]

        @pl.when(pl.program_id(0) == 0)
        def _():
            for r in (dgq_ref, dgkv_ref, guq_ref, gukv_ref):
                r[...] = jnp.zeros_like(r)

        c, sa, sb = c_ref[...], sa_ref[...], sb_ref[...]
        dq = _rope_t(dq_ref[...] * ATTN_SCALE, jnp.tile(c, (1, HEADS)), jnp.tile(sa, (1, HEADS)),
                     jnp.tile(sb, (1, HEADS))).astype(BF16)
        dcq, dg = _rms_bwd(_dot_nt(dq, wq_ref[...]), cq_ref[...].astype(F32), gq_ref[...])
        dgq_ref[...] += dg
        dp_ref[:, 0:Q_LORA] = dcq.astype(BF16)

        dk = dk_ref[...]
        dkb, dvb = dk.astype(BF16), dv_ref[...].astype(BF16)
        dkv = jnp.concatenate([p[:, j * CHIP_HEADS_W:(j + 1) * CHIP_HEADS_W] for j in range(N_CHIPS) for p in (dkb, dvb)], axis=1)
        for k in range(N_CHIPS):
            guq_ref[k] += _dot_tn(hq_ref[...], dq[:, k * CHIP_HEADS_W:(k + 1) * CHIP_HEADS_W])
            gukv_ref[k] += _dot_tn(hkv_ref[...], dkv[:, k * CHIP_KV:(k + 1) * CHIP_KV])
        dckv, dg = _rms_bwd(_dot_nt(dkv, wkv_ref[...]), ckv_ref[...].astype(F32), gkv_ref[...])
        dgkv_ref[...] += dg
        dp_ref[:, Q_LORA:Q_LORA + KV_LORA] = dckv.astype(BF16)

        dkr = dk[:, 0:HEAD_PAD]
        for h in range(1, HEADS):
            dkr = dkr + dk[:, h * HEAD_PAD:(h + 1) * HEAD_PAD]
        lane = lax.broadcasted_iota(jnp.int32, (1, HEAD_PAD), 1)
        rope_lanes = (lane >= QK_NOPE) & (lane < QK_NOPE + QK_ROPE)
        dp_ref[:, Q_LORA + KV_LORA:SEC_MLA] = jnp.where(rope_lanes, _rope_t(dkr, c, sa, sb), 0.0).astype(BF16)

    tab = pl.BlockSpec((tm, HEAD_PAD), lambda i: (i % n_si, 0))
    row = lambda w: pl.BlockSpec((tm, w), lambda i: (i, 0))
    wg = [_wgrad_out(l, DEPTH, Q_LORA, CHIP_HEADS_W), _wgrad_out(l, DEPTH, KV_LORA, CHIP_KV)]
    keep = [] if bufs is None else [bufs[n] for n in names]
    n_in = 14
    out = pl.pallas_call(
        body, name="mla_bwd_post", grid=(T // tm,),
        in_specs=[pl.BlockSpec((tm, Q_LORA), lambda i: (i, base // Q_LORA)),
                  pl.BlockSpec((tm, KV_LORA), lambda i: (i, (base + Q_LORA) // KV_LORA)),
                  row(MLA_W), row(MLA_W), row(MLA_W), row(Q_LORA), row(KV_LORA),
                  _full((1, Q_LORA)), _full((1, KV_LORA)), _full((Q_LORA, MLA_W)), _full((KV_LORA, 2 * MLA_W)), tab, tab, tab]
        + [pl.BlockSpec(memory_space=pl.ANY)] * len(keep),
        out_specs=[row(SEC_MLA), _full((1, Q_LORA)), _full((1, KV_LORA))] + [s for s, _ in wg],
        out_shape=[jax.ShapeDtypeStruct((T, SEC_MLA), BF16), jax.ShapeDtypeStruct((1, Q_LORA), F32),
                   jax.ShapeDtypeStruct((1, KV_LORA), F32)] + [s for _, s in wg],
        input_output_aliases={n_in + i: 3 + i for i in range(len(keep))},
        compiler_params=_params(1),
    )(P, P, dQ, dK, dV, hq, hkv, g_cq, g_ckv, wuq, wukv, rope_c, rope_sa, rope_sb, *keep)
    return out[:3], dict(zip(names, out[3:]))


def _proj_bwd(dx1, x, g, dPg, dPa, dPm, w_gates, w_mla, w_mix, ride=None):
    T, D = x.shape
    tm = _tile(T, 512)

    def body(dx1_ref, x_ref, g_ref, dg_ref_in, da_ref, dm_ref, wg_ref, wa_ref, wm_ref, *rest):
        (dx_ref, dg_ref), _, riding = _ride_split(ride, rest, 2, 0)
        _ride_start(riding, pl.program_id(0) == 0)

        @pl.when(pl.program_id(0) == 0)
        def _():
            dg_ref[...] = jnp.zeros_like(dg_ref)

        dh = _dot_nt(dg_ref_in[...], wg_ref[...]) + _dot_nt(da_ref[...], wa_ref[...]) + _dot_nt(dm_ref[...], wm_ref[...])
        dx, dg = _rms_bwd(dh, x_ref[...], g_ref[...])
        dx_ref[...] = dx1_ref[...] + dx
        dg_ref[...] += dg
        _ride_finish(riding, pl.program_id(0) == T // tm - 1)

    row = lambda w: pl.BlockSpec((tm, w), lambda i: (i, 0))
    ride_in, ride_out, ride_shapes, ride_scratch, ride_args, alias = _ride_specs(ride, 9, 2)
    out = pl.pallas_call(
        body, name="proj_bwd", grid=(T // tm,),
        in_specs=[row(D), row(D), _full((1, D)), row(4 * D), row(SEC_MLA), row(SEC_MIX),
                  _full((D, 4 * D)), _full((D, SEC_MLA)), _full((D, SEC_MIX))] + ride_in,
        out_specs=[row(D), _full((1, D))] + ride_out,
        out_shape=[jax.ShapeDtypeStruct((T, D), F32), jax.ShapeDtypeStruct((1, D), F32)] + ride_shapes,
        input_output_aliases=alias, scratch_shapes=ride_scratch,
        compiler_params=_params(1),
    )(dx1, x, g, dPg, dPa, dPm, w_gates, w_mla, w_mix, *ride_args)
    return out[0], out[1], list(out[2:])


def _adamw(w, g, m, v, name):
    R, C = w.shape
    tr = _tile(R, max(8, (1 << 19) // C))

    def body(w_ref, g_ref, m_ref, v_ref, d_ref, mo_ref, vo_ref):
        gv = g_ref[...]
        mn = ADAM_B1 * m_ref[...] + (1.0 - ADAM_B1) * gv
        vn = ADAM_B2 * v_ref[...] + (1.0 - ADAM_B2) * (gv * gv)
        mo_ref[...] = mn
        vo_ref[...] = vn
        m_hat = mn / (1.0 - ADAM_B1 ** ADAM_STEP)
        v_hat = vn / (1.0 - ADAM_B2 ** ADAM_STEP)
        d_ref[...] = -ADAM_LR * (m_hat / (jnp.sqrt(v_hat) + ADAM_EPS) + ADAM_WD * w_ref[...])

    blk = pl.BlockSpec((tr, C), lambda i: (i, 0))
    return pl.pallas_call(
        body, name=name, grid=(R // tr,), in_specs=[blk] * 4, out_specs=[blk] * 3,
        out_shape=[jax.ShapeDtypeStruct((R, C), F32)] * 3, compiler_params=_params(1),
    )(w, g, m, v)


def _rows_tile(rows, cols):
    return _tile(rows, max(16, (1 << 19) // cols), 16)


def _add_halves(G, span, recv, half, name):
    n, L, R, C = G.shape
    l0, nl = span[0], span[1] - span[0]
    hr = R // 2
    tr = _rows_tile(hr, C)
    nb = hr // tr

    def body(half_ref, g_ref, r_ref, o_ref):
        o_ref[...] = (g_ref[...] + r_ref[...]).astype(BF16)

    grid_spec = pltpu.PrefetchScalarGridSpec(
        num_scalar_prefetch=1, grid=(n, nl, nb),
        in_specs=[pl.BlockSpec((None, None, tr, C), lambda k, l, i, h: (k, l0 + l, h[0] * nb + i, 0)),
                  pl.BlockSpec((None, None, tr, C), lambda k, l, i, h: (k, l, i, 0))],
        out_specs=pl.BlockSpec((None, None, tr, C), lambda k, l, i, h: (k, l, i, 0)))
    return pl.pallas_call(
        body, name="rs_add_halves_" + name, grid_spec=grid_spec,
        out_shape=jax.ShapeDtypeStruct((n, nl, hr, C), BF16), compiler_params=_params(3),
    )(half.reshape(1).astype(jnp.int32), G, recv)


def _sum_slots(H, slots, place, name, l0, n_layers, prev=None):
    _, nl, hr, C = slots.shape
    tr = _rows_tile(hr, C)
    nb = hr // tr

    def body(x_ref, y_ref, c_ref, own_ref, s1_ref, s2_ref, s3_ref, *rest):
        o_ref = rest[-1]
        o_ref[...] = ((own_ref[...].astype(F32) + s1_ref[...].astype(F32)) + s2_ref[...].astype(F32)) + s3_ref[...].astype(F32)

    def src(fx, fy):
        def index(l, j, px, py, pc):
            cx = px[0] + fx - 2 * fx * px[0]
            cy = py[0] + fy - 2 * fy * py[0]
            return (2 * cx + cy, l, j, 0)
        return pl.BlockSpec((None, None, tr, C), index)

    keep = [] if prev is None else [prev]
    grid_spec = pltpu.PrefetchScalarGridSpec(
        num_scalar_prefetch=3, grid=(nl, nb),
        in_specs=[src(0, 0), src(0, 1), src(1, 0), src(1, 1)] + [pl.BlockSpec(memory_space=pl.ANY)] * len(keep),
        out_specs=pl.BlockSpec((None, tr, C), lambda l, j, px, py, pc: (l0 + l, pc[0] * nb + j, 0)))
    return pl.pallas_call(
        body, name="rs_sum_slots_" + name, grid_spec=grid_spec,
        out_shape=jax.ShapeDtypeStruct((n_layers, 2 * hr, C), F32),
        input_output_aliases={7: 0} if keep else {}, compiler_params=_params(2),
    )(*place, H, slots, slots, slots, *keep)


HBM = pl.BlockSpec(memory_space=pltpu.HBM)


def _place():
    x, y, c = lax.axis_index("x"), lax.axis_index("y"), lax.axis_index("c")
    return x, y, c, 2 * x + y


def _chip_device(chip, c):
    return (chip // 2, chip % 2, c)


def _remote(src, dst, send_sem, recv_sem, to):
    return pltpu.make_async_remote_copy(src_ref=src, dst_ref=dst, send_sem=send_sem, recv_sem=recv_sem, device_id=to,
                                        device_id_type=MESH)


def _place_own(w, chip, name):
    L, R, C = w.shape
    tr = _rows_tile(R, C)

    def body(p_ref, w_ref, o_ref):
        o_ref[...] = w_ref[...].astype(BF16)

    grid_spec = pltpu.PrefetchScalarGridSpec(
        num_scalar_prefetch=1, grid=(L, R // tr), in_specs=[pl.BlockSpec((None, tr, C), lambda l, j, p: (l, j, 0))],
        out_specs=pl.BlockSpec((None, None, tr, C), lambda l, j, p: (l, p[0], j, 0)))
    return pl.pallas_call(
        body, name="place_" + name, grid_spec=grid_spec,
        out_shape=jax.ShapeDtypeStruct((L, N_CHIPS, R, C), BF16), compiler_params=_params(2),
    )(chip.reshape(1).astype(jnp.int32), w)


def _gather_weights(bufs, l):
    n = len(bufs)

    def body(*refs):
        o_refs = refs[n:2 * n]
        send_sems, recv_sems = refs[2 * n:]
        x, y, c, me = _place()
        sibling = (x, y, 1 - c)

        def copy(t, k, chip, half, to):
            hr = o_refs[t].shape[2] // 2
            block = o_refs[t].at[l, chip, pl.ds(half * hr, hr), :]
            return _remote(block, block, send_sems.at[6 * t + k], recv_sems.at[6 * t + k], to)

        first = [copy(t, d - 1, me, c, _chip_device(me ^ d, c)) for t in range(n) for d in (1, 2, 3)]
        for cp in first:
            cp.start()
        passed = []
        for t in range(n):
            for d in (1, 2, 3):
                copy(t, d - 1, me ^ d, c, sibling).wait_recv()
                passed.append(copy(t, 2 + d, me ^ d, c, sibling))
                passed[-1].start()
        for t in range(n):
            for d in (1, 2, 3):
                copy(t, 2 + d, me ^ d, 1 - c, sibling).wait_recv()
        for cp in first + passed:
            cp.wait_send()

    return pl.pallas_call(
        body, name="gather_weights", in_specs=[HBM] * n, out_specs=[HBM] * n,
        out_shape=[jax.ShapeDtypeStruct(b.shape, b.dtype) for b in bufs],
        input_output_aliases={t: t for t in range(n)},
        scratch_shapes=[pltpu.SemaphoreType.DMA((6 * n,)), pltpu.SemaphoreType.DMA((6 * n,))],
    )(*bufs)


def _join_halves(bufs):
    n = len(bufs)

    def body(*refs):
        o_refs = refs[n:2 * n]
        send_sems, recv_sems = refs[2 * n:]
        x, y, c, _ = _place()

        def half(t, which):
            hr = o_refs[t].shape[1] // 2
            return o_refs[t].at[:, pl.ds(which * hr, hr), :]

        sends = [_remote(half(t, c), half(t, c), send_sems.at[t], recv_sems.at[t], (x, y, 1 - c)) for t in range(n)]
        for cp in sends:
            cp.start()
        for t in range(n):
            _remote(half(t, 1 - c), half(t, 1 - c), send_sems.at[t], recv_sems.at[t], (x, y, 1 - c)).wait_recv()
        for cp in sends:
            cp.wait_send()

    return pl.pallas_call(
        body, name="rs_join_halves", in_specs=[HBM] * n, out_specs=[HBM] * n,
        out_shape=[jax.ShapeDtypeStruct(b.shape, b.dtype) for b in bufs],
        input_output_aliases={t: t for t in range(n)},
        scratch_shapes=[pltpu.SemaphoreType.DMA((n,)), pltpu.SemaphoreType.DMA((n,))],
    )(*bufs)


def _all_reduce_small(v, name):
    R, C = v.shape

    def body(v_ref, o_ref, sib, pair, slots, send_sems, recv_sems):
        x, y, c, me = _place()
        swap = _remote(v_ref, sib, send_sems.at[0], recv_sems.at[0], (x, y, 1 - c))
        swap.start()
        swap.wait()
        mine, theirs = v_ref[...], sib[...]
        pair[...] = jnp.where(c == 0, mine, theirs) + jnp.where(c == 0, theirs, mine)
        slots[me] = pair[...]
        sends = [_remote(pair, slots.at[me], send_sems.at[d], recv_sems.at[d], _chip_device(me ^ d, c)) for d in (1, 2, 3)]
        for cp in sends:
            cp.start()
        for d in (1, 2, 3):
            _remote(pair, slots.at[me ^ d], send_sems.at[d], recv_sems.at[d], _chip_device(me ^ d, c)).wait_recv()
        for cp in sends:
            cp.wait_send()
        acc = slots[0]
        for k in range(1, N_CHIPS):
            acc = acc + slots[k]
        o_ref[...] = acc

    vm = pl.BlockSpec(memory_space=pltpu.VMEM)
    return pl.pallas_call(
        body, name=name, in_specs=[vm], out_specs=vm, out_shape=jax.ShapeDtypeStruct((R, C), F32),
        scratch_shapes=[pltpu.VMEM((R, C), F32), pltpu.VMEM((R, C), F32), pltpu.VMEM((N_CHIPS, R, C), F32),
                        pltpu.SemaphoreType.DMA((4,)), pltpu.SemaphoreType.DMA((4,))],
    )(v)


SHARDED = ("w_in", "w_uq", "w_ukv", "conv_w", "w_br_a", "w_br_b", "w_br_c", "w_br_d", "w_out", "w_ffn_gate", "w_ffn_up",
           "w_ffn_down")
ROW_SHARDED = ("w_out", "w_ffn_down")
REPLICATED = ("g_pre_mix", "g_cq", "g_ckv", "pool_w", "pool_scale", "g_sgu_v", "sgu_w", "sgu_b", "g_post_mix", "g_pre_ffn",
              "g_post_ffn")
WEIGHTS = ("w_in", "g_pre_mix", "g_cq", "g_ckv", "w_uq", "w_ukv", "pool_w", "pool_scale", "g_sgu_v", "sgu_w", "sgu_b",
           "conv_w", "w_br_a", "w_br_b", "w_br_c", "w_br_d", "w_out", "g_post_mix", "g_pre_ffn", "w_ffn_gate", "w_ffn_up",
           "w_ffn_down", "g_post_ffn")
GATHERED = tuple(n for n in SHARDED if n != "conv_w")


def _unpack(packed, shapes):
    flat = packed.reshape(-1)
    out, o = [], 0
    for s in shapes:
        n = int(np.prod(s))
        out.append(flat[o:o + n].reshape(s))
        o += n
    return out


def _join_cols(g, l):
    return jnp.concatenate([g[l, k] for k in range(N_CHIPS)], axis=1)


def _pad_heads(w, real):
    lead = w.shape[:-1]
    w = w.reshape(lead + (HEADS, real))
    return jnp.pad(w, [(0, 0)] * len(lead) + [(0, 0), (0, HEAD_PAD - real)]).reshape(lead + (MLA_W,))


def _unpad_heads(w, real):
    lead = w.shape[:-1]
    return w.reshape(lead + (HEADS, HEAD_PAD))[..., :real].reshape(lead + (HEADS * real,))


IN_OFFSETS = {"cq": 0, "ckv": Q_LORA, "kr": Q_LORA + KV_LORA, "mix": Q_LORA + KV_LORA + QK_ROPE}
IN_GATES = Q_LORA + KV_LORA + QK_ROPE + SEC_MIX


def _pad_w_in(w):
    K = w.shape[0]
    z = lambda n: jnp.zeros((K, n), w.dtype)
    return jnp.concatenate([w[:, IN_GATES:], w[:, :IN_OFFSETS["kr"]], z(QK_NOPE), w[:, IN_OFFSETS["kr"]:IN_OFFSETS["mix"]],
                            z(HEAD_PAD - QK_NOPE - QK_ROPE), w[:, IN_OFFSETS["mix"]:IN_GATES]], axis=1)


def _unpad_w_in(d_gates, d_mla, d_mix):
    kr = d_mla[:, Q_LORA + KV_LORA + QK_NOPE:Q_LORA + KV_LORA + QK_NOPE + QK_ROPE]
    return jnp.concatenate([d_mla[:, :Q_LORA + KV_LORA], kr, d_mix, d_gates], axis=1)


def _rope_tables(S):
    half = QK_ROPE // 2
    inv = ROPE_THETA ** (-jnp.arange(0, QK_ROPE, 2, dtype=F32) / QK_ROPE)
    ang = jnp.arange(S, dtype=F32)[:, None] * inv[None, :]
    cos, sin = jnp.cos(ang), jnp.sin(ang)
    one, zero = jnp.ones((S, QK_NOPE), F32), jnp.zeros((S, half), F32)
    tail = HEAD_PAD - QK_NOPE - QK_ROPE
    c = jnp.concatenate([one, cos, cos, jnp.ones((S, tail), F32)], axis=1)
    sa = jnp.concatenate([0 * one, zero, sin, jnp.zeros((S, tail), F32)], axis=1)
    sb = jnp.concatenate([0 * one, -sin, zero, jnp.zeros((S, tail), F32)], axis=1)
    return c, sa, sb


def _layer_weights(gathered, full, l, D):
    w = {}
    w_in = _pad_w_in(_join_cols(gathered["w_in"], l))
    w["w_in"] = w_in
    w["w_in_gates"], w["w_in_mla"], w["w_in_mix"] = w_in[:, :4 * D], w_in[:, 4 * D:4 * D + SEC_MLA], w_in[:, 4 * D + SEC_MLA:]
    w["w_uq"] = _pad_heads(_join_cols(gathered["w_uq"], l), QK_NOPE + QK_ROPE)
    ukv = _join_cols(gathered["w_ukv"], l).reshape(KV_LORA, HEADS, QK_NOPE + V_HEAD)
    pad = ((0, 0), (0, 0), (0, HEAD_PAD - QK_NOPE))
    k_pad = jnp.pad(ukv[:, :, :QK_NOPE], pad).reshape(KV_LORA, N_CHIPS, CHIP_HEADS_W)
    v_pad = jnp.pad(ukv[:, :, QK_NOPE:], pad).reshape(KV_LORA, N_CHIPS, CHIP_HEADS_W)
    w["w_ukv"] = jnp.concatenate([k_pad, v_pad], axis=2).reshape(KV_LORA, 2 * MLA_W)
    w["w_br_a"] = jnp.pad(_join_cols(gathered["w_br_a"], l).reshape(HEADS, V_HEAD, D),
                          ((0, 0), (0, HEAD_PAD - V_HEAD), (0, 0))).reshape(MLA_W, D)
    for n in ("w_br_b", "w_br_c", "w_br_d"):
        w[n] = _join_cols(gathered[n], l)
    w["w_out"] = gathered["w_out"][l].reshape(D, D)
    for n in ("g_pre_mix", "g_cq", "g_ckv", "pool_scale", "g_sgu_v", "g_post_mix", "g_pre_ffn", "g_post_ffn"):
        w[n] = full[n][l].reshape(1, -1)
    pw = full["pool_w"][l]
    w["pool_bd"] = jax.scipy.linalg.block_diag(*[pw[g] for g in range(GROUPS)]).astype(BF16)
    w["sgu_w"] = full["sgu_w"][l]
    w["sgu_bias"] = jnp.repeat(full["sgu_b"][l].T, GROUP_DIM, axis=1)
    w["conv_w"] = jnp.pad(full["conv_w"][l].reshape(3, MIX_W), ((0, CONV_HALO - 3), (0, 0)))
    return w


def _layer_fwd(x, w, gathered, l, S, rope, gather_next):
    D = x.shape[1]
    bufs = [gathered[n] for n in GATHERED]
    P, h = _norm_matmul(x, w["g_pre_mix"], w["w_in"], "proj_fwd")
    Q, K, V, hq, hkv = _mla_prep(P, D, w["g_cq"], w["g_ckv"], w["w_uq"], w["w_ukv"], *rope, S)
    A, lse, bufs = _flash_fwd(Q, K, V, S, _gather_ride(bufs, l + 1, "chips") if gather_next else None)
    Bm, C, Dv, bufs = _mixers_fwd(P, D, S, w["pool_bd"], w["pool_scale"], w["g_sgu_v"], w["sgu_w"], w["sgu_bias"], w["conv_w"],
                                  _gather_ride(bufs, l + 1, "cores") if gather_next else None)
    if gather_next:
        gathered = dict(zip(GATHERED, bufs))
    x1, merged, o = _merge_fwd(x, P, A, Bm, C, Dv, w["w_br_a"], w["w_br_b"], w["w_br_c"], w["w_br_d"], w["w_out"], w["g_post_mix"])
    x2, h2, gt, up, f = _ffn_fwd(x1, w["g_pre_ffn"], gathered["w_ffn_gate"], gathered["w_ffn_up"], gathered["w_ffn_down"],
                                 w["g_post_ffn"], l)
    saved = dict(x=x, P=P, h=h, Q=Q, K=K, V=V, hq=hq, hkv=hkv, A=A, lse=lse, Bm=Bm, C=C, Dv=Dv, x1=x1, merged=merged, o=o,
                 h2=h2, gt=gt, up=up, f=f)
    return x2, saved, gathered


RIDE_SETS = (("w_ffn_gate", "w_out", "w_br_a"), ("w_in", "w_br_b", "w_br_c", "w_br_d", "w_uq", "w_ukv"),
             ("w_ffn_up", "w_ffn_down"))


def _layer_bwd(dx2, w, gathered, l, s, S, rope, bufs, early=None, send_ffn=False):
    D = dx2.shape[1]
    g = {}
    others = [n for n in GATHERED if n not in FFN_WEIGHTS]

    def scatter(k):
        return _scatter_ride([sums[n] for n in RIDE_SETS[k]]) if early else None

    sums, slots = {}, {}
    if early:
        done, spans, half, got_ffn = early
        ride = _exchange_ride([done[n] for n in others], [spans[n] for n in others])
    df, g["g_post_ffn"] = _norm_bwd(dx2, s["f"], w["g_post_ffn"], "ffn_bwd_pre")
    dh_parts, filled, got = _ffn_bwd_main(df, s["gt"], s["up"], s["h2"], gathered["w_ffn_gate"], gathered["w_ffn_up"],
                                          gathered["w_ffn_down"], l, bufs if "w_ffn_gate" in bufs else None,
                                          ride if early else None)
    bufs.update(filled)
    dx1, g["g_pre_ffn"] = _norm_bwd(None, s["x1"], w["g_pre_ffn"], "ffn_bwd_post", add=dx2, parts=dh_parts)
    if early:
        sums = {n: _add_halves(done[n], spans[n], r, half, n) for n, r in zip(others, got)}
        sums.update({n: _add_halves(filled[n], spans[n], got_ffn[n], half, n) for n in FFN_WEIGHTS})
        sums, dx1 = lax.optimization_barrier((sums, dx1))

    (dPg, dA, dBm, dC, dDv, g["g_post_mix"], delta_t), filled, got = _merge_bwd(
        dx1, s["o"], s["merged"], s["P"], s["A"], s["Bm"], s["C"], s["Dv"], w["w_br_a"], w["w_br_b"], w["w_br_c"], w["w_br_d"],
        w["w_out"], w["g_post_mix"], l, bufs if "w_out" in bufs else None, scatter(0))
    bufs.update(filled)
    slots.update(zip(RIDE_SETS[0], got))

    dPm, d_pool_bd, g_ps, g_gv, g["sgu_w"], d_bias, d_cw = _mixers_bwd(
        s["P"], D, S, dBm, dC, dDv, w["pool_bd"], w["pool_scale"], w["g_sgu_v"], w["sgu_w"], w["sgu_bias"], w["conv_w"])
    g["pool_w"] = jnp.stack([d_pool_bd[k * GROUP_DIM:(k + 1) * GROUP_DIM, k * GROUP_DIM:(k + 1) * GROUP_DIM] for k in range(GROUPS)])
    g["pool_scale"], g["g_sgu_v"] = g_ps, g_gv
    g["sgu_b"] = d_bias.reshape(SGU_BLOCK, GROUPS, GROUP_DIM).sum(-1).T
    g["conv_w"] = d_cw[:3].reshape(3, 1, MIX_W)

    dQ, dK, dV, got = _flash_bwd(s["Q"], s["K"], s["V"], dA, s["lse"], delta_t, S, scatter(1))
    slots.update(zip(RIDE_SETS[1], got))
    (dPa, g["g_cq"], g["g_ckv"]), filled = _mla_bwd_post(
        s["P"], D, S, dQ, dK, dV, s["hq"], s["hkv"], w["g_cq"], w["g_ckv"], w["w_uq"], w["w_ukv"], *rope, l,
        bufs if "w_uq" in bufs else None)
    bufs.update(filled)

    ride = scatter(2)
    if send_ffn:
        ride = _exchange_ride([bufs[n] for n in FFN_WEIGHTS], [(l, DEPTH)] * len(FFN_WEIGHTS))
    dx, g["g_pre_mix"], got = _proj_bwd(dx1, s["x"], w["g_pre_mix"], dPg, dPa, dPm, w["w_in_gates"], w["w_in_mla"],
                                        w["w_in_mix"], ride)
    if send_ffn:
        slots = dict(zip(FFN_WEIGHTS, got))
    else:
        slots.update(zip(RIDE_SETS[2], got))
    d_w_in = _unpad_w_in(_matmul_tn(s["h"], dPg, "wgrad_in_gates"), _matmul_tn(s["h"], dPa, "wgrad_in_mla"),
                         _matmul_tn(s["h"], dPm, "wgrad_in_mix"))
    g["w_in"] = d_w_in.reshape(D, N_CHIPS, -1).transpose(1, 0, 2)
    for n in ("g_pre_mix", "g_cq", "g_ckv", "pool_scale", "g_sgu_v", "g_post_mix", "g_pre_ffn", "g_post_ffn"):
        g[n] = g[n].reshape(-1)
    return dx, g, (sums, slots)


SMALL = REPLICATED + ("conv_w",)


def _local_step(x, target, gathered, full, core):
    n_seq, S, D = x.shape
    rope = _rope_tables(S)
    xs = x.reshape(n_seq * S, D)
    weights, saved = [], []
    for l in range(DEPTH):
        w = _layer_weights(gathered, full, l, D)
        gather_next = l + 1 < DEPTH
        if gather_next:
            w, gathered = lax.optimization_barrier((w, gathered))
        xs, s, gathered = _layer_fwd(xs, w, gathered, l, S, rope, gather_next)
        weights.append(w)
        saved.append(s)
    loss_parts, dx = _loss_grad(xs, target.reshape(n_seq * S, D))
    grads, bufs = [None] * DEPTH, {}
    for l in reversed(range(1, DEPTH)):
        dx, grads[l], (_, got_ffn) = _layer_bwd(dx, weights[l], gathered, l, saved[l], S, rope, bufs, send_ffn=l == 1)
    done = dict(bufs, w_in=jnp.stack([grads[l]["w_in"] for l in range(1, DEPTH)], axis=1))
    spans = dict({n: (1, DEPTH) for n in GATHERED}, w_in=(0, DEPTH - 1))
    dx, grads[0], early = _layer_bwd(dx, weights[0], gathered, 0, saved[0], S, rope, bufs, (done, spans, core, got_ffn))
    last = dict(bufs, w_in=grads[0]["w_in"][:, None])
    small = {n: jnp.stack([grads[l][n] for l in range(DEPTH)]) for n in SMALL}
    return loss_parts, dx.reshape(n_seq, S, D), early, last, small


def _unpad_reduced(n, r):
    L = r.shape[0]
    if n == "w_uq":
        return r.reshape(L, Q_LORA, 2, HEAD_PAD)[..., :QK_NOPE + QK_ROPE].reshape(L, Q_LORA, -1)
    if n == "w_ukv":
        r = r.reshape(L, KV_LORA, 2, 2, HEAD_PAD)[..., :QK_NOPE]
        return jnp.concatenate([r[:, :, 0], r[:, :, 1]], axis=-1).reshape(L, KV_LORA, -1)
    if n == "w_br_a":
        return r.reshape(L, HEADS, HEAD_PAD, -1)[:, :, :V_HEAD].reshape(L, HEADS * V_HEAD, -1)
    return r


def _small_rows(n):
    return -(-n // (8 * 128)) * 8


def _to_small(parts):
    flat = jnp.concatenate([p.reshape(-1) for p in parts])
    rows = _small_rows(flat.shape[0])
    return jnp.pad(flat, (0, rows * 128 - flat.shape[0])).reshape(rows, 128)


def kernel(x, w_in, g_pre_mix, g_cq, g_ckv, w_uq, w_ukv, pool_w, pool_scale, g_sgu_v, sgu_w, sgu_b, conv_w, w_br_a, w_br_b, w_br_c, w_br_d, w_out, g_post_mix, g_pre_ffn, w_ffn_gate, w_ffn_up, w_ffn_down, g_post_ffn, loss_target, m_w_in, m_g_pre_mix, m_g_cq, m_g_ckv, m_w_uq, m_w_ukv, m_pool_w, m_pool_scale, m_g_sgu_v, m_sgu_w, m_sgu_b, m_conv_w, m_w_br_a, m_w_br_b, m_w_br_c, m_w_br_d, m_w_out, m_g_post_mix, m_g_pre_ffn, m_w_ffn_gate, m_w_ffn_up, m_w_ffn_down, m_g_post_ffn, v_w_in, v_g_pre_mix, v_g_cq, v_g_ckv, v_w_uq, v_w_ukv, v_pool_w, v_pool_scale, v_g_sgu_v, v_sgu_w, v_sgu_b, v_conv_w, v_w_br_a, v_w_br_b, v_w_br_c, v_w_br_d, v_w_out, v_g_post_mix, v_g_pre_ffn, v_w_ffn_gate, v_w_ffn_up, v_w_ffn_down, v_g_post_ffn):
    local = dict(locals())
    W = {n: local[n] for n in WEIGHTS}
    M = {n: local["m_" + n] for n in WEIGHTS}
    V = {n: local["v_" + n] for n in WEIGHTS}
    chip = 2 * lax.axis_index("x") + lax.axis_index("y")
    core = lax.axis_index("c")

    gathered = dict(zip(GATHERED, _gather_weights([_place_own(W[n], chip, n) for n in GATHERED], 0)))
    conv_shape = conv_w.shape
    conv_cols = conv_shape[-1]
    conv_full_shape = conv_shape[:-1] + (N_CHIPS * conv_cols,)
    placed = lax.dynamic_update_slice(jnp.zeros(conv_full_shape, F32), conv_w, (0, 0, 0, chip * conv_cols))
    n_conv = int(np.prod(conv_full_shape))
    conv_sum = _all_reduce_small(_to_small([placed]), "gather_conv_w")
    full = {n: W[n] for n in REPLICATED}
    full["conv_w"] = 0.5 * conv_sum.reshape(-1)[:n_conv].reshape(conv_full_shape)

    loss_parts, grad_x, (sums_up, slots_up), last, small = _local_step(x, loss_target, gathered, full, core)
    loss = lax.psum(jnp.sum(loss_parts), ("x", "y", "c"))

    small_sum = _all_reduce_small(_to_small([small[n] for n in SMALL]), "reduce_small_grads")
    small_grads = dict(zip(SMALL, _unpack(small_sum, [small[n].shape for n in SMALL])))
    small_grads["conv_w"] = lax.dynamic_slice(small_grads["conv_w"], (0, 0, 0, chip * conv_cols), conv_shape)

    first = [(0, 1)] * len(GATHERED)
    Gs = [last[n] for n in GATHERED]
    got = _run_ride(_exchange_ride(Gs, first), "rs_exchange_halves")
    sums_0 = [_add_halves(g, (0, 1), r, core, n) for n, g, r in zip(GATHERED, Gs, got)]
    slots_0 = _run_ride(_scatter_ride(sums_0), "rs_scatter_partials")
    place = [lax.axis_index(a).reshape(1).astype(jnp.int32) for a in ("x", "y", "c")]
    halves = []
    for n, h, s in zip(GATHERED, sums_0, slots_0):
        upper = _sum_slots(sums_up[n], slots_up[n], place, n, 1, DEPTH)
        halves.append(_sum_slots(h, s, place, n, 0, DEPTH, prev=upper))
    shard_grads = {n: _unpad_reduced(n, r).reshape(W[n].shape) for n, r in zip(GATHERED, _join_halves(halves))}

    out_g, out_d, out_m, out_v = {}, {}, {}, {}
    for n in GATHERED:
        shp = W[n].shape
        flat = lambda a: a.reshape(-1, shp[-1])
        d, m2, v2 = _adamw(flat(W[n]), flat(shard_grads[n]), flat(M[n]), flat(V[n]), "adamw_" + n)
        out_g[n], out_d[n], out_m[n], out_v[n] = shard_grads[n], d.reshape(shp), m2.reshape(shp), v2.reshape(shp)
    rest_shapes = [W[n].shape for n in SMALL]
    d, m2, v2 = _adamw(_to_small([W[n] for n in SMALL]), _to_small([small_grads[n] for n in SMALL]),
                       _to_small([M[n] for n in SMALL]), _to_small([V[n] for n in SMALL]), "adamw_small")
    for n, dd, mm, vv in zip(SMALL, _unpack(d, rest_shapes), _unpack(m2, rest_shapes), _unpack(v2, rest_shapes)):
        out_g[n], out_d[n], out_m[n], out_v[n] = small_grads[n], dd, mm, vv

    return (loss, grad_x, *[out_g[n] for n in WEIGHTS], *[out_d[n] for n in WEIGHTS], *[out_m[n] for n in WEIGHTS],
            *[out_v[n] for n in WEIGHTS])
```
